```python
import math
import jax, jax.numpy as jnp
from jax import lax
import numpy as np

D_MODEL = 1024
BATCH = 8
SEQ = 4096
DEPTH = 2

MEM_LEN = 256
D_MIX = D_MODEL
SSD_DIM = D_MIX // 2
SSD_HEAD_DIM = 64
SSD_HEADS = SSD_DIM // SSD_HEAD_DIM
SSD_GROUPS = 2
SSD_STATE = 64
CONV_K = 4
CHUNK = 128
SB_DIM = D_MIX - SSD_DIM
SB_HEAD_DIM = 64
SB_HEADS = SB_DIM // SB_HEAD_DIM
Q_BLOCK = 128
XA_HEADS = 4
XA_HEAD_DIM = 128
XA_DIM = XA_HEADS * XA_HEAD_DIM
D_FF = 4 * D_MODEL
EPS = 1e-5
GN = SSD_GROUPS * SSD_STATE
CONV_DIM = SSD_DIM + 2 * GN
IN_DIM = SSD_DIM + CONV_DIM + SSD_HEADS + 3 * SB_DIM

kernel_name = "hymba_ssd_stickbreaking_memxattn_trunk"


def rmsnorm(x, g):
    xf = x.astype(jnp.float32)
    y = xf * lax.rsqrt(jnp.mean(xf * xf, axis=-1, keepdims=True) + EPS)
    return (y * g.astype(jnp.float32)).astype(x.dtype)


def causal_dwconv(x, w, b):
    y = lax.conv_general_dilated(
        x, w.astype(x.dtype)[:, None, :], window_strides=(1,), padding=[(CONV_K - 1, 0)],
        dimension_numbers=("NWC", "WIO", "NWC"), feature_group_count=x.shape[-1])
    return y + b.astype(x.dtype)


def ssd_chunked(xh, dt, a, bm, cm):
    b, s, h, p = xh.shape
    g, n = bm.shape[-2:]
    e = h // g
    nc = s // CHUNK
    x_c = (xh * dt[..., None]).reshape(b, nc, CHUNK, g, e, p)
    a_c = (dt * a).reshape(b, nc, CHUNK, g, e)
    b_c = bm.reshape(b, nc, CHUNK, g, n)
    c_c = cm.reshape(b, nc, CHUNK, g, n)
    a_cum = jnp.cumsum(a_c, axis=2)
    seg = a_cum[:, :, :, None] - a_cum[:, :, None, :]
    causal = jnp.tril(jnp.ones((CHUNK, CHUNK), dtype=bool))[:, :, None, None]
    decay = jnp.exp(jnp.where(causal, seg, -jnp.inf))
    cb = jnp.einsum("bclgn,bcsgn->bclsg", c_c, b_c)
    y_diag = jnp.einsum("bclsg,bclsge,bcsgep->bclgep", cb, decay, x_c)
    decay_states = jnp.exp(a_cum[:, :, -1:] - a_cum)
    states = jnp.einsum("bclgn,bclge,bclgep->bcgepn", b_c, decay_states, x_c)
    chunk_decay = jnp.exp(a_cum[:, :, -1])

    def step(prev, inp):
        st, dec = inp
        return prev * dec[..., None, None] + st, prev

    init = jnp.zeros((b, g, e, p, n), jnp.float32)
    _, prev_states = lax.scan(step, init, (jnp.swapaxes(states, 0, 1), jnp.swapaxes(chunk_decay, 0, 1)))
    prev_states = jnp.swapaxes(prev_states, 0, 1)
    y_off = jnp.einsum("bclgn,bcgepn,bclge->bclgep", c_c, prev_states, jnp.exp(a_cum))
    return (y_diag + y_off).reshape(b, s, h, p)


def stick_breaking_attention(q, k, v):
    s_len, dh = q.shape[2], q.shape[3]
    scale = 1.0 / math.sqrt(dh)
    outs = []
    for i in range(s_len // Q_BLOCK):
        start = i * Q_BLOCK
        end = start + Q_BLOCK
        qb = q[:, :, start:end].astype(jnp.float32)
        kb = k[:, :, :end].astype(jnp.float32)
        vb = v[:, :, :end].astype(jnp.float32)
        logits = jnp.einsum("bhqd,bhkd->bhqk", qb, kb) * scale
        t_pos = start + jnp.arange(Q_BLOCK)[:, None]
        s_pos = jnp.arange(end)[None, :]
        mask = s_pos < t_pos
        log_stay = jnp.where(mask, jax.nn.log_sigmoid(-logits), 0.0)
        log_beta = jax.nn.log_sigmoid(logits)
        later = lax.cumsum(log_stay, axis=3, reverse=True) - log_stay
        w = jnp.where(mask, jnp.exp(log_beta + later), 0.0)
        outs.append(jnp.einsum("bhqk,bhkd->bhqd", w, vb))
    return jnp.concatenate(outs, axis=2)


def hybrid_mixer(hn, w_in, conv_w, conv_b, dt_bias, a_log, d_skip, ssd_norm_g, sb_norm_g, w_out):
    b, s, _ = hn.shape
    proj = hn @ w_in.astype(hn.dtype)
    z, xbc, dt_raw, qkv = jnp.split(
        proj, [SSD_DIM, SSD_DIM + CONV_DIM, SSD_DIM + CONV_DIM + SSD_HEADS], axis=-1)
    xbc = jax.nn.silu(causal_dwconv(xbc, conv_w, conv_b))
    xs, bm, cm = jnp.split(xbc, [SSD_DIM, SSD_DIM + GN], axis=-1)
    dt = jax.nn.softplus(dt_raw.astype(jnp.float32) + dt_bias.astype(jnp.float32))
    a = -jnp.exp(a_log.astype(jnp.float32))
    xh = xs.astype(jnp.float32).reshape(b, s, SSD_HEADS, SSD_HEAD_DIM)
    y = ssd_chunked(xh, dt, a,
                    bm.astype(jnp.float32).reshape(b, s, SSD_GROUPS, SSD_STATE),
                    cm.astype(jnp.float32).reshape(b, s, SSD_GROUPS, SSD_STATE))
    y = y + d_skip.astype(jnp.float32)[:, None] * xh
    y = y.reshape(b, s, SSD_DIM) * jax.nn.silu(z.astype(jnp.float32))
    y_ssd = rmsnorm(y, ssd_norm_g)
    q, k, v = jnp.split(qkv, 3, axis=-1)
    to_heads = lambda t: t.reshape(b, s, SB_HEADS, SB_HEAD_DIM).transpose(0, 2, 1, 3)
    o = stick_breaking_attention(to_heads(q), to_heads(k), to_heads(v))
    y_sb = rmsnorm(o.transpose(0, 2, 1, 3).reshape(b, s, SB_DIM), sb_norm_g)
    y_all = jnp.concatenate([y_ssd, y_sb], axis=-1).astype(hn.dtype)
    return y_all @ w_out.astype(hn.dtype)


def memory_cross_attention(hn, mem, g_mem, w_q, w_k, w_v, w_o):
    b, s, _ = hn.shape
    m = rmsnorm(mem, g_mem)
    q = (hn @ w_q.astype(hn.dtype)).reshape(b, s, XA_HEADS, XA_HEAD_DIM).astype(jnp.float32)
    k = (m @ w_k.astype(m.dtype)).reshape(b, -1, XA_HEADS, XA_HEAD_DIM).astype(jnp.float32)
    v = (m @ w_v.astype(m.dtype)).reshape(b, -1, XA_HEADS, XA_HEAD_DIM).astype(jnp.float32)
    logits = jnp.einsum("bshd,bmhd->bhsm", q, k) * (1.0 / math.sqrt(XA_HEAD_DIM))
    p = jax.nn.softmax(logits, axis=-1)
    o = jnp.einsum("bhsm,bmhd->bshd", p, v).reshape(b, s, XA_DIM).astype(hn.dtype)
    return o @ w_o.astype(hn.dtype)


def sqrelu_mlp(hn, w1, w2):
    u = hn @ w1.astype(hn.dtype)
    u = jnp.square(jax.nn.relu(u))
    return u @ w2.astype(hn.dtype)


def _fwd_setup_inputs(seed: int = 0) -> dict:
    key = jax.random.key(seed)
    ks = jax.random.split(key, 24)
    f32 = jnp.float32
    nrm = lambda k, shape, fan_in: jax.random.normal(k, shape, f32) * (fan_in ** -0.5)
    gain = lambda k, shape: 1.0 + 0.02 * jax.random.normal(k, shape, f32)
    u = jax.random.uniform(ks[5], (DEPTH, SSD_HEADS), f32)
    dt0 = jnp.exp(u * (math.log(0.1) - math.log(0.001)) + math.log(0.001))
    dt_bias = dt0 + jnp.log(-jnp.expm1(-dt0))
    a_log = jnp.log(jax.random.uniform(ks[6], (DEPTH, SSD_HEADS), f32, minval=1.0, maxval=16.0))
    return {
        "x": jax.random.normal(ks[0], (BATCH, SEQ, D_MODEL), f32),
        "mem": jax.random.normal(ks[1], (BATCH, MEM_LEN, D_MODEL), f32),
        "norm_mix_g": gain(ks[2], (DEPTH, D_MODEL)),
        "w_in": nrm(ks[3], (DEPTH, D_MODEL, IN_DIM), D_MODEL),
        "conv_w": 0.5 * jax.random.normal(ks[4], (DEPTH, CONV_K, CONV_DIM), f32),
        "conv_b": 0.02 * jax.random.normal(ks[7], (DEPTH, CONV_DIM), f32),
        "dt_bias": dt_bias,
        "a_log": a_log,
        "d_skip": 1.0 + 0.1 * jax.random.normal(ks[8], (DEPTH, SSD_HEADS), f32),
        "ssd_norm_g": gain(ks[9], (DEPTH, SSD_DIM)),
        "sb_norm_g": gain(ks[10], (DEPTH, SB_DIM)),
        "w_out": nrm(ks[11], (DEPTH, D_MIX, D_MODEL), D_MIX),
        "norm_xa_g": gain(ks[12], (DEPTH, D_MODEL)),
        "norm_mem_g": gain(ks[13], (DEPTH, D_MODEL)),
        "w_xq": nrm(ks[14], (DEPTH, D_MODEL, XA_DIM), D_MODEL),
        "w_xk": nrm(ks[15], (DEPTH, D_MODEL, XA_DIM), D_MODEL),
        "w_xv": nrm(ks[16], (DEPTH, D_MODEL, XA_DIM), D_MODEL),
        "w_xo": nrm(ks[17], (DEPTH, XA_DIM, D_MODEL), XA_DIM),
        "norm_ff_g": gain(ks[18], (DEPTH, D_MODEL)),
        "w_ff1": nrm(ks[19], (DEPTH, D_MODEL, D_FF), D_MODEL),
        "w_ff2": nrm(ks[20], (DEPTH, D_FF, D_MODEL), D_FF),
        "final_g": gain(ks[21], (D_MODEL,)),
    }


def _fwd_reference(x, mem, norm_mix_g, w_in, conv_w, conv_b, dt_bias, a_log, d_skip, ssd_norm_g,
              sb_norm_g, w_out, norm_xa_g, norm_mem_g, w_xq, w_xk, w_xv, w_xo,
              norm_ff_g, w_ff1, w_ff2, final_g):
    h = x
    for l in range(DEPTH):
        h = h + hybrid_mixer(rmsnorm(h, norm_mix_g[l]), w_in[l], conv_w[l], conv_b[l], dt_bias[l],
                             a_log[l], d_skip[l], ssd_norm_g[l], sb_norm_g[l], w_out[l])
        h = h + memory_cross_attention(rmsnorm(h, norm_xa_g[l]), mem, norm_mem_g[l],
                                       w_xq[l], w_xk[l], w_xv[l], w_xo[l])
        h = h + sqrelu_mlp(rmsnorm(h, norm_ff_g[l]), w_ff1[l], w_ff2[l])
    return rmsnorm(h, final_g)


import jax as _jax
import jax.numpy as _jnp

TWIN_FORMAT = 'train_step'
FWD_PARAMS = ['x', 'mem', 'norm_mix_g', 'w_in', 'conv_w', 'conv_b', 'dt_bias', 'a_log', 'd_skip', 'ssd_norm_g', 'sb_norm_g', 'w_out', 'norm_xa_g', 'norm_mem_g', 'w_xq', 'w_xk', 'w_xv', 'w_xo', 'norm_ff_g', 'w_ff1', 'w_ff2', 'final_g']
TWIN_WEIGHTS = ['norm_mix_g', 'w_in', 'conv_w', 'conv_b', 'dt_bias', 'a_log', 'd_skip', 'ssd_norm_g', 'sb_norm_g', 'w_out', 'norm_xa_g', 'norm_mem_g', 'w_xq', 'w_xk', 'w_xv', 'w_xo', 'norm_ff_g', 'w_ff1', 'w_ff2', 'final_g']
TWIN_DIFF_INPUT = 'x'
TWIN_INPUTS = ['x', 'mem', 'norm_mix_g', 'w_in', 'conv_w', 'conv_b', 'dt_bias', 'a_log', 'd_skip', 'ssd_norm_g', 'sb_norm_g', 'w_out', 'norm_xa_g', 'norm_mem_g', 'w_xq', 'w_xk', 'w_xv', 'w_xo', 'norm_ff_g', 'w_ff1', 'w_ff2', 'final_g', 'loss_target', 'm_norm_mix_g', 'm_w_in', 'm_conv_w', 'm_conv_b', 'm_dt_bias', 'm_a_log', 'm_d_skip', 'm_ssd_norm_g', 'm_sb_norm_g', 'm_w_out', 'm_norm_xa_g', 'm_norm_mem_g', 'm_w_xq', 'm_w_xk', 'm_w_xv', 'm_w_xo', 'm_norm_ff_g', 'm_w_ff1', 'm_w_ff2', 'm_final_g', 'v_norm_mix_g', 'v_w_in', 'v_conv_w', 'v_conv_b', 'v_dt_bias', 'v_a_log', 'v_d_skip', 'v_ssd_norm_g', 'v_sb_norm_g', 'v_w_out', 'v_norm_xa_g', 'v_norm_mem_g', 'v_w_xq', 'v_w_xk', 'v_w_xv', 'v_w_xo', 'v_norm_ff_g', 'v_w_ff1', 'v_w_ff2', 'v_final_g']
TWIN_OUTPUTS = ['loss', 'grad_x', 'grad_norm_mix_g', 'grad_w_in', 'grad_conv_w', 'grad_conv_b', 'grad_dt_bias', 'grad_a_log', 'grad_d_skip', 'grad_ssd_norm_g', 'grad_sb_norm_g', 'grad_w_out', 'grad_norm_xa_g', 'grad_norm_mem_g', 'grad_w_xq', 'grad_w_xk', 'grad_w_xv', 'grad_w_xo', 'grad_norm_ff_g', 'grad_w_ff1', 'grad_w_ff2', 'grad_final_g', 'delta_norm_mix_g', 'delta_w_in', 'delta_conv_w', 'delta_conv_b', 'delta_dt_bias', 'delta_a_log', 'delta_d_skip', 'delta_ssd_norm_g', 'delta_sb_norm_g', 'delta_w_out', 'delta_norm_xa_g', 'delta_norm_mem_g', 'delta_w_xq', 'delta_w_xk', 'delta_w_xv', 'delta_w_xo', 'delta_norm_ff_g', 'delta_w_ff1', 'delta_w_ff2', 'delta_final_g', 'new_m_norm_mix_g', 'new_m_w_in', 'new_m_conv_w', 'new_m_conv_b', 'new_m_dt_bias', 'new_m_a_log', 'new_m_d_skip', 'new_m_ssd_norm_g', 'new_m_sb_norm_g', 'new_m_w_out', 'new_m_norm_xa_g', 'new_m_norm_mem_g', 'new_m_w_xq', 'new_m_w_xk', 'new_m_w_xv', 'new_m_w_xo', 'new_m_norm_ff_g', 'new_m_w_ff1', 'new_m_w_ff2', 'new_m_final_g', 'new_v_norm_mix_g', 'new_v_w_in', 'new_v_conv_w', 'new_v_conv_b', 'new_v_dt_bias', 'new_v_a_log', 'new_v_d_skip', 'new_v_ssd_norm_g', 'new_v_sb_norm_g', 'new_v_w_out', 'new_v_norm_xa_g', 'new_v_norm_mem_g', 'new_v_w_xq', 'new_v_w_xk', 'new_v_w_xv', 'new_v_w_xo', 'new_v_norm_ff_g', 'new_v_w_ff1', 'new_v_w_ff2', 'new_v_final_g']
TWIN_LEAF_KINDS = {'loss': 'loss', 'grad_x': 'grad_x', 'grad_norm_mix_g': 'grad_w', 'grad_w_in': 'grad_w', 'grad_conv_w': 'grad_w', 'grad_conv_b': 'grad_w', 'grad_dt_bias': 'grad_w', 'grad_a_log': 'grad_w', 'grad_d_skip': 'grad_w', 'grad_ssd_norm_g': 'grad_w', 'grad_sb_norm_g': 'grad_w', 'grad_w_out': 'grad_w', 'grad_norm_xa_g': 'grad_w', 'grad_norm_mem_g': 'grad_w', 'grad_w_xq': 'grad_w', 'grad_w_xk': 'grad_w', 'grad_w_xv': 'grad_w', 'grad_w_xo': 'grad_w', 'grad_norm_ff_g': 'grad_w', 'grad_w_ff1': 'grad_w', 'grad_w_ff2': 'grad_w', 'grad_final_g': 'grad_w', 'delta_norm_mix_g': 'delta_w', 'delta_w_in': 'delta_w', 'delta_conv_w': 'delta_w', 'delta_conv_b': 'delta_w', 'delta_dt_bias': 'delta_w', 'delta_a_log': 'delta_w', 'delta_d_skip': 'delta_w', 'delta_ssd_norm_g': 'delta_w', 'delta_sb_norm_g': 'delta_w', 'delta_w_out': 'delta_w', 'delta_norm_xa_g': 'delta_w', 'delta_norm_mem_g': 'delta_w', 'delta_w_xq': 'delta_w', 'delta_w_xk': 'delta_w', 'delta_w_xv': 'delta_w', 'delta_w_xo': 'delta_w', 'delta_norm_ff_g': 'delta_w', 'delta_w_ff1': 'delta_w', 'delta_w_ff2': 'delta_w', 'delta_final_g': 'delta_w', 'new_m_norm_mix_g': 'new_m', 'new_m_w_in': 'new_m', 'new_m_conv_w': 'new_m', 'new_m_conv_b': 'new_m', 'new_m_dt_bias': 'new_m', 'new_m_a_log': 'new_m', 'new_m_d_skip': 'new_m', 'new_m_ssd_norm_g': 'new_m', 'new_m_sb_norm_g': 'new_m', 'new_m_w_out': 'new_m', 'new_m_norm_xa_g': 'new_m', 'new_m_norm_mem_g': 'new_m', 'new_m_w_xq': 'new_m', 'new_m_w_xk': 'new_m', 'new_m_w_xv': 'new_m', 'new_m_w_xo': 'new_m', 'new_m_norm_ff_g': 'new_m', 'new_m_w_ff1': 'new_m', 'new_m_w_ff2': 'new_m', 'new_m_final_g': 'new_m', 'new_v_norm_mix_g': 'new_v', 'new_v_w_in': 'new_v', 'new_v_conv_w': 'new_v', 'new_v_conv_b': 'new_v', 'new_v_dt_bias': 'new_v', 'new_v_a_log': 'new_v', 'new_v_d_skip': 'new_v', 'new_v_ssd_norm_g': 'new_v', 'new_v_sb_norm_g': 'new_v', 'new_v_w_out': 'new_v', 'new_v_norm_xa_g': 'new_v', 'new_v_norm_mem_g': 'new_v', 'new_v_w_xq': 'new_v', 'new_v_w_xk': 'new_v', 'new_v_w_xv': 'new_v', 'new_v_w_xo': 'new_v', 'new_v_norm_ff_g': 'new_v', 'new_v_w_ff1': 'new_v', 'new_v_w_ff2': 'new_v', 'new_v_final_g': 'new_v'}


def _forward(args):
    return _fwd_reference(*[args[k] for k in FWD_PARAMS])


def _output_shape():
    def fwd():
        inp = _fwd_setup_inputs(0)
        return _fwd_reference(*[inp[k] for k in FWD_PARAMS])
    out = _jax.eval_shape(fwd)
    return out.shape, out.dtype

N_MICROBATCH = 1
ADAM_LR = 0.001
ADAM_B1 = 0.9
ADAM_B2 = 0.999
ADAM_EPS = 1e-08
ADAM_WD = 0.01
ADAM_STEP = 10
PER_EXAMPLE_BATCH_AXIS = {'x': 0, 'mem': 0, 'loss_target': 0}
SHARED_INPUTS = []
_WEIGHT_DTYPES = {'norm_mix_g': _jnp.float32, 'w_in': _jnp.float32, 'conv_w': _jnp.float32, 'conv_b': _jnp.float32, 'dt_bias': _jnp.float32, 'a_log': _jnp.float32, 'd_skip': _jnp.float32, 'ssd_norm_g': _jnp.float32, 'sb_norm_g': _jnp.float32, 'w_out': _jnp.float32, 'norm_xa_g': _jnp.float32, 'norm_mem_g': _jnp.float32, 'w_xq': _jnp.float32, 'w_xk': _jnp.float32, 'w_xv': _jnp.float32, 'w_xo': _jnp.float32, 'norm_ff_g': _jnp.float32, 'w_ff1': _jnp.float32, 'w_ff2': _jnp.float32, 'final_g': _jnp.float32}
MOMENT_SCALE = {'norm_mix_g': 1.751773e-01, 'w_in': 1.030989e-01, 'conv_w': 1.115302e-01, 'conv_b': 1.318055e-01, 'dt_bias': 2.015818e-01, 'a_log': 1.432706e-01, 'd_skip': 7.418109e-01, 'ssd_norm_g': 1.307912e-01, 'sb_norm_g': 1.345712e-01, 'w_out': 1.276033e-01, 'norm_xa_g': 1.310377e-02, 'norm_mem_g': 1.990702e-02, 'w_xq': 1.820579e-02, 'w_xk': 1.826012e-02, 'w_xv': 1.969107e-02, 'w_xo': 1.370564e-02, 'norm_ff_g': 1.307141e-01, 'w_ff1': 6.527532e-02, 'w_ff2': 1.353136e-01, 'final_g': 3.244341e+01}


def _to_microbatches(a, axis):
    t = _jnp.moveaxis(a, axis, 0)
    t = t.reshape((N_MICROBATCH, t.shape[0] // N_MICROBATCH) + t.shape[1:])
    return _jnp.moveaxis(t, 1, axis + 1)


def setup_inputs(seed: int = 0) -> dict:
    inp = _fwd_setup_inputs(seed)
    key = _jax.random.fold_in(_jax.random.key(seed), 7919)
    shape, _ = _output_shape()
    out = dict(inp)
    out["loss_target"] = _jax.random.normal(_jax.random.fold_in(key, 0), shape, _jnp.float32)
    for i, name in enumerate(TWIN_WEIGHTS):
        w = inp[name].astype(_jnp.float32)
        if MOMENT_SCALE is None:
            s = _jnp.sqrt(_jnp.mean(_jnp.square(w)) + 1e-30)
        else:
            s = MOMENT_SCALE[name]
        km, kv = _jax.random.split(_jax.random.fold_in(key, i + 1))
        out[name] = w
        out["m_" + name] = s * _jax.random.normal(km, w.shape, _jnp.float32)
        out["v_" + name] = (s * s) * _jax.random.uniform(kv, w.shape, _jnp.float32, 0.5, 1.5)
    if N_MICROBATCH > 1:
        for name, axis in PER_EXAMPLE_BATCH_AXIS.items():
            out[name] = _to_microbatches(out[name], axis)
    return {'x': out['x'], 'mem': out['mem'], 'norm_mix_g': out['norm_mix_g'], 'w_in': out['w_in'], 'conv_w': out['conv_w'], 'conv_b': out['conv_b'], 'dt_bias': out['dt_bias'], 'a_log': out['a_log'], 'd_skip': out['d_skip'], 'ssd_norm_g': out['ssd_norm_g'], 'sb_norm_g': out['sb_norm_g'], 'w_out': out['w_out'], 'norm_xa_g': out['norm_xa_g'], 'norm_mem_g': out['norm_mem_g'], 'w_xq': out['w_xq'], 'w_xk': out['w_xk'], 'w_xv': out['w_xv'], 'w_xo': out['w_xo'], 'norm_ff_g': out['norm_ff_g'], 'w_ff1': out['w_ff1'], 'w_ff2': out['w_ff2'], 'final_g': out['final_g'], 'loss_target': out['loss_target'], 'm_norm_mix_g': out['m_norm_mix_g'], 'm_w_in': out['m_w_in'], 'm_conv_w': out['m_conv_w'], 'm_conv_b': out['m_conv_b'], 'm_dt_bias': out['m_dt_bias'], 'm_a_log': out['m_a_log'], 'm_d_skip': out['m_d_skip'], 'm_ssd_norm_g': out['m_ssd_norm_g'], 'm_sb_norm_g': out['m_sb_norm_g'], 'm_w_out': out['m_w_out'], 'm_norm_xa_g': out['m_norm_xa_g'], 'm_norm_mem_g': out['m_norm_mem_g'], 'm_w_xq': out['m_w_xq'], 'm_w_xk': out['m_w_xk'], 'm_w_xv': out['m_w_xv'], 'm_w_xo': out['m_w_xo'], 'm_norm_ff_g': out['m_norm_ff_g'], 'm_w_ff1': out['m_w_ff1'], 'm_w_ff2': out['m_w_ff2'], 'm_final_g': out['m_final_g'], 'v_norm_mix_g': out['v_norm_mix_g'], 'v_w_in': out['v_w_in'], 'v_conv_w': out['v_conv_w'], 'v_conv_b': out['v_conv_b'], 'v_dt_bias': out['v_dt_bias'], 'v_a_log': out['v_a_log'], 'v_d_skip': out['v_d_skip'], 'v_ssd_norm_g': out['v_ssd_norm_g'], 'v_sb_norm_g': out['v_sb_norm_g'], 'v_w_out': out['v_w_out'], 'v_norm_xa_g': out['v_norm_xa_g'], 'v_norm_mem_g': out['v_norm_mem_g'], 'v_w_xq': out['v_w_xq'], 'v_w_xk': out['v_w_xk'], 'v_w_xv': out['v_w_xv'], 'v_w_xo': out['v_w_xo'], 'v_norm_ff_g': out['v_norm_ff_g'], 'v_w_ff1': out['v_w_ff1'], 'v_w_ff2': out['v_w_ff2'], 'v_final_g': out['v_final_g']}


def _loss(weights, diff, rest, loss_target):
    with _jax.named_scope("forward"):
        args = {**rest, TWIN_DIFF_INPUT: diff, **{k: w.astype(_WEIGHT_DTYPES[k]) for k, w in weights.items()}}
        y = _forward(args)
    with _jax.named_scope("loss_head"):
        err = _jnp.square(y.astype(_jnp.float32) - loss_target)
        return 0.5 * _jnp.sum(_jnp.mean(err, axis=-1)) if err.ndim else 0.5 * err


def _adamw(w, g, m, v):
    m = ADAM_B1 * m + (1.0 - ADAM_B1) * g
    v = ADAM_B2 * v + (1.0 - ADAM_B2) * _jnp.square(g)
    m_hat = m / (1.0 - ADAM_B1 ** ADAM_STEP)
    v_hat = v / (1.0 - ADAM_B2 ** ADAM_STEP)
    delta = -ADAM_LR * (m_hat / (_jnp.sqrt(v_hat) + ADAM_EPS) + ADAM_WD * w)
    return delta, m, v


def reference(x, mem, norm_mix_g, w_in, conv_w, conv_b, dt_bias, a_log, d_skip, ssd_norm_g, sb_norm_g, w_out, norm_xa_g, norm_mem_g, w_xq, w_xk, w_xv, w_xo, norm_ff_g, w_ff1, w_ff2, final_g, loss_target, m_norm_mix_g, m_w_in, m_conv_w, m_conv_b, m_dt_bias, m_a_log, m_d_skip, m_ssd_norm_g, m_sb_norm_g, m_w_out, m_norm_xa_g, m_norm_mem_g, m_w_xq, m_w_xk, m_w_xv, m_w_xo, m_norm_ff_g, m_w_ff1, m_w_ff2, m_final_g, v_norm_mix_g, v_w_in, v_conv_w, v_conv_b, v_dt_bias, v_a_log, v_d_skip, v_ssd_norm_g, v_sb_norm_g, v_w_out, v_norm_xa_g, v_norm_mem_g, v_w_xq, v_w_xk, v_w_xv, v_w_xo, v_norm_ff_g, v_w_ff1, v_w_ff2, v_final_g):
    given = dict(x=x, mem=mem, norm_mix_g=norm_mix_g, w_in=w_in, conv_w=conv_w, conv_b=conv_b, dt_bias=dt_bias, a_log=a_log, d_skip=d_skip, ssd_norm_g=ssd_norm_g, sb_norm_g=sb_norm_g, w_out=w_out, norm_xa_g=norm_xa_g, norm_mem_g=norm_mem_g, w_xq=w_xq, w_xk=w_xk, w_xv=w_xv, w_xo=w_xo, norm_ff_g=norm_ff_g, w_ff1=w_ff1, w_ff2=w_ff2, final_g=final_g, loss_target=loss_target, m_norm_mix_g=m_norm_mix_g, m_w_in=m_w_in, m_conv_w=m_conv_w, m_conv_b=m_conv_b, m_dt_bias=m_dt_bias, m_a_log=m_a_log, m_d_skip=m_d_skip, m_ssd_norm_g=m_ssd_norm_g, m_sb_norm_g=m_sb_norm_g, m_w_out=m_w_out, m_norm_xa_g=m_norm_xa_g, m_norm_mem_g=m_norm_mem_g, m_w_xq=m_w_xq, m_w_xk=m_w_xk, m_w_xv=m_w_xv, m_w_xo=m_w_xo, m_norm_ff_g=m_norm_ff_g, m_w_ff1=m_w_ff1, m_w_ff2=m_w_ff2, m_final_g=m_final_g, v_norm_mix_g=v_norm_mix_g, v_w_in=v_w_in, v_conv_w=v_conv_w, v_conv_b=v_conv_b, v_dt_bias=v_dt_bias, v_a_log=v_a_log, v_d_skip=v_d_skip, v_ssd_norm_g=v_ssd_norm_g, v_sb_norm_g=v_sb_norm_g, v_w_out=v_w_out, v_norm_xa_g=v_norm_xa_g, v_norm_mem_g=v_norm_mem_g, v_w_xq=v_w_xq, v_w_xk=v_w_xk, v_w_xv=v_w_xv, v_w_xo=v_w_xo, v_norm_ff_g=v_norm_ff_g, v_w_ff1=v_w_ff1, v_w_ff2=v_w_ff2, v_final_g=v_final_g)
    weights = {n: given[n] for n in TWIN_WEIGHTS}
    shared = {n: given[n] for n in SHARED_INPUTS}
    per_example = {n: given[n] for n in ['x', 'mem']}
    grad_fn = _jax.value_and_grad(_loss, argnums=(0, 1))

    def one_microbatch(ex, loss_target):
        ex = dict(ex)
        diff = ex.pop(TWIN_DIFF_INPUT)
        return grad_fn(weights, diff, {**shared, **ex}, loss_target)

    if N_MICROBATCH == 1:
        loss, (grad_w, grad_x) = one_microbatch(per_example, given["loss_target"])
    else:
        def body(carry, xs):
            loss_sum, grad_sum = carry
            l_k, (gw_k, gx_k) = one_microbatch(xs[0], xs[1])
            with _jax.named_scope("update"):
                return (loss_sum + l_k, _jax.tree.map(_jnp.add, grad_sum, gw_k)), gx_k

        init = (_jnp.zeros((), _jnp.float32), _jax.tree.map(_jnp.zeros_like, weights))
        (loss, grad_w), grad_x = _jax.lax.scan(body, init, (per_example, given["loss_target"]))
    with _jax.named_scope("update"):
        delta_w, new_m, new_v = {}, {}, {}
        for n in TWIN_WEIGHTS:
            delta_w[n], new_m[n], new_v[n] = _adamw(weights[n], grad_w[n], given["m_" + n], given["v_" + n])
    return (loss, grad_x, *[grad_w[n] for n in TWIN_WEIGHTS], *[delta_w[n] for n in TWIN_WEIGHTS],
            *[new_m[n] for n in TWIN_WEIGHTS], *[new_v[n] for n in TWIN_WEIGHTS])
```

```python
import functools
import math

import jax
import jax.numpy as jnp
from jax import lax
from jax.experimental import pallas as pl
from jax.experimental.pallas import tpu as pltpu

F32 = jnp.float32
BF16 = jnp.bfloat16
MESH = pl.DeviceIdType.MESH

D_MODEL = 1024
DEPTH = 2
SSD_DIM = 512
SSD_HEAD_DIM = 64
SSD_HEADS = 8
SSD_GROUPS = 2
SSD_STATE = 64
CONV_K = 4
CHUNK = 128
SB_DIM = 512
SB_HEAD_DIM = 64
XA_HEADS = 4
XA_HEAD_DIM = 128
XA_DIM = 512
D_FF = 4096
EPS = 1e-5
GN = SSD_GROUPS * SSD_STATE
CONV_DIM = SSD_DIM + 2 * GN
IN_DIM = SSD_DIM + CONV_DIM + SSD_HEADS + 3 * SB_DIM
LANES = 128
DT_PAD = LANES
IN_PAD = SSD_DIM + CONV_DIM + 3 * SB_DIM + DT_PAD
Q_OFF = SSD_DIM + CONV_DIM
DT_OFF = Q_OFF + 3 * SB_DIM
HALO = 8

ADAM_LR = 0.001
ADAM_B1 = 0.9
ADAM_B2 = 0.999
ADAM_EPS = 1e-08
ADAM_WD = 0.01
ADAM_STEP = 10

N_CHIPS = 4
N_DEV = 8
PACK_COLS = 1024
VMEM_LIMIT = 56 * 1024 * 1024

MATS = (
    ("w_in", (D_MODEL, IN_DIM), 1),
    ("w_out", (D_MODEL, D_MODEL), 0),
    ("w_xq", (D_MODEL, XA_DIM), 0),
    ("w_xk", (D_MODEL, XA_DIM), 0),
    ("w_xv", (D_MODEL, XA_DIM), 0),
    ("w_xo", (XA_DIM, D_MODEL), 1),
    ("w_ff1", (D_MODEL, D_FF), 1),
    ("w_ff2", (D_FF, D_MODEL), 0),
)


def _shard_shape(shape, axis):
    s = list(shape)
    s[axis] //= N_CHIPS
    return tuple(s)


def _pack_rows_total():
    n = 0
    for _, shape, axis in MATS:
        ss = _shard_shape(shape, axis)
        n += ss[0] * ss[1] // PACK_COLS
    return n * DEPTH


PACK_USED = _pack_rows_total()
PACK_ROWS = -(-PACK_USED // 32) * 32
PACK_HALF = PACK_ROWS // 2

SMALL_ROWS = 24
_SM_PER_LAYER = 7
_SM_FINAL = 14
_SM_CONVW = 15
_SM_LOSS = 23


_NN = ((1,), (0,))
_NT = ((1,), (1,))
_TN = ((0,), (0,))


def _dg(a, b, dims):
    return lax.dot_general(a.astype(BF16), b.astype(BF16), (dims, ((), ())), preferred_element_type=F32)


@jax.custom_vjp
def mm_nn(a, b):
    return _dg(a, b, _NN)


@jax.custom_vjp
def mm_nt(a, b):
    return _dg(a, b, _NT)


@jax.custom_vjp
def mm_tn(a, b):
    return _dg(a, b, _TN)


def _nn_fwd(a, b):
    return _dg(a, b, _NN), (a, b)


def _nn_bwd(res, g):
    a, b = res
    return mm_nt(g, b).astype(a.dtype), mm_tn(a, g).astype(b.dtype)


def _nt_fwd(a, b):
    return _dg(a, b, _NT), (a, b)


def _nt_bwd(res, g):
    a, b = res
    return mm_nn(g, b).astype(a.dtype), mm_tn(g, a).astype(b.dtype)


def _tn_fwd(a, b):
    return _dg(a, b, _TN), (a, b)


def _tn_bwd(res, g):
    a, b = res
    return mm_nt(b, g).astype(a.dtype), mm_nn(a, g).astype(b.dtype)


mm_nn.defvjp(_nn_fwd, _nn_bwd)
mm_nt.defvjp(_nt_fwd, _nt_bwd)
mm_tn.defvjp(_tn_fwd, _tn_bwd)


def _rms(x, g):
    return x * lax.rsqrt(jnp.mean(x * x, axis=-1, keepdims=True) + EPS) * g


def _params(sem=None, vmem=VMEM_LIMIT):
    return pltpu.CompilerParams(dimension_semantics=sem, vmem_limit_bytes=vmem)


def _rowcall(name, fn, rows, fulls, row_out, acc_out=(), tm=256):
    s = rows[0].shape[0]
    tm = min(tm, s)
    nt = s // tm
    n_r, n_f, n_ro, n_ao = len(rows), len(fulls), len(row_out), len(acc_out)

    def body(*refs):
        ins = [r[...] for r in refs[: n_r + n_f]]
        outs = fn(*ins)
        o_refs = refs[n_r + n_f:]
        for o_ref, val in zip(o_refs[:n_ro], outs[:n_ro]):
            o_ref[...] = val.astype(o_ref.dtype)
        if n_ao:
            first = pl.program_id(0) == 0

            @pl.when(first)
            def _():
                for o_ref, val in zip(o_refs[n_ro:], outs[n_ro:]):
                    o_ref[...] = val.astype(o_ref.dtype)

            @pl.when(jnp.logical_not(first))
            def _():
                for o_ref, val in zip(o_refs[n_ro:], outs[n_ro:]):
                    o_ref[...] += val.astype(o_ref.dtype)

    in_specs = [pl.BlockSpec((tm, a.shape[1]), lambda i: (i, 0)) for a in rows]
    in_specs += [pl.BlockSpec(a.shape, lambda i: (0, 0), pipeline_mode=pl.Buffered(1)) for a in fulls]
    out_specs = [pl.BlockSpec((tm, c), lambda i: (i, 0)) for c, _ in row_out]
    out_specs += [pl.BlockSpec(shape, lambda i: (0, 0)) for shape, _ in acc_out]
    out_shape = [jax.ShapeDtypeStruct((s, c), dt) for c, dt in row_out]
    out_shape += [jax.ShapeDtypeStruct(shape, dt) for shape, dt in acc_out]
    return pl.pallas_call(
        body, name=name, grid=(nt,), in_specs=in_specs, out_specs=out_specs, out_shape=out_shape,
        compiler_params=_params(("arbitrary",)),
    )(*rows, *fulls)


def _mm_tn_call(name, a, b, tm, tn, tk):
    s, m = a.shape
    n = b.shape[1]
    tk = min(tk, s)

    def body(a_ref, b_ref, o_ref):
        d = _dg(a_ref[...], b_ref[...], _TN)
        first = pl.program_id(2) == 0

        @pl.when(first)
        def _():
            o_ref[...] = d

        @pl.when(jnp.logical_not(first))
        def _():
            o_ref[...] += d

    return pl.pallas_call(
        body, name=name, grid=(m // tm, n // tn, s // tk),
        in_specs=[pl.BlockSpec((tk, tm), lambda i, j, k: (k, i)), pl.BlockSpec((tk, tn), lambda i, j, k: (k, j))],
        out_specs=pl.BlockSpec((tm, tn), lambda i, j, k: (i, j)),
        out_shape=jax.ShapeDtypeStruct((m, n), F32),
        compiler_params=_params(("parallel", "parallel", "arbitrary")),
    )(a, b)


def _proj_tile(h, g, w):
    p = mm_nn(_rms(h, g), w)
    return (p[:, :SSD_DIM], p[:, SSD_DIM:Q_OFF], p[:, Q_OFF:Q_OFF + SB_DIM],
            p[:, Q_OFF + SB_DIM:Q_OFF + 2 * SB_DIM], p[:, Q_OFF + 2 * SB_DIM:DT_OFF], p[:, DT_OFF:])


def _proj_fwd(tag, h, g, w):
    return _rowcall(
        "proj_fwd" + tag, _proj_tile, [h], [g, w],
        [(SSD_DIM, F32), (CONV_DIM, F32), (SB_DIM, BF16), (SB_DIM, BF16), (SB_DIM, BF16), (DT_PAD, F32)], tm=512)


def _proj_bwd(tag, h, dh_out, dz, dxbc, dq, dk, dv, ddt, g, w):
    def fn(h, dh_out, dz, dxbc, dq, dk, dv, ddt, g, w):
        dp = jnp.concatenate([dz.astype(BF16), dxbc.astype(BF16), dq.astype(BF16), dk.astype(BF16),
                              dv.astype(BF16), ddt.astype(BF16)], axis=1)
        hn, vjp = jax.vjp(_rms, h, g)
        dh, dg = vjp(mm_nt(dp, w))
        return dh_out + dh, hn, dp, dg

    return _rowcall(
        "proj_bwd" + tag, fn, [h, dh_out, dz, dxbc, dq, dk, dv, ddt], [g, w],
        [(D_MODEL, F32), (D_MODEL, BF16), (IN_PAD, BF16)], [((1, D_MODEL), F32)], tm=256)


def _shift_down(x, tail, j):
    if j == 0:
        return x
    r = pltpu.roll(x, j, 0)
    rt = pltpu.roll(tail, j, 0)
    row = lax.broadcasted_iota(jnp.int32, (HALO, x.shape[1]), 0)
    first = jnp.where(row < j, rt, r[:HALO])
    if x.shape[0] == HALO:
        return first
    return jnp.concatenate([first, r[HALO:]], axis=0)


def _shift_up(x, head, j):
    if j == 0:
        return x
    n = x.shape[0]
    r = pltpu.roll(x, n - j, 0)
    rh = pltpu.roll(head, HALO - j, 0)
    row = lax.broadcasted_iota(jnp.int32, (HALO, x.shape[1]), 0)
    return jnp.concatenate([r[:n - HALO], jnp.where(row >= HALO - j, rh, r[n - HALO:])], axis=0)


def _conv_pre(x, tail, w, b):
    acc = b + w[CONV_K - 1:CONV_K] * x
    for j in range(1, CONV_K):
        acc = acc + w[CONV_K - 1 - j:CONV_K - j] * _shift_down(x, tail, j)
    return acc


def _dsilu(p):
    s = jax.nn.sigmoid(p)
    return s * (1.0 + p * (1.0 - s))


def _conv_fwd(tag, xbc, w, b, tc=512):
    s, c = xbc.shape
    tc = min(tc, s)
    per = tc // HALO

    def body(x_ref, prev_ref, w_ref, b_ref, o_ref):
        tail = jnp.where(pl.program_id(0) > 0, prev_ref[...], 0.0)
        o_ref[...] = jax.nn.silu(_conv_pre(x_ref[...], tail, w_ref[...], b_ref[...]))

    return pl.pallas_call(
        body, name="conv_fwd" + tag, grid=(s // tc,),
        in_specs=[pl.BlockSpec((tc, c), lambda i: (i, 0)),
                  pl.BlockSpec((HALO, c), lambda i: (jnp.maximum(i * per - 1, 0), 0)),
                  pl.BlockSpec((CONV_K, c), lambda i: (0, 0)), pl.BlockSpec((1, c), lambda i: (0, 0))],
        out_specs=pl.BlockSpec((tc, c), lambda i: (i, 0)),
        out_shape=jax.ShapeDtypeStruct((s, c), F32),
        compiler_params=_params(("arbitrary",)),
    )(xbc, xbc, w, b)


def _conv_bwd(tag, xbc, dact, w, b, tc=512):
    s, c = xbc.shape
    tc = min(tc, s)
    per = tc // HALO
    nt = s // tc
    last_blk = s // HALO - 1

    def body(x_ref, prev_ref, next_ref, d_ref, dnext_ref, w_ref, b_ref, dx_ref, dw_ref, db_ref):
        i = pl.program_id(0)
        x = x_ref[...]
        wv = w_ref[...]
        tail = jnp.where(i > 0, prev_ref[...], 0.0)
        dpre = d_ref[...] * _dsilu(_conv_pre(x, tail, wv, b_ref[...]))
        pre_n = _conv_pre(next_ref[...], x[tc - HALO:], wv, b_ref[...])
        dpre_n = jnp.where(i < nt - 1, dnext_ref[...] * _dsilu(pre_n), 0.0)
        dx = wv[CONV_K - 1:CONV_K] * dpre
        for j in range(1, CONV_K):
            dx = dx + wv[CONV_K - 1 - j:CONV_K - j] * _shift_up(dpre, dpre_n, j)
        dx_ref[...] = dx
        dws = [jnp.sum(dpre * _shift_down(x, tail, CONV_K - 1 - k), axis=0, keepdims=True) for k in range(CONV_K)]
        dwv = jnp.concatenate(dws, axis=0)
        dbv = jnp.sum(dpre, axis=0, keepdims=True)

        @pl.when(i == 0)
        def _():
            dw_ref[...] = dwv
            db_ref[...] = dbv

        @pl.when(i > 0)
        def _():
            dw_ref[...] += dwv
            db_ref[...] += dbv

    tile = pl.BlockSpec((tc, c), lambda i: (i, 0))
    prev = pl.BlockSpec((HALO, c), lambda i: (jnp.maximum(i * per - 1, 0), 0))
    nxt = pl.BlockSpec((HALO, c), lambda i: (jnp.minimum((i + 1) * per, last_blk), 0))
    return pl.pallas_call(
        body, name="conv_bwd" + tag, grid=(nt,),
        in_specs=[tile, prev, nxt, tile, nxt, pl.BlockSpec((CONV_K, c), lambda i: (0, 0)),
                  pl.BlockSpec((1, c), lambda i: (0, 0))],
        out_specs=[tile, pl.BlockSpec((CONV_K, c), lambda i: (0, 0)), pl.BlockSpec((1, c), lambda i: (0, 0))],
        out_shape=[jax.ShapeDtypeStruct((s, c), F32), jax.ShapeDtypeStruct((CONV_K, c), F32),
                   jax.ShapeDtypeStruct((1, c), F32)],
        compiler_params=_params(("arbitrary",)),
    )(xbc, xbc, xbc, dact, dact, w, b)


def _ssd_chunk(xs, bm, cm, dtr, z, dt_bias, a_log, d_skip, g, s_prev):
    n = CHUNK
    row = lax.broadcasted_iota(jnp.int32, (n, n), 0)
    col = lax.broadcasted_iota(jnp.int32, (n, n), 1)
    causal = row >= col
    dt = jax.nn.softplus(dtr + dt_bias)
    a_c = dt * (-jnp.exp(a_log))
    hi = lax.Precision.HIGHEST
    a_cum = jnp.dot(causal.astype(F32), a_c, precision=hi, preferred_element_type=F32)
    a_cum_t = lax.dot_general(a_c, (row <= col).astype(F32), (_TN, ((), ())), precision=hi,
                              preferred_element_type=F32)
    p, st = SSD_HEAD_DIM, SSD_STATE
    cb = [mm_nt(cm[:, k * st:(k + 1) * st], bm[:, k * st:(k + 1) * st]) for k in range(SSD_GROUPS)]
    ys, s_new = [], []
    for h in range(SSD_HEADS):
        k = h // (SSD_HEADS // SSD_GROUPS)
        acol = a_cum[:, h:h + 1]
        decay = jnp.exp(jnp.where(causal, acol - a_cum_t[h:h + 1, :], -jnp.inf))
        xh = xs[:, h * p:(h + 1) * p]
        xdt = xh * dt[:, h:h + 1]
        a_last = a_cum[n - 1:n, h:h + 1]
        sp = s_prev[h * p:(h + 1) * p, :]
        bg = bm[:, k * st:(k + 1) * st]
        cg = cm[:, k * st:(k + 1) * st]
        s_new.append(sp * jnp.exp(a_last) + mm_tn(xdt * jnp.exp(a_last - acol), bg))
        y = mm_nn(cb[k] * decay, xdt) + mm_nt(cg, sp) * jnp.exp(acol) + d_skip[:, h:h + 1] * xh
        ys.append(y)
    y = jnp.concatenate(ys, axis=1) * jax.nn.silu(z)
    return _rms(y, g), jnp.concatenate(s_new, axis=0)


def _split_xbc(t):
    return t[:, :SSD_DIM], t[:, SSD_DIM:SSD_DIM + GN], t[:, SSD_DIM + GN:]


def _ssd_fwd(tag, xact, dtr, z, dt_bias, a_log, d_skip, g):
    s = xact.shape[0]
    nc = s // CHUNK
    srows = SSD_HEADS * SSD_HEAD_DIM

    def body(x_ref, dt_ref, z_ref, b_ref, al_ref, ds_ref, g_ref, y_ref, st_ref, state):
        @pl.when(pl.program_id(0) == 0)
        def _():
            state[...] = jnp.zeros_like(state)

        sp = state[...]
        st_ref[0] = sp
        xs, bm, cm = _split_xbc(x_ref[...])
        y, sn = _ssd_chunk(xs, bm, cm, dt_ref[...][:, :SSD_HEADS], z_ref[...], b_ref[...], al_ref[...],
                           ds_ref[...], g_ref[...], sp)
        y_ref[...] = y
        state[...] = sn

    small = pl.BlockSpec((1, SSD_HEADS), lambda i: (0, 0))
    return pl.pallas_call(
        body, name="ssd_fwd" + tag, grid=(nc,),
        in_specs=[pl.BlockSpec((CHUNK, CONV_DIM), lambda i: (i, 0)), pl.BlockSpec((CHUNK, DT_PAD), lambda i: (i, 0)),
                  pl.BlockSpec((CHUNK, SSD_DIM), lambda i: (i, 0)), small, small, small,
                  pl.BlockSpec((1, SSD_DIM), lambda i: (0, 0))],
        out_specs=[pl.BlockSpec((CHUNK, SSD_DIM), lambda i: (i, 0)),
                   pl.BlockSpec((1, srows, SSD_STATE), lambda i: (i, 0, 0))],
        out_shape=[jax.ShapeDtypeStruct((s, SSD_DIM), F32), jax.ShapeDtypeStruct((nc, srows, SSD_STATE), F32)],
        scratch_shapes=[pltpu.VMEM((srows, SSD_STATE), F32)],
        compiler_params=_params(("arbitrary",)),
    )(xact, dtr, z, dt_bias, a_log, d_skip, g)


def _ssd_bwd(tag, xact, dtr, z, states, dy, dt_bias, a_log, d_skip, g):
    s = xact.shape[0]
    nc = s // CHUNK
    srows = SSD_HEADS * SSD_HEAD_DIM

    def body(x_ref, dt_ref, z_ref, sp_ref, dy_ref, b_ref, al_ref, ds_ref, g_ref,
             dx_ref, ddt_ref, dz_ref, db_ref, dal_ref, dds_ref, dg_ref, dstate):
        first = pl.program_id(0) == 0

        @pl.when(first)
        def _():
            dstate[...] = jnp.zeros_like(dstate)

        xs, bm, cm = _split_xbc(x_ref[...])
        _, vjp = jax.vjp(_ssd_chunk, xs, bm, cm, dt_ref[...][:, :SSD_HEADS], z_ref[...], b_ref[...], al_ref[...],
                         ds_ref[...], g_ref[...], sp_ref[0])
        dxs, dbm, dcm, ddt, dz, db, dal, dds, dg, dsp = vjp((dy_ref[...], dstate[...]))
        dx_ref[...] = jnp.concatenate([dxs, dbm, dcm], axis=1)
        ddt_ref[...] = jnp.concatenate([ddt, jnp.zeros((CHUNK, DT_PAD - SSD_HEADS), F32)], axis=1)
        dz_ref[...] = dz
        dstate[...] = dsp

        @pl.when(first)
        def _():
            db_ref[...] = db
            dal_ref[...] = dal
            dds_ref[...] = dds
            dg_ref[...] = dg

        @pl.when(jnp.logical_not(first))
        def _():
            db_ref[...] += db
            dal_ref[...] += dal
            dds_ref[...] += dds
            dg_ref[...] += dg

    def rev(c):
        return lambda i: (nc - 1 - i, 0)

    small = pl.BlockSpec((1, SSD_HEADS), lambda i: (0, 0))
    gspec = pl.BlockSpec((1, SSD_DIM), lambda i: (0, 0))
    return pl.pallas_call(
        body, name="ssd_bwd" + tag, grid=(nc,),
        in_specs=[pl.BlockSpec((CHUNK, CONV_DIM), rev(0)), pl.BlockSpec((CHUNK, DT_PAD), rev(0)),
                  pl.BlockSpec((CHUNK, SSD_DIM), rev(0)),
                  pl.BlockSpec((1, srows, SSD_STATE), lambda i: (nc - 1 - i, 0, 0)),
                  pl.BlockSpec((CHUNK, SSD_DIM), rev(0)), small, small, small, gspec],
        out_specs=[pl.BlockSpec((CHUNK, CONV_DIM), rev(0)), pl.BlockSpec((CHUNK, DT_PAD), rev(0)),
                   pl.BlockSpec((CHUNK, SSD_DIM), rev(0)), small, small, small, gspec],
        out_shape=[jax.ShapeDtypeStruct((s, CONV_DIM), F32), jax.ShapeDtypeStruct((s, DT_PAD), F32),
                   jax.ShapeDtypeStruct((s, SSD_DIM), F32), jax.ShapeDtypeStruct((1, SSD_HEADS), F32),
                   jax.ShapeDtypeStruct((1, SSD_HEADS), F32), jax.ShapeDtypeStruct((1, SSD_HEADS), F32),
                   jax.ShapeDtypeStruct((1, SSD_DIM), F32)],
        scratch_shapes=[pltpu.VMEM((srows, SSD_STATE), F32)],
        compiler_params=_params(("arbitrary",)),
    )(xact, dtr, z, states, dy, dt_bias, a_log, d_skip, g)


TQ = 128
TK = 128
SB_SCALE = 1.0 / math.sqrt(SB_HEAD_DIM)


def _split2(x):
    hi = x.astype(BF16)
    return hi, (x - hi.astype(F32)).astype(BF16)


def _sb_logits(qh, kb, t0, s0):
    z = _dg(qh, kb, _NT) * SB_SCALE
    t_pos = t0 + lax.broadcasted_iota(jnp.int32, (TQ, TK), 0)
    s_pos = s0 + lax.broadcasted_iota(jnp.int32, (TQ, TK), 1)
    mask = s_pos < t_pos
    lb = jnp.minimum(z, 0.0) - jnp.log(1.0 + jnp.exp(-jnp.abs(z)))
    ls = jnp.where(mask, lb - z, 0.0)
    return lb, ls, mask


def _lane_sums(x, u):
    hi, lo = _split2(x)
    return _dg(hi, u, _NN) + _dg(lo, u, _NN)


def _tri(cmp):
    j = lax.broadcasted_iota(jnp.int32, (TK, TK), 0)
    s = lax.broadcasted_iota(jnp.int32, (TK, TK), 1)
    return cmp(j, s).astype(BF16)


def _sb_fwd(tag, q, k, v):
    s = q.shape[0]
    npair = SB_DIM // LANES

    def body(q_ref, k_ref, v_ref, o_ref, t_ref):
        tq = pl.program_id(1)
        qp = q_ref[...]
        lane = lax.broadcasted_iota(jnp.int32, (1, LANES), 1)
        u_gt = _tri(lambda j, s: j > s)
        out = jnp.zeros((TQ, LANES), F32)
        tot = jnp.zeros((TQ, LANES), F32)
        for hh in range(LANES // SB_HEAD_DIM):
            hm = (lane // SB_HEAD_DIM) == hh
            qh = jnp.where(hm, qp, jnp.zeros_like(qp))

            def step(j, carry):
                r, acc = carry
                sb = tq - j
                off = pl.multiple_of(sb * TK, TK)
                kb = k_ref[pl.ds(off, TK), :]
                vb = v_ref[pl.ds(off, TK), :]
                lb, ls, mask = _sb_logits(qh, kb, tq * TQ, sb * TK)
                w = jnp.where(mask, jnp.exp(lb + r + _lane_sums(ls, u_gt)), 0.0)
                return r + jnp.sum(ls, axis=1, keepdims=True), acc + _dg(w, vb, _NN)

            r, acc = lax.fori_loop(0, tq + 1, step, (jnp.zeros((TQ, 1), F32), jnp.zeros((TQ, LANES), F32)))
            out = out + jnp.where(hm, acc, 0.0)
            tot = tot + jnp.where(hm, r, 0.0)
        o_ref[...] = out
        t_ref[...] = tot

    tile = pl.BlockSpec((TQ, LANES), lambda p, t: (t, p))
    full = pl.BlockSpec((s, LANES), lambda p, t: (0, p))
    return pl.pallas_call(
        body, name="sb_fwd" + tag, grid=(npair, s // TQ),
        in_specs=[tile, full, full], out_specs=[tile, tile],
        out_shape=[jax.ShapeDtypeStruct((s, SB_DIM), F32)] * 2,
        compiler_params=_params(("parallel", "arbitrary")),
    )(q, k, v)


def _sb_bwd(tag, q, k, v, tot, do):
    s = q.shape[0]
    npair = SB_DIM // LANES

    def body(q_ref, k_ref, v_ref, t_ref, do_ref, dq_ref, dk_ref, dv_ref):
        tq = pl.program_id(1)

        @pl.when(tq == 0)
        def _():
            dk_ref[...] = jnp.zeros_like(dk_ref)
            dv_ref[...] = jnp.zeros_like(dv_ref)

        qp = q_ref[...]
        dop = do_ref[...]
        totp = t_ref[...]
        lane = lax.broadcasted_iota(jnp.int32, (1, LANES), 1)
        u_le = _tri(lambda j, s: j <= s)
        u_lt = _tri(lambda j, s: j < s)
        dq = jnp.zeros((TQ, LANES), F32)
        for hh in range(LANES // SB_HEAD_DIM):
            hm = (lane // SB_HEAD_DIM) == hh
            qh = jnp.where(hm, qp, jnp.zeros_like(qp))
            doh = jnp.where(hm, dop, 0.0).astype(BF16)
            total = jnp.sum(jnp.where(lane == hh * SB_HEAD_DIM, totp, 0.0), axis=1, keepdims=True)

            def step(sb, carry):
                pre, gc, acc = carry
                off = pl.multiple_of(sb * TK, TK)
                kb = k_ref[pl.ds(off, TK), :]
                vb = v_ref[pl.ds(off, TK), :]
                lb, ls, mask = _sb_logits(qh, kb, tq * TQ, sb * TK)
                w = jnp.where(mask, jnp.exp(lb + (total - (pre + _lane_sums(ls, u_le)))), 0.0)
                g = w * _dg(doh, vb, _NT)
                g_left = gc + _lane_sums(g, u_lt)
                beta = jnp.exp(lb)
                dz = jnp.where(mask, g * (1.0 - beta) - beta * g_left, 0.0) * SB_SCALE
                dzb = dz.astype(BF16)
                dk_ref[pl.ds(off, TK), :] += _dg(dzb, qh, _TN)
                dv_ref[pl.ds(off, TK), :] += _dg(w, doh, _TN)
                return (pre + jnp.sum(ls, axis=1, keepdims=True), gc + jnp.sum(g, axis=1, keepdims=True),
                        acc + _dg(dzb, kb, _NN))

            zero = jnp.zeros((TQ, 1), F32)
            _, _, acc = lax.fori_loop(0, tq + 1, step, (zero, zero, jnp.zeros((TQ, LANES), F32)))
            dq = dq + jnp.where(hm, acc, 0.0)
        dq_ref[...] = dq

    tile = pl.BlockSpec((TQ, LANES), lambda p, t: (t, p))
    full = pl.BlockSpec((s, LANES), lambda p, t: (0, p))
    return pl.pallas_call(
        body, name="sb_bwd" + tag, grid=(npair, s // TQ),
        in_specs=[tile, full, full, tile, tile],
        out_specs=[tile, full, full],
        out_shape=[jax.ShapeDtypeStruct((s, SB_DIM), F32)] * 3,
        compiler_params=_params(("parallel", "arbitrary")),
    )(q, k, v, tot, do)


def _out_tile(y_ssd, o, sb_g, w_out):
    y_all = jnp.concatenate([y_ssd, _rms(o, sb_g)], axis=1)
    return mm_nn(y_all, w_out)


def _out_fwd(tag, h, y_ssd, o, sb_g, w_out):
    return _rowcall("out_fwd" + tag, lambda h, y, o, g, w: (h + _out_tile(y, o, g, w),),
                    [h, y_ssd, o], [sb_g, w_out], [(D_MODEL, F32)], tm=512)[0]


def _out_bwd(tag, y_ssd, o, dh, sb_g, w_out):
    def fn(y, o, dh, g, w):
        _, vjp = jax.vjp(_out_tile, y, o, g, w.astype(F32))
        return vjp(dh)

    return _rowcall("out_bwd" + tag, fn, [y_ssd, o, dh], [sb_g, w_out], [(SSD_DIM, F32), (SB_DIM, F32)],
                    [((1, SB_DIM), F32), ((D_MODEL, D_MODEL), F32)], tm=256)


def _mem_tile(mem, g, w_k, w_v):
    m = _rms(mem, g)
    return mm_nn(m, w_k), mm_nn(m, w_v)


def _mem_fwd(tag, mem, g, w_k, w_v):
    return _rowcall("mem_fwd" + tag, _mem_tile, [mem], [g, w_k, w_v], [(XA_DIM, F32), (XA_DIM, F32)], tm=256)


def _mem_bwd(tag, mem, dkx, dvx, g, w_k, w_v):
    def fn(mem, dkx, dvx, g, w_k, w_v):
        _, vjp = jax.vjp(lambda g, a, b: _mem_tile(mem, g, a, b), g, w_k.astype(F32), w_v.astype(F32))
        return vjp((dkx, dvx))

    return _rowcall("mem_bwd" + tag, fn, [mem, dkx, dvx], [g, w_k, w_v], [],
                    [((1, D_MODEL), F32), ((D_MODEL, XA_DIM), F32), ((D_MODEL, XA_DIM), F32)], tm=256)


def _xattn_tile(h, g, w_q, kx, vx, w_o):
    q = mm_nn(_rms(h, g), w_q)
    scale = 1.0 / math.sqrt(XA_HEAD_DIM)
    outs = []
    for i in range(XA_HEADS):
        sl = slice(i * XA_HEAD_DIM, (i + 1) * XA_HEAD_DIM)
        p = jax.nn.softmax(mm_nt(q[:, sl], kx[:, sl]) * scale, axis=-1)
        outs.append(mm_nn(p, vx[:, sl]))
    return mm_nn(jnp.concatenate(outs, axis=1), w_o)


def _xattn_fwd(tag, h, g, w_q, kx, vx, w_o):
    return _rowcall("xattn_fwd" + tag, lambda h, g, wq, kx, vx, wo: (h + _xattn_tile(h, g, wq, kx, vx, wo),),
                    [h], [g, w_q, kx, vx, w_o], [(D_MODEL, F32)], tm=512)[0]


def _xattn_bwd(tag, h, dh_out, g, w_q, kx, vx, w_o):
    def fn(h, dh_out, g, w_q, kx, vx, w_o):
        _, vjp = jax.vjp(_xattn_tile, h, g, w_q.astype(F32), kx, vx, w_o.astype(F32))
        dh, dg, dwq, dkx, dvx, dwo = vjp(dh_out)
        return dh_out + dh, dg, dwq, dkx, dvx, dwo

    mlen = kx.shape[0]
    return _rowcall("xattn_bwd" + tag, fn, [h, dh_out], [g, w_q, kx, vx, w_o], [(D_MODEL, F32)],
                    [((1, D_MODEL), F32), ((D_MODEL, XA_DIM), F32), ((mlen, XA_DIM), F32), ((mlen, XA_DIM), F32),
                     ((XA_DIM, D_MODEL), F32)], tm=256)


def _mlp_fwd(tag, h, g, w1, w2):
    def fn(h, g, w1, w2):
        u = jnp.square(jnp.maximum(mm_nn(_rms(h, g), w1), 0.0))
        return (h + mm_nn(u, w2),)

    return _rowcall("mlp_fwd" + tag, fn, [h], [g, w1, w2], [(D_MODEL, F32)], tm=256)[0]


def _mlp_bwd(tag, h, dh_out, g, w1, w2):
    def fn(h, dh_out, g, w1, w2):
        hn, vjp = jax.vjp(_rms, h, g)
        r = jnp.maximum(mm_nn(hn, w1), 0.0)
        dob = dh_out.astype(BF16)
        dp = mm_nt(dob, w2) * (2.0 * r)
        dh, dg = vjp(mm_nt(dp, w1))
        return dh_out + dh, hn, dp, r * r, dob, dg

    return _rowcall("mlp_bwd" + tag, fn, [h, dh_out], [g, w1, w2],
                    [(D_MODEL, F32), (D_MODEL, BF16), (D_FF, BF16), (D_FF, BF16), (D_MODEL, BF16)],
                    [((1, D_MODEL), F32)], tm=256)


def _head(h, g, target):
    def lossfn(h, g, t):
        err = jnp.square(_rms(h, g) - t)
        return 0.5 * jnp.sum(jnp.mean(err, axis=-1))

    def fn(h, t, g):
        loss, vjp = jax.vjp(lambda h, g: lossfn(h, g, t), h, g)
        dh, dg = vjp(jnp.ones((), F32))
        return dh, jnp.full((1, LANES), loss, F32), dg

    return _rowcall("head", fn, [h, target], [g], [(D_MODEL, F32)], [((1, LANES), F32), ((1, D_MODEL), F32)], tm=512)


def _row(v):
    return v.reshape(1, -1)


def _local_step(x, mem, target, sw, mats):
    h = x
    saved = []
    for l in range(DEPTH):
        tag = str(l)
        m = mats[l]
        z, xbc, q, k, v, dtr = _proj_fwd(tag, h, _row(sw["norm_mix_g"][l]), m["w_in"])
        xact = _conv_fwd(tag, xbc, sw["conv_w"][l], _row(sw["conv_b"][l]))
        y_ssd, states = _ssd_fwd(tag, xact, dtr, z, _row(sw["dt_bias"][l]), _row(sw["a_log"][l]),
                                 _row(sw["d_skip"][l]), _row(sw["ssd_norm_g"][l]))
        o, sb_tot = _sb_fwd(tag, q, k, v)
        h1 = _out_fwd(tag, h, y_ssd, o, _row(sw["sb_norm_g"][l]), m["w_out"])
        kx, vx = _mem_fwd(tag, mem, _row(sw["norm_mem_g"][l]), m["w_xk"], m["w_xv"])
        h2 = _xattn_fwd(tag, h1, _row(sw["norm_xa_g"][l]), m["w_xq"], kx, vx, m["w_xo"])
        h3 = _mlp_fwd(tag, h2, _row(sw["norm_ff_g"][l]), m["w_ff1"], m["w_ff2"])
        saved.append((h, z, xbc, q, k, v, dtr, xact, y_ssd, states, o, sb_tot, h1, kx, vx, h2))
        h = h3

    dh, loss, d_final = _head(h, _row(sw["final_g"]), target)
    gm = [dict() for _ in range(DEPTH)]
    gs = {name: [None] * DEPTH for name in ("norm_mix_g", "conv_w", "conv_b", "dt_bias", "a_log", "d_skip",
                                            "ssd_norm_g", "sb_norm_g", "norm_xa_g", "norm_mem_g", "norm_ff_g")}
    for l in reversed(range(DEPTH)):
        tag = str(l)
        m = mats[l]
        h0, z, xbc, q, k, v, dtr, xact, y_ssd, states, o, sb_tot, h1, kx, vx, h2 = saved[l]
        dh2, hn_b, dp_b, a_b, do_b, gs["norm_ff_g"][l] = _mlp_bwd(
            tag, h2, dh, _row(sw["norm_ff_g"][l]), m["w_ff1"], m["w_ff2"])
        gm[l]["w_ff1"] = _mm_tn_call("dw_ff1" + tag, hn_b, dp_b, 512, 1024, 2048)
        gm[l]["w_ff2"] = _mm_tn_call("dw_ff2" + tag, a_b, do_b, 1024, 1024, 2048)
        dh1, gs["norm_xa_g"][l], gm[l]["w_xq"], dkx, dvx, gm[l]["w_xo"] = _xattn_bwd(
            tag, h1, dh2, _row(sw["norm_xa_g"][l]), m["w_xq"], kx, vx, m["w_xo"])
        gs["norm_mem_g"][l], gm[l]["w_xk"], gm[l]["w_xv"] = _mem_bwd(
            tag, mem, dkx, dvx, _row(sw["norm_mem_g"][l]), m["w_xk"], m["w_xv"])
        dy_ssd, do, gs["sb_norm_g"][l], gm[l]["w_out"] = _out_bwd(
            tag, y_ssd, o, dh1, _row(sw["sb_norm_g"][l]), m["w_out"])
        dq, dk, dv = _sb_bwd(tag, q, k, v, sb_tot, do)
        dxact, ddtr, dz, gs["dt_bias"][l], gs["a_log"][l], gs["d_skip"][l], gs["ssd_norm_g"][l] = _ssd_bwd(
            tag, xact, dtr, z, states, dy_ssd, _row(sw["dt_bias"][l]), _row(sw["a_log"][l]),
            _row(sw["d_skip"][l]), _row(sw["ssd_norm_g"][l]))
        dxbc, gs["conv_w"][l], gs["conv_b"][l] = _conv_bwd(tag, xbc, dxact, sw["conv_w"][l], _row(sw["conv_b"][l]))
        dh, hn_b, dp_b, gs["norm_mix_g"][l] = _proj_bwd(
            tag, h0, dh1, dz, dxbc, dq, dk, dv, ddtr, _row(sw["norm_mix_g"][l]), m["w_in"])
        gm[l]["w_in"] = _mm_tn_call("dw_in" + tag, hn_b, dp_b, 512, IN_PAD, 1024)
    gs["final_g"] = d_final
    return loss, dh, gm, gs


ANY = pl.BlockSpec(memory_space=pl.ANY)
VMEM_SPEC = pl.BlockSpec(memory_space=pltpu.VMEM)


def _place():
    return lax.axis_index("x"), lax.axis_index("y"), lax.axis_index("c")


def _other_chips(x, y):
    return [(1 - x, y), (x, 1 - y), (1 - x, 1 - y)]


def _gather_weights(pack):
    rows = pack.shape[0]
    half = rows // 2

    def body(p_ref, o_ref, send_sems, recv_sems, local_sem):
        x, y, c = _place()
        me_chip = 2 * x + y
        sibling = (x, y, 1 - c)
        chips = _other_chips(x, y)

        def part(chip, hh):
            return o_ref.at[chip, pl.ds(pl.multiple_of(hh * half, 16), half), :]

        def copy(kk, src, dst, to):
            return pltpu.make_async_remote_copy(src_ref=src, dst_ref=dst, send_sem=send_sems.at[kk],
                                                recv_sem=recv_sems.at[kk], device_id=to, device_id_type=MESH)

        mine = pltpu.make_async_copy(p_ref, o_ref.at[me_chip], local_sem)
        mine.start()
        my_half = p_ref.at[pl.ds(pl.multiple_of(c * half, 16), half), :]
        first = [copy(kk, my_half, part(me_chip, c), (cx, cy, c)) for kk, (cx, cy) in enumerate(chips)]
        for cp in first:
            cp.start()
        passed = []
        for kk, (cx, cy) in enumerate(chips):
            got = part(2 * cx + cy, c)
            copy(kk, got, got, (x, y, c)).wait_recv()
            fwd = copy(3 + kk, got, got, sibling)
            fwd.start()
            passed.append(fwd)
        for kk, (cx, cy) in enumerate(chips):
            got = part(2 * cx + cy, 1 - c)
            copy(3 + kk, got, got, (x, y, c)).wait_recv()
        for cp in first + passed:
            cp.wait_send()
        mine.wait()

    return pl.pallas_call(
        body, name="gather_weights", in_specs=[ANY], out_specs=ANY,
        out_shape=jax.ShapeDtypeStruct((N_CHIPS, rows, PACK_COLS), pack.dtype),
        scratch_shapes=[pltpu.SemaphoreType.DMA((6,)), pltpu.SemaphoreType.DMA((6,)), pltpu.SemaphoreType.DMA],
    )(pack)


def _gather_small(tag, buf):
    shape = buf.shape

    def body(b_ref, o_ref, sum_ref, send_sems, recv_sems, local_sem):
        x, y, c = _place()
        me = 4 * x + 2 * y + c
        mine = pltpu.make_async_copy(b_ref, o_ref.at[me], local_sem)
        mine.start()
        flips = [(dx, dy, dc) for dx in (0, 1) for dy in (0, 1) for dc in (0, 1) if (dx, dy, dc) != (0, 0, 0)]
        sends = []

        def peer(dx, dy, dc):
            return (1 - x if dx else x, 1 - y if dy else y, 1 - c if dc else c)

        for kk, flip in enumerate(flips):
            cp = pltpu.make_async_remote_copy(src_ref=b_ref, dst_ref=o_ref.at[me], send_sem=send_sems.at[kk],
                                              recv_sem=recv_sems.at[kk], device_id=peer(*flip), device_id_type=MESH)
            cp.start()
            sends.append(cp)
        for kk, flip in enumerate(flips):
            px, py, pc = peer(*flip)
            frm = 4 * px + 2 * py + pc
            pltpu.make_async_remote_copy(src_ref=b_ref, dst_ref=o_ref.at[frm], send_sem=send_sems.at[kk],
                                         recv_sem=recv_sems.at[kk], device_id=(x, y, c),
                                         device_id_type=MESH).wait_recv()
        for cp in sends:
            cp.wait_send()
        mine.wait()
        total = o_ref[0]
        for d in range(1, N_DEV):
            total = total + o_ref[d]
        sum_ref[...] = total

    return pl.pallas_call(
        body, name="gather_small" + tag, in_specs=[VMEM_SPEC], out_specs=[VMEM_SPEC, VMEM_SPEC],
        out_shape=[jax.ShapeDtypeStruct((N_DEV,) + shape, buf.dtype), jax.ShapeDtypeStruct(shape, buf.dtype)],
        scratch_shapes=[pltpu.SemaphoreType.DMA((N_DEV - 1,)), pltpu.SemaphoreType.DMA((N_DEV - 1,)),
                        pltpu.SemaphoreType.DMA],
    )(buf)


def _swap_halves(g):
    half = g.shape[1] // 2

    def body(g_ref, r_ref, send_sem, recv_sem):
        x, y, c = _place()
        src = g_ref.at[:, pl.ds(pl.multiple_of((1 - c) * half, 8), half), :]
        cp = pltpu.make_async_remote_copy(src_ref=src, dst_ref=r_ref, send_sem=send_sem, recv_sem=recv_sem,
                                          device_id=(x, y, 1 - c), device_id_type=MESH)
        cp.start()
        cp.wait()

    return pl.pallas_call(
        body, name="swap_halves", in_specs=[ANY], out_specs=ANY,
        out_shape=jax.ShapeDtypeStruct((N_CHIPS, half, PACK_COLS), g.dtype),
        scratch_shapes=[pltpu.SemaphoreType.DMA, pltpu.SemaphoreType.DMA],
    )(g)


def _add_halves(g, r, tr=208):
    half = r.shape[1]
    core = lax.axis_index("c").astype(jnp.int32).reshape(1)
    per = half // tr

    def body(c_ref, g_ref, r_ref, o_ref):
        o_ref[...] = (g_ref[...] + r_ref[...]).astype(o_ref.dtype)

    return pl.pallas_call(
        body, name="add_halves",
        grid_spec=pltpu.PrefetchScalarGridSpec(
            num_scalar_prefetch=1, grid=(N_CHIPS, per),
            in_specs=[pl.BlockSpec((1, tr, PACK_COLS), lambda p, i, c_ref: (p, c_ref[0] * per + i, 0)),
                      pl.BlockSpec((1, tr, PACK_COLS), lambda p, i, c_ref: (p, i, 0))],
            out_specs=pl.BlockSpec((1, tr, PACK_COLS), lambda p, i, c_ref: (p, i, 0))),
        out_shape=jax.ShapeDtypeStruct(r.shape, BF16),
        compiler_params=_params(("arbitrary", "arbitrary")),
    )(core, g, r)


def _scatter_chips(sb):
    def body(s_ref, o_ref, send_sems, recv_sems, local_sem):
        x, y, c = _place()
        me_chip = 2 * x + y
        chips = _other_chips(x, y)
        mine = pltpu.make_async_copy(s_ref.at[me_chip], o_ref.at[me_chip], local_sem)
        mine.start()
        sends = []
        for kk, (cx, cy) in enumerate(chips):
            cp = pltpu.make_async_remote_copy(src_ref=s_ref.at[2 * cx + cy], dst_ref=o_ref.at[me_chip],
                                              send_sem=send_sems.at[kk], recv_sem=recv_sems.at[kk],
                                              device_id=(cx, cy, c), device_id_type=MESH)
            cp.start()
            sends.append(cp)
        for kk, (cx, cy) in enumerate(chips):
            got = o_ref.at[2 * cx + cy]
            pltpu.make_async_remote_copy(src_ref=got, dst_ref=got, send_sem=send_sems.at[kk],
                                         recv_sem=recv_sems.at[kk], device_id=(x, y, c),
                                         device_id_type=MESH).wait_recv()
        for cp in sends:
            cp.wait_send()
        mine.wait()

    return pl.pallas_call(
        body, name="scatter_chips", in_specs=[ANY], out_specs=ANY,
        out_shape=jax.ShapeDtypeStruct(sb.shape, sb.dtype),
        scratch_shapes=[pltpu.SemaphoreType.DMA((3,)), pltpu.SemaphoreType.DMA((3,)), pltpu.SemaphoreType.DMA],
    )(sb)


def _sum_parts(parts, tr=208):
    half = parts.shape[1]

    def body(p_ref, o_ref):
        total = p_ref[0].astype(F32)
        for p in range(1, N_CHIPS):
            total = total + p_ref[p].astype(F32)
        o_ref[...] = total

    return pl.pallas_call(
        body, name="sum_parts", grid=(half // tr,),
        in_specs=[pl.BlockSpec((N_CHIPS, tr, PACK_COLS), lambda i: (0, i, 0))],
        out_specs=pl.BlockSpec((tr, PACK_COLS), lambda i: (i, 0)),
        out_shape=jax.ShapeDtypeStruct((half, PACK_COLS), F32),
        compiler_params=_params(("parallel",)),
    )(parts)


def _join_halves(red):
    half = red.shape[0]

    def body(r_ref, o_ref, send_sem, recv_sem, local_sem):
        x, y, c = _place()
        here = o_ref.at[pl.ds(pl.multiple_of(c * half, 8), half), :]
        there = o_ref.at[pl.ds(pl.multiple_of((1 - c) * half, 8), half), :]
        mine = pltpu.make_async_copy(r_ref, here, local_sem)
        mine.start()
        cp = pltpu.make_async_remote_copy(src_ref=r_ref, dst_ref=here, send_sem=send_sem, recv_sem=recv_sem,
                                          device_id=(x, y, 1 - c), device_id_type=MESH)
        cp.start()
        pltpu.make_async_remote_copy(src_ref=r_ref, dst_ref=there, send_sem=send_sem, recv_sem=recv_sem,
                                     device_id=(x, y, c), device_id_type=MESH).wait_recv()
        cp.wait_send()
        mine.wait()

    return pl.pallas_call(
        body, name="join_halves", in_specs=[ANY], out_specs=ANY,
        out_shape=jax.ShapeDtypeStruct((2 * half, PACK_COLS), red.dtype),
        scratch_shapes=[pltpu.SemaphoreType.DMA, pltpu.SemaphoreType.DMA, pltpu.SemaphoreType.DMA],
    )(red)


def _adamw_math(w, g, m, v):
    m = ADAM_B1 * m + (1.0 - ADAM_B1) * g
    v = ADAM_B2 * v + (1.0 - ADAM_B2) * jnp.square(g)
    m_hat = m / (1.0 - ADAM_B1 ** ADAM_STEP)
    v_hat = v / (1.0 - ADAM_B2 ** ADAM_STEP)
    delta = -ADAM_LR * (m_hat / (jnp.sqrt(v_hat) + ADAM_EPS) + ADAM_WD * w)
    return delta, m, v


def _adamw(tag, w, g, m, v, tr=256):
    rows, cols = w.shape
    tr = min(tr, rows)

    def body(w_ref, g_ref, m_ref, v_ref, d_ref, nm_ref, nv_ref):
        d_ref[...], nm_ref[...], nv_ref[...] = _adamw_math(w_ref[...], g_ref[...], m_ref[...], v_ref[...])

    spec = pl.BlockSpec((tr, cols), lambda i: (i, 0))
    return pl.pallas_call(
        body, name="adamw_" + tag, grid=(rows // tr,), in_specs=[spec] * 4, out_specs=[spec] * 3,
        out_shape=[jax.ShapeDtypeStruct(w.shape, F32)] * 3,
        compiler_params=_params(("parallel",)),
    )(w, g, m, v)


def _pad_rows(a, rows):
    return jnp.concatenate([a, jnp.zeros((rows - a.shape[0], a.shape[1]), a.dtype)], axis=0)


def _pack_shards(get):
    parts = [get(l, name).reshape(-1, PACK_COLS) for l in range(DEPTH) for name, _, _ in MATS]
    return _pad_rows(jnp.concatenate(parts, axis=0), PACK_ROWS)


def _unpack_shards(rows2d):
    out = {name: [] for name, _, _ in MATS}
    off = 0
    for l in range(DEPTH):
        for name, shape, axis in MATS:
            ss = _shard_shape(shape, axis)
            n = ss[0] * ss[1] // PACK_COLS
            out[name].append(rows2d[off:off + n].reshape(ss))
            off += n
    return {name: jnp.stack(v) for name, v in out.items()}


def _w_in_to_padded(w):
    d0 = SSD_DIM + CONV_DIM
    return jnp.concatenate([w[:, :d0], w[:, d0 + SSD_HEADS:], w[:, d0:d0 + SSD_HEADS],
                            jnp.zeros((w.shape[0], DT_PAD - SSD_HEADS), w.dtype)], axis=1)


def _w_in_from_padded(w):
    d0 = SSD_DIM + CONV_DIM
    return jnp.concatenate([w[:, :d0], w[:, DT_OFF:DT_OFF + SSD_HEADS], w[:, d0:DT_OFF]], axis=1)


def _small_layout():
    return (("norm_mix_g", 0, 0, D_MODEL), ("norm_xa_g", 1, 0, D_MODEL), ("norm_mem_g", 2, 0, D_MODEL),
            ("norm_ff_g", 3, 0, D_MODEL), ("conv_b", 4, 0, CONV_DIM), ("ssd_norm_g", 5, 0, SSD_DIM),
            ("sb_norm_g", 5, SSD_DIM, SB_DIM), ("dt_bias", 6, 0, SSD_HEADS), ("a_log", 6, LANES, SSD_HEADS),
            ("d_skip", 6, 2 * LANES, SSD_HEADS))


def _pack_small(vals, conv_w_full, extra=None):
    rows = []
    for l in range(DEPTH):
        lay = _small_layout()
        for r in range(_SM_PER_LAYER):
            cells = [(c0, vals[name][l].reshape(-1)) for name, rr, c0, _ in lay if rr == r]
            line, pos = [], 0
            for c0, val in cells:
                if c0 > pos:
                    line.append(jnp.zeros((c0 - pos,), F32))
                line.append(val)
                pos = c0 + val.shape[0]
            line.append(jnp.zeros((PACK_COLS - pos,), F32))
            rows.append(jnp.concatenate(line))
    rows.append(vals["final_g"].reshape(-1))
    rows.append(conv_w_full.reshape(-1, PACK_COLS))
    used = _SM_CONVW + DEPTH * CONV_K * CONV_DIM // PACK_COLS
    last = jnp.zeros((SMALL_ROWS - used, PACK_COLS), F32)
    if extra is not None:
        last = last.at[SMALL_ROWS - used - 1, 0].set(extra)
    return jnp.concatenate([r.reshape(-1, PACK_COLS) for r in rows] + [last], axis=0)


def _unpack_small(buf):
    out = {}
    for name, rr, c0, width in _small_layout():
        out[name] = jnp.stack([buf[l * _SM_PER_LAYER + rr, c0:c0 + width] for l in range(DEPTH)])
    out["final_g"] = buf[_SM_FINAL]
    n = DEPTH * CONV_K * CONV_DIM // PACK_COLS
    out["conv_w"] = buf[_SM_CONVW:_SM_CONVW + n].reshape(DEPTH, CONV_K, CONV_DIM)
    return out


CONV_BLOCK_ROWS = 8


def _conv_w_block(cw):
    flat = cw.reshape(-1)
    pad = jnp.zeros((CONV_BLOCK_ROWS * PACK_COLS - flat.shape[0],), F32)
    return jnp.concatenate([flat, pad]).reshape(CONV_BLOCK_ROWS, PACK_COLS)


def _conv_w_from_slots(slots):
    cols = CONV_DIM // N_CHIPS
    n = DEPTH * CONV_K * cols
    shards = [slots[2 * j].reshape(-1)[:n].reshape(DEPTH, CONV_K, cols) for j in range(N_CHIPS)]
    return jnp.concatenate(shards, axis=2)


def _conv_w_padded(cw):
    return jnp.concatenate([cw, jnp.zeros(cw.shape[:2] + (CONV_DIM - cw.shape[2],), F32)], axis=2)


SMALL_NAMES = ("norm_mix_g", "conv_b", "dt_bias", "a_log", "d_skip", "ssd_norm_g", "sb_norm_g", "norm_xa_g",
               "norm_mem_g", "norm_ff_g", "final_g")
WEIGHT_ORDER = ("norm_mix_g", "w_in", "conv_w", "conv_b", "dt_bias", "a_log", "d_skip", "ssd_norm_g", "sb_norm_g",
                "w_out", "norm_xa_g", "norm_mem_g", "w_xq", "w_xk", "w_xv", "w_xo", "norm_ff_g", "w_ff1", "w_ff2",
                "final_g")


def kernel(x, mem, norm_mix_g, w_in, conv_w, conv_b, dt_bias, a_log, d_skip, ssd_norm_g, sb_norm_g, w_out, norm_xa_g, norm_mem_g, w_xq, w_xk, w_xv, w_xo, norm_ff_g, w_ff1, w_ff2, final_g, loss_target, m_norm_mix_g, m_w_in, m_conv_w, m_conv_b, m_dt_bias, m_a_log, m_d_skip, m_ssd_norm_g, m_sb_norm_g, m_w_out, m_norm_xa_g, m_norm_mem_g, m_w_xq, m_w_xk, m_w_xv, m_w_xo, m_norm_ff_g, m_w_ff1, m_w_ff2, m_final_g, v_norm_mix_g, v_w_in, v_conv_w, v_conv_b, v_dt_bias, v_a_log, v_d_skip, v_ssd_norm_g, v_sb_norm_g, v_w_out, v_norm_xa_g, v_norm_mem_g, v_w_xq, v_w_xk, v_w_xv, v_w_xo, v_norm_ff_g, v_w_ff1, v_w_ff2, v_final_g):
    w = dict(norm_mix_g=norm_mix_g, w_in=w_in, conv_w=conv_w, conv_b=conv_b, dt_bias=dt_bias, a_log=a_log,
             d_skip=d_skip, ssd_norm_g=ssd_norm_g, sb_norm_g=sb_norm_g, w_out=w_out, norm_xa_g=norm_xa_g,
             norm_mem_g=norm_mem_g, w_xq=w_xq, w_xk=w_xk, w_xv=w_xv, w_xo=w_xo, norm_ff_g=norm_ff_g, w_ff1=w_ff1,
             w_ff2=w_ff2, final_g=final_g)
    mom = dict(norm_mix_g=m_norm_mix_g, w_in=m_w_in, conv_w=m_conv_w, conv_b=m_conv_b, dt_bias=m_dt_bias,
               a_log=m_a_log, d_skip=m_d_skip, ssd_norm_g=m_ssd_norm_g, sb_norm_g=m_sb_norm_g, w_out=m_w_out,
               norm_xa_g=m_norm_xa_g, norm_mem_g=m_norm_mem_g, w_xq=m_w_xq, w_xk=m_w_xk, w_xv=m_w_xv, w_xo=m_w_xo,
               norm_ff_g=m_norm_ff_g, w_ff1=m_w_ff1, w_ff2=m_w_ff2, final_g=m_final_g)
    var = dict(norm_mix_g=v_norm_mix_g, w_in=v_w_in, conv_w=v_conv_w, conv_b=v_conv_b, dt_bias=v_dt_bias,
               a_log=v_a_log, d_skip=v_d_skip, ssd_norm_g=v_ssd_norm_g, sb_norm_g=v_sb_norm_g, w_out=v_w_out,
               norm_xa_g=v_norm_xa_g, norm_mem_g=v_norm_mem_g, w_xq=v_w_xq, w_xk=v_w_xk, w_xv=v_w_xv, w_xo=v_w_xo,
               norm_ff_g=v_norm_ff_g, w_ff1=v_w_ff1, w_ff2=v_w_ff2, final_g=v_final_g)
    chip = 2 * lax.axis_index("x") + lax.axis_index("y")
    conv_cols = CONV_DIM // N_CHIPS

    gathered = _gather_weights(_pack_shards(lambda l, name: w[name][l]).astype(BF16))
    per_chip = [_unpack_shards(gathered[j]) for j in range(N_CHIPS)]
    mats = []
    for l in range(DEPTH):
        full = {name: jnp.concatenate([per_chip[j][name][l] for j in range(N_CHIPS)], axis=axis)
                for name, _, axis in MATS}
        full["w_in"] = _w_in_to_padded(full["w_in"])
        mats.append(full)
    conv_slots, _ = _gather_small("_conv", _conv_w_block(conv_w))
    conv_w_full = _conv_w_from_slots(conv_slots)

    sw = {name: w[name] for name in SMALL_NAMES}
    sw["conv_w"] = conv_w_full
    loss, grad_x, gm, gs = _local_step(x[0], mem[0], loss_target[0], sw, mats)

    small_vals = {name: jnp.concatenate(gs[name], axis=0) for name in gs if name not in ("conv_w", "final_g")}
    small_vals["final_g"] = gs["final_g"]
    small_buf = _pack_small(small_vals, jnp.stack(gs["conv_w"]), extra=loss[0, 0])
    _, small_sum = _gather_small("_grads", small_buf)
    g_small = _unpack_small(small_sum)
    loss_out = small_sum[_SM_LOSS, 0]

    def grad_part(j):
        def get(l, name):
            g = gm[l][name]
            if name == "w_in":
                g = _w_in_from_padded(g)
            axis = dict((n, a) for n, _, a in MATS)[name]
            size = g.shape[axis] // N_CHIPS
            return lax.slice_in_dim(g, j * size, (j + 1) * size, axis=axis)
        return _pack_shards(get)

    g_all = jnp.stack([grad_part(j) for j in range(N_CHIPS)])
    from_sibling = _swap_halves(g_all)
    to_chips = _add_halves(g_all, from_sibling)
    parts = _scatter_chips(to_chips)
    reduced = _join_halves(_sum_parts(parts))
    g_mats = _unpack_shards(reduced)

    grads, deltas, new_m, new_v = {}, {}, {}, {}
    for name, _, _ in MATS:
        g = g_mats[name]
        cols = g.shape[-1]
        d, nm, nv = _adamw(name, w[name].reshape(-1, cols), g.reshape(-1, cols), mom[name].reshape(-1, cols),
                           var[name].reshape(-1, cols))
        grads[name] = g
        deltas[name], new_m[name], new_v[name] = (t.reshape(g.shape) for t in (d, nm, nv))
    g_conv_w = lax.dynamic_slice_in_dim(g_small["conv_w"], chip * conv_cols, conv_cols, axis=2)
    w_small = _pack_small({n: w[n] for n in SMALL_NAMES}, _conv_w_padded(conv_w))
    m_small = _pack_small({n: mom[n] for n in SMALL_NAMES}, _conv_w_padded(mom["conv_w"]))
    v_small = _pack_small({n: var[n] for n in SMALL_NAMES}, _conv_w_padded(var["conv_w"]))
    g_small_local = _pack_small({n: g_small[n] for n in SMALL_NAMES}, _conv_w_padded(g_conv_w))
    d_s, m_s, v_s = (_unpack_small(t) for t in _adamw("small", w_small, g_small_local, m_small, v_small))
    for name in SMALL_NAMES:
        grads[name] = g_small[name]
        deltas[name], new_m[name], new_v[name] = d_s[name], m_s[name], v_s[name]
    grads["conv_w"] = g_conv_w
    deltas["conv_w"], new_m["conv_w"], new_v["conv_w"] = (t["conv_w"][:, :, :conv_cols] for t in (d_s, m_s, v_s))

    return (loss_out, grad_x[None], *[grads[n] for n in WEIGHT_ORDER], *[deltas[n] for n in WEIGHT_ORDER],
            *[new_m[n] for n in WEIGHT_ORDER], *[new_v[n] for n in WEIGHT_ORDER])
```

```python
import functools
import math

import jax
import jax.numpy as jnp
from jax import lax
from jax.experimental import pallas as pl
from jax.experimental.pallas import tpu as pltpu

F32 = jnp.float32
BF16 = jnp.bfloat16
MESH = pl.DeviceIdType.MESH

D_MODEL = 1024
DEPTH = 2
SSD_DIM = 512
SSD_HEAD_DIM = 64
SSD_HEADS = 8
SSD_GROUPS = 2
SSD_STATE = 64
CONV_K = 4
CHUNK = 128
SB_DIM = 512
SB_HEAD_DIM = 64
XA_HEADS = 4
XA_HEAD_DIM = 128
XA_DIM = 512
D_FF = 4096
EPS = 1e-5
GN = SSD_GROUPS * SSD_STATE
CONV_DIM = SSD_DIM + 2 * GN
IN_DIM = SSD_DIM + CONV_DIM + SSD_HEADS + 3 * SB_DIM
LANES = 128
DT_PAD = LANES
IN_PAD = SSD_DIM + CONV_DIM + 3 * SB_DIM + DT_PAD
Q_OFF = SSD_DIM + CONV_DIM
DT_OFF = Q_OFF + 3 * SB_DIM
HALO = 8

ADAM_LR = 0.001
ADAM_B1 = 0.9
ADAM_B2 = 0.999
ADAM_EPS = 1e-08
ADAM_WD = 0.01
ADAM_STEP = 10

N_CHIPS = 4
N_DEV = 8
PACK_COLS = 1024
VMEM_LIMIT = 56 * 1024 * 1024

MATS = (
    ("w_in", (D_MODEL, IN_DIM), 1),
    ("w_out", (D_MODEL, D_MODEL), 0),
    ("w_xq", (D_MODEL, XA_DIM), 0),
    ("w_xk", (D_MODEL, XA_DIM), 0),
    ("w_xv", (D_MODEL, XA_DIM), 0),
    ("w_xo", (XA_DIM, D_MODEL), 1),
    ("w_ff1", (D_MODEL, D_FF), 1),
    ("w_ff2", (D_FF, D_MODEL), 0),
)


def _shard_shape(shape, axis):
    s = list(shape)
    s[axis] //= N_CHIPS
    return tuple(s)


def _pack_rows_total():
    n = 0
    for _, shape, axis in MATS:
        ss = _shard_shape(shape, axis)
        n += ss[0] * ss[1] // PACK_COLS
    return n * DEPTH


PACK_USED = _pack_rows_total()
PACK_ROWS = -(-PACK_USED // 32) * 32
PACK_HALF = PACK_ROWS // 2

SMALL_ROWS = 24
_SM_PER_LAYER = 7
_SM_FINAL = 14
_SM_CONVW = 15
_SM_LOSS = 23


_NN = ((1,), (0,))
_NT = ((1,), (1,))
_TN = ((0,), (0,))


def _dg(a, b, dims):
    return lax.dot_general(a.astype(BF16), b.astype(BF16), (dims, ((), ())), preferred_element_type=F32)


@jax.custom_vjp
def mm_nn(a, b):
    return _dg(a, b, _NN)


@jax.custom_vjp
def mm_nt(a, b):
    return _dg(a, b, _NT)


@jax.custom_vjp
def mm_tn(a, b):
    return _dg(a, b, _TN)


def _nn_fwd(a, b):
    return _dg(a, b, _NN), (a, b)


def _nn_bwd(res, g):
    a, b = res
    return mm_nt(g, b).astype(a.dtype), mm_tn(a, g).astype(b.dtype)


def _nt_fwd(a, b):
    return _dg(a, b, _NT), (a, b)


def _nt_bwd(res, g):
    a, b = res
    return mm_nn(g, b).astype(a.dtype), mm_tn(g, a).astype(b.dtype)


def _tn_fwd(a, b):
    return _dg(a, b, _TN), (a, b)


def _tn_bwd(res, g):
    a, b = res
    return mm_nt(b, g).astype(a.dtype), mm_nn(a, g).astype(b.dtype)


mm_nn.defvjp(_nn_fwd, _nn_bwd)
mm_nt.defvjp(_nt_fwd, _nt_bwd)
mm_tn.defvjp(_tn_fwd, _tn_bwd)


def _rms(x, g):
    return x * lax.rsqrt(jnp.mean(x * x, axis=-1, keepdims=True) + EPS) * g


def _params(sem=None, vmem=VMEM_LIMIT):
    return pltpu.CompilerParams(dimension_semantics=sem, vmem_limit_bytes=vmem)


def _rowcall(name, fn, rows, fulls, row_out, acc_out=(), tm=256):
    s = rows[0].shape[0]
    tm = min(tm, s)
    nt = s // tm
    n_r, n_f, n_ro, n_ao = len(rows), len(fulls), len(row_out), len(acc_out)

    def body(*refs):
        ins = [r[...] for r in refs[: n_r + n_f]]
        outs = fn(*ins)
        o_refs = refs[n_r + n_f:]
        for o_ref, val in zip(o_refs[:n_ro], outs[:n_ro]):
            o_ref[...] = val.astype(o_ref.dtype)
        if n_ao:
            first = pl.program_id(0) == 0

            @pl.when(first)
            def _():
                for o_ref, val in zip(o_refs[n_ro:], outs[n_ro:]):
                    o_ref[...] = val.astype(o_ref.dtype)

            @pl.when(jnp.logical_not(first))
            def _():
                for o_ref, val in zip(o_refs[n_ro:], outs[n_ro:]):
                    o_ref[...] += val.astype(o_ref.dtype)

    in_specs = [pl.BlockSpec((tm, a.shape[1]), lambda i: (i, 0)) for a in rows]
    in_specs += [pl.BlockSpec(a.shape, lambda i: (0, 0), pipeline_mode=pl.Buffered(1)) for a in fulls]
    out_specs = [pl.BlockSpec((tm, c), lambda i: (i, 0)) for c, _ in row_out]
    out_specs += [pl.BlockSpec(shape, lambda i: (0, 0)) for shape, _ in acc_out]
    out_shape = [jax.ShapeDtypeStruct((s, c), dt) for c, dt in row_out]
    out_shape += [jax.ShapeDtypeStruct(shape, dt) for shape, dt in acc_out]
    return pl.pallas_call(
        body, name=name, grid=(nt,), in_specs=in_specs, out_specs=out_specs, out_shape=out_shape,
        compiler_params=_params(("arbitrary",)),
    )(*rows, *fulls)


def _mm_tn_call(name, a, b, tm, tn, tk):
    s, m = a.shape
    n = b.shape[1]
    tk = min(tk, s)

    def body(a_ref, b_ref, o_ref):
        d = _dg(a_ref[...], b_ref[...], _TN)
        first = pl.program_id(2) == 0

        @pl.when(first)
        def _():
            o_ref[...] = d

        @pl.when(jnp.logical_not(first))
        def _():
            o_ref[...] += d

    return pl.pallas_call(
        body, name=name, grid=(m // tm, n // tn, s // tk),
        in_specs=[pl.BlockSpec((tk, tm), lambda i, j, k: (k, i)), pl.BlockSpec((tk, tn), lambda i, j, k: (k, j))],
        out_specs=pl.BlockSpec((tm, tn), lambda i, j, k: (i, j)),
        out_shape=jax.ShapeDtypeStruct((m, n), F32),
        compiler_params=_params(("parallel", "parallel", "arbitrary")),
    )(a, b)


def _proj_tile(h, g, w):
    p = mm_nn(_rms(h, g), w)
    return (p[:, :SSD_DIM], p[:, SSD_DIM:Q_OFF], p[:, Q_OFF:Q_OFF + SB_DIM],
            p[:, Q_OFF + SB_DIM:Q_OFF + 2 * SB_DIM], p[:, Q_OFF + 2 * SB_DIM:DT_OFF], p[:, DT_OFF:])


def _proj_fwd(tag, h, g, w):
    return _rowcall(
        "proj_fwd" + tag, _proj_tile, [h], [g, w],
        [(SSD_DIM, F32), (CONV_DIM, F32), (SB_DIM, BF16), (SB_DIM, BF16), (SB_DIM, BF16), (DT_PAD, F32)], tm=512)


def _proj_bwd(tag, h, dh_out, dz, dxbc, dq, dk, dv, ddt, g, w):
    def fn(h, dh_out, dz, dxbc, dq, dk, dv, ddt, g, w):
        dp = jnp.concatenate([dz.astype(BF16), dxbc.astype(BF16), dq.astype(BF16), dk.astype(BF16),
                              dv.astype(BF16), ddt.astype(BF16)], axis=1)
        hn, vjp = jax.vjp(_rms, h, g)
        dh, dg = vjp(mm_nt(dp, w))
        return dh_out + dh, hn, dp, dg

    return _rowcall(
        "proj_bwd" + tag, fn, [h, dh_out, dz, dxbc, dq, dk, dv, ddt], [g, w],
        [(D_MODEL, F32), (D_MODEL, BF16), (IN_PAD, BF16)], [((1, D_MODEL), F32)], tm=256)


def _shift_down(x, tail, j):
    if j == 0:
        return x
    r = pltpu.roll(x, j, 0)
    rt = pltpu.roll(tail, j, 0)
    row = lax.broadcasted_iota(jnp.int32, (HALO, x.shape[1]), 0)
    first = jnp.where(row < j, rt, r[:HALO])
    if x.shape[0] == HALO:
        return first
    return jnp.concatenate([first, r[HALO:]], axis=0)


def _shift_up(x, head, j):
    if j == 0:
        return x
    n = x.shape[0]
    r = pltpu.roll(x, n - j, 0)
    rh = pltpu.roll(head, HALO - j, 0)
    row = lax.broadcasted_iota(jnp.int32, (HALO, x.shape[1]), 0)
    return jnp.concatenate([r[:n - HALO], jnp.where(row >= HALO - j, rh, r[n - HALO:])], axis=0)


def _conv_pre(x, tail, w, b):
    acc = b + w[CONV_K - 1:CONV_K] * x
    for j in range(1, CONV_K):
        acc = acc + w[CONV_K - 1 - j:CONV_K - j] * _shift_down(x, tail, j)
    return acc


def _dsilu(p):
    s = jax.nn.sigmoid(p)
    return s * (1.0 + p * (1.0 - s))


def _conv_fwd(tag, xbc, w, b, tc=512):
    s, c = xbc.shape
    tc = min(tc, s)
    per = tc // HALO

    def body(x_ref, prev_ref, w_ref, b_ref, o_ref):
        tail = jnp.where(pl.program_id(0) > 0, prev_ref[...], 0.0)
        o_ref[...] = jax.nn.silu(_conv_pre(x_ref[...], tail, w_ref[...], b_ref[...]))

    return pl.pallas_call(
        body, name="conv_fwd" + tag, grid=(s // tc,),
        in_specs=[pl.BlockSpec((tc, c), lambda i: (i, 0)),
                  pl.BlockSpec((HALO, c), lambda i: (jnp.maximum(i * per - 1, 0), 0)),
                  pl.BlockSpec((CONV_K, c), lambda i: (0, 0)), pl.BlockSpec((1, c), lambda i: (0, 0))],
        out_specs=pl.BlockSpec((tc, c), lambda i: (i, 0)),
        out_shape=jax.ShapeDtypeStruct((s, c), F32),
        compiler_params=_params(("arbitrary",)),
    )(xbc, xbc, w, b)


def _conv_bwd(tag, xbc, dact, w, b, tc=512):
    s, c = xbc.shape
    tc = min(tc, s)
    per = tc // HALO
    nt = s // tc
    last_blk = s // HALO - 1

    def body(x_ref, prev_ref, next_ref, d_ref, dnext_ref, w_ref, b_ref, dx_ref, dw_ref, db_ref):
        i = pl.program_id(0)
        x = x_ref[...]
        wv = w_ref[...]
        tail = jnp.where(i > 0, prev_ref[...], 0.0)
        dpre = d_ref[...] * _dsilu(_conv_pre(x, tail, wv, b_ref[...]))
        pre_n = _conv_pre(next_ref[...], x[tc - HALO:], wv, b_ref[...])
        dpre_n = jnp.where(i < nt - 1, dnext_ref[...] * _dsilu(pre_n), 0.0)
        dx = wv[CONV_K - 1:CONV_K] * dpre
        for j in range(1, CONV_K):
            dx = dx + wv[CONV_K - 1 - j:CONV_K - j] * _shift_up(dpre, dpre_n, j)
        dx_ref[...] = dx
        dws = [jnp.sum(dpre * _shift_down(x, tail, CONV_K - 1 - k), axis=0, keepdims=True) for k in range(CONV_K)]
        dwv = jnp.concatenate(dws, axis=0)
        dbv = jnp.sum(dpre, axis=0, keepdims=True)

        @pl.when(i == 0)
        def _():
            dw_ref[...] = dwv
            db_ref[...] = dbv

        @pl.when(i > 0)
        def _():
            dw_ref[...] += dwv
            db_ref[...] += dbv

    tile = pl.BlockSpec((tc, c), lambda i: (i, 0))
    prev = pl.BlockSpec((HALO, c), lambda i: (jnp.maximum(i * per - 1, 0), 0))
    nxt = pl.BlockSpec((HALO, c), lambda i: (jnp.minimum((i + 1) * per, last_blk), 0))
    return pl.pallas_call(
        body, name="conv_bwd" + tag, grid=(nt,),
        in_specs=[tile, prev, nxt, tile, nxt, pl.BlockSpec((CONV_K, c), lambda i: (0, 0)),
                  pl.BlockSpec((1, c), lambda i: (0, 0))],
        out_specs=[tile, pl.BlockSpec((CONV_K, c), lambda i: (0, 0)), pl.BlockSpec((1, c), lambda i: (0, 0))],
        out_shape=[jax.ShapeDtypeStruct((s, c), F32), jax.ShapeDtypeStruct((CONV_K, c), F32),
                   jax.ShapeDtypeStruct((1, c), F32)],
        compiler_params=_params(("arbitrary",)),
    )(xbc, xbc, xbc, dact, dact, w, b)


def _ssd_chunk(xs, bm, cm, dtr, z, dt_bias, a_log, d_skip, g, s_prev):
    n = CHUNK
    row = lax.broadcasted_iota(jnp.int32, (n, n), 0)
    col = lax.broadcasted_iota(jnp.int32, (n, n), 1)
    causal = row >= col
    dt = jax.nn.softplus(dtr + dt_bias)
    a_c = dt * (-jnp.exp(a_log))
    hi = lax.Precision.HIGHEST
    a_cum = jnp.dot(causal.astype(F32), a_c, precision=hi, preferred_element_type=F32)
    a_cum_t = lax.dot_general(a_c, (row <= col).astype(F32), (_TN, ((), ())), precision=hi,
                              preferred_element_type=F32)
    p, st = SSD_HEAD_DIM, SSD_STATE
    cb = [mm_nt(cm[:, k * st:(k + 1) * st], bm[:, k * st:(k + 1) * st]) for k in range(SSD_GROUPS)]
    ys, s_new = [], []
    for h in range(SSD_HEADS):
        k = h // (SSD_HEADS // SSD_GROUPS)
        acol = a_cum[:, h:h + 1]
        decay = jnp.exp(jnp.where(causal, acol - a_cum_t[h:h + 1, :], -jnp.inf))
        xh = xs[:, h * p:(h + 1) * p]
        xdt = xh * dt[:, h:h + 1]
        a_last = a_cum[n - 1:n, h:h + 1]
        sp = s_prev[h * p:(h + 1) * p, :]
        bg = bm[:, k * st:(k + 1) * st]
        cg = cm[:, k * st:(k + 1) * st]
        s_new.append(sp * jnp.exp(a_last) + mm_tn(xdt * jnp.exp(a_last - acol), bg))
        y = mm_nn(cb[k] * decay, xdt) + mm_nt(cg, sp) * jnp.exp(acol) + d_skip[:, h:h + 1] * xh
        ys.append(y)
    y = jnp.concatenate(ys, axis=1) * jax.nn.silu(z)
    return _rms(y, g), jnp.concatenate(s_new, axis=0)


def _split_xbc(t):
    return t[:, :SSD_DIM], t[:, SSD_DIM:SSD_DIM + GN], t[:, SSD_DIM + GN:]


def _ssd_fwd(tag, xact, dtr, z, dt_bias, a_log, d_skip, g):
    s = xact.shape[0]
    nc = s // CHUNK
    srows = SSD_HEADS * SSD_HEAD_DIM

    def body(x_ref, dt_ref, z_ref, b_ref, al_ref, ds_ref, g_ref, y_ref, st_ref, state):
        @pl.when(pl.program_id(0) == 0)
        def _():
            state[...] = jnp.zeros_like(state)

        sp = state[...]
        st_ref[0] = sp
        xs, bm, cm = _split_xbc(x_ref[...])
        y, sn = _ssd_chunk(xs, bm, cm, dt_ref[...][:, :SSD_HEADS], z_ref[...], b_ref[...], al_ref[...],
                           ds_ref[...], g_ref[...], sp)
        y_ref[...] = y
        state[...] = sn

    small = pl.BlockSpec((1, SSD_HEADS), lambda i: (0, 0))
    return pl.pallas_call(
        body, name="ssd_fwd" + tag, grid=(nc,),
        in_specs=[pl.BlockSpec((CHUNK, CONV_DIM), lambda i: (i, 0)), pl.BlockSpec((CHUNK, DT_PAD), lambda i: (i, 0)),
                  pl.BlockSpec((CHUNK, SSD_DIM), lambda i: (i, 0)), small, small, small,
                  pl.BlockSpec((1, SSD_DIM), lambda i: (0, 0))],
        out_specs=[pl.BlockSpec((CHUNK, SSD_DIM), lambda i: (i, 0)),
                   pl.BlockSpec((1, srows, SSD_STATE), lambda i: (i, 0, 0))],
        out_shape=[jax.ShapeDtypeStruct((s, SSD_DIM), F32), jax.ShapeDtypeStruct((nc, srows, SSD_STATE), F32)],
        scratch_shapes=[pltpu.VMEM((srows, SSD_STATE), F32)],
        compiler_params=_params(("arbitrary",)),
    )(xact, dtr, z, dt_bias, a_log, d_skip, g)


def _ssd_bwd(tag, xact, dtr, z, states, dy, dt_bias, a_log, d_skip, g):
    s = xact.shape[0]
    nc = s // CHUNK
    srows = SSD_HEADS * SSD_HEAD_DIM

    def body(x_ref, dt_ref, z_ref, sp_ref, dy_ref, b_ref, al_ref, ds_ref, g_ref,
             dx_ref, ddt_ref, dz_ref, db_ref, dal_ref, dds_ref, dg_ref, dstate):
        first = pl.program_id(0) == 0

        @pl.when(first)
        def _():
            dstate[...] = jnp.zeros_like(dstate)

        xs, bm, cm = _split_xbc(x_ref[...])
        _, vjp = jax.vjp(_ssd_chunk, xs, bm, cm, dt_ref[...][:, :SSD_HEADS], z_ref[...], b_ref[...], al_ref[...],
                         ds_ref[...], g_ref[...], sp_ref[0])
        dxs, dbm, dcm, ddt, dz, db, dal, dds, dg, dsp = vjp((dy_ref[...], dstate[...]))
        dx_ref[...] = jnp.concatenate([dxs, dbm, dcm], axis=1)
        ddt_ref[...] = jnp.concatenate([ddt, jnp.zeros((CHUNK, DT_PAD - SSD_HEADS), F32)], axis=1)
        dz_ref[...] = dz
        dstate[...] = dsp

        @pl.when(first)
        def _():
            db_ref[...] = db
            dal_ref[...] = dal
            dds_ref[...] = dds
            dg_ref[...] = dg

        @pl.when(jnp.logical_not(first))
        def _():
            db_ref[...] += db
            dal_ref[...] += dal
            dds_ref[...] += dds
            dg_ref[...] += dg

    def rev(c):
        return lambda i: (nc - 1 - i, 0)

    small = pl.BlockSpec((1, SSD_HEADS), lambda i: (0, 0))
    gspec = pl.BlockSpec((1, SSD_DIM), lambda i: (0, 0))
    return pl.pallas_call(
        body, name="ssd_bwd" + tag, grid=(nc,),
        in_specs=[pl.BlockSpec((CHUNK, CONV_DIM), rev(0)), pl.BlockSpec((CHUNK, DT_PAD), rev(0)),
                  pl.BlockSpec((CHUNK, SSD_DIM), rev(0)),
                  pl.BlockSpec((1, srows, SSD_STATE), lambda i: (nc - 1 - i, 0, 0)),
                  pl.BlockSpec((CHUNK, SSD_DIM), rev(0)), small, small, small, gspec],
        out_specs=[pl.BlockSpec((CHUNK, CONV_DIM), rev(0)), pl.BlockSpec((CHUNK, DT_PAD), rev(0)),
                   pl.BlockSpec((CHUNK, SSD_DIM), rev(0)), small, small, small, gspec],
        out_shape=[jax.ShapeDtypeStruct((s, CONV_DIM), F32), jax.ShapeDtypeStruct((s, DT_PAD), F32),
                   jax.ShapeDtypeStruct((s, SSD_DIM), F32), jax.ShapeDtypeStruct((1, SSD_HEADS), F32),
                   jax.ShapeDtypeStruct((1, SSD_HEADS), F32), jax.ShapeDtypeStruct((1, SSD_HEADS), F32),
                   jax.ShapeDtypeStruct((1, SSD_DIM), F32)],
        scratch_shapes=[pltpu.VMEM((srows, SSD_STATE), F32)],
        compiler_params=_params(("arbitrary",)),
    )(xact, dtr, z, states, dy, dt_bias, a_log, d_skip, g)


TQ = 128
TK = 128
SB_SCALE = 1.0 / math.sqrt(SB_HEAD_DIM)


def _split2(x):
    hi = x.astype(BF16)
    return hi, (x - hi.astype(F32)).astype(BF16)


TK_WIDE = 512


def _sb_logits(qh, kb, t0, s0, masked):
    z = _dg(qh, kb, _NT) * SB_SCALE
    lb = jnp.minimum(z, 0.0) - jnp.log(1.0 + jnp.exp(-jnp.abs(z)))
    ls = lb - z
    if not masked:
        return lb, ls, None
    t_pos = t0 + lax.broadcasted_iota(jnp.int32, z.shape, 0)
    s_pos = s0 + lax.broadcasted_iota(jnp.int32, z.shape, 1)
    mask = s_pos < t_pos
    return lb, jnp.where(mask, ls, 0.0), mask


def _running_sums(x, start, u, reverse):
    nsub = x.shape[1] // TK
    run = start
    parts = [None] * nsub
    for c in (reversed(range(nsub)) if reverse else range(nsub)):
        xc = x[:, c * TK:(c + 1) * TK]
        parts[c] = run + _lane_sums(xc, u)
        run = run + jnp.sum(xc, axis=1, keepdims=True)
    return (parts[0] if nsub == 1 else jnp.concatenate(parts, axis=1)), run


def _lane_sums(x, u):
    hi, lo = _split2(x)
    return _dg(hi, u, _NN) + _dg(lo, u, _NN)


def _tri(cmp):
    j = lax.broadcasted_iota(jnp.int32, (TK, TK), 0)
    s = lax.broadcasted_iota(jnp.int32, (TK, TK), 1)
    return cmp(j, s).astype(BF16)


def _sb_fwd(tag, q, k, v):
    s = q.shape[0]
    npair = SB_DIM // LANES

    wide = min(TK_WIDE, s)
    per = wide // TQ

    def body(q_ref, k_ref, v_ref, o_ref, t_ref):
        tq = pl.program_id(1)
        diag = tq // per
        qp = q_ref[...]
        lane = lax.broadcasted_iota(jnp.int32, (1, LANES), 1)
        u_gt = _tri(lambda j, s: j > s)
        out = jnp.zeros((TQ, LANES), F32)
        tot = jnp.zeros((TQ, LANES), F32)
        for hh in range(LANES // SB_HEAD_DIM):
            hm = (lane // SB_HEAD_DIM) == hh
            qh = jnp.where(hm, qp, jnp.zeros_like(qp))

            def block(wb, r, acc, masked):
                off = pl.multiple_of(wb * wide, wide)
                kb = k_ref[pl.ds(off, wide), :]
                vb = v_ref[pl.ds(off, wide), :]
                lb, ls, mask = _sb_logits(qh, kb, tq * TQ, wb * wide, masked)
                later, r = _running_sums(ls, r, u_gt, reverse=True)
                w = jnp.exp(lb + later)
                if masked:
                    w = jnp.where(mask, w, 0.0)
                return r, acc + _dg(w, vb, _NN)

            r, acc = block(diag, jnp.zeros((TQ, 1), F32), jnp.zeros((TQ, LANES), F32), True)
            r, acc = lax.fori_loop(0, diag, lambda j, c: block(diag - 1 - j, c[0], c[1], False), (r, acc))
            out = out + jnp.where(hm, acc, 0.0)
            tot = tot + jnp.where(hm, r, 0.0)
        o_ref[...] = out
        t_ref[...] = tot

    tile = pl.BlockSpec((TQ, LANES), lambda p, t: (t, p))
    full = pl.BlockSpec((s, LANES), lambda p, t: (0, p))
    return pl.pallas_call(
        body, name="sb_fwd" + tag, grid=(npair, s // TQ),
        in_specs=[tile, full, full], out_specs=[tile, tile],
        out_shape=[jax.ShapeDtypeStruct((s, SB_DIM), F32)] * 2,
        compiler_params=_params(("parallel", "arbitrary")),
    )(q, k, v)


def _sb_bwd(tag, q, k, v, tot, do):
    s = q.shape[0]
    npair = SB_DIM // LANES

    wide = min(TK_WIDE, s)
    per = wide // TQ

    def body(q_ref, k_ref, v_ref, t_ref, do_ref, dq_ref, dk_ref, dv_ref):
        tq = pl.program_id(1)
        diag = tq // per

        @pl.when(tq == 0)
        def _():
            dk_ref[...] = jnp.zeros_like(dk_ref)
            dv_ref[...] = jnp.zeros_like(dv_ref)

        qp = q_ref[...]
        dop = do_ref[...]
        totp = t_ref[...]
        lane = lax.broadcasted_iota(jnp.int32, (1, LANES), 1)
        u_le = _tri(lambda j, s: j <= s)
        u_lt = _tri(lambda j, s: j < s)
        dq = jnp.zeros((TQ, LANES), F32)
        for hh in range(LANES // SB_HEAD_DIM):
            hm = (lane // SB_HEAD_DIM) == hh
            qh = jnp.where(hm, qp, jnp.zeros_like(qp))
            doh = jnp.where(hm, dop, 0.0).astype(BF16)
            total = jnp.sum(jnp.where(lane == hh * SB_HEAD_DIM, totp, 0.0), axis=1, keepdims=True)

            def block(wb, pre, gc, acc, masked):
                off = pl.multiple_of(wb * wide, wide)
                kb = k_ref[pl.ds(off, wide), :]
                vb = v_ref[pl.ds(off, wide), :]
                lb, ls, mask = _sb_logits(qh, kb, tq * TQ, wb * wide, masked)
                before, pre = _running_sums(ls, pre, u_le, reverse=False)
                w = jnp.exp(lb + (total - before))
                if masked:
                    w = jnp.where(mask, w, 0.0)
                g = w * _dg(doh, vb, _NT)
                g_left, gc = _running_sums(g, gc, u_lt, reverse=False)
                beta = jnp.exp(lb)
                dz = (g * (1.0 - beta) - beta * g_left) * SB_SCALE
                if masked:
                    dz = jnp.where(mask, dz, 0.0)
                dzb = dz.astype(BF16)
                dk_ref[pl.ds(off, wide), :] += _dg(dzb, qh, _TN)
                dv_ref[pl.ds(off, wide), :] += _dg(w, doh, _TN)
                return pre, gc, acc + _dg(dzb, kb, _NN)

            zero = jnp.zeros((TQ, 1), F32)
            carry = lax.fori_loop(0, diag, lambda j, c: block(j, c[0], c[1], c[2], False),
                                  (zero, zero, jnp.zeros((TQ, LANES), F32)))
            _, _, acc = block(diag, carry[0], carry[1], carry[2], True)
            dq = dq + jnp.where(hm, acc, 0.0)
        dq_ref[...] = dq

    tile = pl.BlockSpec((TQ, LANES), lambda p, t: (t, p))
    full = pl.BlockSpec((s, LANES), lambda p, t: (0, p))
    return pl.pallas_call(
        body, name="sb_bwd" + tag, grid=(npair, s // TQ),
        in_specs=[tile, full, full, tile, tile],
        out_specs=[tile, full, full],
        out_shape=[jax.ShapeDtypeStruct((s, SB_DIM), F32)] * 3,
        compiler_params=_params(("parallel", "arbitrary")),
    )(q, k, v, tot, do)


def _out_tile(y_ssd, o, sb_g, w_out):
    y_all = jnp.concatenate([y_ssd, _rms(o, sb_g)], axis=1)
    return mm_nn(y_all, w_out)


def _out_fwd(tag, h, y_ssd, o, sb_g, w_out):
    return _rowcall("out_fwd" + tag, lambda h, y, o, g, w: (h + _out_tile(y, o, g, w),),
                    [h, y_ssd, o], [sb_g, w_out], [(D_MODEL, F32)], tm=512)[0]


def _out_bwd(tag, y_ssd, o, dh, sb_g, w_out):
    def fn(y, o, dh, g, w):
        _, vjp = jax.vjp(_out_tile, y, o, g, w.astype(F32))
        return vjp(dh)

    return _rowcall("out_bwd" + tag, fn, [y_ssd, o, dh], [sb_g, w_out], [(SSD_DIM, F32), (SB_DIM, F32)],
                    [((1, SB_DIM), F32), ((D_MODEL, D_MODEL), F32)], tm=256)


def _mem_tile(mem, g, w_k, w_v):
    m = _rms(mem, g)
    return mm_nn(m, w_k), mm_nn(m, w_v)


def _mem_fwd(tag, mem, g, w_k, w_v):
    return _rowcall("mem_fwd" + tag, _mem_tile, [mem], [g, w_k, w_v], [(XA_DIM, F32), (XA_DIM, F32)], tm=256)


def _mem_bwd(tag, mem, dkx, dvx, g, w_k, w_v):
    def fn(mem, dkx, dvx, g, w_k, w_v):
        _, vjp = jax.vjp(lambda g, a, b: _mem_tile(mem, g, a, b), g, w_k.astype(F32), w_v.astype(F32))
        return vjp((dkx, dvx))

    return _rowcall("mem_bwd" + tag, fn, [mem, dkx, dvx], [g, w_k, w_v], [],
                    [((1, D_MODEL), F32), ((D_MODEL, XA_DIM), F32), ((D_MODEL, XA_DIM), F32)], tm=256)


def _xattn_tile(h, g, w_q, kx, vx, w_o):
    q = mm_nn(_rms(h, g), w_q)
    scale = 1.0 / math.sqrt(XA_HEAD_DIM)
    outs = []
    for i in range(XA_HEADS):
        sl = slice(i * XA_HEAD_DIM, (i + 1) * XA_HEAD_DIM)
        p = jax.nn.softmax(mm_nt(q[:, sl], kx[:, sl]) * scale, axis=-1)
        outs.append(mm_nn(p, vx[:, sl]))
    return mm_nn(jnp.concatenate(outs, axis=1), w_o)


def _xattn_fwd(tag, h, g, w_q, kx, vx, w_o):
    return _rowcall("xattn_fwd" + tag, lambda h, g, wq, kx, vx, wo: (h + _xattn_tile(h, g, wq, kx, vx, wo),),
                    [h], [g, w_q, kx, vx, w_o], [(D_MODEL, F32)], tm=512)[0]


def _xattn_bwd(tag, h, dh_out, g, w_q, kx, vx, w_o):
    def fn(h, dh_out, g, w_q, kx, vx, w_o):
        _, vjp = jax.vjp(_xattn_tile, h, g, w_q.astype(F32), kx, vx, w_o.astype(F32))
        dh, dg, dwq, dkx, dvx, dwo = vjp(dh_out)
        return dh_out + dh, dg, dwq, dkx, dvx, dwo

    mlen = kx.shape[0]
    return _rowcall("xattn_bwd" + tag, fn, [h, dh_out], [g, w_q, kx, vx, w_o], [(D_MODEL, F32)],
                    [((1, D_MODEL), F32), ((D_MODEL, XA_DIM), F32), ((mlen, XA_DIM), F32), ((mlen, XA_DIM), F32),
                     ((XA_DIM, D_MODEL), F32)], tm=256)


def _mlp_fwd(tag, h, g, w1, w2):
    def fn(h, g, w1, w2):
        u = jnp.square(jnp.maximum(mm_nn(_rms(h, g), w1), 0.0))
        return (h + mm_nn(u, w2),)

    return _rowcall("mlp_fwd" + tag, fn, [h], [g, w1, w2], [(D_MODEL, F32)], tm=256)[0]


def _mlp_bwd(tag, h, dh_out, g, w1, w2):
    def fn(h, dh_out, g, w1, w2):
        hn, vjp = jax.vjp(_rms, h, g)
        r = jnp.maximum(mm_nn(hn, w1), 0.0)
        dob = dh_out.astype(BF16)
        dp = mm_nt(dob, w2) * (2.0 * r)
        dh, dg = vjp(mm_nt(dp, w1))
        return dh_out + dh, hn, dp, r * r, dob, dg

    return _rowcall("mlp_bwd" + tag, fn, [h, dh_out], [g, w1, w2],
                    [(D_MODEL, F32), (D_MODEL, BF16), (D_FF, BF16), (D_FF, BF16), (D_MODEL, BF16)],
                    [((1, D_MODEL), F32)], tm=256)


def _head(h, g, target):
    def lossfn(h, g, t):
        err = jnp.square(_rms(h, g) - t)
        return 0.5 * jnp.sum(jnp.mean(err, axis=-1))

    def fn(h, t, g):
        loss, vjp = jax.vjp(lambda h, g: lossfn(h, g, t), h, g)
        dh, dg = vjp(jnp.ones((), F32))
        return dh, jnp.full((1, LANES), loss, F32), dg

    return _rowcall("head", fn, [h, target], [g], [(D_MODEL, F32)], [((1, LANES), F32), ((1, D_MODEL), F32)], tm=512)


def _row(v):
    return v.reshape(1, -1)


def _local_step(x, mem, target, sw, mats):
    h = x
    saved = []
    for l in range(DEPTH):
        tag = str(l)
        m = mats[l]
        z, xbc, q, k, v, dtr = _proj_fwd(tag, h, _row(sw["norm_mix_g"][l]), m["w_in"])
        xact = _conv_fwd(tag, xbc, sw["conv_w"][l], _row(sw["conv_b"][l]))
        y_ssd, states = _ssd_fwd(tag, xact, dtr, z, _row(sw["dt_bias"][l]), _row(sw["a_log"][l]),
                                 _row(sw["d_skip"][l]), _row(sw["ssd_norm_g"][l]))
        o, sb_tot = _sb_fwd(tag, q, k, v)
        h1 = _out_fwd(tag, h, y_ssd, o, _row(sw["sb_norm_g"][l]), m["w_out"])
        kx, vx = _mem_fwd(tag, mem, _row(sw["norm_mem_g"][l]), m["w_xk"], m["w_xv"])
        h2 = _xattn_fwd(tag, h1, _row(sw["norm_xa_g"][l]), m["w_xq"], kx, vx, m["w_xo"])
        h3 = _mlp_fwd(tag, h2, _row(sw["norm_ff_g"][l]), m["w_ff1"], m["w_ff2"])
        saved.append((h, z, xbc, q, k, v, dtr, xact, y_ssd, states, o, sb_tot, h1, kx, vx, h2))
        h = h3

    dh, loss, d_final = _head(h, _row(sw["final_g"]), target)
    gm = [dict() for _ in range(DEPTH)]
    gs = {name: [None] * DEPTH for name in ("norm_mix_g", "conv_w", "conv_b", "dt_bias", "a_log", "d_skip",
                                            "ssd_norm_g", "sb_norm_g", "norm_xa_g", "norm_mem_g", "norm_ff_g")}
    for l in reversed(range(DEPTH)):
        tag = str(l)
        m = mats[l]
        h0, z, xbc, q, k, v, dtr, xact, y_ssd, states, o, sb_tot, h1, kx, vx, h2 = saved[l]
        dh2, hn_b, dp_b, a_b, do_b, gs["norm_ff_g"][l] = _mlp_bwd(
            tag, h2, dh, _row(sw["norm_ff_g"][l]), m["w_ff1"], m["w_ff2"])
        gm[l]["w_ff1"] = _mm_tn_call("dw_ff1" + tag, hn_b, dp_b, 512, 1024, 2048)
        gm[l]["w_ff2"] = _mm_tn_call("dw_ff2" + tag, a_b, do_b, 1024, 1024, 2048)
        dh1, gs["norm_xa_g"][l], gm[l]["w_xq"], dkx, dvx, gm[l]["w_xo"] = _xattn_bwd(
            tag, h1, dh2, _row(sw["norm_xa_g"][l]), m["w_xq"], kx, vx, m["w_xo"])
        gs["norm_mem_g"][l], gm[l]["w_xk"], gm[l]["w_xv"] = _mem_bwd(
            tag, mem, dkx, dvx, _row(sw["norm_mem_g"][l]), m["w_xk"], m["w_xv"])
        dy_ssd, do, gs["sb_norm_g"][l], gm[l]["w_out"] = _out_bwd(
            tag, y_ssd, o, dh1, _row(sw["sb_norm_g"][l]), m["w_out"])
        dq, dk, dv = _sb_bwd(tag, q, k, v, sb_tot, do)
        dxact, ddtr, dz, gs["dt_bias"][l], gs["a_log"][l], gs["d_skip"][l], gs["ssd_norm_g"][l] = _ssd_bwd(
            tag, xact, dtr, z, states, dy_ssd, _row(sw["dt_bias"][l]), _row(sw["a_log"][l]),
            _row(sw["d_skip"][l]), _row(sw["ssd_norm_g"][l]))
        dxbc, gs["conv_w"][l], gs["conv_b"][l] = _conv_bwd(tag, xbc, dxact, sw["conv_w"][l], _row(sw["conv_b"][l]))
        dh, hn_b, dp_b, gs["norm_mix_g"][l] = _proj_bwd(
            tag, h0, dh1, dz, dxbc, dq, dk, dv, ddtr, _row(sw["norm_mix_g"][l]), m["w_in"])
        gm[l]["w_in"] = _mm_tn_call("dw_in" + tag, hn_b, dp_b, 512, IN_PAD, 1024)
    gs["final_g"] = d_final
    return loss, dh, gm, gs


ANY = pl.BlockSpec(memory_space=pl.ANY)
VMEM_SPEC = pl.BlockSpec(memory_space=pltpu.VMEM)


def _place():
    return lax.axis_index("x"), lax.axis_index("y"), lax.axis_index("c")


def _other_chips(x, y):
    return [(1 - x, y), (x, 1 - y), (1 - x, 1 - y)]


def _gather_weights(pack):
    rows = pack.shape[0]
    half = rows // 2

    def body(p_ref, o_ref, send_sems, recv_sems, local_sem):
        x, y, c = _place()
        me_chip = 2 * x + y
        sibling = (x, y, 1 - c)
        chips = _other_chips(x, y)

        def part(chip, hh):
            return o_ref.at[chip, pl.ds(pl.multiple_of(hh * half, 16), half), :]

        def copy(kk, src, dst, to):
            return pltpu.make_async_remote_copy(src_ref=src, dst_ref=dst, send_sem=send_sems.at[kk],
                                                recv_sem=recv_sems.at[kk], device_id=to, device_id_type=MESH)

        mine = pltpu.make_async_copy(p_ref, o_ref.at[me_chip], local_sem)
        mine.start()
        my_half = p_ref.at[pl.ds(pl.multiple_of(c * half, 16), half), :]
        first = [copy(kk, my_half, part(me_chip, c), (cx, cy, c)) for kk, (cx, cy) in enumerate(chips)]
        for cp in first:
            cp.start()
        passed = []
        for kk, (cx, cy) in enumerate(chips):
            got = part(2 * cx + cy, c)
            copy(kk, got, got, (x, y, c)).wait_recv()
            fwd = copy(3 + kk, got, got, sibling)
            fwd.start()
            passed.append(fwd)
        for kk, (cx, cy) in enumerate(chips):
            got = part(2 * cx + cy, 1 - c)
            copy(3 + kk, got, got, (x, y, c)).wait_recv()
        for cp in first + passed:
            cp.wait_send()
        mine.wait()

    return pl.pallas_call(
        body, name="gather_weights", in_specs=[ANY], out_specs=ANY,
        out_shape=jax.ShapeDtypeStruct((N_CHIPS, rows, PACK_COLS), pack.dtype),
        scratch_shapes=[pltpu.SemaphoreType.DMA((6,)), pltpu.SemaphoreType.DMA((6,)), pltpu.SemaphoreType.DMA],
    )(pack)


def _gather_small(tag, buf):
    shape = buf.shape

    def body(b_ref, o_ref, sum_ref, send_sems, recv_sems, local_sem):
        x, y, c = _place()
        me = 4 * x + 2 * y + c
        mine = pltpu.make_async_copy(b_ref, o_ref.at[me], local_sem)
        mine.start()
        flips = [(dx, dy, dc) for dx in (0, 1) for dy in (0, 1) for dc in (0, 1) if (dx, dy, dc) != (0, 0, 0)]
        sends = []

        def peer(dx, dy, dc):
            return (1 - x if dx else x, 1 - y if dy else y, 1 - c if dc else c)

        for kk, flip in enumerate(flips):
            cp = pltpu.make_async_remote_copy(src_ref=b_ref, dst_ref=o_ref.at[me], send_sem=send_sems.at[kk],
                                              recv_sem=recv_sems.at[kk], device_id=peer(*flip), device_id_type=MESH)
            cp.start()
            sends.append(cp)
        for kk, flip in enumerate(flips):
            px, py, pc = peer(*flip)
            frm = 4 * px + 2 * py + pc
            pltpu.make_async_remote_copy(src_ref=b_ref, dst_ref=o_ref.at[frm], send_sem=send_sems.at[kk],
                                         recv_sem=recv_sems.at[kk], device_id=(x, y, c),
                                         device_id_type=MESH).wait_recv()
        for cp in sends:
            cp.wait_send()
        mine.wait()
        total = o_ref[0]
        for d in range(1, N_DEV):
            total = total + o_ref[d]
        sum_ref[...] = total

    return pl.pallas_call(
        body, name="gather_small" + tag, in_specs=[VMEM_SPEC], out_specs=[VMEM_SPEC, VMEM_SPEC],
        out_shape=[jax.ShapeDtypeStruct((N_DEV,) + shape, buf.dtype), jax.ShapeDtypeStruct(shape, buf.dtype)],
        scratch_shapes=[pltpu.SemaphoreType.DMA((N_DEV - 1,)), pltpu.SemaphoreType.DMA((N_DEV - 1,)),
                        pltpu.SemaphoreType.DMA],
    )(buf)


def _swap_halves(g):
    half = g.shape[1] // 2

    def body(g_ref, r_ref, send_sem, recv_sem):
        x, y, c = _place()
        src = g_ref.at[:, pl.ds(pl.multiple_of((1 - c) * half, 8), half), :]
        cp = pltpu.make_async_remote_copy(src_ref=src, dst_ref=r_ref, send_sem=send_sem, recv_sem=recv_sem,
                                          device_id=(x, y, 1 - c), device_id_type=MESH)
        cp.start()
        cp.wait()

    return pl.pallas_call(
        body, name="swap_halves", in_specs=[ANY], out_specs=ANY,
        out_shape=jax.ShapeDtypeStruct((N_CHIPS, half, PACK_COLS), g.dtype),
        scratch_shapes=[pltpu.SemaphoreType.DMA, pltpu.SemaphoreType.DMA],
    )(g)


def _add_halves(g, r, tr=208):
    half = r.shape[1]
    core = lax.axis_index("c").astype(jnp.int32).reshape(1)
    per = half // tr

    def body(c_ref, g_ref, r_ref, o_ref):
        o_ref[...] = (g_ref[...] + r_ref[...]).astype(o_ref.dtype)

    return pl.pallas_call(
        body, name="add_halves",
        grid_spec=pltpu.PrefetchScalarGridSpec(
            num_scalar_prefetch=1, grid=(N_CHIPS, per),
            in_specs=[pl.BlockSpec((1, tr, PACK_COLS), lambda p, i, c_ref: (p, c_ref[0] * per + i, 0)),
                      pl.BlockSpec((1, tr, PACK_COLS), lambda p, i, c_ref: (p, i, 0))],
            out_specs=pl.BlockSpec((1, tr, PACK_COLS), lambda p, i, c_ref: (p, i, 0))),
        out_shape=jax.ShapeDtypeStruct(r.shape, BF16),
        compiler_params=_params(("arbitrary", "arbitrary")),
    )(core, g, r)


def _scatter_chips(sb):
    def body(s_ref, o_ref, send_sems, recv_sems, local_sem):
        x, y, c = _place()
        me_chip = 2 * x + y
        chips = _other_chips(x, y)
        mine = pltpu.make_async_copy(s_ref.at[me_chip], o_ref.at[me_chip], local_sem)
        mine.start()
        sends = []
        for kk, (cx, cy) in enumerate(chips):
            cp = pltpu.make_async_remote_copy(src_ref=s_ref.at[2 * cx + cy], dst_ref=o_ref.at[me_chip],
                                              send_sem=send_sems.at[kk], recv_sem=recv_sems.at[kk],
                                              device_id=(cx, cy, c), device_id_type=MESH)
            cp.start()
            sends.append(cp)
        for kk, (cx, cy) in enumerate(chips):
            got = o_ref.at[2 * cx + cy]
            pltpu.make_async_remote_copy(src_ref=got, dst_ref=got, send_sem=send_sems.at[kk],
                                         recv_sem=recv_sems.at[kk], device_id=(x, y, c),
                                         device_id_type=MESH).wait_recv()
        for cp in sends:
            cp.wait_send()
        mine.wait()

    return pl.pallas_call(
        body, name="scatter_chips", in_specs=[ANY], out_specs=ANY,
        out_shape=jax.ShapeDtypeStruct(sb.shape, sb.dtype),
        scratch_shapes=[pltpu.SemaphoreType.DMA((3,)), pltpu.SemaphoreType.DMA((3,)), pltpu.SemaphoreType.DMA],
    )(sb)


def _sum_parts(parts, tr=208):
    half = parts.shape[1]

    def body(p_ref, o_ref):
        total = p_ref[0].astype(F32)
        for p in range(1, N_CHIPS):
            total = total + p_ref[p].astype(F32)
        o_ref[...] = total

    return pl.pallas_call(
        body, name="sum_parts", grid=(half // tr,),
        in_specs=[pl.BlockSpec((N_CHIPS, tr, PACK_COLS), lambda i: (0, i, 0))],
        out_specs=pl.BlockSpec((tr, PACK_COLS), lambda i: (i, 0)),
        out_shape=jax.ShapeDtypeStruct((half, PACK_COLS), F32),
        compiler_params=_params(("parallel",)),
    )(parts)


def _join_halves(red):
    half = red.shape[0]

    def body(r_ref, o_ref, send_sem, recv_sem, local_sem):
        x, y, c = _place()
        here = o_ref.at[pl.ds(pl.multiple_of(c * half, 8), half), :]
        there = o_ref.at[pl.ds(pl.multiple_of((1 - c) * half, 8), half), :]
        mine = pltpu.make_async_copy(r_ref, here, local_sem)
        mine.start()
        cp = pltpu.make_async_remote_copy(src_ref=r_ref, dst_ref=here, send_sem=send_sem, recv_sem=recv_sem,
                                          device_id=(x, y, 1 - c), device_id_type=MESH)
        cp.start()
        pltpu.make_async_remote_copy(src_ref=r_ref, dst_ref=there, send_sem=send_sem, recv_sem=recv_sem,
                                     device_id=(x, y, c), device_id_type=MESH).wait_recv()
        cp.wait_send()
        mine.wait()

    return pl.pallas_call(
        body, name="join_halves", in_specs=[ANY], out_specs=ANY,
        out_shape=jax.ShapeDtypeStruct((2 * half, PACK_COLS), red.dtype),
        scratch_shapes=[pltpu.SemaphoreType.DMA, pltpu.SemaphoreType.DMA, pltpu.SemaphoreType.DMA],
    )(red)


def _adamw_math(w, g, m, v):
    m = ADAM_B1 * m + (1.0 - ADAM_B1) * g
    v = ADAM_B2 * v + (1.0 - ADAM_B2) * jnp.square(g)
    m_hat = m / (1.0 - ADAM_B1 ** ADAM_STEP)
    v_hat = v / (1.0 - ADAM_B2 ** ADAM_STEP)
    delta = -ADAM_LR * (m_hat / (jnp.sqrt(v_hat) + ADAM_EPS) + ADAM_WD * w)
    return delta, m, v


def _adamw(tag, w, g, m, v, tr=256):
    rows, cols = w.shape
    tr = min(tr, rows)

    def body(w_ref, g_ref, m_ref, v_ref, d_ref, nm_ref, nv_ref):
        d_ref[...], nm_ref[...], nv_ref[...] = _adamw_math(w_ref[...], g_ref[...], m_ref[...], v_ref[...])

    spec = pl.BlockSpec((tr, cols), lambda i: (i, 0))
    return pl.pallas_call(
        body, name="adamw_" + tag, grid=(rows // tr,), in_specs=[spec] * 4, out_specs=[spec] * 3,
        out_shape=[jax.ShapeDtypeStruct(w.shape, F32)] * 3,
        compiler_params=_params(("parallel",)),
    )(w, g, m, v)


def _pad_rows(a, rows):
    return jnp.concatenate([a, jnp.zeros((rows - a.shape[0], a.shape[1]), a.dtype)], axis=0)


def _pack_shards(get):
    parts = [get(l, name).reshape(-1, PACK_COLS) for l in range(DEPTH) for name, _, _ in MATS]
    return _pad_rows(jnp.concatenate(parts, axis=0), PACK_ROWS)


def _unpack_shards(rows2d):
    out = {name: [] for name, _, _ in MATS}
    off = 0
    for l in range(DEPTH):
        for name, shape, axis in MATS:
            ss = _shard_shape(shape, axis)
            n = ss[0] * ss[1] // PACK_COLS
            out[name].append(rows2d[off:off + n].reshape(ss))
            off += n
    return {name: jnp.stack(v) for name, v in out.items()}


def _w_in_to_padded(w):
    d0 = SSD_DIM + CONV_DIM
    return jnp.concatenate([w[:, :d0], w[:, d0 + SSD_HEADS:], w[:, d0:d0 + SSD_HEADS],
                            jnp.zeros((w.shape[0], DT_PAD - SSD_HEADS), w.dtype)], axis=1)


def _w_in_from_padded(w):
    d0 = SSD_DIM + CONV_DIM
    return jnp.concatenate([w[:, :d0], w[:, DT_OFF:DT_OFF + SSD_HEADS], w[:, d0:DT_OFF]], axis=1)


def _small_layout():
    return (("norm_mix_g", 0, 0, D_MODEL), ("norm_xa_g", 1, 0, D_MODEL), ("norm_mem_g", 2, 0, D_MODEL),
            ("norm_ff_g", 3, 0, D_MODEL), ("conv_b", 4, 0, CONV_DIM), ("ssd_norm_g", 5, 0, SSD_DIM),
            ("sb_norm_g", 5, SSD_DIM, SB_DIM), ("dt_bias", 6, 0, SSD_HEADS), ("a_log", 6, LANES, SSD_HEADS),
            ("d_skip", 6, 2 * LANES, SSD_HEADS))


def _pack_small(vals, conv_w_full, extra=None):
    rows = []
    for l in range(DEPTH):
        lay = _small_layout()
        for r in range(_SM_PER_LAYER):
            cells = [(c0, vals[name][l].reshape(-1)) for name, rr, c0, _ in lay if rr == r]
            line, pos = [], 0
            for c0, val in cells:
                if c0 > pos:
                    line.append(jnp.zeros((c0 - pos,), F32))
                line.append(val)
                pos = c0 + val.shape[0]
            line.append(jnp.zeros((PACK_COLS - pos,), F32))
            rows.append(jnp.concatenate(line))
    rows.append(vals["final_g"].reshape(-1))
    rows.append(conv_w_full.reshape(-1, PACK_COLS))
    used = _SM_CONVW + DEPTH * CONV_K * CONV_DIM // PACK_COLS
    last = jnp.zeros((SMALL_ROWS - used, PACK_COLS), F32)
    if extra is not None:
        last = last.at[SMALL_ROWS - used - 1, 0].set(extra)
    return jnp.concatenate([r.reshape(-1, PACK_COLS) for r in rows] + [last], axis=0)


def _unpack_small(buf):
    out = {}
    for name, rr, c0, width in _small_layout():
        out[name] = jnp.stack([buf[l * _SM_PER_LAYER + rr, c0:c0 + width] for l in range(DEPTH)])
    out["final_g"] = buf[_SM_FINAL]
    n = DEPTH * CONV_K * CONV_DIM // PACK_COLS
    out["conv_w"] = buf[_SM_CONVW:_SM_CONVW + n].reshape(DEPTH, CONV_K, CONV_DIM)
    return out


CONV_BLOCK_ROWS = 8


def _conv_w_block(cw):
    flat = cw.reshape(-1)
    pad = jnp.zeros((CONV_BLOCK_ROWS * PACK_COLS - flat.shape[0],), F32)
    return jnp.concatenate([flat, pad]).reshape(CONV_BLOCK_ROWS, PACK_COLS)


def _conv_w_from_slots(slots):
    cols = CONV_DIM // N_CHIPS
    n = DEPTH * CONV_K * cols
    shards = [slots[2 * j].reshape(-1)[:n].reshape(DEPTH, CONV_K, cols) for j in range(N_CHIPS)]
    return jnp.concatenate(shards, axis=2)


def _conv_w_padded(cw):
    return jnp.concatenate([cw, jnp.zeros(cw.shape[:2] + (CONV_DIM - cw.shape[2],), F32)], axis=2)


SMALL_NAMES = ("norm_mix_g", "conv_b", "dt_bias", "a_log", "d_skip", "ssd_norm_g", "sb_norm_g", "norm_xa_g",
               "norm_mem_g", "norm_ff_g", "final_g")
WEIGHT_ORDER = ("norm_mix_g", "w_in", "conv_w", "conv_b", "dt_bias", "a_log", "d_skip", "ssd_norm_g", "sb_norm_g",
                "w_out", "norm_xa_g", "norm_mem_g", "w_xq", "w_xk", "w_xv", "w_xo", "norm_ff_g", "w_ff1", "w_ff2",
                "final_g")


def kernel(x, mem, norm_mix_g, w_in, conv_w, conv_b, dt_bias, a_log, d_skip, ssd_norm_g, sb_norm_g, w_out, norm_xa_g, norm_mem_g, w_xq, w_xk, w_xv, w_xo, norm_ff_g, w_ff1, w_ff2, final_g, loss_target, m_norm_mix_g, m_w_in, m_conv_w, m_conv_b, m_dt_bias, m_a_log, m_d_skip, m_ssd_norm_g, m_sb_norm_g, m_w_out, m_norm_xa_g, m_norm_mem_g, m_w_xq, m_w_xk, m_w_xv, m_w_xo, m_norm_ff_g, m_w_ff1, m_w_ff2, m_final_g, v_norm_mix_g, v_w_in, v_conv_w, v_conv_b, v_dt_bias, v_a_log, v_d_skip, v_ssd_norm_g, v_sb_norm_g, v_w_out, v_norm_xa_g, v_norm_mem_g, v_w_xq, v_w_xk, v_w_xv, v_w_xo, v_norm_ff_g, v_w_ff1, v_w_ff2, v_final_g):
    w = dict(norm_mix_g=norm_mix_g, w_in=w_in, conv_w=conv_w, conv_b=conv_b, dt_bias=dt_bias, a_log=a_log,
             d_skip=d_skip, ssd_norm_g=ssd_norm_g, sb_norm_g=sb_norm_g, w_out=w_out, norm_xa_g=norm_xa_g,
             norm_mem_g=norm_mem_g, w_xq=w_xq, w_xk=w_xk, w_xv=w_xv, w_xo=w_xo, norm_ff_g=norm_ff_g, w_ff1=w_ff1,
             w_ff2=w_ff2, final_g=final_g)
    mom = dict(norm_mix_g=m_norm_mix_g, w_in=m_w_in, conv_w=m_conv_w, conv_b=m_conv_b, dt_bias=m_dt_bias,
               a_log=m_a_log, d_skip=m_d_skip, ssd_norm_g=m_ssd_norm_g, sb_norm_g=m_sb_norm_g, w_out=m_w_out,
               norm_xa_g=m_norm_xa_g, norm_mem_g=m_norm_mem_g, w_xq=m_w_xq, w_xk=m_w_xk, w_xv=m_w_xv, w_xo=m_w_xo,
               norm_ff_g=m_norm_ff_g, w_ff1=m_w_ff1, w_ff2=m_w_ff2, final_g=m_final_g)
    var = dict(norm_mix_g=v_norm_mix_g, w_in=v_w_in, conv_w=v_conv_w, conv_b=v_conv_b, dt_bias=v_dt_bias,
               a_log=v_a_log, d_skip=v_d_skip, ssd_norm_g=v_ssd_norm_g, sb_norm_g=v_sb_norm_g, w_out=v_w_out,
               norm_xa_g=v_norm_xa_g, norm_mem_g=v_norm_mem_g, w_xq=v_w_xq, w_xk=v_w_xk, w_xv=v_w_xv, w_xo=v_w_xo,
               norm_ff_g=v_norm_ff_g, w_ff1=v_w_ff1, w_ff2=v_w_ff2, final_g=v_final_g)
    chip = 2 * lax.axis_index("x") + lax.axis_index("y")
    conv_cols = CONV_DIM // N_CHIPS

    gathered = _gather_weights(_pack_shards(lambda l, name: w[name][l]).astype(BF16))
    per_chip = [_unpack_shards(gathered[j]) for j in range(N_CHIPS)]
    mats = []
    for l in range(DEPTH):
        full = {name: jnp.concatenate([per_chip[j][name][l] for j in range(N_CHIPS)], axis=axis)
                for name, _, axis in MATS}
        full["w_in"] = _w_in_to_padded(full["w_in"])
        mats.append(full)
    conv_slots, _ = _gather_small("_conv", _conv_w_block(conv_w))
    conv_w_full = _conv_w_from_slots(conv_slots)

    sw = {name: w[name] for name in SMALL_NAMES}
    sw["conv_w"] = conv_w_full
    loss, grad_x, gm, gs = _local_step(x[0], mem[0], loss_target[0], sw, mats)

    small_vals = {name: jnp.concatenate(gs[name], axis=0) for name in gs if name not in ("conv_w", "final_g")}
    small_vals["final_g"] = gs["final_g"]
    small_buf = _pack_small(small_vals, jnp.stack(gs["conv_w"]), extra=loss[0, 0])
    _, small_sum = _gather_small("_grads", small_buf)
    g_small = _unpack_small(small_sum)
    loss_out = small_sum[_SM_LOSS, 0]

    def grad_part(j):
        def get(l, name):
            g = gm[l][name]
            if name == "w_in":
                g = _w_in_from_padded(g)
            axis = dict((n, a) for n, _, a in MATS)[name]
            size = g.shape[axis] // N_CHIPS
            return lax.slice_in_dim(g, j * size, (j + 1) * size, axis=axis)
        return _pack_shards(get)

    g_all = jnp.stack([grad_part(j) for j in range(N_CHIPS)])
    from_sibling = _swap_halves(g_all)
    to_chips = _add_halves(g_all, from_sibling)
    parts = _scatter_chips(to_chips)
    reduced = _join_halves(_sum_parts(parts))
    g_mats = _unpack_shards(reduced)

    grads, deltas, new_m, new_v = {}, {}, {}, {}
    for name, _, _ in MATS:
        g = g_mats[name]
        cols = g.shape[-1]
        d, nm, nv = _adamw(name, w[name].reshape(-1, cols), g.reshape(-1, cols), mom[name].reshape(-1, cols),
                           var[name].reshape(-1, cols))
        grads[name] = g
        deltas[name], new_m[name], new_v[name] = (t.reshape(g.shape) for t in (d, nm, nv))
    g_conv_w = lax.dynamic_slice_in_dim(g_small["conv_w"], chip * conv_cols, conv_cols, axis=2)
    w_small = _pack_small({n: w[n] for n in SMALL_NAMES}, _conv_w_padded(conv_w))
    m_small = _pack_small({n: mom[n] for n in SMALL_NAMES}, _conv_w_padded(mom["conv_w"]))
    v_small = _pack_small({n: var[n] for n in SMALL_NAMES}, _conv_w_padded(var["conv_w"]))
    g_small_local = _pack_small({n: g_small[n] for n in SMALL_NAMES}, _conv_w_padded(g_conv_w))
    d_s, m_s, v_s = (_unpack_small(t) for t in _adamw("small", w_small, g_small_local, m_small, v_small))
    for name in SMALL_NAMES:
        grads[name] = g_small[name]
        deltas[name], new_m[name], new_v[name] = d_s[name], m_s[name], v_s[name]
    grads["conv_w"] = g_conv_w
    deltas["conv_w"], new_m["conv_w"], new_v["conv_w"] = (t["conv_w"][:, :, :conv_cols] for t in (d_s, m_s, v_s))

    return (loss_out, grad_x[None], *[grads[n] for n in WEIGHT_ORDER], *[deltas[n] for n in WEIGHT_ORDER],
            *[new_m[n] for n in WEIGHT_ORDER], *[new_v[n] for n in WEIGHT_ORDER])
```

```python
import functools
import math

import jax
import jax.numpy as jnp
from jax import lax
from jax.experimental import pallas as pl
from jax.experimental.pallas import tpu as pltpu

F32 = jnp.float32
BF16 = jnp.bfloat16
MESH = pl.DeviceIdType.MESH

D_MODEL = 1024
DEPTH = 2
SSD_DIM = 512
SSD_HEAD_DIM = 64
SSD_HEADS = 8
SSD_GROUPS = 2
SSD_STATE = 64
CONV_K = 4
CHUNK = 128
SB_DIM = 512
SB_HEAD_DIM = 64
XA_HEADS = 4
XA_HEAD_DIM = 128
XA_DIM = 512
D_FF = 4096
EPS = 1e-5
GN = SSD_GROUPS * SSD_STATE
CONV_DIM = SSD_DIM + 2 * GN
IN_DIM = SSD_DIM + CONV_DIM + SSD_HEADS + 3 * SB_DIM
LANES = 128
DT_PAD = LANES
IN_PAD = SSD_DIM + CONV_DIM + 3 * SB_DIM + DT_PAD
Q_OFF = SSD_DIM + CONV_DIM
DT_OFF = Q_OFF + 3 * SB_DIM
HALO = 8

ADAM_LR = 0.001
ADAM_B1 = 0.9
ADAM_B2 = 0.999
ADAM_EPS = 1e-08
ADAM_WD = 0.01
ADAM_STEP = 10

N_CHIPS = 4
N_DEV = 8
PACK_COLS = 1024
VMEM_LIMIT = 56 * 1024 * 1024

MATS = (
    ("w_in", (D_MODEL, IN_DIM), 1),
    ("w_out", (D_MODEL, D_MODEL), 0),
    ("w_xq", (D_MODEL, XA_DIM), 0),
    ("w_xk", (D_MODEL, XA_DIM), 0),
    ("w_xv", (D_MODEL, XA_DIM), 0),
    ("w_xo", (XA_DIM, D_MODEL), 1),
    ("w_ff1", (D_MODEL, D_FF), 1),
    ("w_ff2", (D_FF, D_MODEL), 0),
)


def _shard_shape(shape, axis):
    s = list(shape)
    s[axis] //= N_CHIPS
    return tuple(s)


def _pack_rows_total():
    n = 0
    for _, shape, axis in MATS:
        ss = _shard_shape(shape, axis)
        n += ss[0] * ss[1] // PACK_COLS
    return n * DEPTH


PACK_USED = _pack_rows_total()
PACK_ROWS = -(-PACK_USED // 32) * 32
PACK_HALF = PACK_ROWS // 2

SMALL_ROWS = 24
_SM_PER_LAYER = 7
_SM_FINAL = 14
_SM_CONVW = 15
_SM_LOSS = 23


_NN = ((1,), (0,))
_NT = ((1,), (1,))
_TN = ((0,), (0,))


def _dg(a, b, dims):
    return lax.dot_general(a.astype(BF16), b.astype(BF16), (dims, ((), ())), preferred_element_type=F32)


@jax.custom_vjp
def mm_nn(a, b):
    return _dg(a, b, _NN)


@jax.custom_vjp
def mm_nt(a, b):
    return _dg(a, b, _NT)


@jax.custom_vjp
def mm_tn(a, b):
    return _dg(a, b, _TN)


def _nn_fwd(a, b):
    return _dg(a, b, _NN), (a, b)


def _nn_bwd(res, g):
    a, b = res
    return mm_nt(g, b).astype(a.dtype), mm_tn(a, g).astype(b.dtype)


def _nt_fwd(a, b):
    return _dg(a, b, _NT), (a, b)


def _nt_bwd(res, g):
    a, b = res
    return mm_nn(g, b).astype(a.dtype), mm_tn(g, a).astype(b.dtype)


def _tn_fwd(a, b):
    return _dg(a, b, _TN), (a, b)


def _tn_bwd(res, g):
    a, b = res
    return mm_nt(b, g).astype(a.dtype), mm_nn(a, g).astype(b.dtype)


mm_nn.defvjp(_nn_fwd, _nn_bwd)
mm_nt.defvjp(_nt_fwd, _nt_bwd)
mm_tn.defvjp(_tn_fwd, _tn_bwd)


def _rms(x, g):
    return x * lax.rsqrt(jnp.mean(x * x, axis=-1, keepdims=True) + EPS) * g


def _params(sem=None, vmem=VMEM_LIMIT):
    return pltpu.CompilerParams(dimension_semantics=sem, vmem_limit_bytes=vmem)


def _rowcall(name, fn, rows, fulls, row_out, acc_out=(), tm=256):
    s = rows[0].shape[0]
    tm = min(tm, s)
    nt = s // tm
    n_r, n_f, n_ro, n_ao = len(rows), len(fulls), len(row_out), len(acc_out)

    def body(*refs):
        ins = [r[...] for r in refs[: n_r + n_f]]
        outs = fn(*ins)
        o_refs = refs[n_r + n_f:]
        for o_ref, val in zip(o_refs[:n_ro], outs[:n_ro]):
            o_ref[...] = val.astype(o_ref.dtype)
        if n_ao:
            first = pl.program_id(0) == 0

            @pl.when(first)
            def _():
                for o_ref, val in zip(o_refs[n_ro:], outs[n_ro:]):
                    o_ref[...] = val.astype(o_ref.dtype)

            @pl.when(jnp.logical_not(first))
            def _():
                for o_ref, val in zip(o_refs[n_ro:], outs[n_ro:]):
                    o_ref[...] += val.astype(o_ref.dtype)

    in_specs = [pl.BlockSpec((tm, a.shape[1]), lambda i: (i, 0)) for a in rows]
    in_specs += [pl.BlockSpec(a.shape, lambda i: (0, 0), pipeline_mode=pl.Buffered(1)) for a in fulls]
    out_specs = [pl.BlockSpec((tm, c), lambda i: (i, 0)) for c, _ in row_out]
    out_specs += [pl.BlockSpec(shape, lambda i: (0, 0)) for shape, _ in acc_out]
    out_shape = [jax.ShapeDtypeStruct((s, c), dt) for c, dt in row_out]
    out_shape += [jax.ShapeDtypeStruct(shape, dt) for shape, dt in acc_out]
    return pl.pallas_call(
        body, name=name, grid=(nt,), in_specs=in_specs, out_specs=out_specs, out_shape=out_shape,
        compiler_params=_params(("arbitrary",)),
    )(*rows, *fulls)


def _mm_tn_call(name, a, b, tm, tn, tk):
    s, m = a.shape
    n = b.shape[1]
    tk = min(tk, s)

    def body(a_ref, b_ref, o_ref):
        d = _dg(a_ref[...], b_ref[...], _TN)
        first = pl.program_id(2) == 0

        @pl.when(first)
        def _():
            o_ref[...] = d

        @pl.when(jnp.logical_not(first))
        def _():
            o_ref[...] += d

    return pl.pallas_call(
        body, name=name, grid=(m // tm, n // tn, s // tk),
        in_specs=[pl.BlockSpec((tk, tm), lambda i, j, k: (k, i)), pl.BlockSpec((tk, tn), lambda i, j, k: (k, j))],
        out_specs=pl.BlockSpec((tm, tn), lambda i, j, k: (i, j)),
        out_shape=jax.ShapeDtypeStruct((m, n), F32),
        compiler_params=_params(("parallel", "parallel", "arbitrary")),
    )(a, b)


def _proj_tile(h, g, w):
    p = mm_nn(_rms(h, g), w)
    return (p[:, :SSD_DIM], p[:, SSD_DIM:Q_OFF], p[:, Q_OFF:Q_OFF + SB_DIM],
            p[:, Q_OFF + SB_DIM:Q_OFF + 2 * SB_DIM], p[:, Q_OFF + 2 * SB_DIM:DT_OFF], p[:, DT_OFF:])


def _proj_fwd(tag, h, g, w):
    return _rowcall(
        "proj_fwd" + tag, _proj_tile, [h], [g, w],
        [(SSD_DIM, F32), (CONV_DIM, F32), (SB_DIM, BF16), (SB_DIM, BF16), (SB_DIM, BF16), (DT_PAD, F32)], tm=512)


def _proj_bwd(tag, h, dh_out, dz, dxbc, dq, dk, dv, ddt, g, w):
    def fn(h, dh_out, dz, dxbc, dq, dk, dv, ddt, g, w):
        dp = jnp.concatenate([dz.astype(BF16), dxbc.astype(BF16), dq.astype(BF16), dk.astype(BF16),
                              dv.astype(BF16), ddt.astype(BF16)], axis=1)
        hn, vjp = jax.vjp(_rms, h, g)
        dh, dg = vjp(mm_nt(dp, w))
        return dh_out + dh, hn, dp, dg

    return _rowcall(
        "proj_bwd" + tag, fn, [h, dh_out, dz, dxbc, dq, dk, dv, ddt], [g, w],
        [(D_MODEL, F32), (D_MODEL, BF16), (IN_PAD, BF16)], [((1, D_MODEL), F32)], tm=256)


def _shift_down(x, tail, j):
    if j == 0:
        return x
    r = pltpu.roll(x, j, 0)
    rt = pltpu.roll(tail, j, 0)
    row = lax.broadcasted_iota(jnp.int32, (HALO, x.shape[1]), 0)
    first = jnp.where(row < j, rt, r[:HALO])
    if x.shape[0] == HALO:
        return first
    return jnp.concatenate([first, r[HALO:]], axis=0)


def _shift_up(x, head, j):
    if j == 0:
        return x
    n = x.shape[0]
    r = pltpu.roll(x, n - j, 0)
    rh = pltpu.roll(head, HALO - j, 0)
    row = lax.broadcasted_iota(jnp.int32, (HALO, x.shape[1]), 0)
    return jnp.concatenate([r[:n - HALO], jnp.where(row >= HALO - j, rh, r[n - HALO:])], axis=0)


def _conv_pre(x, tail, w, b):
    acc = b + w[CONV_K - 1:CONV_K] * x
    for j in range(1, CONV_K):
        acc = acc + w[CONV_K - 1 - j:CONV_K - j] * _shift_down(x, tail, j)
    return acc


def _dsilu(p):
    s = jax.nn.sigmoid(p)
    return s * (1.0 + p * (1.0 - s))


def _conv_fwd(tag, xbc, w, b, tc=512):
    s, c = xbc.shape
    tc = min(tc, s)
    per = tc // HALO

    def body(x_ref, prev_ref, w_ref, b_ref, o_ref):
        tail = jnp.where(pl.program_id(0) > 0, prev_ref[...], 0.0)
        o_ref[...] = jax.nn.silu(_conv_pre(x_ref[...], tail, w_ref[...], b_ref[...]))

    return pl.pallas_call(
        body, name="conv_fwd" + tag, grid=(s // tc,),
        in_specs=[pl.BlockSpec((tc, c), lambda i: (i, 0)),
                  pl.BlockSpec((HALO, c), lambda i: (jnp.maximum(i * per - 1, 0), 0)),
                  pl.BlockSpec((CONV_K, c), lambda i: (0, 0)), pl.BlockSpec((1, c), lambda i: (0, 0))],
        out_specs=pl.BlockSpec((tc, c), lambda i: (i, 0)),
        out_shape=jax.ShapeDtypeStruct((s, c), F32),
        compiler_params=_params(("arbitrary",)),
    )(xbc, xbc, w, b)


def _conv_bwd(tag, xbc, dact, w, b, tc=512):
    s, c = xbc.shape
    tc = min(tc, s)
    per = tc // HALO
    nt = s // tc
    last_blk = s // HALO - 1

    def body(x_ref, prev_ref, next_ref, d_ref, dnext_ref, w_ref, b_ref, dx_ref, dw_ref, db_ref):
        i = pl.program_id(0)
        x = x_ref[...]
        wv = w_ref[...]
        tail = jnp.where(i > 0, prev_ref[...], 0.0)
        dpre = d_ref[...] * _dsilu(_conv_pre(x, tail, wv, b_ref[...]))
        pre_n = _conv_pre(next_ref[...], x[tc - HALO:], wv, b_ref[...])
        dpre_n = jnp.where(i < nt - 1, dnext_ref[...] * _dsilu(pre_n), 0.0)
        dx = wv[CONV_K - 1:CONV_K] * dpre
        for j in range(1, CONV_K):
            dx = dx + wv[CONV_K - 1 - j:CONV_K - j] * _shift_up(dpre, dpre_n, j)
        dx_ref[...] = dx
        dws = [jnp.sum(dpre * _shift_down(x, tail, CONV_K - 1 - k), axis=0, keepdims=True) for k in range(CONV_K)]
        dwv = jnp.concatenate(dws, axis=0)
        dbv = jnp.sum(dpre, axis=0, keepdims=True)

        @pl.when(i == 0)
        def _():
            dw_ref[...] = dwv
            db_ref[...] = dbv

        @pl.when(i > 0)
        def _():
            dw_ref[...] += dwv
            db_ref[...] += dbv

    tile = pl.BlockSpec((tc, c), lambda i: (i, 0))
    prev = pl.BlockSpec((HALO, c), lambda i: (jnp.maximum(i * per - 1, 0), 0))
    nxt = pl.BlockSpec((HALO, c), lambda i: (jnp.minimum((i + 1) * per, last_blk), 0))
    return pl.pallas_call(
        body, name="conv_bwd" + tag, grid=(nt,),
        in_specs=[tile, prev, nxt, tile, nxt, pl.BlockSpec((CONV_K, c), lambda i: (0, 0)),
                  pl.BlockSpec((1, c), lambda i: (0, 0))],
        out_specs=[tile, pl.BlockSpec((CONV_K, c), lambda i: (0, 0)), pl.BlockSpec((1, c), lambda i: (0, 0))],
        out_shape=[jax.ShapeDtypeStruct((s, c), F32), jax.ShapeDtypeStruct((CONV_K, c), F32),
                   jax.ShapeDtypeStruct((1, c), F32)],
        compiler_params=_params(("arbitrary",)),
    )(xbc, xbc, xbc, dact, dact, w, b)


def _ssd_chunk(xs, bm, cm, dtr, z, dt_bias, a_log, d_skip, g, s_prev):
    n = CHUNK
    row = lax.broadcasted_iota(jnp.int32, (n, n), 0)
    col = lax.broadcasted_iota(jnp.int32, (n, n), 1)
    causal = row >= col
    dt = jax.nn.softplus(dtr + dt_bias)
    a_c = dt * (-jnp.exp(a_log))
    hi = lax.Precision.HIGHEST
    a_cum = jnp.dot(causal.astype(F32), a_c, precision=hi, preferred_element_type=F32)
    a_cum_t = lax.dot_general(a_c, (row <= col).astype(F32), (_TN, ((), ())), precision=hi,
                              preferred_element_type=F32)
    p, st = SSD_HEAD_DIM, SSD_STATE
    cb = [mm_nt(cm[:, k * st:(k + 1) * st], bm[:, k * st:(k + 1) * st]) for k in range(SSD_GROUPS)]
    ys, s_new = [], []
    for h in range(SSD_HEADS):
        k = h // (SSD_HEADS // SSD_GROUPS)
        acol = a_cum[:, h:h + 1]
        decay = jnp.exp(jnp.where(causal, acol - a_cum_t[h:h + 1, :], -jnp.inf))
        xh = xs[:, h * p:(h + 1) * p]
        xdt = xh * dt[:, h:h + 1]
        a_last = a_cum[n - 1:n, h:h + 1]
        sp = s_prev[h * p:(h + 1) * p, :]
        bg = bm[:, k * st:(k + 1) * st]
        cg = cm[:, k * st:(k + 1) * st]
        s_new.append(sp * jnp.exp(a_last) + mm_tn(xdt * jnp.exp(a_last - acol), bg))
        y = mm_nn(cb[k] * decay, xdt) + mm_nt(cg, sp) * jnp.exp(acol) + d_skip[:, h:h + 1] * xh
        ys.append(y)
    y = jnp.concatenate(ys, axis=1) * jax.nn.silu(z)
    return _rms(y, g), jnp.concatenate(s_new, axis=0)


def _split_xbc(t):
    return t[:, :SSD_DIM], t[:, SSD_DIM:SSD_DIM + GN], t[:, SSD_DIM + GN:]


def _ssd_fwd(tag, xact, dtr, z, dt_bias, a_log, d_skip, g):
    s = xact.shape[0]
    nc = s // CHUNK
    srows = SSD_HEADS * SSD_HEAD_DIM

    def body(x_ref, dt_ref, z_ref, b_ref, al_ref, ds_ref, g_ref, y_ref, st_ref, state):
        @pl.when(pl.program_id(0) == 0)
        def _():
            state[...] = jnp.zeros_like(state)

        sp = state[...]
        st_ref[0] = sp
        xs, bm, cm = _split_xbc(x_ref[...])
        y, sn = _ssd_chunk(xs, bm, cm, dt_ref[...][:, :SSD_HEADS], z_ref[...], b_ref[...], al_ref[...],
                           ds_ref[...], g_ref[...], sp)
        y_ref[...] = y
        state[...] = sn

    small = pl.BlockSpec((1, SSD_HEADS), lambda i: (0, 0))
    return pl.pallas_call(
        body, name="ssd_fwd" + tag, grid=(nc,),
        in_specs=[pl.BlockSpec((CHUNK, CONV_DIM), lambda i: (i, 0)), pl.BlockSpec((CHUNK, DT_PAD), lambda i: (i, 0)),
                  pl.BlockSpec((CHUNK, SSD_DIM), lambda i: (i, 0)), small, small, small,
                  pl.BlockSpec((1, SSD_DIM), lambda i: (0, 0))],
        out_specs=[pl.BlockSpec((CHUNK, SSD_DIM), lambda i: (i, 0)),
                   pl.BlockSpec((1, srows, SSD_STATE), lambda i: (i, 0, 0))],
        out_shape=[jax.ShapeDtypeStruct((s, SSD_DIM), F32), jax.ShapeDtypeStruct((nc, srows, SSD_STATE), F32)],
        scratch_shapes=[pltpu.VMEM((srows, SSD_STATE), F32)],
        compiler_params=_params(("arbitrary",)),
    )(xact, dtr, z, dt_bias, a_log, d_skip, g)


def _ssd_bwd(tag, xact, dtr, z, states, dy, dt_bias, a_log, d_skip, g):
    s = xact.shape[0]
    nc = s // CHUNK
    srows = SSD_HEADS * SSD_HEAD_DIM

    def body(x_ref, dt_ref, z_ref, sp_ref, dy_ref, b_ref, al_ref, ds_ref, g_ref,
             dx_ref, ddt_ref, dz_ref, db_ref, dal_ref, dds_ref, dg_ref, dstate):
        first = pl.program_id(0) == 0

        @pl.when(first)
        def _():
            dstate[...] = jnp.zeros_like(dstate)

        xs, bm, cm = _split_xbc(x_ref[...])
        _, vjp = jax.vjp(_ssd_chunk, xs, bm, cm, dt_ref[...][:, :SSD_HEADS], z_ref[...], b_ref[...], al_ref[...],
                         ds_ref[...], g_ref[...], sp_ref[0])
        dxs, dbm, dcm, ddt, dz, db, dal, dds, dg, dsp = vjp((dy_ref[...], dstate[...]))
        dx_ref[...] = jnp.concatenate([dxs, dbm, dcm], axis=1)
        ddt_ref[...] = jnp.concatenate([ddt, jnp.zeros((CHUNK, DT_PAD - SSD_HEADS), F32)], axis=1)
        dz_ref[...] = dz
        dstate[...] = dsp

        @pl.when(first)
        def _():
            db_ref[...] = db
            dal_ref[...] = dal
            dds_ref[...] = dds
            dg_ref[...] = dg

        @pl.when(jnp.logical_not(first))
        def _():
            db_ref[...] += db
            dal_ref[...] += dal
            dds_ref[...] += dds
            dg_ref[...] += dg

    def rev(c):
        return lambda i: (nc - 1 - i, 0)

    small = pl.BlockSpec((1, SSD_HEADS), lambda i: (0, 0))
    gspec = pl.BlockSpec((1, SSD_DIM), lambda i: (0, 0))
    return pl.pallas_call(
        body, name="ssd_bwd" + tag, grid=(nc,),
        in_specs=[pl.BlockSpec((CHUNK, CONV_DIM), rev(0)), pl.BlockSpec((CHUNK, DT_PAD), rev(0)),
                  pl.BlockSpec((CHUNK, SSD_DIM), rev(0)),
                  pl.BlockSpec((1, srows, SSD_STATE), lambda i: (nc - 1 - i, 0, 0)),
                  pl.BlockSpec((CHUNK, SSD_DIM), rev(0)), small, small, small, gspec],
        out_specs=[pl.BlockSpec((CHUNK, CONV_DIM), rev(0)), pl.BlockSpec((CHUNK, DT_PAD), rev(0)),
                   pl.BlockSpec((CHUNK, SSD_DIM), rev(0)), small, small, small, gspec],
        out_shape=[jax.ShapeDtypeStruct((s, CONV_DIM), F32), jax.ShapeDtypeStruct((s, DT_PAD), F32),
                   jax.ShapeDtypeStruct((s, SSD_DIM), F32), jax.ShapeDtypeStruct((1, SSD_HEADS), F32),
                   jax.ShapeDtypeStruct((1, SSD_HEADS), F32), jax.ShapeDtypeStruct((1, SSD_HEADS), F32),
                   jax.ShapeDtypeStruct((1, SSD_DIM), F32)],
        scratch_shapes=[pltpu.VMEM((srows, SSD_STATE), F32)],
        compiler_params=_params(("arbitrary",)),
    )(xact, dtr, z, states, dy, dt_bias, a_log, d_skip, g)


TQ = 128
TK = 128
SB_SCALE = 1.0 / math.sqrt(SB_HEAD_DIM)


def _split2(x):
    hi = x.astype(BF16)
    return hi, (x - hi.astype(F32)).astype(BF16)


TK_WIDE = 512


def _sb_logits(qh, kb, t0, s0, masked):
    z = _dg(qh, kb, _NT)
    lb = jnp.minimum(z, 0.0) - jnp.log(1.0 + jnp.exp(-jnp.abs(z)))
    ls = lb - z
    if not masked:
        return lb, ls, None
    t_pos = t0 + lax.broadcasted_iota(jnp.int32, z.shape, 0)
    s_pos = s0 + lax.broadcasted_iota(jnp.int32, z.shape, 1)
    mask = s_pos < t_pos
    return lb, jnp.where(mask, ls, 0.0), mask


def _running_sums(x, start, u, reverse, two_terms=True):
    nsub = x.shape[1] // TK
    run = start
    parts = [None] * nsub
    for c in (reversed(range(nsub)) if reverse else range(nsub)):
        xc = x[:, c * TK:(c + 1) * TK]
        parts[c] = run + (_lane_sums(xc, u) if two_terms else _dg(xc, u, _NN))
        run = run + jnp.sum(xc, axis=1, keepdims=True)
    return (parts[0] if nsub == 1 else jnp.concatenate(parts, axis=1)), run


def _lane_sums(x, u):
    hi, lo = _split2(x)
    return _dg(hi, u, _NN) + _dg(lo, u, _NN)


def _tri(cmp):
    j = lax.broadcasted_iota(jnp.int32, (TK, TK), 0)
    s = lax.broadcasted_iota(jnp.int32, (TK, TK), 1)
    return cmp(j, s).astype(BF16)


def _sb_fwd(tag, q, k, v):
    s = q.shape[0]
    npair = SB_DIM // LANES

    wide = min(TK_WIDE, s)
    per = wide // TQ

    def body(q_ref, k_ref, v_ref, o_ref, t_ref):
        tq = pl.program_id(1)
        diag = tq // per
        qp = q_ref[...]
        lane = lax.broadcasted_iota(jnp.int32, (1, LANES), 1)
        u_gt = _tri(lambda j, s: j > s)
        out = jnp.zeros((TQ, LANES), F32)
        tot = jnp.zeros((TQ, LANES), F32)
        for hh in range(LANES // SB_HEAD_DIM):
            hm = (lane // SB_HEAD_DIM) == hh
            qh = jnp.where(hm, qp, jnp.zeros_like(qp)) * SB_SCALE

            def block(wb, r, acc, masked):
                off = pl.multiple_of(wb * wide, wide)
                kb = k_ref[pl.ds(off, wide), :]
                vb = v_ref[pl.ds(off, wide), :]
                lb, ls, mask = _sb_logits(qh, kb, tq * TQ, wb * wide, masked)
                later, r = _running_sums(ls, r, u_gt, reverse=True)
                w = jnp.exp(lb + later)
                if masked:
                    w = jnp.where(mask, w, 0.0)
                return r, acc + _dg(w, vb, _NN)

            r, acc = block(diag, jnp.zeros((TQ, 1), F32), jnp.zeros((TQ, LANES), F32), True)
            r, acc = lax.fori_loop(0, diag, lambda j, c: block(diag - 1 - j, c[0], c[1], False), (r, acc))
            out = out + jnp.where(hm, acc, 0.0)
            tot = tot + jnp.where(hm, r, 0.0)
        o_ref[...] = out
        t_ref[...] = tot

    tile = pl.BlockSpec((TQ, LANES), lambda p, t: (t, p))
    full = pl.BlockSpec((s, LANES), lambda p, t: (0, p))
    return pl.pallas_call(
        body, name="sb_fwd" + tag, grid=(npair, s // TQ),
        in_specs=[tile, full, full], out_specs=[tile, tile],
        out_shape=[jax.ShapeDtypeStruct((s, SB_DIM), F32)] * 2,
        compiler_params=_params(("parallel", "arbitrary")),
    )(q, k, v)


def _sb_bwd(tag, q, k, v, tot, do):
    s = q.shape[0]
    npair = SB_DIM // LANES

    wide = min(TK_WIDE, s)
    per = wide // TQ

    def body(q_ref, k_ref, v_ref, t_ref, do_ref, dq_ref, dk_ref, dv_ref):
        tq = pl.program_id(1)
        diag = tq // per

        @pl.when(tq == 0)
        def _():
            dk_ref[...] = jnp.zeros_like(dk_ref)
            dv_ref[...] = jnp.zeros_like(dv_ref)

        qp = q_ref[...]
        dop = do_ref[...]
        totp = t_ref[...]
        lane = lax.broadcasted_iota(jnp.int32, (1, LANES), 1)
        u_le = _tri(lambda j, s: j <= s)
        u_lt = _tri(lambda j, s: j < s)
        dq = jnp.zeros((TQ, LANES), F32)
        for hh in range(LANES // SB_HEAD_DIM):
            hm = (lane // SB_HEAD_DIM) == hh
            qh = jnp.where(hm, qp, jnp.zeros_like(qp)) * SB_SCALE
            doh = jnp.where(hm, dop, 0.0).astype(BF16)
            total = jnp.sum(jnp.where(lane == hh * SB_HEAD_DIM, totp, 0.0), axis=1, keepdims=True)

            def block(wb, pre, gc, acc, masked):
                off = pl.multiple_of(wb * wide, wide)
                kb = k_ref[pl.ds(off, wide), :]
                vb = v_ref[pl.ds(off, wide), :]
                lb, ls, mask = _sb_logits(qh, kb, tq * TQ, wb * wide, masked)
                before, pre = _running_sums(ls, pre, u_le, reverse=False)
                w = jnp.exp(lb + (total - before))
                if masked:
                    w = jnp.where(mask, w, 0.0)
                g = w * _dg(doh, vb, _NT)
                g_left, gc = _running_sums(g, gc, u_lt, reverse=False, two_terms=False)
                dz = g - jnp.exp(lb) * (g + g_left)
                if masked:
                    dz = jnp.where(mask, dz, 0.0)
                dzb = dz.astype(BF16)
                dk_ref[pl.ds(off, wide), :] += _dg(dzb, qh, _TN)
                dv_ref[pl.ds(off, wide), :] += _dg(w, doh, _TN)
                return pre, gc, acc + _dg(dzb, kb, _NN)

            zero = jnp.zeros((TQ, 1), F32)
            carry = lax.fori_loop(0, diag, lambda j, c: block(j, c[0], c[1], c[2], False),
                                  (zero, zero, jnp.zeros((TQ, LANES), F32)))
            _, _, acc = block(diag, carry[0], carry[1], carry[2], True)
            dq = dq + jnp.where(hm, acc, 0.0)
        dq_ref[...] = dq * SB_SCALE

    tile = pl.BlockSpec((TQ, LANES), lambda p, t: (t, p))
    full = pl.BlockSpec((s, LANES), lambda p, t: (0, p))
    return pl.pallas_call(
        body, name="sb_bwd" + tag, grid=(npair, s // TQ),
        in_specs=[tile, full, full, tile, tile],
        out_specs=[tile, full, full],
        out_shape=[jax.ShapeDtypeStruct((s, SB_DIM), F32)] * 3,
        compiler_params=_params(("parallel", "arbitrary")),
    )(q, k, v, tot, do)


def _out_tile(y_ssd, o, sb_g, w_out):
    y_all = jnp.concatenate([y_ssd, _rms(o, sb_g)], axis=1)
    return mm_nn(y_all, w_out)


def _out_fwd(tag, h, y_ssd, o, sb_g, w_out):
    return _rowcall("out_fwd" + tag, lambda h, y, o, g, w: (h + _out_tile(y, o, g, w),),
                    [h, y_ssd, o], [sb_g, w_out], [(D_MODEL, F32)], tm=512)[0]


def _out_bwd(tag, y_ssd, o, dh, sb_g, w_out):
    def fn(y, o, dh, g, w):
        _, vjp = jax.vjp(_out_tile, y, o, g, w.astype(F32))
        return vjp(dh)

    return _rowcall("out_bwd" + tag, fn, [y_ssd, o, dh], [sb_g, w_out], [(SSD_DIM, F32), (SB_DIM, F32)],
                    [((1, SB_DIM), F32), ((D_MODEL, D_MODEL), F32)], tm=256)


def _mem_tile(mem, g, w_k, w_v):
    m = _rms(mem, g)
    return mm_nn(m, w_k), mm_nn(m, w_v)


def _mem_fwd(tag, mem, g, w_k, w_v):
    return _rowcall("mem_fwd" + tag, _mem_tile, [mem], [g, w_k, w_v], [(XA_DIM, F32), (XA_DIM, F32)], tm=256)


def _mem_bwd(tag, mem, dkx, dvx, g, w_k, w_v):
    def fn(mem, dkx, dvx, g, w_k, w_v):
        _, vjp = jax.vjp(lambda g, a, b: _mem_tile(mem, g, a, b), g, w_k.astype(F32), w_v.astype(F32))
        return vjp((dkx, dvx))

    return _rowcall("mem_bwd" + tag, fn, [mem, dkx, dvx], [g, w_k, w_v], [],
                    [((1, D_MODEL), F32), ((D_MODEL, XA_DIM), F32), ((D_MODEL, XA_DIM), F32)], tm=256)


def _xattn_tile(h, g, w_q, kx, vx, w_o):
    q = mm_nn(_rms(h, g), w_q)
    scale = 1.0 / math.sqrt(XA_HEAD_DIM)
    outs = []
    for i in range(XA_HEADS):
        sl = slice(i * XA_HEAD_DIM, (i + 1) * XA_HEAD_DIM)
        p = jax.nn.softmax(mm_nt(q[:, sl], kx[:, sl]) * scale, axis=-1)
        outs.append(mm_nn(p, vx[:, sl]))
    return mm_nn(jnp.concatenate(outs, axis=1), w_o)


def _xattn_fwd(tag, h, g, w_q, kx, vx, w_o):
    return _rowcall("xattn_fwd" + tag, lambda h, g, wq, kx, vx, wo: (h + _xattn_tile(h, g, wq, kx, vx, wo),),
                    [h], [g, w_q, kx, vx, w_o], [(D_MODEL, F32)], tm=512)[0]


def _xattn_bwd(tag, h, dh_out, g, w_q, kx, vx, w_o):
    def fn(h, dh_out, g, w_q, kx, vx, w_o):
        _, vjp = jax.vjp(_xattn_tile, h, g, w_q.astype(F32), kx, vx, w_o.astype(F32))
        dh, dg, dwq, dkx, dvx, dwo = vjp(dh_out)
        return dh_out + dh, dg, dwq, dkx, dvx, dwo

    mlen = kx.shape[0]
    return _rowcall("xattn_bwd" + tag, fn, [h, dh_out], [g, w_q, kx, vx, w_o], [(D_MODEL, F32)],
                    [((1, D_MODEL), F32), ((D_MODEL, XA_DIM), F32), ((mlen, XA_DIM), F32), ((mlen, XA_DIM), F32),
                     ((XA_DIM, D_MODEL), F32)], tm=256)


def _mlp_fwd(tag, h, g, w1, w2):
    def fn(h, g, w1, w2):
        u = jnp.square(jnp.maximum(mm_nn(_rms(h, g), w1), 0.0))
        return (h + mm_nn(u, w2),)

    return _rowcall("mlp_fwd" + tag, fn, [h], [g, w1, w2], [(D_MODEL, F32)], tm=256)[0]


def _mlp_bwd(tag, h, dh_out, g, w1, w2):
    def fn(h, dh_out, g, w1, w2):
        hn, vjp = jax.vjp(_rms, h, g)
        r = jnp.maximum(mm_nn(hn, w1), 0.0)
        dob = dh_out.astype(BF16)
        dp = mm_nt(dob, w2) * (2.0 * r)
        dh, dg = vjp(mm_nt(dp, w1))
        return dh_out + dh, hn, dp, r * r, dob, dg

    return _rowcall("mlp_bwd" + tag, fn, [h, dh_out], [g, w1, w2],
                    [(D_MODEL, F32), (D_MODEL, BF16), (D_FF, BF16), (D_FF, BF16), (D_MODEL, BF16)],
                    [((1, D_MODEL), F32)], tm=256)


def _head(h, g, target):
    def lossfn(h, g, t):
        err = jnp.square(_rms(h, g) - t)
        return 0.5 * jnp.sum(jnp.mean(err, axis=-1))

    def fn(h, t, g):
        loss, vjp = jax.vjp(lambda h, g: lossfn(h, g, t), h, g)
        dh, dg = vjp(jnp.ones((), F32))
        return dh, jnp.full((1, LANES), loss, F32), dg

    return _rowcall("head", fn, [h, target], [g], [(D_MODEL, F32)], [((1, LANES), F32), ((1, D_MODEL), F32)], tm=512)


def _row(v):
    return v.reshape(1, -1)


def _local_step(x, mem, target, sw, mats):
    h = x
    saved = []
    for l in range(DEPTH):
        tag = str(l)
        m = mats[l]
        z, xbc, q, k, v, dtr = _proj_fwd(tag, h, _row(sw["norm_mix_g"][l]), m["w_in"])
        xact = _conv_fwd(tag, xbc, sw["conv_w"][l], _row(sw["conv_b"][l]))
        y_ssd, states = _ssd_fwd(tag, xact, dtr, z, _row(sw["dt_bias"][l]), _row(sw["a_log"][l]),
                                 _row(sw["d_skip"][l]), _row(sw["ssd_norm_g"][l]))
        o, sb_tot = _sb_fwd(tag, q, k, v)
        h1 = _out_fwd(tag, h, y_ssd, o, _row(sw["sb_norm_g"][l]), m["w_out"])
        kx, vx = _mem_fwd(tag, mem, _row(sw["norm_mem_g"][l]), m["w_xk"], m["w_xv"])
        h2 = _xattn_fwd(tag, h1, _row(sw["norm_xa_g"][l]), m["w_xq"], kx, vx, m["w_xo"])
        h3 = _mlp_fwd(tag, h2, _row(sw["norm_ff_g"][l]), m["w_ff1"], m["w_ff2"])
        saved.append((h, z, xbc, q, k, v, dtr, xact, y_ssd, states, o, sb_tot, h1, kx, vx, h2))
        h = h3

    dh, loss, d_final = _head(h, _row(sw["final_g"]), target)
    gm = [dict() for _ in range(DEPTH)]
    gs = {name: [None] * DEPTH for name in ("norm_mix_g", "conv_w", "conv_b", "dt_bias", "a_log", "d_skip",
                                            "ssd_norm_g", "sb_norm_g", "norm_xa_g", "norm_mem_g", "norm_ff_g")}
    for l in reversed(range(DEPTH)):
        tag = str(l)
        m = mats[l]
        h0, z, xbc, q, k, v, dtr, xact, y_ssd, states, o, sb_tot, h1, kx, vx, h2 = saved[l]
        dh2, hn_b, dp_b, a_b, do_b, gs["norm_ff_g"][l] = _mlp_bwd(
            tag, h2, dh, _row(sw["norm_ff_g"][l]), m["w_ff1"], m["w_ff2"])
        gm[l]["w_ff1"] = _mm_tn_call("dw_ff1" + tag, hn_b, dp_b, 512, 1024, 2048)
        gm[l]["w_ff2"] = _mm_tn_call("dw_ff2" + tag, a_b, do_b, 1024, 1024, 2048)
        dh1, gs["norm_xa_g"][l], gm[l]["w_xq"], dkx, dvx, gm[l]["w_xo"] = _xattn_bwd(
            tag, h1, dh2, _row(sw["norm_xa_g"][l]), m["w_xq"], kx, vx, m["w_xo"])
        gs["norm_mem_g"][l], gm[l]["w_xk"], gm[l]["w_xv"] = _mem_bwd(
            tag, mem, dkx, dvx, _row(sw["norm_mem_g"][l]), m["w_xk"], m["w_xv"])
        dy_ssd, do, gs["sb_norm_g"][l], gm[l]["w_out"] = _out_bwd(
            tag, y_ssd, o, dh1, _row(sw["sb_norm_g"][l]), m["w_out"])
        dq, dk, dv = _sb_bwd(tag, q, k, v, sb_tot, do)
        dxact, ddtr, dz, gs["dt_bias"][l], gs["a_log"][l], gs["d_skip"][l], gs["ssd_norm_g"][l] = _ssd_bwd(
            tag, xact, dtr, z, states, dy_ssd, _row(sw["dt_bias"][l]), _row(sw["a_log"][l]),
            _row(sw["d_skip"][l]), _row(sw["ssd_norm_g"][l]))
        dxbc, gs["conv_w"][l], gs["conv_b"][l] = _conv_bwd(tag, xbc, dxact, sw["conv_w"][l], _row(sw["conv_b"][l]))
        dh, hn_b, dp_b, gs["norm_mix_g"][l] = _proj_bwd(
            tag, h0, dh1, dz, dxbc, dq, dk, dv, ddtr, _row(sw["norm_mix_g"][l]), m["w_in"])
        gm[l]["w_in"] = _mm_tn_call("dw_in" + tag, hn_b, dp_b, 512, IN_PAD, 1024)
    gs["final_g"] = d_final
    return loss, dh, gm, gs


ANY = pl.BlockSpec(memory_space=pl.ANY)
VMEM_SPEC = pl.BlockSpec(memory_space=pltpu.VMEM)


def _place():
    return lax.axis_index("x"), lax.axis_index("y"), lax.axis_index("c")


def _other_chips(x, y):
    return [(1 - x, y), (x, 1 - y), (1 - x, 1 - y)]


def _gather_weights(pack):
    rows = pack.shape[0]
    half = rows // 2

    def body(p_ref, o_ref, send_sems, recv_sems):
        x, y, c = _place()
        me_chip = 2 * x + y
        sibling = (x, y, 1 - c)
        chips = _other_chips(x, y)

        def part(chip, hh):
            return o_ref.at[chip, pl.ds(pl.multiple_of(hh * half, 16), half), :]

        def copy(kk, src, dst, to):
            return pltpu.make_async_remote_copy(src_ref=src, dst_ref=dst, send_sem=send_sems.at[kk],
                                                recv_sem=recv_sems.at[kk], device_id=to, device_id_type=MESH)

        my_half = p_ref.at[pl.ds(pl.multiple_of(c * half, 16), half), :]
        first = [copy(kk, my_half, part(me_chip, c), (cx, cy, c)) for kk, (cx, cy) in enumerate(chips)]
        for cp in first:
            cp.start()
        passed = []
        for kk, (cx, cy) in enumerate(chips):
            got = part(2 * cx + cy, c)
            copy(kk, got, got, (x, y, c)).wait_recv()
            fwd = copy(3 + kk, got, got, sibling)
            fwd.start()
            passed.append(fwd)
        for kk, (cx, cy) in enumerate(chips):
            got = part(2 * cx + cy, 1 - c)
            copy(3 + kk, got, got, (x, y, c)).wait_recv()
        for cp in first + passed:
            cp.wait_send()

    return pl.pallas_call(
        body, name="gather_weights", in_specs=[ANY], out_specs=ANY,
        out_shape=jax.ShapeDtypeStruct((N_CHIPS, rows, PACK_COLS), pack.dtype),
        scratch_shapes=[pltpu.SemaphoreType.DMA((6,)), pltpu.SemaphoreType.DMA((6,))],
    )(pack)


def _gather_small(tag, buf):
    shape = buf.shape

    def body(b_ref, o_ref, sum_ref, send_sems, recv_sems, local_sem):
        x, y, c = _place()
        me = 4 * x + 2 * y + c
        mine = pltpu.make_async_copy(b_ref, o_ref.at[me], local_sem)
        mine.start()
        flips = [(dx, dy, dc) for dx in (0, 1) for dy in (0, 1) for dc in (0, 1) if (dx, dy, dc) != (0, 0, 0)]
        sends = []

        def peer(dx, dy, dc):
            return (1 - x if dx else x, 1 - y if dy else y, 1 - c if dc else c)

        for kk, flip in enumerate(flips):
            cp = pltpu.make_async_remote_copy(src_ref=b_ref, dst_ref=o_ref.at[me], send_sem=send_sems.at[kk],
                                              recv_sem=recv_sems.at[kk], device_id=peer(*flip), device_id_type=MESH)
            cp.start()
            sends.append(cp)
        for kk, flip in enumerate(flips):
            px, py, pc = peer(*flip)
            frm = 4 * px + 2 * py + pc
            pltpu.make_async_remote_copy(src_ref=b_ref, dst_ref=o_ref.at[frm], send_sem=send_sems.at[kk],
                                         recv_sem=recv_sems.at[kk], device_id=(x, y, c),
                                         device_id_type=MESH).wait_recv()
        for cp in sends:
            cp.wait_send()
        mine.wait()
        total = o_ref[0]
        for d in range(1, N_DEV):
            total = total + o_ref[d]
        sum_ref[...] = total

    return pl.pallas_call(
        body, name="gather_small" + tag, in_specs=[VMEM_SPEC], out_specs=[VMEM_SPEC, VMEM_SPEC],
        out_shape=[jax.ShapeDtypeStruct((N_DEV,) + shape, buf.dtype), jax.ShapeDtypeStruct(shape, buf.dtype)],
        scratch_shapes=[pltpu.SemaphoreType.DMA((N_DEV - 1,)), pltpu.SemaphoreType.DMA((N_DEV - 1,)),
                        pltpu.SemaphoreType.DMA],
    )(buf)


def _swap_halves(g):
    half = g.shape[1] // 2

    def body(g_ref, r_ref, send_sem, recv_sem):
        x, y, c = _place()
        src = g_ref.at[:, pl.ds(pl.multiple_of((1 - c) * half, 8), half), :]
        cp = pltpu.make_async_remote_copy(src_ref=src, dst_ref=r_ref, send_sem=send_sem, recv_sem=recv_sem,
                                          device_id=(x, y, 1 - c), device_id_type=MESH)
        cp.start()
        cp.wait()

    return pl.pallas_call(
        body, name="swap_halves", in_specs=[ANY], out_specs=ANY,
        out_shape=jax.ShapeDtypeStruct((N_CHIPS, half, PACK_COLS), g.dtype),
        scratch_shapes=[pltpu.SemaphoreType.DMA, pltpu.SemaphoreType.DMA],
    )(g)


def _add_halves(g, r, tr=208):
    half = r.shape[1]
    core = lax.axis_index("c").astype(jnp.int32).reshape(1)
    per = half // tr

    def body(c_ref, g_ref, r_ref, o_ref):
        o_ref[...] = (g_ref[...] + r_ref[...]).astype(o_ref.dtype)

    return pl.pallas_call(
        body, name="add_halves",
        grid_spec=pltpu.PrefetchScalarGridSpec(
            num_scalar_prefetch=1, grid=(N_CHIPS, per),
            in_specs=[pl.BlockSpec((1, tr, PACK_COLS), lambda p, i, c_ref: (p, c_ref[0] * per + i, 0)),
                      pl.BlockSpec((1, tr, PACK_COLS), lambda p, i, c_ref: (p, i, 0))],
            out_specs=pl.BlockSpec((1, tr, PACK_COLS), lambda p, i, c_ref: (p, i, 0))),
        out_shape=jax.ShapeDtypeStruct(r.shape, BF16),
        compiler_params=_params(("arbitrary", "arbitrary")),
    )(core, g, r)


def _scatter_chips(sb):
    def body(s_ref, o_ref, send_sems, recv_sems):
        x, y, c = _place()
        me_chip = 2 * x + y
        chips = _other_chips(x, y)
        sends = []
        for kk, (cx, cy) in enumerate(chips):
            cp = pltpu.make_async_remote_copy(src_ref=s_ref.at[2 * cx + cy], dst_ref=o_ref.at[me_chip],
                                              send_sem=send_sems.at[kk], recv_sem=recv_sems.at[kk],
                                              device_id=(cx, cy, c), device_id_type=MESH)
            cp.start()
            sends.append(cp)
        for kk, (cx, cy) in enumerate(chips):
            got = o_ref.at[2 * cx + cy]
            pltpu.make_async_remote_copy(src_ref=got, dst_ref=got, send_sem=send_sems.at[kk],
                                         recv_sem=recv_sems.at[kk], device_id=(x, y, c),
                                         device_id_type=MESH).wait_recv()
        for cp in sends:
            cp.wait_send()

    return pl.pallas_call(
        body, name="scatter_chips", in_specs=[ANY], out_specs=ANY,
        out_shape=jax.ShapeDtypeStruct(sb.shape, sb.dtype),
        scratch_shapes=[pltpu.SemaphoreType.DMA((3,)), pltpu.SemaphoreType.DMA((3,))],
    )(sb)


def _sum_parts(own, parts, tr=208):
    half = parts.shape[1]
    chip = (2 * lax.axis_index("x") + lax.axis_index("y")).astype(jnp.int32).reshape(1)

    def body(c_ref, own_ref, p1_ref, p2_ref, p3_ref, o_ref):
        total = own_ref[0].astype(F32)
        for p_ref in (p1_ref, p2_ref, p3_ref):
            total = total + p_ref[0].astype(F32)
        o_ref[...] = total

    def after(kk):
        return pl.BlockSpec((1, tr, PACK_COLS), lambda i, c_ref: ((c_ref[0] + kk) % N_CHIPS, i, 0))

    return pl.pallas_call(
        body, name="sum_parts",
        grid_spec=pltpu.PrefetchScalarGridSpec(
            num_scalar_prefetch=1, grid=(half // tr,), in_specs=[after(0), after(1), after(2), after(3)],
            out_specs=pl.BlockSpec((tr, PACK_COLS), lambda i, c_ref: (i, 0))),
        out_shape=jax.ShapeDtypeStruct((half, PACK_COLS), F32),
        compiler_params=_params(("arbitrary",)),
    )(chip, own, parts, parts, parts)


def _swap_reduced(red):
    def body(r_ref, o_ref, send_sem, recv_sem):
        x, y, c = _place()
        cp = pltpu.make_async_remote_copy(src_ref=r_ref, dst_ref=o_ref, send_sem=send_sem, recv_sem=recv_sem,
                                          device_id=(x, y, 1 - c), device_id_type=MESH)
        cp.start()
        cp.wait()

    return pl.pallas_call(
        body, name="swap_reduced", in_specs=[ANY], out_specs=ANY,
        out_shape=jax.ShapeDtypeStruct(red.shape, red.dtype),
        scratch_shapes=[pltpu.SemaphoreType.DMA, pltpu.SemaphoreType.DMA],
    )(red)


def _adamw_math(w, g, m, v):
    m = ADAM_B1 * m + (1.0 - ADAM_B1) * g
    v = ADAM_B2 * v + (1.0 - ADAM_B2) * jnp.square(g)
    m_hat = m / (1.0 - ADAM_B1 ** ADAM_STEP)
    v_hat = v / (1.0 - ADAM_B2 ** ADAM_STEP)
    delta = -ADAM_LR * (m_hat / (jnp.sqrt(v_hat) + ADAM_EPS) + ADAM_WD * w)
    return delta, m, v


def _adamw(tag, w, g, m, v, tr=256):
    rows, cols = w.shape
    tr = min(tr, rows)

    def body(w_ref, g_ref, m_ref, v_ref, d_ref, nm_ref, nv_ref):
        d_ref[...], nm_ref[...], nv_ref[...] = _adamw_math(w_ref[...], g_ref[...], m_ref[...], v_ref[...])

    spec = pl.BlockSpec((tr, cols), lambda i: (i, 0))
    return pl.pallas_call(
        body, name="adamw_" + tag, grid=(rows // tr,), in_specs=[spec] * 4, out_specs=[spec] * 3,
        out_shape=[jax.ShapeDtypeStruct(w.shape, F32)] * 3,
        compiler_params=_params(("parallel",)),
    )(w, g, m, v)


def _pad_rows(a, rows):
    return jnp.concatenate([a, jnp.zeros((rows - a.shape[0], a.shape[1]), a.dtype)], axis=0)


def _pack_shards(get):
    parts = [get(l, name).reshape(-1, PACK_COLS) for l in range(DEPTH) for name, _, _ in MATS]
    return _pad_rows(jnp.concatenate(parts, axis=0), PACK_ROWS)


def _unpack_shards(rows2d):
    out = {name: [] for name, _, _ in MATS}
    off = 0
    for l in range(DEPTH):
        for name, shape, axis in MATS:
            ss = _shard_shape(shape, axis)
            n = ss[0] * ss[1] // PACK_COLS
            out[name].append(rows2d[off:off + n].reshape(ss))
            off += n
    return {name: jnp.stack(v) for name, v in out.items()}


def _w_in_to_padded(w):
    d0 = SSD_DIM + CONV_DIM
    return jnp.concatenate([w[:, :d0], w[:, d0 + SSD_HEADS:], w[:, d0:d0 + SSD_HEADS],
                            jnp.zeros((w.shape[0], DT_PAD - SSD_HEADS), w.dtype)], axis=1)


def _w_in_from_padded(w):
    d0 = SSD_DIM + CONV_DIM
    return jnp.concatenate([w[:, :d0], w[:, DT_OFF:DT_OFF + SSD_HEADS], w[:, d0:DT_OFF]], axis=1)


def _small_layout():
    return (("norm_mix_g", 0, 0, D_MODEL), ("norm_xa_g", 1, 0, D_MODEL), ("norm_mem_g", 2, 0, D_MODEL),
            ("norm_ff_g", 3, 0, D_MODEL), ("conv_b", 4, 0, CONV_DIM), ("ssd_norm_g", 5, 0, SSD_DIM),
            ("sb_norm_g", 5, SSD_DIM, SB_DIM), ("dt_bias", 6, 0, SSD_HEADS), ("a_log", 6, LANES, SSD_HEADS),
            ("d_skip", 6, 2 * LANES, SSD_HEADS))


def _pack_small(vals, conv_w_full, extra=None):
    rows = []
    for l in range(DEPTH):
        lay = _small_layout()
        for r in range(_SM_PER_LAYER):
            cells = [(c0, vals[name][l].reshape(-1)) for name, rr, c0, _ in lay if rr == r]
            line, pos = [], 0
            for c0, val in cells:
                if c0 > pos:
                    line.append(jnp.zeros((c0 - pos,), F32))
                line.append(val)
                pos = c0 + val.shape[0]
            line.append(jnp.zeros((PACK_COLS - pos,), F32))
            rows.append(jnp.concatenate(line))
    rows.append(vals["final_g"].reshape(-1))
    rows.append(conv_w_full.reshape(-1, PACK_COLS))
    used = _SM_CONVW + DEPTH * CONV_K * CONV_DIM // PACK_COLS
    last = jnp.zeros((SMALL_ROWS - used, PACK_COLS), F32)
    if extra is not None:
        last = last.at[SMALL_ROWS - used - 1, 0].set(extra)
    return jnp.concatenate([r.reshape(-1, PACK_COLS) for r in rows] + [last], axis=0)


def _unpack_small(buf):
    out = {}
    for name, rr, c0, width in _small_layout():
        out[name] = jnp.stack([buf[l * _SM_PER_LAYER + rr, c0:c0 + width] for l in range(DEPTH)])
    out["final_g"] = buf[_SM_FINAL]
    n = DEPTH * CONV_K * CONV_DIM // PACK_COLS
    out["conv_w"] = buf[_SM_CONVW:_SM_CONVW + n].reshape(DEPTH, CONV_K, CONV_DIM)
    return out


CONV_BLOCK_ROWS = 8


def _conv_w_block(cw):
    flat = cw.reshape(-1)
    pad = jnp.zeros((CONV_BLOCK_ROWS * PACK_COLS - flat.shape[0],), F32)
    return jnp.concatenate([flat, pad]).reshape(CONV_BLOCK_ROWS, PACK_COLS)


def _conv_w_from_slots(slots):
    cols = CONV_DIM // N_CHIPS
    n = DEPTH * CONV_K * cols
    shards = [slots[2 * j].reshape(-1)[:n].reshape(DEPTH, CONV_K, cols) for j in range(N_CHIPS)]
    return jnp.concatenate(shards, axis=2)


def _conv_w_padded(cw):
    return jnp.concatenate([cw, jnp.zeros(cw.shape[:2] + (CONV_DIM - cw.shape[2],), F32)], axis=2)


SMALL_NAMES = ("norm_mix_g", "conv_b", "dt_bias", "a_log", "d_skip", "ssd_norm_g", "sb_norm_g", "norm_xa_g",
               "norm_mem_g", "norm_ff_g", "final_g")
WEIGHT_ORDER = ("norm_mix_g", "w_in", "conv_w", "conv_b", "dt_bias", "a_log", "d_skip", "ssd_norm_g", "sb_norm_g",
                "w_out", "norm_xa_g", "norm_mem_g", "w_xq", "w_xk", "w_xv", "w_xo", "norm_ff_g", "w_ff1", "w_ff2",
                "final_g")


def kernel(x, mem, norm_mix_g, w_in, conv_w, conv_b, dt_bias, a_log, d_skip, ssd_norm_g, sb_norm_g, w_out, norm_xa_g, norm_mem_g, w_xq, w_xk, w_xv, w_xo, norm_ff_g, w_ff1, w_ff2, final_g, loss_target, m_norm_mix_g, m_w_in, m_conv_w, m_conv_b, m_dt_bias, m_a_log, m_d_skip, m_ssd_norm_g, m_sb_norm_g, m_w_out, m_norm_xa_g, m_norm_mem_g, m_w_xq, m_w_xk, m_w_xv, m_w_xo, m_norm_ff_g, m_w_ff1, m_w_ff2, m_final_g, v_norm_mix_g, v_w_in, v_conv_w, v_conv_b, v_dt_bias, v_a_log, v_d_skip, v_ssd_norm_g, v_sb_norm_g, v_w_out, v_norm_xa_g, v_norm_mem_g, v_w_xq, v_w_xk, v_w_xv, v_w_xo, v_norm_ff_g, v_w_ff1, v_w_ff2, v_final_g):
    w = dict(norm_mix_g=norm_mix_g, w_in=w_in, conv_w=conv_w, conv_b=conv_b, dt_bias=dt_bias, a_log=a_log,
             d_skip=d_skip, ssd_norm_g=ssd_norm_g, sb_norm_g=sb_norm_g, w_out=w_out, norm_xa_g=norm_xa_g,
             norm_mem_g=norm_mem_g, w_xq=w_xq, w_xk=w_xk, w_xv=w_xv, w_xo=w_xo, norm_ff_g=norm_ff_g, w_ff1=w_ff1,
             w_ff2=w_ff2, final_g=final_g)
    mom = dict(norm_mix_g=m_norm_mix_g, w_in=m_w_in, conv_w=m_conv_w, conv_b=m_conv_b, dt_bias=m_dt_bias,
               a_log=m_a_log, d_skip=m_d_skip, ssd_norm_g=m_ssd_norm_g, sb_norm_g=m_sb_norm_g, w_out=m_w_out,
               norm_xa_g=m_norm_xa_g, norm_mem_g=m_norm_mem_g, w_xq=m_w_xq, w_xk=m_w_xk, w_xv=m_w_xv, w_xo=m_w_xo,
               norm_ff_g=m_norm_ff_g, w_ff1=m_w_ff1, w_ff2=m_w_ff2, final_g=m_final_g)
    var = dict(norm_mix_g=v_norm_mix_g, w_in=v_w_in, conv_w=v_conv_w, conv_b=v_conv_b, dt_bias=v_dt_bias,
               a_log=v_a_log, d_skip=v_d_skip, ssd_norm_g=v_ssd_norm_g, sb_norm_g=v_sb_norm_g, w_out=v_w_out,
               norm_xa_g=v_norm_xa_g, norm_mem_g=v_norm_mem_g, w_xq=v_w_xq, w_xk=v_w_xk, w_xv=v_w_xv, w_xo=v_w_xo,
               norm_ff_g=v_norm_ff_g, w_ff1=v_w_ff1, w_ff2=v_w_ff2, final_g=v_final_g)
    chip = 2 * lax.axis_index("x") + lax.axis_index("y")
    conv_cols = CONV_DIM // N_CHIPS

    my_pack = _pack_shards(lambda l, name: w[name][l]).astype(BF16)
    gathered = _gather_weights(my_pack)
    per_chip = [_unpack_shards(jnp.where(chip == j, my_pack, gathered[j])) for j in range(N_CHIPS)]
    mats = []
    for l in range(DEPTH):
        full = {name: jnp.concatenate([per_chip[j][name][l] for j in range(N_CHIPS)], axis=axis)
                for name, _, axis in MATS}
        full["w_in"] = _w_in_to_padded(full["w_in"])
        mats.append(full)
    conv_slots, _ = _gather_small("_conv", _conv_w_block(conv_w))
    conv_w_full = _conv_w_from_slots(conv_slots)

    sw = {name: w[name] for name in SMALL_NAMES}
    sw["conv_w"] = conv_w_full
    loss, grad_x, gm, gs = _local_step(x[0], mem[0], loss_target[0], sw, mats)

    small_vals = {name: jnp.concatenate(gs[name], axis=0) for name in gs if name not in ("conv_w", "final_g")}
    small_vals["final_g"] = gs["final_g"]
    small_buf = _pack_small(small_vals, jnp.stack(gs["conv_w"]), extra=loss[0, 0])
    _, small_sum = _gather_small("_grads", small_buf)
    g_small = _unpack_small(small_sum)
    loss_out = small_sum[_SM_LOSS, 0]

    def grad_part(j):
        def get(l, name):
            g = gm[l][name]
            if name == "w_in":
                g = _w_in_from_padded(g)
            axis = dict((n, a) for n, _, a in MATS)[name]
            size = g.shape[axis] // N_CHIPS
            return lax.slice_in_dim(g, j * size, (j + 1) * size, axis=axis)
        return _pack_shards(get)

    g_all = jnp.stack([grad_part(j) for j in range(N_CHIPS)])
    from_sibling = _swap_halves(g_all)
    to_chips = _add_halves(g_all, from_sibling)
    mine = _sum_parts(to_chips, _scatter_chips(to_chips))
    other = _swap_reduced(mine)
    south = lax.axis_index("c") == 0
    reduced = jnp.concatenate([jnp.where(south, mine, other), jnp.where(south, other, mine)], axis=0)
    g_mats = _unpack_shards(reduced)

    grads, deltas, new_m, new_v = {}, {}, {}, {}
    for name, _, _ in MATS:
        g = g_mats[name]
        cols = g.shape[-1]
        d, nm, nv = _adamw(name, w[name].reshape(-1, cols), g.reshape(-1, cols), mom[name].reshape(-1, cols),
                           var[name].reshape(-1, cols))
        grads[name] = g
        deltas[name], new_m[name], new_v[name] = (t.reshape(g.shape) for t in (d, nm, nv))
    g_conv_w = lax.dynamic_slice_in_dim(g_small["conv_w"], chip * conv_cols, conv_cols, axis=2)
    w_small = _pack_small({n: w[n] for n in SMALL_NAMES}, _conv_w_padded(conv_w))
    m_small = _pack_small({n: mom[n] for n in SMALL_NAMES}, _conv_w_padded(mom["conv_w"]))
    v_small = _pack_small({n: var[n] for n in SMALL_NAMES}, _conv_w_padded(var["conv_w"]))
    g_small_local = _pack_small({n: g_small[n] for n in SMALL_NAMES}, _conv_w_padded(g_conv_w))
    d_s, m_s, v_s = (_unpack_small(t) for t in _adamw("small", w_small, g_small_local, m_small, v_small))
    for name in SMALL_NAMES:
        grads[name] = g_small[name]
        deltas[name], new_m[name], new_v[name] = d_s[name], m_s[name], v_s[name]
    grads["conv_w"] = g_conv_w
    deltas["conv_w"], new_m["conv_w"], new_v["conv_w"] = (t["conv_w"][:, :, :conv_cols] for t in (d_s, m_s, v_s))

    return (loss_out, grad_x[None], *[grads[n] for n in WEIGHT_ORDER], *[deltas[n] for n in WEIGHT_ORDER],
            *[new_m[n] for n in WEIGHT_ORDER], *[new_v[n] for n in WEIGHT_ORDER])
```

```python
import functools
import math

import jax
import jax.numpy as jnp
from jax import lax
from jax.experimental import pallas as pl
from jax.experimental.pallas import tpu as pltpu

F32 = jnp.float32
BF16 = jnp.bfloat16
MESH = pl.DeviceIdType.MESH

D_MODEL = 1024
DEPTH = 2
SSD_DIM = 512
SSD_HEAD_DIM = 64
SSD_HEADS = 8
SSD_GROUPS = 2
SSD_STATE = 64
CONV_K = 4
CHUNK = 128
SB_DIM = 512
SB_HEAD_DIM = 64
XA_HEADS = 4
XA_HEAD_DIM = 128
XA_DIM = 512
D_FF = 4096
EPS = 1e-5
GN = SSD_GROUPS * SSD_STATE
CONV_DIM = SSD_DIM + 2 * GN
IN_DIM = SSD_DIM + CONV_DIM + SSD_HEADS + 3 * SB_DIM
LANES = 128
DT_PAD = LANES
IN_PAD = SSD_DIM + CONV_DIM + 3 * SB_DIM + DT_PAD
Q_OFF = SSD_DIM + CONV_DIM
DT_OFF = Q_OFF + 3 * SB_DIM
HALO = 8

ADAM_LR = 0.001
ADAM_B1 = 0.9
ADAM_B2 = 0.999
ADAM_EPS = 1e-08
ADAM_WD = 0.01
ADAM_STEP = 10

N_CHIPS = 4
N_DEV = 8
PACK_COLS = 1024
VMEM_LIMIT = 56 * 1024 * 1024

MATS = (
    ("w_in", (D_MODEL, IN_DIM), 1),
    ("w_out", (D_MODEL, D_MODEL), 0),
    ("w_xq", (D_MODEL, XA_DIM), 0),
    ("w_xk", (D_MODEL, XA_DIM), 0),
    ("w_xv", (D_MODEL, XA_DIM), 0),
    ("w_xo", (XA_DIM, D_MODEL), 1),
    ("w_ff1", (D_MODEL, D_FF), 1),
    ("w_ff2", (D_FF, D_MODEL), 0),
)


def _shard_shape(shape, axis):
    s = list(shape)
    s[axis] //= N_CHIPS
    return tuple(s)


def _pack_rows_total():
    n = 0
    for _, shape, axis in MATS:
        ss = _shard_shape(shape, axis)
        n += ss[0] * ss[1] // PACK_COLS
    return n * DEPTH


PACK_USED = _pack_rows_total()
PACK_ROWS = -(-PACK_USED // 32) * 32
PACK_HALF = PACK_ROWS // 2

SMALL_ROWS = 24
_SM_PER_LAYER = 7
_SM_FINAL = 14
_SM_CONVW = 15
_SM_LOSS = 23


_NN = ((1,), (0,))
_NT = ((1,), (1,))
_TN = ((0,), (0,))


def _dg(a, b, dims):
    return lax.dot_general(a.astype(BF16), b.astype(BF16), (dims, ((), ())), preferred_element_type=F32)


@jax.custom_vjp
def mm_nn(a, b):
    return _dg(a, b, _NN)


@jax.custom_vjp
def mm_nt(a, b):
    return _dg(a, b, _NT)


@jax.custom_vjp
def mm_tn(a, b):
    return _dg(a, b, _TN)


def _nn_fwd(a, b):
    return _dg(a, b, _NN), (a, b)


def _nn_bwd(res, g):
    a, b = res
    return mm_nt(g, b).astype(a.dtype), mm_tn(a, g).astype(b.dtype)


def _nt_fwd(a, b):
    return _dg(a, b, _NT), (a, b)


def _nt_bwd(res, g):
    a, b = res
    return mm_nn(g, b).astype(a.dtype), mm_tn(g, a).astype(b.dtype)


def _tn_fwd(a, b):
    return _dg(a, b, _TN), (a, b)


def _tn_bwd(res, g):
    a, b = res
    return mm_nt(b, g).astype(a.dtype), mm_nn(a, g).astype(b.dtype)


mm_nn.defvjp(_nn_fwd, _nn_bwd)
mm_nt.defvjp(_nt_fwd, _nt_bwd)
mm_tn.defvjp(_tn_fwd, _tn_bwd)


def _rms(x, g):
    return x * lax.rsqrt(jnp.mean(x * x, axis=-1, keepdims=True) + EPS) * g


def _params(sem=None, vmem=VMEM_LIMIT):
    return pltpu.CompilerParams(dimension_semantics=sem, vmem_limit_bytes=vmem)


def _rowcall(name, fn, rows, fulls, row_out, acc_out=(), tm=256):
    s = rows[0].shape[0]
    tm = min(tm, s)
    nt = s // tm
    n_r, n_f, n_ro, n_ao = len(rows), len(fulls), len(row_out), len(acc_out)

    def body(*refs):
        ins = [r[...] for r in refs[: n_r + n_f]]
        outs = fn(*ins)
        o_refs = refs[n_r + n_f:]
        for o_ref, val in zip(o_refs[:n_ro], outs[:n_ro]):
            o_ref[...] = val.astype(o_ref.dtype)
        if n_ao:
            first = pl.program_id(0) == 0

            @pl.when(first)
            def _():
                for o_ref, val in zip(o_refs[n_ro:], outs[n_ro:]):
                    o_ref[...] = val.astype(o_ref.dtype)

            @pl.when(jnp.logical_not(first))
            def _():
                for o_ref, val in zip(o_refs[n_ro:], outs[n_ro:]):
                    o_ref[...] += val.astype(o_ref.dtype)

    in_specs = [pl.BlockSpec((tm, a.shape[1]), lambda i: (i, 0)) for a in rows]
    in_specs += [pl.BlockSpec(a.shape, lambda i: (0, 0), pipeline_mode=pl.Buffered(1)) for a in fulls]
    out_specs = [pl.BlockSpec((tm, c), lambda i: (i, 0)) for c, _ in row_out]
    out_specs += [pl.BlockSpec(shape, lambda i: (0, 0)) for shape, _ in acc_out]
    out_shape = [jax.ShapeDtypeStruct((s, c), dt) for c, dt in row_out]
    out_shape += [jax.ShapeDtypeStruct(shape, dt) for shape, dt in acc_out]
    return pl.pallas_call(
        body, name=name, grid=(nt,), in_specs=in_specs, out_specs=out_specs, out_shape=out_shape,
        compiler_params=_params(("arbitrary",)),
    )(*rows, *fulls)


def _mm_tn_call(name, a, b, tm, tn, tk):
    s, m = a.shape
    n = b.shape[1]
    tk = min(tk, s)

    def body(a_ref, b_ref, o_ref):
        d = _dg(a_ref[...], b_ref[...], _TN)
        first = pl.program_id(2) == 0

        @pl.when(first)
        def _():
            o_ref[...] = d

        @pl.when(jnp.logical_not(first))
        def _():
            o_ref[...] += d

    return pl.pallas_call(
        body, name=name, grid=(m // tm, n // tn, s // tk),
        in_specs=[pl.BlockSpec((tk, tm), lambda i, j, k: (k, i)), pl.BlockSpec((tk, tn), lambda i, j, k: (k, j))],
        out_specs=pl.BlockSpec((tm, tn), lambda i, j, k: (i, j)),
        out_shape=jax.ShapeDtypeStruct((m, n), F32),
        compiler_params=_params(("parallel", "parallel", "arbitrary")),
    )(a, b)


def _proj_tile(h, g, w):
    p = mm_nn(_rms(h, g), w)
    return (p[:, :SSD_DIM], p[:, SSD_DIM:Q_OFF], p[:, Q_OFF:Q_OFF + SB_DIM],
            p[:, Q_OFF + SB_DIM:Q_OFF + 2 * SB_DIM], p[:, Q_OFF + 2 * SB_DIM:DT_OFF], p[:, DT_OFF:])


def _proj_fwd(tag, h, g, w):
    return _rowcall(
        "proj_fwd" + tag, _proj_tile, [h], [g, w],
        [(SSD_DIM, F32), (CONV_DIM, F32), (SB_DIM, BF16), (SB_DIM, BF16), (SB_DIM, BF16), (DT_PAD, F32)], tm=512)


def _proj_bwd(tag, h, dh_out, dz, dxbc, dq, dk, dv, ddt, g, w):
    def fn(h, dh_out, dz, dxbc, dq, dk, dv, ddt, g, w):
        dp = jnp.concatenate([dz.astype(BF16), dxbc.astype(BF16), dq.astype(BF16), dk.astype(BF16),
                              dv.astype(BF16), ddt.astype(BF16)], axis=1)
        hn, vjp = jax.vjp(_rms, h, g)
        dh, dg = vjp(mm_nt(dp, w))
        return dh_out + dh, hn, dp, dg

    return _rowcall(
        "proj_bwd" + tag, fn, [h, dh_out, dz, dxbc, dq, dk, dv, ddt], [g, w],
        [(D_MODEL, F32), (D_MODEL, BF16), (IN_PAD, BF16)], [((1, D_MODEL), F32)], tm=256)


def _shift_down(x, tail, j):
    if j == 0:
        return x
    r = pltpu.roll(x, j, 0)
    rt = pltpu.roll(tail, j, 0)
    row = lax.broadcasted_iota(jnp.int32, (HALO, x.shape[1]), 0)
    first = jnp.where(row < j, rt, r[:HALO])
    if x.shape[0] == HALO:
        return first
    return jnp.concatenate([first, r[HALO:]], axis=0)


def _shift_up(x, head, j):
    if j == 0:
        return x
    n = x.shape[0]
    r = pltpu.roll(x, n - j, 0)
    rh = pltpu.roll(head, HALO - j, 0)
    row = lax.broadcasted_iota(jnp.int32, (HALO, x.shape[1]), 0)
    return jnp.concatenate([r[:n - HALO], jnp.where(row >= HALO - j, rh, r[n - HALO:])], axis=0)


def _conv_pre(x, tail, w, b):
    acc = b + w[CONV_K - 1:CONV_K] * x
    for j in range(1, CONV_K):
        acc = acc + w[CONV_K - 1 - j:CONV_K - j] * _shift_down(x, tail, j)
    return acc


def _dsilu(p):
    s = jax.nn.sigmoid(p)
    return s * (1.0 + p * (1.0 - s))


def _conv_fwd(tag, xbc, w, b, tc=512):
    s, c = xbc.shape
    tc = min(tc, s)
    per = tc // HALO

    def body(x_ref, prev_ref, w_ref, b_ref, o_ref):
        tail = jnp.where(pl.program_id(0) > 0, prev_ref[...], 0.0)
        o_ref[...] = jax.nn.silu(_conv_pre(x_ref[...], tail, w_ref[...], b_ref[...]))

    return pl.pallas_call(
        body, name="conv_fwd" + tag, grid=(s // tc,),
        in_specs=[pl.BlockSpec((tc, c), lambda i: (i, 0)),
                  pl.BlockSpec((HALO, c), lambda i: (jnp.maximum(i * per - 1, 0), 0)),
                  pl.BlockSpec((CONV_K, c), lambda i: (0, 0)), pl.BlockSpec((1, c), lambda i: (0, 0))],
        out_specs=pl.BlockSpec((tc, c), lambda i: (i, 0)),
        out_shape=jax.ShapeDtypeStruct((s, c), F32),
        compiler_params=_params(("arbitrary",)),
    )(xbc, xbc, w, b)


def _conv_bwd(tag, xbc, dact, w, b, tc=512):
    s, c = xbc.shape
    tc = min(tc, s)
    per = tc // HALO
    nt = s // tc
    last_blk = s // HALO - 1

    def body(x_ref, prev_ref, next_ref, d_ref, dnext_ref, w_ref, b_ref, dx_ref, dw_ref, db_ref):
        i = pl.program_id(0)
        x = x_ref[...]
        wv = w_ref[...]
        tail = jnp.where(i > 0, prev_ref[...], 0.0)
        dpre = d_ref[...] * _dsilu(_conv_pre(x, tail, wv, b_ref[...]))
        pre_n = _conv_pre(next_ref[...], x[tc - HALO:], wv, b_ref[...])
        dpre_n = jnp.where(i < nt - 1, dnext_ref[...] * _dsilu(pre_n), 0.0)
        dx = wv[CONV_K - 1:CONV_K] * dpre
        for j in range(1, CONV_K):
            dx = dx + wv[CONV_K - 1 - j:CONV_K - j] * _shift_up(dpre, dpre_n, j)
        dx_ref[...] = dx
        dws = [jnp.sum(dpre * _shift_down(x, tail, CONV_K - 1 - k), axis=0, keepdims=True) for k in range(CONV_K)]
        dwv = jnp.concatenate(dws, axis=0)
        dbv = jnp.sum(dpre, axis=0, keepdims=True)

        @pl.when(i == 0)
        def _():
            dw_ref[...] = dwv
            db_ref[...] = dbv

        @pl.when(i > 0)
        def _():
            dw_ref[...] += dwv
            db_ref[...] += dbv

    tile = pl.BlockSpec((tc, c), lambda i: (i, 0))
    prev = pl.BlockSpec((HALO, c), lambda i: (jnp.maximum(i * per - 1, 0), 0))
    nxt = pl.BlockSpec((HALO, c), lambda i: (jnp.minimum((i + 1) * per, last_blk), 0))
    return pl.pallas_call(
        body, name="conv_bwd" + tag, grid=(nt,),
        in_specs=[tile, prev, nxt, tile, nxt, pl.BlockSpec((CONV_K, c), lambda i: (0, 0)),
                  pl.BlockSpec((1, c), lambda i: (0, 0))],
        out_specs=[tile, pl.BlockSpec((CONV_K, c), lambda i: (0, 0)), pl.BlockSpec((1, c), lambda i: (0, 0))],
        out_shape=[jax.ShapeDtypeStruct((s, c), F32), jax.ShapeDtypeStruct((CONV_K, c), F32),
                   jax.ShapeDtypeStruct((1, c), F32)],
        compiler_params=_params(("arbitrary",)),
    )(xbc, xbc, xbc, dact, dact, w, b)


def _ssd_chunk(xs, bm, cm, dtr, z, dt_bias, a_log, d_skip, g, s_prev):
    n = CHUNK
    row = lax.broadcasted_iota(jnp.int32, (n, n), 0)
    col = lax.broadcasted_iota(jnp.int32, (n, n), 1)
    causal = row >= col
    dt = jax.nn.softplus(dtr + dt_bias)
    a_c = dt * (-jnp.exp(a_log))
    hi = lax.Precision.HIGHEST
    a_cum = jnp.dot(causal.astype(F32), a_c, precision=hi, preferred_element_type=F32)
    a_cum_t = lax.dot_general(a_c, (row <= col).astype(F32), (_TN, ((), ())), precision=hi,
                              preferred_element_type=F32)
    p, st = SSD_HEAD_DIM, SSD_STATE
    cb = [mm_nt(cm[:, k * st:(k + 1) * st], bm[:, k * st:(k + 1) * st]) for k in range(SSD_GROUPS)]
    ys, s_new = [], []
    for h in range(SSD_HEADS):
        k = h // (SSD_HEADS // SSD_GROUPS)
        acol = a_cum[:, h:h + 1]
        decay = jnp.exp(jnp.where(causal, acol - a_cum_t[h:h + 1, :], -jnp.inf))
        xh = xs[:, h * p:(h + 1) * p]
        xdt = xh * dt[:, h:h + 1]
        a_last = a_cum[n - 1:n, h:h + 1]
        sp = s_prev[h * p:(h + 1) * p, :]
        bg = bm[:, k * st:(k + 1) * st]
        cg = cm[:, k * st:(k + 1) * st]
        s_new.append(sp * jnp.exp(a_last) + mm_tn(xdt * jnp.exp(a_last - acol), bg))
        y = mm_nn(cb[k] * decay, xdt) + mm_nt(cg, sp) * jnp.exp(acol) + d_skip[:, h:h + 1] * xh
        ys.append(y)
    y = jnp.concatenate(ys, axis=1) * jax.nn.silu(z)
    return _rms(y, g), jnp.concatenate(s_new, axis=0)


def _split_xbc(t):
    return t[:, :SSD_DIM], t[:, SSD_DIM:SSD_DIM + GN], t[:, SSD_DIM + GN:]


def _ssd_fwd(tag, xact, dtr, z, dt_bias, a_log, d_skip, g):
    s = xact.shape[0]
    nc = s // CHUNK
    srows = SSD_HEADS * SSD_HEAD_DIM

    def body(x_ref, dt_ref, z_ref, b_ref, al_ref, ds_ref, g_ref, y_ref, st_ref, state):
        @pl.when(pl.program_id(0) == 0)
        def _():
            state[...] = jnp.zeros_like(state)

        sp = state[...]
        st_ref[0] = sp
        xs, bm, cm = _split_xbc(x_ref[...])
        y, sn = _ssd_chunk(xs, bm, cm, dt_ref[...][:, :SSD_HEADS], z_ref[...], b_ref[...], al_ref[...],
                           ds_ref[...], g_ref[...], sp)
        y_ref[...] = y
        state[...] = sn

    small = pl.BlockSpec((1, SSD_HEADS), lambda i: (0, 0))
    return pl.pallas_call(
        body, name="ssd_fwd" + tag, grid=(nc,),
        in_specs=[pl.BlockSpec((CHUNK, CONV_DIM), lambda i: (i, 0)), pl.BlockSpec((CHUNK, DT_PAD), lambda i: (i, 0)),
                  pl.BlockSpec((CHUNK, SSD_DIM), lambda i: (i, 0)), small, small, small,
                  pl.BlockSpec((1, SSD_DIM), lambda i: (0, 0))],
        out_specs=[pl.BlockSpec((CHUNK, SSD_DIM), lambda i: (i, 0)),
                   pl.BlockSpec((1, srows, SSD_STATE), lambda i: (i, 0, 0))],
        out_shape=[jax.ShapeDtypeStruct((s, SSD_DIM), F32), jax.ShapeDtypeStruct((nc, srows, SSD_STATE), F32)],
        scratch_shapes=[pltpu.VMEM((srows, SSD_STATE), F32)],
        compiler_params=_params(("arbitrary",)),
    )(xact, dtr, z, dt_bias, a_log, d_skip, g)


def _ssd_bwd(tag, xact, dtr, z, states, dy, dt_bias, a_log, d_skip, g):
    s = xact.shape[0]
    nc = s // CHUNK
    srows = SSD_HEADS * SSD_HEAD_DIM

    def body(x_ref, dt_ref, z_ref, sp_ref, dy_ref, b_ref, al_ref, ds_ref, g_ref,
             dx_ref, ddt_ref, dz_ref, db_ref, dal_ref, dds_ref, dg_ref, dstate):
        first = pl.program_id(0) == 0

        @pl.when(first)
        def _():
            dstate[...] = jnp.zeros_like(dstate)

        xs, bm, cm = _split_xbc(x_ref[...])
        _, vjp = jax.vjp(_ssd_chunk, xs, bm, cm, dt_ref[...][:, :SSD_HEADS], z_ref[...], b_ref[...], al_ref[...],
                         ds_ref[...], g_ref[...], sp_ref[0])
        dxs, dbm, dcm, ddt, dz, db, dal, dds, dg, dsp = vjp((dy_ref[...], dstate[...]))
        dx_ref[...] = jnp.concatenate([dxs, dbm, dcm], axis=1)
        ddt_ref[...] = jnp.concatenate([ddt, jnp.zeros((CHUNK, DT_PAD - SSD_HEADS), F32)], axis=1)
        dz_ref[...] = dz
        dstate[...] = dsp

        @pl.when(first)
        def _():
            db_ref[...] = db
            dal_ref[...] = dal
            dds_ref[...] = dds
            dg_ref[...] = dg

        @pl.when(jnp.logical_not(first))
        def _():
            db_ref[...] += db
            dal_ref[...] += dal
            dds_ref[...] += dds
            dg_ref[...] += dg

    def rev(c):
        return lambda i: (nc - 1 - i, 0)

    small = pl.BlockSpec((1, SSD_HEADS), lambda i: (0, 0))
    gspec = pl.BlockSpec((1, SSD_DIM), lambda i: (0, 0))
    return pl.pallas_call(
        body, name="ssd_bwd" + tag, grid=(nc,),
        in_specs=[pl.BlockSpec((CHUNK, CONV_DIM), rev(0)), pl.BlockSpec((CHUNK, DT_PAD), rev(0)),
                  pl.BlockSpec((CHUNK, SSD_DIM), rev(0)),
                  pl.BlockSpec((1, srows, SSD_STATE), lambda i: (nc - 1 - i, 0, 0)),
                  pl.BlockSpec((CHUNK, SSD_DIM), rev(0)), small, small, small, gspec],
        out_specs=[pl.BlockSpec((CHUNK, CONV_DIM), rev(0)), pl.BlockSpec((CHUNK, DT_PAD), rev(0)),
                   pl.BlockSpec((CHUNK, SSD_DIM), rev(0)), small, small, small, gspec],
        out_shape=[jax.ShapeDtypeStruct((s, CONV_DIM), F32), jax.ShapeDtypeStruct((s, DT_PAD), F32),
                   jax.ShapeDtypeStruct((s, SSD_DIM), F32), jax.ShapeDtypeStruct((1, SSD_HEADS), F32),
                   jax.ShapeDtypeStruct((1, SSD_HEADS), F32), jax.ShapeDtypeStruct((1, SSD_HEADS), F32),
                   jax.ShapeDtypeStruct((1, SSD_DIM), F32)],
        scratch_shapes=[pltpu.VMEM((srows, SSD_STATE), F32)],
        compiler_params=_params(("arbitrary",)),
    )(xact, dtr, z, states, dy, dt_bias, a_log, d_skip, g)


TQ = 128
TK = 128
SB_SCALE = 1.0 / math.sqrt(SB_HEAD_DIM)


def _split2(x):
    hi = x.astype(BF16)
    return hi, (x - hi.astype(F32)).astype(BF16)


TK_WIDE = 256
SB_UNDERFLOW = -110.0


def _sb_logits(qh, kb, t0, s0, masked):
    z = _dg(qh, kb, _NT)
    lb = jnp.minimum(z, 0.0) - jnp.log(1.0 + jnp.exp(-jnp.abs(z)))
    ls = lb - z
    if not masked:
        return lb, ls, None
    t_pos = t0 + lax.broadcasted_iota(jnp.int32, z.shape, 0)
    s_pos = s0 + lax.broadcasted_iota(jnp.int32, z.shape, 1)
    mask = s_pos < t_pos
    return lb, jnp.where(mask, ls, 0.0), mask


def _running_sums(x, start, u, reverse, two_terms=True):
    nsub = x.shape[1] // TK
    run = start
    parts = [None] * nsub
    for c in (reversed(range(nsub)) if reverse else range(nsub)):
        xc = x[:, c * TK:(c + 1) * TK]
        parts[c] = run + (_lane_sums(xc, u) if two_terms else _dg(xc, u, _NN))
        run = run + jnp.sum(xc, axis=1, keepdims=True)
    return (parts[0] if nsub == 1 else jnp.concatenate(parts, axis=1)), run


def _lane_sums(x, u):
    hi, lo = _split2(x)
    return _dg(hi, u, _NN) + _dg(lo, u, _NN)


def _tri(cmp):
    j = lax.broadcasted_iota(jnp.int32, (TK, TK), 0)
    s = lax.broadcasted_iota(jnp.int32, (TK, TK), 1)
    return cmp(j, s).astype(BF16)


def _sb_fwd(tag, q, k, v):
    s = q.shape[0]
    npair = SB_DIM // LANES

    wide = min(TK_WIDE, s)
    per = wide // TQ

    def body(q_ref, k_ref, v_ref, o_ref, t_ref):
        tq = pl.program_id(1)
        diag = tq // per
        qp = q_ref[...]
        lane = lax.broadcasted_iota(jnp.int32, (1, LANES), 1)
        u_gt = _tri(lambda j, s: j > s)
        out = jnp.zeros((TQ, LANES), F32)
        tot = jnp.zeros((TQ, LANES), F32)
        for hh in range(LANES // SB_HEAD_DIM):
            hm = (lane // SB_HEAD_DIM) == hh
            qh = jnp.where(hm, qp, jnp.zeros_like(qp)) * SB_SCALE

            def block(wb, r, acc, masked):
                off = pl.multiple_of(wb * wide, wide)
                kb = k_ref[pl.ds(off, wide), :]
                vb = v_ref[pl.ds(off, wide), :]
                lb, ls, mask = _sb_logits(qh, kb, tq * TQ, wb * wide, masked)
                later, r = _running_sums(ls, r, u_gt, reverse=True)
                w = jnp.exp(lb + later)
                if masked:
                    w = jnp.where(mask, w, 0.0)
                return r, acc + _dg(w, vb, _NN)

            r, acc = block(diag, jnp.zeros((TQ, 1), F32), jnp.zeros((TQ, LANES), F32), True)

            def more(c):
                return jnp.logical_and(c[0] >= 0, jnp.max(c[1]) > SB_UNDERFLOW)

            def step(c):
                r2, acc2 = block(c[0], c[1], c[2], False)
                return c[0] - 1, r2, acc2

            wb, r, acc = lax.while_loop(more, step, (diag - 1, r, acc))
            first = (wb + 1).astype(F32)
            out = out + jnp.where(hm, acc, 0.0)
            tot = tot + jnp.where(hm, jnp.where(lane % SB_HEAD_DIM == 1, first, r), 0.0)
        o_ref[...] = out
        t_ref[...] = tot

    tile = pl.BlockSpec((TQ, LANES), lambda p, t: (t, p))
    full = pl.BlockSpec((s, LANES), lambda p, t: (0, p))
    return pl.pallas_call(
        body, name="sb_fwd" + tag, grid=(npair, s // TQ),
        in_specs=[tile, full, full], out_specs=[tile, tile],
        out_shape=[jax.ShapeDtypeStruct((s, SB_DIM), F32)] * 2,
        compiler_params=_params(("parallel", "arbitrary")),
    )(q, k, v)


def _sb_bwd(tag, q, k, v, tot, do):
    s = q.shape[0]
    npair = SB_DIM // LANES

    wide = min(TK_WIDE, s)
    per = wide // TQ

    def body(q_ref, k_ref, v_ref, t_ref, do_ref, dq_ref, dk_ref, dv_ref):
        tq = pl.program_id(1)
        diag = tq // per

        @pl.when(tq == 0)
        def _():
            dk_ref[...] = jnp.zeros_like(dk_ref)
            dv_ref[...] = jnp.zeros_like(dv_ref)

        qp = q_ref[...]
        dop = do_ref[...]
        totp = t_ref[...]
        lane = lax.broadcasted_iota(jnp.int32, (1, LANES), 1)
        u_le = _tri(lambda j, s: j <= s)
        u_lt = _tri(lambda j, s: j < s)
        dq = jnp.zeros((TQ, LANES), F32)
        for hh in range(LANES // SB_HEAD_DIM):
            hm = (lane // SB_HEAD_DIM) == hh
            qh = jnp.where(hm, qp, jnp.zeros_like(qp)) * SB_SCALE
            doh = jnp.where(hm, dop, 0.0).astype(BF16)
            total = jnp.sum(jnp.where(lane == hh * SB_HEAD_DIM, totp, 0.0), axis=1, keepdims=True)
            first = jnp.max(jnp.where(lane == hh * SB_HEAD_DIM + 1, totp, 0.0)).astype(jnp.int32)

            def block(wb, pre, gc, acc, masked):
                off = pl.multiple_of(wb * wide, wide)
                kb = k_ref[pl.ds(off, wide), :]
                vb = v_ref[pl.ds(off, wide), :]
                lb, ls, mask = _sb_logits(qh, kb, tq * TQ, wb * wide, masked)
                before, pre = _running_sums(ls, pre, u_le, reverse=False)
                w = jnp.exp(lb + (total - before))
                if masked:
                    w = jnp.where(mask, w, 0.0)
                g = w * _dg(doh, vb, _NT)
                g_left, gc = _running_sums(g, gc, u_lt, reverse=False, two_terms=False)
                dz = g - jnp.exp(lb) * (g + g_left)
                if masked:
                    dz = jnp.where(mask, dz, 0.0)
                dzb = dz.astype(BF16)
                dk_ref[pl.ds(off, wide), :] += _dg(dzb, qh, _TN)
                dv_ref[pl.ds(off, wide), :] += _dg(w, doh, _TN)
                return pre, gc, acc + _dg(dzb, kb, _NN)

            zero = jnp.zeros((TQ, 1), F32)
            carry = lax.fori_loop(first, diag, lambda j, c: block(j, c[0], c[1], c[2], False),
                                  (zero, zero, jnp.zeros((TQ, LANES), F32)))
            _, _, acc = block(diag, carry[0], carry[1], carry[2], True)
            dq = dq + jnp.where(hm, acc, 0.0)
        dq_ref[...] = dq * SB_SCALE

    tile = pl.BlockSpec((TQ, LANES), lambda p, t: (t, p))
    full = pl.BlockSpec((s, LANES), lambda p, t: (0, p))
    return pl.pallas_call(
        body, name="sb_bwd" + tag, grid=(npair, s // TQ),
        in_specs=[tile, full, full, tile, tile],
        out_specs=[tile, full, full],
        out_shape=[jax.ShapeDtypeStruct((s, SB_DIM), F32)] * 3,
        compiler_params=_params(("parallel", "arbitrary")),
    )(q, k, v, tot, do)


def _out_tile(y_ssd, o, sb_g, w_out):
    y_all = jnp.concatenate([y_ssd, _rms(o, sb_g)], axis=1)
    return mm_nn(y_all, w_out)


def _out_fwd(tag, h, y_ssd, o, sb_g, w_out):
    return _rowcall("out_fwd" + tag, lambda h, y, o, g, w: (h + _out_tile(y, o, g, w),),
                    [h, y_ssd, o], [sb_g, w_out], [(D_MODEL, F32)], tm=512)[0]


def _out_bwd(tag, y_ssd, o, dh, sb_g, w_out):
    def fn(y, o, dh, g, w):
        _, vjp = jax.vjp(_out_tile, y, o, g, w.astype(F32))
        return vjp(dh)

    return _rowcall("out_bwd" + tag, fn, [y_ssd, o, dh], [sb_g, w_out], [(SSD_DIM, F32), (SB_DIM, F32)],
                    [((1, SB_DIM), F32), ((D_MODEL, D_MODEL), F32)], tm=256)


def _mem_tile(mem, g, w_k, w_v):
    m = _rms(mem, g)
    return mm_nn(m, w_k), mm_nn(m, w_v)


def _mem_fwd(tag, mem, g, w_k, w_v):
    return _rowcall("mem_fwd" + tag, _mem_tile, [mem], [g, w_k, w_v], [(XA_DIM, F32), (XA_DIM, F32)], tm=256)


def _mem_bwd(tag, mem, dkx, dvx, g, w_k, w_v):
    def fn(mem, dkx, dvx, g, w_k, w_v):
        _, vjp = jax.vjp(lambda g, a, b: _mem_tile(mem, g, a, b), g, w_k.astype(F32), w_v.astype(F32))
        return vjp((dkx, dvx))

    return _rowcall("mem_bwd" + tag, fn, [mem, dkx, dvx], [g, w_k, w_v], [],
                    [((1, D_MODEL), F32), ((D_MODEL, XA_DIM), F32), ((D_MODEL, XA_DIM), F32)], tm=256)


def _xattn_tile(h, g, w_q, kx, vx, w_o):
    q = mm_nn(_rms(h, g), w_q)
    scale = 1.0 / math.sqrt(XA_HEAD_DIM)
    outs = []
    for i in range(XA_HEADS):
        sl = slice(i * XA_HEAD_DIM, (i + 1) * XA_HEAD_DIM)
        p = jax.nn.softmax(mm_nt(q[:, sl], kx[:, sl]) * scale, axis=-1)
        outs.append(mm_nn(p, vx[:, sl]))
    return mm_nn(jnp.concatenate(outs, axis=1), w_o)


def _xattn_fwd(tag, h, g, w_q, kx, vx, w_o):
    return _rowcall("xattn_fwd" + tag, lambda h, g, wq, kx, vx, wo: (h + _xattn_tile(h, g, wq, kx, vx, wo),),
                    [h], [g, w_q, kx, vx, w_o], [(D_MODEL, F32)], tm=512)[0]


def _xattn_bwd(tag, h, dh_out, g, w_q, kx, vx, w_o):
    def fn(h, dh_out, g, w_q, kx, vx, w_o):
        _, vjp = jax.vjp(_xattn_tile, h, g, w_q.astype(F32), kx, vx, w_o.astype(F32))
        dh, dg, dwq, dkx, dvx, dwo = vjp(dh_out)
        return dh_out + dh, dg, dwq, dkx, dvx, dwo

    mlen = kx.shape[0]
    return _rowcall("xattn_bwd" + tag, fn, [h, dh_out], [g, w_q, kx, vx, w_o], [(D_MODEL, F32)],
                    [((1, D_MODEL), F32), ((D_MODEL, XA_DIM), F32), ((mlen, XA_DIM), F32), ((mlen, XA_DIM), F32),
                     ((XA_DIM, D_MODEL), F32)], tm=256)


def _mlp_fwd(tag, h, g, w1, w2):
    def fn(h, g, w1, w2):
        u = jnp.square(jnp.maximum(mm_nn(_rms(h, g), w1), 0.0))
        return (h + mm_nn(u, w2),)

    return _rowcall("mlp_fwd" + tag, fn, [h], [g, w1, w2], [(D_MODEL, F32)], tm=256)[0]


def _mlp_bwd(tag, h, dh_out, g, w1, w2):
    def fn(h, dh_out, g, w1, w2):
        hn, vjp = jax.vjp(_rms, h, g)
        r = jnp.maximum(mm_nn(hn, w1), 0.0)
        dob = dh_out.astype(BF16)
        dp = mm_nt(dob, w2) * (2.0 * r)
        dh, dg = vjp(mm_nt(dp, w1))
        return dh_out + dh, hn, dp, r * r, dob, dg

    return _rowcall("mlp_bwd" + tag, fn, [h, dh_out], [g, w1, w2],
                    [(D_MODEL, F32), (D_MODEL, BF16), (D_FF, BF16), (D_FF, BF16), (D_MODEL, BF16)],
                    [((1, D_MODEL), F32)], tm=256)


def _head(h, g, target):
    def lossfn(h, g, t):
        err = jnp.square(_rms(h, g) - t)
        return 0.5 * jnp.sum(jnp.mean(err, axis=-1))

    def fn(h, t, g):
        loss, vjp = jax.vjp(lambda h, g: lossfn(h, g, t), h, g)
        dh, dg = vjp(jnp.ones((), F32))
        return dh, jnp.full((1, LANES), loss, F32), dg

    return _rowcall("head", fn, [h, target], [g], [(D_MODEL, F32)], [((1, LANES), F32), ((1, D_MODEL), F32)], tm=512)


def _row(v):
    return v.reshape(1, -1)


def _local_step(x, mem, target, sw, mats):
    h = x
    saved = []
    for l in range(DEPTH):
        tag = str(l)
        m = mats[l]
        z, xbc, q, k, v, dtr = _proj_fwd(tag, h, _row(sw["norm_mix_g"][l]), m["w_in"])
        xact = _conv_fwd(tag, xbc, sw["conv_w"][l], _row(sw["conv_b"][l]))
        y_ssd, states = _ssd_fwd(tag, xact, dtr, z, _row(sw["dt_bias"][l]), _row(sw["a_log"][l]),
                                 _row(sw["d_skip"][l]), _row(sw["ssd_norm_g"][l]))
        o, sb_tot = _sb_fwd(tag, q, k, v)
        h1 = _out_fwd(tag, h, y_ssd, o, _row(sw["sb_norm_g"][l]), m["w_out"])
        kx, vx = _mem_fwd(tag, mem, _row(sw["norm_mem_g"][l]), m["w_xk"], m["w_xv"])
        h2 = _xattn_fwd(tag, h1, _row(sw["norm_xa_g"][l]), m["w_xq"], kx, vx, m["w_xo"])
        h3 = _mlp_fwd(tag, h2, _row(sw["norm_ff_g"][l]), m["w_ff1"], m["w_ff2"])
        saved.append((h, z, xbc, q, k, v, dtr, xact, y_ssd, states, o, sb_tot, h1, kx, vx, h2))
        h = h3

    dh, loss, d_final = _head(h, _row(sw["final_g"]), target)
    gm = [dict() for _ in range(DEPTH)]
    gs = {name: [None] * DEPTH for name in ("norm_mix_g", "conv_w", "conv_b", "dt_bias", "a_log", "d_skip",
                                            "ssd_norm_g", "sb_norm_g", "norm_xa_g", "norm_mem_g", "norm_ff_g")}
    for l in reversed(range(DEPTH)):
        tag = str(l)
        m = mats[l]
        h0, z, xbc, q, k, v, dtr, xact, y_ssd, states, o, sb_tot, h1, kx, vx, h2 = saved[l]
        dh2, hn_b, dp_b, a_b, do_b, gs["norm_ff_g"][l] = _mlp_bwd(
            tag, h2, dh, _row(sw["norm_ff_g"][l]), m["w_ff1"], m["w_ff2"])
        gm[l]["w_ff1"] = _mm_tn_call("dw_ff1" + tag, hn_b, dp_b, 512, 1024, 2048)
        gm[l]["w_ff2"] = _mm_tn_call("dw_ff2" + tag, a_b, do_b, 1024, 1024, 2048)
        dh1, gs["norm_xa_g"][l], gm[l]["w_xq"], dkx, dvx, gm[l]["w_xo"] = _xattn_bwd(
            tag, h1, dh2, _row(sw["norm_xa_g"][l]), m["w_xq"], kx, vx, m["w_xo"])
        gs["norm_mem_g"][l], gm[l]["w_xk"], gm[l]["w_xv"] = _mem_bwd(
            tag, mem, dkx, dvx, _row(sw["norm_mem_g"][l]), m["w_xk"], m["w_xv"])
        dy_ssd, do, gs["sb_norm_g"][l], gm[l]["w_out"] = _out_bwd(
            tag, y_ssd, o, dh1, _row(sw["sb_norm_g"][l]), m["w_out"])
        dq, dk, dv = _sb_bwd(tag, q, k, v, sb_tot, do)
        dxact, ddtr, dz, gs["dt_bias"][l], gs["a_log"][l], gs["d_skip"][l], gs["ssd_norm_g"][l] = _ssd_bwd(
            tag, xact, dtr, z, states, dy_ssd, _row(sw["dt_bias"][l]), _row(sw["a_log"][l]),
            _row(sw["d_skip"][l]), _row(sw["ssd_norm_g"][l]))
        dxbc, gs["conv_w"][l], gs["conv_b"][l] = _conv_bwd(tag, xbc, dxact, sw["conv_w"][l], _row(sw["conv_b"][l]))
        dh, hn_b, dp_b, gs["norm_mix_g"][l] = _proj_bwd(
            tag, h0, dh1, dz, dxbc, dq, dk, dv, ddtr, _row(sw["norm_mix_g"][l]), m["w_in"])
        gm[l]["w_in"] = _mm_tn_call("dw_in" + tag, hn_b, dp_b, 512, IN_PAD, 1024)
    gs["final_g"] = d_final
    return loss, dh, gm, gs


ANY = pl.BlockSpec(memory_space=pl.ANY)
VMEM_SPEC = pl.BlockSpec(memory_space=pltpu.VMEM)


def _place():
    return lax.axis_index("x"), lax.axis_index("y"), lax.axis_index("c")


def _other_chips(x, y):
    return [(1 - x, y), (x, 1 - y), (1 - x, 1 - y)]


def _gather_weights(pack):
    rows = pack.shape[0]
    half = rows // 2

    def body(p_ref, o_ref, send_sems, recv_sems):
        x, y, c = _place()
        me_chip = 2 * x + y
        sibling = (x, y, 1 - c)
        chips = _other_chips(x, y)

        def part(chip, hh):
            return o_ref.at[chip, pl.ds(pl.multiple_of(hh * half, 16), half), :]

        def copy(kk, src, dst, to):
            return pltpu.make_async_remote_copy(src_ref=src, dst_ref=dst, send_sem=send_sems.at[kk],
                                                recv_sem=recv_sems.at[kk], device_id=to, device_id_type=MESH)

        my_half = p_ref.at[pl.ds(pl.multiple_of(c * half, 16), half), :]
        first = [copy(kk, my_half, part(me_chip, c), (cx, cy, c)) for kk, (cx, cy) in enumerate(chips)]
        for cp in first:
            cp.start()
        passed = []
        for kk, (cx, cy) in enumerate(chips):
            got = part(2 * cx + cy, c)
            copy(kk, got, got, (x, y, c)).wait_recv()
            fwd = copy(3 + kk, got, got, sibling)
            fwd.start()
            passed.append(fwd)
        for kk, (cx, cy) in enumerate(chips):
            got = part(2 * cx + cy, 1 - c)
            copy(3 + kk, got, got, (x, y, c)).wait_recv()
        for cp in first + passed:
            cp.wait_send()

    return pl.pallas_call(
        body, name="gather_weights", in_specs=[ANY], out_specs=ANY,
        out_shape=jax.ShapeDtypeStruct((N_CHIPS, rows, PACK_COLS), pack.dtype),
        scratch_shapes=[pltpu.SemaphoreType.DMA((6,)), pltpu.SemaphoreType.DMA((6,))],
    )(pack)


def _gather_small(tag, buf):
    shape = buf.shape

    def body(b_ref, o_ref, sum_ref, send_sems, recv_sems, local_sem):
        x, y, c = _place()
        me = 4 * x + 2 * y + c
        mine = pltpu.make_async_copy(b_ref, o_ref.at[me], local_sem)
        mine.start()
        flips = [(dx, dy, dc) for dx in (0, 1) for dy in (0, 1) for dc in (0, 1) if (dx, dy, dc) != (0, 0, 0)]
        sends = []

        def peer(dx, dy, dc):
            return (1 - x if dx else x, 1 - y if dy else y, 1 - c if dc else c)

        for kk, flip in enumerate(flips):
            cp = pltpu.make_async_remote_copy(src_ref=b_ref, dst_ref=o_ref.at[me], send_sem=send_sems.at[kk],
                                              recv_sem=recv_sems.at[kk], device_id=peer(*flip), device_id_type=MESH)
            cp.start()
            sends.append(cp)
        for kk, flip in enumerate(flips):
            px, py, pc = peer(*flip)
            frm = 4 * px + 2 * py + pc
            pltpu.make_async_remote_copy(src_ref=b_ref, dst_ref=o_ref.at[frm], send_sem=send_sems.at[kk],
                                         recv_sem=recv_sems.at[kk], device_id=(x, y, c),
                                         device_id_type=MESH).wait_recv()
        for cp in sends:
            cp.wait_send()
        mine.wait()
        total = o_ref[0]
        for d in range(1, N_DEV):
            total = total + o_ref[d]
        sum_ref[...] = total

    return pl.pallas_call(
        body, name="gather_small" + tag, in_specs=[VMEM_SPEC], out_specs=[VMEM_SPEC, VMEM_SPEC],
        out_shape=[jax.ShapeDtypeStruct((N_DEV,) + shape, buf.dtype), jax.ShapeDtypeStruct(shape, buf.dtype)],
        scratch_shapes=[pltpu.SemaphoreType.DMA((N_DEV - 1,)), pltpu.SemaphoreType.DMA((N_DEV - 1,)),
                        pltpu.SemaphoreType.DMA],
    )(buf)


def _swap_halves(g):
    half = g.shape[1] // 2

    def body(g_ref, r_ref, send_sem, recv_sem):
        x, y, c = _place()
        src = g_ref.at[:, pl.ds(pl.multiple_of((1 - c) * half, 8), half), :]
        cp = pltpu.make_async_remote_copy(src_ref=src, dst_ref=r_ref, send_sem=send_sem, recv_sem=recv_sem,
                                          device_id=(x, y, 1 - c), device_id_type=MESH)
        cp.start()
        cp.wait()

    return pl.pallas_call(
        body, name="swap_halves", in_specs=[ANY], out_specs=ANY,
        out_shape=jax.ShapeDtypeStruct((N_CHIPS, half, PACK_COLS), g.dtype),
        scratch_shapes=[pltpu.SemaphoreType.DMA, pltpu.SemaphoreType.DMA],
    )(g)


def _add_halves(g, r, tr=208):
    half = r.shape[1]
    core = lax.axis_index("c").astype(jnp.int32).reshape(1)
    per = half // tr

    def body(c_ref, g_ref, r_ref, o_ref):
        o_ref[...] = (g_ref[...] + r_ref[...]).astype(o_ref.dtype)

    return pl.pallas_call(
        body, name="add_halves",
        grid_spec=pltpu.PrefetchScalarGridSpec(
            num_scalar_prefetch=1, grid=(N_CHIPS, per),
            in_specs=[pl.BlockSpec((1, tr, PACK_COLS), lambda p, i, c_ref: (p, c_ref[0] * per + i, 0)),
                      pl.BlockSpec((1, tr, PACK_COLS), lambda p, i, c_ref: (p, i, 0))],
            out_specs=pl.BlockSpec((1, tr, PACK_COLS), lambda p, i, c_ref: (p, i, 0))),
        out_shape=jax.ShapeDtypeStruct(r.shape, BF16),
        compiler_params=_params(("arbitrary", "arbitrary")),
    )(core, g, r)


def _scatter_chips(sb):
    def body(s_ref, o_ref, send_sems, recv_sems):
        x, y, c = _place()
        me_chip = 2 * x + y
        chips = _other_chips(x, y)
        sends = []
        for kk, (cx, cy) in enumerate(chips):
            cp = pltpu.make_async_remote_copy(src_ref=s_ref.at[2 * cx + cy], dst_ref=o_ref.at[me_chip],
                                              send_sem=send_sems.at[kk], recv_sem=recv_sems.at[kk],
                                              device_id=(cx, cy, c), device_id_type=MESH)
            cp.start()
            sends.append(cp)
        for kk, (cx, cy) in enumerate(chips):
            got = o_ref.at[2 * cx + cy]
            pltpu.make_async_remote_copy(src_ref=got, dst_ref=got, send_sem=send_sems.at[kk],
                                         recv_sem=recv_sems.at[kk], device_id=(x, y, c),
                                         device_id_type=MESH).wait_recv()
        for cp in sends:
            cp.wait_send()

    return pl.pallas_call(
        body, name="scatter_chips", in_specs=[ANY], out_specs=ANY,
        out_shape=jax.ShapeDtypeStruct(sb.shape, sb.dtype),
        scratch_shapes=[pltpu.SemaphoreType.DMA((3,)), pltpu.SemaphoreType.DMA((3,))],
    )(sb)


def _sum_parts(own, parts, tr=208):
    half = parts.shape[1]
    chip = (2 * lax.axis_index("x") + lax.axis_index("y")).astype(jnp.int32).reshape(1)

    def body(c_ref, own_ref, p1_ref, p2_ref, p3_ref, o_ref):
        total = own_ref[0].astype(F32)
        for p_ref in (p1_ref, p2_ref, p3_ref):
            total = total + p_ref[0].astype(F32)
        o_ref[...] = total

    def after(kk):
        return pl.BlockSpec((1, tr, PACK_COLS), lambda i, c_ref: ((c_ref[0] + kk) % N_CHIPS, i, 0))

    return pl.pallas_call(
        body, name="sum_parts",
        grid_spec=pltpu.PrefetchScalarGridSpec(
            num_scalar_prefetch=1, grid=(half // tr,), in_specs=[after(0), after(1), after(2), after(3)],
            out_specs=pl.BlockSpec((tr, PACK_COLS), lambda i, c_ref: (i, 0))),
        out_shape=jax.ShapeDtypeStruct((half, PACK_COLS), F32),
        compiler_params=_params(("arbitrary",)),
    )(chip, own, parts, parts, parts)


def _swap_reduced(red):
    def body(r_ref, o_ref, send_sem, recv_sem):
        x, y, c = _place()
        cp = pltpu.make_async_remote_copy(src_ref=r_ref, dst_ref=o_ref, send_sem=send_sem, recv_sem=recv_sem,
                                          device_id=(x, y, 1 - c), device_id_type=MESH)
        cp.start()
        cp.wait()

    return pl.pallas_call(
        body, name="swap_reduced", in_specs=[ANY], out_specs=ANY,
        out_shape=jax.ShapeDtypeStruct(red.shape, red.dtype),
        scratch_shapes=[pltpu.SemaphoreType.DMA, pltpu.SemaphoreType.DMA],
    )(red)


def _adamw_math(w, g, m, v):
    m = ADAM_B1 * m + (1.0 - ADAM_B1) * g
    v = ADAM_B2 * v + (1.0 - ADAM_B2) * jnp.square(g)
    m_hat = m / (1.0 - ADAM_B1 ** ADAM_STEP)
    v_hat = v / (1.0 - ADAM_B2 ** ADAM_STEP)
    delta = -ADAM_LR * (m_hat / (jnp.sqrt(v_hat) + ADAM_EPS) + ADAM_WD * w)
    return delta, m, v


def _adamw(tag, w, g, m, v, tr=256):
    rows, cols = w.shape
    tr = min(tr, rows)

    def body(w_ref, g_ref, m_ref, v_ref, d_ref, nm_ref, nv_ref):
        d_ref[...], nm_ref[...], nv_ref[...] = _adamw_math(w_ref[...], g_ref[...], m_ref[...], v_ref[...])

    spec = pl.BlockSpec((tr, cols), lambda i: (i, 0))
    return pl.pallas_call(
        body, name="adamw_" + tag, grid=(rows // tr,), in_specs=[spec] * 4, out_specs=[spec] * 3,
        out_shape=[jax.ShapeDtypeStruct(w.shape, F32)] * 3,
        compiler_params=_params(("parallel",)),
    )(w, g, m, v)


def _pad_rows(a, rows):
    return jnp.concatenate([a, jnp.zeros((rows - a.shape[0], a.shape[1]), a.dtype)], axis=0)


def _pack_shards(get):
    parts = [get(l, name).reshape(-1, PACK_COLS) for l in range(DEPTH) for name, _, _ in MATS]
    return _pad_rows(jnp.concatenate(parts, axis=0), PACK_ROWS)


def _unpack_shards(rows2d):
    out = {name: [] for name, _, _ in MATS}
    off = 0
    for l in range(DEPTH):
        for name, shape, axis in MATS:
            ss = _shard_shape(shape, axis)
            n = ss[0] * ss[1] // PACK_COLS
            out[name].append(rows2d[off:off + n].reshape(ss))
            off += n
    return {name: jnp.stack(v) for name, v in out.items()}


def _w_in_to_padded(w):
    d0 = SSD_DIM + CONV_DIM
    return jnp.concatenate([w[:, :d0], w[:, d0 + SSD_HEADS:], w[:, d0:d0 + SSD_HEADS],
                            jnp.zeros((w.shape[0], DT_PAD - SSD_HEADS), w.dtype)], axis=1)


def _w_in_from_padded(w):
    d0 = SSD_DIM + CONV_DIM
    return jnp.concatenate([w[:, :d0], w[:, DT_OFF:DT_OFF + SSD_HEADS], w[:, d0:DT_OFF]], axis=1)


def _small_layout():
    return (("norm_mix_g", 0, 0, D_MODEL), ("norm_xa_g", 1, 0, D_MODEL), ("norm_mem_g", 2, 0, D_MODEL),
            ("norm_ff_g", 3, 0, D_MODEL), ("conv_b", 4, 0, CONV_DIM), ("ssd_norm_g", 5, 0, SSD_DIM),
            ("sb_norm_g", 5, SSD_DIM, SB_DIM), ("dt_bias", 6, 0, SSD_HEADS), ("a_log", 6, LANES, SSD_HEADS),
            ("d_skip", 6, 2 * LANES, SSD_HEADS))


def _pack_small(vals, conv_w_full, extra=None):
    rows = []
    for l in range(DEPTH):
        lay = _small_layout()
        for r in range(_SM_PER_LAYER):
            cells = [(c0, vals[name][l].reshape(-1)) for name, rr, c0, _ in lay if rr == r]
            line, pos = [], 0
            for c0, val in cells:
                if c0 > pos:
                    line.append(jnp.zeros((c0 - pos,), F32))
                line.append(val)
                pos = c0 + val.shape[0]
            line.append(jnp.zeros((PACK_COLS - pos,), F32))
            rows.append(jnp.concatenate(line))
    rows.append(vals["final_g"].reshape(-1))
    rows.append(conv_w_full.reshape(-1, PACK_COLS))
    used = _SM_CONVW + DEPTH * CONV_K * CONV_DIM // PACK_COLS
    last = jnp.zeros((SMALL_ROWS - used, PACK_COLS), F32)
    if extra is not None:
        last = last.at[SMALL_ROWS - used - 1, 0].set(extra)
    return jnp.concatenate([r.reshape(-1, PACK_COLS) for r in rows] + [last], axis=0)


def _unpack_small(buf):
    out = {}
    for name, rr, c0, width in _small_layout():
        out[name] = jnp.stack([buf[l * _SM_PER_LAYER + rr, c0:c0 + width] for l in range(DEPTH)])
    out["final_g"] = buf[_SM_FINAL]
    n = DEPTH * CONV_K * CONV_DIM // PACK_COLS
    out["conv_w"] = buf[_SM_CONVW:_SM_CONVW + n].reshape(DEPTH, CONV_K, CONV_DIM)
    return out


CONV_BLOCK_ROWS = 8


def _conv_w_block(cw):
    flat = cw.reshape(-1)
    pad = jnp.zeros((CONV_BLOCK_ROWS * PACK_COLS - flat.shape[0],), F32)
    return jnp.concatenate([flat, pad]).reshape(CONV_BLOCK_ROWS, PACK_COLS)


def _conv_w_from_slots(slots):
    cols = CONV_DIM // N_CHIPS
    n = DEPTH * CONV_K * cols
    shards = [slots[2 * j].reshape(-1)[:n].reshape(DEPTH, CONV_K, cols) for j in range(N_CHIPS)]
    return jnp.concatenate(shards, axis=2)


def _conv_w_padded(cw):
    return jnp.concatenate([cw, jnp.zeros(cw.shape[:2] + (CONV_DIM - cw.shape[2],), F32)], axis=2)


SMALL_NAMES = ("norm_mix_g", "conv_b", "dt_bias", "a_log", "d_skip", "ssd_norm_g", "sb_norm_g", "norm_xa_g",
               "norm_mem_g", "norm_ff_g", "final_g")
WEIGHT_ORDER = ("norm_mix_g", "w_in", "conv_w", "conv_b", "dt_bias", "a_log", "d_skip", "ssd_norm_g", "sb_norm_g",
                "w_out", "norm_xa_g", "norm_mem_g", "w_xq", "w_xk", "w_xv", "w_xo", "norm_ff_g", "w_ff1", "w_ff2",
                "final_g")


def kernel(x, mem, norm_mix_g, w_in, conv_w, conv_b, dt_bias, a_log, d_skip, ssd_norm_g, sb_norm_g, w_out, norm_xa_g, norm_mem_g, w_xq, w_xk, w_xv, w_xo, norm_ff_g, w_ff1, w_ff2, final_g, loss_target, m_norm_mix_g, m_w_in, m_conv_w, m_conv_b, m_dt_bias, m_a_log, m_d_skip, m_ssd_norm_g, m_sb_norm_g, m_w_out, m_norm_xa_g, m_norm_mem_g, m_w_xq, m_w_xk, m_w_xv, m_w_xo, m_norm_ff_g, m_w_ff1, m_w_ff2, m_final_g, v_norm_mix_g, v_w_in, v_conv_w, v_conv_b, v_dt_bias, v_a_log, v_d_skip, v_ssd_norm_g, v_sb_norm_g, v_w_out, v_norm_xa_g, v_norm_mem_g, v_w_xq, v_w_xk, v_w_xv, v_w_xo, v_norm_ff_g, v_w_ff1, v_w_ff2, v_final_g):
    w = dict(norm_mix_g=norm_mix_g, w_in=w_in, conv_w=conv_w, conv_b=conv_b, dt_bias=dt_bias, a_log=a_log,
             d_skip=d_skip, ssd_norm_g=ssd_norm_g, sb_norm_g=sb_norm_g, w_out=w_out, norm_xa_g=norm_xa_g,
             norm_mem_g=norm_mem_g, w_xq=w_xq, w_xk=w_xk, w_xv=w_xv, w_xo=w_xo, norm_ff_g=norm_ff_g, w_ff1=w_ff1,
             w_ff2=w_ff2, final_g=final_g)
    mom = dict(norm_mix_g=m_norm_mix_g, w_in=m_w_in, conv_w=m_conv_w, conv_b=m_conv_b, dt_bias=m_dt_bias,
               a_log=m_a_log, d_skip=m_d_skip, ssd_norm_g=m_ssd_norm_g, sb_norm_g=m_sb_norm_g, w_out=m_w_out,
               norm_xa_g=m_norm_xa_g, norm_mem_g=m_norm_mem_g, w_xq=m_w_xq, w_xk=m_w_xk, w_xv=m_w_xv, w_xo=m_w_xo,
               norm_ff_g=m_norm_ff_g, w_ff1=m_w_ff1, w_ff2=m_w_ff2, final_g=m_final_g)
    var = dict(norm_mix_g=v_norm_mix_g, w_in=v_w_in, conv_w=v_conv_w, conv_b=v_conv_b, dt_bias=v_dt_bias,
               a_log=v_a_log, d_skip=v_d_skip, ssd_norm_g=v_ssd_norm_g, sb_norm_g=v_sb_norm_g, w_out=v_w_out,
               norm_xa_g=v_norm_xa_g, norm_mem_g=v_norm_mem_g, w_xq=v_w_xq, w_xk=v_w_xk, w_xv=v_w_xv, w_xo=v_w_xo,
               norm_ff_g=v_norm_ff_g, w_ff1=v_w_ff1, w_ff2=v_w_ff2, final_g=v_final_g)
    chip = 2 * lax.axis_index("x") + lax.axis_index("y")
    conv_cols = CONV_DIM // N_CHIPS

    my_pack = _pack_shards(lambda l, name: w[name][l]).astype(BF16)
    gathered = _gather_weights(my_pack)
    per_chip = [_unpack_shards(jnp.where(chip == j, my_pack, gathered[j])) for j in range(N_CHIPS)]
    mats = []
    for l in range(DEPTH):
        full = {name: jnp.concatenate([per_chip[j][name][l] for j in range(N_CHIPS)], axis=axis)
                for name, _, axis in MATS}
        full["w_in"] = _w_in_to_padded(full["w_in"])
        mats.append(full)
    conv_slots, _ = _gather_small("_conv", _conv_w_block(conv_w))
    conv_w_full = _conv_w_from_slots(conv_slots)

    sw = {name: w[name] for name in SMALL_NAMES}
    sw["conv_w"] = conv_w_full
    loss, grad_x, gm, gs = _local_step(x[0], mem[0], loss_target[0], sw, mats)

    small_vals = {name: jnp.concatenate(gs[name], axis=0) for name in gs if name not in ("conv_w", "final_g")}
    small_vals["final_g"] = gs["final_g"]
    small_buf = _pack_small(small_vals, jnp.stack(gs["conv_w"]), extra=loss[0, 0])
    _, small_sum = _gather_small("_grads", small_buf)
    g_small = _unpack_small(small_sum)
    loss_out = small_sum[_SM_LOSS, 0]

    def grad_part(j):
        def get(l, name):
            g = gm[l][name]
            if name == "w_in":
                g = _w_in_from_padded(g)
            axis = dict((n, a) for n, _, a in MATS)[name]
            size = g.shape[axis] // N_CHIPS
            return lax.slice_in_dim(g, j * size, (j + 1) * size, axis=axis)
        return _pack_shards(get)

    g_all = jnp.stack([grad_part(j) for j in range(N_CHIPS)])
    from_sibling = _swap_halves(g_all)
    to_chips = _add_halves(g_all, from_sibling)
    mine = _sum_parts(to_chips, _scatter_chips(to_chips))
    other = _swap_reduced(mine)
    south = lax.axis_index("c") == 0
    reduced = jnp.concatenate([jnp.where(south, mine, other), jnp.where(south, other, mine)], axis=0)
    g_mats = _unpack_shards(reduced)

    grads, deltas, new_m, new_v = {}, {}, {}, {}
    for name, _, _ in MATS:
        g = g_mats[name]
        cols = g.shape[-1]
        d, nm, nv = _adamw(name, w[name].reshape(-1, cols), g.reshape(-1, cols), mom[name].reshape(-1, cols),
                           var[name].reshape(-1, cols))
        grads[name] = g
        deltas[name], new_m[name], new_v[name] = (t.reshape(g.shape) for t in (d, nm, nv))
    g_conv_w = lax.dynamic_slice_in_dim(g_small["conv_w"], chip * conv_cols, conv_cols, axis=2)
    w_small = _pack_small({n: w[n] for n in SMALL_NAMES}, _conv_w_padded(conv_w))
    m_small = _pack_small({n: mom[n] for n in SMALL_NAMES}, _conv_w_padded(mom["conv_w"]))
    v_small = _pack_small({n: var[n] for n in SMALL_NAMES}, _conv_w_padded(var["conv_w"]))
    g_small_local = _pack_small({n: g_small[n] for n in SMALL_NAMES}, _conv_w_padded(g_conv_w))
    d_s, m_s, v_s = (_unpack_small(t) for t in _adamw("small", w_small, g_small_local, m_small, v_small))
    for name in SMALL_NAMES:
        grads[name] = g_small[name]
        deltas[name], new_m[name], new_v[name] = d_s[name], m_s[name], v_s[name]
    grads["conv_w"] = g_conv_w
    deltas["conv_w"], new_m["conv_w"], new_v["conv_w"] = (t["conv_w"][:, :, :conv_cols] for t in (d_s, m_s, v_s))

    return (loss_out, grad_x[None], *[grads[n] for n in WEIGHT_ORDER], *[deltas[n] for n in WEIGHT_ORDER],
            *[new_m[n] for n in WEIGHT_ORDER], *[new_v[n] for n in WEIGHT_ORDER])
```

```python
import functools
import math

import jax
import jax.numpy as jnp
from jax import lax
from jax.experimental import pallas as pl
from jax.experimental.pallas import tpu as pltpu

F32 = jnp.float32
BF16 = jnp.bfloat16
MESH = pl.DeviceIdType.MESH

D_MODEL = 1024
DEPTH = 2
SSD_DIM = 512
SSD_HEAD_DIM = 64
SSD_HEADS = 8
SSD_GROUPS = 2
SSD_STATE = 64
CONV_K = 4
CHUNK = 128
SB_DIM = 512
SB_HEAD_DIM = 64
XA_HEADS = 4
XA_HEAD_DIM = 128
XA_DIM = 512
D_FF = 4096
EPS = 1e-5
GN = SSD_GROUPS * SSD_STATE
CONV_DIM = SSD_DIM + 2 * GN
IN_DIM = SSD_DIM + CONV_DIM + SSD_HEADS + 3 * SB_DIM
LANES = 128
DT_PAD = LANES
IN_PAD = SSD_DIM + CONV_DIM + 3 * SB_DIM + DT_PAD
Q_OFF = SSD_DIM + CONV_DIM
DT_OFF = Q_OFF + 3 * SB_DIM
HALO = 8

ADAM_LR = 0.001
ADAM_B1 = 0.9
ADAM_B2 = 0.999
ADAM_EPS = 1e-08
ADAM_WD = 0.01
ADAM_STEP = 10

N_CHIPS = 4
N_DEV = 8
PACK_COLS = 1024
VMEM_LIMIT = 56 * 1024 * 1024

MATS = (
    ("w_in", (D_MODEL, IN_DIM), 1),
    ("w_out", (D_MODEL, D_MODEL), 0),
    ("w_xq", (D_MODEL, XA_DIM), 0),
    ("w_xk", (D_MODEL, XA_DIM), 0),
    ("w_xv", (D_MODEL, XA_DIM), 0),
    ("w_xo", (XA_DIM, D_MODEL), 1),
    ("w_ff1", (D_MODEL, D_FF), 1),
    ("w_ff2", (D_FF, D_MODEL), 0),
)


SMALL_ROWS = 24
_SM_PER_LAYER = 7
_SM_FINAL = 14
_SM_CONVW = 15
_SM_LOSS = 23


_NN = ((1,), (0,))
_NT = ((1,), (1,))
_TN = ((0,), (0,))


def _dg(a, b, dims):
    return lax.dot_general(a.astype(BF16), b.astype(BF16), (dims, ((), ())), preferred_element_type=F32)


@jax.custom_vjp
def mm_nn(a, b):
    return _dg(a, b, _NN)


@jax.custom_vjp
def mm_nt(a, b):
    return _dg(a, b, _NT)


@jax.custom_vjp
def mm_tn(a, b):
    return _dg(a, b, _TN)


def _nn_fwd(a, b):
    return _dg(a, b, _NN), (a, b)


def _nn_bwd(res, g):
    a, b = res
    return mm_nt(g, b).astype(a.dtype), mm_tn(a, g).astype(b.dtype)


def _nt_fwd(a, b):
    return _dg(a, b, _NT), (a, b)


def _nt_bwd(res, g):
    a, b = res
    return mm_nn(g, b).astype(a.dtype), mm_tn(g, a).astype(b.dtype)


def _tn_fwd(a, b):
    return _dg(a, b, _TN), (a, b)


def _tn_bwd(res, g):
    a, b = res
    return mm_nt(b, g).astype(a.dtype), mm_nn(a, g).astype(b.dtype)


mm_nn.defvjp(_nn_fwd, _nn_bwd)
mm_nt.defvjp(_nt_fwd, _nt_bwd)
mm_tn.defvjp(_tn_fwd, _tn_bwd)


def _rms(x, g):
    return x * lax.rsqrt(jnp.mean(x * x, axis=-1, keepdims=True) + EPS) * g


def _params(sem=None, vmem=VMEM_LIMIT):
    return pltpu.CompilerParams(dimension_semantics=sem, vmem_limit_bytes=vmem)


def _rowcall(name, fn, rows, fulls, row_out, acc_out=(), tm=256):
    s = rows[0].shape[0]
    tm = min(tm, s)
    nt = s // tm
    n_r, n_f, n_ro, n_ao = len(rows), len(fulls), len(row_out), len(acc_out)

    def body(*refs):
        ins = [r[...] for r in refs[: n_r + n_f]]
        outs = fn(*ins)
        o_refs = refs[n_r + n_f:]
        for o_ref, val in zip(o_refs[:n_ro], outs[:n_ro]):
            o_ref[...] = val.astype(o_ref.dtype)
        if n_ao:
            first = pl.program_id(0) == 0

            @pl.when(first)
            def _():
                for o_ref, val in zip(o_refs[n_ro:], outs[n_ro:]):
                    o_ref[...] = val.astype(o_ref.dtype)

            @pl.when(jnp.logical_not(first))
            def _():
                for o_ref, val in zip(o_refs[n_ro:], outs[n_ro:]):
                    o_ref[...] += val.astype(o_ref.dtype)

    in_specs = [pl.BlockSpec((tm, a.shape[1]), lambda i: (i, 0)) for a in rows]
    in_specs += [pl.BlockSpec(a.shape, lambda i: (0, 0), pipeline_mode=pl.Buffered(1)) for a in fulls]
    out_specs = [pl.BlockSpec((tm, c), lambda i: (i, 0)) for c, _ in row_out]
    out_specs += [pl.BlockSpec(shape, lambda i: (0, 0)) for shape, _ in acc_out]
    out_shape = [jax.ShapeDtypeStruct((s, c), dt) for c, dt in row_out]
    out_shape += [jax.ShapeDtypeStruct(shape, dt) for shape, dt in acc_out]
    return pl.pallas_call(
        body, name=name, grid=(nt,), in_specs=in_specs, out_specs=out_specs, out_shape=out_shape,
        compiler_params=_params(("arbitrary",)),
    )(*rows, *fulls)


def _mm_tn_call(name, a, b, tm, tn, tk):
    s, m = a.shape
    n = b.shape[1]
    tk = min(tk, s)

    def body(a_ref, b_ref, o_ref):
        d = _dg(a_ref[...], b_ref[...], _TN)
        first = pl.program_id(2) == 0

        @pl.when(first)
        def _():
            o_ref[...] = d

        @pl.when(jnp.logical_not(first))
        def _():
            o_ref[...] += d

    return pl.pallas_call(
        body, name=name, grid=(m // tm, n // tn, s // tk),
        in_specs=[pl.BlockSpec((tk, tm), lambda i, j, k: (k, i)), pl.BlockSpec((tk, tn), lambda i, j, k: (k, j))],
        out_specs=pl.BlockSpec((tm, tn), lambda i, j, k: (i, j)),
        out_shape=jax.ShapeDtypeStruct((m, n), F32),
        compiler_params=_params(("parallel", "parallel", "arbitrary")),
    )(a, b)


def _mm_tn_parts(name, a, b, layer, buf, by_cols, tm=512, tk=2048):
    s, m = a.shape
    n = b.shape[1]
    r, c = (m, n // N_CHIPS) if by_cols else (m // N_CHIPS, n)
    per = r // tm
    tk = min(tk, s)

    def body(*refs):
        a_ref, b_ref, o_ref = refs[0], refs[1], refs[-1]
        d = _dg(a_ref[...], b_ref[...], _TN)
        first = pl.program_id(2) == 0

        @pl.when(first)
        def _():
            o_ref[0, 0] = d

        @pl.when(jnp.logical_not(first))
        def _():
            o_ref[0, 0] += d

    if by_cols:
        out_map = lambda i, j, k: (layer, j, i, 0)
    else:
        out_map = lambda i, j, k: (layer, i // per, i % per, 0)
    in_specs = [pl.BlockSpec((tk, tm), lambda i, j, k: (k, i)), pl.BlockSpec((tk, c), lambda i, j, k: (k, j))]
    args = [a, b]
    aliases = {}
    if buf is not None:
        in_specs.append(pl.BlockSpec(memory_space=pl.ANY))
        args.append(buf)
        aliases = {2: 0}
    return pl.pallas_call(
        body, name=name, grid=(m // tm, n // c, s // tk), in_specs=in_specs,
        out_specs=pl.BlockSpec((1, 1, tm, c), out_map),
        out_shape=jax.ShapeDtypeStruct((DEPTH, N_CHIPS, r, c), F32), input_output_aliases=aliases,
        compiler_params=_params(("parallel", "parallel", "arbitrary")),
    )(*args)


def _proj_tile(h, g, w):
    p = mm_nn(_rms(h, g), w)
    return (p[:, :SSD_DIM], p[:, SSD_DIM:Q_OFF], p[:, Q_OFF:Q_OFF + SB_DIM],
            p[:, Q_OFF + SB_DIM:Q_OFF + 2 * SB_DIM], p[:, Q_OFF + 2 * SB_DIM:DT_OFF], p[:, DT_OFF:])


def _proj_fwd(tag, h, g, w):
    return _rowcall(
        "proj_fwd" + tag, _proj_tile, [h], [g, w],
        [(SSD_DIM, F32), (CONV_DIM, F32), (SB_DIM, BF16), (SB_DIM, BF16), (SB_DIM, BF16), (DT_PAD, F32)], tm=512)


def _proj_bwd(tag, h, dh_out, dz, dxbc, dq, dk, dv, ddt, g, w):
    def fn(h, dh_out, dz, dxbc, dq, dk, dv, ddt, g, w):
        dp = jnp.concatenate([dz.astype(BF16), dxbc.astype(BF16), dq.astype(BF16), dk.astype(BF16),
                              dv.astype(BF16), ddt.astype(BF16)], axis=1)
        hn, vjp = jax.vjp(_rms, h, g)
        dh, dg = vjp(mm_nt(dp, w))
        return dh_out + dh, hn, dp, dg

    return _rowcall(
        "proj_bwd" + tag, fn, [h, dh_out, dz, dxbc, dq, dk, dv, ddt], [g, w],
        [(D_MODEL, F32), (D_MODEL, BF16), (IN_PAD, BF16)], [((1, D_MODEL), F32)], tm=256)


def _shift_down(x, tail, j):
    if j == 0:
        return x
    r = pltpu.roll(x, j, 0)
    rt = pltpu.roll(tail, j, 0)
    row = lax.broadcasted_iota(jnp.int32, (HALO, x.shape[1]), 0)
    first = jnp.where(row < j, rt, r[:HALO])
    if x.shape[0] == HALO:
        return first
    return jnp.concatenate([first, r[HALO:]], axis=0)


def _shift_up(x, head, j):
    if j == 0:
        return x
    n = x.shape[0]
    r = pltpu.roll(x, n - j, 0)
    rh = pltpu.roll(head, HALO - j, 0)
    row = lax.broadcasted_iota(jnp.int32, (HALO, x.shape[1]), 0)
    return jnp.concatenate([r[:n - HALO], jnp.where(row >= HALO - j, rh, r[n - HALO:])], axis=0)


def _conv_pre(x, tail, w, b):
    acc = b + w[CONV_K - 1:CONV_K] * x
    for j in range(1, CONV_K):
        acc = acc + w[CONV_K - 1 - j:CONV_K - j] * _shift_down(x, tail, j)
    return acc


def _dsilu(p):
    s = jax.nn.sigmoid(p)
    return s * (1.0 + p * (1.0 - s))


def _conv_fwd(tag, xbc, w, b, tc=512):
    s, c = xbc.shape
    tc = min(tc, s)
    per = tc // HALO

    def body(x_ref, prev_ref, w_ref, b_ref, o_ref):
        tail = jnp.where(pl.program_id(0) > 0, prev_ref[...], 0.0)
        o_ref[...] = jax.nn.silu(_conv_pre(x_ref[...], tail, w_ref[...], b_ref[...]))

    return pl.pallas_call(
        body, name="conv_fwd" + tag, grid=(s // tc,),
        in_specs=[pl.BlockSpec((tc, c), lambda i: (i, 0)),
                  pl.BlockSpec((HALO, c), lambda i: (jnp.maximum(i * per - 1, 0), 0)),
                  pl.BlockSpec((CONV_K, c), lambda i: (0, 0)), pl.BlockSpec((1, c), lambda i: (0, 0))],
        out_specs=pl.BlockSpec((tc, c), lambda i: (i, 0)),
        out_shape=jax.ShapeDtypeStruct((s, c), F32),
        compiler_params=_params(("arbitrary",)),
    )(xbc, xbc, w, b)


def _conv_bwd(tag, xbc, dact, w, b, tc=512):
    s, c = xbc.shape
    tc = min(tc, s)
    per = tc // HALO
    nt = s // tc
    last_blk = s // HALO - 1

    def body(x_ref, prev_ref, next_ref, d_ref, dnext_ref, w_ref, b_ref, dx_ref, dw_ref, db_ref):
        i = pl.program_id(0)
        x = x_ref[...]
        wv = w_ref[...]
        tail = jnp.where(i > 0, prev_ref[...], 0.0)
        dpre = d_ref[...] * _dsilu(_conv_pre(x, tail, wv, b_ref[...]))
        pre_n = _conv_pre(next_ref[...], x[tc - HALO:], wv, b_ref[...])
        dpre_n = jnp.where(i < nt - 1, dnext_ref[...] * _dsilu(pre_n), 0.0)
        dx = wv[CONV_K - 1:CONV_K] * dpre
        for j in range(1, CONV_K):
            dx = dx + wv[CONV_K - 1 - j:CONV_K - j] * _shift_up(dpre, dpre_n, j)
        dx_ref[...] = dx
        dws = [jnp.sum(dpre * _shift_down(x, tail, CONV_K - 1 - k), axis=0, keepdims=True) for k in range(CONV_K)]
        dwv = jnp.concatenate(dws, axis=0)
        dbv = jnp.sum(dpre, axis=0, keepdims=True)

        @pl.when(i == 0)
        def _():
            dw_ref[...] = dwv
            db_ref[...] = dbv

        @pl.when(i > 0)
        def _():
            dw_ref[...] += dwv
            db_ref[...] += dbv

    tile = pl.BlockSpec((tc, c), lambda i: (i, 0))
    prev = pl.BlockSpec((HALO, c), lambda i: (jnp.maximum(i * per - 1, 0), 0))
    nxt = pl.BlockSpec((HALO, c), lambda i: (jnp.minimum((i + 1) * per, last_blk), 0))
    return pl.pallas_call(
        body, name="conv_bwd" + tag, grid=(nt,),
        in_specs=[tile, prev, nxt, tile, nxt, pl.BlockSpec((CONV_K, c), lambda i: (0, 0)),
                  pl.BlockSpec((1, c), lambda i: (0, 0))],
        out_specs=[tile, pl.BlockSpec((CONV_K, c), lambda i: (0, 0)), pl.BlockSpec((1, c), lambda i: (0, 0))],
        out_shape=[jax.ShapeDtypeStruct((s, c), F32), jax.ShapeDtypeStruct((CONV_K, c), F32),
                   jax.ShapeDtypeStruct((1, c), F32)],
        compiler_params=_params(("arbitrary",)),
    )(xbc, xbc, xbc, dact, dact, w, b)


def _ssd_chunk(xs, bm, cm, dtr, z, dt_bias, a_log, d_skip, g, s_prev):
    n = CHUNK
    row = lax.broadcasted_iota(jnp.int32, (n, n), 0)
    col = lax.broadcasted_iota(jnp.int32, (n, n), 1)
    causal = row >= col
    dt = jax.nn.softplus(dtr + dt_bias)
    a_c = dt * (-jnp.exp(a_log))
    hi = lax.Precision.HIGHEST
    a_cum = jnp.dot(causal.astype(F32), a_c, precision=hi, preferred_element_type=F32)
    a_cum_t = lax.dot_general(a_c, (row <= col).astype(F32), (_TN, ((), ())), precision=hi,
                              preferred_element_type=F32)
    p, st = SSD_HEAD_DIM, SSD_STATE
    cb = [mm_nt(cm[:, k * st:(k + 1) * st], bm[:, k * st:(k + 1) * st]) for k in range(SSD_GROUPS)]
    ys, s_new = [], []
    for h in range(SSD_HEADS):
        k = h // (SSD_HEADS // SSD_GROUPS)
        acol = a_cum[:, h:h + 1]
        decay = jnp.exp(jnp.where(causal, acol - a_cum_t[h:h + 1, :], -jnp.inf))
        xh = xs[:, h * p:(h + 1) * p]
        xdt = xh * dt[:, h:h + 1]
        a_last = a_cum[n - 1:n, h:h + 1]
        sp = s_prev[h * p:(h + 1) * p, :]
        bg = bm[:, k * st:(k + 1) * st]
        cg = cm[:, k * st:(k + 1) * st]
        s_new.append(sp * jnp.exp(a_last) + mm_tn(xdt * jnp.exp(a_last - acol), bg))
        y = mm_nn(cb[k] * decay, xdt) + mm_nt(cg, sp) * jnp.exp(acol) + d_skip[:, h:h + 1] * xh
        ys.append(y)
    y = jnp.concatenate(ys, axis=1) * jax.nn.silu(z)
    return _rms(y, g), jnp.concatenate(s_new, axis=0)


def _split_xbc(t):
    return t[:, :SSD_DIM], t[:, SSD_DIM:SSD_DIM + GN], t[:, SSD_DIM + GN:]


def _ssd_fwd(tag, xact, dtr, z, dt_bias, a_log, d_skip, g):
    s = xact.shape[0]
    nc = s // CHUNK
    srows = SSD_HEADS * SSD_HEAD_DIM

    def body(x_ref, dt_ref, z_ref, b_ref, al_ref, ds_ref, g_ref, y_ref, st_ref, state):
        @pl.when(pl.program_id(0) == 0)
        def _():
            state[...] = jnp.zeros_like(state)

        sp = state[...]
        st_ref[0] = sp
        xs, bm, cm = _split_xbc(x_ref[...])
        y, sn = _ssd_chunk(xs, bm, cm, dt_ref[...][:, :SSD_HEADS], z_ref[...], b_ref[...], al_ref[...],
                           ds_ref[...], g_ref[...], sp)
        y_ref[...] = y
        state[...] = sn

    small = pl.BlockSpec((1, SSD_HEADS), lambda i: (0, 0))
    return pl.pallas_call(
        body, name="ssd_fwd" + tag, grid=(nc,),
        in_specs=[pl.BlockSpec((CHUNK, CONV_DIM), lambda i: (i, 0)), pl.BlockSpec((CHUNK, DT_PAD), lambda i: (i, 0)),
                  pl.BlockSpec((CHUNK, SSD_DIM), lambda i: (i, 0)), small, small, small,
                  pl.BlockSpec((1, SSD_DIM), lambda i: (0, 0))],
        out_specs=[pl.BlockSpec((CHUNK, SSD_DIM), lambda i: (i, 0)),
                   pl.BlockSpec((1, srows, SSD_STATE), lambda i: (i, 0, 0))],
        out_shape=[jax.ShapeDtypeStruct((s, SSD_DIM), F32), jax.ShapeDtypeStruct((nc, srows, SSD_STATE), F32)],
        scratch_shapes=[pltpu.VMEM((srows, SSD_STATE), F32)],
        compiler_params=_params(("arbitrary",)),
    )(xact, dtr, z, dt_bias, a_log, d_skip, g)


def _ssd_bwd(tag, xact, dtr, z, states, dy, dt_bias, a_log, d_skip, g):
    s = xact.shape[0]
    nc = s // CHUNK
    srows = SSD_HEADS * SSD_HEAD_DIM

    def body(x_ref, dt_ref, z_ref, sp_ref, dy_ref, b_ref, al_ref, ds_ref, g_ref,
             dx_ref, ddt_ref, dz_ref, db_ref, dal_ref, dds_ref, dg_ref, dstate):
        first = pl.program_id(0) == 0

        @pl.when(first)
        def _():
            dstate[...] = jnp.zeros_like(dstate)

        xs, bm, cm = _split_xbc(x_ref[...])
        _, vjp = jax.vjp(_ssd_chunk, xs, bm, cm, dt_ref[...][:, :SSD_HEADS], z_ref[...], b_ref[...], al_ref[...],
                         ds_ref[...], g_ref[...], sp_ref[0])
        dxs, dbm, dcm, ddt, dz, db, dal, dds, dg, dsp = vjp((dy_ref[...], dstate[...]))
        dx_ref[...] = jnp.concatenate([dxs, dbm, dcm], axis=1)
        ddt_ref[...] = jnp.concatenate([ddt, jnp.zeros((CHUNK, DT_PAD - SSD_HEADS), F32)], axis=1)
        dz_ref[...] = dz
        dstate[...] = dsp

        @pl.when(first)
        def _():
            db_ref[...] = db
            dal_ref[...] = dal
            dds_ref[...] = dds
            dg_ref[...] = dg

        @pl.when(jnp.logical_not(first))
        def _():
            db_ref[...] += db
            dal_ref[...] += dal
            dds_ref[...] += dds
            dg_ref[...] += dg

    def rev(c):
        return lambda i: (nc - 1 - i, 0)

    small = pl.BlockSpec((1, SSD_HEADS), lambda i: (0, 0))
    gspec = pl.BlockSpec((1, SSD_DIM), lambda i: (0, 0))
    return pl.pallas_call(
        body, name="ssd_bwd" + tag, grid=(nc,),
        in_specs=[pl.BlockSpec((CHUNK, CONV_DIM), rev(0)), pl.BlockSpec((CHUNK, DT_PAD), rev(0)),
                  pl.BlockSpec((CHUNK, SSD_DIM), rev(0)),
                  pl.BlockSpec((1, srows, SSD_STATE), lambda i: (nc - 1 - i, 0, 0)),
                  pl.BlockSpec((CHUNK, SSD_DIM), rev(0)), small, small, small, gspec],
        out_specs=[pl.BlockSpec((CHUNK, CONV_DIM), rev(0)), pl.BlockSpec((CHUNK, DT_PAD), rev(0)),
                   pl.BlockSpec((CHUNK, SSD_DIM), rev(0)), small, small, small, gspec],
        out_shape=[jax.ShapeDtypeStruct((s, CONV_DIM), F32), jax.ShapeDtypeStruct((s, DT_PAD), F32),
                   jax.ShapeDtypeStruct((s, SSD_DIM), F32), jax.ShapeDtypeStruct((1, SSD_HEADS), F32),
                   jax.ShapeDtypeStruct((1, SSD_HEADS), F32), jax.ShapeDtypeStruct((1, SSD_HEADS), F32),
                   jax.ShapeDtypeStruct((1, SSD_DIM), F32)],
        scratch_shapes=[pltpu.VMEM((srows, SSD_STATE), F32)],
        compiler_params=_params(("arbitrary",)),
    )(xact, dtr, z, states, dy, dt_bias, a_log, d_skip, g)


TQ = 128
TK = 128
SB_SCALE = 1.0 / math.sqrt(SB_HEAD_DIM)


def _split2(x):
    hi = x.astype(BF16)
    return hi, (x - hi.astype(F32)).astype(BF16)


TK_WIDE = 256
SB_UNDERFLOW = -110.0


def _sb_logits(qh, kb, t0, s0, masked):
    z = _dg(qh, kb, _NT)
    lb = jnp.minimum(z, 0.0) - jnp.log(1.0 + jnp.exp(-jnp.abs(z)))
    ls = lb - z
    if not masked:
        return lb, ls, None
    t_pos = t0 + lax.broadcasted_iota(jnp.int32, z.shape, 0)
    s_pos = s0 + lax.broadcasted_iota(jnp.int32, z.shape, 1)
    mask = s_pos < t_pos
    return lb, jnp.where(mask, ls, 0.0), mask


def _running_sums(x, start, u, reverse, two_terms=True):
    nsub = x.shape[1] // TK
    run = start
    parts = [None] * nsub
    for c in (reversed(range(nsub)) if reverse else range(nsub)):
        xc = x[:, c * TK:(c + 1) * TK]
        parts[c] = run + (_lane_sums(xc, u) if two_terms else _dg(xc, u, _NN))
        run = run + jnp.sum(xc, axis=1, keepdims=True)
    return (parts[0] if nsub == 1 else jnp.concatenate(parts, axis=1)), run


def _lane_sums(x, u):
    hi, lo = _split2(x)
    return _dg(hi, u, _NN) + _dg(lo, u, _NN)


def _tri(cmp):
    j = lax.broadcasted_iota(jnp.int32, (TK, TK), 0)
    s = lax.broadcasted_iota(jnp.int32, (TK, TK), 1)
    return cmp(j, s).astype(BF16)


def _sb_fwd(tag, q, k, v):
    s = q.shape[0]
    npair = SB_DIM // LANES

    wide = min(TK_WIDE, s)
    per = wide // TQ

    def body(q_ref, k_ref, v_ref, o_ref, t_ref):
        tq = pl.program_id(1)
        diag = tq // per
        qp = q_ref[...]
        lane = lax.broadcasted_iota(jnp.int32, (1, LANES), 1)
        u_gt = _tri(lambda j, s: j > s)
        out = jnp.zeros((TQ, LANES), F32)
        tot = jnp.zeros((TQ, LANES), F32)
        for hh in range(LANES // SB_HEAD_DIM):
            hm = (lane // SB_HEAD_DIM) == hh
            qh = jnp.where(hm, qp, jnp.zeros_like(qp)) * SB_SCALE

            def block(wb, r, acc, masked):
                off = pl.multiple_of(wb * wide, wide)
                kb = k_ref[pl.ds(off, wide), :]
                vb = v_ref[pl.ds(off, wide), :]
                lb, ls, mask = _sb_logits(qh, kb, tq * TQ, wb * wide, masked)
                later, r = _running_sums(ls, r, u_gt, reverse=True)
                w = jnp.exp(lb + later)
                if masked:
                    w = jnp.where(mask, w, 0.0)
                return r, acc + _dg(w, vb, _NN)

            r, acc = block(diag, jnp.zeros((TQ, 1), F32), jnp.zeros((TQ, LANES), F32), True)

            def more(c):
                return jnp.logical_and(c[0] >= 0, jnp.max(c[1]) > SB_UNDERFLOW)

            def step(c):
                r2, acc2 = block(c[0], c[1], c[2], False)
                return c[0] - 1, r2, acc2

            wb, r, acc = lax.while_loop(more, step, (diag - 1, r, acc))
            first = (wb + 1).astype(F32)
            out = out + jnp.where(hm, acc, 0.0)
            tot = tot + jnp.where(hm, jnp.where(lane % SB_HEAD_DIM == 1, first, r), 0.0)
        o_ref[...] = out
        t_ref[...] = tot

    tile = pl.BlockSpec((TQ, LANES), lambda p, t: (t, p))
    full = pl.BlockSpec((s, LANES), lambda p, t: (0, p))
    return pl.pallas_call(
        body, name="sb_fwd" + tag, grid=(npair, s // TQ),
        in_specs=[tile, full, full], out_specs=[tile, tile],
        out_shape=[jax.ShapeDtypeStruct((s, SB_DIM), F32)] * 2,
        compiler_params=_params(("parallel", "arbitrary")),
    )(q, k, v)


def _sb_bwd(tag, q, k, v, tot, do):
    s = q.shape[0]
    npair = SB_DIM // LANES

    wide = min(TK_WIDE, s)
    per = wide // TQ

    def body(q_ref, k_ref, v_ref, t_ref, do_ref, dq_ref, dk_ref, dv_ref):
        tq = pl.program_id(1)
        diag = tq // per

        @pl.when(tq == 0)
        def _():
            dk_ref[...] = jnp.zeros_like(dk_ref)
            dv_ref[...] = jnp.zeros_like(dv_ref)

        qp = q_ref[...]
        dop = do_ref[...]
        totp = t_ref[...]
        lane = lax.broadcasted_iota(jnp.int32, (1, LANES), 1)
        u_le = _tri(lambda j, s: j <= s)
        u_lt = _tri(lambda j, s: j < s)
        dq = jnp.zeros((TQ, LANES), F32)
        for hh in range(LANES // SB_HEAD_DIM):
            hm = (lane // SB_HEAD_DIM) == hh
            qh = jnp.where(hm, qp, jnp.zeros_like(qp)) * SB_SCALE
            doh = jnp.where(hm, dop, 0.0).astype(BF16)
            total = jnp.sum(jnp.where(lane == hh * SB_HEAD_DIM, totp, 0.0), axis=1, keepdims=True)
            first = jnp.max(jnp.where(lane == hh * SB_HEAD_DIM + 1, totp, 0.0)).astype(jnp.int32)

            def block(wb, pre, gc, acc, masked):
                off = pl.multiple_of(wb * wide, wide)
                kb = k_ref[pl.ds(off, wide), :]
                vb = v_ref[pl.ds(off, wide), :]
                lb, ls, mask = _sb_logits(qh, kb, tq * TQ, wb * wide, masked)
                before, pre = _running_sums(ls, pre, u_le, reverse=False)
                w = jnp.exp(lb + (total - before))
                if masked:
                    w = jnp.where(mask, w, 0.0)
                g = w * _dg(doh, vb, _NT)
                g_left, gc = _running_sums(g, gc, u_lt, reverse=False, two_terms=False)
                dz = g - jnp.exp(lb) * (g + g_left)
                if masked:
                    dz = jnp.where(mask, dz, 0.0)
                dzb = dz.astype(BF16)
                dk_ref[pl.ds(off, wide), :] += _dg(dzb, qh, _TN)
                dv_ref[pl.ds(off, wide), :] += _dg(w, doh, _TN)
                return pre, gc, acc + _dg(dzb, kb, _NN)

            zero = jnp.zeros((TQ, 1), F32)
            carry = lax.fori_loop(first, diag, lambda j, c: block(j, c[0], c[1], c[2], False),
                                  (zero, zero, jnp.zeros((TQ, LANES), F32)))
            _, _, acc = block(diag, carry[0], carry[1], carry[2], True)
            dq = dq + jnp.where(hm, acc, 0.0)
        dq_ref[...] = dq * SB_SCALE

    tile = pl.BlockSpec((TQ, LANES), lambda p, t: (t, p))
    full = pl.BlockSpec((s, LANES), lambda p, t: (0, p))
    return pl.pallas_call(
        body, name="sb_bwd" + tag, grid=(npair, s // TQ),
        in_specs=[tile, full, full, tile, tile],
        out_specs=[tile, full, full],
        out_shape=[jax.ShapeDtypeStruct((s, SB_DIM), F32)] * 3,
        compiler_params=_params(("parallel", "arbitrary")),
    )(q, k, v, tot, do)


def _out_tile(y_ssd, o, sb_g, w_out):
    y_all = jnp.concatenate([y_ssd, _rms(o, sb_g)], axis=1)
    return mm_nn(y_all, w_out)


def _out_fwd(tag, h, y_ssd, o, sb_g, w_out):
    return _rowcall("out_fwd" + tag, lambda h, y, o, g, w: (h + _out_tile(y, o, g, w),),
                    [h, y_ssd, o], [sb_g, w_out], [(D_MODEL, F32)], tm=512)[0]


def _out_bwd(tag, y_ssd, o, dh, sb_g, w_out):
    def fn(y, o, dh, g, w):
        _, vjp = jax.vjp(_out_tile, y, o, g, w.astype(F32))
        return vjp(dh)

    return _rowcall("out_bwd" + tag, fn, [y_ssd, o, dh], [sb_g, w_out], [(SSD_DIM, F32), (SB_DIM, F32)],
                    [((1, SB_DIM), F32), ((D_MODEL, D_MODEL), F32)], tm=256)


def _mem_tile(mem, g, w_k, w_v):
    m = _rms(mem, g)
    return mm_nn(m, w_k), mm_nn(m, w_v)


def _mem_fwd(tag, mem, g, w_k, w_v):
    return _rowcall("mem_fwd" + tag, _mem_tile, [mem], [g, w_k, w_v], [(XA_DIM, F32), (XA_DIM, F32)], tm=256)


def _mem_bwd(tag, mem, dkx, dvx, g, w_k, w_v):
    def fn(mem, dkx, dvx, g, w_k, w_v):
        _, vjp = jax.vjp(lambda g, a, b: _mem_tile(mem, g, a, b), g, w_k.astype(F32), w_v.astype(F32))
        return vjp((dkx, dvx))

    return _rowcall("mem_bwd" + tag, fn, [mem, dkx, dvx], [g, w_k, w_v], [],
                    [((1, D_MODEL), F32), ((D_MODEL, XA_DIM), F32), ((D_MODEL, XA_DIM), F32)], tm=256)


def _xattn_tile(h, g, w_q, kx, vx, w_o):
    q = mm_nn(_rms(h, g), w_q)
    scale = 1.0 / math.sqrt(XA_HEAD_DIM)
    outs = []
    for i in range(XA_HEADS):
        sl = slice(i * XA_HEAD_DIM, (i + 1) * XA_HEAD_DIM)
        p = jax.nn.softmax(mm_nt(q[:, sl], kx[:, sl]) * scale, axis=-1)
        outs.append(mm_nn(p, vx[:, sl]))
    return mm_nn(jnp.concatenate(outs, axis=1), w_o)


def _xattn_fwd(tag, h, g, w_q, kx, vx, w_o):
    return _rowcall("xattn_fwd" + tag, lambda h, g, wq, kx, vx, wo: (h + _xattn_tile(h, g, wq, kx, vx, wo),),
                    [h], [g, w_q, kx, vx, w_o], [(D_MODEL, F32)], tm=512)[0]


def _xattn_bwd(tag, h, dh_out, g, w_q, kx, vx, w_o):
    def fn(h, dh_out, g, w_q, kx, vx, w_o):
        _, vjp = jax.vjp(_xattn_tile, h, g, w_q.astype(F32), kx, vx, w_o.astype(F32))
        dh, dg, dwq, dkx, dvx, dwo = vjp(dh_out)
        return dh_out + dh, dg, dwq, dkx, dvx, dwo

    mlen = kx.shape[0]
    return _rowcall("xattn_bwd" + tag, fn, [h, dh_out], [g, w_q, kx, vx, w_o], [(D_MODEL, F32)],
                    [((1, D_MODEL), F32), ((D_MODEL, XA_DIM), F32), ((mlen, XA_DIM), F32), ((mlen, XA_DIM), F32),
                     ((XA_DIM, D_MODEL), F32)], tm=256)


def _mlp_fwd(tag, h, g, w1, w2):
    def fn(h, g, w1, w2):
        u = jnp.square(jnp.maximum(mm_nn(_rms(h, g), w1), 0.0))
        return (h + mm_nn(u, w2),)

    return _rowcall("mlp_fwd" + tag, fn, [h], [g, w1, w2], [(D_MODEL, F32)], tm=256)[0]


def _mlp_bwd(tag, h, dh_out, g, w1, w2):
    def fn(h, dh_out, g, w1, w2):
        hn, vjp = jax.vjp(_rms, h, g)
        r = jnp.maximum(mm_nn(hn, w1), 0.0)
        dob = dh_out.astype(BF16)
        dp = mm_nt(dob, w2) * (2.0 * r)
        dh, dg = vjp(mm_nt(dp, w1))
        return dh_out + dh, hn, dp, r * r, dob, dg

    return _rowcall("mlp_bwd" + tag, fn, [h, dh_out], [g, w1, w2],
                    [(D_MODEL, F32), (D_MODEL, BF16), (D_FF, BF16), (D_FF, BF16), (D_MODEL, BF16)],
                    [((1, D_MODEL), F32)], tm=256)


def _head(h, g, target):
    def lossfn(h, g, t):
        err = jnp.square(_rms(h, g) - t)
        return 0.5 * jnp.sum(jnp.mean(err, axis=-1))

    def fn(h, t, g):
        loss, vjp = jax.vjp(lambda h, g: lossfn(h, g, t), h, g)
        dh, dg = vjp(jnp.ones((), F32))
        return dh, jnp.full((1, LANES), loss, F32), dg

    return _rowcall("head", fn, [h, target], [g], [(D_MODEL, F32)], [((1, LANES), F32), ((1, D_MODEL), F32)], tm=512)


def _row(v):
    return v.reshape(1, -1)


def _local_step(x, mem, target, sw, mats):
    h = x
    saved = []
    for l in range(DEPTH):
        tag = str(l)
        m = mats[l]
        z, xbc, q, k, v, dtr = _proj_fwd(tag, h, _row(sw["norm_mix_g"][l]), m["w_in"])
        xact = _conv_fwd(tag, xbc, sw["conv_w"][l], _row(sw["conv_b"][l]))
        y_ssd, states = _ssd_fwd(tag, xact, dtr, z, _row(sw["dt_bias"][l]), _row(sw["a_log"][l]),
                                 _row(sw["d_skip"][l]), _row(sw["ssd_norm_g"][l]))
        o, sb_tot = _sb_fwd(tag, q, k, v)
        h1 = _out_fwd(tag, h, y_ssd, o, _row(sw["sb_norm_g"][l]), m["w_out"])
        kx, vx = _mem_fwd(tag, mem, _row(sw["norm_mem_g"][l]), m["w_xk"], m["w_xv"])
        h2 = _xattn_fwd(tag, h1, _row(sw["norm_xa_g"][l]), m["w_xq"], kx, vx, m["w_xo"])
        h3 = _mlp_fwd(tag, h2, _row(sw["norm_ff_g"][l]), m["w_ff1"], m["w_ff2"])
        saved.append((h, z, xbc, q, k, v, dtr, xact, y_ssd, states, o, sb_tot, h1, kx, vx, h2))
        h = h3

    dh, loss, d_final = _head(h, _row(sw["final_g"]), target)
    gm = [dict() for _ in range(DEPTH)]
    gs = {name: [None] * DEPTH for name in ("norm_mix_g", "conv_w", "conv_b", "dt_bias", "a_log", "d_skip",
                                            "ssd_norm_g", "sb_norm_g", "norm_xa_g", "norm_mem_g", "norm_ff_g")}
    g_ff1 = g_ff2 = None
    for l in reversed(range(DEPTH)):
        tag = str(l)
        m = mats[l]
        h0, z, xbc, q, k, v, dtr, xact, y_ssd, states, o, sb_tot, h1, kx, vx, h2 = saved[l]
        dh2, hn_b, dp_b, a_b, do_b, gs["norm_ff_g"][l] = _mlp_bwd(
            tag, h2, dh, _row(sw["norm_ff_g"][l]), m["w_ff1"], m["w_ff2"])
        g_ff1 = _mm_tn_parts("dw_ff1" + tag, hn_b, dp_b, l, g_ff1, True)
        g_ff2 = _mm_tn_parts("dw_ff2" + tag, a_b, do_b, l, g_ff2, False)
        dh1, gs["norm_xa_g"][l], gm[l]["w_xq"], dkx, dvx, gm[l]["w_xo"] = _xattn_bwd(
            tag, h1, dh2, _row(sw["norm_xa_g"][l]), m["w_xq"], kx, vx, m["w_xo"])
        gs["norm_mem_g"][l], gm[l]["w_xk"], gm[l]["w_xv"] = _mem_bwd(
            tag, mem, dkx, dvx, _row(sw["norm_mem_g"][l]), m["w_xk"], m["w_xv"])
        dy_ssd, do, gs["sb_norm_g"][l], gm[l]["w_out"] = _out_bwd(
            tag, y_ssd, o, dh1, _row(sw["sb_norm_g"][l]), m["w_out"])
        dq, dk, dv = _sb_bwd(tag, q, k, v, sb_tot, do)
        dxact, ddtr, dz, gs["dt_bias"][l], gs["a_log"][l], gs["d_skip"][l], gs["ssd_norm_g"][l] = _ssd_bwd(
            tag, xact, dtr, z, states, dy_ssd, _row(sw["dt_bias"][l]), _row(sw["a_log"][l]),
            _row(sw["d_skip"][l]), _row(sw["ssd_norm_g"][l]))
        dxbc, gs["conv_w"][l], gs["conv_b"][l] = _conv_bwd(tag, xbc, dxact, sw["conv_w"][l], _row(sw["conv_b"][l]))
        dh, hn_b, dp_b, gs["norm_mix_g"][l] = _proj_bwd(
            tag, h0, dh1, dz, dxbc, dq, dk, dv, ddtr, _row(sw["norm_mix_g"][l]), m["w_in"])
        gm[l]["w_in"] = _mm_tn_call("dw_in" + tag, hn_b, dp_b, 512, IN_PAD, 1024)
    gs["final_g"] = d_final
    return loss, dh, gm, gs, {"w_ff1": g_ff1, "w_ff2": g_ff2}


ANY = pl.BlockSpec(memory_space=pl.ANY)
VMEM_SPEC = pl.BlockSpec(memory_space=pltpu.VMEM)


def _place():
    return lax.axis_index("x"), lax.axis_index("y"), lax.axis_index("c")


def _other_chips(x, y):
    return [(1 - x, y), (x, 1 - y), (1 - x, 1 - y)]


def _gather_weights(shards):
    n = len(shards)

    def body(*refs):
        w_refs, o_refs = refs[:n], refs[n:2 * n]
        send_sems, recv_sems = refs[2 * n:]
        x, y, c = _place()
        me_chip = 2 * x + y
        sibling = (x, y, 1 - c)
        chips = _other_chips(x, y)

        def copy(idx, src, dst, to):
            return pltpu.make_async_remote_copy(src_ref=src, dst_ref=dst, send_sem=send_sems.at[idx],
                                                recv_sem=recv_sems.at[idx], device_id=to, device_id_type=MESH)

        first = [copy(6 * i + kk, w_refs[i].at[c], o_refs[i].at[c, me_chip], (cx, cy, c))
                 for i in range(n) for kk, (cx, cy) in enumerate(chips)]
        for cp in first:
            cp.start()
        passed = []
        for i in range(n):
            for kk, (cx, cy) in enumerate(chips):
                got = o_refs[i].at[c, 2 * cx + cy]
                copy(6 * i + kk, got, got, (x, y, c)).wait_recv()
                fwd = copy(6 * i + 3 + kk, got, got, sibling)
                fwd.start()
                passed.append(fwd)
        for i in range(n):
            for kk, (cx, cy) in enumerate(chips):
                got = o_refs[i].at[1 - c, 2 * cx + cy]
                copy(6 * i + 3 + kk, got, got, (x, y, c)).wait_recv()
        for cp in first + passed:
            cp.wait_send()

    return pl.pallas_call(
        body, name="gather_weights", in_specs=[ANY] * n, out_specs=[ANY] * n,
        out_shape=[jax.ShapeDtypeStruct((DEPTH, N_CHIPS) + s.shape[1:], s.dtype) for s in shards],
        scratch_shapes=[pltpu.SemaphoreType.DMA((6 * n,)), pltpu.SemaphoreType.DMA((6 * n,))],
    )(*shards)


def _gather_small(tag, buf):
    shape = buf.shape

    def body(b_ref, o_ref, sum_ref, send_sems, recv_sems, local_sem):
        x, y, c = _place()
        me = 4 * x + 2 * y + c
        mine = pltpu.make_async_copy(b_ref, o_ref.at[me], local_sem)
        mine.start()
        flips = [(dx, dy, dc) for dx in (0, 1) for dy in (0, 1) for dc in (0, 1) if (dx, dy, dc) != (0, 0, 0)]
        sends = []

        def peer(dx, dy, dc):
            return (1 - x if dx else x, 1 - y if dy else y, 1 - c if dc else c)

        for kk, flip in enumerate(flips):
            cp = pltpu.make_async_remote_copy(src_ref=b_ref, dst_ref=o_ref.at[me], send_sem=send_sems.at[kk],
                                              recv_sem=recv_sems.at[kk], device_id=peer(*flip), device_id_type=MESH)
            cp.start()
            sends.append(cp)
        for kk, flip in enumerate(flips):
            px, py, pc = peer(*flip)
            frm = 4 * px + 2 * py + pc
            pltpu.make_async_remote_copy(src_ref=b_ref, dst_ref=o_ref.at[frm], send_sem=send_sems.at[kk],
                                         recv_sem=recv_sems.at[kk], device_id=(x, y, c),
                                         device_id_type=MESH).wait_recv()
        for cp in sends:
            cp.wait_send()
        mine.wait()
        total = o_ref[0]
        for d in range(1, N_DEV):
            total = total + o_ref[d]
        sum_ref[...] = total

    return pl.pallas_call(
        body, name="gather_small" + tag, in_specs=[VMEM_SPEC], out_specs=[VMEM_SPEC, VMEM_SPEC],
        out_shape=[jax.ShapeDtypeStruct((N_DEV,) + shape, buf.dtype), jax.ShapeDtypeStruct(shape, buf.dtype)],
        scratch_shapes=[pltpu.SemaphoreType.DMA((N_DEV - 1,)), pltpu.SemaphoreType.DMA((N_DEV - 1,)),
                        pltpu.SemaphoreType.DMA],
    )(buf)


def _swap_with_sibling(name, arrays, pick=None):
    n = len(arrays)

    def body(*refs):
        a_refs, o_refs = refs[:n], refs[n:2 * n]
        send_sems, recv_sems = refs[2 * n:]
        x, y, c = _place()
        cps = [pltpu.make_async_remote_copy(
            src_ref=a_refs[i] if pick is None else a_refs[i].at[pick(c)], dst_ref=o_refs[i],
            send_sem=send_sems.at[i], recv_sem=recv_sems.at[i], device_id=(x, y, 1 - c), device_id_type=MESH)
            for i in range(n)]
        for cp in cps:
            cp.start()
        for cp in cps:
            cp.wait()

    return pl.pallas_call(
        body, name=name, in_specs=[ANY] * n, out_specs=[ANY] * n,
        out_shape=[jax.ShapeDtypeStruct(a.shape if pick is None else a.shape[1:], a.dtype) for a in arrays],
        scratch_shapes=[pltpu.SemaphoreType.DMA((n,)), pltpu.SemaphoreType.DMA((n,))],
    )(*arrays)


def _add_layers(tag, g, r, tr=256):
    _, _, rows, cols = g.shape
    tr = min(tr, rows)
    core = lax.axis_index("c").astype(jnp.int32).reshape(1)

    def body(c_ref, g_ref, r_ref, o_ref):
        o_ref[...] = (g_ref[0] + r_ref[...]).astype(o_ref.dtype)

    return pl.pallas_call(
        body, name="add_layers_" + tag,
        grid_spec=pltpu.PrefetchScalarGridSpec(
            num_scalar_prefetch=1, grid=(N_CHIPS, rows // tr),
            in_specs=[pl.BlockSpec((1, 1, tr, cols), lambda p, i, c_ref: (c_ref[0], p, i, 0)),
                      pl.BlockSpec((1, tr, cols), lambda p, i, c_ref: (p, i, 0))],
            out_specs=pl.BlockSpec((1, tr, cols), lambda p, i, c_ref: (p, i, 0))),
        out_shape=jax.ShapeDtypeStruct(r.shape, BF16),
        compiler_params=_params(("arbitrary", "arbitrary")),
    )(core, g, r)


def _scatter_chips(parts):
    n = len(parts)

    def body(*refs):
        s_refs, o_refs = refs[:n], refs[n:2 * n]
        send_sems, recv_sems = refs[2 * n:]
        x, y, c = _place()
        me_chip = 2 * x + y
        chips = _other_chips(x, y)
        sends = []
        for i in range(n):
            for kk, (cx, cy) in enumerate(chips):
                cp = pltpu.make_async_remote_copy(src_ref=s_refs[i].at[2 * cx + cy], dst_ref=o_refs[i].at[me_chip],
                                                  send_sem=send_sems.at[3 * i + kk], recv_sem=recv_sems.at[3 * i + kk],
                                                  device_id=(cx, cy, c), device_id_type=MESH)
                cp.start()
                sends.append(cp)
        for i in range(n):
            for kk, (cx, cy) in enumerate(chips):
                got = o_refs[i].at[2 * cx + cy]
                pltpu.make_async_remote_copy(src_ref=got, dst_ref=got, send_sem=send_sems.at[3 * i + kk],
                                             recv_sem=recv_sems.at[3 * i + kk], device_id=(x, y, c),
                                             device_id_type=MESH).wait_recv()
        for cp in sends:
            cp.wait_send()

    return pl.pallas_call(
        body, name="scatter_chips", in_specs=[ANY] * n, out_specs=[ANY] * n,
        out_shape=[jax.ShapeDtypeStruct(p.shape, p.dtype) for p in parts],
        scratch_shapes=[pltpu.SemaphoreType.DMA((3 * n,)), pltpu.SemaphoreType.DMA((3 * n,))],
    )(*parts)


def _sum_parts(tag, own, parts, tr=256):
    _, rows, cols = parts.shape
    tr = min(tr, rows)
    chip = (2 * lax.axis_index("x") + lax.axis_index("y")).astype(jnp.int32).reshape(1)

    def body(c_ref, own_ref, p1_ref, p2_ref, p3_ref, o_ref):
        total = own_ref[0].astype(F32)
        for p_ref in (p1_ref, p2_ref, p3_ref):
            total = total + p_ref[0].astype(F32)
        o_ref[...] = total

    def after(kk):
        return pl.BlockSpec((1, tr, cols), lambda i, c_ref: ((c_ref[0] + kk) % N_CHIPS, i, 0))

    return pl.pallas_call(
        body, name="sum_parts_" + tag,
        grid_spec=pltpu.PrefetchScalarGridSpec(
            num_scalar_prefetch=1, grid=(rows // tr,), in_specs=[after(0), after(1), after(2), after(3)],
            out_specs=pl.BlockSpec((tr, cols), lambda i, c_ref: (i, 0))),
        out_shape=jax.ShapeDtypeStruct((rows, cols), F32),
        compiler_params=_params(("arbitrary",)),
    )(chip, own, parts, parts, parts)


def _adamw_math(w, g, m, v):
    m = ADAM_B1 * m + (1.0 - ADAM_B1) * g
    v = ADAM_B2 * v + (1.0 - ADAM_B2) * jnp.square(g)
    m_hat = m / (1.0 - ADAM_B1 ** ADAM_STEP)
    v_hat = v / (1.0 - ADAM_B2 ** ADAM_STEP)
    delta = -ADAM_LR * (m_hat / (jnp.sqrt(v_hat) + ADAM_EPS) + ADAM_WD * w)
    return delta, m, v


def _adamw(tag, w, g, m, v, tr=256):
    rows, cols = w.shape
    tr = min(tr, rows)

    def body(w_ref, g_ref, m_ref, v_ref, d_ref, nm_ref, nv_ref):
        d_ref[...], nm_ref[...], nv_ref[...] = _adamw_math(w_ref[...], g_ref[...], m_ref[...], v_ref[...])

    spec = pl.BlockSpec((tr, cols), lambda i: (i, 0))
    return pl.pallas_call(
        body, name="adamw_" + tag, grid=(rows // tr,), in_specs=[spec] * 4, out_specs=[spec] * 3,
        out_shape=[jax.ShapeDtypeStruct(w.shape, F32)] * 3,
        compiler_params=_params(("parallel",)),
    )(w, g, m, v)


def _w_in_to_padded(w):
    d0 = SSD_DIM + CONV_DIM
    return jnp.concatenate([w[:, :d0], w[:, d0 + SSD_HEADS:], w[:, d0:d0 + SSD_HEADS],
                            jnp.zeros((w.shape[0], DT_PAD - SSD_HEADS), w.dtype)], axis=1)


def _w_in_from_padded(w):
    d0 = SSD_DIM + CONV_DIM
    return jnp.concatenate([w[:, :d0], w[:, DT_OFF:DT_OFF + SSD_HEADS], w[:, d0:DT_OFF]], axis=1)


def _small_layout():
    return (("norm_mix_g", 0, 0, D_MODEL), ("norm_xa_g", 1, 0, D_MODEL), ("norm_mem_g", 2, 0, D_MODEL),
            ("norm_ff_g", 3, 0, D_MODEL), ("conv_b", 4, 0, CONV_DIM), ("ssd_norm_g", 5, 0, SSD_DIM),
            ("sb_norm_g", 5, SSD_DIM, SB_DIM), ("dt_bias", 6, 0, SSD_HEADS), ("a_log", 6, LANES, SSD_HEADS),
            ("d_skip", 6, 2 * LANES, SSD_HEADS))


def _pack_small(gs, loss):
    lay = _small_layout()
    args = [gs[name][l] for l in range(DEPTH) for name, _, _, _ in lay]
    args += [gs["conv_w"][l] for l in range(DEPTH)] + [gs["final_g"], loss]
    n_lay = len(lay)

    def body(*refs):
        o_ref = refs[-1]
        o_ref[...] = jnp.zeros_like(o_ref)
        for l in range(DEPTH):
            for i, (_, rr, c0, width) in enumerate(lay):
                row = l * _SM_PER_LAYER + rr
                o_ref[row:row + 1, c0:c0 + width] = refs[l * n_lay + i][...]
            row = _SM_CONVW + l * CONV_K
            o_ref[row:row + CONV_K, 0:CONV_DIM] = refs[DEPTH * n_lay + l][...]
        o_ref[_SM_FINAL:_SM_FINAL + 1, :] = refs[DEPTH * n_lay + DEPTH][...]
        o_ref[_SM_LOSS:_SM_LOSS + 1, 0:LANES] = refs[DEPTH * n_lay + DEPTH + 1][...]

    return pl.pallas_call(
        body, name="pack_small", in_specs=[VMEM_SPEC] * len(args), out_specs=VMEM_SPEC,
        out_shape=jax.ShapeDtypeStruct((SMALL_ROWS, PACK_COLS), F32),
    )(*args)


def _small_update(buf, w, mom, var):
    lay = _small_layout()
    names = [name for name, _, _, _ in lay] + ["final_g", "conv_w"]
    conv_cols = CONV_DIM // N_CHIPS
    shapes2d = {name: (DEPTH, width) for name, _, _, width in lay}
    shapes2d["final_g"] = (1, D_MODEL)
    shapes2d["conv_w"] = (DEPTH * CONV_K, conv_cols)
    args = [buf]
    for src in (w, mom, var):
        args += [src[name].reshape(shapes2d[name]) for name in names]
    n = len(names)

    def body(*refs):
        b_ref = refs[0]
        w_refs, m_refs, v_refs = refs[1:1 + n], refs[1 + n:1 + 2 * n], refs[1 + 2 * n:1 + 3 * n]
        outs = refs[1 + 3 * n:]
        chip = 2 * lax.axis_index("x") + lax.axis_index("y")
        for i, name in enumerate(names):
            if name == "final_g":
                g = b_ref[_SM_FINAL:_SM_FINAL + 1, :]
            elif name == "conv_w":
                rows = b_ref[_SM_CONVW:_SM_CONVW + DEPTH * CONV_K, 0:CONV_DIM]
                g = jnp.zeros((DEPTH * CONV_K, conv_cols), F32)
                for j in range(N_CHIPS):
                    g = g + jnp.where(chip == j, rows[:, j * conv_cols:(j + 1) * conv_cols], 0.0)
            else:
                _, rr, c0, width = lay[i]
                g = jnp.concatenate([b_ref[l * _SM_PER_LAYER + rr:l * _SM_PER_LAYER + rr + 1, c0:c0 + width]
                                     for l in range(DEPTH)], axis=0)
            d, m2, v2 = _adamw_math(w_refs[i][...], g, m_refs[i][...], v_refs[i][...])
            outs[i][...] = g
            outs[n + i][...] = d
            outs[2 * n + i][...] = m2
            outs[3 * n + i][...] = v2

    out_shape = [jax.ShapeDtypeStruct(shapes2d[name], F32) for _ in range(4) for name in names]
    res = pl.pallas_call(
        body, name="small_update", in_specs=[VMEM_SPEC] * len(args), out_specs=[VMEM_SPEC] * (4 * n),
        out_shape=out_shape,
    )(*args)
    return tuple({name: res[k * n + i].reshape(w[name].shape) for i, name in enumerate(names)} for k in range(4))


CONV_BLOCK_ROWS = 8


def _conv_w_block(cw):
    flat = cw.reshape(-1)
    pad = jnp.zeros((CONV_BLOCK_ROWS * PACK_COLS - flat.shape[0],), F32)
    return jnp.concatenate([flat, pad]).reshape(CONV_BLOCK_ROWS, PACK_COLS)


def _conv_w_from_slots(slots):
    cols = CONV_DIM // N_CHIPS
    n = DEPTH * CONV_K * cols
    shards = [slots[2 * j].reshape(-1)[:n].reshape(DEPTH, CONV_K, cols) for j in range(N_CHIPS)]
    return jnp.concatenate(shards, axis=2)


SMALL_NAMES = ("norm_mix_g", "conv_b", "dt_bias", "a_log", "d_skip", "ssd_norm_g", "sb_norm_g", "norm_xa_g",
               "norm_mem_g", "norm_ff_g", "final_g")
WEIGHT_ORDER = ("norm_mix_g", "w_in", "conv_w", "conv_b", "dt_bias", "a_log", "d_skip", "ssd_norm_g", "sb_norm_g",
                "w_out", "norm_xa_g", "norm_mem_g", "w_xq", "w_xk", "w_xv", "w_xo", "norm_ff_g", "w_ff1", "w_ff2",
                "final_g")


def kernel(x, mem, norm_mix_g, w_in, conv_w, conv_b, dt_bias, a_log, d_skip, ssd_norm_g, sb_norm_g, w_out, norm_xa_g, norm_mem_g, w_xq, w_xk, w_xv, w_xo, norm_ff_g, w_ff1, w_ff2, final_g, loss_target, m_norm_mix_g, m_w_in, m_conv_w, m_conv_b, m_dt_bias, m_a_log, m_d_skip, m_ssd_norm_g, m_sb_norm_g, m_w_out, m_norm_xa_g, m_norm_mem_g, m_w_xq, m_w_xk, m_w_xv, m_w_xo, m_norm_ff_g, m_w_ff1, m_w_ff2, m_final_g, v_norm_mix_g, v_w_in, v_conv_w, v_conv_b, v_dt_bias, v_a_log, v_d_skip, v_ssd_norm_g, v_sb_norm_g, v_w_out, v_norm_xa_g, v_norm_mem_g, v_w_xq, v_w_xk, v_w_xv, v_w_xo, v_norm_ff_g, v_w_ff1, v_w_ff2, v_final_g):
    w = dict(norm_mix_g=norm_mix_g, w_in=w_in, conv_w=conv_w, conv_b=conv_b, dt_bias=dt_bias, a_log=a_log,
             d_skip=d_skip, ssd_norm_g=ssd_norm_g, sb_norm_g=sb_norm_g, w_out=w_out, norm_xa_g=norm_xa_g,
             norm_mem_g=norm_mem_g, w_xq=w_xq, w_xk=w_xk, w_xv=w_xv, w_xo=w_xo, norm_ff_g=norm_ff_g, w_ff1=w_ff1,
             w_ff2=w_ff2, final_g=final_g)
    mom = dict(norm_mix_g=m_norm_mix_g, w_in=m_w_in, conv_w=m_conv_w, conv_b=m_conv_b, dt_bias=m_dt_bias,
               a_log=m_a_log, d_skip=m_d_skip, ssd_norm_g=m_ssd_norm_g, sb_norm_g=m_sb_norm_g, w_out=m_w_out,
               norm_xa_g=m_norm_xa_g, norm_mem_g=m_norm_mem_g, w_xq=m_w_xq, w_xk=m_w_xk, w_xv=m_w_xv, w_xo=m_w_xo,
               norm_ff_g=m_norm_ff_g, w_ff1=m_w_ff1, w_ff2=m_w_ff2, final_g=m_final_g)
    var = dict(norm_mix_g=v_norm_mix_g, w_in=v_w_in, conv_w=v_conv_w, conv_b=v_conv_b, dt_bias=v_dt_bias,
               a_log=v_a_log, d_skip=v_d_skip, ssd_norm_g=v_ssd_norm_g, sb_norm_g=v_sb_norm_g, w_out=v_w_out,
               norm_xa_g=v_norm_xa_g, norm_mem_g=v_norm_mem_g, w_xq=v_w_xq, w_xk=v_w_xk, w_xv=v_w_xv, w_xo=v_w_xo,
               norm_ff_g=v_norm_ff_g, w_ff1=v_w_ff1, w_ff2=v_w_ff2, final_g=v_final_g)
    chip = 2 * lax.axis_index("x") + lax.axis_index("y")

    shards = [w[name].astype(BF16) for name, _, _ in MATS]
    gathered = _gather_weights(shards)
    mats = []
    for l in range(DEPTH):
        full = {}
        for (name, _, axis), mine_bf, theirs in zip(MATS, shards, gathered):
            parts = [jnp.where(chip == j, mine_bf[l], theirs[l, j]) for j in range(N_CHIPS)]
            full[name] = jnp.concatenate(parts, axis=axis)
        full["w_in"] = _w_in_to_padded(full["w_in"])
        mats.append(full)
    conv_slots, _ = _gather_small("_conv", _conv_w_block(conv_w))
    conv_w_full = _conv_w_from_slots(conv_slots)

    sw = {name: w[name] for name in SMALL_NAMES}
    sw["conv_w"] = conv_w_full
    loss, grad_x, gm, gs, g_stacked = _local_step(x[0], mem[0], loss_target[0], sw, mats)

    _, small_sum = _gather_small("_grads", _pack_small(gs, loss))
    loss_out = small_sum[_SM_LOSS, 0]

    def parts_of(g, axis):
        if axis == 0:
            return g.reshape((N_CHIPS, g.shape[0] // N_CHIPS, g.shape[1]))
        return jnp.swapaxes(g.reshape((g.shape[0], N_CHIPS, g.shape[1] // N_CHIPS)), 0, 1)

    g_parts = []
    for name, _, axis in MATS:
        if name in g_stacked:
            g_parts.append(g_stacked[name])
            continue
        per_layer = [gm[l][name] for l in range(DEPTH)]
        if name == "w_in":
            per_layer = [_w_in_from_padded(g) for g in per_layer]
        g_parts.append(jnp.stack([parts_of(g, axis) for g in per_layer]))
    from_sibling = _swap_with_sibling("swap_layers", g_parts, pick=lambda c: 1 - c)
    to_chips = [_add_layers(name, g, r) for (name, _, _), g, r in zip(MATS, g_parts, from_sibling)]
    received = _scatter_chips(to_chips)
    mine = [_sum_parts(name, own, got) for (name, _, _), own, got in zip(MATS, to_chips, received)]
    other = _swap_with_sibling("swap_reduced", mine)
    south = lax.axis_index("c") == 0
    g_mats = {name: jnp.stack([jnp.where(south, a, b), jnp.where(south, b, a)])
              for (name, _, _), a, b in zip(MATS, mine, other)}

    grads, deltas, new_m, new_v = {}, {}, {}, {}
    for name, _, _ in MATS:
        g = g_mats[name]
        cols = g.shape[-1]
        d, nm, nv = _adamw(name, w[name].reshape(-1, cols), g.reshape(-1, cols), mom[name].reshape(-1, cols),
                           var[name].reshape(-1, cols))
        grads[name] = g
        deltas[name], new_m[name], new_v[name] = (t.reshape(g.shape) for t in (d, nm, nv))
    g_s, d_s, m_s, v_s = _small_update(small_sum, w, mom, var)
    for name in g_s:
        grads[name], deltas[name], new_m[name], new_v[name] = g_s[name], d_s[name], m_s[name], v_s[name]

    return (loss_out, grad_x[None], *[grads[n] for n in WEIGHT_ORDER], *[deltas[n] for n in WEIGHT_ORDER],
            *[new_m[n] for n in WEIGHT_ORDER], *[new_v[n] for n in WEIGHT_ORDER])
```

```python
import functools
import math

import jax
import jax.numpy as jnp
from jax import lax
from jax.experimental import pallas as pl
from jax.experimental.pallas import tpu as pltpu

F32 = jnp.float32
BF16 = jnp.bfloat16
MESH = pl.DeviceIdType.MESH

D_MODEL = 1024
DEPTH = 2
SSD_DIM = 512
SSD_HEAD_DIM = 64
SSD_HEADS = 8
SSD_GROUPS = 2
SSD_STATE = 64
CONV_K = 4
CHUNK = 128
SB_DIM = 512
SB_HEAD_DIM = 64
XA_HEADS = 4
XA_HEAD_DIM = 128
XA_DIM = 512
D_FF = 4096
EPS = 1e-5
GN = SSD_GROUPS * SSD_STATE
CONV_DIM = SSD_DIM + 2 * GN
IN_DIM = SSD_DIM + CONV_DIM + SSD_HEADS + 3 * SB_DIM
LANES = 128
DT_PAD = LANES
IN_PAD = SSD_DIM + CONV_DIM + 3 * SB_DIM + DT_PAD
Q_OFF = SSD_DIM + CONV_DIM
DT_OFF = Q_OFF + 3 * SB_DIM
HALO = 8

ADAM_LR = 0.001
ADAM_B1 = 0.9
ADAM_B2 = 0.999
ADAM_EPS = 1e-08
ADAM_WD = 0.01
ADAM_STEP = 10

N_CHIPS = 4
N_DEV = 8
PACK_COLS = 1024
VMEM_LIMIT = 56 * 1024 * 1024

MATS = (
    ("w_in", (D_MODEL, IN_DIM), 1),
    ("w_out", (D_MODEL, D_MODEL), 0),
    ("w_xq", (D_MODEL, XA_DIM), 0),
    ("w_xk", (D_MODEL, XA_DIM), 0),
    ("w_xv", (D_MODEL, XA_DIM), 0),
    ("w_xo", (XA_DIM, D_MODEL), 1),
    ("w_ff1", (D_MODEL, D_FF), 1),
    ("w_ff2", (D_FF, D_MODEL), 0),
)


SMALL_ROWS = 24
_SM_PER_LAYER = 7
_SM_FINAL = 14
_SM_CONVW = 15
_SM_LOSS = 23


_NN = ((1,), (0,))
_NT = ((1,), (1,))
_TN = ((0,), (0,))


def _dg(a, b, dims):
    return lax.dot_general(a.astype(BF16), b.astype(BF16), (dims, ((), ())), preferred_element_type=F32)


@jax.custom_vjp
def mm_nn(a, b):
    return _dg(a, b, _NN)


@jax.custom_vjp
def mm_nt(a, b):
    return _dg(a, b, _NT)


@jax.custom_vjp
def mm_tn(a, b):
    return _dg(a, b, _TN)


def _nn_fwd(a, b):
    return _dg(a, b, _NN), (a, b)


def _nn_bwd(res, g):
    a, b = res
    return mm_nt(g, b).astype(a.dtype), mm_tn(a, g).astype(b.dtype)


def _nt_fwd(a, b):
    return _dg(a, b, _NT), (a, b)


def _nt_bwd(res, g):
    a, b = res
    return mm_nn(g, b).astype(a.dtype), mm_tn(g, a).astype(b.dtype)


def _tn_fwd(a, b):
    return _dg(a, b, _TN), (a, b)


def _tn_bwd(res, g):
    a, b = res
    return mm_nt(b, g).astype(a.dtype), mm_nn(a, g).astype(b.dtype)


mm_nn.defvjp(_nn_fwd, _nn_bwd)
mm_nt.defvjp(_nt_fwd, _nt_bwd)
mm_tn.defvjp(_tn_fwd, _tn_bwd)


def _rms(x, g):
    return x * lax.rsqrt(jnp.mean(x * x, axis=-1, keepdims=True) + EPS) * g


def _params(sem=None, vmem=VMEM_LIMIT):
    return pltpu.CompilerParams(dimension_semantics=sem, vmem_limit_bytes=vmem)


def _rowcall(name, fn, rows, fulls, row_out, acc_out=(), tm=256):
    s = rows[0].shape[0]
    tm = min(tm, s)
    nt = s // tm
    n_r, n_f, n_ro, n_ao = len(rows), len(fulls), len(row_out), len(acc_out)

    def body(*refs):
        ins = [r[...] for r in refs[: n_r + n_f]]
        outs = fn(*ins)
        o_refs = refs[n_r + n_f:]
        for o_ref, val in zip(o_refs[:n_ro], outs[:n_ro]):
            o_ref[...] = val.astype(o_ref.dtype)
        if n_ao:
            first = pl.program_id(0) == 0

            @pl.when(first)
            def _():
                for o_ref, val in zip(o_refs[n_ro:], outs[n_ro:]):
                    o_ref[...] = val.astype(o_ref.dtype)

            @pl.when(jnp.logical_not(first))
            def _():
                for o_ref, val in zip(o_refs[n_ro:], outs[n_ro:]):
                    o_ref[...] += val.astype(o_ref.dtype)

    in_specs = [pl.BlockSpec((tm, a.shape[1]), lambda i: (i, 0)) for a in rows]
    in_specs += [pl.BlockSpec(a.shape, lambda i: (0, 0), pipeline_mode=pl.Buffered(1)) for a in fulls]
    out_specs = [pl.BlockSpec((tm, c), lambda i: (i, 0)) for c, _ in row_out]
    out_specs += [pl.BlockSpec(shape, lambda i: (0, 0)) for shape, _ in acc_out]
    out_shape = [jax.ShapeDtypeStruct((s, c), dt) for c, dt in row_out]
    out_shape += [jax.ShapeDtypeStruct(shape, dt) for shape, dt in acc_out]
    return pl.pallas_call(
        body, name=name, grid=(nt,), in_specs=in_specs, out_specs=out_specs, out_shape=out_shape,
        compiler_params=_params(("arbitrary",)),
    )(*rows, *fulls)


def _mm_tn_call(name, a, b, tm, tn, tk):
    s, m = a.shape
    n = b.shape[1]
    tk = min(tk, s)

    def body(a_ref, b_ref, o_ref):
        d = _dg(a_ref[...], b_ref[...], _TN)
        first = pl.program_id(2) == 0

        @pl.when(first)
        def _():
            o_ref[...] = d

        @pl.when(jnp.logical_not(first))
        def _():
            o_ref[...] += d

    return pl.pallas_call(
        body, name=name, grid=(m // tm, n // tn, s // tk),
        in_specs=[pl.BlockSpec((tk, tm), lambda i, j, k: (k, i)), pl.BlockSpec((tk, tn), lambda i, j, k: (k, j))],
        out_specs=pl.BlockSpec((tm, tn), lambda i, j, k: (i, j)),
        out_shape=jax.ShapeDtypeStruct((m, n), F32),
        compiler_params=_params(("parallel", "parallel", "arbitrary")),
    )(a, b)


def _mm_tn_parts(name, a, b, layer, buf, by_cols, tm=512, tk=2048):
    s, m = a.shape
    n = b.shape[1]
    r, c = (m, n // N_CHIPS) if by_cols else (m // N_CHIPS, n)
    per = r // tm
    tk = min(tk, s)

    def body(*refs):
        a_ref, b_ref, o_ref = refs[0], refs[1], refs[-1]
        d = _dg(a_ref[...], b_ref[...], _TN)
        first = pl.program_id(2) == 0

        @pl.when(first)
        def _():
            o_ref[0, 0] = d

        @pl.when(jnp.logical_not(first))
        def _():
            o_ref[0, 0] += d

    if by_cols:
        out_map = lambda i, j, k: (layer, j, i, 0)
    else:
        out_map = lambda i, j, k: (layer, i // per, i % per, 0)
    in_specs = [pl.BlockSpec((tk, tm), lambda i, j, k: (k, i)), pl.BlockSpec((tk, c), lambda i, j, k: (k, j))]
    args = [a, b]
    aliases = {}
    if buf is not None:
        in_specs.append(pl.BlockSpec(memory_space=pl.ANY))
        args.append(buf)
        aliases = {2: 0}
    return pl.pallas_call(
        body, name=name, grid=(m // tm, n // c, s // tk), in_specs=in_specs,
        out_specs=pl.BlockSpec((1, 1, tm, c), out_map),
        out_shape=jax.ShapeDtypeStruct((DEPTH, N_CHIPS, r, c), F32), input_output_aliases=aliases,
        compiler_params=_params(("parallel", "parallel", "arbitrary")),
    )(*args)


def _proj_tile(h, g, w):
    p = mm_nn(_rms(h, g), w)
    return (p[:, :SSD_DIM], p[:, SSD_DIM:Q_OFF], p[:, Q_OFF:Q_OFF + SB_DIM],
            p[:, Q_OFF + SB_DIM:Q_OFF + 2 * SB_DIM], p[:, Q_OFF + 2 * SB_DIM:DT_OFF], p[:, DT_OFF:])


def _proj_fwd(tag, h, g, w):
    return _rowcall(
        "proj_fwd" + tag, _proj_tile, [h], [g, w],
        [(SSD_DIM, F32), (CONV_DIM, F32), (SB_DIM, BF16), (SB_DIM, BF16), (SB_DIM, BF16), (DT_PAD, F32)], tm=512)


def _proj_bwd(tag, h, dh_out, dz, dxbc, dq, dk, dv, ddt, g, w):
    def fn(h, dh_out, dz, dxbc, dq, dk, dv, ddt, g, w):
        dp = jnp.concatenate([dz.astype(BF16), dxbc.astype(BF16), dq.astype(BF16), dk.astype(BF16),
                              dv.astype(BF16), ddt.astype(BF16)], axis=1)
        hn, vjp = jax.vjp(_rms, h, g)
        dh, dg = vjp(mm_nt(dp, w))
        return dh_out + dh, hn, dp, dg

    return _rowcall(
        "proj_bwd" + tag, fn, [h, dh_out, dz, dxbc, dq, dk, dv, ddt], [g, w],
        [(D_MODEL, F32), (D_MODEL, BF16), (IN_PAD, BF16)], [((1, D_MODEL), F32)], tm=256)


def _shift_down(x, tail, j):
    if j == 0:
        return x
    r = pltpu.roll(x, j, 0)
    rt = pltpu.roll(tail, j, 0)
    row = lax.broadcasted_iota(jnp.int32, (HALO, x.shape[1]), 0)
    first = jnp.where(row < j, rt, r[:HALO])
    if x.shape[0] == HALO:
        return first
    return jnp.concatenate([first, r[HALO:]], axis=0)


def _shift_up(x, head, j):
    if j == 0:
        return x
    n = x.shape[0]
    r = pltpu.roll(x, n - j, 0)
    rh = pltpu.roll(head, HALO - j, 0)
    row = lax.broadcasted_iota(jnp.int32, (HALO, x.shape[1]), 0)
    return jnp.concatenate([r[:n - HALO], jnp.where(row >= HALO - j, rh, r[n - HALO:])], axis=0)


def _conv_pre(x, tail, w, b):
    acc = b + w[CONV_K - 1:CONV_K] * x
    for j in range(1, CONV_K):
        acc = acc + w[CONV_K - 1 - j:CONV_K - j] * _shift_down(x, tail, j)
    return acc


def _dsilu(p):
    s = jax.nn.sigmoid(p)
    return s * (1.0 + p * (1.0 - s))


def _conv_fwd(tag, xbc, w, b, tc=512):
    s, c = xbc.shape
    tc = min(tc, s)
    per = tc // HALO

    def body(x_ref, prev_ref, w_ref, b_ref, o_ref):
        tail = jnp.where(pl.program_id(0) > 0, prev_ref[...], 0.0)
        o_ref[...] = jax.nn.silu(_conv_pre(x_ref[...], tail, w_ref[...], b_ref[...]))

    return pl.pallas_call(
        body, name="conv_fwd" + tag, grid=(s // tc,),
        in_specs=[pl.BlockSpec((tc, c), lambda i: (i, 0)),
                  pl.BlockSpec((HALO, c), lambda i: (jnp.maximum(i * per - 1, 0), 0)),
                  pl.BlockSpec((CONV_K, c), lambda i: (0, 0)), pl.BlockSpec((1, c), lambda i: (0, 0))],
        out_specs=pl.BlockSpec((tc, c), lambda i: (i, 0)),
        out_shape=jax.ShapeDtypeStruct((s, c), F32),
        compiler_params=_params(("arbitrary",)),
    )(xbc, xbc, w, b)


def _conv_bwd(tag, xbc, dact, w, b, tc=512):
    s, c = xbc.shape
    tc = min(tc, s)
    per = tc // HALO
    nt = s // tc
    last_blk = s // HALO - 1

    def body(x_ref, prev_ref, next_ref, d_ref, dnext_ref, w_ref, b_ref, dx_ref, dw_ref, db_ref):
        i = pl.program_id(0)
        x = x_ref[...]
        wv = w_ref[...]
        tail = jnp.where(i > 0, prev_ref[...], 0.0)
        dpre = d_ref[...] * _dsilu(_conv_pre(x, tail, wv, b_ref[...]))
        pre_n = _conv_pre(next_ref[...], x[tc - HALO:], wv, b_ref[...])
        dpre_n = jnp.where(i < nt - 1, dnext_ref[...] * _dsilu(pre_n), 0.0)
        dx = wv[CONV_K - 1:CONV_K] * dpre
        for j in range(1, CONV_K):
            dx = dx + wv[CONV_K - 1 - j:CONV_K - j] * _shift_up(dpre, dpre_n, j)
        dx_ref[...] = dx
        dws = [jnp.sum(dpre * _shift_down(x, tail, CONV_K - 1 - k), axis=0, keepdims=True) for k in range(CONV_K)]
        dwv = jnp.concatenate(dws, axis=0)
        dbv = jnp.sum(dpre, axis=0, keepdims=True)

        @pl.when(i == 0)
        def _():
            dw_ref[...] = dwv
            db_ref[...] = dbv

        @pl.when(i > 0)
        def _():
            dw_ref[...] += dwv
            db_ref[...] += dbv

    tile = pl.BlockSpec((tc, c), lambda i: (i, 0))
    prev = pl.BlockSpec((HALO, c), lambda i: (jnp.maximum(i * per - 1, 0), 0))
    nxt = pl.BlockSpec((HALO, c), lambda i: (jnp.minimum((i + 1) * per, last_blk), 0))
    return pl.pallas_call(
        body, name="conv_bwd" + tag, grid=(nt,),
        in_specs=[tile, prev, nxt, tile, nxt, pl.BlockSpec((CONV_K, c), lambda i: (0, 0)),
                  pl.BlockSpec((1, c), lambda i: (0, 0))],
        out_specs=[tile, pl.BlockSpec((CONV_K, c), lambda i: (0, 0)), pl.BlockSpec((1, c), lambda i: (0, 0))],
        out_shape=[jax.ShapeDtypeStruct((s, c), F32), jax.ShapeDtypeStruct((CONV_K, c), F32),
                   jax.ShapeDtypeStruct((1, c), F32)],
        compiler_params=_params(("arbitrary",)),
    )(xbc, xbc, xbc, dact, dact, w, b)


def _ssd_chunk(xs, bm, cm, dtr, z, dt_bias, a_log, d_skip, g, s_prev):
    n = CHUNK
    row = lax.broadcasted_iota(jnp.int32, (n, n), 0)
    col = lax.broadcasted_iota(jnp.int32, (n, n), 1)
    causal = row >= col
    dt = jax.nn.softplus(dtr + dt_bias)
    a_c = dt * (-jnp.exp(a_log))
    hi = lax.Precision.HIGHEST
    a_cum = jnp.dot(causal.astype(F32), a_c, precision=hi, preferred_element_type=F32)
    a_cum_t = lax.dot_general(a_c, (row <= col).astype(F32), (_TN, ((), ())), precision=hi,
                              preferred_element_type=F32)
    p, st = SSD_HEAD_DIM, SSD_STATE
    cb = [mm_nt(cm[:, k * st:(k + 1) * st], bm[:, k * st:(k + 1) * st]) for k in range(SSD_GROUPS)]
    ys, s_new = [], []
    for h in range(SSD_HEADS):
        k = h // (SSD_HEADS // SSD_GROUPS)
        acol = a_cum[:, h:h + 1]
        decay = jnp.exp(jnp.where(causal, acol - a_cum_t[h:h + 1, :], -jnp.inf))
        xh = xs[:, h * p:(h + 1) * p]
        xdt = xh * dt[:, h:h + 1]
        a_last = a_cum[n - 1:n, h:h + 1]
        sp = s_prev[h * p:(h + 1) * p, :]
        bg = bm[:, k * st:(k + 1) * st]
        cg = cm[:, k * st:(k + 1) * st]
        s_new.append(sp * jnp.exp(a_last) + mm_tn(xdt * jnp.exp(a_last - acol), bg))
        y = mm_nn(cb[k] * decay, xdt) + mm_nt(cg, sp) * jnp.exp(acol) + d_skip[:, h:h + 1] * xh
        ys.append(y)
    y = jnp.concatenate(ys, axis=1) * jax.nn.silu(z)
    return _rms(y, g), jnp.concatenate(s_new, axis=0)


def _split_xbc(t):
    return t[:, :SSD_DIM], t[:, SSD_DIM:SSD_DIM + GN], t[:, SSD_DIM + GN:]


def _ssd_fwd(tag, xact, dtr, z, dt_bias, a_log, d_skip, g):
    s = xact.shape[0]
    nc = s // CHUNK
    srows = SSD_HEADS * SSD_HEAD_DIM

    def body(x_ref, dt_ref, z_ref, b_ref, al_ref, ds_ref, g_ref, y_ref, st_ref, state):
        @pl.when(pl.program_id(0) == 0)
        def _():
            state[...] = jnp.zeros_like(state)

        sp = state[...]
        st_ref[0] = sp
        xs, bm, cm = _split_xbc(x_ref[...])
        y, sn = _ssd_chunk(xs, bm, cm, dt_ref[...][:, :SSD_HEADS], z_ref[...], b_ref[...], al_ref[...],
                           ds_ref[...], g_ref[...], sp)
        y_ref[...] = y
        state[...] = sn

    small = pl.BlockSpec((1, SSD_HEADS), lambda i: (0, 0))
    return pl.pallas_call(
        body, name="ssd_fwd" + tag, grid=(nc,),
        in_specs=[pl.BlockSpec((CHUNK, CONV_DIM), lambda i: (i, 0)), pl.BlockSpec((CHUNK, DT_PAD), lambda i: (i, 0)),
                  pl.BlockSpec((CHUNK, SSD_DIM), lambda i: (i, 0)), small, small, small,
                  pl.BlockSpec((1, SSD_DIM), lambda i: (0, 0))],
        out_specs=[pl.BlockSpec((CHUNK, SSD_DIM), lambda i: (i, 0)),
                   pl.BlockSpec((1, srows, SSD_STATE), lambda i: (i, 0, 0))],
        out_shape=[jax.ShapeDtypeStruct((s, SSD_DIM), F32), jax.ShapeDtypeStruct((nc, srows, SSD_STATE), F32)],
        scratch_shapes=[pltpu.VMEM((srows, SSD_STATE), F32)],
        compiler_params=_params(("arbitrary",)),
    )(xact, dtr, z, dt_bias, a_log, d_skip, g)


def _ssd_bwd(tag, xact, dtr, z, states, dy, dt_bias, a_log, d_skip, g):
    s = xact.shape[0]
    nc = s // CHUNK
    srows = SSD_HEADS * SSD_HEAD_DIM

    def body(x_ref, dt_ref, z_ref, sp_ref, dy_ref, b_ref, al_ref, ds_ref, g_ref,
             dx_ref, ddt_ref, dz_ref, db_ref, dal_ref, dds_ref, dg_ref, dstate):
        first = pl.program_id(0) == 0

        @pl.when(first)
        def _():
            dstate[...] = jnp.zeros_like(dstate)

        xs, bm, cm = _split_xbc(x_ref[...])
        _, vjp = jax.vjp(_ssd_chunk, xs, bm, cm, dt_ref[...][:, :SSD_HEADS], z_ref[...], b_ref[...], al_ref[...],
                         ds_ref[...], g_ref[...], sp_ref[0])
        dxs, dbm, dcm, ddt, dz, db, dal, dds, dg, dsp = vjp((dy_ref[...], dstate[...]))
        dx_ref[...] = jnp.concatenate([dxs, dbm, dcm], axis=1)
        ddt_ref[...] = jnp.concatenate([ddt, jnp.zeros((CHUNK, DT_PAD - SSD_HEADS), F32)], axis=1)
        dz_ref[...] = dz
        dstate[...] = dsp

        @pl.when(first)
        def _():
            db_ref[...] = db
            dal_ref[...] = dal
            dds_ref[...] = dds
            dg_ref[...] = dg

        @pl.when(jnp.logical_not(first))
        def _():
            db_ref[...] += db
            dal_ref[...] += dal
            dds_ref[...] += dds
            dg_ref[...] += dg

    def rev(c):
        return lambda i: (nc - 1 - i, 0)

    small = pl.BlockSpec((1, SSD_HEADS), lambda i: (0, 0))
    gspec = pl.BlockSpec((1, SSD_DIM), lambda i: (0, 0))
    return pl.pallas_call(
        body, name="ssd_bwd" + tag, grid=(nc,),
        in_specs=[pl.BlockSpec((CHUNK, CONV_DIM), rev(0)), pl.BlockSpec((CHUNK, DT_PAD), rev(0)),
                  pl.BlockSpec((CHUNK, SSD_DIM), rev(0)),
                  pl.BlockSpec((1, srows, SSD_STATE), lambda i: (nc - 1 - i, 0, 0)),
                  pl.BlockSpec((CHUNK, SSD_DIM), rev(0)), small, small, small, gspec],
        out_specs=[pl.BlockSpec((CHUNK, CONV_DIM), rev(0)), pl.BlockSpec((CHUNK, DT_PAD), rev(0)),
                   pl.BlockSpec((CHUNK, SSD_DIM), rev(0)), small, small, small, gspec],
        out_shape=[jax.ShapeDtypeStruct((s, CONV_DIM), F32), jax.ShapeDtypeStruct((s, DT_PAD), F32),
                   jax.ShapeDtypeStruct((s, SSD_DIM), F32), jax.ShapeDtypeStruct((1, SSD_HEADS), F32),
                   jax.ShapeDtypeStruct((1, SSD_HEADS), F32), jax.ShapeDtypeStruct((1, SSD_HEADS), F32),
                   jax.ShapeDtypeStruct((1, SSD_DIM), F32)],
        scratch_shapes=[pltpu.VMEM((srows, SSD_STATE), F32)],
        compiler_params=_params(("arbitrary",)),
    )(xact, dtr, z, states, dy, dt_bias, a_log, d_skip, g)


TQ = 128
TK = 128
SB_SCALE = 1.0 / math.sqrt(SB_HEAD_DIM)


def _split2(x):
    hi = x.astype(BF16)
    return hi, (x - hi.astype(F32)).astype(BF16)


TK_WIDE = 256
SB_UNDERFLOW = -110.0


def _sb_logits(qhs, kb, t0, s0, masked):
    zs = [_dg(qh, kb, _NT) for qh in qhs]
    mask = None
    if masked:
        t_pos = t0 + lax.broadcasted_iota(jnp.int32, zs[0].shape, 0)
        s_pos = s0 + lax.broadcasted_iota(jnp.int32, zs[0].shape, 1)
        mask = s_pos < t_pos
    lbs = [jnp.minimum(z, 0.0) - jnp.log(1.0 + jnp.exp(-jnp.abs(z))) for z in zs]
    lss = [lb - z for lb, z in zip(lbs, zs)]
    if masked:
        lss = [jnp.where(mask, ls, 0.0) for ls in lss]
    return lbs, lss, mask


def _running_sums(xs, starts, u, reverse, two_terms=True):
    nsub = xs[0].shape[1] // TK
    order = list(reversed(range(nsub))) if reverse else list(range(nsub))
    chunks = [[x[:, c * TK:(c + 1) * TK] for c in range(nsub)] for x in xs]
    sums = [[(_lane_sums(xc, u) if two_terms else _dg(xc, u, _NN)) for xc in row] for row in chunks]
    out = []
    for row, srow, run in zip(chunks, sums, starts):
        parts = [None] * nsub
        for c in order:
            parts[c] = run + srow[c]
            run = run + jnp.sum(row[c], axis=1, keepdims=True)
        out.append((parts[0] if nsub == 1 else jnp.concatenate(parts, axis=1), run))
    return out


def _lane_sums(x, u):
    hi, lo = _split2(x)
    return _dg(hi, u, _NN) + _dg(lo, u, _NN)


def _tri(cmp):
    j = lax.broadcasted_iota(jnp.int32, (TK, TK), 0)
    s = lax.broadcasted_iota(jnp.int32, (TK, TK), 1)
    return cmp(j, s).astype(BF16)


def _sb_fwd(tag, q, k, v):
    s = q.shape[0]
    npair = SB_DIM // LANES

    wide = min(TK_WIDE, s)
    per = wide // TQ

    def body(q_ref, k_ref, v_ref, o_ref, t_ref):
        tq = pl.program_id(1)
        diag = tq // per
        qp = q_ref[...]
        lane = lax.broadcasted_iota(jnp.int32, (1, LANES), 1)
        u_gt = _tri(lambda j, s: j > s)
        heads = range(LANES // SB_HEAD_DIM)
        hms = [(lane // SB_HEAD_DIM) == hh for hh in heads]
        qhs = [jnp.where(hm, qp, jnp.zeros_like(qp)) * SB_SCALE for hm in hms]

        def block(wb, carry, masked):
            off = pl.multiple_of(wb * wide, wide)
            kb = k_ref[pl.ds(off, wide), :]
            vb = v_ref[pl.ds(off, wide), :]
            lbs, lss, mask = _sb_logits(qhs, kb, tq * TQ, wb * wide, masked)
            sums = _running_sums(lss, [c[0] for c in carry], u_gt, reverse=True)
            ws = [jnp.exp(lb + later) for lb, (later, _) in zip(lbs, sums)]
            if masked:
                ws = [jnp.where(mask, w, 0.0) for w in ws]
            pvs = [_dg(w, vb, _NN) for w in ws]
            return tuple((r, c[1] + pv) for (_, r), c, pv in zip(sums, carry, pvs))

        def more(c):
            alive = jnp.max(c[1][0][0])
            for hh in heads[1:]:
                alive = jnp.maximum(alive, jnp.max(c[1][hh][0]))
            return jnp.logical_and(c[0] >= 0, alive > SB_UNDERFLOW)

        start = tuple((jnp.zeros((TQ, 1), F32), jnp.zeros((TQ, LANES), F32)) for _ in heads)
        wb, done = lax.while_loop(more, lambda c: (c[0] - 1, block(c[0], c[1], False)),
                                  (diag - 1, block(diag, start, True)))
        first = (wb + 1).astype(F32)
        out = jnp.zeros((TQ, LANES), F32)
        tot = jnp.zeros((TQ, LANES), F32)
        for hh in heads:
            r, acc = done[hh]
            out = out + jnp.where(hms[hh], acc, 0.0)
            tot = tot + jnp.where(hms[hh], jnp.where(lane % SB_HEAD_DIM == 1, first, r), 0.0)
        o_ref[...] = out
        t_ref[...] = tot

    tile = pl.BlockSpec((TQ, LANES), lambda p, t: (t, p))
    full = pl.BlockSpec((s, LANES), lambda p, t: (0, p))
    return pl.pallas_call(
        body, name="sb_fwd" + tag, grid=(npair, s // TQ),
        in_specs=[tile, full, full], out_specs=[tile, tile],
        out_shape=[jax.ShapeDtypeStruct((s, SB_DIM), F32)] * 2,
        compiler_params=_params(("parallel", "arbitrary")),
    )(q, k, v)


def _sb_bwd(tag, q, k, v, tot, do):
    s = q.shape[0]
    npair = SB_DIM // LANES

    wide = min(TK_WIDE, s)
    per = wide // TQ

    def body(q_ref, k_ref, v_ref, t_ref, do_ref, dq_ref, dk_ref, dv_ref):
        tq = pl.program_id(1)
        diag = tq // per

        @pl.when(tq == 0)
        def _():
            dk_ref[...] = jnp.zeros_like(dk_ref)
            dv_ref[...] = jnp.zeros_like(dv_ref)

        qp = q_ref[...]
        dop = do_ref[...]
        totp = t_ref[...]
        lane = lax.broadcasted_iota(jnp.int32, (1, LANES), 1)
        u_le = _tri(lambda j, s: j <= s)
        u_lt = _tri(lambda j, s: j < s)
        heads = range(LANES // SB_HEAD_DIM)
        hms = [(lane // SB_HEAD_DIM) == hh for hh in heads]
        qhs = [jnp.where(hm, qp, jnp.zeros_like(qp)) * SB_SCALE for hm in hms]
        dohs = [jnp.where(hm, dop, 0.0).astype(BF16) for hm in hms]
        totals = [jnp.sum(jnp.where(lane == hh * SB_HEAD_DIM, totp, 0.0), axis=1, keepdims=True) for hh in heads]
        first = jnp.max(jnp.where(lane == 1, totp, 0.0)).astype(jnp.int32)

        def block(wb, carry, masked):
            off = pl.multiple_of(wb * wide, wide)
            kb = k_ref[pl.ds(off, wide), :]
            vb = v_ref[pl.ds(off, wide), :]
            lbs, lss, mask = _sb_logits(qhs, kb, tq * TQ, wb * wide, masked)
            dws = [_dg(doh, vb, _NT) for doh in dohs]
            pres = _running_sums(lss, [c[0] for c in carry], u_le, reverse=False)
            ws = [jnp.exp(lb + (total - before)) for lb, total, (before, _) in zip(lbs, totals, pres)]
            if masked:
                ws = [jnp.where(mask, w, 0.0) for w in ws]
            gs = [w * dw for w, dw in zip(ws, dws)]
            lefts = _running_sums(gs, [c[1] for c in carry], u_lt, reverse=False, two_terms=False)
            dzs = [g - jnp.exp(lb) * (g + g_left) for g, lb, (g_left, _) in zip(gs, lbs, lefts)]
            if masked:
                dzs = [jnp.where(mask, dz, 0.0) for dz in dzs]
            dzbs = [dz.astype(BF16) for dz in dzs]
            dks = [_dg(dzb, qh, _TN) for dzb, qh in zip(dzbs, qhs)]
            dvs = [_dg(w, doh, _TN) for w, doh in zip(ws, dohs)]
            dqs = [_dg(dzb, kb, _NN) for dzb in dzbs]
            dk_ref[pl.ds(off, wide), :] += functools.reduce(jnp.add, dks)
            dv_ref[pl.ds(off, wide), :] += functools.reduce(jnp.add, dvs)
            return tuple((pre, gc, c[2] + dq) for (_, pre), (_, gc), c, dq in zip(pres, lefts, carry, dqs))

        zero = jnp.zeros((TQ, 1), F32)
        start = tuple((zero, zero, jnp.zeros((TQ, LANES), F32)) for _ in heads)
        done = block(diag, lax.fori_loop(first, diag, lambda j, c: block(j, c, False), start), True)
        dq = jnp.zeros((TQ, LANES), F32)
        for hh in heads:
            dq = dq + jnp.where(hms[hh], done[hh][2], 0.0)
        dq_ref[...] = dq * SB_SCALE

    tile = pl.BlockSpec((TQ, LANES), lambda p, t: (t, p))
    full = pl.BlockSpec((s, LANES), lambda p, t: (0, p))
    return pl.pallas_call(
        body, name="sb_bwd" + tag, grid=(npair, s // TQ),
        in_specs=[tile, full, full, tile, tile],
        out_specs=[tile, full, full],
        out_shape=[jax.ShapeDtypeStruct((s, SB_DIM), F32)] * 3,
        compiler_params=_params(("parallel", "arbitrary")),
    )(q, k, v, tot, do)


def _out_tile(y_ssd, o, sb_g, w_out):
    y_all = jnp.concatenate([y_ssd, _rms(o, sb_g)], axis=1)
    return mm_nn(y_all, w_out)


def _out_fwd(tag, h, y_ssd, o, sb_g, w_out):
    return _rowcall("out_fwd" + tag, lambda h, y, o, g, w: (h + _out_tile(y, o, g, w),),
                    [h, y_ssd, o], [sb_g, w_out], [(D_MODEL, F32)], tm=512)[0]


def _out_bwd(tag, y_ssd, o, dh, sb_g, w_out):
    def fn(y, o, dh, g, w):
        _, vjp = jax.vjp(_out_tile, y, o, g, w.astype(F32))
        return vjp(dh)

    return _rowcall("out_bwd" + tag, fn, [y_ssd, o, dh], [sb_g, w_out], [(SSD_DIM, F32), (SB_DIM, F32)],
                    [((1, SB_DIM), F32), ((D_MODEL, D_MODEL), F32)], tm=256)


def _mem_tile(mem, g, w_k, w_v):
    m = _rms(mem, g)
    return mm_nn(m, w_k), mm_nn(m, w_v)


def _mem_fwd(tag, mem, g, w_k, w_v):
    return _rowcall("mem_fwd" + tag, _mem_tile, [mem], [g, w_k, w_v], [(XA_DIM, F32), (XA_DIM, F32)], tm=256)


def _mem_bwd(tag, mem, dkx, dvx, g, w_k, w_v):
    def fn(mem, dkx, dvx, g, w_k, w_v):
        _, vjp = jax.vjp(lambda g, a, b: _mem_tile(mem, g, a, b), g, w_k.astype(F32), w_v.astype(F32))
        return vjp((dkx, dvx))

    return _rowcall("mem_bwd" + tag, fn, [mem, dkx, dvx], [g, w_k, w_v], [],
                    [((1, D_MODEL), F32), ((D_MODEL, XA_DIM), F32), ((D_MODEL, XA_DIM), F32)], tm=256)


def _xattn_tile(h, g, w_q, kx, vx, w_o):
    q = mm_nn(_rms(h, g), w_q)
    scale = 1.0 / math.sqrt(XA_HEAD_DIM)
    outs = []
    for i in range(XA_HEADS):
        sl = slice(i * XA_HEAD_DIM, (i + 1) * XA_HEAD_DIM)
        p = jax.nn.softmax(mm_nt(q[:, sl], kx[:, sl]) * scale, axis=-1)
        outs.append(mm_nn(p, vx[:, sl]))
    return mm_nn(jnp.concatenate(outs, axis=1), w_o)


def _xattn_fwd(tag, h, g, w_q, kx, vx, w_o):
    return _rowcall("xattn_fwd" + tag, lambda h, g, wq, kx, vx, wo: (h + _xattn_tile(h, g, wq, kx, vx, wo),),
                    [h], [g, w_q, kx, vx, w_o], [(D_MODEL, F32)], tm=512)[0]


def _xattn_bwd(tag, h, dh_out, g, w_q, kx, vx, w_o):
    def fn(h, dh_out, g, w_q, kx, vx, w_o):
        _, vjp = jax.vjp(_xattn_tile, h, g, w_q.astype(F32), kx, vx, w_o.astype(F32))
        dh, dg, dwq, dkx, dvx, dwo = vjp(dh_out)
        return dh_out + dh, dg, dwq, dkx, dvx, dwo

    mlen = kx.shape[0]
    return _rowcall("xattn_bwd" + tag, fn, [h, dh_out], [g, w_q, kx, vx, w_o], [(D_MODEL, F32)],
                    [((1, D_MODEL), F32), ((D_MODEL, XA_DIM), F32), ((mlen, XA_DIM), F32), ((mlen, XA_DIM), F32),
                     ((XA_DIM, D_MODEL), F32)], tm=256)


def _mlp_fwd(tag, h, g, w1, w2):
    def fn(h, g, w1, w2):
        u = jnp.square(jnp.maximum(mm_nn(_rms(h, g), w1), 0.0))
        return (h + mm_nn(u, w2),)

    return _rowcall("mlp_fwd" + tag, fn, [h], [g, w1, w2], [(D_MODEL, F32)], tm=256)[0]


def _mlp_bwd(tag, h, dh_out, g, w1, w2):
    def fn(h, dh_out, g, w1, w2):
        hn, vjp = jax.vjp(_rms, h, g)
        r = jnp.maximum(mm_nn(hn, w1), 0.0)
        dob = dh_out.astype(BF16)
        dp = mm_nt(dob, w2) * (2.0 * r)
        dh, dg = vjp(mm_nt(dp, w1))
        return dh_out + dh, hn, dp, r * r, dob, dg

    return _rowcall("mlp_bwd" + tag, fn, [h, dh_out], [g, w1, w2],
                    [(D_MODEL, F32), (D_MODEL, BF16), (D_FF, BF16), (D_FF, BF16), (D_MODEL, BF16)],
                    [((1, D_MODEL), F32)], tm=256)


def _head(h, g, target):
    def lossfn(h, g, t):
        err = jnp.square(_rms(h, g) - t)
        return 0.5 * jnp.sum(jnp.mean(err, axis=-1))

    def fn(h, t, g):
        loss, vjp = jax.vjp(lambda h, g: lossfn(h, g, t), h, g)
        dh, dg = vjp(jnp.ones((), F32))
        return dh, jnp.full((1, LANES), loss, F32), dg

    return _rowcall("head", fn, [h, target], [g], [(D_MODEL, F32)], [((1, LANES), F32), ((1, D_MODEL), F32)], tm=512)


def _row(v):
    return v.reshape(1, -1)


def _local_step(x, mem, target, sw, mats):
    h = x
    saved = []
    for l in range(DEPTH):
        tag = str(l)
        m = mats[l]
        z, xbc, q, k, v, dtr = _proj_fwd(tag, h, _row(sw["norm_mix_g"][l]), m["w_in"])
        xact = _conv_fwd(tag, xbc, sw["conv_w"][l], _row(sw["conv_b"][l]))
        y_ssd, states = _ssd_fwd(tag, xact, dtr, z, _row(sw["dt_bias"][l]), _row(sw["a_log"][l]),
                                 _row(sw["d_skip"][l]), _row(sw["ssd_norm_g"][l]))
        o, sb_tot = _sb_fwd(tag, q, k, v)
        h1 = _out_fwd(tag, h, y_ssd, o, _row(sw["sb_norm_g"][l]), m["w_out"])
        kx, vx = _mem_fwd(tag, mem, _row(sw["norm_mem_g"][l]), m["w_xk"], m["w_xv"])
        h2 = _xattn_fwd(tag, h1, _row(sw["norm_xa_g"][l]), m["w_xq"], kx, vx, m["w_xo"])
        h3 = _mlp_fwd(tag, h2, _row(sw["norm_ff_g"][l]), m["w_ff1"], m["w_ff2"])
        saved.append((h, z, xbc, q, k, v, dtr, xact, y_ssd, states, o, sb_tot, h1, kx, vx, h2))
        h = h3

    dh, loss, d_final = _head(h, _row(sw["final_g"]), target)
    gm = [dict() for _ in range(DEPTH)]
    gs = {name: [None] * DEPTH for name in ("norm_mix_g", "conv_w", "conv_b", "dt_bias", "a_log", "d_skip",
                                            "ssd_norm_g", "sb_norm_g", "norm_xa_g", "norm_mem_g", "norm_ff_g")}
    g_ff1 = g_ff2 = None
    for l in reversed(range(DEPTH)):
        tag = str(l)
        m = mats[l]
        h0, z, xbc, q, k, v, dtr, xact, y_ssd, states, o, sb_tot, h1, kx, vx, h2 = saved[l]
        dh2, hn_b, dp_b, a_b, do_b, gs["norm_ff_g"][l] = _mlp_bwd(
            tag, h2, dh, _row(sw["norm_ff_g"][l]), m["w_ff1"], m["w_ff2"])
        g_ff1 = _mm_tn_parts("dw_ff1" + tag, hn_b, dp_b, l, g_ff1, True)
        g_ff2 = _mm_tn_parts("dw_ff2" + tag, a_b, do_b, l, g_ff2, False)
        dh1, gs["norm_xa_g"][l], gm[l]["w_xq"], dkx, dvx, gm[l]["w_xo"] = _xattn_bwd(
            tag, h1, dh2, _row(sw["norm_xa_g"][l]), m["w_xq"], kx, vx, m["w_xo"])
        gs["norm_mem_g"][l], gm[l]["w_xk"], gm[l]["w_xv"] = _mem_bwd(
            tag, mem, dkx, dvx, _row(sw["norm_mem_g"][l]), m["w_xk"], m["w_xv"])
        dy_ssd, do, gs["sb_norm_g"][l], gm[l]["w_out"] = _out_bwd(
            tag, y_ssd, o, dh1, _row(sw["sb_norm_g"][l]), m["w_out"])
        dq, dk, dv = _sb_bwd(tag, q, k, v, sb_tot, do)
        dxact, ddtr, dz, gs["dt_bias"][l], gs["a_log"][l], gs["d_skip"][l], gs["ssd_norm_g"][l] = _ssd_bwd(
            tag, xact, dtr, z, states, dy_ssd, _row(sw["dt_bias"][l]), _row(sw["a_log"][l]),
            _row(sw["d_skip"][l]), _row(sw["ssd_norm_g"][l]))
        dxbc, gs["conv_w"][l], gs["conv_b"][l] = _conv_bwd(tag, xbc, dxact, sw["conv_w"][l], _row(sw["conv_b"][l]))
        dh, hn_b, dp_b, gs["norm_mix_g"][l] = _proj_bwd(
            tag, h0, dh1, dz, dxbc, dq, dk, dv, ddtr, _row(sw["norm_mix_g"][l]), m["w_in"])
        gm[l]["w_in"] = _mm_tn_call("dw_in" + tag, hn_b, dp_b, 512, IN_PAD, 1024)
    gs["final_g"] = d_final
    return loss, dh, gm, gs, {"w_ff1": g_ff1, "w_ff2": g_ff2}


ANY = pl.BlockSpec(memory_space=pl.ANY)
VMEM_SPEC = pl.BlockSpec(memory_space=pltpu.VMEM)


def _place():
    return lax.axis_index("x"), lax.axis_index("y"), lax.axis_index("c")


def _other_chips(x, y):
    return [(1 - x, y), (x, 1 - y), (1 - x, 1 - y)]


def _gather_weights(shards):
    n = len(shards)

    def body(*refs):
        w_refs, o_refs = refs[:n], refs[n:2 * n]
        send_sems, recv_sems = refs[2 * n:]
        x, y, c = _place()
        me_chip = 2 * x + y
        sibling = (x, y, 1 - c)
        chips = _other_chips(x, y)

        def copy(idx, src, dst, to):
            return pltpu.make_async_remote_copy(src_ref=src, dst_ref=dst, send_sem=send_sems.at[idx],
                                                recv_sem=recv_sems.at[idx], device_id=to, device_id_type=MESH)

        first = [copy(6 * i + kk, w_refs[i].at[c], o_refs[i].at[c, me_chip], (cx, cy, c))
                 for i in range(n) for kk, (cx, cy) in enumerate(chips)]
        for cp in first:
            cp.start()
        passed = []
        for i in range(n):
            for kk, (cx, cy) in enumerate(chips):
                got = o_refs[i].at[c, 2 * cx + cy]
                copy(6 * i + kk, got, got, (x, y, c)).wait_recv()
                fwd = copy(6 * i + 3 + kk, got, got, sibling)
                fwd.start()
                passed.append(fwd)
        for i in range(n):
            for kk, (cx, cy) in enumerate(chips):
                got = o_refs[i].at[1 - c, 2 * cx + cy]
                copy(6 * i + 3 + kk, got, got, (x, y, c)).wait_recv()
        for cp in first + passed:
            cp.wait_send()

    return pl.pallas_call(
        body, name="gather_weights", in_specs=[ANY] * n, out_specs=[ANY] * n,
        out_shape=[jax.ShapeDtypeStruct((DEPTH, N_CHIPS) + s.shape[1:], s.dtype) for s in shards],
        scratch_shapes=[pltpu.SemaphoreType.DMA((6 * n,)), pltpu.SemaphoreType.DMA((6 * n,))],
    )(*shards)


def _gather_small(tag, buf):
    shape = buf.shape

    def body(b_ref, o_ref, sum_ref, send_sems, recv_sems, local_sem):
        x, y, c = _place()
        me = 4 * x + 2 * y + c
        mine = pltpu.make_async_copy(b_ref, o_ref.at[me], local_sem)
        mine.start()
        flips = [(dx, dy, dc) for dx in (0, 1) for dy in (0, 1) for dc in (0, 1) if (dx, dy, dc) != (0, 0, 0)]
        sends = []

        def peer(dx, dy, dc):
            return (1 - x if dx else x, 1 - y if dy else y, 1 - c if dc else c)

        for kk, flip in enumerate(flips):
            cp = pltpu.make_async_remote_copy(src_ref=b_ref, dst_ref=o_ref.at[me], send_sem=send_sems.at[kk],
                                              recv_sem=recv_sems.at[kk], device_id=peer(*flip), device_id_type=MESH)
            cp.start()
            sends.append(cp)
        for kk, flip in enumerate(flips):
            px, py, pc = peer(*flip)
            frm = 4 * px + 2 * py + pc
            pltpu.make_async_remote_copy(src_ref=b_ref, dst_ref=o_ref.at[frm], send_sem=send_sems.at[kk],
                                         recv_sem=recv_sems.at[kk], device_id=(x, y, c),
                                         device_id_type=MESH).wait_recv()
        for cp in sends:
            cp.wait_send()
        mine.wait()
        total = o_ref[0]
        for d in range(1, N_DEV):
            total = total + o_ref[d]
        sum_ref[...] = total

    return pl.pallas_call(
        body, name="gather_small" + tag, in_specs=[VMEM_SPEC], out_specs=[VMEM_SPEC, VMEM_SPEC],
        out_shape=[jax.ShapeDtypeStruct((N_DEV,) + shape, buf.dtype), jax.ShapeDtypeStruct(shape, buf.dtype)],
        scratch_shapes=[pltpu.SemaphoreType.DMA((N_DEV - 1,)), pltpu.SemaphoreType.DMA((N_DEV - 1,)),
                        pltpu.SemaphoreType.DMA],
    )(buf)


def _swap_with_sibling(name, arrays, pick=None):
    n = len(arrays)

    def body(*refs):
        a_refs, o_refs = refs[:n], refs[n:2 * n]
        send_sems, recv_sems = refs[2 * n:]
        x, y, c = _place()
        cps = [pltpu.make_async_remote_copy(
            src_ref=a_refs[i] if pick is None else a_refs[i].at[pick(c)], dst_ref=o_refs[i],
            send_sem=send_sems.at[i], recv_sem=recv_sems.at[i], device_id=(x, y, 1 - c), device_id_type=MESH)
            for i in range(n)]
        for cp in cps:
            cp.start()
        for cp in cps:
            cp.wait()

    return pl.pallas_call(
        body, name=name, in_specs=[ANY] * n, out_specs=[ANY] * n,
        out_shape=[jax.ShapeDtypeStruct(a.shape if pick is None else a.shape[1:], a.dtype) for a in arrays],
        scratch_shapes=[pltpu.SemaphoreType.DMA((n,)), pltpu.SemaphoreType.DMA((n,))],
    )(*arrays)


def _add_layers(tag, g, r, tr=256):
    _, _, rows, cols = g.shape
    tr = min(tr, rows)
    core = lax.axis_index("c").astype(jnp.int32).reshape(1)

    def body(c_ref, g_ref, r_ref, o_ref):
        o_ref[...] = (g_ref[0] + r_ref[...]).astype(o_ref.dtype)

    return pl.pallas_call(
        body, name="add_layers_" + tag,
        grid_spec=pltpu.PrefetchScalarGridSpec(
            num_scalar_prefetch=1, grid=(N_CHIPS, rows // tr),
            in_specs=[pl.BlockSpec((1, 1, tr, cols), lambda p, i, c_ref: (c_ref[0], p, i, 0)),
                      pl.BlockSpec((1, tr, cols), lambda p, i, c_ref: (p, i, 0))],
            out_specs=pl.BlockSpec((1, tr, cols), lambda p, i, c_ref: (p, i, 0))),
        out_shape=jax.ShapeDtypeStruct(r.shape, BF16),
        compiler_params=_params(("arbitrary", "arbitrary")),
    )(core, g, r)


def _scatter_chips(parts):
    n = len(parts)

    def body(*refs):
        s_refs, o_refs = refs[:n], refs[n:2 * n]
        send_sems, recv_sems = refs[2 * n:]
        x, y, c = _place()
        me_chip = 2 * x + y
        chips = _other_chips(x, y)
        sends = []
        for i in range(n):
            for kk, (cx, cy) in enumerate(chips):
                cp = pltpu.make_async_remote_copy(src_ref=s_refs[i].at[2 * cx + cy], dst_ref=o_refs[i].at[me_chip],
                                                  send_sem=send_sems.at[3 * i + kk], recv_sem=recv_sems.at[3 * i + kk],
                                                  device_id=(cx, cy, c), device_id_type=MESH)
                cp.start()
                sends.append(cp)
        for i in range(n):
            for kk, (cx, cy) in enumerate(chips):
                got = o_refs[i].at[2 * cx + cy]
                pltpu.make_async_remote_copy(src_ref=got, dst_ref=got, send_sem=send_sems.at[3 * i + kk],
                                             recv_sem=recv_sems.at[3 * i + kk], device_id=(x, y, c),
                                             device_id_type=MESH).wait_recv()
        for cp in sends:
            cp.wait_send()

    return pl.pallas_call(
        body, name="scatter_chips", in_specs=[ANY] * n, out_specs=[ANY] * n,
        out_shape=[jax.ShapeDtypeStruct(p.shape, p.dtype) for p in parts],
        scratch_shapes=[pltpu.SemaphoreType.DMA((3 * n,)), pltpu.SemaphoreType.DMA((3 * n,))],
    )(*parts)


def _sum_parts(tag, own, parts, tr=256):
    _, rows, cols = parts.shape
    tr = min(tr, rows)
    chip = (2 * lax.axis_index("x") + lax.axis_index("y")).astype(jnp.int32).reshape(1)

    def body(c_ref, own_ref, p1_ref, p2_ref, p3_ref, o_ref):
        total = own_ref[0].astype(F32)
        for p_ref in (p1_ref, p2_ref, p3_ref):
            total = total + p_ref[0].astype(F32)
        o_ref[...] = total

    def after(kk):
        return pl.BlockSpec((1, tr, cols), lambda i, c_ref: ((c_ref[0] + kk) % N_CHIPS, i, 0))

    return pl.pallas_call(
        body, name="sum_parts_" + tag,
        grid_spec=pltpu.PrefetchScalarGridSpec(
            num_scalar_prefetch=1, grid=(rows // tr,), in_specs=[after(0), after(1), after(2), after(3)],
            out_specs=pl.BlockSpec((tr, cols), lambda i, c_ref: (i, 0))),
        out_shape=jax.ShapeDtypeStruct((rows, cols), F32),
        compiler_params=_params(("arbitrary",)),
    )(chip, own, parts, parts, parts)


def _adamw_math(w, g, m, v):
    m = ADAM_B1 * m + (1.0 - ADAM_B1) * g
    v = ADAM_B2 * v + (1.0 - ADAM_B2) * jnp.square(g)
    m_hat = m / (1.0 - ADAM_B1 ** ADAM_STEP)
    v_hat = v / (1.0 - ADAM_B2 ** ADAM_STEP)
    delta = -ADAM_LR * (m_hat / (jnp.sqrt(v_hat) + ADAM_EPS) + ADAM_WD * w)
    return delta, m, v


def _adamw(tag, w, g, m, v, tr=256):
    rows, cols = w.shape
    tr = min(tr, rows)

    def body(w_ref, g_ref, m_ref, v_ref, d_ref, nm_ref, nv_ref):
        d_ref[...], nm_ref[...], nv_ref[...] = _adamw_math(w_ref[...], g_ref[...], m_ref[...], v_ref[...])

    spec = pl.BlockSpec((tr, cols), lambda i: (i, 0))
    return pl.pallas_call(
        body, name="adamw_" + tag, grid=(rows // tr,), in_specs=[spec] * 4, out_specs=[spec] * 3,
        out_shape=[jax.ShapeDtypeStruct(w.shape, F32)] * 3,
        compiler_params=_params(("parallel",)),
    )(w, g, m, v)


def _w_in_to_padded(w):
    d0 = SSD_DIM + CONV_DIM
    return jnp.concatenate([w[:, :d0], w[:, d0 + SSD_HEADS:], w[:, d0:d0 + SSD_HEADS],
                            jnp.zeros((w.shape[0], DT_PAD - SSD_HEADS), w.dtype)], axis=1)


def _w_in_from_padded(w):
    d0 = SSD_DIM + CONV_DIM
    return jnp.concatenate([w[:, :d0], w[:, DT_OFF:DT_OFF + SSD_HEADS], w[:, d0:DT_OFF]], axis=1)


def _small_layout():
    return (("norm_mix_g", 0, 0, D_MODEL), ("norm_xa_g", 1, 0, D_MODEL), ("norm_mem_g", 2, 0, D_MODEL),
            ("norm_ff_g", 3, 0, D_MODEL), ("conv_b", 4, 0, CONV_DIM), ("ssd_norm_g", 5, 0, SSD_DIM),
            ("sb_norm_g", 5, SSD_DIM, SB_DIM), ("dt_bias", 6, 0, SSD_HEADS), ("a_log", 6, LANES, SSD_HEADS),
            ("d_skip", 6, 2 * LANES, SSD_HEADS))


def _pack_small(gs, loss):
    lay = _small_layout()
    args = [gs[name][l] for l in range(DEPTH) for name, _, _, _ in lay]
    args += [gs["conv_w"][l] for l in range(DEPTH)] + [gs["final_g"], loss]
    n_lay = len(lay)

    def body(*refs):
        o_ref = refs[-1]
        o_ref[...] = jnp.zeros_like(o_ref)
        for l in range(DEPTH):
            for i, (_, rr, c0, width) in enumerate(lay):
                row = l * _SM_PER_LAYER + rr
                o_ref[row:row + 1, c0:c0 + width] = refs[l * n_lay + i][...]
            row = _SM_CONVW + l * CONV_K
            o_ref[row:row + CONV_K, 0:CONV_DIM] = refs[DEPTH * n_lay + l][...]
        o_ref[_SM_FINAL:_SM_FINAL + 1, :] = refs[DEPTH * n_lay + DEPTH][...]
        o_ref[_SM_LOSS:_SM_LOSS + 1, 0:LANES] = refs[DEPTH * n_lay + DEPTH + 1][...]

    return pl.pallas_call(
        body, name="pack_small", in_specs=[VMEM_SPEC] * len(args), out_specs=VMEM_SPEC,
        out_shape=jax.ShapeDtypeStruct((SMALL_ROWS, PACK_COLS), F32),
    )(*args)


def _small_update(buf, w, mom, var):
    lay = _small_layout()
    names = [name for name, _, _, _ in lay] + ["final_g", "conv_w"]
    conv_cols = CONV_DIM // N_CHIPS
    shapes2d = {name: (DEPTH, width) for name, _, _, width in lay}
    shapes2d["final_g"] = (1, D_MODEL)
    shapes2d["conv_w"] = (DEPTH * CONV_K, conv_cols)
    args = [buf]
    for src in (w, mom, var):
        args += [src[name].reshape(shapes2d[name]) for name in names]
    n = len(names)

    def body(*refs):
        b_ref = refs[0]
        w_refs, m_refs, v_refs = refs[1:1 + n], refs[1 + n:1 + 2 * n], refs[1 + 2 * n:1 + 3 * n]
        outs = refs[1 + 3 * n:]
        chip = 2 * lax.axis_index("x") + lax.axis_index("y")
        for i, name in enumerate(names):
            if name == "final_g":
                g = b_ref[_SM_FINAL:_SM_FINAL + 1, :]
            elif name == "conv_w":
                rows = b_ref[_SM_CONVW:_SM_CONVW + DEPTH * CONV_K, 0:CONV_DIM]
                g = jnp.zeros((DEPTH * CONV_K, conv_cols), F32)
                for j in range(N_CHIPS):
                    g = g + jnp.where(chip == j, rows[:, j * conv_cols:(j + 1) * conv_cols], 0.0)
            else:
                _, rr, c0, width = lay[i]
                g = jnp.concatenate([b_ref[l * _SM_PER_LAYER + rr:l * _SM_PER_LAYER + rr + 1, c0:c0 + width]
                                     for l in range(DEPTH)], axis=0)
            d, m2, v2 = _adamw_math(w_refs[i][...], g, m_refs[i][...], v_refs[i][...])
            outs[i][...] = g
            outs[n + i][...] = d
            outs[2 * n + i][...] = m2
            outs[3 * n + i][...] = v2

    out_shape = [jax.ShapeDtypeStruct(shapes2d[name], F32) for _ in range(4) for name in names]
    res = pl.pallas_call(
        body, name="small_update", in_specs=[VMEM_SPEC] * len(args), out_specs=[VMEM_SPEC] * (4 * n),
        out_shape=out_shape,
    )(*args)
    return tuple({name: res[k * n + i].reshape(w[name].shape) for i, name in enumerate(names)} for k in range(4))


CONV_BLOCK_ROWS = 8


def _conv_w_block(cw):
    flat = cw.reshape(-1)
    pad = jnp.zeros((CONV_BLOCK_ROWS * PACK_COLS - flat.shape[0],), F32)
    return jnp.concatenate([flat, pad]).reshape(CONV_BLOCK_ROWS, PACK_COLS)


def _conv_w_from_slots(slots):
    cols = CONV_DIM // N_CHIPS
    n = DEPTH * CONV_K * cols
    shards = [slots[2 * j].reshape(-1)[:n].reshape(DEPTH, CONV_K, cols) for j in range(N_CHIPS)]
    return jnp.concatenate(shards, axis=2)


SMALL_NAMES = ("norm_mix_g", "conv_b", "dt_bias", "a_log", "d_skip", "ssd_norm_g", "sb_norm_g", "norm_xa_g",
               "norm_mem_g", "norm_ff_g", "final_g")
WEIGHT_ORDER = ("norm_mix_g", "w_in", "conv_w", "conv_b", "dt_bias", "a_log", "d_skip", "ssd_norm_g", "sb_norm_g",
                "w_out", "norm_xa_g", "norm_mem_g", "w_xq", "w_xk", "w_xv", "w_xo", "norm_ff_g", "w_ff1", "w_ff2",
                "final_g")


def kernel(x, mem, norm_mix_g, w_in, conv_w, conv_b, dt_bias, a_log, d_skip, ssd_norm_g, sb_norm_g, w_out, norm_xa_g, norm_mem_g, w_xq, w_xk, w_xv, w_xo, norm_ff_g, w_ff1, w_ff2, final_g, loss_target, m_norm_mix_g, m_w_in, m_conv_w, m_conv_b, m_dt_bias, m_a_log, m_d_skip, m_ssd_norm_g, m_sb_norm_g, m_w_out, m_norm_xa_g, m_norm_mem_g, m_w_xq, m_w_xk, m_w_xv, m_w_xo, m_norm_ff_g, m_w_ff1, m_w_ff2, m_final_g, v_norm_mix_g, v_w_in, v_conv_w, v_conv_b, v_dt_bias, v_a_log, v_d_skip, v_ssd_norm_g, v_sb_norm_g, v_w_out, v_norm_xa_g, v_norm_mem_g, v_w_xq, v_w_xk, v_w_xv, v_w_xo, v_norm_ff_g, v_w_ff1, v_w_ff2, v_final_g):
    w = dict(norm_mix_g=norm_mix_g, w_in=w_in, conv_w=conv_w, conv_b=conv_b, dt_bias=dt_bias, a_log=a_log,
             d_skip=d_skip, ssd_norm_g=ssd_norm_g, sb_norm_g=sb_norm_g, w_out=w_out, norm_xa_g=norm_xa_g,
             norm_mem_g=norm_mem_g, w_xq=w_xq, w_xk=w_xk, w_xv=w_xv, w_xo=w_xo, norm_ff_g=norm_ff_g, w_ff1=w_ff1,
             w_ff2=w_ff2, final_g=final_g)
    mom = dict(norm_mix_g=m_norm_mix_g, w_in=m_w_in, conv_w=m_conv_w, conv_b=m_conv_b, dt_bias=m_dt_bias,
               a_log=m_a_log, d_skip=m_d_skip, ssd_norm_g=m_ssd_norm_g, sb_norm_g=m_sb_norm_g, w_out=m_w_out,
               norm_xa_g=m_norm_xa_g, norm_mem_g=m_norm_mem_g, w_xq=m_w_xq, w_xk=m_w_xk, w_xv=m_w_xv, w_xo=m_w_xo,
               norm_ff_g=m_norm_ff_g, w_ff1=m_w_ff1, w_ff2=m_w_ff2, final_g=m_final_g)
    var = dict(norm_mix_g=v_norm_mix_g, w_in=v_w_in, conv_w=v_conv_w, conv_b=v_conv_b, dt_bias=v_dt_bias,
               a_log=v_a_log, d_skip=v_d_skip, ssd_norm_g=v_ssd_norm_g, sb_norm_g=v_sb_norm_g, w_out=v_w_out,
               norm_xa_g=v_norm_xa_g, norm_mem_g=v_norm_mem_g, w_xq=v_w_xq, w_xk=v_w_xk, w_xv=v_w_xv, w_xo=v_w_xo,
               norm_ff_g=v_norm_ff_g, w_ff1=v_w_ff1, w_ff2=v_w_ff2, final_g=v_final_g)
    chip = 2 * lax.axis_index("x") + lax.axis_index("y")

    shards = [w[name].astype(BF16) for name, _, _ in MATS]
    gathered = _gather_weights(shards)
    mats = []
    for l in range(DEPTH):
        full = {}
        for (name, _, axis), mine_bf, theirs in zip(MATS, shards, gathered):
            parts = [jnp.where(chip == j, mine_bf[l], theirs[l, j]) for j in range(N_CHIPS)]
            full[name] = jnp.concatenate(parts, axis=axis)
        full["w_in"] = _w_in_to_padded(full["w_in"])
        mats.append(full)
    conv_slots, _ = _gather_small("_conv", _conv_w_block(conv_w))
    conv_w_full = _conv_w_from_slots(conv_slots)

    sw = {name: w[name] for name in SMALL_NAMES}
    sw["conv_w"] = conv_w_full
    loss, grad_x, gm, gs, g_stacked = _local_step(x[0], mem[0], loss_target[0], sw, mats)

    _, small_sum = _gather_small("_grads", _pack_small(gs, loss))
    loss_out = small_sum[_SM_LOSS, 0]

    def parts_of(g, axis):
        if axis == 0:
            return g.reshape((N_CHIPS, g.shape[0] // N_CHIPS, g.shape[1]))
        return jnp.swapaxes(g.reshape((g.shape[0], N_CHIPS, g.shape[1] // N_CHIPS)), 0, 1)

    g_parts = []
    for name, _, axis in MATS:
        if name in g_stacked:
            g_parts.append(g_stacked[name])
            continue
        per_layer = [gm[l][name] for l in range(DEPTH)]
        if name == "w_in":
            per_layer = [_w_in_from_padded(g) for g in per_layer]
        g_parts.append(jnp.stack([parts_of(g, axis) for g in per_layer]))
    from_sibling = _swap_with_sibling("swap_layers", g_parts, pick=lambda c: 1 - c)
    to_chips = [_add_layers(name, g, r) for (name, _, _), g, r in zip(MATS, g_parts, from_sibling)]
    received = _scatter_chips(to_chips)
    mine = [_sum_parts(name, own, got) for (name, _, _), own, got in zip(MATS, to_chips, received)]
    other = _swap_with_sibling("swap_reduced", mine)
    south = lax.axis_index("c") == 0
    g_mats = {name: jnp.stack([jnp.where(south, a, b), jnp.where(south, b, a)])
              for (name, _, _), a, b in zip(MATS, mine, other)}

    grads, deltas, new_m, new_v = {}, {}, {}, {}
    for name, _, _ in MATS:
        g = g_mats[name]
        cols = g.shape[-1]
        d, nm, nv = _adamw(name, w[name].reshape(-1, cols), g.reshape(-1, cols), mom[name].reshape(-1, cols),
                           var[name].reshape(-1, cols))
        grads[name] = g
        deltas[name], new_m[name], new_v[name] = (t.reshape(g.shape) for t in (d, nm, nv))
    g_s, d_s, m_s, v_s = _small_update(small_sum, w, mom, var)
    for name in g_s:
        grads[name], deltas[name], new_m[name], new_v[name] = g_s[name], d_s[name], m_s[name], v_s[name]

    return (loss_out, grad_x[None], *[grads[n] for n in WEIGHT_ORDER], *[deltas[n] for n in WEIGHT_ORDER],
            *[new_m[n] for n in WEIGHT_ORDER], *[new_v[n] for n in WEIGHT_ORDER])
```

```python
import functools
import math

import jax
import jax.numpy as jnp
from jax import lax
from jax.experimental import pallas as pl
from jax.experimental.pallas import tpu as pltpu

F32 = jnp.float32
BF16 = jnp.bfloat16
MESH = pl.DeviceIdType.MESH

D_MODEL = 1024
DEPTH = 2
SSD_DIM = 512
SSD_HEAD_DIM = 64
SSD_HEADS = 8
SSD_GROUPS = 2
SSD_STATE = 64
CONV_K = 4
CHUNK = 128
SB_DIM = 512
SB_HEAD_DIM = 64
XA_HEADS = 4
XA_HEAD_DIM = 128
XA_DIM = 512
D_FF = 4096
EPS = 1e-5
GN = SSD_GROUPS * SSD_STATE
CONV_DIM = SSD_DIM + 2 * GN
IN_DIM = SSD_DIM + CONV_DIM + SSD_HEADS + 3 * SB_DIM
LANES = 128
DT_PAD = LANES
IN_PAD = SSD_DIM + CONV_DIM + 3 * SB_DIM + DT_PAD
Q_OFF = SSD_DIM + CONV_DIM
DT_OFF = Q_OFF + 3 * SB_DIM
HALO = 8

ADAM_LR = 0.001
ADAM_B1 = 0.9
ADAM_B2 = 0.999
ADAM_EPS = 1e-08
ADAM_WD = 0.01
ADAM_STEP = 10

N_CHIPS = 4
N_DEV = 8
PACK_COLS = 1024
VMEM_LIMIT = 56 * 1024 * 1024

MATS = (
    ("w_in", (D_MODEL, IN_DIM), 1),
    ("w_out", (D_MODEL, D_MODEL), 0),
    ("w_xq", (D_MODEL, XA_DIM), 0),
    ("w_xk", (D_MODEL, XA_DIM), 0),
    ("w_xv", (D_MODEL, XA_DIM), 0),
    ("w_xo", (XA_DIM, D_MODEL), 1),
    ("w_ff1", (D_MODEL, D_FF), 1),
    ("w_ff2", (D_FF, D_MODEL), 0),
)


SMALL_ROWS = 24
_SM_PER_LAYER = 7
_SM_FINAL = 14
_SM_CONVW = 15
_SM_LOSS = 23


_NN = ((1,), (0,))
_NT = ((1,), (1,))
_TN = ((0,), (0,))


def _dg(a, b, dims):
    return lax.dot_general(a.astype(BF16), b.astype(BF16), (dims, ((), ())), preferred_element_type=F32)


@jax.custom_vjp
def mm_nn(a, b):
    return _dg(a, b, _NN)


@jax.custom_vjp
def mm_nt(a, b):
    return _dg(a, b, _NT)


@jax.custom_vjp
def mm_tn(a, b):
    return _dg(a, b, _TN)


def _nn_fwd(a, b):
    return _dg(a, b, _NN), (a, b)


def _nn_bwd(res, g):
    a, b = res
    return mm_nt(g, b).astype(a.dtype), mm_tn(a, g).astype(b.dtype)


def _nt_fwd(a, b):
    return _dg(a, b, _NT), (a, b)


def _nt_bwd(res, g):
    a, b = res
    return mm_nn(g, b).astype(a.dtype), mm_tn(g, a).astype(b.dtype)


def _tn_fwd(a, b):
    return _dg(a, b, _TN), (a, b)


def _tn_bwd(res, g):
    a, b = res
    return mm_nt(b, g).astype(a.dtype), mm_nn(a, g).astype(b.dtype)


mm_nn.defvjp(_nn_fwd, _nn_bwd)
mm_nt.defvjp(_nt_fwd, _nt_bwd)
mm_tn.defvjp(_tn_fwd, _tn_bwd)


def _rms(x, g):
    return x * lax.rsqrt(jnp.mean(x * x, axis=-1, keepdims=True) + EPS) * g


def _params(sem=None, vmem=VMEM_LIMIT):
    return pltpu.CompilerParams(dimension_semantics=sem, vmem_limit_bytes=vmem)


class Exchange:
    def __init__(self, inputs, outputs, n_sems, start, end, mid=None):
        self.inputs, self.outputs, self.n_sems = list(inputs), list(outputs), n_sems
        self.start, self.mid, self.end = start, mid, end

    def specs(self):
        hbm = pl.BlockSpec(memory_space=pl.ANY)
        sems = [pltpu.SemaphoreType.DMA((self.n_sems,)), pltpu.SemaphoreType.DMA((self.n_sems,))]
        return [hbm] * len(self.inputs), [hbm] * len(self.outputs), sems

    def pick(self, refs, n_before_in, n_before_out):
        n_in, n_out = len(self.inputs), len(self.outputs)
        o0 = n_before_in + n_in + n_before_out
        return refs[n_before_in:n_before_in + n_in], refs[o0:o0 + n_out], refs[-2], refs[-1]

    def before_work(self, mine, first, mid=None):
        @pl.when(first)
        def _():
            self.start(*mine)

        if self.mid is not None and mid is not None:
            @pl.when(mid)
            def _():
                self.mid(*mine)

    def after_work(self, mine, last, mid_done):
        @pl.when(last)
        def _():
            if self.mid is not None and not mid_done:
                self.mid(*mine)
            self.end(*mine)


def _rowcall(name, fn, rows, fulls, row_out, acc_out=(), tm=256, exchange=None):
    s = rows[0].shape[0]
    tm = min(tm, s)
    nt = s // tm
    n_r, n_f, n_ro, n_ao = len(rows), len(fulls), len(row_out), len(acc_out)
    n_xi = len(exchange.inputs) if exchange else 0

    def body(*refs):
        if exchange:
            mine = exchange.pick(refs, n_r + n_f, n_ro + n_ao)
            exchange.before_work(mine, pl.program_id(0) == 0)
        ins = [r[...] for r in refs[: n_r + n_f]]
        outs = fn(*ins)
        o_refs = refs[n_r + n_f + n_xi:]
        for o_ref, val in zip(o_refs[:n_ro], outs[:n_ro]):
            o_ref[...] = val.astype(o_ref.dtype)
        if n_ao:
            first = pl.program_id(0) == 0

            @pl.when(first)
            def _():
                for o_ref, val in zip(o_refs[n_ro:], outs[n_ro:]):
                    o_ref[...] = val.astype(o_ref.dtype)

            @pl.when(jnp.logical_not(first))
            def _():
                for o_ref, val in zip(o_refs[n_ro:], outs[n_ro:]):
                    o_ref[...] += val.astype(o_ref.dtype)
        if exchange:
            exchange.after_work(mine, pl.program_id(0) == nt - 1, mid_done=False)

    in_specs = [pl.BlockSpec((tm, a.shape[1]), lambda i: (i, 0)) for a in rows]
    in_specs += [pl.BlockSpec(a.shape, lambda i: (0, 0), pipeline_mode=pl.Buffered(1)) for a in fulls]
    out_specs = [pl.BlockSpec((tm, c), lambda i: (i, 0)) for c, _ in row_out]
    out_specs += [pl.BlockSpec(shape, lambda i: (0, 0)) for shape, _ in acc_out]
    out_shape = [jax.ShapeDtypeStruct((s, c), dt) for c, dt in row_out]
    out_shape += [jax.ShapeDtypeStruct(shape, dt) for shape, dt in acc_out]
    args, scratch = [*rows, *fulls], []
    if exchange:
        x_in, x_out, scratch = exchange.specs()
        in_specs += x_in
        out_specs += x_out
        out_shape += exchange.outputs
        args += exchange.inputs
    return pl.pallas_call(
        body, name=name, grid=(nt,), in_specs=in_specs, out_specs=out_specs, out_shape=out_shape,
        scratch_shapes=scratch, compiler_params=_params(("arbitrary",)),
    )(*args)


def _mm_tn_call(name, a, b, tm, tn, tk):
    s, m = a.shape
    n = b.shape[1]
    tk = min(tk, s)

    def body(a_ref, b_ref, o_ref):
        d = _dg(a_ref[...], b_ref[...], _TN)
        first = pl.program_id(2) == 0

        @pl.when(first)
        def _():
            o_ref[...] = d

        @pl.when(jnp.logical_not(first))
        def _():
            o_ref[...] += d

    return pl.pallas_call(
        body, name=name, grid=(m // tm, n // tn, s // tk),
        in_specs=[pl.BlockSpec((tk, tm), lambda i, j, k: (k, i)), pl.BlockSpec((tk, tn), lambda i, j, k: (k, j))],
        out_specs=pl.BlockSpec((tm, tn), lambda i, j, k: (i, j)),
        out_shape=jax.ShapeDtypeStruct((m, n), F32),
        compiler_params=_params(("parallel", "parallel", "arbitrary")),
    )(a, b)


def _mm_tn_parts(name, a, b, by_cols, tm=512, tk=2048):
    s, m = a.shape
    n = b.shape[1]
    r, c = (m, n // N_CHIPS) if by_cols else (m // N_CHIPS, n)
    per = r // tm
    tk = min(tk, s)

    def body(a_ref, b_ref, o_ref):
        d = _dg(a_ref[...], b_ref[...], _TN)
        first = pl.program_id(2) == 0

        @pl.when(first)
        def _():
            o_ref[0] = d

        @pl.when(jnp.logical_not(first))
        def _():
            o_ref[0] += d

    if by_cols:
        out_map = lambda i, j, k: (j, i, 0)
    else:
        out_map = lambda i, j, k: (i // per, i % per, 0)
    return pl.pallas_call(
        body, name=name, grid=(m // tm, n // c, s // tk),
        in_specs=[pl.BlockSpec((tk, tm), lambda i, j, k: (k, i)), pl.BlockSpec((tk, c), lambda i, j, k: (k, j))],
        out_specs=pl.BlockSpec((1, tm, c), out_map),
        out_shape=jax.ShapeDtypeStruct((N_CHIPS, r, c), F32),
        compiler_params=_params(("parallel", "parallel", "arbitrary")),
    )(a, b)


def _proj_tile(h, g, w):
    p = mm_nn(_rms(h, g), w)
    return (p[:, :SSD_DIM], p[:, SSD_DIM:Q_OFF], p[:, Q_OFF:Q_OFF + SB_DIM],
            p[:, Q_OFF + SB_DIM:Q_OFF + 2 * SB_DIM], p[:, Q_OFF + 2 * SB_DIM:DT_OFF], p[:, DT_OFF:])


def _proj_fwd(tag, h, g, w):
    return _rowcall(
        "proj_fwd" + tag, _proj_tile, [h], [g, w],
        [(SSD_DIM, F32), (CONV_DIM, F32), (SB_DIM, BF16), (SB_DIM, BF16), (SB_DIM, BF16), (DT_PAD, F32)], tm=512)


def _proj_bwd(tag, h, dh_out, dz, dxbc, dq, dk, dv, ddt, g, w):
    def fn(h, dh_out, dz, dxbc, dq, dk, dv, ddt, g, w):
        dp = jnp.concatenate([dz.astype(BF16), dxbc.astype(BF16), dq.astype(BF16), dk.astype(BF16),
                              dv.astype(BF16), ddt.astype(BF16)], axis=1)
        hn, vjp = jax.vjp(_rms, h, g)
        dh, dg = vjp(mm_nt(dp, w))
        return dh_out + dh, hn, dp, dg

    return _rowcall(
        "proj_bwd" + tag, fn, [h, dh_out, dz, dxbc, dq, dk, dv, ddt], [g, w],
        [(D_MODEL, F32), (D_MODEL, BF16), (IN_PAD, BF16)], [((1, D_MODEL), F32)], tm=256)


def _shift_down(x, tail, j):
    if j == 0:
        return x
    r = pltpu.roll(x, j, 0)
    rt = pltpu.roll(tail, j, 0)
    row = lax.broadcasted_iota(jnp.int32, (HALO, x.shape[1]), 0)
    first = jnp.where(row < j, rt, r[:HALO])
    if x.shape[0] == HALO:
        return first
    return jnp.concatenate([first, r[HALO:]], axis=0)


def _shift_up(x, head, j):
    if j == 0:
        return x
    n = x.shape[0]
    r = pltpu.roll(x, n - j, 0)
    rh = pltpu.roll(head, HALO - j, 0)
    row = lax.broadcasted_iota(jnp.int32, (HALO, x.shape[1]), 0)
    return jnp.concatenate([r[:n - HALO], jnp.where(row >= HALO - j, rh, r[n - HALO:])], axis=0)


def _conv_pre(x, tail, w, b):
    acc = b + w[CONV_K - 1:CONV_K] * x
    for j in range(1, CONV_K):
        acc = acc + w[CONV_K - 1 - j:CONV_K - j] * _shift_down(x, tail, j)
    return acc


def _dsilu(p):
    s = jax.nn.sigmoid(p)
    return s * (1.0 + p * (1.0 - s))


def _conv_fwd(tag, xbc, w, b, tc=512):
    s, c = xbc.shape
    tc = min(tc, s)
    per = tc // HALO

    def body(x_ref, prev_ref, w_ref, b_ref, o_ref):
        tail = jnp.where(pl.program_id(0) > 0, prev_ref[...], 0.0)
        o_ref[...] = jax.nn.silu(_conv_pre(x_ref[...], tail, w_ref[...], b_ref[...]))

    return pl.pallas_call(
        body, name="conv_fwd" + tag, grid=(s // tc,),
        in_specs=[pl.BlockSpec((tc, c), lambda i: (i, 0)),
                  pl.BlockSpec((HALO, c), lambda i: (jnp.maximum(i * per - 1, 0), 0)),
                  pl.BlockSpec((CONV_K, c), lambda i: (0, 0)), pl.BlockSpec((1, c), lambda i: (0, 0))],
        out_specs=pl.BlockSpec((tc, c), lambda i: (i, 0)),
        out_shape=jax.ShapeDtypeStruct((s, c), F32),
        compiler_params=_params(("arbitrary",)),
    )(xbc, xbc, w, b)


def _conv_bwd(tag, xbc, dact, w, b, tc=512):
    s, c = xbc.shape
    tc = min(tc, s)
    per = tc // HALO
    nt = s // tc
    last_blk = s // HALO - 1

    def body(x_ref, prev_ref, next_ref, d_ref, dnext_ref, w_ref, b_ref, dx_ref, dw_ref, db_ref):
        i = pl.program_id(0)
        x = x_ref[...]
        wv = w_ref[...]
        tail = jnp.where(i > 0, prev_ref[...], 0.0)
        dpre = d_ref[...] * _dsilu(_conv_pre(x, tail, wv, b_ref[...]))
        pre_n = _conv_pre(next_ref[...], x[tc - HALO:], wv, b_ref[...])
        dpre_n = jnp.where(i < nt - 1, dnext_ref[...] * _dsilu(pre_n), 0.0)
        dx = wv[CONV_K - 1:CONV_K] * dpre
        for j in range(1, CONV_K):
            dx = dx + wv[CONV_K - 1 - j:CONV_K - j] * _shift_up(dpre, dpre_n, j)
        dx_ref[...] = dx
        dws = [jnp.sum(dpre * _shift_down(x, tail, CONV_K - 1 - k), axis=0, keepdims=True) for k in range(CONV_K)]
        dwv = jnp.concatenate(dws, axis=0)
        dbv = jnp.sum(dpre, axis=0, keepdims=True)

        @pl.when(i == 0)
        def _():
            dw_ref[...] = dwv
            db_ref[...] = dbv

        @pl.when(i > 0)
        def _():
            dw_ref[...] += dwv
            db_ref[...] += dbv

    tile = pl.BlockSpec((tc, c), lambda i: (i, 0))
    prev = pl.BlockSpec((HALO, c), lambda i: (jnp.maximum(i * per - 1, 0), 0))
    nxt = pl.BlockSpec((HALO, c), lambda i: (jnp.minimum((i + 1) * per, last_blk), 0))
    return pl.pallas_call(
        body, name="conv_bwd" + tag, grid=(nt,),
        in_specs=[tile, prev, nxt, tile, nxt, pl.BlockSpec((CONV_K, c), lambda i: (0, 0)),
                  pl.BlockSpec((1, c), lambda i: (0, 0))],
        out_specs=[tile, pl.BlockSpec((CONV_K, c), lambda i: (0, 0)), pl.BlockSpec((1, c), lambda i: (0, 0))],
        out_shape=[jax.ShapeDtypeStruct((s, c), F32), jax.ShapeDtypeStruct((CONV_K, c), F32),
                   jax.ShapeDtypeStruct((1, c), F32)],
        compiler_params=_params(("arbitrary",)),
    )(xbc, xbc, xbc, dact, dact, w, b)


def _ssd_chunk(xs, bm, cm, dtr, z, dt_bias, a_log, d_skip, g, s_prev):
    n = CHUNK
    row = lax.broadcasted_iota(jnp.int32, (n, n), 0)
    col = lax.broadcasted_iota(jnp.int32, (n, n), 1)
    causal = row >= col
    dt = jax.nn.softplus(dtr + dt_bias)
    a_c = dt * (-jnp.exp(a_log))
    hi = lax.Precision.HIGHEST
    a_cum = jnp.dot(causal.astype(F32), a_c, precision=hi, preferred_element_type=F32)
    a_cum_t = lax.dot_general(a_c, (row <= col).astype(F32), (_TN, ((), ())), precision=hi,
                              preferred_element_type=F32)
    p, st = SSD_HEAD_DIM, SSD_STATE
    heads = range(SSD_HEADS)
    grp = [h // (SSD_HEADS // SSD_GROUPS) for h in heads]
    bgs = [bm[:, k * st:(k + 1) * st] for k in range(SSD_GROUPS)]
    cgs = [cm[:, k * st:(k + 1) * st] for k in range(SSD_GROUPS)]
    cb = [mm_nt(cgs[k], bgs[k]) for k in range(SSD_GROUPS)]
    acols = [a_cum[:, h:h + 1] for h in heads]
    a_lasts = [a_cum[n - 1:n, h:h + 1] for h in heads]
    xhs = [xs[:, h * p:(h + 1) * p] for h in heads]
    sps = [s_prev[h * p:(h + 1) * p, :] for h in heads]
    xdts = [xhs[h] * dt[:, h:h + 1] for h in heads]
    decays = [jnp.exp(jnp.where(causal, acols[h] - a_cum_t[h:h + 1, :], -jnp.inf)) for h in heads]
    y_offs = [mm_nt(cgs[grp[h]], sps[h]) for h in heads]
    y_diags = [mm_nn(cb[grp[h]] * decays[h], xdts[h]) for h in heads]
    states = [mm_tn(xdts[h] * jnp.exp(a_lasts[h] - acols[h]), bgs[grp[h]]) for h in heads]
    s_new = [sps[h] * jnp.exp(a_lasts[h]) + states[h] for h in heads]
    ys = [y_diags[h] + y_offs[h] * jnp.exp(acols[h]) + d_skip[:, h:h + 1] * xhs[h] for h in heads]
    y = jnp.concatenate(ys, axis=1) * jax.nn.silu(z)
    return _rms(y, g), jnp.concatenate(s_new, axis=0)


def _split_xbc(t):
    return t[:, :SSD_DIM], t[:, SSD_DIM:SSD_DIM + GN], t[:, SSD_DIM + GN:]


def _ssd_fwd(tag, xact, dtr, z, dt_bias, a_log, d_skip, g):
    s = xact.shape[0]
    nc = s // CHUNK
    srows = SSD_HEADS * SSD_HEAD_DIM

    def body(x_ref, dt_ref, z_ref, b_ref, al_ref, ds_ref, g_ref, y_ref, st_ref, state):
        @pl.when(pl.program_id(0) == 0)
        def _():
            state[...] = jnp.zeros_like(state)

        sp = state[...]
        st_ref[0] = sp
        xs, bm, cm = _split_xbc(x_ref[...])
        y, sn = _ssd_chunk(xs, bm, cm, dt_ref[...][:, :SSD_HEADS], z_ref[...], b_ref[...], al_ref[...],
                           ds_ref[...], g_ref[...], sp)
        y_ref[...] = y
        state[...] = sn

    small = pl.BlockSpec((1, SSD_HEADS), lambda i: (0, 0))
    return pl.pallas_call(
        body, name="ssd_fwd" + tag, grid=(nc,),
        in_specs=[pl.BlockSpec((CHUNK, CONV_DIM), lambda i: (i, 0)), pl.BlockSpec((CHUNK, DT_PAD), lambda i: (i, 0)),
                  pl.BlockSpec((CHUNK, SSD_DIM), lambda i: (i, 0)), small, small, small,
                  pl.BlockSpec((1, SSD_DIM), lambda i: (0, 0))],
        out_specs=[pl.BlockSpec((CHUNK, SSD_DIM), lambda i: (i, 0)),
                   pl.BlockSpec((1, srows, SSD_STATE), lambda i: (i, 0, 0))],
        out_shape=[jax.ShapeDtypeStruct((s, SSD_DIM), F32), jax.ShapeDtypeStruct((nc, srows, SSD_STATE), F32)],
        scratch_shapes=[pltpu.VMEM((srows, SSD_STATE), F32)],
        compiler_params=_params(("arbitrary",)),
    )(xact, dtr, z, dt_bias, a_log, d_skip, g)


def _ssd_bwd(tag, xact, dtr, z, states, dy, dt_bias, a_log, d_skip, g):
    s = xact.shape[0]
    nc = s // CHUNK
    srows = SSD_HEADS * SSD_HEAD_DIM

    def body(x_ref, dt_ref, z_ref, sp_ref, dy_ref, b_ref, al_ref, ds_ref, g_ref,
             dx_ref, ddt_ref, dz_ref, db_ref, dal_ref, dds_ref, dg_ref, dstate):
        first = pl.program_id(0) == 0

        @pl.when(first)
        def _():
            dstate[...] = jnp.zeros_like(dstate)

        xs, bm, cm = _split_xbc(x_ref[...])
        _, vjp = jax.vjp(_ssd_chunk, xs, bm, cm, dt_ref[...][:, :SSD_HEADS], z_ref[...], b_ref[...], al_ref[...],
                         ds_ref[...], g_ref[...], sp_ref[0])
        dxs, dbm, dcm, ddt, dz, db, dal, dds, dg, dsp = vjp((dy_ref[...], dstate[...]))
        dx_ref[...] = jnp.concatenate([dxs, dbm, dcm], axis=1)
        ddt_ref[...] = jnp.concatenate([ddt, jnp.zeros((CHUNK, DT_PAD - SSD_HEADS), F32)], axis=1)
        dz_ref[...] = dz
        dstate[...] = dsp

        @pl.when(first)
        def _():
            db_ref[...] = db
            dal_ref[...] = dal
            dds_ref[...] = dds
            dg_ref[...] = dg

        @pl.when(jnp.logical_not(first))
        def _():
            db_ref[...] += db
            dal_ref[...] += dal
            dds_ref[...] += dds
            dg_ref[...] += dg

    def rev(c):
        return lambda i: (nc - 1 - i, 0)

    small = pl.BlockSpec((1, SSD_HEADS), lambda i: (0, 0))
    gspec = pl.BlockSpec((1, SSD_DIM), lambda i: (0, 0))
    return pl.pallas_call(
        body, name="ssd_bwd" + tag, grid=(nc,),
        in_specs=[pl.BlockSpec((CHUNK, CONV_DIM), rev(0)), pl.BlockSpec((CHUNK, DT_PAD), rev(0)),
                  pl.BlockSpec((CHUNK, SSD_DIM), rev(0)),
                  pl.BlockSpec((1, srows, SSD_STATE), lambda i: (nc - 1 - i, 0, 0)),
                  pl.BlockSpec((CHUNK, SSD_DIM), rev(0)), small, small, small, gspec],
        out_specs=[pl.BlockSpec((CHUNK, CONV_DIM), rev(0)), pl.BlockSpec((CHUNK, DT_PAD), rev(0)),
                   pl.BlockSpec((CHUNK, SSD_DIM), rev(0)), small, small, small, gspec],
        out_shape=[jax.ShapeDtypeStruct((s, CONV_DIM), F32), jax.ShapeDtypeStruct((s, DT_PAD), F32),
                   jax.ShapeDtypeStruct((s, SSD_DIM), F32), jax.ShapeDtypeStruct((1, SSD_HEADS), F32),
                   jax.ShapeDtypeStruct((1, SSD_HEADS), F32), jax.ShapeDtypeStruct((1, SSD_HEADS), F32),
                   jax.ShapeDtypeStruct((1, SSD_DIM), F32)],
        scratch_shapes=[pltpu.VMEM((srows, SSD_STATE), F32)],
        compiler_params=_params(("arbitrary",)),
    )(xact, dtr, z, states, dy, dt_bias, a_log, d_skip, g)


TQ = 128
TK = 128
SB_SCALE = 1.0 / math.sqrt(SB_HEAD_DIM)


def _split2(x):
    hi = x.astype(BF16)
    return hi, (x - hi.astype(F32)).astype(BF16)


TK_WIDE = 256
SB_UNDERFLOW = -110.0


def _sb_logits(qhs, kb, t0, s0, masked):
    zs = [_dg(qh, kb, _NT) for qh in qhs]
    mask = None
    if masked:
        t_pos = t0 + lax.broadcasted_iota(jnp.int32, zs[0].shape, 0)
        s_pos = s0 + lax.broadcasted_iota(jnp.int32, zs[0].shape, 1)
        mask = s_pos < t_pos
    lbs = [jnp.minimum(z, 0.0) - jnp.log(1.0 + jnp.exp(-jnp.abs(z))) for z in zs]
    lss = [lb - z for lb, z in zip(lbs, zs)]
    if masked:
        lss = [jnp.where(mask, ls, 0.0) for ls in lss]
    return lbs, lss, mask


def _running_sums(xs, starts, u, reverse, two_terms=True):
    nsub = xs[0].shape[1] // TK
    order = list(reversed(range(nsub))) if reverse else list(range(nsub))
    chunks = [[x[:, c * TK:(c + 1) * TK] for c in range(nsub)] for x in xs]
    sums = [[(_lane_sums(xc, u) if two_terms else _dg(xc, u, _NN)) for xc in row] for row in chunks]
    out = []
    for row, srow, run in zip(chunks, sums, starts):
        parts = [None] * nsub
        for c in order:
            parts[c] = run + srow[c]
            run = run + jnp.sum(row[c], axis=1, keepdims=True)
        out.append((parts[0] if nsub == 1 else jnp.concatenate(parts, axis=1), run))
    return out


def _lane_sums(x, u):
    hi, lo = _split2(x)
    return _dg(hi, u, _NN) + _dg(lo, u, _NN)


def _tri(cmp):
    j = lax.broadcasted_iota(jnp.int32, (TK, TK), 0)
    s = lax.broadcasted_iota(jnp.int32, (TK, TK), 1)
    return cmp(j, s).astype(BF16)


def _sb_fwd(tag, q, k, v, exchange=None):
    s = q.shape[0]
    npair = SB_DIM // LANES
    nq = s // TQ
    wide = min(TK_WIDE, s)
    per = wide // TQ
    n_xi = len(exchange.inputs) if exchange else 0

    def body(*refs):
        q_ref, k_ref, v_ref = refs[:3]
        o_ref, t_ref = refs[3 + n_xi:5 + n_xi]
        tq = pl.program_id(1)
        if exchange:
            pair = pl.program_id(0)
            mine = exchange.pick(refs, 3, 2)
            exchange.before_work(mine, jnp.logical_and(pair == 0, tq == 0),
                                 jnp.logical_and(pair == npair - 1, tq == 0))
        diag = tq // per
        qp = q_ref[...]
        lane = lax.broadcasted_iota(jnp.int32, (1, LANES), 1)
        u_gt = _tri(lambda j, s: j > s)
        heads = range(LANES // SB_HEAD_DIM)
        hms = [(lane // SB_HEAD_DIM) == hh for hh in heads]
        qhs = [jnp.where(hm, qp, jnp.zeros_like(qp)) * SB_SCALE for hm in hms]

        def block(wb, carry, masked):
            off = pl.multiple_of(wb * wide, wide)
            kb = k_ref[pl.ds(off, wide), :]
            vb = v_ref[pl.ds(off, wide), :]
            lbs, lss, mask = _sb_logits(qhs, kb, tq * TQ, wb * wide, masked)
            sums = _running_sums(lss, [c[0] for c in carry], u_gt, reverse=True)
            ws = [jnp.exp(lb + later) for lb, (later, _) in zip(lbs, sums)]
            if masked:
                ws = [jnp.where(mask, w, 0.0) for w in ws]
            pvs = [_dg(w, vb, _NN) for w in ws]
            return tuple((r, c[1] + pv) for (_, r), c, pv in zip(sums, carry, pvs))

        def more(c):
            alive = jnp.max(c[1][0][0])
            for hh in heads[1:]:
                alive = jnp.maximum(alive, jnp.max(c[1][hh][0]))
            return jnp.logical_and(c[0] >= 0, alive > SB_UNDERFLOW)

        start = tuple((jnp.zeros((TQ, 1), F32), jnp.zeros((TQ, LANES), F32)) for _ in heads)
        wb, done = lax.while_loop(more, lambda c: (c[0] - 1, block(c[0], c[1], False)),
                                  (diag - 1, block(diag, start, True)))
        first = (wb + 1).astype(F32)
        out = jnp.zeros((TQ, LANES), F32)
        tot = jnp.zeros((TQ, LANES), F32)
        for hh in heads:
            r, acc = done[hh]
            out = out + jnp.where(hms[hh], acc, 0.0)
            tot = tot + jnp.where(hms[hh], jnp.where(lane % SB_HEAD_DIM == 1, first, r), 0.0)
        o_ref[...] = out
        t_ref[...] = tot
        if exchange:
            exchange.after_work(mine, jnp.logical_and(pair == npair - 1, tq == nq - 1), mid_done=True)

    tile = pl.BlockSpec((TQ, LANES), lambda p, t: (t, p))
    full = pl.BlockSpec((s, LANES), lambda p, t: (0, p))
    in_specs, out_specs = [tile, full, full], [tile, tile]
    out_shape = [jax.ShapeDtypeStruct((s, SB_DIM), F32)] * 2
    args, scratch = [q, k, v], []
    if exchange:
        x_in, x_out, scratch = exchange.specs()
        in_specs, out_specs = in_specs + x_in, out_specs + x_out
        out_shape, args = out_shape + exchange.outputs, args + exchange.inputs
    return pl.pallas_call(
        body, name="sb_fwd" + tag, grid=(npair, nq), in_specs=in_specs, out_specs=out_specs, out_shape=out_shape,
        scratch_shapes=scratch, compiler_params=_params(("arbitrary", "arbitrary")),
    )(*args)


def _sb_bwd(tag, q, k, v, tot, do, exchange=None):
    s = q.shape[0]
    npair = SB_DIM // LANES
    nq = s // TQ
    wide = min(TK_WIDE, s)
    per = wide // TQ
    n_xi = len(exchange.inputs) if exchange else 0

    def body(*refs):
        q_ref, k_ref, v_ref, t_ref, do_ref = refs[:5]
        dq_ref, dk_ref, dv_ref = refs[5 + n_xi:8 + n_xi]
        tq = pl.program_id(1)
        if exchange:
            pair = pl.program_id(0)
            mine = exchange.pick(refs, 5, 3)
            exchange.before_work(mine, jnp.logical_and(pair == 0, tq == 0),
                                 jnp.logical_and(pair == npair - 1, tq == 0))
        diag = tq // per

        @pl.when(tq == 0)
        def _():
            dk_ref[...] = jnp.zeros_like(dk_ref)
            dv_ref[...] = jnp.zeros_like(dv_ref)

        qp = q_ref[...]
        dop = do_ref[...]
        totp = t_ref[...]
        lane = lax.broadcasted_iota(jnp.int32, (1, LANES), 1)
        u_le = _tri(lambda j, s: j <= s)
        u_lt = _tri(lambda j, s: j < s)
        heads = range(LANES // SB_HEAD_DIM)
        hms = [(lane // SB_HEAD_DIM) == hh for hh in heads]
        qhs = [jnp.where(hm, qp, jnp.zeros_like(qp)) * SB_SCALE for hm in hms]
        dohs = [jnp.where(hm, dop, 0.0).astype(BF16) for hm in hms]
        totals = [jnp.sum(jnp.where(lane == hh * SB_HEAD_DIM, totp, 0.0), axis=1, keepdims=True) for hh in heads]
        first = jnp.max(jnp.where(lane == 1, totp, 0.0)).astype(jnp.int32)

        def block(wb, carry, masked):
            off = pl.multiple_of(wb * wide, wide)
            kb = k_ref[pl.ds(off, wide), :]
            vb = v_ref[pl.ds(off, wide), :]
            lbs, lss, mask = _sb_logits(qhs, kb, tq * TQ, wb * wide, masked)
            dws = [_dg(doh, vb, _NT) for doh in dohs]
            pres = _running_sums(lss, [c[0] for c in carry], u_le, reverse=False)
            ws = [jnp.exp(lb + (total - before)) for lb, total, (before, _) in zip(lbs, totals, pres)]
            if masked:
                ws = [jnp.where(mask, w, 0.0) for w in ws]
            gs = [w * dw for w, dw in zip(ws, dws)]
            lefts = _running_sums(gs, [c[1] for c in carry], u_lt, reverse=False, two_terms=False)
            dzs = [g - jnp.exp(lb) * (g + g_left) for g, lb, (g_left, _) in zip(gs, lbs, lefts)]
            if masked:
                dzs = [jnp.where(mask, dz, 0.0) for dz in dzs]
            dzbs = [dz.astype(BF16) for dz in dzs]
            dks = [_dg(dzb, qh, _TN) for dzb, qh in zip(dzbs, qhs)]
            dvs = [_dg(w, doh, _TN) for w, doh in zip(ws, dohs)]
            dqs = [_dg(dzb, kb, _NN) for dzb in dzbs]
            dk_ref[pl.ds(off, wide), :] += functools.reduce(jnp.add, dks)
            dv_ref[pl.ds(off, wide), :] += functools.reduce(jnp.add, dvs)
            return tuple((pre, gc, c[2] + dq) for (_, pre), (_, gc), c, dq in zip(pres, lefts, carry, dqs))

        zero = jnp.zeros((TQ, 1), F32)
        start = tuple((zero, zero, jnp.zeros((TQ, LANES), F32)) for _ in heads)
        done = block(diag, lax.fori_loop(first, diag, lambda j, c: block(j, c, False), start), True)
        dq = jnp.zeros((TQ, LANES), F32)
        for hh in heads:
            dq = dq + jnp.where(hms[hh], done[hh][2], 0.0)
        dq_ref[...] = dq * SB_SCALE
        if exchange:
            exchange.after_work(mine, jnp.logical_and(pair == npair - 1, tq == nq - 1), mid_done=True)

    tile = pl.BlockSpec((TQ, LANES), lambda p, t: (t, p))
    full = pl.BlockSpec((s, LANES), lambda p, t: (0, p))
    in_specs, out_specs = [tile, full, full, tile, tile], [tile, full, full]
    out_shape = [jax.ShapeDtypeStruct((s, SB_DIM), F32)] * 3
    args, scratch = [q, k, v, tot, do], []
    if exchange:
        x_in, x_out, scratch = exchange.specs()
        in_specs, out_specs = in_specs + x_in, out_specs + x_out
        out_shape, args = out_shape + exchange.outputs, args + exchange.inputs
    return pl.pallas_call(
        body, name="sb_bwd" + tag, grid=(npair, nq), in_specs=in_specs, out_specs=out_specs, out_shape=out_shape,
        scratch_shapes=scratch, compiler_params=_params(("arbitrary", "arbitrary")),
    )(*args)


def _out_tile(y_ssd, o, sb_g, w_out):
    y_all = jnp.concatenate([y_ssd, _rms(o, sb_g)], axis=1)
    return mm_nn(y_all, w_out)


def _out_fwd(tag, h, y_ssd, o, sb_g, w_out):
    return _rowcall("out_fwd" + tag, lambda h, y, o, g, w: (h + _out_tile(y, o, g, w),),
                    [h, y_ssd, o], [sb_g, w_out], [(D_MODEL, F32)], tm=512)[0]


def _out_bwd(tag, y_ssd, o, dh, sb_g, w_out):
    def fn(y, o, dh, g, w):
        _, vjp = jax.vjp(_out_tile, y, o, g, w.astype(F32))
        return vjp(dh)

    return _rowcall("out_bwd" + tag, fn, [y_ssd, o, dh], [sb_g, w_out], [(SSD_DIM, F32), (SB_DIM, F32)],
                    [((1, SB_DIM), F32), ((D_MODEL, D_MODEL), F32)], tm=256)


def _mem_tile(mem, g, w_k, w_v):
    m = _rms(mem, g)
    return mm_nn(m, w_k), mm_nn(m, w_v)


def _mem_fwd(tag, mem, g, w_k, w_v):
    return _rowcall("mem_fwd" + tag, _mem_tile, [mem], [g, w_k, w_v], [(XA_DIM, F32), (XA_DIM, F32)], tm=256)


def _mem_bwd(tag, mem, dkx, dvx, g, w_k, w_v):
    def fn(mem, dkx, dvx, g, w_k, w_v):
        _, vjp = jax.vjp(lambda g, a, b: _mem_tile(mem, g, a, b), g, w_k.astype(F32), w_v.astype(F32))
        return vjp((dkx, dvx))

    return _rowcall("mem_bwd" + tag, fn, [mem, dkx, dvx], [g, w_k, w_v], [],
                    [((1, D_MODEL), F32), ((D_MODEL, XA_DIM), F32), ((D_MODEL, XA_DIM), F32)], tm=256)


def _xattn_tile(h, g, w_q, kx, vx, w_o):
    q = mm_nn(_rms(h, g), w_q)
    scale = 1.0 / math.sqrt(XA_HEAD_DIM)
    outs = []
    for i in range(XA_HEADS):
        sl = slice(i * XA_HEAD_DIM, (i + 1) * XA_HEAD_DIM)
        p = jax.nn.softmax(mm_nt(q[:, sl], kx[:, sl]) * scale, axis=-1)
        outs.append(mm_nn(p, vx[:, sl]))
    return mm_nn(jnp.concatenate(outs, axis=1), w_o)


def _xattn_fwd(tag, h, g, w_q, kx, vx, w_o):
    return _rowcall("xattn_fwd" + tag, lambda h, g, wq, kx, vx, wo: (h + _xattn_tile(h, g, wq, kx, vx, wo),),
                    [h], [g, w_q, kx, vx, w_o], [(D_MODEL, F32)], tm=512)[0]


def _xattn_bwd(tag, h, dh_out, g, w_q, kx, vx, w_o):
    def fn(h, dh_out, g, w_q, kx, vx, w_o):
        _, vjp = jax.vjp(_xattn_tile, h, g, w_q.astype(F32), kx, vx, w_o.astype(F32))
        dh, dg, dwq, dkx, dvx, dwo = vjp(dh_out)
        return dh_out + dh, dg, dwq, dkx, dvx, dwo

    mlen = kx.shape[0]
    return _rowcall("xattn_bwd" + tag, fn, [h, dh_out], [g, w_q, kx, vx, w_o], [(D_MODEL, F32)],
                    [((1, D_MODEL), F32), ((D_MODEL, XA_DIM), F32), ((mlen, XA_DIM), F32), ((mlen, XA_DIM), F32),
                     ((XA_DIM, D_MODEL), F32)], tm=256)


def _mlp_fwd(tag, h, g, w1, w2):
    def fn(h, g, w1, w2):
        u = jnp.square(jnp.maximum(mm_nn(_rms(h, g), w1), 0.0))
        return (h + mm_nn(u, w2),)

    return _rowcall("mlp_fwd" + tag, fn, [h], [g, w1, w2], [(D_MODEL, F32)], tm=256)[0]


def _mlp_bwd(tag, h, dh_out, g, w1, w2, exchange=None):
    def fn(h, dh_out, g, w1, w2):
        hn, vjp = jax.vjp(_rms, h, g)
        r = jnp.maximum(mm_nn(hn, w1), 0.0)
        dob = dh_out.astype(BF16)
        dp = mm_nt(dob, w2) * (2.0 * r)
        dh, dg = vjp(mm_nt(dp, w1))
        return dh_out + dh, hn, dp, r * r, dob, dg

    return _rowcall("mlp_bwd" + tag, fn, [h, dh_out], [g, w1, w2],
                    [(D_MODEL, F32), (D_MODEL, BF16), (D_FF, BF16), (D_FF, BF16), (D_MODEL, BF16)],
                    [((1, D_MODEL), F32)], tm=256, exchange=exchange)


def _head(h, g, target):
    def lossfn(h, g, t):
        err = jnp.square(_rms(h, g) - t)
        return 0.5 * jnp.sum(jnp.mean(err, axis=-1))

    def fn(h, t, g):
        loss, vjp = jax.vjp(lambda h, g: lossfn(h, g, t), h, g)
        dh, dg = vjp(jnp.ones((), F32))
        return dh, jnp.full((1, LANES), loss, F32), dg

    return _rowcall("head", fn, [h, target], [g], [(D_MODEL, F32)], [((1, LANES), F32), ((1, D_MODEL), F32)], tm=512)


def _row(v):
    return v.reshape(1, -1)


class LocalPlan:
    def __init__(self, mats):
        self.mats = mats

    def weights(self, l):
        return self.mats[l]

    def in_sb_fwd(self, l):
        return None

    def after_sb_fwd(self, l, outs):
        pass

    def grads_done(self, l, gm):
        pass

    def in_mlp_bwd(self, l):
        return None

    def after_mlp_bwd(self, l, outs):
        pass

    def in_sb_bwd(self, l):
        return None

    def after_sb_bwd(self, l, outs):
        pass


def _local_step(x, mem, target, sw, plan):
    h = x
    saved = []
    for l in range(DEPTH):
        tag = str(l)
        m = plan.weights(l)
        z, xbc, q, k, v, dtr = _proj_fwd(tag, h, _row(sw["norm_mix_g"][l]), m["w_in"])
        xact = _conv_fwd(tag, xbc, sw["conv_w"][l], _row(sw["conv_b"][l]))
        y_ssd, states = _ssd_fwd(tag, xact, dtr, z, _row(sw["dt_bias"][l]), _row(sw["a_log"][l]),
                                 _row(sw["d_skip"][l]), _row(sw["ssd_norm_g"][l]))
        o, sb_tot, *carried = _sb_fwd(tag, q, k, v, exchange=plan.in_sb_fwd(l))
        plan.after_sb_fwd(l, carried)
        h1 = _out_fwd(tag, h, y_ssd, o, _row(sw["sb_norm_g"][l]), m["w_out"])
        kx, vx = _mem_fwd(tag, mem, _row(sw["norm_mem_g"][l]), m["w_xk"], m["w_xv"])
        h2 = _xattn_fwd(tag, h1, _row(sw["norm_xa_g"][l]), m["w_xq"], kx, vx, m["w_xo"])
        h3 = _mlp_fwd(tag, h2, _row(sw["norm_ff_g"][l]), m["w_ff1"], m["w_ff2"])
        saved.append((h, z, xbc, q, k, v, dtr, xact, y_ssd, states, o, sb_tot, h1, kx, vx, h2))
        h = h3

    dh, loss, d_final = _head(h, _row(sw["final_g"]), target)
    gm = [dict() for _ in range(DEPTH)]
    gs = {name: [None] * DEPTH for name in ("norm_mix_g", "conv_w", "conv_b", "dt_bias", "a_log", "d_skip",
                                            "ssd_norm_g", "sb_norm_g", "norm_xa_g", "norm_mem_g", "norm_ff_g")}
    for l in reversed(range(DEPTH)):
        tag = str(l)
        m = plan.weights(l)
        h0, z, xbc, q, k, v, dtr, xact, y_ssd, states, o, sb_tot, h1, kx, vx, h2 = saved[l]
        dh2, hn_b, dp_b, a_b, do_b, gs["norm_ff_g"][l], *carried = _mlp_bwd(
            tag, h2, dh, _row(sw["norm_ff_g"][l]), m["w_ff1"], m["w_ff2"], exchange=plan.in_mlp_bwd(l))
        plan.after_mlp_bwd(l, carried)
        gm[l]["w_ff1"] = _mm_tn_parts("dw_ff1" + tag, hn_b, dp_b, True)
        gm[l]["w_ff2"] = _mm_tn_parts("dw_ff2" + tag, a_b, do_b, False)
        dh1, gs["norm_xa_g"][l], gm[l]["w_xq"], dkx, dvx, gm[l]["w_xo"] = _xattn_bwd(
            tag, h1, dh2, _row(sw["norm_xa_g"][l]), m["w_xq"], kx, vx, m["w_xo"])
        gs["norm_mem_g"][l], gm[l]["w_xk"], gm[l]["w_xv"] = _mem_bwd(
            tag, mem, dkx, dvx, _row(sw["norm_mem_g"][l]), m["w_xk"], m["w_xv"])
        dy_ssd, do, gs["sb_norm_g"][l], gm[l]["w_out"] = _out_bwd(
            tag, y_ssd, o, dh1, _row(sw["sb_norm_g"][l]), m["w_out"])
        dq, dk, dv, *carried = _sb_bwd(tag, q, k, v, sb_tot, do, exchange=plan.in_sb_bwd(l))
        plan.after_sb_bwd(l, carried)
        dxact, ddtr, dz, gs["dt_bias"][l], gs["a_log"][l], gs["d_skip"][l], gs["ssd_norm_g"][l] = _ssd_bwd(
            tag, xact, dtr, z, states, dy_ssd, _row(sw["dt_bias"][l]), _row(sw["a_log"][l]),
            _row(sw["d_skip"][l]), _row(sw["ssd_norm_g"][l]))
        dxbc, gs["conv_w"][l], gs["conv_b"][l] = _conv_bwd(tag, xbc, dxact, sw["conv_w"][l], _row(sw["conv_b"][l]))
        dh, hn_b, dp_b, gs["norm_mix_g"][l] = _proj_bwd(
            tag, h0, dh1, dz, dxbc, dq, dk, dv, ddtr, _row(sw["norm_mix_g"][l]), m["w_in"])
        gm[l]["w_in"] = _mm_tn_call("dw_in" + tag, hn_b, dp_b, 512, IN_PAD, 1024)
        plan.grads_done(l, gm[l])
    gs["final_g"] = d_final
    return loss, dh, gm, gs


ANY = pl.BlockSpec(memory_space=pl.ANY)
VMEM_SPEC = pl.BlockSpec(memory_space=pltpu.VMEM)


def _place():
    return lax.axis_index("x"), lax.axis_index("y"), lax.axis_index("c")


def _other_chips(x, y):
    return [(1 - x, y), (x, 1 - y), (1 - x, 1 - y)]


def _remote(send_sems, recv_sems, idx, src, dst, to):
    return pltpu.make_async_remote_copy(src_ref=src, dst_ref=dst, send_sem=send_sems.at[idx],
                                        recv_sem=recv_sems.at[idx], device_id=to, device_id_type=MESH)


def _run_exchange(name, ex):
    def body(*refs):
        mine = ex.pick(refs, 0, 0)
        ex.start(*mine)
        if ex.mid is not None:
            ex.mid(*mine)
        ex.end(*mine)

    x_in, x_out, scratch = ex.specs()
    return pl.pallas_call(body, name=name, in_specs=x_in, out_specs=x_out, out_shape=ex.outputs,
                          scratch_shapes=scratch)(*ex.inputs)


def _gather_exchange(layer, shards):
    n = len(shards)
    outs = [jax.ShapeDtypeStruct((N_CHIPS,) + s.shape[1:], s.dtype) for s in shards]

    def start(w_refs, o_refs, ss, rs):
        x, y, c = _place()

        @pl.when(c == layer)
        def _():
            for i in range(n):
                for kk, (cx, cy) in enumerate(_other_chips(x, y)):
                    _remote(ss, rs, 6 * i + kk, w_refs[i].at[layer], o_refs[i].at[2 * x + y], (cx, cy, layer)).start()

    def mid(w_refs, o_refs, ss, rs):
        x, y, c = _place()

        @pl.when(c == layer)
        def _():
            for i in range(n):
                for kk, (cx, cy) in enumerate(_other_chips(x, y)):
                    got = o_refs[i].at[2 * cx + cy]
                    _remote(ss, rs, 6 * i + kk, got, got, (x, y, c)).wait_recv()
                    _remote(ss, rs, 6 * i + 3 + kk, got, got, (x, y, 1 - layer)).start()

    def end(w_refs, o_refs, ss, rs):
        x, y, c = _place()
        for i in range(n):
            for kk, (cx, cy) in enumerate(_other_chips(x, y)):
                got = o_refs[i].at[2 * cx + cy]

                @pl.when(c == layer)
                def _():
                    _remote(ss, rs, 6 * i + kk, w_refs[i].at[layer], got, (x, y, c)).wait_send()
                    _remote(ss, rs, 6 * i + 3 + kk, got, got, (x, y, c)).wait_send()

                @pl.when(c != layer)
                def _():
                    _remote(ss, rs, 6 * i + 3 + kk, got, got, (x, y, c)).wait_recv()

    return Exchange(shards, outs, 6 * n, start, end, mid)


def _handover_exchange(layer, grads):
    n = len(grads)
    outs = [jax.ShapeDtypeStruct(g.shape, g.dtype) for g in grads]

    def start(g_refs, o_refs, ss, rs):
        x, y, c = _place()

        @pl.when(c != layer)
        def _():
            for i in range(n):
                _remote(ss, rs, i, g_refs[i], o_refs[i], (x, y, layer)).start()

    def end(g_refs, o_refs, ss, rs):
        x, y, c = _place()
        for i in range(n):
            @pl.when(c != layer)
            def _():
                _remote(ss, rs, i, g_refs[i], o_refs[i], (x, y, c)).wait_send()

            @pl.when(c == layer)
            def _():
                _remote(ss, rs, i, g_refs[i], o_refs[i], (x, y, c)).wait_recv()

    return Exchange(grads, outs, n, start, end)


def _scatter_exchange(layer, parts):
    n = len(parts)
    outs = [jax.ShapeDtypeStruct(p.shape, p.dtype) for p in parts]

    def start(s_refs, o_refs, ss, rs):
        x, y, c = _place()

        @pl.when(c == layer)
        def _():
            for i in range(n):
                for kk, (cx, cy) in enumerate(_other_chips(x, y)):
                    _remote(ss, rs, 3 * i + kk, s_refs[i].at[2 * cx + cy], o_refs[i].at[2 * x + y],
                            (cx, cy, layer)).start()

    def end(s_refs, o_refs, ss, rs):
        x, y, c = _place()

        @pl.when(c == layer)
        def _():
            for i in range(n):
                for kk, (cx, cy) in enumerate(_other_chips(x, y)):
                    got = o_refs[i].at[2 * cx + cy]
                    _remote(ss, rs, 3 * i + kk, got, got, (x, y, c)).wait_recv()
            for i in range(n):
                for kk, (cx, cy) in enumerate(_other_chips(x, y)):
                    _remote(ss, rs, 3 * i + kk, s_refs[i].at[2 * cx + cy], o_refs[i].at[2 * x + y],
                            (x, y, c)).wait_send()

    return Exchange(parts, outs, 3 * n, start, end)


def _return_exchange(reduced):
    flat = [g for layer in range(DEPTH) for g in reduced[layer]]
    n = len(reduced[0])
    outs = [jax.ShapeDtypeStruct(g.shape, g.dtype) for g in flat]

    def start(g_refs, o_refs, ss, rs):
        x, y, c = _place()
        for layer in range(DEPTH):
            @pl.when(c == layer)
            def _():
                for i in range(n):
                    k = layer * n + i
                    _remote(ss, rs, k, g_refs[k], o_refs[k], (x, y, 1 - layer)).start()

    def end(g_refs, o_refs, ss, rs):
        x, y, c = _place()
        for layer in range(DEPTH):
            for i in range(n):
                k = layer * n + i

                @pl.when(c == layer)
                def _():
                    _remote(ss, rs, k, g_refs[k], o_refs[k], (x, y, c)).wait_send()

                @pl.when(c != layer)
                def _():
                    _remote(ss, rs, k, g_refs[k], o_refs[k], (x, y, c)).wait_recv()

    return Exchange(flat, outs, DEPTH * n, start, end)


def _gather_small(tag, buf):
    shape = buf.shape

    def body(b_ref, o_ref, sum_ref, send_sems, recv_sems, local_sem):
        x, y, c = _place()
        me = 4 * x + 2 * y + c
        mine = pltpu.make_async_copy(b_ref, o_ref.at[me], local_sem)
        mine.start()
        flips = [(dx, dy, dc) for dx in (0, 1) for dy in (0, 1) for dc in (0, 1) if (dx, dy, dc) != (0, 0, 0)]
        sends = []

        def peer(dx, dy, dc):
            return (1 - x if dx else x, 1 - y if dy else y, 1 - c if dc else c)

        for kk, flip in enumerate(flips):
            cp = pltpu.make_async_remote_copy(src_ref=b_ref, dst_ref=o_ref.at[me], send_sem=send_sems.at[kk],
                                              recv_sem=recv_sems.at[kk], device_id=peer(*flip), device_id_type=MESH)
            cp.start()
            sends.append(cp)
        for kk, flip in enumerate(flips):
            px, py, pc = peer(*flip)
            frm = 4 * px + 2 * py + pc
            pltpu.make_async_remote_copy(src_ref=b_ref, dst_ref=o_ref.at[frm], send_sem=send_sems.at[kk],
                                         recv_sem=recv_sems.at[kk], device_id=(x, y, c),
                                         device_id_type=MESH).wait_recv()
        for cp in sends:
            cp.wait_send()
        mine.wait()
        total = o_ref[0]
        for d in range(1, N_DEV):
            total = total + o_ref[d]
        sum_ref[...] = total

    return pl.pallas_call(
        body, name="gather_small" + tag, in_specs=[VMEM_SPEC], out_specs=[VMEM_SPEC, VMEM_SPEC],
        out_shape=[jax.ShapeDtypeStruct((N_DEV,) + shape, buf.dtype), jax.ShapeDtypeStruct(shape, buf.dtype)],
        scratch_shapes=[pltpu.SemaphoreType.DMA((N_DEV - 1,)), pltpu.SemaphoreType.DMA((N_DEV - 1,)),
                        pltpu.SemaphoreType.DMA],
    )(buf)


def _add_handed(tag, layer, g, r, tr=256):
    _, rows, cols = g.shape
    tr = min(tr, rows)

    def body(g_ref, r_ref, o_ref):
        @pl.when(lax.axis_index("c") == layer)
        def _():
            o_ref[...] = (g_ref[...] + r_ref[...]).astype(o_ref.dtype)

    spec = pl.BlockSpec((1, tr, cols), lambda p, i: (p, i, 0))
    return pl.pallas_call(
        body, name="add_handed_" + tag, grid=(N_CHIPS, rows // tr), in_specs=[spec, spec], out_specs=spec,
        out_shape=jax.ShapeDtypeStruct(g.shape, BF16), compiler_params=_params(("arbitrary", "arbitrary")),
    )(g, r)


def _sum_parts(tag, layer, own, parts, tr=256):
    _, rows, cols = parts.shape
    tr = min(tr, rows)
    chip = (2 * lax.axis_index("x") + lax.axis_index("y")).astype(jnp.int32).reshape(1)

    def body(c_ref, own_ref, p1_ref, p2_ref, p3_ref, o_ref):
        @pl.when(lax.axis_index("c") == layer)
        def _():
            total = own_ref[0].astype(F32)
            for p_ref in (p1_ref, p2_ref, p3_ref):
                total = total + p_ref[0].astype(F32)
            o_ref[...] = total

    def after(kk):
        return pl.BlockSpec((1, tr, cols), lambda i, c_ref: ((c_ref[0] + kk) % N_CHIPS, i, 0))

    return pl.pallas_call(
        body, name="sum_parts_" + tag,
        grid_spec=pltpu.PrefetchScalarGridSpec(
            num_scalar_prefetch=1, grid=(rows // tr,), in_specs=[after(0), after(1), after(2), after(3)],
            out_specs=pl.BlockSpec((tr, cols), lambda i, c_ref: (i, 0))),
        out_shape=jax.ShapeDtypeStruct((rows, cols), F32),
        compiler_params=_params(("arbitrary",)),
    )(chip, own, parts, parts, parts)


def _adamw_math(w, g, m, v):
    m = ADAM_B1 * m + (1.0 - ADAM_B1) * g
    v = ADAM_B2 * v + (1.0 - ADAM_B2) * jnp.square(g)
    m_hat = m / (1.0 - ADAM_B1 ** ADAM_STEP)
    v_hat = v / (1.0 - ADAM_B2 ** ADAM_STEP)
    delta = -ADAM_LR * (m_hat / (jnp.sqrt(v_hat) + ADAM_EPS) + ADAM_WD * w)
    return delta, m, v


def _adamw(tag, w, g, m, v, tr=256):
    rows, cols = w.shape
    tr = min(tr, rows)

    def body(w_ref, g_ref, m_ref, v_ref, d_ref, nm_ref, nv_ref):
        d_ref[...], nm_ref[...], nv_ref[...] = _adamw_math(w_ref[...], g_ref[...], m_ref[...], v_ref[...])

    spec = pl.BlockSpec((tr, cols), lambda i: (i, 0))
    return pl.pallas_call(
        body, name="adamw_" + tag, grid=(rows // tr,), in_specs=[spec] * 4, out_specs=[spec] * 3,
        out_shape=[jax.ShapeDtypeStruct(w.shape, F32)] * 3,
        compiler_params=_params(("parallel",)),
    )(w, g, m, v)


def _w_in_to_padded(w):
    d0 = SSD_DIM + CONV_DIM
    return jnp.concatenate([w[:, :d0], w[:, d0 + SSD_HEADS:], w[:, d0:d0 + SSD_HEADS],
                            jnp.zeros((w.shape[0], DT_PAD - SSD_HEADS), w.dtype)], axis=1)


def _w_in_from_padded(w):
    d0 = SSD_DIM + CONV_DIM
    return jnp.concatenate([w[:, :d0], w[:, DT_OFF:DT_OFF + SSD_HEADS], w[:, d0:DT_OFF]], axis=1)


def _small_layout():
    return (("norm_mix_g", 0, 0, D_MODEL), ("norm_xa_g", 1, 0, D_MODEL), ("norm_mem_g", 2, 0, D_MODEL),
            ("norm_ff_g", 3, 0, D_MODEL), ("conv_b", 4, 0, CONV_DIM), ("ssd_norm_g", 5, 0, SSD_DIM),
            ("sb_norm_g", 5, SSD_DIM, SB_DIM), ("dt_bias", 6, 0, SSD_HEADS), ("a_log", 6, LANES, SSD_HEADS),
            ("d_skip", 6, 2 * LANES, SSD_HEADS))


def _pack_small(gs, loss):
    lay = _small_layout()
    args = [gs[name][l] for l in range(DEPTH) for name, _, _, _ in lay]
    args += [gs["conv_w"][l] for l in range(DEPTH)] + [gs["final_g"], loss]
    n_lay = len(lay)

    def body(*refs):
        o_ref = refs[-1]
        o_ref[...] = jnp.zeros_like(o_ref)
        for l in range(DEPTH):
            for i, (_, rr, c0, width) in enumerate(lay):
                row = l * _SM_PER_LAYER + rr
                o_ref[row:row + 1, c0:c0 + width] = refs[l * n_lay + i][...]
            row = _SM_CONVW + l * CONV_K
            o_ref[row:row + CONV_K, 0:CONV_DIM] = refs[DEPTH * n_lay + l][...]
        o_ref[_SM_FINAL:_SM_FINAL + 1, :] = refs[DEPTH * n_lay + DEPTH][...]
        o_ref[_SM_LOSS:_SM_LOSS + 1, 0:LANES] = refs[DEPTH * n_lay + DEPTH + 1][...]

    return pl.pallas_call(
        body, name="pack_small", in_specs=[VMEM_SPEC] * len(args), out_specs=VMEM_SPEC,
        out_shape=jax.ShapeDtypeStruct((SMALL_ROWS, PACK_COLS), F32),
    )(*args)


def _small_update(buf, w, mom, var):
    lay = _small_layout()
    names = [name for name, _, _, _ in lay] + ["final_g", "conv_w"]
    conv_cols = CONV_DIM // N_CHIPS
    shapes2d = {name: (DEPTH, width) for name, _, _, width in lay}
    shapes2d["final_g"] = (1, D_MODEL)
    shapes2d["conv_w"] = (DEPTH * CONV_K, conv_cols)
    args = [buf]
    for src in (w, mom, var):
        args += [src[name].reshape(shapes2d[name]) for name in names]
    n = len(names)

    def body(*refs):
        b_ref = refs[0]
        w_refs, m_refs, v_refs = refs[1:1 + n], refs[1 + n:1 + 2 * n], refs[1 + 2 * n:1 + 3 * n]
        outs = refs[1 + 3 * n:]
        chip = 2 * lax.axis_index("x") + lax.axis_index("y")
        for i, name in enumerate(names):
            if name == "final_g":
                g = b_ref[_SM_FINAL:_SM_FINAL + 1, :]
            elif name == "conv_w":
                rows = b_ref[_SM_CONVW:_SM_CONVW + DEPTH * CONV_K, 0:CONV_DIM]
                g = jnp.zeros((DEPTH * CONV_K, conv_cols), F32)
                for j in range(N_CHIPS):
                    g = g + jnp.where(chip == j, rows[:, j * conv_cols:(j + 1) * conv_cols], 0.0)
            else:
                _, rr, c0, width = lay[i]
                g = jnp.concatenate([b_ref[l * _SM_PER_LAYER + rr:l * _SM_PER_LAYER + rr + 1, c0:c0 + width]
                                     for l in range(DEPTH)], axis=0)
            d, m2, v2 = _adamw_math(w_refs[i][...], g, m_refs[i][...], v_refs[i][...])
            outs[i][...] = g
            outs[n + i][...] = d
            outs[2 * n + i][...] = m2
            outs[3 * n + i][...] = v2

    out_shape = [jax.ShapeDtypeStruct(shapes2d[name], F32) for _ in range(4) for name in names]
    res = pl.pallas_call(
        body, name="small_update", in_specs=[VMEM_SPEC] * len(args), out_specs=[VMEM_SPEC] * (4 * n),
        out_shape=out_shape,
    )(*args)
    return tuple({name: res[k * n + i].reshape(w[name].shape) for i, name in enumerate(names)} for k in range(4))


CONV_BLOCK_ROWS = 8


def _conv_w_block(cw):
    flat = cw.reshape(-1)
    pad = jnp.zeros((CONV_BLOCK_ROWS * PACK_COLS - flat.shape[0],), F32)
    return jnp.concatenate([flat, pad]).reshape(CONV_BLOCK_ROWS, PACK_COLS)


def _conv_w_from_slots(slots):
    cols = CONV_DIM // N_CHIPS
    n = DEPTH * CONV_K * cols
    shards = [slots[2 * j].reshape(-1)[:n].reshape(DEPTH, CONV_K, cols) for j in range(N_CHIPS)]
    return jnp.concatenate(shards, axis=2)


SMALL_NAMES = ("norm_mix_g", "conv_b", "dt_bias", "a_log", "d_skip", "ssd_norm_g", "sb_norm_g", "norm_xa_g",
               "norm_mem_g", "norm_ff_g", "final_g")
WEIGHT_ORDER = ("norm_mix_g", "w_in", "conv_w", "conv_b", "dt_bias", "a_log", "d_skip", "ssd_norm_g", "sb_norm_g",
                "w_out", "norm_xa_g", "norm_mem_g", "w_xq", "w_xk", "w_xv", "w_xo", "norm_ff_g", "w_ff1", "w_ff2",
                "final_g")


class PipelinedPlan(LocalPlan):
    def __init__(self, shards):
        self.shards = shards
        self.chip = 2 * lax.axis_index("x") + lax.axis_index("y")
        self.mats = [None] * DEPTH
        self.parts = [None] * DEPTH
        self.handed = [None] * DEPTH
        self.to_chips = [None] * DEPTH
        self.reduced = [None] * DEPTH
        self._build(0, _run_exchange("gather_layer0", _gather_exchange(0, shards)))

    def _build(self, l, gathered):
        full = {}
        for (name, _, axis), mine, theirs in zip(MATS, self.shards, gathered):
            full[name] = jnp.concatenate([jnp.where(self.chip == j, mine[l], theirs[j]) for j in range(N_CHIPS)],
                                         axis=axis)
        full["w_in"] = _w_in_to_padded(full["w_in"])
        self.mats[l] = full

    def in_sb_fwd(self, l):
        return _gather_exchange(l + 1, self.shards) if l + 1 < DEPTH else None

    def after_sb_fwd(self, l, outs):
        if outs:
            self._build(l + 1, outs)

    def grads_done(self, l, gm):
        def parts_of(g, axis):
            if g.ndim == 3:
                return g
            if axis == 0:
                return g.reshape((N_CHIPS, g.shape[0] // N_CHIPS, g.shape[1]))
            return jnp.swapaxes(g.reshape((g.shape[0], N_CHIPS, g.shape[1] // N_CHIPS)), 0, 1)

        self.parts[l] = [parts_of(_w_in_from_padded(gm[name]) if name == "w_in" else gm[name], axis)
                         for name, _, axis in MATS]
        if l == 0:
            self._add(0, _run_exchange("handover_layer0", _handover_exchange(0, self.parts[0])))
            self._sum(0, _run_exchange("scatter_layer0", _scatter_exchange(0, self.to_chips[0])))

    def _add(self, l, handed):
        self.to_chips[l] = [_add_handed(name + str(l), l, g, r)
                            for (name, _, _), g, r in zip(MATS, self.parts[l], handed)]

    def _sum(self, l, received):
        self.reduced[l] = [_sum_parts(name + str(l), l, own, got)
                           for (name, _, _), own, got in zip(MATS, self.to_chips[l], received)]

    def in_mlp_bwd(self, l):
        return _handover_exchange(l + 1, self.parts[l + 1]) if l + 1 < DEPTH else None

    def after_mlp_bwd(self, l, outs):
        if outs:
            self._add(l + 1, outs)

    def in_sb_bwd(self, l):
        return _scatter_exchange(l + 1, self.to_chips[l + 1]) if l + 1 < DEPTH else None

    def after_sb_bwd(self, l, outs):
        if outs:
            self._sum(l + 1, outs)

    def reduced_gradients(self):
        returned = _run_exchange("return_reduced", _return_exchange(self.reduced))
        n = len(MATS)
        core = lax.axis_index("c")
        return {name: jnp.stack([jnp.where(core == l, self.reduced[l][i], returned[l * n + i]) for l in range(DEPTH)])
                for i, (name, _, _) in enumerate(MATS)}


def kernel(x, mem, norm_mix_g, w_in, conv_w, conv_b, dt_bias, a_log, d_skip, ssd_norm_g, sb_norm_g, w_out, norm_xa_g, norm_mem_g, w_xq, w_xk, w_xv, w_xo, norm_ff_g, w_ff1, w_ff2, final_g, loss_target, m_norm_mix_g, m_w_in, m_conv_w, m_conv_b, m_dt_bias, m_a_log, m_d_skip, m_ssd_norm_g, m_sb_norm_g, m_w_out, m_norm_xa_g, m_norm_mem_g, m_w_xq, m_w_xk, m_w_xv, m_w_xo, m_norm_ff_g, m_w_ff1, m_w_ff2, m_final_g, v_norm_mix_g, v_w_in, v_conv_w, v_conv_b, v_dt_bias, v_a_log, v_d_skip, v_ssd_norm_g, v_sb_norm_g, v_w_out, v_norm_xa_g, v_norm_mem_g, v_w_xq, v_w_xk, v_w_xv, v_w_xo, v_norm_ff_g, v_w_ff1, v_w_ff2, v_final_g):
    w = dict(norm_mix_g=norm_mix_g, w_in=w_in, conv_w=conv_w, conv_b=conv_b, dt_bias=dt_bias, a_log=a_log,
             d_skip=d_skip, ssd_norm_g=ssd_norm_g, sb_norm_g=sb_norm_g, w_out=w_out, norm_xa_g=norm_xa_g,
             norm_mem_g=norm_mem_g, w_xq=w_xq, w_xk=w_xk, w_xv=w_xv, w_xo=w_xo, norm_ff_g=norm_ff_g, w_ff1=w_ff1,
             w_ff2=w_ff2, final_g=final_g)
    mom = dict(norm_mix_g=m_norm_mix_g, w_in=m_w_in, conv_w=m_conv_w, conv_b=m_conv_b, dt_bias=m_dt_bias,
               a_log=m_a_log, d_skip=m_d_skip, ssd_norm_g=m_ssd_norm_g, sb_norm_g=m_sb_norm_g, w_out=m_w_out,
               norm_xa_g=m_norm_xa_g, norm_mem_g=m_norm_mem_g, w_xq=m_w_xq, w_xk=m_w_xk, w_xv=m_w_xv, w_xo=m_w_xo,
               norm_ff_g=m_norm_ff_g, w_ff1=m_w_ff1, w_ff2=m_w_ff2, final_g=m_final_g)
    var = dict(norm_mix_g=v_norm_mix_g, w_in=v_w_in, conv_w=v_conv_w, conv_b=v_conv_b, dt_bias=v_dt_bias,
               a_log=v_a_log, d_skip=v_d_skip, ssd_norm_g=v_ssd_norm_g, sb_norm_g=v_sb_norm_g, w_out=v_w_out,
               norm_xa_g=v_norm_xa_g, norm_mem_g=v_norm_mem_g, w_xq=v_w_xq, w_xk=v_w_xk, w_xv=v_w_xv, w_xo=v_w_xo,
               norm_ff_g=v_norm_ff_g, w_ff1=v_w_ff1, w_ff2=v_w_ff2, final_g=v_final_g)
    conv_slots, _ = _gather_small("_conv", _conv_w_block(conv_w))
    sw = {name: w[name] for name in SMALL_NAMES}
    sw["conv_w"] = _conv_w_from_slots(conv_slots)
    plan = PipelinedPlan([w[name].astype(BF16) for name, _, _ in MATS])
    loss, grad_x, gm, gs = _local_step(x[0], mem[0], loss_target[0], sw, plan)
    g_mats = plan.reduced_gradients()

    _, small_sum = _gather_small("_grads", _pack_small(gs, loss))
    loss_out = small_sum[_SM_LOSS, 0]

    grads, deltas, new_m, new_v = {}, {}, {}, {}
    for name, _, _ in MATS:
        g = g_mats[name]
        cols = g.shape[-1]
        d, nm, nv = _adamw(name, w[name].reshape(-1, cols), g.reshape(-1, cols), mom[name].reshape(-1, cols),
                           var[name].reshape(-1, cols))
        grads[name] = g
        deltas[name], new_m[name], new_v[name] = (t.reshape(g.shape) for t in (d, nm, nv))
    g_s, d_s, m_s, v_s = _small_update(small_sum, w, mom, var)
    for name in g_s:
        grads[name], deltas[name], new_m[name], new_v[name] = g_s[name], d_s[name], m_s[name], v_s[name]

    return (loss_out, grad_x[None], *[grads[n] for n in WEIGHT_ORDER], *[deltas[n] for n in WEIGHT_ORDER],
            *[new_m[n] for n in WEIGHT_ORDER], *[new_v[n] for n in WEIGHT_ORDER])
```

```python
import functools
import math

import jax
import jax.numpy as jnp
from jax import lax
from jax.experimental import pallas as pl
from jax.experimental.pallas import tpu as pltpu

F32 = jnp.float32
BF16 = jnp.bfloat16
MESH = pl.DeviceIdType.MESH

D_MODEL = 1024
DEPTH = 2
SSD_DIM = 512
SSD_HEAD_DIM = 64
SSD_HEADS = 8
SSD_GROUPS = 2
SSD_STATE = 64
CONV_K = 4
CHUNK = 128
SB_DIM = 512
SB_HEAD_DIM = 64
XA_HEADS = 4
XA_HEAD_DIM = 128
XA_DIM = 512
D_FF = 4096
EPS = 1e-5
GN = SSD_GROUPS * SSD_STATE
CONV_DIM = SSD_DIM + 2 * GN
IN_DIM = SSD_DIM + CONV_DIM + SSD_HEADS + 3 * SB_DIM
LANES = 128
DT_PAD = LANES
IN_PAD = SSD_DIM + CONV_DIM + 3 * SB_DIM + DT_PAD
Q_OFF = SSD_DIM + CONV_DIM
DT_OFF = Q_OFF + 3 * SB_DIM
HALO = 8

ADAM_LR = 0.001
ADAM_B1 = 0.9
ADAM_B2 = 0.999
ADAM_EPS = 1e-08
ADAM_WD = 0.01
ADAM_STEP = 10

N_CHIPS = 4
N_DEV = 8
PACK_COLS = 1024
VMEM_LIMIT = 56 * 1024 * 1024

MATS = (
    ("w_in", (D_MODEL, IN_DIM), 1),
    ("w_out", (D_MODEL, D_MODEL), 0),
    ("w_xq", (D_MODEL, XA_DIM), 0),
    ("w_xk", (D_MODEL, XA_DIM), 0),
    ("w_xv", (D_MODEL, XA_DIM), 0),
    ("w_xo", (XA_DIM, D_MODEL), 1),
    ("w_ff1", (D_MODEL, D_FF), 1),
    ("w_ff2", (D_FF, D_MODEL), 0),
)


SMALL_ROWS = 24
_SM_PER_LAYER = 7
_SM_FINAL = 14
_SM_CONVW = 15
_SM_LOSS = 23


_NN = ((1,), (0,))
_NT = ((1,), (1,))
_TN = ((0,), (0,))


def _dg(a, b, dims):
    return lax.dot_general(a.astype(BF16), b.astype(BF16), (dims, ((), ())), preferred_element_type=F32)


@jax.custom_vjp
def mm_nn(a, b):
    return _dg(a, b, _NN)


@jax.custom_vjp
def mm_nt(a, b):
    return _dg(a, b, _NT)


@jax.custom_vjp
def mm_tn(a, b):
    return _dg(a, b, _TN)


def _nn_fwd(a, b):
    return _dg(a, b, _NN), (a, b)


def _nn_bwd(res, g):
    a, b = res
    return mm_nt(g, b).astype(a.dtype), mm_tn(a, g).astype(b.dtype)


def _nt_fwd(a, b):
    return _dg(a, b, _NT), (a, b)


def _nt_bwd(res, g):
    a, b = res
    return mm_nn(g, b).astype(a.dtype), mm_tn(g, a).astype(b.dtype)


def _tn_fwd(a, b):
    return _dg(a, b, _TN), (a, b)


def _tn_bwd(res, g):
    a, b = res
    return mm_nt(b, g).astype(a.dtype), mm_nn(a, g).astype(b.dtype)


mm_nn.defvjp(_nn_fwd, _nn_bwd)
mm_nt.defvjp(_nt_fwd, _nt_bwd)
mm_tn.defvjp(_tn_fwd, _tn_bwd)


def _rms(x, g):
    return x * lax.rsqrt(jnp.mean(x * x, axis=-1, keepdims=True) + EPS) * g


def _params(sem=None, vmem=VMEM_LIMIT):
    return pltpu.CompilerParams(dimension_semantics=sem, vmem_limit_bytes=vmem)


class Exchange:
    def __init__(self, inputs, outputs, n_sems, start, end, mid=None):
        self.inputs, self.outputs, self.n_sems = list(inputs), list(outputs), n_sems
        self.start, self.mid, self.end = start, mid, end

    def specs(self):
        hbm = pl.BlockSpec(memory_space=pl.ANY)
        sems = [pltpu.SemaphoreType.DMA((self.n_sems,)), pltpu.SemaphoreType.DMA((self.n_sems,))]
        return [hbm] * len(self.inputs), [hbm] * len(self.outputs), sems

    def pick(self, refs, n_before_in, n_before_out):
        n_in, n_out = len(self.inputs), len(self.outputs)
        o0 = n_before_in + n_in + n_before_out
        return refs[n_before_in:n_before_in + n_in], refs[o0:o0 + n_out], refs[-2], refs[-1]

    def before_work(self, mine, first, mid=None):
        @pl.when(first)
        def _():
            self.start(*mine)

        if self.mid is not None and mid is not None:
            @pl.when(mid)
            def _():
                self.mid(*mine)

    def after_work(self, mine, last, mid_done):
        @pl.when(last)
        def _():
            if self.mid is not None and not mid_done:
                self.mid(*mine)
            self.end(*mine)


class _Shifted:
    def __init__(self, ref, base):
        self.ref, self.base = ref, base

    @property
    def at(self):
        return self

    def __getitem__(self, idx):
        return self.ref.at[self.base + idx]


def _both(a, b):
    if a is None or b is None:
        return a or b
    n_i, n_o, n_s = len(a.inputs), len(a.outputs), a.n_sems

    def joined(fa, fb):
        def f(i_refs, o_refs, ss, rs):
            if fa is not None:
                fa(i_refs[:n_i], o_refs[:n_o], ss, rs)
            if fb is not None:
                fb(i_refs[n_i:], o_refs[n_o:], _Shifted(ss, n_s), _Shifted(rs, n_s))
        return f

    mid = joined(a.mid, b.mid) if (a.mid is not None or b.mid is not None) else None
    return Exchange(a.inputs + b.inputs, a.outputs + b.outputs, n_s + b.n_sems, joined(a.start, b.start),
                    joined(a.end, b.end), mid)


def _rowcall(name, fn, rows, fulls, row_out, acc_out=(), tm=256, exchange=None):
    s = rows[0].shape[0]
    tm = min(tm, s)
    nt = s // tm
    n_r, n_f, n_ro, n_ao = len(rows), len(fulls), len(row_out), len(acc_out)
    n_xi = len(exchange.inputs) if exchange else 0

    def body(*refs):
        if exchange:
            mine = exchange.pick(refs, n_r + n_f, n_ro + n_ao)
            exchange.before_work(mine, pl.program_id(0) == 0)
        ins = [r[...] for r in refs[: n_r + n_f]]
        outs = fn(*ins)
        o_refs = refs[n_r + n_f + n_xi:]
        for o_ref, val in zip(o_refs[:n_ro], outs[:n_ro]):
            o_ref[...] = val.astype(o_ref.dtype)
        if n_ao:
            first = pl.program_id(0) == 0

            @pl.when(first)
            def _():
                for o_ref, val in zip(o_refs[n_ro:], outs[n_ro:]):
                    o_ref[...] = val.astype(o_ref.dtype)

            @pl.when(jnp.logical_not(first))
            def _():
                for o_ref, val in zip(o_refs[n_ro:], outs[n_ro:]):
                    o_ref[...] += val.astype(o_ref.dtype)
        if exchange:
            exchange.after_work(mine, pl.program_id(0) == nt - 1, mid_done=False)

    in_specs = [pl.BlockSpec((tm, a.shape[1]), lambda i: (i, 0)) for a in rows]
    in_specs += [pl.BlockSpec(a.shape, lambda i: (0, 0), pipeline_mode=pl.Buffered(1)) for a in fulls]
    out_specs = [pl.BlockSpec((tm, c), lambda i: (i, 0)) for c, _ in row_out]
    out_specs += [pl.BlockSpec(shape, lambda i: (0, 0)) for shape, _ in acc_out]
    out_shape = [jax.ShapeDtypeStruct((s, c), dt) for c, dt in row_out]
    out_shape += [jax.ShapeDtypeStruct(shape, dt) for shape, dt in acc_out]
    args, scratch = [*rows, *fulls], []
    if exchange:
        x_in, x_out, scratch = exchange.specs()
        in_specs += x_in
        out_specs += x_out
        out_shape += exchange.outputs
        args += exchange.inputs
    return pl.pallas_call(
        body, name=name, grid=(nt,), in_specs=in_specs, out_specs=out_specs, out_shape=out_shape,
        scratch_shapes=scratch, compiler_params=_params(("arbitrary",)),
    )(*args)


def _mm_tn_call(name, a, b, tm, tn, tk):
    s, m = a.shape
    n = b.shape[1]
    tk = min(tk, s)

    def body(a_ref, b_ref, o_ref):
        d = _dg(a_ref[...], b_ref[...], _TN)
        first = pl.program_id(2) == 0

        @pl.when(first)
        def _():
            o_ref[...] = d

        @pl.when(jnp.logical_not(first))
        def _():
            o_ref[...] += d

    return pl.pallas_call(
        body, name=name, grid=(m // tm, n // tn, s // tk),
        in_specs=[pl.BlockSpec((tk, tm), lambda i, j, k: (k, i)), pl.BlockSpec((tk, tn), lambda i, j, k: (k, j))],
        out_specs=pl.BlockSpec((tm, tn), lambda i, j, k: (i, j)),
        out_shape=jax.ShapeDtypeStruct((m, n), F32),
        compiler_params=_params(("parallel", "parallel", "arbitrary")),
    )(a, b)


def _mm_tn_parts(name, a, b, by_cols, tm=512, tk=2048):
    s, m = a.shape
    n = b.shape[1]
    r, c = (m, n // N_CHIPS) if by_cols else (m // N_CHIPS, n)
    per = r // tm
    tk = min(tk, s)

    def body(a_ref, b_ref, o_ref):
        d = _dg(a_ref[...], b_ref[...], _TN)
        first = pl.program_id(2) == 0

        @pl.when(first)
        def _():
            o_ref[0] = d

        @pl.when(jnp.logical_not(first))
        def _():
            o_ref[0] += d

    if by_cols:
        out_map = lambda i, j, k: (j, i, 0)
    else:
        out_map = lambda i, j, k: (i // per, i % per, 0)
    return pl.pallas_call(
        body, name=name, grid=(m // tm, n // c, s // tk),
        in_specs=[pl.BlockSpec((tk, tm), lambda i, j, k: (k, i)), pl.BlockSpec((tk, c), lambda i, j, k: (k, j))],
        out_specs=pl.BlockSpec((1, tm, c), out_map),
        out_shape=jax.ShapeDtypeStruct((N_CHIPS, r, c), F32),
        compiler_params=_params(("parallel", "parallel", "arbitrary")),
    )(a, b)


def _proj_tile(h, g, w):
    p = mm_nn(_rms(h, g), w)
    return (p[:, :SSD_DIM], p[:, SSD_DIM:Q_OFF], p[:, Q_OFF:Q_OFF + SB_DIM],
            p[:, Q_OFF + SB_DIM:Q_OFF + 2 * SB_DIM], p[:, Q_OFF + 2 * SB_DIM:DT_OFF], p[:, DT_OFF:])


def _proj_fwd(tag, h, g, w):
    return _rowcall(
        "proj_fwd" + tag, _proj_tile, [h], [g, w],
        [(SSD_DIM, F32), (CONV_DIM, F32), (SB_DIM, BF16), (SB_DIM, BF16), (SB_DIM, BF16), (DT_PAD, F32)], tm=512)


def _proj_bwd(tag, h, dh_out, dz, dxbc, dq, dk, dv, ddt, g, w):
    def fn(h, dh_out, dz, dxbc, dq, dk, dv, ddt, g, w):
        dp = jnp.concatenate([dz.astype(BF16), dxbc.astype(BF16), dq.astype(BF16), dk.astype(BF16),
                              dv.astype(BF16), ddt.astype(BF16)], axis=1)
        hn, vjp = jax.vjp(_rms, h, g)
        dh, dg = vjp(mm_nt(dp, w))
        return dh_out + dh, hn, dp, dg

    return _rowcall(
        "proj_bwd" + tag, fn, [h, dh_out, dz, dxbc, dq, dk, dv, ddt], [g, w],
        [(D_MODEL, F32), (D_MODEL, BF16), (IN_PAD, BF16)], [((1, D_MODEL), F32)], tm=256)


def _shift_down(x, tail, j):
    if j == 0:
        return x
    r = pltpu.roll(x, j, 0)
    rt = pltpu.roll(tail, j, 0)
    row = lax.broadcasted_iota(jnp.int32, (HALO, x.shape[1]), 0)
    first = jnp.where(row < j, rt, r[:HALO])
    if x.shape[0] == HALO:
        return first
    return jnp.concatenate([first, r[HALO:]], axis=0)


def _shift_up(x, head, j):
    if j == 0:
        return x
    n = x.shape[0]
    r = pltpu.roll(x, n - j, 0)
    rh = pltpu.roll(head, HALO - j, 0)
    row = lax.broadcasted_iota(jnp.int32, (HALO, x.shape[1]), 0)
    return jnp.concatenate([r[:n - HALO], jnp.where(row >= HALO - j, rh, r[n - HALO:])], axis=0)


def _conv_pre(x, tail, w, b):
    acc = b + w[CONV_K - 1:CONV_K] * x
    for j in range(1, CONV_K):
        acc = acc + w[CONV_K - 1 - j:CONV_K - j] * _shift_down(x, tail, j)
    return acc


def _dsilu(p):
    s = jax.nn.sigmoid(p)
    return s * (1.0 + p * (1.0 - s))


def _conv_fwd(tag, xbc, w, b, tc=512):
    s, c = xbc.shape
    tc = min(tc, s)
    per = tc // HALO

    def body(x_ref, prev_ref, w_ref, b_ref, o_ref):
        tail = jnp.where(pl.program_id(0) > 0, prev_ref[...], 0.0)
        o_ref[...] = jax.nn.silu(_conv_pre(x_ref[...], tail, w_ref[...], b_ref[...]))

    return pl.pallas_call(
        body, name="conv_fwd" + tag, grid=(s // tc,),
        in_specs=[pl.BlockSpec((tc, c), lambda i: (i, 0)),
                  pl.BlockSpec((HALO, c), lambda i: (jnp.maximum(i * per - 1, 0), 0)),
                  pl.BlockSpec((CONV_K, c), lambda i: (0, 0)), pl.BlockSpec((1, c), lambda i: (0, 0))],
        out_specs=pl.BlockSpec((tc, c), lambda i: (i, 0)),
        out_shape=jax.ShapeDtypeStruct((s, c), F32),
        compiler_params=_params(("arbitrary",)),
    )(xbc, xbc, w, b)


def _conv_bwd(tag, xbc, dact, w, b, tc=512):
    s, c = xbc.shape
    tc = min(tc, s)
    per = tc // HALO
    nt = s // tc
    last_blk = s // HALO - 1

    def body(x_ref, prev_ref, next_ref, d_ref, dnext_ref, w_ref, b_ref, dx_ref, dw_ref, db_ref):
        i = pl.program_id(0)
        x = x_ref[...]
        wv = w_ref[...]
        tail = jnp.where(i > 0, prev_ref[...], 0.0)
        dpre = d_ref[...] * _dsilu(_conv_pre(x, tail, wv, b_ref[...]))
        pre_n = _conv_pre(next_ref[...], x[tc - HALO:], wv, b_ref[...])
        dpre_n = jnp.where(i < nt - 1, dnext_ref[...] * _dsilu(pre_n), 0.0)
        dx = wv[CONV_K - 1:CONV_K] * dpre
        for j in range(1, CONV_K):
            dx = dx + wv[CONV_K - 1 - j:CONV_K - j] * _shift_up(dpre, dpre_n, j)
        dx_ref[...] = dx
        dws = [jnp.sum(dpre * _shift_down(x, tail, CONV_K - 1 - k), axis=0, keepdims=True) for k in range(CONV_K)]
        dwv = jnp.concatenate(dws, axis=0)
        dbv = jnp.sum(dpre, axis=0, keepdims=True)

        @pl.when(i == 0)
        def _():
            dw_ref[...] = dwv
            db_ref[...] = dbv

        @pl.when(i > 0)
        def _():
            dw_ref[...] += dwv
            db_ref[...] += dbv

    tile = pl.BlockSpec((tc, c), lambda i: (i, 0))
    prev = pl.BlockSpec((HALO, c), lambda i: (jnp.maximum(i * per - 1, 0), 0))
    nxt = pl.BlockSpec((HALO, c), lambda i: (jnp.minimum((i + 1) * per, last_blk), 0))
    return pl.pallas_call(
        body, name="conv_bwd" + tag, grid=(nt,),
        in_specs=[tile, prev, nxt, tile, nxt, pl.BlockSpec((CONV_K, c), lambda i: (0, 0)),
                  pl.BlockSpec((1, c), lambda i: (0, 0))],
        out_specs=[tile, pl.BlockSpec((CONV_K, c), lambda i: (0, 0)), pl.BlockSpec((1, c), lambda i: (0, 0))],
        out_shape=[jax.ShapeDtypeStruct((s, c), F32), jax.ShapeDtypeStruct((CONV_K, c), F32),
                   jax.ShapeDtypeStruct((1, c), F32)],
        compiler_params=_params(("arbitrary",)),
    )(xbc, xbc, xbc, dact, dact, w, b)


def _ssd_chunk(xs, bm, cm, dtr, z, dt_bias, a_log, d_skip, g, s_prev):
    n = CHUNK
    row = lax.broadcasted_iota(jnp.int32, (n, n), 0)
    col = lax.broadcasted_iota(jnp.int32, (n, n), 1)
    causal = row >= col
    dt = jax.nn.softplus(dtr + dt_bias)
    a_c = dt * (-jnp.exp(a_log))
    hi = lax.Precision.HIGHEST
    a_cum = jnp.dot(causal.astype(F32), a_c, precision=hi, preferred_element_type=F32)
    a_cum_t = lax.dot_general(a_c, (row <= col).astype(F32), (_TN, ((), ())), precision=hi,
                              preferred_element_type=F32)
    p, st = SSD_HEAD_DIM, SSD_STATE
    heads = range(SSD_HEADS)
    grp = [h // (SSD_HEADS // SSD_GROUPS) for h in heads]
    bgs = [bm[:, k * st:(k + 1) * st] for k in range(SSD_GROUPS)]
    cgs = [cm[:, k * st:(k + 1) * st] for k in range(SSD_GROUPS)]
    cb = [mm_nt(cgs[k], bgs[k]) for k in range(SSD_GROUPS)]
    acols = [a_cum[:, h:h + 1] for h in heads]
    a_lasts = [a_cum[n - 1:n, h:h + 1] for h in heads]
    xhs = [xs[:, h * p:(h + 1) * p] for h in heads]
    sps = [s_prev[h * p:(h + 1) * p, :] for h in heads]
    xdts = [xhs[h] * dt[:, h:h + 1] for h in heads]
    decays = [jnp.exp(jnp.where(causal, acols[h] - a_cum_t[h:h + 1, :], -jnp.inf)) for h in heads]
    y_offs = [mm_nt(cgs[grp[h]], sps[h]) for h in heads]
    y_diags = [mm_nn(cb[grp[h]] * decays[h], xdts[h]) for h in heads]
    states = [mm_tn(xdts[h] * jnp.exp(a_lasts[h] - acols[h]), bgs[grp[h]]) for h in heads]
    s_new = [sps[h] * jnp.exp(a_lasts[h]) + states[h] for h in heads]
    ys = [y_diags[h] + y_offs[h] * jnp.exp(acols[h]) + d_skip[:, h:h + 1] * xhs[h] for h in heads]
    y = jnp.concatenate(ys, axis=1) * jax.nn.silu(z)
    return _rms(y, g), jnp.concatenate(s_new, axis=0)


def _split_xbc(t):
    return t[:, :SSD_DIM], t[:, SSD_DIM:SSD_DIM + GN], t[:, SSD_DIM + GN:]


def _ssd_fwd(tag, xact, dtr, z, dt_bias, a_log, d_skip, g, exchange=None):
    s = xact.shape[0]
    nc = s // CHUNK
    srows = SSD_HEADS * SSD_HEAD_DIM
    n_xi = len(exchange.inputs) if exchange else 0

    def body(*refs):
        x_ref, dt_ref, z_ref, b_ref, al_ref, ds_ref, g_ref = refs[:7]
        y_ref, st_ref = refs[7 + n_xi:9 + n_xi]
        state = refs[-3] if exchange else refs[-1]
        if exchange:
            mine = exchange.pick(refs, 7, 2)
            exchange.before_work(mine, pl.program_id(0) == 0)

        @pl.when(pl.program_id(0) == 0)
        def _():
            state[...] = jnp.zeros_like(state)

        sp = state[...]
        st_ref[0] = sp
        xs, bm, cm = _split_xbc(x_ref[...])
        y, sn = _ssd_chunk(xs, bm, cm, dt_ref[...][:, :SSD_HEADS], z_ref[...], b_ref[...], al_ref[...],
                           ds_ref[...], g_ref[...], sp)
        y_ref[...] = y
        state[...] = sn
        if exchange:
            exchange.after_work(mine, pl.program_id(0) == nc - 1, mid_done=False)

    small = pl.BlockSpec((1, SSD_HEADS), lambda i: (0, 0))
    in_specs = [pl.BlockSpec((CHUNK, CONV_DIM), lambda i: (i, 0)), pl.BlockSpec((CHUNK, DT_PAD), lambda i: (i, 0)),
                pl.BlockSpec((CHUNK, SSD_DIM), lambda i: (i, 0)), small, small, small,
                pl.BlockSpec((1, SSD_DIM), lambda i: (0, 0))]
    out_specs = [pl.BlockSpec((CHUNK, SSD_DIM), lambda i: (i, 0)),
                 pl.BlockSpec((1, srows, SSD_STATE), lambda i: (i, 0, 0))]
    out_shape = [jax.ShapeDtypeStruct((s, SSD_DIM), F32), jax.ShapeDtypeStruct((nc, srows, SSD_STATE), F32)]
    args, scratch = [xact, dtr, z, dt_bias, a_log, d_skip, g], [pltpu.VMEM((srows, SSD_STATE), F32)]
    if exchange:
        x_in, x_out, sems = exchange.specs()
        in_specs, out_specs, scratch = in_specs + x_in, out_specs + x_out, scratch + sems
        out_shape, args = out_shape + exchange.outputs, args + exchange.inputs
    return pl.pallas_call(
        body, name="ssd_fwd" + tag, grid=(nc,), in_specs=in_specs, out_specs=out_specs, out_shape=out_shape,
        scratch_shapes=scratch, compiler_params=_params(("arbitrary",)),
    )(*args)


def _ssd_bwd(tag, xact, dtr, z, states, dy, dt_bias, a_log, d_skip, g):
    s = xact.shape[0]
    nc = s // CHUNK
    srows = SSD_HEADS * SSD_HEAD_DIM

    def body(x_ref, dt_ref, z_ref, sp_ref, dy_ref, b_ref, al_ref, ds_ref, g_ref,
             dx_ref, ddt_ref, dz_ref, db_ref, dal_ref, dds_ref, dg_ref, dstate):
        first = pl.program_id(0) == 0

        @pl.when(first)
        def _():
            dstate[...] = jnp.zeros_like(dstate)

        xs, bm, cm = _split_xbc(x_ref[...])
        _, vjp = jax.vjp(_ssd_chunk, xs, bm, cm, dt_ref[...][:, :SSD_HEADS], z_ref[...], b_ref[...], al_ref[...],
                         ds_ref[...], g_ref[...], sp_ref[0])
        dxs, dbm, dcm, ddt, dz, db, dal, dds, dg, dsp = vjp((dy_ref[...], dstate[...]))
        dx_ref[...] = jnp.concatenate([dxs, dbm, dcm], axis=1)
        ddt_ref[...] = jnp.concatenate([ddt, jnp.zeros((CHUNK, DT_PAD - SSD_HEADS), F32)], axis=1)
        dz_ref[...] = dz
        dstate[...] = dsp

        @pl.when(first)
        def _():
            db_ref[...] = db
            dal_ref[...] = dal
            dds_ref[...] = dds
            dg_ref[...] = dg

        @pl.when(jnp.logical_not(first))
        def _():
            db_ref[...] += db
            dal_ref[...] += dal
            dds_ref[...] += dds
            dg_ref[...] += dg

    def rev(c):
        return lambda i: (nc - 1 - i, 0)

    small = pl.BlockSpec((1, SSD_HEADS), lambda i: (0, 0))
    gspec = pl.BlockSpec((1, SSD_DIM), lambda i: (0, 0))
    return pl.pallas_call(
        body, name="ssd_bwd" + tag, grid=(nc,),
        in_specs=[pl.BlockSpec((CHUNK, CONV_DIM), rev(0)), pl.BlockSpec((CHUNK, DT_PAD), rev(0)),
                  pl.BlockSpec((CHUNK, SSD_DIM), rev(0)),
                  pl.BlockSpec((1, srows, SSD_STATE), lambda i: (nc - 1 - i, 0, 0)),
                  pl.BlockSpec((CHUNK, SSD_DIM), rev(0)), small, small, small, gspec],
        out_specs=[pl.BlockSpec((CHUNK, CONV_DIM), rev(0)), pl.BlockSpec((CHUNK, DT_PAD), rev(0)),
                   pl.BlockSpec((CHUNK, SSD_DIM), rev(0)), small, small, small, gspec],
        out_shape=[jax.ShapeDtypeStruct((s, CONV_DIM), F32), jax.ShapeDtypeStruct((s, DT_PAD), F32),
                   jax.ShapeDtypeStruct((s, SSD_DIM), F32), jax.ShapeDtypeStruct((1, SSD_HEADS), F32),
                   jax.ShapeDtypeStruct((1, SSD_HEADS), F32), jax.ShapeDtypeStruct((1, SSD_HEADS), F32),
                   jax.ShapeDtypeStruct((1, SSD_DIM), F32)],
        scratch_shapes=[pltpu.VMEM((srows, SSD_STATE), F32)],
        compiler_params=_params(("arbitrary",)),
    )(xact, dtr, z, states, dy, dt_bias, a_log, d_skip, g)


TQ = 128
TK = 128
SB_SCALE = 1.0 / math.sqrt(SB_HEAD_DIM)


def _split2(x):
    hi = x.astype(BF16)
    return hi, (x - hi.astype(F32)).astype(BF16)


TK_WIDE = 256
SB_UNDERFLOW = -110.0


def _sb_logits(qhs, kb, t0, s0, masked):
    zs = [_dg(qh, kb, _NT) for qh in qhs]
    mask = None
    if masked:
        t_pos = t0 + lax.broadcasted_iota(jnp.int32, zs[0].shape, 0)
        s_pos = s0 + lax.broadcasted_iota(jnp.int32, zs[0].shape, 1)
        mask = s_pos < t_pos
    lbs = [jnp.minimum(z, 0.0) - jnp.log(1.0 + jnp.exp(-jnp.abs(z))) for z in zs]
    lss = [lb - z for lb, z in zip(lbs, zs)]
    if masked:
        lss = [jnp.where(mask, ls, 0.0) for ls in lss]
    return lbs, lss, mask


def _running_sums(xs, starts, u, reverse, two_terms=True):
    nsub = xs[0].shape[1] // TK
    order = list(reversed(range(nsub))) if reverse else list(range(nsub))
    chunks = [[x[:, c * TK:(c + 1) * TK] for c in range(nsub)] for x in xs]
    sums = [[(_lane_sums(xc, u) if two_terms else _dg(xc, u, _NN)) for xc in row] for row in chunks]
    out = []
    for row, srow, run in zip(chunks, sums, starts):
        parts = [None] * nsub
        for c in order:
            parts[c] = run + srow[c]
            run = run + jnp.sum(row[c], axis=1, keepdims=True)
        out.append((parts[0] if nsub == 1 else jnp.concatenate(parts, axis=1), run))
    return out


def _lane_sums(x, u):
    hi, lo = _split2(x)
    return _dg(hi, u, _NN) + _dg(lo, u, _NN)


def _tri(cmp):
    j = lax.broadcasted_iota(jnp.int32, (TK, TK), 0)
    s = lax.broadcasted_iota(jnp.int32, (TK, TK), 1)
    return cmp(j, s).astype(BF16)


def _sb_fwd(tag, q, k, v, exchange=None):
    s = q.shape[0]
    npair = SB_DIM // LANES
    nq = s // TQ
    wide = min(TK_WIDE, s)
    per = wide // TQ
    n_xi = len(exchange.inputs) if exchange else 0

    def body(*refs):
        q_ref, k_ref, v_ref = refs[:3]
        o_ref, t_ref = refs[3 + n_xi:5 + n_xi]
        tq = pl.program_id(1)
        if exchange:
            pair = pl.program_id(0)
            mine = exchange.pick(refs, 3, 2)
            exchange.before_work(mine, jnp.logical_and(pair == 0, tq == 0),
                                 jnp.logical_and(pair == npair - 1, tq == 0))
        diag = tq // per
        qp = q_ref[...]
        lane = lax.broadcasted_iota(jnp.int32, (1, LANES), 1)
        u_gt = _tri(lambda j, s: j > s)
        heads = range(LANES // SB_HEAD_DIM)
        hms = [(lane // SB_HEAD_DIM) == hh for hh in heads]
        qhs = [jnp.where(hm, qp, jnp.zeros_like(qp)) * SB_SCALE for hm in hms]

        def block(wb, carry, masked):
            off = pl.multiple_of(wb * wide, wide)
            kb = k_ref[pl.ds(off, wide), :]
            vb = v_ref[pl.ds(off, wide), :]
            lbs, lss, mask = _sb_logits(qhs, kb, tq * TQ, wb * wide, masked)
            sums = _running_sums(lss, [c[0] for c in carry], u_gt, reverse=True)
            ws = [jnp.exp(lb + later) for lb, (later, _) in zip(lbs, sums)]
            if masked:
                ws = [jnp.where(mask, w, 0.0) for w in ws]
            pvs = [_dg(w, vb, _NN) for w in ws]
            return tuple((r, c[1] + pv) for (_, r), c, pv in zip(sums, carry, pvs))

        def more(c):
            alive = jnp.max(c[1][0][0])
            for hh in heads[1:]:
                alive = jnp.maximum(alive, jnp.max(c[1][hh][0]))
            return jnp.logical_and(c[0] >= 0, alive > SB_UNDERFLOW)

        start = tuple((jnp.zeros((TQ, 1), F32), jnp.zeros((TQ, LANES), F32)) for _ in heads)
        wb, done = lax.while_loop(more, lambda c: (c[0] - 1, block(c[0], c[1], False)),
                                  (diag - 1, block(diag, start, True)))
        first = (wb + 1).astype(F32)
        out = jnp.zeros((TQ, LANES), F32)
        tot = jnp.zeros((TQ, LANES), F32)
        for hh in heads:
            r, acc = done[hh]
            out = out + jnp.where(hms[hh], acc, 0.0)
            tot = tot + jnp.where(hms[hh], jnp.where(lane % SB_HEAD_DIM == 1, first, r), 0.0)
        o_ref[...] = out
        t_ref[...] = tot
        if exchange:
            exchange.after_work(mine, jnp.logical_and(pair == npair - 1, tq == nq - 1), mid_done=True)

    tile = pl.BlockSpec((TQ, LANES), lambda p, t: (t, p))
    full = pl.BlockSpec((s, LANES), lambda p, t: (0, p))
    in_specs, out_specs = [tile, full, full], [tile, tile]
    out_shape = [jax.ShapeDtypeStruct((s, SB_DIM), F32)] * 2
    args, scratch = [q, k, v], []
    if exchange:
        x_in, x_out, scratch = exchange.specs()
        in_specs, out_specs = in_specs + x_in, out_specs + x_out
        out_shape, args = out_shape + exchange.outputs, args + exchange.inputs
    return pl.pallas_call(
        body, name="sb_fwd" + tag, grid=(npair, nq), in_specs=in_specs, out_specs=out_specs, out_shape=out_shape,
        scratch_shapes=scratch, compiler_params=_params(("arbitrary", "arbitrary")),
    )(*args)


def _sb_bwd(tag, q, k, v, tot, do, exchange=None):
    s = q.shape[0]
    npair = SB_DIM // LANES
    nq = s // TQ
    wide = min(TK_WIDE, s)
    per = wide // TQ
    n_xi = len(exchange.inputs) if exchange else 0

    def body(*refs):
        q_ref, k_ref, v_ref, t_ref, do_ref = refs[:5]
        dq_ref, dk_ref, dv_ref = refs[5 + n_xi:8 + n_xi]
        tq = pl.program_id(1)
        if exchange:
            pair = pl.program_id(0)
            mine = exchange.pick(refs, 5, 3)
            exchange.before_work(mine, jnp.logical_and(pair == 0, tq == 0),
                                 jnp.logical_and(pair == npair - 1, tq == 0))
        diag = tq // per

        @pl.when(tq == 0)
        def _():
            dk_ref[...] = jnp.zeros_like(dk_ref)
            dv_ref[...] = jnp.zeros_like(dv_ref)

        qp = q_ref[...]
        dop = do_ref[...]
        totp = t_ref[...]
        lane = lax.broadcasted_iota(jnp.int32, (1, LANES), 1)
        u_le = _tri(lambda j, s: j <= s)
        u_lt = _tri(lambda j, s: j < s)
        heads = range(LANES // SB_HEAD_DIM)
        hms = [(lane // SB_HEAD_DIM) == hh for hh in heads]
        qhs = [jnp.where(hm, qp, jnp.zeros_like(qp)) * SB_SCALE for hm in hms]
        dohs = [jnp.where(hm, dop, 0.0).astype(BF16) for hm in hms]
        totals = [jnp.sum(jnp.where(lane == hh * SB_HEAD_DIM, totp, 0.0), axis=1, keepdims=True) for hh in heads]
        first = jnp.max(jnp.where(lane == 1, totp, 0.0)).astype(jnp.int32)

        def block(wb, carry, masked):
            off = pl.multiple_of(wb * wide, wide)
            kb = k_ref[pl.ds(off, wide), :]
            vb = v_ref[pl.ds(off, wide), :]
            lbs, lss, mask = _sb_logits(qhs, kb, tq * TQ, wb * wide, masked)
            dws = [_dg(doh, vb, _NT) for doh in dohs]
            pres = _running_sums(lss, [c[0] for c in carry], u_le, reverse=False)
            ws = [jnp.exp(lb + (total - before)) for lb, total, (before, _) in zip(lbs, totals, pres)]
            if masked:
                ws = [jnp.where(mask, w, 0.0) for w in ws]
            gs = [w * dw for w, dw in zip(ws, dws)]
            lefts = _running_sums(gs, [c[1] for c in carry], u_lt, reverse=False, two_terms=False)
            dzs = [g - jnp.exp(lb) * (g + g_left) for g, lb, (g_left, _) in zip(gs, lbs, lefts)]
            if masked:
                dzs = [jnp.where(mask, dz, 0.0) for dz in dzs]
            dzbs = [dz.astype(BF16) for dz in dzs]
            dks = [_dg(dzb, qh, _TN) for dzb, qh in zip(dzbs, qhs)]
            dvs = [_dg(w, doh, _TN) for w, doh in zip(ws, dohs)]
            dqs = [_dg(dzb, kb, _NN) for dzb in dzbs]
            dk_ref[pl.ds(off, wide), :] += functools.reduce(jnp.add, dks)
            dv_ref[pl.ds(off, wide), :] += functools.reduce(jnp.add, dvs)
            return tuple((pre, gc, c[2] + dq) for (_, pre), (_, gc), c, dq in zip(pres, lefts, carry, dqs))

        zero = jnp.zeros((TQ, 1), F32)
        start = tuple((zero, zero, jnp.zeros((TQ, LANES), F32)) for _ in heads)
        done = block(diag, lax.fori_loop(first, diag, lambda j, c: block(j, c, False), start), True)
        dq = jnp.zeros((TQ, LANES), F32)
        for hh in heads:
            dq = dq + jnp.where(hms[hh], done[hh][2], 0.0)
        dq_ref[...] = dq * SB_SCALE
        if exchange:
            exchange.after_work(mine, jnp.logical_and(pair == npair - 1, tq == nq - 1), mid_done=True)

    tile = pl.BlockSpec((TQ, LANES), lambda p, t: (t, p))
    full = pl.BlockSpec((s, LANES), lambda p, t: (0, p))
    in_specs, out_specs = [tile, full, full, tile, tile], [tile, full, full]
    out_shape = [jax.ShapeDtypeStruct((s, SB_DIM), F32)] * 3
    args, scratch = [q, k, v, tot, do], []
    if exchange:
        x_in, x_out, scratch = exchange.specs()
        in_specs, out_specs = in_specs + x_in, out_specs + x_out
        out_shape, args = out_shape + exchange.outputs, args + exchange.inputs
    return pl.pallas_call(
        body, name="sb_bwd" + tag, grid=(npair, nq), in_specs=in_specs, out_specs=out_specs, out_shape=out_shape,
        scratch_shapes=scratch, compiler_params=_params(("arbitrary", "arbitrary")),
    )(*args)


def _out_tile(y_ssd, o, sb_g, w_out):
    y_all = jnp.concatenate([y_ssd, _rms(o, sb_g)], axis=1)
    return mm_nn(y_all, w_out)


def _out_fwd(tag, h, y_ssd, o, sb_g, w_out):
    return _rowcall("out_fwd" + tag, lambda h, y, o, g, w: (h + _out_tile(y, o, g, w),),
                    [h, y_ssd, o], [sb_g, w_out], [(D_MODEL, F32)], tm=512)[0]


def _out_bwd(tag, y_ssd, o, dh, sb_g, w_out):
    def fn(y, o, dh, g, w):
        _, vjp = jax.vjp(_out_tile, y, o, g, w.astype(F32))
        return vjp(dh)

    return _rowcall("out_bwd" + tag, fn, [y_ssd, o, dh], [sb_g, w_out], [(SSD_DIM, F32), (SB_DIM, F32)],
                    [((1, SB_DIM), F32), ((D_MODEL, D_MODEL), F32)], tm=256)


def _mem_tile(mem, g, w_k, w_v):
    m = _rms(mem, g)
    return mm_nn(m, w_k), mm_nn(m, w_v)


def _mem_fwd(tag, mem, g, w_k, w_v):
    return _rowcall("mem_fwd" + tag, _mem_tile, [mem], [g, w_k, w_v], [(XA_DIM, F32), (XA_DIM, F32)], tm=256)


def _mem_bwd(tag, mem, dkx, dvx, g, w_k, w_v):
    def fn(mem, dkx, dvx, g, w_k, w_v):
        _, vjp = jax.vjp(lambda g, a, b: _mem_tile(mem, g, a, b), g, w_k.astype(F32), w_v.astype(F32))
        return vjp((dkx, dvx))

    return _rowcall("mem_bwd" + tag, fn, [mem, dkx, dvx], [g, w_k, w_v], [],
                    [((1, D_MODEL), F32), ((D_MODEL, XA_DIM), F32), ((D_MODEL, XA_DIM), F32)], tm=256)


def _xattn_tile(h, g, w_q, kx, vx, w_o):
    q = mm_nn(_rms(h, g), w_q)
    scale = 1.0 / math.sqrt(XA_HEAD_DIM)
    outs = []
    for i in range(XA_HEADS):
        sl = slice(i * XA_HEAD_DIM, (i + 1) * XA_HEAD_DIM)
        p = jax.nn.softmax(mm_nt(q[:, sl], kx[:, sl]) * scale, axis=-1)
        outs.append(mm_nn(p, vx[:, sl]))
    return mm_nn(jnp.concatenate(outs, axis=1), w_o)


def _xattn_fwd(tag, h, g, w_q, kx, vx, w_o):
    return _rowcall("xattn_fwd" + tag, lambda h, g, wq, kx, vx, wo: (h + _xattn_tile(h, g, wq, kx, vx, wo),),
                    [h], [g, w_q, kx, vx, w_o], [(D_MODEL, F32)], tm=512)[0]


def _xattn_bwd(tag, h, dh_out, g, w_q, kx, vx, w_o, exchange=None):
    def fn(h, dh_out, g, w_q, kx, vx, w_o):
        _, vjp = jax.vjp(_xattn_tile, h, g, w_q.astype(F32), kx, vx, w_o.astype(F32))
        dh, dg, dwq, dkx, dvx, dwo = vjp(dh_out)
        return dh_out + dh, dg, dwq, dkx, dvx, dwo

    mlen = kx.shape[0]
    return _rowcall("xattn_bwd" + tag, fn, [h, dh_out], [g, w_q, kx, vx, w_o], [(D_MODEL, F32)],
                    [((1, D_MODEL), F32), ((D_MODEL, XA_DIM), F32), ((mlen, XA_DIM), F32), ((mlen, XA_DIM), F32),
                     ((XA_DIM, D_MODEL), F32)], tm=256, exchange=exchange)


def _mlp_fwd(tag, h, g, w1, w2):
    def fn(h, g, w1, w2):
        u = jnp.square(jnp.maximum(mm_nn(_rms(h, g), w1), 0.0))
        return (h + mm_nn(u, w2),)

    return _rowcall("mlp_fwd" + tag, fn, [h], [g, w1, w2], [(D_MODEL, F32)], tm=256)[0]


def _mlp_bwd(tag, h, dh_out, g, w1, w2, exchange=None):
    def fn(h, dh_out, g, w1, w2):
        hn, vjp = jax.vjp(_rms, h, g)
        r = jnp.maximum(mm_nn(hn, w1), 0.0)
        dob = dh_out.astype(BF16)
        dp = mm_nt(dob, w2) * (2.0 * r)
        dh, dg = vjp(mm_nt(dp, w1))
        return dh_out + dh, hn, dp, r * r, dob, dg

    return _rowcall("mlp_bwd" + tag, fn, [h, dh_out], [g, w1, w2],
                    [(D_MODEL, F32), (D_MODEL, BF16), (D_FF, BF16), (D_FF, BF16), (D_MODEL, BF16)],
                    [((1, D_MODEL), F32)], tm=256, exchange=exchange)


def _head(h, g, target):
    def lossfn(h, g, t):
        err = jnp.square(_rms(h, g) - t)
        return 0.5 * jnp.sum(jnp.mean(err, axis=-1))

    def fn(h, t, g):
        loss, vjp = jax.vjp(lambda h, g: lossfn(h, g, t), h, g)
        dh, dg = vjp(jnp.ones((), F32))
        return dh, jnp.full((1, LANES), loss, F32), dg

    return _rowcall("head", fn, [h, target], [g], [(D_MODEL, F32)], [((1, LANES), F32), ((1, D_MODEL), F32)], tm=512)


def _row(v):
    return v.reshape(1, -1)


class LocalPlan:
    def __init__(self, mats):
        self.mats = mats

    def weights(self, l):
        return self.mats[l]

    def carried_by(self, kernel, l):
        return None

    def carried_out(self, kernel, l, outs):
        pass

    def mlp_grads_done(self, l, gm):
        pass

    def grads_done(self, l, gm):
        pass


def _local_step(x, mem, target, sw, plan):
    h = x
    saved = []
    for l in range(DEPTH):
        tag = str(l)
        m = plan.weights(l)
        z, xbc, q, k, v, dtr = _proj_fwd(tag, h, _row(sw["norm_mix_g"][l]), m["w_in"])
        xact = _conv_fwd(tag, xbc, sw["conv_w"][l], _row(sw["conv_b"][l]))
        y_ssd, states, *carried = _ssd_fwd(tag, xact, dtr, z, _row(sw["dt_bias"][l]), _row(sw["a_log"][l]),
                                           _row(sw["d_skip"][l]), _row(sw["ssd_norm_g"][l]),
                                           exchange=plan.carried_by("ssd_fwd", l))
        plan.carried_out("ssd_fwd", l, carried)
        o, sb_tot, *carried = _sb_fwd(tag, q, k, v, exchange=plan.carried_by("sb_fwd", l))
        plan.carried_out("sb_fwd", l, carried)
        h1 = _out_fwd(tag, h, y_ssd, o, _row(sw["sb_norm_g"][l]), m["w_out"])
        kx, vx = _mem_fwd(tag, mem, _row(sw["norm_mem_g"][l]), m["w_xk"], m["w_xv"])
        h2 = _xattn_fwd(tag, h1, _row(sw["norm_xa_g"][l]), m["w_xq"], kx, vx, m["w_xo"])
        h3 = _mlp_fwd(tag, h2, _row(sw["norm_ff_g"][l]), m["w_ff1"], m["w_ff2"])
        saved.append((h, z, xbc, q, k, v, dtr, xact, y_ssd, states, o, sb_tot, h1, kx, vx, h2))
        h = h3

    dh, loss, d_final = _head(h, _row(sw["final_g"]), target)
    gm = [dict() for _ in range(DEPTH)]
    gs = {name: [None] * DEPTH for name in ("norm_mix_g", "conv_w", "conv_b", "dt_bias", "a_log", "d_skip",
                                            "ssd_norm_g", "sb_norm_g", "norm_xa_g", "norm_mem_g", "norm_ff_g")}
    for l in reversed(range(DEPTH)):
        tag = str(l)
        m = plan.weights(l)
        h0, z, xbc, q, k, v, dtr, xact, y_ssd, states, o, sb_tot, h1, kx, vx, h2 = saved[l]
        dh2, hn_b, dp_b, a_b, do_b, gs["norm_ff_g"][l], *carried = _mlp_bwd(
            tag, h2, dh, _row(sw["norm_ff_g"][l]), m["w_ff1"], m["w_ff2"], exchange=plan.carried_by("mlp_bwd", l))
        plan.carried_out("mlp_bwd", l, carried)
        gm[l]["w_ff1"] = _mm_tn_parts("dw_ff1" + tag, hn_b, dp_b, True)
        gm[l]["w_ff2"] = _mm_tn_parts("dw_ff2" + tag, a_b, do_b, False)
        plan.mlp_grads_done(l, gm[l])
        dh1, gs["norm_xa_g"][l], gm[l]["w_xq"], dkx, dvx, gm[l]["w_xo"], *carried = _xattn_bwd(
            tag, h1, dh2, _row(sw["norm_xa_g"][l]), m["w_xq"], kx, vx, m["w_xo"],
            exchange=plan.carried_by("xattn_bwd", l))
        plan.carried_out("xattn_bwd", l, carried)
        gs["norm_mem_g"][l], gm[l]["w_xk"], gm[l]["w_xv"] = _mem_bwd(
            tag, mem, dkx, dvx, _row(sw["norm_mem_g"][l]), m["w_xk"], m["w_xv"])
        dy_ssd, do, gs["sb_norm_g"][l], gm[l]["w_out"] = _out_bwd(
            tag, y_ssd, o, dh1, _row(sw["sb_norm_g"][l]), m["w_out"])
        dq, dk, dv, *carried = _sb_bwd(tag, q, k, v, sb_tot, do, exchange=plan.carried_by("sb_bwd", l))
        plan.carried_out("sb_bwd", l, carried)
        dxact, ddtr, dz, gs["dt_bias"][l], gs["a_log"][l], gs["d_skip"][l], gs["ssd_norm_g"][l] = _ssd_bwd(
            tag, xact, dtr, z, states, dy_ssd, _row(sw["dt_bias"][l]), _row(sw["a_log"][l]),
            _row(sw["d_skip"][l]), _row(sw["ssd_norm_g"][l]))
        dxbc, gs["conv_w"][l], gs["conv_b"][l] = _conv_bwd(tag, xbc, dxact, sw["conv_w"][l], _row(sw["conv_b"][l]))
        dh, hn_b, dp_b, gs["norm_mix_g"][l] = _proj_bwd(
            tag, h0, dh1, dz, dxbc, dq, dk, dv, ddtr, _row(sw["norm_mix_g"][l]), m["w_in"])
        gm[l]["w_in"] = _mm_tn_call("dw_in" + tag, hn_b, dp_b, 512, IN_PAD, 1024)
        plan.grads_done(l, gm[l])
    gs["final_g"] = d_final
    return loss, dh, gm, gs


ANY = pl.BlockSpec(memory_space=pl.ANY)
VMEM_SPEC = pl.BlockSpec(memory_space=pltpu.VMEM)


def _place():
    return lax.axis_index("x"), lax.axis_index("y"), lax.axis_index("c")


def _other_chips(x, y):
    return [(1 - x, y), (x, 1 - y), (1 - x, 1 - y)]


def _remote(send_sems, recv_sems, idx, src, dst, to):
    return pltpu.make_async_remote_copy(src_ref=src, dst_ref=dst, send_sem=send_sems.at[idx],
                                        recv_sem=recv_sems.at[idx], device_id=to, device_id_type=MESH)


def _run_exchange(name, ex):
    def body(*refs):
        mine = ex.pick(refs, 0, 0)
        ex.start(*mine)
        if ex.mid is not None:
            ex.mid(*mine)
        ex.end(*mine)

    x_in, x_out, scratch = ex.specs()
    return pl.pallas_call(body, name=name, in_specs=x_in, out_specs=x_out, out_shape=ex.outputs,
                          scratch_shapes=scratch)(*ex.inputs)


def _gather_exchange(layer, shards):
    n = len(shards)
    outs = [jax.ShapeDtypeStruct((N_CHIPS,) + s.shape[1:], s.dtype) for s in shards]

    def start(w_refs, o_refs, ss, rs):
        x, y, c = _place()

        @pl.when(c == layer)
        def _():
            for i in range(n):
                for kk, (cx, cy) in enumerate(_other_chips(x, y)):
                    _remote(ss, rs, 6 * i + kk, w_refs[i].at[layer], o_refs[i].at[2 * x + y], (cx, cy, layer)).start()

    def mid(w_refs, o_refs, ss, rs):
        x, y, c = _place()

        @pl.when(c == layer)
        def _():
            for i in range(n):
                for kk, (cx, cy) in enumerate(_other_chips(x, y)):
                    got = o_refs[i].at[2 * cx + cy]
                    _remote(ss, rs, 6 * i + kk, got, got, (x, y, c)).wait_recv()
                    _remote(ss, rs, 6 * i + 3 + kk, got, got, (x, y, 1 - layer)).start()

    def end(w_refs, o_refs, ss, rs):
        x, y, c = _place()
        for i in range(n):
            for kk, (cx, cy) in enumerate(_other_chips(x, y)):
                got = o_refs[i].at[2 * cx + cy]

                @pl.when(c == layer)
                def _():
                    _remote(ss, rs, 6 * i + kk, w_refs[i].at[layer], got, (x, y, c)).wait_send()
                    _remote(ss, rs, 6 * i + 3 + kk, got, got, (x, y, c)).wait_send()

                @pl.when(c != layer)
                def _():
                    _remote(ss, rs, 6 * i + 3 + kk, got, got, (x, y, c)).wait_recv()

    return Exchange(shards, outs, 6 * n, start, end, mid)


def _handover_exchange(layer, grads):
    n = len(grads)
    outs = [jax.ShapeDtypeStruct(g.shape, g.dtype) for g in grads]

    def start(g_refs, o_refs, ss, rs):
        x, y, c = _place()

        @pl.when(c != layer)
        def _():
            for i in range(n):
                _remote(ss, rs, i, g_refs[i], o_refs[i], (x, y, layer)).start()

    def end(g_refs, o_refs, ss, rs):
        x, y, c = _place()
        for i in range(n):
            @pl.when(c != layer)
            def _():
                _remote(ss, rs, i, g_refs[i], o_refs[i], (x, y, c)).wait_send()

            @pl.when(c == layer)
            def _():
                _remote(ss, rs, i, g_refs[i], o_refs[i], (x, y, c)).wait_recv()

    return Exchange(grads, outs, n, start, end)


def _scatter_exchange(layer, parts):
    n = len(parts)
    outs = [jax.ShapeDtypeStruct(p.shape, p.dtype) for p in parts]

    def start(s_refs, o_refs, ss, rs):
        x, y, c = _place()

        @pl.when(c == layer)
        def _():
            for i in range(n):
                for kk, (cx, cy) in enumerate(_other_chips(x, y)):
                    _remote(ss, rs, 3 * i + kk, s_refs[i].at[2 * cx + cy], o_refs[i].at[2 * x + y],
                            (cx, cy, layer)).start()

    def end(s_refs, o_refs, ss, rs):
        x, y, c = _place()

        @pl.when(c == layer)
        def _():
            for i in range(n):
                for kk, (cx, cy) in enumerate(_other_chips(x, y)):
                    got = o_refs[i].at[2 * cx + cy]
                    _remote(ss, rs, 3 * i + kk, got, got, (x, y, c)).wait_recv()
            for i in range(n):
                for kk, (cx, cy) in enumerate(_other_chips(x, y)):
                    _remote(ss, rs, 3 * i + kk, s_refs[i].at[2 * cx + cy], o_refs[i].at[2 * x + y],
                            (x, y, c)).wait_send()

    return Exchange(parts, outs, 3 * n, start, end)


def _return_exchange(reduced):
    flat = [g for layer in range(DEPTH) for g in reduced[layer]]
    n = len(reduced[0])
    outs = [jax.ShapeDtypeStruct(g.shape, g.dtype) for g in flat]

    def start(g_refs, o_refs, ss, rs):
        x, y, c = _place()
        for layer in range(DEPTH):
            @pl.when(c == layer)
            def _():
                for i in range(n):
                    k = layer * n + i
                    _remote(ss, rs, k, g_refs[k], o_refs[k], (x, y, 1 - layer)).start()

    def end(g_refs, o_refs, ss, rs):
        x, y, c = _place()
        for layer in range(DEPTH):
            for i in range(n):
                k = layer * n + i

                @pl.when(c == layer)
                def _():
                    _remote(ss, rs, k, g_refs[k], o_refs[k], (x, y, c)).wait_send()

                @pl.when(c != layer)
                def _():
                    _remote(ss, rs, k, g_refs[k], o_refs[k], (x, y, c)).wait_recv()

    return Exchange(flat, outs, DEPTH * n, start, end)


def _gather_small(tag, buf):
    shape = buf.shape

    def body(b_ref, o_ref, sum_ref, send_sems, recv_sems, local_sem):
        x, y, c = _place()
        me = 4 * x + 2 * y + c
        mine = pltpu.make_async_copy(b_ref, o_ref.at[me], local_sem)
        mine.start()
        flips = [(dx, dy, dc) for dx in (0, 1) for dy in (0, 1) for dc in (0, 1) if (dx, dy, dc) != (0, 0, 0)]
        sends = []

        def peer(dx, dy, dc):
            return (1 - x if dx else x, 1 - y if dy else y, 1 - c if dc else c)

        for kk, flip in enumerate(flips):
            cp = pltpu.make_async_remote_copy(src_ref=b_ref, dst_ref=o_ref.at[me], send_sem=send_sems.at[kk],
                                              recv_sem=recv_sems.at[kk], device_id=peer(*flip), device_id_type=MESH)
            cp.start()
            sends.append(cp)
        for kk, flip in enumerate(flips):
            px, py, pc = peer(*flip)
            frm = 4 * px + 2 * py + pc
            pltpu.make_async_remote_copy(src_ref=b_ref, dst_ref=o_ref.at[frm], send_sem=send_sems.at[kk],
                                         recv_sem=recv_sems.at[kk], device_id=(x, y, c),
                                         device_id_type=MESH).wait_recv()
        for cp in sends:
            cp.wait_send()
        mine.wait()
        total = o_ref[0]
        for d in range(1, N_DEV):
            total = total + o_ref[d]
        sum_ref[...] = total

    return pl.pallas_call(
        body, name="gather_small" + tag, in_specs=[VMEM_SPEC], out_specs=[VMEM_SPEC, VMEM_SPEC],
        out_shape=[jax.ShapeDtypeStruct((N_DEV,) + shape, buf.dtype), jax.ShapeDtypeStruct(shape, buf.dtype)],
        scratch_shapes=[pltpu.SemaphoreType.DMA((N_DEV - 1,)), pltpu.SemaphoreType.DMA((N_DEV - 1,)),
                        pltpu.SemaphoreType.DMA],
    )(buf)


def _add_handed(tag, layer, g, r, tr=256):
    _, rows, cols = g.shape
    tr = min(tr, rows)

    def body(g_ref, r_ref, o_ref):
        @pl.when(lax.axis_index("c") == layer)
        def _():
            o_ref[...] = (g_ref[...] + r_ref[...]).astype(o_ref.dtype)

    spec = pl.BlockSpec((1, tr, cols), lambda p, i: (p, i, 0))
    return pl.pallas_call(
        body, name="add_handed_" + tag, grid=(N_CHIPS, rows // tr), in_specs=[spec, spec], out_specs=spec,
        out_shape=jax.ShapeDtypeStruct(g.shape, BF16), compiler_params=_params(("arbitrary", "arbitrary")),
    )(g, r)


def _sum_parts(tag, layer, own, parts, tr=256):
    _, rows, cols = parts.shape
    tr = min(tr, rows)
    chip = (2 * lax.axis_index("x") + lax.axis_index("y")).astype(jnp.int32).reshape(1)

    def body(c_ref, own_ref, p1_ref, p2_ref, p3_ref, o_ref):
        @pl.when(lax.axis_index("c") == layer)
        def _():
            total = own_ref[0].astype(F32)
            for p_ref in (p1_ref, p2_ref, p3_ref):
                total = total + p_ref[0].astype(F32)
            o_ref[...] = total

    def after(kk):
        return pl.BlockSpec((1, tr, cols), lambda i, c_ref: ((c_ref[0] + kk) % N_CHIPS, i, 0))

    return pl.pallas_call(
        body, name="sum_parts_" + tag,
        grid_spec=pltpu.PrefetchScalarGridSpec(
            num_scalar_prefetch=1, grid=(rows // tr,), in_specs=[after(0), after(1), after(2), after(3)],
            out_specs=pl.BlockSpec((tr, cols), lambda i, c_ref: (i, 0))),
        out_shape=jax.ShapeDtypeStruct((rows, cols), F32),
        compiler_params=_params(("arbitrary",)),
    )(chip, own, parts, parts, parts)


def _adamw_math(w, g, m, v):
    m = ADAM_B1 * m + (1.0 - ADAM_B1) * g
    v = ADAM_B2 * v + (1.0 - ADAM_B2) * jnp.square(g)
    m_hat = m / (1.0 - ADAM_B1 ** ADAM_STEP)
    v_hat = v / (1.0 - ADAM_B2 ** ADAM_STEP)
    delta = -ADAM_LR * (m_hat / (jnp.sqrt(v_hat) + ADAM_EPS) + ADAM_WD * w)
    return delta, m, v


def _adamw(tag, w, g, m, v, tr=256):
    rows, cols = w.shape
    tr = min(tr, rows)

    def body(w_ref, g_ref, m_ref, v_ref, d_ref, nm_ref, nv_ref):
        d_ref[...], nm_ref[...], nv_ref[...] = _adamw_math(w_ref[...], g_ref[...], m_ref[...], v_ref[...])

    spec = pl.BlockSpec((tr, cols), lambda i: (i, 0))
    return pl.pallas_call(
        body, name="adamw_" + tag, grid=(rows // tr,), in_specs=[spec] * 4, out_specs=[spec] * 3,
        out_shape=[jax.ShapeDtypeStruct(w.shape, F32)] * 3,
        compiler_params=_params(("parallel",)),
    )(w, g, m, v)


def _w_in_to_padded(w):
    d0 = SSD_DIM + CONV_DIM
    return jnp.concatenate([w[:, :d0], w[:, d0 + SSD_HEADS:], w[:, d0:d0 + SSD_HEADS],
                            jnp.zeros((w.shape[0], DT_PAD - SSD_HEADS), w.dtype)], axis=1)


def _w_in_from_padded(w):
    d0 = SSD_DIM + CONV_DIM
    return jnp.concatenate([w[:, :d0], w[:, DT_OFF:DT_OFF + SSD_HEADS], w[:, d0:DT_OFF]], axis=1)


def _small_layout():
    return (("norm_mix_g", 0, 0, D_MODEL), ("norm_xa_g", 1, 0, D_MODEL), ("norm_mem_g", 2, 0, D_MODEL),
            ("norm_ff_g", 3, 0, D_MODEL), ("conv_b", 4, 0, CONV_DIM), ("ssd_norm_g", 5, 0, SSD_DIM),
            ("sb_norm_g", 5, SSD_DIM, SB_DIM), ("dt_bias", 6, 0, SSD_HEADS), ("a_log", 6, LANES, SSD_HEADS),
            ("d_skip", 6, 2 * LANES, SSD_HEADS))


def _pack_small(gs, loss):
    lay = _small_layout()
    args = [gs[name][l] for l in range(DEPTH) for name, _, _, _ in lay]
    args += [gs["conv_w"][l] for l in range(DEPTH)] + [gs["final_g"], loss]
    n_lay = len(lay)

    def body(*refs):
        o_ref = refs[-1]
        o_ref[...] = jnp.zeros_like(o_ref)
        for l in range(DEPTH):
            for i, (_, rr, c0, width) in enumerate(lay):
                row = l * _SM_PER_LAYER + rr
                o_ref[row:row + 1, c0:c0 + width] = refs[l * n_lay + i][...]
            row = _SM_CONVW + l * CONV_K
            o_ref[row:row + CONV_K, 0:CONV_DIM] = refs[DEPTH * n_lay + l][...]
        o_ref[_SM_FINAL:_SM_FINAL + 1, :] = refs[DEPTH * n_lay + DEPTH][...]
        o_ref[_SM_LOSS:_SM_LOSS + 1, 0:LANES] = refs[DEPTH * n_lay + DEPTH + 1][...]

    return pl.pallas_call(
        body, name="pack_small", in_specs=[VMEM_SPEC] * len(args), out_specs=VMEM_SPEC,
        out_shape=jax.ShapeDtypeStruct((SMALL_ROWS, PACK_COLS), F32),
    )(*args)


def _small_update(buf, w, mom, var):
    lay = _small_layout()
    names = [name for name, _, _, _ in lay] + ["final_g", "conv_w"]
    conv_cols = CONV_DIM // N_CHIPS
    shapes2d = {name: (DEPTH, width) for name, _, _, width in lay}
    shapes2d["final_g"] = (1, D_MODEL)
    shapes2d["conv_w"] = (DEPTH * CONV_K, conv_cols)
    args = [buf]
    for src in (w, mom, var):
        args += [src[name].reshape(shapes2d[name]) for name in names]
    n = len(names)

    def body(*refs):
        b_ref = refs[0]
        w_refs, m_refs, v_refs = refs[1:1 + n], refs[1 + n:1 + 2 * n], refs[1 + 2 * n:1 + 3 * n]
        outs = refs[1 + 3 * n:]
        chip = 2 * lax.axis_index("x") + lax.axis_index("y")
        for i, name in enumerate(names):
            if name == "final_g":
                g = b_ref[_SM_FINAL:_SM_FINAL + 1, :]
            elif name == "conv_w":
                rows = b_ref[_SM_CONVW:_SM_CONVW + DEPTH * CONV_K, 0:CONV_DIM]
                g = jnp.zeros((DEPTH * CONV_K, conv_cols), F32)
                for j in range(N_CHIPS):
                    g = g + jnp.where(chip == j, rows[:, j * conv_cols:(j + 1) * conv_cols], 0.0)
            else:
                _, rr, c0, width = lay[i]
                g = jnp.concatenate([b_ref[l * _SM_PER_LAYER + rr:l * _SM_PER_LAYER + rr + 1, c0:c0 + width]
                                     for l in range(DEPTH)], axis=0)
            d, m2, v2 = _adamw_math(w_refs[i][...], g, m_refs[i][...], v_refs[i][...])
            outs[i][...] = g
            outs[n + i][...] = d
            outs[2 * n + i][...] = m2
            outs[3 * n + i][...] = v2

    out_shape = [jax.ShapeDtypeStruct(shapes2d[name], F32) for _ in range(4) for name in names]
    res = pl.pallas_call(
        body, name="small_update", in_specs=[VMEM_SPEC] * len(args), out_specs=[VMEM_SPEC] * (4 * n),
        out_shape=out_shape,
    )(*args)
    return tuple({name: res[k * n + i].reshape(w[name].shape) for i, name in enumerate(names)} for k in range(4))


CONV_BLOCK_ROWS = 8


def _conv_w_block(cw):
    flat = cw.reshape(-1)
    pad = jnp.zeros((CONV_BLOCK_ROWS * PACK_COLS - flat.shape[0],), F32)
    return jnp.concatenate([flat, pad]).reshape(CONV_BLOCK_ROWS, PACK_COLS)


def _conv_w_from_slots(slots):
    cols = CONV_DIM // N_CHIPS
    n = DEPTH * CONV_K * cols
    shards = [slots[2 * j].reshape(-1)[:n].reshape(DEPTH, CONV_K, cols) for j in range(N_CHIPS)]
    return jnp.concatenate(shards, axis=2)


SMALL_NAMES = ("norm_mix_g", "conv_b", "dt_bias", "a_log", "d_skip", "ssd_norm_g", "sb_norm_g", "norm_xa_g",
               "norm_mem_g", "norm_ff_g", "final_g")
WEIGHT_ORDER = ("norm_mix_g", "w_in", "conv_w", "conv_b", "dt_bias", "a_log", "d_skip", "ssd_norm_g", "sb_norm_g",
                "w_out", "norm_xa_g", "norm_mem_g", "w_xq", "w_xk", "w_xv", "w_xo", "norm_ff_g", "w_ff1", "w_ff2",
                "final_g")


_ALL = tuple(range(len(MATS)))
_FIRST_USED = tuple(i for i in _ALL if MATS[i][0] == "w_in")
_MLP = tuple(i for i in _ALL if MATS[i][0] in ("w_ff1", "w_ff2"))
_NOT_FIRST = tuple(i for i in _ALL if i not in _FIRST_USED)
_NOT_MLP = tuple(i for i in _ALL if i not in _MLP)


class PipelinedPlan(LocalPlan):
    def __init__(self, shards):
        self.shards = shards
        self.chip = 2 * lax.axis_index("x") + lax.axis_index("y")
        self.mats = [dict() for _ in range(DEPTH)]
        n = len(MATS)
        self.parts = [[None] * n for _ in range(DEPTH)]
        self.to_chips = [[None] * n for _ in range(DEPTH)]
        self.reduced = [[None] * n for _ in range(DEPTH)]
        self.riders = {}
        self._gathered(0, _FIRST_USED, _run_exchange("gather_first", self._gather(0, _FIRST_USED)))
        self._ride("ssd_fwd", 0, self._gather(0, _NOT_FIRST), lambda outs: self._gathered(0, _NOT_FIRST, outs))
        for l in range(1, DEPTH):
            self._ride("sb_fwd", l - 1, self._gather(l, _ALL), lambda outs, l=l: self._gathered(l, _ALL, outs))

    def _ride(self, kernel, l, exchange, then):
        self.riders.setdefault((kernel, l), []).append((exchange, then))

    def carried_by(self, kernel, l):
        exchange = None
        for ex, _ in self.riders.get((kernel, l), []):
            exchange = _both(exchange, ex)
        return exchange

    def carried_out(self, kernel, l, outs):
        for ex, then in self.riders.pop((kernel, l), []):
            then(outs[:len(ex.outputs)])
            outs = outs[len(ex.outputs):]

    def _gather(self, l, which):
        return _gather_exchange(l, [self.shards[i] for i in which])

    def _gathered(self, l, which, outs):
        for i, theirs in zip(which, outs):
            name, _, axis = MATS[i]
            full = jnp.concatenate([jnp.where(self.chip == j, self.shards[i][l], theirs[j]) for j in range(N_CHIPS)],
                                   axis=axis)
            self.mats[l][name] = _w_in_to_padded(full) if name == "w_in" else full

    def _set_parts(self, l, which, gm):
        for i in which:
            name, _, axis = MATS[i]
            g = _w_in_from_padded(gm[name]) if name == "w_in" else gm[name]
            if g.ndim == 2 and axis == 0:
                g = g.reshape((N_CHIPS, g.shape[0] // N_CHIPS, g.shape[1]))
            elif g.ndim == 2:
                g = jnp.swapaxes(g.reshape((g.shape[0], N_CHIPS, g.shape[1] // N_CHIPS)), 0, 1)
            self.parts[l][i] = g

    def _handover(self, l, which):
        return _handover_exchange(l, [self.parts[l][i] for i in which])

    def _handed(self, l, which, outs):
        for i, r in zip(which, outs):
            self.to_chips[l][i] = _add_handed(MATS[i][0] + str(l), l, self.parts[l][i], r)

    def _scatter(self, l, which):
        return _scatter_exchange(l, [self.to_chips[l][i] for i in which])

    def _scattered(self, l, which, outs):
        for i, got in zip(which, outs):
            self.reduced[l][i] = _sum_parts(MATS[i][0] + str(l), l, self.to_chips[l][i], got)

    def _send_behind(self, l, which, hand_kernel, cross_kernel, host):
        def handed(outs):
            self._handed(l, which, outs)
            self._ride(cross_kernel, host, self._scatter(l, which), lambda o: self._scattered(l, which, o))

        self._ride(hand_kernel, host, self._handover(l, which), handed)

    def mlp_grads_done(self, l, gm):
        self._set_parts(l, _MLP, gm)
        self._send_behind(l, _MLP, "xattn_bwd", "sb_bwd", l)

    def grads_done(self, l, gm):
        self._set_parts(l, _NOT_MLP, gm)
        if l > 0:
            self._send_behind(l, _NOT_MLP, "mlp_bwd", "sb_bwd", l - 1)
        else:
            self._handed(0, _NOT_MLP, _run_exchange("handover_last", self._handover(0, _NOT_MLP)))
            self._scattered(0, _NOT_MLP, _run_exchange("scatter_last", self._scatter(0, _NOT_MLP)))

    def reduced_gradients(self):
        returned = _run_exchange("return_reduced", _return_exchange(self.reduced))
        n = len(MATS)
        core = lax.axis_index("c")
        return {name: jnp.stack([jnp.where(core == l, self.reduced[l][i], returned[l * n + i]) for l in range(DEPTH)])
                for i, (name, _, _) in enumerate(MATS)}


def kernel(x, mem, norm_mix_g, w_in, conv_w, conv_b, dt_bias, a_log, d_skip, ssd_norm_g, sb_norm_g, w_out, norm_xa_g, norm_mem_g, w_xq, w_xk, w_xv, w_xo, norm_ff_g, w_ff1, w_ff2, final_g, loss_target, m_norm_mix_g, m_w_in, m_conv_w, m_conv_b, m_dt_bias, m_a_log, m_d_skip, m_ssd_norm_g, m_sb_norm_g, m_w_out, m_norm_xa_g, m_norm_mem_g, m_w_xq, m_w_xk, m_w_xv, m_w_xo, m_norm_ff_g, m_w_ff1, m_w_ff2, m_final_g, v_norm_mix_g, v_w_in, v_conv_w, v_conv_b, v_dt_bias, v_a_log, v_d_skip, v_ssd_norm_g, v_sb_norm_g, v_w_out, v_norm_xa_g, v_norm_mem_g, v_w_xq, v_w_xk, v_w_xv, v_w_xo, v_norm_ff_g, v_w_ff1, v_w_ff2, v_final_g):
    w = dict(norm_mix_g=norm_mix_g, w_in=w_in, conv_w=conv_w, conv_b=conv_b, dt_bias=dt_bias, a_log=a_log,
             d_skip=d_skip, ssd_norm_g=ssd_norm_g, sb_norm_g=sb_norm_g, w_out=w_out, norm_xa_g=norm_xa_g,
             norm_mem_g=norm_mem_g, w_xq=w_xq, w_xk=w_xk, w_xv=w_xv, w_xo=w_xo, norm_ff_g=norm_ff_g, w_ff1=w_ff1,
             w_ff2=w_ff2, final_g=final_g)
    mom = dict(norm_mix_g=m_norm_mix_g, w_in=m_w_in, conv_w=m_conv_w, conv_b=m_conv_b, dt_bias=m_dt_bias,
               a_log=m_a_log, d_skip=m_d_skip, ssd_norm_g=m_ssd_norm_g, sb_norm_g=m_sb_norm_g, w_out=m_w_out,
               norm_xa_g=m_norm_xa_g, norm_mem_g=m_norm_mem_g, w_xq=m_w_xq, w_xk=m_w_xk, w_xv=m_w_xv, w_xo=m_w_xo,
               norm_ff_g=m_norm_ff_g, w_ff1=m_w_ff1, w_ff2=m_w_ff2, final_g=m_final_g)
    var = dict(norm_mix_g=v_norm_mix_g, w_in=v_w_in, conv_w=v_conv_w, conv_b=v_conv_b, dt_bias=v_dt_bias,
               a_log=v_a_log, d_skip=v_d_skip, ssd_norm_g=v_ssd_norm_g, sb_norm_g=v_sb_norm_g, w_out=v_w_out,
               norm_xa_g=v_norm_xa_g, norm_mem_g=v_norm_mem_g, w_xq=v_w_xq, w_xk=v_w_xk, w_xv=v_w_xv, w_xo=v_w_xo,
               norm_ff_g=v_norm_ff_g, w_ff1=v_w_ff1, w_ff2=v_w_ff2, final_g=v_final_g)
    conv_slots, _ = _gather_small("_conv", _conv_w_block(conv_w))
    sw = {name: w[name] for name in SMALL_NAMES}
    sw["conv_w"] = _conv_w_from_slots(conv_slots)
    plan = PipelinedPlan([w[name].astype(BF16) for name, _, _ in MATS])
    loss, grad_x, gm, gs = _local_step(x[0], mem[0], loss_target[0], sw, plan)
    g_mats = plan.reduced_gradients()

    _, small_sum = _gather_small("_grads", _pack_small(gs, loss))
    loss_out = small_sum[_SM_LOSS, 0]

    grads, deltas, new_m, new_v = {}, {}, {}, {}
    for name, _, _ in MATS:
        g = g_mats[name]
        cols = g.shape[-1]
        d, nm, nv = _adamw(name, w[name].reshape(-1, cols), g.reshape(-1, cols), mom[name].reshape(-1, cols),
                           var[name].reshape(-1, cols))
        grads[name] = g
        deltas[name], new_m[name], new_v[name] = (t.reshape(g.shape) for t in (d, nm, nv))
    g_s, d_s, m_s, v_s = _small_update(small_sum, w, mom, var)
    for name in g_s:
        grads[name], deltas[name], new_m[name], new_v[name] = g_s[name], d_s[name], m_s[name], v_s[name]

    return (loss_out, grad_x[None], *[grads[n] for n in WEIGHT_ORDER], *[deltas[n] for n in WEIGHT_ORDER],
            *[new_m[n] for n in WEIGHT_ORDER], *[new_v[n] for n in WEIGHT_ORDER])
```

```python
import functools
import math

import jax
import jax.numpy as jnp
from jax import lax
from jax.experimental import pallas as pl
from jax.experimental.pallas import tpu as pltpu

F32 = jnp.float32
BF16 = jnp.bfloat16
MESH = pl.DeviceIdType.MESH

D_MODEL = 1024
DEPTH = 2
SSD_DIM = 512
SSD_HEAD_DIM = 64
SSD_HEADS = 8
SSD_GROUPS = 2
SSD_STATE = 64
CONV_K = 4
CHUNK = 128
SB_DIM = 512
SB_HEAD_DIM = 64
XA_HEADS = 4
XA_HEAD_DIM = 128
XA_DIM = 512
D_FF = 4096
EPS = 1e-5
GN = SSD_GROUPS * SSD_STATE
CONV_DIM = SSD_DIM + 2 * GN
IN_DIM = SSD_DIM + CONV_DIM + SSD_HEADS + 3 * SB_DIM
LANES = 128
DT_PAD = LANES
IN_PAD = SSD_DIM + CONV_DIM + 3 * SB_DIM + DT_PAD
Q_OFF = SSD_DIM + CONV_DIM
DT_OFF = Q_OFF + 3 * SB_DIM
HALO = 8

ADAM_LR = 0.001
ADAM_B1 = 0.9
ADAM_B2 = 0.999
ADAM_EPS = 1e-08
ADAM_WD = 0.01
ADAM_STEP = 10

N_CHIPS = 4
N_DEV = 8
PACK_COLS = 1024
VMEM_LIMIT = 56 * 1024 * 1024

MATS = (
    ("w_in", (D_MODEL, IN_DIM), 1),
    ("w_out", (D_MODEL, D_MODEL), 0),
    ("w_xq", (D_MODEL, XA_DIM), 0),
    ("w_xk", (D_MODEL, XA_DIM), 0),
    ("w_xv", (D_MODEL, XA_DIM), 0),
    ("w_xo", (XA_DIM, D_MODEL), 1),
    ("w_ff1", (D_MODEL, D_FF), 1),
    ("w_ff2", (D_FF, D_MODEL), 0),
)


SMALL_ROWS = 24
_SM_PER_LAYER = 7
_SM_FINAL = 14
_SM_CONVW = 15
_SM_LOSS = 23


_NN = ((1,), (0,))
_NT = ((1,), (1,))
_TN = ((0,), (0,))


def _dg(a, b, dims):
    return lax.dot_general(a.astype(BF16), b.astype(BF16), (dims, ((), ())), preferred_element_type=F32)


@jax.custom_vjp
def mm_nn(a, b):
    return _dg(a, b, _NN)


@jax.custom_vjp
def mm_nt(a, b):
    return _dg(a, b, _NT)


@jax.custom_vjp
def mm_tn(a, b):
    return _dg(a, b, _TN)


def _nn_fwd(a, b):
    return _dg(a, b, _NN), (a, b)


def _nn_bwd(res, g):
    a, b = res
    return mm_nt(g, b).astype(a.dtype), mm_tn(a, g).astype(b.dtype)


def _nt_fwd(a, b):
    return _dg(a, b, _NT), (a, b)


def _nt_bwd(res, g):
    a, b = res
    return mm_nn(g, b).astype(a.dtype), mm_tn(g, a).astype(b.dtype)


def _tn_fwd(a, b):
    return _dg(a, b, _TN), (a, b)


def _tn_bwd(res, g):
    a, b = res
    return mm_nt(b, g).astype(a.dtype), mm_nn(a, g).astype(b.dtype)


mm_nn.defvjp(_nn_fwd, _nn_bwd)
mm_nt.defvjp(_nt_fwd, _nt_bwd)
mm_tn.defvjp(_tn_fwd, _tn_bwd)


def _rms(x, g):
    return x * lax.rsqrt(jnp.mean(x * x, axis=-1, keepdims=True) + EPS) * g


def _params(sem=None, vmem=VMEM_LIMIT):
    return pltpu.CompilerParams(dimension_semantics=sem, vmem_limit_bytes=vmem)


class Exchange:
    def __init__(self, inputs, outputs, n_sems, start, end, mid=None):
        self.inputs, self.outputs, self.n_sems = list(inputs), list(outputs), n_sems
        self.start, self.mid, self.end = start, mid, end

    def specs(self):
        hbm = pl.BlockSpec(memory_space=pl.ANY)
        sems = [pltpu.SemaphoreType.DMA((self.n_sems,)), pltpu.SemaphoreType.DMA((self.n_sems,))]
        return [hbm] * len(self.inputs), [hbm] * len(self.outputs), sems

    def pick(self, refs, n_before_in, n_before_out):
        n_in, n_out = len(self.inputs), len(self.outputs)
        o0 = n_before_in + n_in + n_before_out
        return refs[n_before_in:n_before_in + n_in], refs[o0:o0 + n_out], refs[-2], refs[-1]

    def before_work(self, mine, first, mid=None):
        @pl.when(first)
        def _():
            self.start(*mine)

        if self.mid is not None and mid is not None:
            @pl.when(mid)
            def _():
                self.mid(*mine)

    def after_work(self, mine, last, mid_done):
        @pl.when(last)
        def _():
            if self.mid is not None and not mid_done:
                self.mid(*mine)
            self.end(*mine)


class _Shifted:
    def __init__(self, ref, base):
        self.ref, self.base = ref, base

    @property
    def at(self):
        return self

    def __getitem__(self, idx):
        return self.ref.at[self.base + idx]


def _both(a, b):
    if a is None or b is None:
        return a or b
    n_i, n_o, n_s = len(a.inputs), len(a.outputs), a.n_sems

    def joined(fa, fb):
        def f(i_refs, o_refs, ss, rs):
            if fa is not None:
                fa(i_refs[:n_i], o_refs[:n_o], ss, rs)
            if fb is not None:
                fb(i_refs[n_i:], o_refs[n_o:], _Shifted(ss, n_s), _Shifted(rs, n_s))
        return f

    mid = joined(a.mid, b.mid) if (a.mid is not None or b.mid is not None) else None
    return Exchange(a.inputs + b.inputs, a.outputs + b.outputs, n_s + b.n_sems, joined(a.start, b.start),
                    joined(a.end, b.end), mid)


def _rowcall(name, fn, rows, fulls, row_out, acc_out=(), tm=256, exchange=None):
    s = rows[0].shape[0]
    tm = min(tm, s)
    nt = s // tm
    n_r, n_f, n_ro, n_ao = len(rows), len(fulls), len(row_out), len(acc_out)
    n_xi = len(exchange.inputs) if exchange else 0

    def body(*refs):
        if exchange:
            mine = exchange.pick(refs, n_r + n_f, n_ro + n_ao)
            exchange.before_work(mine, pl.program_id(0) == 0)
        ins = [r[...] for r in refs[: n_r + n_f]]
        outs = fn(*ins)
        o_refs = refs[n_r + n_f + n_xi:]
        for o_ref, val in zip(o_refs[:n_ro], outs[:n_ro]):
            o_ref[...] = val.astype(o_ref.dtype)
        if n_ao:
            first = pl.program_id(0) == 0

            @pl.when(first)
            def _():
                for o_ref, val in zip(o_refs[n_ro:], outs[n_ro:]):
                    o_ref[...] = val.astype(o_ref.dtype)

            @pl.when(jnp.logical_not(first))
            def _():
                for o_ref, val in zip(o_refs[n_ro:], outs[n_ro:]):
                    o_ref[...] += val.astype(o_ref.dtype)
        if exchange:
            exchange.after_work(mine, pl.program_id(0) == nt - 1, mid_done=False)

    in_specs = [pl.BlockSpec((tm, a.shape[1]), lambda i: (i, 0)) for a in rows]
    in_specs += [pl.BlockSpec(a.shape, lambda i: (0, 0), pipeline_mode=pl.Buffered(1)) for a in fulls]
    out_specs = [pl.BlockSpec((tm, c), lambda i: (i, 0)) for c, _ in row_out]
    out_specs += [pl.BlockSpec(shape, lambda i: (0, 0)) for shape, _ in acc_out]
    out_shape = [jax.ShapeDtypeStruct((s, c), dt) for c, dt in row_out]
    out_shape += [jax.ShapeDtypeStruct(shape, dt) for shape, dt in acc_out]
    args, scratch = [*rows, *fulls], []
    if exchange:
        x_in, x_out, scratch = exchange.specs()
        in_specs += x_in
        out_specs += x_out
        out_shape += exchange.outputs
        args += exchange.inputs
    return pl.pallas_call(
        body, name=name, grid=(nt,), in_specs=in_specs, out_specs=out_specs, out_shape=out_shape,
        scratch_shapes=scratch, compiler_params=_params(("arbitrary",)),
    )(*args)


def _mm_tn_call(name, a, b, tm, tn, tk):
    s, m = a.shape
    n = b.shape[1]
    tk = min(tk, s)

    def body(a_ref, b_ref, o_ref):
        d = _dg(a_ref[...], b_ref[...], _TN)
        first = pl.program_id(2) == 0

        @pl.when(first)
        def _():
            o_ref[...] = d

        @pl.when(jnp.logical_not(first))
        def _():
            o_ref[...] += d

    return pl.pallas_call(
        body, name=name, grid=(m // tm, n // tn, s // tk),
        in_specs=[pl.BlockSpec((tk, tm), lambda i, j, k: (k, i)), pl.BlockSpec((tk, tn), lambda i, j, k: (k, j))],
        out_specs=pl.BlockSpec((tm, tn), lambda i, j, k: (i, j)),
        out_shape=jax.ShapeDtypeStruct((m, n), F32),
        compiler_params=_params(("parallel", "parallel", "arbitrary")),
    )(a, b)


def _mm_tn_parts(name, a, b, by_cols, tm=512, tk=2048):
    s, m = a.shape
    n = b.shape[1]
    r, c = (m, n // N_CHIPS) if by_cols else (m // N_CHIPS, n)
    per = r // tm
    tk = min(tk, s)

    def body(a_ref, b_ref, o_ref):
        d = _dg(a_ref[...], b_ref[...], _TN)
        first = pl.program_id(2) == 0

        @pl.when(first)
        def _():
            o_ref[0] = d

        @pl.when(jnp.logical_not(first))
        def _():
            o_ref[0] += d

    if by_cols:
        out_map = lambda i, j, k: (j, i, 0)
    else:
        out_map = lambda i, j, k: (i // per, i % per, 0)
    return pl.pallas_call(
        body, name=name, grid=(m // tm, n // c, s // tk),
        in_specs=[pl.BlockSpec((tk, tm), lambda i, j, k: (k, i)), pl.BlockSpec((tk, c), lambda i, j, k: (k, j))],
        out_specs=pl.BlockSpec((1, tm, c), out_map),
        out_shape=jax.ShapeDtypeStruct((N_CHIPS, r, c), F32),
        compiler_params=_params(("parallel", "parallel", "arbitrary")),
    )(a, b)


def _proj_tile(h, g, w):
    p = mm_nn(_rms(h, g), w)
    return (p[:, :SSD_DIM], p[:, SSD_DIM:Q_OFF], p[:, Q_OFF:Q_OFF + SB_DIM],
            p[:, Q_OFF + SB_DIM:Q_OFF + 2 * SB_DIM], p[:, Q_OFF + 2 * SB_DIM:DT_OFF], p[:, DT_OFF:])


def _proj_fwd(tag, h, g, w):
    return _rowcall(
        "proj_fwd" + tag, _proj_tile, [h], [g, w],
        [(SSD_DIM, F32), (CONV_DIM, F32), (SB_DIM, BF16), (SB_DIM, BF16), (SB_DIM, BF16), (DT_PAD, F32)], tm=512)


def _proj_bwd(tag, h, dh_out, dz, dxbc, dq, dk, dv, ddt, g, w):
    def fn(h, dh_out, dz, dxbc, dq, dk, dv, ddt, g, w):
        dp = jnp.concatenate([dz.astype(BF16), dxbc.astype(BF16), dq.astype(BF16), dk.astype(BF16),
                              dv.astype(BF16), ddt.astype(BF16)], axis=1)
        hn, vjp = jax.vjp(_rms, h, g)
        dh, dg = vjp(mm_nt(dp, w))
        return dh_out + dh, hn, dp, dg

    return _rowcall(
        "proj_bwd" + tag, fn, [h, dh_out, dz, dxbc, dq, dk, dv, ddt], [g, w],
        [(D_MODEL, F32), (D_MODEL, BF16), (IN_PAD, BF16)], [((1, D_MODEL), F32)], tm=256)


def _shift_down(x, tail, j):
    if j == 0:
        return x
    r = pltpu.roll(x, j, 0)
    rt = pltpu.roll(tail, j, 0)
    row = lax.broadcasted_iota(jnp.int32, (HALO, x.shape[1]), 0)
    first = jnp.where(row < j, rt, r[:HALO])
    if x.shape[0] == HALO:
        return first
    return jnp.concatenate([first, r[HALO:]], axis=0)


def _shift_up(x, head, j):
    if j == 0:
        return x
    n = x.shape[0]
    r = pltpu.roll(x, n - j, 0)
    rh = pltpu.roll(head, HALO - j, 0)
    row = lax.broadcasted_iota(jnp.int32, (HALO, x.shape[1]), 0)
    return jnp.concatenate([r[:n - HALO], jnp.where(row >= HALO - j, rh, r[n - HALO:])], axis=0)


def _conv_pre(x, tail, w, b):
    acc = b + w[CONV_K - 1:CONV_K] * x
    for j in range(1, CONV_K):
        acc = acc + w[CONV_K - 1 - j:CONV_K - j] * _shift_down(x, tail, j)
    return acc


def _dsilu(p):
    s = jax.nn.sigmoid(p)
    return s * (1.0 + p * (1.0 - s))


def _conv_fwd(tag, xbc, w, b, tc=512):
    s, c = xbc.shape
    tc = min(tc, s)
    per = tc // HALO

    def body(x_ref, prev_ref, w_ref, b_ref, o_ref):
        tail = jnp.where(pl.program_id(0) > 0, prev_ref[...], 0.0)
        o_ref[...] = jax.nn.silu(_conv_pre(x_ref[...], tail, w_ref[...], b_ref[...]))

    return pl.pallas_call(
        body, name="conv_fwd" + tag, grid=(s // tc,),
        in_specs=[pl.BlockSpec((tc, c), lambda i: (i, 0)),
                  pl.BlockSpec((HALO, c), lambda i: (jnp.maximum(i * per - 1, 0), 0)),
                  pl.BlockSpec((CONV_K, c), lambda i: (0, 0)), pl.BlockSpec((1, c), lambda i: (0, 0))],
        out_specs=pl.BlockSpec((tc, c), lambda i: (i, 0)),
        out_shape=jax.ShapeDtypeStruct((s, c), F32),
        compiler_params=_params(("arbitrary",)),
    )(xbc, xbc, w, b)


def _conv_bwd(tag, xbc, dact, w, b, tc=512):
    s, c = xbc.shape
    tc = min(tc, s)
    per = tc // HALO
    nt = s // tc
    last_blk = s // HALO - 1

    def body(x_ref, prev_ref, next_ref, d_ref, dnext_ref, w_ref, b_ref, dx_ref, dw_ref, db_ref):
        i = pl.program_id(0)
        x = x_ref[...]
        wv = w_ref[...]
        tail = jnp.where(i > 0, prev_ref[...], 0.0)
        dpre = d_ref[...] * _dsilu(_conv_pre(x, tail, wv, b_ref[...]))
        pre_n = _conv_pre(next_ref[...], x[tc - HALO:], wv, b_ref[...])
        dpre_n = jnp.where(i < nt - 1, dnext_ref[...] * _dsilu(pre_n), 0.0)
        dx = wv[CONV_K - 1:CONV_K] * dpre
        for j in range(1, CONV_K):
            dx = dx + wv[CONV_K - 1 - j:CONV_K - j] * _shift_up(dpre, dpre_n, j)
        dx_ref[...] = dx
        dws = [jnp.sum(dpre * _shift_down(x, tail, CONV_K - 1 - k), axis=0, keepdims=True) for k in range(CONV_K)]
        dwv = jnp.concatenate(dws, axis=0)
        dbv = jnp.sum(dpre, axis=0, keepdims=True)

        @pl.when(i == 0)
        def _():
            dw_ref[...] = dwv
            db_ref[...] = dbv

        @pl.when(i > 0)
        def _():
            dw_ref[...] += dwv
            db_ref[...] += dbv

    tile = pl.BlockSpec((tc, c), lambda i: (i, 0))
    prev = pl.BlockSpec((HALO, c), lambda i: (jnp.maximum(i * per - 1, 0), 0))
    nxt = pl.BlockSpec((HALO, c), lambda i: (jnp.minimum((i + 1) * per, last_blk), 0))
    return pl.pallas_call(
        body, name="conv_bwd" + tag, grid=(nt,),
        in_specs=[tile, prev, nxt, tile, nxt, pl.BlockSpec((CONV_K, c), lambda i: (0, 0)),
                  pl.BlockSpec((1, c), lambda i: (0, 0))],
        out_specs=[tile, pl.BlockSpec((CONV_K, c), lambda i: (0, 0)), pl.BlockSpec((1, c), lambda i: (0, 0))],
        out_shape=[jax.ShapeDtypeStruct((s, c), F32), jax.ShapeDtypeStruct((CONV_K, c), F32),
                   jax.ShapeDtypeStruct((1, c), F32)],
        compiler_params=_params(("arbitrary",)),
    )(xbc, xbc, xbc, dact, dact, w, b)


def _ssd_chunk(xs, bm, cm, dtr, z, dt_bias, a_log, d_skip, g, s_prev):
    n = CHUNK
    row = lax.broadcasted_iota(jnp.int32, (n, n), 0)
    col = lax.broadcasted_iota(jnp.int32, (n, n), 1)
    causal = row >= col
    dt = jax.nn.softplus(dtr + dt_bias)
    a_c = dt * (-jnp.exp(a_log))
    hi = lax.Precision.HIGHEST
    a_cum = jnp.dot(causal.astype(F32), a_c, precision=hi, preferred_element_type=F32)
    a_cum_t = lax.dot_general(a_c, (row <= col).astype(F32), (_TN, ((), ())), precision=hi,
                              preferred_element_type=F32)
    p, st = SSD_HEAD_DIM, SSD_STATE
    heads = range(SSD_HEADS)
    grp = [h // (SSD_HEADS // SSD_GROUPS) for h in heads]
    bgs = [bm[:, k * st:(k + 1) * st] for k in range(SSD_GROUPS)]
    cgs = [cm[:, k * st:(k + 1) * st] for k in range(SSD_GROUPS)]
    cb = [mm_nt(cgs[k], bgs[k]) for k in range(SSD_GROUPS)]
    acols = [a_cum[:, h:h + 1] for h in heads]
    a_lasts = [a_cum[n - 1:n, h:h + 1] for h in heads]
    xhs = [xs[:, h * p:(h + 1) * p] for h in heads]
    sps = [s_prev[h * p:(h + 1) * p, :] for h in heads]
    xdts = [xhs[h] * dt[:, h:h + 1] for h in heads]
    decays = [jnp.exp(jnp.where(causal, acols[h] - a_cum_t[h:h + 1, :], -jnp.inf)) for h in heads]
    y_offs = [mm_nt(cgs[grp[h]], sps[h]) for h in heads]
    y_diags = [mm_nn(cb[grp[h]] * decays[h], xdts[h]) for h in heads]
    states = [mm_tn(xdts[h] * jnp.exp(a_lasts[h] - acols[h]), bgs[grp[h]]) for h in heads]
    s_new = [sps[h] * jnp.exp(a_lasts[h]) + states[h] for h in heads]
    ys = [y_diags[h] + y_offs[h] * jnp.exp(acols[h]) + d_skip[:, h:h + 1] * xhs[h] for h in heads]
    y = jnp.concatenate(ys, axis=1) * jax.nn.silu(z)
    return _rms(y, g), jnp.concatenate(s_new, axis=0)


def _split_xbc(t):
    return t[:, :SSD_DIM], t[:, SSD_DIM:SSD_DIM + GN], t[:, SSD_DIM + GN:]


def _ssd_fwd(tag, xact, dtr, z, dt_bias, a_log, d_skip, g, exchange=None):
    s = xact.shape[0]
    nc = s // CHUNK
    srows = SSD_HEADS * SSD_HEAD_DIM
    n_xi = len(exchange.inputs) if exchange else 0

    def body(*refs):
        x_ref, dt_ref, z_ref, b_ref, al_ref, ds_ref, g_ref = refs[:7]
        y_ref, st_ref = refs[7 + n_xi:9 + n_xi]
        state = refs[-3] if exchange else refs[-1]
        if exchange:
            mine = exchange.pick(refs, 7, 2)
            exchange.before_work(mine, pl.program_id(0) == 0)

        @pl.when(pl.program_id(0) == 0)
        def _():
            state[...] = jnp.zeros_like(state)

        sp = state[...]
        st_ref[0] = sp
        xs, bm, cm = _split_xbc(x_ref[...])
        y, sn = _ssd_chunk(xs, bm, cm, dt_ref[...][:, :SSD_HEADS], z_ref[...], b_ref[...], al_ref[...],
                           ds_ref[...], g_ref[...], sp)
        y_ref[...] = y
        state[...] = sn
        if exchange:
            exchange.after_work(mine, pl.program_id(0) == nc - 1, mid_done=False)

    small = pl.BlockSpec((1, SSD_HEADS), lambda i: (0, 0))
    in_specs = [pl.BlockSpec((CHUNK, CONV_DIM), lambda i: (i, 0)), pl.BlockSpec((CHUNK, DT_PAD), lambda i: (i, 0)),
                pl.BlockSpec((CHUNK, SSD_DIM), lambda i: (i, 0)), small, small, small,
                pl.BlockSpec((1, SSD_DIM), lambda i: (0, 0))]
    out_specs = [pl.BlockSpec((CHUNK, SSD_DIM), lambda i: (i, 0)),
                 pl.BlockSpec((1, srows, SSD_STATE), lambda i: (i, 0, 0))]
    out_shape = [jax.ShapeDtypeStruct((s, SSD_DIM), F32), jax.ShapeDtypeStruct((nc, srows, SSD_STATE), F32)]
    args, scratch = [xact, dtr, z, dt_bias, a_log, d_skip, g], [pltpu.VMEM((srows, SSD_STATE), F32)]
    if exchange:
        x_in, x_out, sems = exchange.specs()
        in_specs, out_specs, scratch = in_specs + x_in, out_specs + x_out, scratch + sems
        out_shape, args = out_shape + exchange.outputs, args + exchange.inputs
    return pl.pallas_call(
        body, name="ssd_fwd" + tag, grid=(nc,), in_specs=in_specs, out_specs=out_specs, out_shape=out_shape,
        scratch_shapes=scratch, compiler_params=_params(("arbitrary",)),
    )(*args)


def _ssd_bwd(tag, xact, dtr, z, states, dy, dt_bias, a_log, d_skip, g, exchange=None):
    s = xact.shape[0]
    nc = s // CHUNK
    srows = SSD_HEADS * SSD_HEAD_DIM
    n_xi = len(exchange.inputs) if exchange else 0

    def body(*refs):
        x_ref, dt_ref, z_ref, sp_ref, dy_ref, b_ref, al_ref, ds_ref, g_ref = refs[:9]
        dx_ref, ddt_ref, dz_ref, db_ref, dal_ref, dds_ref, dg_ref = refs[9 + n_xi:16 + n_xi]
        dstate = refs[-3] if exchange else refs[-1]
        first = pl.program_id(0) == 0
        if exchange:
            mine = exchange.pick(refs, 9, 7)
            exchange.before_work(mine, first)

        @pl.when(first)
        def _():
            dstate[...] = jnp.zeros_like(dstate)

        xs, bm, cm = _split_xbc(x_ref[...])
        _, vjp = jax.vjp(_ssd_chunk, xs, bm, cm, dt_ref[...][:, :SSD_HEADS], z_ref[...], b_ref[...], al_ref[...],
                         ds_ref[...], g_ref[...], sp_ref[0])
        dxs, dbm, dcm, ddt, dz, db, dal, dds, dg, dsp = vjp((dy_ref[...], dstate[...]))
        dx_ref[...] = jnp.concatenate([dxs, dbm, dcm], axis=1)
        ddt_ref[...] = jnp.concatenate([ddt, jnp.zeros((CHUNK, DT_PAD - SSD_HEADS), F32)], axis=1)
        dz_ref[...] = dz
        dstate[...] = dsp

        @pl.when(first)
        def _():
            db_ref[...] = db
            dal_ref[...] = dal
            dds_ref[...] = dds
            dg_ref[...] = dg

        @pl.when(jnp.logical_not(first))
        def _():
            db_ref[...] += db
            dal_ref[...] += dal
            dds_ref[...] += dds
            dg_ref[...] += dg

        if exchange:
            exchange.after_work(mine, pl.program_id(0) == nc - 1, mid_done=False)

    def rev(c):
        return lambda i: (nc - 1 - i, 0)

    small = pl.BlockSpec((1, SSD_HEADS), lambda i: (0, 0))
    gspec = pl.BlockSpec((1, SSD_DIM), lambda i: (0, 0))
    in_specs = [pl.BlockSpec((CHUNK, CONV_DIM), rev(0)), pl.BlockSpec((CHUNK, DT_PAD), rev(0)),
                pl.BlockSpec((CHUNK, SSD_DIM), rev(0)),
                pl.BlockSpec((1, srows, SSD_STATE), lambda i: (nc - 1 - i, 0, 0)),
                pl.BlockSpec((CHUNK, SSD_DIM), rev(0)), small, small, small, gspec]
    out_specs = [pl.BlockSpec((CHUNK, CONV_DIM), rev(0)), pl.BlockSpec((CHUNK, DT_PAD), rev(0)),
                 pl.BlockSpec((CHUNK, SSD_DIM), rev(0)), small, small, small, gspec]
    out_shape = [jax.ShapeDtypeStruct((s, CONV_DIM), F32), jax.ShapeDtypeStruct((s, DT_PAD), F32),
                 jax.ShapeDtypeStruct((s, SSD_DIM), F32), jax.ShapeDtypeStruct((1, SSD_HEADS), F32),
                 jax.ShapeDtypeStruct((1, SSD_HEADS), F32), jax.ShapeDtypeStruct((1, SSD_HEADS), F32),
                 jax.ShapeDtypeStruct((1, SSD_DIM), F32)]
    args, scratch = [xact, dtr, z, states, dy, dt_bias, a_log, d_skip, g], [pltpu.VMEM((srows, SSD_STATE), F32)]
    if exchange:
        x_in, x_out, sems = exchange.specs()
        in_specs, out_specs, scratch = in_specs + x_in, out_specs + x_out, scratch + sems
        out_shape, args = out_shape + exchange.outputs, args + exchange.inputs
    return pl.pallas_call(
        body, name="ssd_bwd" + tag, grid=(nc,), in_specs=in_specs, out_specs=out_specs, out_shape=out_shape,
        scratch_shapes=scratch, compiler_params=_params(("arbitrary",)),
    )(*args)


TQ = 128
TK = 128
SB_SCALE = 1.0 / math.sqrt(SB_HEAD_DIM)


def _split2(x):
    hi = x.astype(BF16)
    return hi, (x - hi.astype(F32)).astype(BF16)


TK_WIDE = 256
TK_NEAR = 512
SB_UNDERFLOW = -110.0


def _sb_logits(qhs, kb, t0, s0, masked):
    zs = [_dg(qh, kb, _NT) for qh in qhs]
    mask = None
    if masked:
        t_pos = t0 + lax.broadcasted_iota(jnp.int32, zs[0].shape, 0)
        s_pos = s0 + lax.broadcasted_iota(jnp.int32, zs[0].shape, 1)
        mask = s_pos < t_pos
    lbs = [jnp.minimum(z, 0.0) - jnp.log(1.0 + jnp.exp(-jnp.abs(z))) for z in zs]
    lss = [lb - z for lb, z in zip(lbs, zs)]
    if masked:
        lss = [jnp.where(mask, ls, 0.0) for ls in lss]
    return lbs, lss, mask


def _running_sums(xs, starts, u, reverse, two_terms=True):
    nsub = xs[0].shape[1] // TK
    order = list(reversed(range(nsub))) if reverse else list(range(nsub))
    chunks = [[x[:, c * TK:(c + 1) * TK] for c in range(nsub)] for x in xs]
    sums = [[(_lane_sums(xc, u) if two_terms else _dg(xc, u, _NN)) for xc in row] for row in chunks]
    out = []
    for row, srow, run in zip(chunks, sums, starts):
        parts = [None] * nsub
        for c in order:
            parts[c] = run + srow[c]
            run = run + jnp.sum(row[c], axis=1, keepdims=True)
        out.append((parts[0] if nsub == 1 else jnp.concatenate(parts, axis=1), run))
    return out


def _lane_sums(x, u):
    hi, lo = _split2(x)
    return _dg(hi, u, _NN) + _dg(lo, u, _NN)


def _tri(cmp):
    j = lax.broadcasted_iota(jnp.int32, (TK, TK), 0)
    s = lax.broadcasted_iota(jnp.int32, (TK, TK), 1)
    return cmp(j, s).astype(BF16)


def _sb_fwd(tag, q, k, v, exchange=None):
    s = q.shape[0]
    npair = SB_DIM // LANES
    nq = s // TQ
    wide = min(TK_WIDE, s)
    near = min(TK_NEAR, s)
    per = wide // TQ
    n_xi = len(exchange.inputs) if exchange else 0

    def body(*refs):
        q_ref, k_ref, v_ref = refs[:3]
        o_ref, t_ref = refs[3 + n_xi:5 + n_xi]
        tq = pl.program_id(1)
        if exchange:
            pair = pl.program_id(0)
            mine = exchange.pick(refs, 3, 2)
            exchange.before_work(mine, jnp.logical_and(pair == 0, tq == 0),
                                 jnp.logical_and(pair == npair - 1, tq == 0))
        diag = tq // per
        qp = q_ref[...]
        lane = lax.broadcasted_iota(jnp.int32, (1, LANES), 1)
        u_gt = _tri(lambda j, s: j > s)
        heads = range(LANES // SB_HEAD_DIM)
        hms = [(lane // SB_HEAD_DIM) == hh for hh in heads]
        qhs = [jnp.where(hm, qp, jnp.zeros_like(qp)) * SB_SCALE for hm in hms]

        def block(wb, width, carry, masked):
            off = pl.multiple_of(wb * wide, wide)
            kb = k_ref[pl.ds(off, width), :]
            vb = v_ref[pl.ds(off, width), :]
            lbs, lss, mask = _sb_logits(qhs, kb, tq * TQ, wb * wide, masked)
            sums = _running_sums(lss, [c[0] for c in carry], u_gt, reverse=True)
            ws = [jnp.exp(lb + later) for lb, (later, _) in zip(lbs, sums)]
            if masked:
                ws = [jnp.where(mask, w, 0.0) for w in ws]
            pvs = [_dg(w, vb, _NN) for w in ws]
            return tuple((r, c[1] + pv) for (_, r), c, pv in zip(sums, carry, pvs))

        def more(c):
            alive = jnp.max(c[1][0][0])
            for hh in heads[1:]:
                alive = jnp.maximum(alive, jnp.max(c[1][hh][0]))
            return jnp.logical_and(c[0] >= 0, alive > SB_UNDERFLOW)

        start = tuple((jnp.zeros((TQ, 1), F32), jnp.zeros((TQ, LANES), F32)) for _ in heads)
        near_blk = jnp.maximum(diag - (near // wide - 1), 0)
        wb, done = lax.while_loop(more, lambda c: (c[0] - 1, block(c[0], wide, c[1], False)),
                                  (near_blk - 1, block(near_blk, near, start, True)))
        first = (wb + 1).astype(F32)
        out = jnp.zeros((TQ, LANES), F32)
        tot = jnp.zeros((TQ, LANES), F32)
        for hh in heads:
            r, acc = done[hh]
            out = out + jnp.where(hms[hh], acc, 0.0)
            tot = tot + jnp.where(hms[hh], jnp.where(lane % SB_HEAD_DIM == 1, first, r), 0.0)
        o_ref[...] = out
        t_ref[...] = tot
        if exchange:
            exchange.after_work(mine, jnp.logical_and(pair == npair - 1, tq == nq - 1), mid_done=True)

    tile = pl.BlockSpec((TQ, LANES), lambda p, t: (t, p))
    full = pl.BlockSpec((s, LANES), lambda p, t: (0, p))
    in_specs, out_specs = [tile, full, full], [tile, tile]
    out_shape = [jax.ShapeDtypeStruct((s, SB_DIM), F32)] * 2
    args, scratch = [q, k, v], []
    if exchange:
        x_in, x_out, scratch = exchange.specs()
        in_specs, out_specs = in_specs + x_in, out_specs + x_out
        out_shape, args = out_shape + exchange.outputs, args + exchange.inputs
    return pl.pallas_call(
        body, name="sb_fwd" + tag, grid=(npair, nq), in_specs=in_specs, out_specs=out_specs, out_shape=out_shape,
        scratch_shapes=scratch, compiler_params=_params(("arbitrary", "arbitrary")),
    )(*args)


def _sb_bwd(tag, q, k, v, tot, do, exchange=None):
    s = q.shape[0]
    npair = SB_DIM // LANES
    nq = s // TQ
    wide = min(TK_WIDE, s)
    near = min(TK_NEAR, s)
    per = wide // TQ
    n_xi = len(exchange.inputs) if exchange else 0

    def body(*refs):
        q_ref, k_ref, v_ref, t_ref, do_ref = refs[:5]
        dq_ref, dk_ref, dv_ref = refs[5 + n_xi:8 + n_xi]
        tq = pl.program_id(1)
        if exchange:
            pair = pl.program_id(0)
            mine = exchange.pick(refs, 5, 3)
            exchange.before_work(mine, jnp.logical_and(pair == 0, tq == 0),
                                 jnp.logical_and(pair == npair - 1, tq == 0))
        diag = tq // per

        @pl.when(tq == 0)
        def _():
            dk_ref[...] = jnp.zeros_like(dk_ref)
            dv_ref[...] = jnp.zeros_like(dv_ref)

        qp = q_ref[...]
        dop = do_ref[...]
        totp = t_ref[...]
        lane = lax.broadcasted_iota(jnp.int32, (1, LANES), 1)
        u_le = _tri(lambda j, s: j <= s)
        u_lt = _tri(lambda j, s: j < s)
        heads = range(LANES // SB_HEAD_DIM)
        hms = [(lane // SB_HEAD_DIM) == hh for hh in heads]
        qhs = [jnp.where(hm, qp, jnp.zeros_like(qp)) * SB_SCALE for hm in hms]
        dohs = [jnp.where(hm, dop, 0.0).astype(BF16) for hm in hms]
        totals = [jnp.sum(jnp.where(lane == hh * SB_HEAD_DIM, totp, 0.0), axis=1, keepdims=True) for hh in heads]
        first = jnp.max(jnp.where(lane == 1, totp, 0.0)).astype(jnp.int32)

        def block(wb, width, carry, masked):
            off = pl.multiple_of(wb * wide, wide)
            kb = k_ref[pl.ds(off, width), :]
            vb = v_ref[pl.ds(off, width), :]
            lbs, lss, mask = _sb_logits(qhs, kb, tq * TQ, wb * wide, masked)
            dws = [_dg(doh, vb, _NT) for doh in dohs]
            pres = _running_sums(lss, [c[0] for c in carry], u_le, reverse=False)
            ws = [jnp.exp(lb + (total - before)) for lb, total, (before, _) in zip(lbs, totals, pres)]
            if masked:
                ws = [jnp.where(mask, w, 0.0) for w in ws]
            gs = [w * dw for w, dw in zip(ws, dws)]
            lefts = _running_sums(gs, [c[1] for c in carry], u_lt, reverse=False, two_terms=False)
            dzs = [g - jnp.exp(lb) * (g + g_left) for g, lb, (g_left, _) in zip(gs, lbs, lefts)]
            if masked:
                dzs = [jnp.where(mask, dz, 0.0) for dz in dzs]
            dzbs = [dz.astype(BF16) for dz in dzs]
            dks = [_dg(dzb, qh, _TN) for dzb, qh in zip(dzbs, qhs)]
            dvs = [_dg(w, doh, _TN) for w, doh in zip(ws, dohs)]
            dqs = [_dg(dzb, kb, _NN) for dzb in dzbs]
            dk_ref[pl.ds(off, width), :] += functools.reduce(jnp.add, dks)
            dv_ref[pl.ds(off, width), :] += functools.reduce(jnp.add, dvs)
            return tuple((pre, gc, c[2] + dq) for (_, pre), (_, gc), c, dq in zip(pres, lefts, carry, dqs))

        zero = jnp.zeros((TQ, 1), F32)
        start = tuple((zero, zero, jnp.zeros((TQ, LANES), F32)) for _ in heads)
        near_blk = jnp.maximum(diag - (near // wide - 1), 0)
        far = lax.fori_loop(first, near_blk, lambda j, c: block(j, wide, c, False), start)
        done = block(near_blk, near, far, True)
        dq = jnp.zeros((TQ, LANES), F32)
        for hh in heads:
            dq = dq + jnp.where(hms[hh], done[hh][2], 0.0)
        dq_ref[...] = dq * SB_SCALE
        if exchange:
            exchange.after_work(mine, jnp.logical_and(pair == npair - 1, tq == nq - 1), mid_done=True)

    tile = pl.BlockSpec((TQ, LANES), lambda p, t: (t, p))
    full = pl.BlockSpec((s, LANES), lambda p, t: (0, p))
    in_specs, out_specs = [tile, full, full, tile, tile], [tile, full, full]
    out_shape = [jax.ShapeDtypeStruct((s, SB_DIM), F32)] * 3
    args, scratch = [q, k, v, tot, do], []
    if exchange:
        x_in, x_out, scratch = exchange.specs()
        in_specs, out_specs = in_specs + x_in, out_specs + x_out
        out_shape, args = out_shape + exchange.outputs, args + exchange.inputs
    return pl.pallas_call(
        body, name="sb_bwd" + tag, grid=(npair, nq), in_specs=in_specs, out_specs=out_specs, out_shape=out_shape,
        scratch_shapes=scratch, compiler_params=_params(("arbitrary", "arbitrary")),
    )(*args)


def _out_tile(y_ssd, o, sb_g, w_out):
    y_all = jnp.concatenate([y_ssd, _rms(o, sb_g)], axis=1)
    return mm_nn(y_all, w_out)


def _out_fwd(tag, h, y_ssd, o, sb_g, w_out):
    return _rowcall("out_fwd" + tag, lambda h, y, o, g, w: (h + _out_tile(y, o, g, w),),
                    [h, y_ssd, o], [sb_g, w_out], [(D_MODEL, F32)], tm=512)[0]


def _out_bwd(tag, y_ssd, o, dh, sb_g, w_out):
    def fn(y, o, dh, g, w):
        _, vjp = jax.vjp(_out_tile, y, o, g, w.astype(F32))
        return vjp(dh)

    return _rowcall("out_bwd" + tag, fn, [y_ssd, o, dh], [sb_g, w_out], [(SSD_DIM, F32), (SB_DIM, F32)],
                    [((1, SB_DIM), F32), ((D_MODEL, D_MODEL), F32)], tm=256)


def _mem_tile(mem, g, w_k, w_v):
    m = _rms(mem, g)
    return mm_nn(m, w_k), mm_nn(m, w_v)


def _mem_fwd(tag, mem, g, w_k, w_v):
    return _rowcall("mem_fwd" + tag, _mem_tile, [mem], [g, w_k, w_v], [(XA_DIM, F32), (XA_DIM, F32)], tm=256)


def _mem_bwd(tag, mem, dkx, dvx, g, w_k, w_v):
    def fn(mem, dkx, dvx, g, w_k, w_v):
        _, vjp = jax.vjp(lambda g, a, b: _mem_tile(mem, g, a, b), g, w_k.astype(F32), w_v.astype(F32))
        return vjp((dkx, dvx))

    return _rowcall("mem_bwd" + tag, fn, [mem, dkx, dvx], [g, w_k, w_v], [],
                    [((1, D_MODEL), F32), ((D_MODEL, XA_DIM), F32), ((D_MODEL, XA_DIM), F32)], tm=256)


def _xattn_tile(h, g, w_q, kx, vx, w_o):
    q = mm_nn(_rms(h, g), w_q)
    scale = 1.0 / math.sqrt(XA_HEAD_DIM)
    outs = []
    for i in range(XA_HEADS):
        sl = slice(i * XA_HEAD_DIM, (i + 1) * XA_HEAD_DIM)
        p = jax.nn.softmax(mm_nt(q[:, sl], kx[:, sl]) * scale, axis=-1)
        outs.append(mm_nn(p, vx[:, sl]))
    return mm_nn(jnp.concatenate(outs, axis=1), w_o)


def _xattn_fwd(tag, h, g, w_q, kx, vx, w_o):
    return _rowcall("xattn_fwd" + tag, lambda h, g, wq, kx, vx, wo: (h + _xattn_tile(h, g, wq, kx, vx, wo),),
                    [h], [g, w_q, kx, vx, w_o], [(D_MODEL, F32)], tm=512)[0]


def _xattn_bwd(tag, h, dh_out, g, w_q, kx, vx, w_o, exchange=None):
    def fn(h, dh_out, g, w_q, kx, vx, w_o):
        _, vjp = jax.vjp(_xattn_tile, h, g, w_q.astype(F32), kx, vx, w_o.astype(F32))
        dh, dg, dwq, dkx, dvx, dwo = vjp(dh_out)
        return dh_out + dh, dg, dwq, dkx, dvx, dwo

    mlen = kx.shape[0]
    return _rowcall("xattn_bwd" + tag, fn, [h, dh_out], [g, w_q, kx, vx, w_o], [(D_MODEL, F32)],
                    [((1, D_MODEL), F32), ((D_MODEL, XA_DIM), F32), ((mlen, XA_DIM), F32), ((mlen, XA_DIM), F32),
                     ((XA_DIM, D_MODEL), F32)], tm=256, exchange=exchange)


def _mlp_fwd(tag, h, g, w1, w2, exchange=None):
    def fn(h, g, w1, w2):
        u = jnp.square(jnp.maximum(mm_nn(_rms(h, g), w1), 0.0))
        return (h + mm_nn(u, w2),)

    return _rowcall("mlp_fwd" + tag, fn, [h], [g, w1, w2], [(D_MODEL, F32)], tm=256, exchange=exchange)


def _mlp_bwd(tag, h, dh_out, g, w1, w2, exchange=None):
    def fn(h, dh_out, g, w1, w2):
        hn, vjp = jax.vjp(_rms, h, g)
        r = jnp.maximum(mm_nn(hn, w1), 0.0)
        dob = dh_out.astype(BF16)
        dp = mm_nt(dob, w2) * (2.0 * r)
        dh, dg = vjp(mm_nt(dp, w1))
        return dh_out + dh, hn, dp, r * r, dob, dg

    return _rowcall("mlp_bwd" + tag, fn, [h, dh_out], [g, w1, w2],
                    [(D_MODEL, F32), (D_MODEL, BF16), (D_FF, BF16), (D_FF, BF16), (D_MODEL, BF16)],
                    [((1, D_MODEL), F32)], tm=256, exchange=exchange)


def _head(h, g, target):
    def lossfn(h, g, t):
        err = jnp.square(_rms(h, g) - t)
        return 0.5 * jnp.sum(jnp.mean(err, axis=-1))

    def fn(h, t, g):
        loss, vjp = jax.vjp(lambda h, g: lossfn(h, g, t), h, g)
        dh, dg = vjp(jnp.ones((), F32))
        return dh, jnp.full((1, LANES), loss, F32), dg

    return _rowcall("head", fn, [h, target], [g], [(D_MODEL, F32)], [((1, LANES), F32), ((1, D_MODEL), F32)], tm=512)


def _row(v):
    return v.reshape(1, -1)


class LocalPlan:
    def __init__(self, mats):
        self.mats = mats

    def weights(self, l):
        return self.mats[l]

    def carried_by(self, kernel, l):
        return None

    def carried_out(self, kernel, l, outs):
        pass

    def mlp_grads_done(self, l, gm):
        pass

    def mixer_grads_done(self, l, gm):
        pass

    def grads_done(self, l, gm):
        pass


def _local_step(x, mem, target, sw, plan):
    h = x
    saved = []
    for l in range(DEPTH):
        tag = str(l)
        m = plan.weights(l)
        z, xbc, q, k, v, dtr = _proj_fwd(tag, h, _row(sw["norm_mix_g"][l]), m["w_in"])
        xact = _conv_fwd(tag, xbc, sw["conv_w"][l], _row(sw["conv_b"][l]))
        y_ssd, states, *carried = _ssd_fwd(tag, xact, dtr, z, _row(sw["dt_bias"][l]), _row(sw["a_log"][l]),
                                           _row(sw["d_skip"][l]), _row(sw["ssd_norm_g"][l]),
                                           exchange=plan.carried_by("ssd_fwd", l))
        plan.carried_out("ssd_fwd", l, carried)
        o, sb_tot, *carried = _sb_fwd(tag, q, k, v, exchange=plan.carried_by("sb_fwd", l))
        plan.carried_out("sb_fwd", l, carried)
        h1 = _out_fwd(tag, h, y_ssd, o, _row(sw["sb_norm_g"][l]), m["w_out"])
        kx, vx = _mem_fwd(tag, mem, _row(sw["norm_mem_g"][l]), m["w_xk"], m["w_xv"])
        h2 = _xattn_fwd(tag, h1, _row(sw["norm_xa_g"][l]), m["w_xq"], kx, vx, m["w_xo"])
        h3, *carried = _mlp_fwd(tag, h2, _row(sw["norm_ff_g"][l]), m["w_ff1"], m["w_ff2"],
                                exchange=plan.carried_by("mlp_fwd", l))
        plan.carried_out("mlp_fwd", l, carried)
        saved.append((h, z, xbc, q, k, v, dtr, xact, y_ssd, states, o, sb_tot, h1, kx, vx, h2))
        h = h3

    dh, loss, d_final = _head(h, _row(sw["final_g"]), target)
    gm = [dict() for _ in range(DEPTH)]
    gs = {name: [None] * DEPTH for name in ("norm_mix_g", "conv_w", "conv_b", "dt_bias", "a_log", "d_skip",
                                            "ssd_norm_g", "sb_norm_g", "norm_xa_g", "norm_mem_g", "norm_ff_g")}
    for l in reversed(range(DEPTH)):
        tag = str(l)
        m = plan.weights(l)
        h0, z, xbc, q, k, v, dtr, xact, y_ssd, states, o, sb_tot, h1, kx, vx, h2 = saved[l]
        dh2, hn_b, dp_b, a_b, do_b, gs["norm_ff_g"][l], *carried = _mlp_bwd(
            tag, h2, dh, _row(sw["norm_ff_g"][l]), m["w_ff1"], m["w_ff2"], exchange=plan.carried_by("mlp_bwd", l))
        plan.carried_out("mlp_bwd", l, carried)
        gm[l]["w_ff1"] = _mm_tn_parts("dw_ff1" + tag, hn_b, dp_b, True)
        gm[l]["w_ff2"] = _mm_tn_parts("dw_ff2" + tag, a_b, do_b, False)
        plan.mlp_grads_done(l, gm[l])
        dh1, gs["norm_xa_g"][l], gm[l]["w_xq"], dkx, dvx, gm[l]["w_xo"], *carried = _xattn_bwd(
            tag, h1, dh2, _row(sw["norm_xa_g"][l]), m["w_xq"], kx, vx, m["w_xo"],
            exchange=plan.carried_by("xattn_bwd", l))
        plan.carried_out("xattn_bwd", l, carried)
        gs["norm_mem_g"][l], gm[l]["w_xk"], gm[l]["w_xv"] = _mem_bwd(
            tag, mem, dkx, dvx, _row(sw["norm_mem_g"][l]), m["w_xk"], m["w_xv"])
        dy_ssd, do, gs["sb_norm_g"][l], gm[l]["w_out"] = _out_bwd(
            tag, y_ssd, o, dh1, _row(sw["sb_norm_g"][l]), m["w_out"])
        plan.mixer_grads_done(l, gm[l])
        dq, dk, dv, *carried = _sb_bwd(tag, q, k, v, sb_tot, do, exchange=plan.carried_by("sb_bwd", l))
        plan.carried_out("sb_bwd", l, carried)
        dxact, ddtr, dz, gs["dt_bias"][l], gs["a_log"][l], gs["d_skip"][l], gs["ssd_norm_g"][l], *carried = _ssd_bwd(
            tag, xact, dtr, z, states, dy_ssd, _row(sw["dt_bias"][l]), _row(sw["a_log"][l]),
            _row(sw["d_skip"][l]), _row(sw["ssd_norm_g"][l]), exchange=plan.carried_by("ssd_bwd", l))
        plan.carried_out("ssd_bwd", l, carried)
        dxbc, gs["conv_w"][l], gs["conv_b"][l] = _conv_bwd(tag, xbc, dxact, sw["conv_w"][l], _row(sw["conv_b"][l]))
        dh, hn_b, dp_b, gs["norm_mix_g"][l] = _proj_bwd(
            tag, h0, dh1, dz, dxbc, dq, dk, dv, ddtr, _row(sw["norm_mix_g"][l]), m["w_in"])
        gm[l]["w_in"] = _mm_tn_call("dw_in" + tag, hn_b, dp_b, 512, IN_PAD, 1024)
        plan.grads_done(l, gm[l])
    gs["final_g"] = d_final
    return loss, dh, gm, gs


ANY = pl.BlockSpec(memory_space=pl.ANY)
VMEM_SPEC = pl.BlockSpec(memory_space=pltpu.VMEM)


def _place():
    return lax.axis_index("x"), lax.axis_index("y"), lax.axis_index("c")


def _other_chips(x, y):
    return [(1 - x, y), (x, 1 - y), (1 - x, 1 - y)]


def _remote(send_sems, recv_sems, idx, src, dst, to):
    return pltpu.make_async_remote_copy(src_ref=src, dst_ref=dst, send_sem=send_sems.at[idx],
                                        recv_sem=recv_sems.at[idx], device_id=to, device_id_type=MESH)


def _run_exchange(name, ex):
    def body(*refs):
        mine = ex.pick(refs, 0, 0)
        ex.start(*mine)
        if ex.mid is not None:
            ex.mid(*mine)
        ex.end(*mine)

    x_in, x_out, scratch = ex.specs()
    return pl.pallas_call(body, name=name, in_specs=x_in, out_specs=x_out, out_shape=ex.outputs,
                          scratch_shapes=scratch)(*ex.inputs)


def _gather_exchange(layer, shards):
    n = len(shards)
    outs = [jax.ShapeDtypeStruct((N_CHIPS,) + s.shape[1:], s.dtype) for s in shards]

    def start(w_refs, o_refs, ss, rs):
        x, y, c = _place()

        @pl.when(c == layer)
        def _():
            for i in range(n):
                for kk, (cx, cy) in enumerate(_other_chips(x, y)):
                    _remote(ss, rs, 6 * i + kk, w_refs[i].at[layer], o_refs[i].at[2 * x + y], (cx, cy, layer)).start()

    def mid(w_refs, o_refs, ss, rs):
        x, y, c = _place()

        @pl.when(c == layer)
        def _():
            for i in range(n):
                for kk, (cx, cy) in enumerate(_other_chips(x, y)):
                    got = o_refs[i].at[2 * cx + cy]
                    _remote(ss, rs, 6 * i + kk, got, got, (x, y, c)).wait_recv()
                    _remote(ss, rs, 6 * i + 3 + kk, got, got, (x, y, 1 - layer)).start()

    def end(w_refs, o_refs, ss, rs):
        x, y, c = _place()
        for i in range(n):
            for kk, (cx, cy) in enumerate(_other_chips(x, y)):
                got = o_refs[i].at[2 * cx + cy]

                @pl.when(c == layer)
                def _():
                    _remote(ss, rs, 6 * i + kk, w_refs[i].at[layer], got, (x, y, c)).wait_send()
                    _remote(ss, rs, 6 * i + 3 + kk, got, got, (x, y, c)).wait_send()

                @pl.when(c != layer)
                def _():
                    _remote(ss, rs, 6 * i + 3 + kk, got, got, (x, y, c)).wait_recv()

    return Exchange(shards, outs, 6 * n, start, end, mid)


def _handover_exchange(layer, grads):
    n = len(grads)
    outs = [jax.ShapeDtypeStruct(g.shape, g.dtype) for g in grads]

    def start(g_refs, o_refs, ss, rs):
        x, y, c = _place()

        @pl.when(c != layer)
        def _():
            for i in range(n):
                _remote(ss, rs, i, g_refs[i], o_refs[i], (x, y, layer)).start()

    def end(g_refs, o_refs, ss, rs):
        x, y, c = _place()
        for i in range(n):
            @pl.when(c != layer)
            def _():
                _remote(ss, rs, i, g_refs[i], o_refs[i], (x, y, c)).wait_send()

            @pl.when(c == layer)
            def _():
                _remote(ss, rs, i, g_refs[i], o_refs[i], (x, y, c)).wait_recv()

    return Exchange(grads, outs, n, start, end)


def _scatter_exchange(layer, parts):
    n = len(parts)
    outs = [jax.ShapeDtypeStruct(p.shape, p.dtype) for p in parts]

    def start(s_refs, o_refs, ss, rs):
        x, y, c = _place()

        @pl.when(c == layer)
        def _():
            for i in range(n):
                for kk, (cx, cy) in enumerate(_other_chips(x, y)):
                    _remote(ss, rs, 3 * i + kk, s_refs[i].at[2 * cx + cy], o_refs[i].at[2 * x + y],
                            (cx, cy, layer)).start()

    def end(s_refs, o_refs, ss, rs):
        x, y, c = _place()

        @pl.when(c == layer)
        def _():
            for i in range(n):
                for kk, (cx, cy) in enumerate(_other_chips(x, y)):
                    got = o_refs[i].at[2 * cx + cy]
                    _remote(ss, rs, 3 * i + kk, got, got, (x, y, c)).wait_recv()
            for i in range(n):
                for kk, (cx, cy) in enumerate(_other_chips(x, y)):
                    _remote(ss, rs, 3 * i + kk, s_refs[i].at[2 * cx + cy], o_refs[i].at[2 * x + y],
                            (x, y, c)).wait_send()

    return Exchange(parts, outs, 3 * n, start, end)


def _return_exchange(reduced):
    flat = [g for layer in range(DEPTH) for g in reduced[layer]]
    n = len(reduced[0])
    outs = [jax.ShapeDtypeStruct(g.shape, g.dtype) for g in flat]

    def start(g_refs, o_refs, ss, rs):
        x, y, c = _place()
        for layer in range(DEPTH):
            @pl.when(c == layer)
            def _():
                for i in range(n):
                    k = layer * n + i
                    _remote(ss, rs, k, g_refs[k], o_refs[k], (x, y, 1 - layer)).start()

    def end(g_refs, o_refs, ss, rs):
        x, y, c = _place()
        for layer in range(DEPTH):
            for i in range(n):
                k = layer * n + i

                @pl.when(c == layer)
                def _():
                    _remote(ss, rs, k, g_refs[k], o_refs[k], (x, y, c)).wait_send()

                @pl.when(c != layer)
                def _():
                    _remote(ss, rs, k, g_refs[k], o_refs[k], (x, y, c)).wait_recv()

    return Exchange(flat, outs, DEPTH * n, start, end)


def _gather_small(tag, buf):
    shape = buf.shape

    def body(b_ref, o_ref, sum_ref, send_sems, recv_sems, local_sem):
        x, y, c = _place()
        me = 4 * x + 2 * y + c
        mine = pltpu.make_async_copy(b_ref, o_ref.at[me], local_sem)
        mine.start()
        flips = [(dx, dy, dc) for dx in (0, 1) for dy in (0, 1) for dc in (0, 1) if (dx, dy, dc) != (0, 0, 0)]
        sends = []

        def peer(dx, dy, dc):
            return (1 - x if dx else x, 1 - y if dy else y, 1 - c if dc else c)

        for kk, flip in enumerate(flips):
            cp = pltpu.make_async_remote_copy(src_ref=b_ref, dst_ref=o_ref.at[me], send_sem=send_sems.at[kk],
                                              recv_sem=recv_sems.at[kk], device_id=peer(*flip), device_id_type=MESH)
            cp.start()
            sends.append(cp)
        for kk, flip in enumerate(flips):
            px, py, pc = peer(*flip)
            frm = 4 * px + 2 * py + pc
            pltpu.make_async_remote_copy(src_ref=b_ref, dst_ref=o_ref.at[frm], send_sem=send_sems.at[kk],
                                         recv_sem=recv_sems.at[kk], device_id=(x, y, c),
                                         device_id_type=MESH).wait_recv()
        for cp in sends:
            cp.wait_send()
        mine.wait()
        total = o_ref[0]
        for d in range(1, N_DEV):
            total = total + o_ref[d]
        sum_ref[...] = total

    return pl.pallas_call(
        body, name="gather_small" + tag, in_specs=[VMEM_SPEC], out_specs=[VMEM_SPEC, VMEM_SPEC],
        out_shape=[jax.ShapeDtypeStruct((N_DEV,) + shape, buf.dtype), jax.ShapeDtypeStruct(shape, buf.dtype)],
        scratch_shapes=[pltpu.SemaphoreType.DMA((N_DEV - 1,)), pltpu.SemaphoreType.DMA((N_DEV - 1,)),
                        pltpu.SemaphoreType.DMA],
    )(buf)


def _add_handed(tag, layer, g, r, tr=256):
    _, rows, cols = g.shape
    tr = min(tr, rows)

    def body(g_ref, r_ref, o_ref):
        @pl.when(lax.axis_index("c") == layer)
        def _():
            o_ref[...] = (g_ref[...] + r_ref[...]).astype(o_ref.dtype)

    spec = pl.BlockSpec((1, tr, cols), lambda p, i: (p, i, 0))
    return pl.pallas_call(
        body, name="add_handed_" + tag, grid=(N_CHIPS, rows // tr), in_specs=[spec, spec], out_specs=spec,
        out_shape=jax.ShapeDtypeStruct(g.shape, BF16), compiler_params=_params(("arbitrary", "arbitrary")),
    )(g, r)


def _sum_parts(tag, layer, own, parts, tr=256):
    _, rows, cols = parts.shape
    tr = min(tr, rows)
    chip = (2 * lax.axis_index("x") + lax.axis_index("y")).astype(jnp.int32).reshape(1)

    def body(c_ref, own_ref, p1_ref, p2_ref, p3_ref, o_ref):
        @pl.when(lax.axis_index("c") == layer)
        def _():
            total = own_ref[0].astype(F32)
            for p_ref in (p1_ref, p2_ref, p3_ref):
                total = total + p_ref[0].astype(F32)
            o_ref[...] = total

    def after(kk):
        return pl.BlockSpec((1, tr, cols), lambda i, c_ref: ((c_ref[0] + kk) % N_CHIPS, i, 0))

    return pl.pallas_call(
        body, name="sum_parts_" + tag,
        grid_spec=pltpu.PrefetchScalarGridSpec(
            num_scalar_prefetch=1, grid=(rows // tr,), in_specs=[after(0), after(1), after(2), after(3)],
            out_specs=pl.BlockSpec((tr, cols), lambda i, c_ref: (i, 0))),
        out_shape=jax.ShapeDtypeStruct((rows, cols), F32),
        compiler_params=_params(("arbitrary",)),
    )(chip, own, parts, parts, parts)


def _adamw_math(w, g, m, v):
    m = ADAM_B1 * m + (1.0 - ADAM_B1) * g
    v = ADAM_B2 * v + (1.0 - ADAM_B2) * jnp.square(g)
    m_hat = m / (1.0 - ADAM_B1 ** ADAM_STEP)
    v_hat = v / (1.0 - ADAM_B2 ** ADAM_STEP)
    delta = -ADAM_LR * (m_hat / (jnp.sqrt(v_hat) + ADAM_EPS) + ADAM_WD * w)
    return delta, m, v


def _adamw(tag, w, g, m, v, tr=256):
    rows, cols = w.shape
    tr = min(tr, rows)

    def body(w_ref, g_ref, m_ref, v_ref, d_ref, nm_ref, nv_ref):
        d_ref[...], nm_ref[...], nv_ref[...] = _adamw_math(w_ref[...], g_ref[...], m_ref[...], v_ref[...])

    spec = pl.BlockSpec((tr, cols), lambda i: (i, 0))
    return pl.pallas_call(
        body, name="adamw_" + tag, grid=(rows // tr,), in_specs=[spec] * 4, out_specs=[spec] * 3,
        out_shape=[jax.ShapeDtypeStruct(w.shape, F32)] * 3,
        compiler_params=_params(("parallel",)),
    )(w, g, m, v)


def _w_in_to_padded(w):
    d0 = SSD_DIM + CONV_DIM
    return jnp.concatenate([w[:, :d0], w[:, d0 + SSD_HEADS:], w[:, d0:d0 + SSD_HEADS],
                            jnp.zeros((w.shape[0], DT_PAD - SSD_HEADS), w.dtype)], axis=1)


def _w_in_from_padded(w):
    d0 = SSD_DIM + CONV_DIM
    return jnp.concatenate([w[:, :d0], w[:, DT_OFF:DT_OFF + SSD_HEADS], w[:, d0:DT_OFF]], axis=1)


def _small_layout():
    return (("norm_mix_g", 0, 0, D_MODEL), ("norm_xa_g", 1, 0, D_MODEL), ("norm_mem_g", 2, 0, D_MODEL),
            ("norm_ff_g", 3, 0, D_MODEL), ("conv_b", 4, 0, CONV_DIM), ("ssd_norm_g", 5, 0, SSD_DIM),
            ("sb_norm_g", 5, SSD_DIM, SB_DIM), ("dt_bias", 6, 0, SSD_HEADS), ("a_log", 6, LANES, SSD_HEADS),
            ("d_skip", 6, 2 * LANES, SSD_HEADS))


def _pack_small(gs, loss):
    lay = _small_layout()
    args = [gs[name][l] for l in range(DEPTH) for name, _, _, _ in lay]
    args += [gs["conv_w"][l] for l in range(DEPTH)] + [gs["final_g"], loss]
    n_lay = len(lay)

    def body(*refs):
        o_ref = refs[-1]
        o_ref[...] = jnp.zeros_like(o_ref)
        for l in range(DEPTH):
            for i, (_, rr, c0, width) in enumerate(lay):
                row = l * _SM_PER_LAYER + rr
                o_ref[row:row + 1, c0:c0 + width] = refs[l * n_lay + i][...]
            row = _SM_CONVW + l * CONV_K
            o_ref[row:row + CONV_K, 0:CONV_DIM] = refs[DEPTH * n_lay + l][...]
        o_ref[_SM_FINAL:_SM_FINAL + 1, :] = refs[DEPTH * n_lay + DEPTH][...]
        o_ref[_SM_LOSS:_SM_LOSS + 1, 0:LANES] = refs[DEPTH * n_lay + DEPTH + 1][...]

    return pl.pallas_call(
        body, name="pack_small", in_specs=[VMEM_SPEC] * len(args), out_specs=VMEM_SPEC,
        out_shape=jax.ShapeDtypeStruct((SMALL_ROWS, PACK_COLS), F32),
    )(*args)


def _small_update(buf, w, mom, var):
    lay = _small_layout()
    names = [name for name, _, _, _ in lay] + ["final_g", "conv_w"]
    conv_cols = CONV_DIM // N_CHIPS
    shapes2d = {name: (DEPTH, width) for name, _, _, width in lay}
    shapes2d["final_g"] = (1, D_MODEL)
    shapes2d["conv_w"] = (DEPTH * CONV_K, conv_cols)
    args = [buf]
    for src in (w, mom, var):
        args += [src[name].reshape(shapes2d[name]) for name in names]
    n = len(names)

    def body(*refs):
        b_ref = refs[0]
        w_refs, m_refs, v_refs = refs[1:1 + n], refs[1 + n:1 + 2 * n], refs[1 + 2 * n:1 + 3 * n]
        outs = refs[1 + 3 * n:]
        chip = 2 * lax.axis_index("x") + lax.axis_index("y")
        for i, name in enumerate(names):
            if name == "final_g":
                g = b_ref[_SM_FINAL:_SM_FINAL + 1, :]
            elif name == "conv_w":
                rows = b_ref[_SM_CONVW:_SM_CONVW + DEPTH * CONV_K, 0:CONV_DIM]
                g = jnp.zeros((DEPTH * CONV_K, conv_cols), F32)
                for j in range(N_CHIPS):
                    g = g + jnp.where(chip == j, rows[:, j * conv_cols:(j + 1) * conv_cols], 0.0)
            else:
                _, rr, c0, width = lay[i]
                g = jnp.concatenate([b_ref[l * _SM_PER_LAYER + rr:l * _SM_PER_LAYER + rr + 1, c0:c0 + width]
                                     for l in range(DEPTH)], axis=0)
            d, m2, v2 = _adamw_math(w_refs[i][...], g, m_refs[i][...], v_refs[i][...])
            outs[i][...] = g
            outs[n + i][...] = d
            outs[2 * n + i][...] = m2
            outs[3 * n + i][...] = v2

    out_shape = [jax.ShapeDtypeStruct(shapes2d[name], F32) for _ in range(4) for name in names]
    res = pl.pallas_call(
        body, name="small_update", in_specs=[VMEM_SPEC] * len(args), out_specs=[VMEM_SPEC] * (4 * n),
        out_shape=out_shape,
    )(*args)
    return tuple({name: res[k * n + i].reshape(w[name].shape) for i, name in enumerate(names)} for k in range(4))


CONV_BLOCK_ROWS = 8


def _conv_w_block(cw):
    flat = cw.reshape(-1)
    pad = jnp.zeros((CONV_BLOCK_ROWS * PACK_COLS - flat.shape[0],), F32)
    return jnp.concatenate([flat, pad]).reshape(CONV_BLOCK_ROWS, PACK_COLS)


def _conv_w_from_slots(slots):
    cols = CONV_DIM // N_CHIPS
    n = DEPTH * CONV_K * cols
    shards = [slots[2 * j].reshape(-1)[:n].reshape(DEPTH, CONV_K, cols) for j in range(N_CHIPS)]
    return jnp.concatenate(shards, axis=2)


SMALL_NAMES = ("norm_mix_g", "conv_b", "dt_bias", "a_log", "d_skip", "ssd_norm_g", "sb_norm_g", "norm_xa_g",
               "norm_mem_g", "norm_ff_g", "final_g")
WEIGHT_ORDER = ("norm_mix_g", "w_in", "conv_w", "conv_b", "dt_bias", "a_log", "d_skip", "ssd_norm_g", "sb_norm_g",
                "w_out", "norm_xa_g", "norm_mem_g", "w_xq", "w_xk", "w_xv", "w_xo", "norm_ff_g", "w_ff1", "w_ff2",
                "final_g")


_ALL = tuple(range(len(MATS)))
_IN = tuple(i for i in _ALL if MATS[i][0] == "w_in")
_MLP = tuple(i for i in _ALL if MATS[i][0] in ("w_ff1", "w_ff2"))
_MIXER = tuple(i for i in _ALL if i not in _IN + _MLP)


class PipelinedPlan(LocalPlan):
    def __init__(self, shards):
        self.shards = shards
        self.chip = 2 * lax.axis_index("x") + lax.axis_index("y")
        self.mats = [dict() for _ in range(DEPTH)]
        n = len(MATS)
        self.parts = [[None] * n for _ in range(DEPTH)]
        self.to_chips = [[None] * n for _ in range(DEPTH)]
        self.reduced = [[None] * n for _ in range(DEPTH)]
        self.riders = {}
        self._gathered(0, _IN, _run_exchange("gather_first", self._gather(0, _IN)))
        self._gather_behind(0, _MIXER, "ssd_fwd", 0)
        self._gather_behind(0, _MLP, "sb_fwd", 0)
        for l in range(1, DEPTH):
            self._gather_behind(l, _IN + _MIXER, "sb_fwd", l - 1)
            self._gather_behind(l, _MLP, "mlp_fwd", l - 1)

    def _ride(self, kernel, l, exchange, then):
        self.riders.setdefault((kernel, l), []).append((exchange, then))

    def carried_by(self, kernel, l):
        exchange = None
        for ex, _ in self.riders.get((kernel, l), []):
            exchange = _both(exchange, ex)
        return exchange

    def carried_out(self, kernel, l, outs):
        for ex, then in self.riders.pop((kernel, l), []):
            then(outs[:len(ex.outputs)])
            outs = outs[len(ex.outputs):]

    def _gather(self, l, which):
        return _gather_exchange(l, [self.shards[i] for i in which])

    def _gather_behind(self, l, which, kernel, host):
        self._ride(kernel, host, self._gather(l, which), lambda outs: self._gathered(l, which, outs))

    def _gathered(self, l, which, outs):
        for i, theirs in zip(which, outs):
            name, _, axis = MATS[i]
            full = jnp.concatenate([jnp.where(self.chip == j, self.shards[i][l], theirs[j]) for j in range(N_CHIPS)],
                                   axis=axis)
            self.mats[l][name] = _w_in_to_padded(full) if name == "w_in" else full

    def _set_parts(self, l, which, gm):
        for i in which:
            name, _, axis = MATS[i]
            g = _w_in_from_padded(gm[name]) if name == "w_in" else gm[name]
            if g.ndim == 2 and axis == 0:
                g = g.reshape((N_CHIPS, g.shape[0] // N_CHIPS, g.shape[1]))
            elif g.ndim == 2:
                g = jnp.swapaxes(g.reshape((g.shape[0], N_CHIPS, g.shape[1] // N_CHIPS)), 0, 1)
            self.parts[l][i] = g

    def _handover(self, l, which):
        return _handover_exchange(l, [self.parts[l][i] for i in which])

    def _handed(self, l, which, outs):
        for i, r in zip(which, outs):
            self.to_chips[l][i] = _add_handed(MATS[i][0] + str(l), l, self.parts[l][i], r)

    def _scatter(self, l, which):
        return _scatter_exchange(l, [self.to_chips[l][i] for i in which])

    def _scattered(self, l, which, outs):
        for i, got in zip(which, outs):
            self.reduced[l][i] = _sum_parts(MATS[i][0] + str(l), l, self.to_chips[l][i], got)

    def _send_behind(self, l, which, hand_kernel, cross_kernel, host):
        def handed(outs):
            self._handed(l, which, outs)
            self._ride(cross_kernel, host, self._scatter(l, which), lambda o: self._scattered(l, which, o))

        self._ride(hand_kernel, host, self._handover(l, which), handed)

    def mlp_grads_done(self, l, gm):
        self._set_parts(l, _MLP, gm)
        self._send_behind(l, _MLP, "xattn_bwd", "sb_bwd", l)

    def mixer_grads_done(self, l, gm):
        self._set_parts(l, _MIXER, gm)
        self._send_behind(l, _MIXER, "sb_bwd", "ssd_bwd", l)

    def grads_done(self, l, gm):
        self._set_parts(l, _IN, gm)
        if l > 0:
            self._send_behind(l, _IN, "mlp_bwd", "sb_bwd", l - 1)
        else:
            self._handed(0, _IN, _run_exchange("handover_last", self._handover(0, _IN)))
            self._scattered(0, _IN, _run_exchange("scatter_last", self._scatter(0, _IN)))

    def reduced_gradients(self):
        returned = _run_exchange("return_reduced", _return_exchange(self.reduced))
        n = len(MATS)
        core = lax.axis_index("c")
        return {name: jnp.stack([jnp.where(core == l, self.reduced[l][i], returned[l * n + i]) for l in range(DEPTH)])
                for i, (name, _, _) in enumerate(MATS)}


def kernel(x, mem, norm_mix_g, w_in, conv_w, conv_b, dt_bias, a_log, d_skip, ssd_norm_g, sb_norm_g, w_out, norm_xa_g, norm_mem_g, w_xq, w_xk, w_xv, w_xo, norm_ff_g, w_ff1, w_ff2, final_g, loss_target, m_norm_mix_g, m_w_in, m_conv_w, m_conv_b, m_dt_bias, m_a_log, m_d_skip, m_ssd_norm_g, m_sb_norm_g, m_w_out, m_norm_xa_g, m_norm_mem_g, m_w_xq, m_w_xk, m_w_xv, m_w_xo, m_norm_ff_g, m_w_ff1, m_w_ff2, m_final_g, v_norm_mix_g, v_w_in, v_conv_w, v_conv_b, v_dt_bias, v_a_log, v_d_skip, v_ssd_norm_g, v_sb_norm_g, v_w_out, v_norm_xa_g, v_norm_mem_g, v_w_xq, v_w_xk, v_w_xv, v_w_xo, v_norm_ff_g, v_w_ff1, v_w_ff2, v_final_g):
    w = dict(norm_mix_g=norm_mix_g, w_in=w_in, conv_w=conv_w, conv_b=conv_b, dt_bias=dt_bias, a_log=a_log,
             d_skip=d_skip, ssd_norm_g=ssd_norm_g, sb_norm_g=sb_norm_g, w_out=w_out, norm_xa_g=norm_xa_g,
             norm_mem_g=norm_mem_g, w_xq=w_xq, w_xk=w_xk, w_xv=w_xv, w_xo=w_xo, norm_ff_g=norm_ff_g, w_ff1=w_ff1,
             w_ff2=w_ff2, final_g=final_g)
    mom = dict(norm_mix_g=m_norm_mix_g, w_in=m_w_in, conv_w=m_conv_w, conv_b=m_conv_b, dt_bias=m_dt_bias,
               a_log=m_a_log, d_skip=m_d_skip, ssd_norm_g=m_ssd_norm_g, sb_norm_g=m_sb_norm_g, w_out=m_w_out,
               norm_xa_g=m_norm_xa_g, norm_mem_g=m_norm_mem_g, w_xq=m_w_xq, w_xk=m_w_xk, w_xv=m_w_xv, w_xo=m_w_xo,
               norm_ff_g=m_norm_ff_g, w_ff1=m_w_ff1, w_ff2=m_w_ff2, final_g=m_final_g)
    var = dict(norm_mix_g=v_norm_mix_g, w_in=v_w_in, conv_w=v_conv_w, conv_b=v_conv_b, dt_bias=v_dt_bias,
               a_log=v_a_log, d_skip=v_d_skip, ssd_norm_g=v_ssd_norm_g, sb_norm_g=v_sb_norm_g, w_out=v_w_out,
               norm_xa_g=v_norm_xa_g, norm_mem_g=v_norm_mem_g, w_xq=v_w_xq, w_xk=v_w_xk, w_xv=v_w_xv, w_xo=v_w_xo,
               norm_ff_g=v_norm_ff_g, w_ff1=v_w_ff1, w_ff2=v_w_ff2, final_g=v_final_g)
    conv_slots, _ = _gather_small("_conv", _conv_w_block(conv_w))
    sw = {name: w[name] for name in SMALL_NAMES}
    sw["conv_w"] = _conv_w_from_slots(conv_slots)
    plan = PipelinedPlan([w[name].astype(BF16) for name, _, _ in MATS])
    loss, grad_x, gm, gs = _local_step(x[0], mem[0], loss_target[0], sw, plan)
    g_mats = plan.reduced_gradients()

    _, small_sum = _gather_small("_grads", _pack_small(gs, loss))
    loss_out = small_sum[_SM_LOSS, 0]

    grads, deltas, new_m, new_v = {}, {}, {}, {}
    for name, _, _ in MATS:
        g = g_mats[name]
        cols = g.shape[-1]
        d, nm, nv = _adamw(name, w[name].reshape(-1, cols), g.reshape(-1, cols), mom[name].reshape(-1, cols),
                           var[name].reshape(-1, cols))
        grads[name] = g
        deltas[name], new_m[name], new_v[name] = (t.reshape(g.shape) for t in (d, nm, nv))
    g_s, d_s, m_s, v_s = _small_update(small_sum, w, mom, var)
    for name in g_s:
        grads[name], deltas[name], new_m[name], new_v[name] = g_s[name], d_s[name], m_s[name], v_s[name]

    return (loss_out, grad_x[None], *[grads[n] for n in WEIGHT_ORDER], *[deltas[n] for n in WEIGHT_ORDER],
            *[new_m[n] for n in WEIGHT_ORDER], *[new_v[n] for n in WEIGHT_ORDER])
```

```python
import functools
import math

import jax
import jax.numpy as jnp
from jax import lax
from jax.experimental import pallas as pl
from jax.experimental.pallas import tpu as pltpu

F32 = jnp.float32
BF16 = jnp.bfloat16
MESH = pl.DeviceIdType.MESH

D_MODEL = 1024
DEPTH = 2
SSD_DIM = 512
SSD_HEAD_DIM = 64
SSD_HEADS = 8
SSD_GROUPS = 2
SSD_STATE = 64
CONV_K = 4
CHUNK = 128
SB_DIM = 512
SB_HEAD_DIM = 64
XA_HEADS = 4
XA_HEAD_DIM = 128
XA_DIM = 512
D_FF = 4096
EPS = 1e-5
GN = SSD_GROUPS * SSD_STATE
CONV_DIM = SSD_DIM + 2 * GN
IN_DIM = SSD_DIM + CONV_DIM + SSD_HEADS + 3 * SB_DIM
LANES = 128
DT_PAD = LANES
IN_PAD = SSD_DIM + CONV_DIM + 3 * SB_DIM + DT_PAD
Q_OFF = SSD_DIM + CONV_DIM
DT_OFF = Q_OFF + 3 * SB_DIM
HALO = 8

ADAM_LR = 0.001
ADAM_B1 = 0.9
ADAM_B2 = 0.999
ADAM_EPS = 1e-08
ADAM_WD = 0.01
ADAM_STEP = 10

N_CHIPS = 4
N_DEV = 8
PACK_COLS = 1024
VMEM_LIMIT = 56 * 1024 * 1024

MATS = (
    ("w_in", (D_MODEL, IN_DIM), 1),
    ("w_out", (D_MODEL, D_MODEL), 0),
    ("w_xq", (D_MODEL, XA_DIM), 0),
    ("w_xk", (D_MODEL, XA_DIM), 0),
    ("w_xv", (D_MODEL, XA_DIM), 0),
    ("w_xo", (XA_DIM, D_MODEL), 1),
    ("w_ff1", (D_MODEL, D_FF), 1),
    ("w_ff2", (D_FF, D_MODEL), 0),
)


SMALL_ROWS = 24
_SM_PER_LAYER = 7
_SM_FINAL = 14
_SM_CONVW = 15
_SM_LOSS = 23


_NN = ((1,), (0,))
_NT = ((1,), (1,))
_TN = ((0,), (0,))


def _dg(a, b, dims):
    return lax.dot_general(a.astype(BF16), b.astype(BF16), (dims, ((), ())), preferred_element_type=F32)


@jax.custom_vjp
def mm_nn(a, b):
    return _dg(a, b, _NN)


@jax.custom_vjp
def mm_nt(a, b):
    return _dg(a, b, _NT)


@jax.custom_vjp
def mm_tn(a, b):
    return _dg(a, b, _TN)


def _nn_fwd(a, b):
    return _dg(a, b, _NN), (a, b)


def _nn_bwd(res, g):
    a, b = res
    return mm_nt(g, b).astype(a.dtype), mm_tn(a, g).astype(b.dtype)


def _nt_fwd(a, b):
    return _dg(a, b, _NT), (a, b)


def _nt_bwd(res, g):
    a, b = res
    return mm_nn(g, b).astype(a.dtype), mm_tn(g, a).astype(b.dtype)


def _tn_fwd(a, b):
    return _dg(a, b, _TN), (a, b)


def _tn_bwd(res, g):
    a, b = res
    return mm_nt(b, g).astype(a.dtype), mm_nn(a, g).astype(b.dtype)


mm_nn.defvjp(_nn_fwd, _nn_bwd)
mm_nt.defvjp(_nt_fwd, _nt_bwd)
mm_tn.defvjp(_tn_fwd, _tn_bwd)


def _rms(x, g):
    return x * lax.rsqrt(jnp.mean(x * x, axis=-1, keepdims=True) + EPS) * g


def _params(sem=None, vmem=VMEM_LIMIT):
    return pltpu.CompilerParams(dimension_semantics=sem, vmem_limit_bytes=vmem)


class Exchange:
    def __init__(self, inputs, outputs, n_sems, start, end, mid=None):
        self.inputs, self.outputs, self.n_sems = list(inputs), list(outputs), n_sems
        self.start, self.mid, self.end = start, mid, end

    def specs(self):
        hbm = pl.BlockSpec(memory_space=pl.ANY)
        sems = [pltpu.SemaphoreType.DMA((self.n_sems,)), pltpu.SemaphoreType.DMA((self.n_sems,))]
        return [hbm] * len(self.inputs), [hbm] * len(self.outputs), sems

    def pick(self, refs, n_before_in, n_before_out):
        n_in, n_out = len(self.inputs), len(self.outputs)
        o0 = n_before_in + n_in + n_before_out
        return refs[n_before_in:n_before_in + n_in], refs[o0:o0 + n_out], refs[-2], refs[-1]

    def before_work(self, mine, first, mid=None):
        @pl.when(first)
        def _():
            self.start(*mine)

        if self.mid is not None and mid is not None:
            @pl.when(mid)
            def _():
                self.mid(*mine)

    def after_work(self, mine, last, mid_done):
        @pl.when(last)
        def _():
            if self.mid is not None and not mid_done:
                self.mid(*mine)
            self.end(*mine)


class _Shifted:
    def __init__(self, ref, base):
        self.ref, self.base = ref, base

    @property
    def at(self):
        return self

    def __getitem__(self, idx):
        return self.ref.at[self.base + idx]


def _both(a, b):
    if a is None or b is None:
        return a or b
    n_i, n_o, n_s = len(a.inputs), len(a.outputs), a.n_sems

    def joined(fa, fb):
        def f(i_refs, o_refs, ss, rs):
            if fa is not None:
                fa(i_refs[:n_i], o_refs[:n_o], ss, rs)
            if fb is not None:
                fb(i_refs[n_i:], o_refs[n_o:], _Shifted(ss, n_s), _Shifted(rs, n_s))
        return f

    mid = joined(a.mid, b.mid) if (a.mid is not None or b.mid is not None) else None
    return Exchange(a.inputs + b.inputs, a.outputs + b.outputs, n_s + b.n_sems, joined(a.start, b.start),
                    joined(a.end, b.end), mid)


def _rowcall(name, fn, rows, fulls, row_out, acc_out=(), tm=256, exchange=None):
    s = rows[0].shape[0]
    tm = min(tm, s)
    nt = s // tm
    n_r, n_f, n_ro, n_ao = len(rows), len(fulls), len(row_out), len(acc_out)
    n_xi = len(exchange.inputs) if exchange else 0

    def body(*refs):
        if exchange:
            mine = exchange.pick(refs, n_r + n_f, n_ro + n_ao)
            exchange.before_work(mine, pl.program_id(0) == 0)
        ins = [r[...] for r in refs[: n_r + n_f]]
        outs = fn(*ins)
        o_refs = refs[n_r + n_f + n_xi:]
        for o_ref, val in zip(o_refs[:n_ro], outs[:n_ro]):
            o_ref[...] = val.astype(o_ref.dtype)
        if n_ao:
            first = pl.program_id(0) == 0

            @pl.when(first)
            def _():
                for o_ref, val in zip(o_refs[n_ro:], outs[n_ro:]):
                    o_ref[...] = val.astype(o_ref.dtype)

            @pl.when(jnp.logical_not(first))
            def _():
                for o_ref, val in zip(o_refs[n_ro:], outs[n_ro:]):
                    o_ref[...] += val.astype(o_ref.dtype)
        if exchange:
            exchange.after_work(mine, pl.program_id(0) == nt - 1, mid_done=False)

    in_specs = [pl.BlockSpec((tm, a.shape[1]), lambda i: (i, 0)) for a in rows]
    in_specs += [pl.BlockSpec(a.shape, lambda i: (0, 0), pipeline_mode=pl.Buffered(1)) for a in fulls]
    out_specs = [pl.BlockSpec((tm, c), lambda i: (i, 0)) for c, _ in row_out]
    out_specs += [pl.BlockSpec(shape, lambda i: (0, 0)) for shape, _ in acc_out]
    out_shape = [jax.ShapeDtypeStruct((s, c), dt) for c, dt in row_out]
    out_shape += [jax.ShapeDtypeStruct(shape, dt) for shape, dt in acc_out]
    args, scratch = [*rows, *fulls], []
    if exchange:
        x_in, x_out, scratch = exchange.specs()
        in_specs += x_in
        out_specs += x_out
        out_shape += exchange.outputs
        args += exchange.inputs
    return pl.pallas_call(
        body, name=name, grid=(nt,), in_specs=in_specs, out_specs=out_specs, out_shape=out_shape,
        scratch_shapes=scratch, compiler_params=_params(("arbitrary",)),
    )(*args)


def _mm_tn_call(name, a, b, tm, tn, tk):
    s, m = a.shape
    n = b.shape[1]
    tk = min(tk, s)

    def body(a_ref, b_ref, o_ref):
        d = _dg(a_ref[...], b_ref[...], _TN)
        first = pl.program_id(2) == 0

        @pl.when(first)
        def _():
            o_ref[...] = d

        @pl.when(jnp.logical_not(first))
        def _():
            o_ref[...] += d

    return pl.pallas_call(
        body, name=name, grid=(m // tm, n // tn, s // tk),
        in_specs=[pl.BlockSpec((tk, tm), lambda i, j, k: (k, i)), pl.BlockSpec((tk, tn), lambda i, j, k: (k, j))],
        out_specs=pl.BlockSpec((tm, tn), lambda i, j, k: (i, j)),
        out_shape=jax.ShapeDtypeStruct((m, n), F32),
        compiler_params=_params(("parallel", "parallel", "arbitrary")),
    )(a, b)


def _mm_tn_parts(name, a, b, by_cols, tm=512, tk=2048):
    s, m = a.shape
    n = b.shape[1]
    r, c = (m, n // N_CHIPS) if by_cols else (m // N_CHIPS, n)
    per = r // tm
    tk = min(tk, s)

    def body(a_ref, b_ref, o_ref):
        d = _dg(a_ref[...], b_ref[...], _TN)
        first = pl.program_id(2) == 0

        @pl.when(first)
        def _():
            o_ref[0] = d

        @pl.when(jnp.logical_not(first))
        def _():
            o_ref[0] += d

    if by_cols:
        out_map = lambda i, j, k: (j, i, 0)
    else:
        out_map = lambda i, j, k: (i // per, i % per, 0)
    return pl.pallas_call(
        body, name=name, grid=(m // tm, n // c, s // tk),
        in_specs=[pl.BlockSpec((tk, tm), lambda i, j, k: (k, i)), pl.BlockSpec((tk, c), lambda i, j, k: (k, j))],
        out_specs=pl.BlockSpec((1, tm, c), out_map),
        out_shape=jax.ShapeDtypeStruct((N_CHIPS, r, c), F32),
        compiler_params=_params(("parallel", "parallel", "arbitrary")),
    )(a, b)


def _proj_tile(h, g, w):
    p = mm_nn(_rms(h, g), w)
    return (p[:, :SSD_DIM], p[:, SSD_DIM:Q_OFF], p[:, Q_OFF:Q_OFF + SB_DIM],
            p[:, Q_OFF + SB_DIM:Q_OFF + 2 * SB_DIM], p[:, Q_OFF + 2 * SB_DIM:DT_OFF], p[:, DT_OFF:])


def _proj_fwd(tag, h, g, w, exchange=None):
    return _rowcall(
        "proj_fwd" + tag, _proj_tile, [h], [g, w],
        [(SSD_DIM, F32), (CONV_DIM, F32), (SB_DIM, BF16), (SB_DIM, BF16), (SB_DIM, BF16), (DT_PAD, F32)], tm=512,
        exchange=exchange)


def _proj_bwd(tag, h, dh_out, dz, dxbc, dq, dk, dv, ddt, g, w):
    def fn(h, dh_out, dz, dxbc, dq, dk, dv, ddt, g, w):
        dp = jnp.concatenate([dz.astype(BF16), dxbc.astype(BF16), dq.astype(BF16), dk.astype(BF16),
                              dv.astype(BF16), ddt.astype(BF16)], axis=1)
        hn, vjp = jax.vjp(_rms, h, g)
        dh, dg = vjp(mm_nt(dp, w))
        return dh_out + dh, hn, dp, dg

    return _rowcall(
        "proj_bwd" + tag, fn, [h, dh_out, dz, dxbc, dq, dk, dv, ddt], [g, w],
        [(D_MODEL, F32), (D_MODEL, BF16), (IN_PAD, BF16)], [((1, D_MODEL), F32)], tm=256)


def _shift_down(x, tail, j):
    if j == 0:
        return x
    r = pltpu.roll(x, j, 0)
    rt = pltpu.roll(tail, j, 0)
    row = lax.broadcasted_iota(jnp.int32, (HALO, x.shape[1]), 0)
    first = jnp.where(row < j, rt, r[:HALO])
    if x.shape[0] == HALO:
        return first
    return jnp.concatenate([first, r[HALO:]], axis=0)


def _shift_up(x, head, j):
    if j == 0:
        return x
    n = x.shape[0]
    r = pltpu.roll(x, n - j, 0)
    rh = pltpu.roll(head, HALO - j, 0)
    row = lax.broadcasted_iota(jnp.int32, (HALO, x.shape[1]), 0)
    return jnp.concatenate([r[:n - HALO], jnp.where(row >= HALO - j, rh, r[n - HALO:])], axis=0)


def _conv_pre(x, tail, w, b):
    acc = b + w[CONV_K - 1:CONV_K] * x
    for j in range(1, CONV_K):
        acc = acc + w[CONV_K - 1 - j:CONV_K - j] * _shift_down(x, tail, j)
    return acc


def _dsilu(p):
    s = jax.nn.sigmoid(p)
    return s * (1.0 + p * (1.0 - s))


def _conv_fwd(tag, xbc, w, b, tc=512):
    s, c = xbc.shape
    tc = min(tc, s)
    per = tc // HALO

    def body(x_ref, prev_ref, w_ref, b_ref, o_ref):
        tail = jnp.where(pl.program_id(0) > 0, prev_ref[...], 0.0)
        o_ref[...] = jax.nn.silu(_conv_pre(x_ref[...], tail, w_ref[...], b_ref[...]))

    return pl.pallas_call(
        body, name="conv_fwd" + tag, grid=(s // tc,),
        in_specs=[pl.BlockSpec((tc, c), lambda i: (i, 0)),
                  pl.BlockSpec((HALO, c), lambda i: (jnp.maximum(i * per - 1, 0), 0)),
                  pl.BlockSpec((CONV_K, c), lambda i: (0, 0)), pl.BlockSpec((1, c), lambda i: (0, 0))],
        out_specs=pl.BlockSpec((tc, c), lambda i: (i, 0)),
        out_shape=jax.ShapeDtypeStruct((s, c), F32),
        compiler_params=_params(("arbitrary",)),
    )(xbc, xbc, w, b)


def _conv_bwd(tag, xbc, dact, w, b, tc=512):
    s, c = xbc.shape
    tc = min(tc, s)
    per = tc // HALO
    nt = s // tc
    last_blk = s // HALO - 1

    def body(x_ref, prev_ref, next_ref, d_ref, dnext_ref, w_ref, b_ref, dx_ref, dw_ref, db_ref):
        i = pl.program_id(0)
        x = x_ref[...]
        wv = w_ref[...]
        tail = jnp.where(i > 0, prev_ref[...], 0.0)
        dpre = d_ref[...] * _dsilu(_conv_pre(x, tail, wv, b_ref[...]))
        pre_n = _conv_pre(next_ref[...], x[tc - HALO:], wv, b_ref[...])
        dpre_n = jnp.where(i < nt - 1, dnext_ref[...] * _dsilu(pre_n), 0.0)
        dx = wv[CONV_K - 1:CONV_K] * dpre
        for j in range(1, CONV_K):
            dx = dx + wv[CONV_K - 1 - j:CONV_K - j] * _shift_up(dpre, dpre_n, j)
        dx_ref[...] = dx
        dws = [jnp.sum(dpre * _shift_down(x, tail, CONV_K - 1 - k), axis=0, keepdims=True) for k in range(CONV_K)]
        dwv = jnp.concatenate(dws, axis=0)
        dbv = jnp.sum(dpre, axis=0, keepdims=True)

        @pl.when(i == 0)
        def _():
            dw_ref[...] = dwv
            db_ref[...] = dbv

        @pl.when(i > 0)
        def _():
            dw_ref[...] += dwv
            db_ref[...] += dbv

    tile = pl.BlockSpec((tc, c), lambda i: (i, 0))
    prev = pl.BlockSpec((HALO, c), lambda i: (jnp.maximum(i * per - 1, 0), 0))
    nxt = pl.BlockSpec((HALO, c), lambda i: (jnp.minimum((i + 1) * per, last_blk), 0))
    return pl.pallas_call(
        body, name="conv_bwd" + tag, grid=(nt,),
        in_specs=[tile, prev, nxt, tile, nxt, pl.BlockSpec((CONV_K, c), lambda i: (0, 0)),
                  pl.BlockSpec((1, c), lambda i: (0, 0))],
        out_specs=[tile, pl.BlockSpec((CONV_K, c), lambda i: (0, 0)), pl.BlockSpec((1, c), lambda i: (0, 0))],
        out_shape=[jax.ShapeDtypeStruct((s, c), F32), jax.ShapeDtypeStruct((CONV_K, c), F32),
                   jax.ShapeDtypeStruct((1, c), F32)],
        compiler_params=_params(("arbitrary",)),
    )(xbc, xbc, xbc, dact, dact, w, b)


def _ssd_chunk(xs, bm, cm, dtr, z, dt_bias, a_log, d_skip, g, s_prev):
    n = CHUNK
    row = lax.broadcasted_iota(jnp.int32, (n, n), 0)
    col = lax.broadcasted_iota(jnp.int32, (n, n), 1)
    causal = row >= col
    dt = jax.nn.softplus(dtr + dt_bias)
    a_c = dt * (-jnp.exp(a_log))
    hi = lax.Precision.HIGHEST
    a_cum = jnp.dot(causal.astype(F32), a_c, precision=hi, preferred_element_type=F32)
    a_cum_t = lax.dot_general(a_c, (row <= col).astype(F32), (_TN, ((), ())), precision=hi,
                              preferred_element_type=F32)
    p, st = SSD_HEAD_DIM, SSD_STATE
    heads = range(SSD_HEADS)
    grp = [h // (SSD_HEADS // SSD_GROUPS) for h in heads]
    bgs = [bm[:, k * st:(k + 1) * st] for k in range(SSD_GROUPS)]
    cgs = [cm[:, k * st:(k + 1) * st] for k in range(SSD_GROUPS)]
    cb = [mm_nt(cgs[k], bgs[k]) for k in range(SSD_GROUPS)]
    acols = [a_cum[:, h:h + 1] for h in heads]
    a_lasts = [a_cum[n - 1:n, h:h + 1] for h in heads]
    xhs = [xs[:, h * p:(h + 1) * p] for h in heads]
    sps = [s_prev[h * p:(h + 1) * p, :] for h in heads]
    xdts = [xhs[h] * dt[:, h:h + 1] for h in heads]
    decays = [jnp.exp(jnp.where(causal, acols[h] - a_cum_t[h:h + 1, :], -jnp.inf)) for h in heads]
    y_offs = [mm_nt(cgs[grp[h]], sps[h]) for h in heads]
    y_diags = [mm_nn(cb[grp[h]] * decays[h], xdts[h]) for h in heads]
    states = [mm_tn(xdts[h] * jnp.exp(a_lasts[h] - acols[h]), bgs[grp[h]]) for h in heads]
    s_new = [sps[h] * jnp.exp(a_lasts[h]) + states[h] for h in heads]
    ys = [y_diags[h] + y_offs[h] * jnp.exp(acols[h]) + d_skip[:, h:h + 1] * xhs[h] for h in heads]
    y = jnp.concatenate(ys, axis=1) * jax.nn.silu(z)
    return _rms(y, g), jnp.concatenate(s_new, axis=0)


def _split_xbc(t):
    return t[:, :SSD_DIM], t[:, SSD_DIM:SSD_DIM + GN], t[:, SSD_DIM + GN:]


def _ssd_fwd(tag, xact, dtr, z, dt_bias, a_log, d_skip, g, exchange=None):
    s = xact.shape[0]
    nc = s // CHUNK
    srows = SSD_HEADS * SSD_HEAD_DIM
    n_xi = len(exchange.inputs) if exchange else 0

    def body(*refs):
        x_ref, dt_ref, z_ref, b_ref, al_ref, ds_ref, g_ref = refs[:7]
        y_ref, st_ref = refs[7 + n_xi:9 + n_xi]
        state = refs[-3] if exchange else refs[-1]
        if exchange:
            mine = exchange.pick(refs, 7, 2)
            exchange.before_work(mine, pl.program_id(0) == 0)

        @pl.when(pl.program_id(0) == 0)
        def _():
            state[...] = jnp.zeros_like(state)

        sp = state[...]
        st_ref[0] = sp
        xs, bm, cm = _split_xbc(x_ref[...])
        y, sn = _ssd_chunk(xs, bm, cm, dt_ref[...][:, :SSD_HEADS], z_ref[...], b_ref[...], al_ref[...],
                           ds_ref[...], g_ref[...], sp)
        y_ref[...] = y
        state[...] = sn
        if exchange:
            exchange.after_work(mine, pl.program_id(0) == nc - 1, mid_done=False)

    small = pl.BlockSpec((1, SSD_HEADS), lambda i: (0, 0))
    in_specs = [pl.BlockSpec((CHUNK, CONV_DIM), lambda i: (i, 0)), pl.BlockSpec((CHUNK, DT_PAD), lambda i: (i, 0)),
                pl.BlockSpec((CHUNK, SSD_DIM), lambda i: (i, 0)), small, small, small,
                pl.BlockSpec((1, SSD_DIM), lambda i: (0, 0))]
    out_specs = [pl.BlockSpec((CHUNK, SSD_DIM), lambda i: (i, 0)),
                 pl.BlockSpec((1, srows, SSD_STATE), lambda i: (i, 0, 0))]
    out_shape = [jax.ShapeDtypeStruct((s, SSD_DIM), F32), jax.ShapeDtypeStruct((nc, srows, SSD_STATE), F32)]
    args, scratch = [xact, dtr, z, dt_bias, a_log, d_skip, g], [pltpu.VMEM((srows, SSD_STATE), F32)]
    if exchange:
        x_in, x_out, sems = exchange.specs()
        in_specs, out_specs, scratch = in_specs + x_in, out_specs + x_out, scratch + sems
        out_shape, args = out_shape + exchange.outputs, args + exchange.inputs
    return pl.pallas_call(
        body, name="ssd_fwd" + tag, grid=(nc,), in_specs=in_specs, out_specs=out_specs, out_shape=out_shape,
        scratch_shapes=scratch, compiler_params=_params(("arbitrary",)),
    )(*args)


def _ssd_bwd(tag, xact, dtr, z, states, dy, dt_bias, a_log, d_skip, g, exchange=None):
    s = xact.shape[0]
    nc = s // CHUNK
    srows = SSD_HEADS * SSD_HEAD_DIM
    n_xi = len(exchange.inputs) if exchange else 0

    def body(*refs):
        x_ref, dt_ref, z_ref, sp_ref, dy_ref, b_ref, al_ref, ds_ref, g_ref = refs[:9]
        dx_ref, ddt_ref, dz_ref, db_ref, dal_ref, dds_ref, dg_ref = refs[9 + n_xi:16 + n_xi]
        dstate = refs[-3] if exchange else refs[-1]
        first = pl.program_id(0) == 0
        if exchange:
            mine = exchange.pick(refs, 9, 7)
            exchange.before_work(mine, first)

        @pl.when(first)
        def _():
            dstate[...] = jnp.zeros_like(dstate)

        xs, bm, cm = _split_xbc(x_ref[...])
        _, vjp = jax.vjp(_ssd_chunk, xs, bm, cm, dt_ref[...][:, :SSD_HEADS], z_ref[...], b_ref[...], al_ref[...],
                         ds_ref[...], g_ref[...], sp_ref[0])
        dxs, dbm, dcm, ddt, dz, db, dal, dds, dg, dsp = vjp((dy_ref[...], dstate[...]))
        dx_ref[...] = jnp.concatenate([dxs, dbm, dcm], axis=1)
        ddt_ref[...] = jnp.concatenate([ddt, jnp.zeros((CHUNK, DT_PAD - SSD_HEADS), F32)], axis=1)
        dz_ref[...] = dz
        dstate[...] = dsp

        @pl.when(first)
        def _():
            db_ref[...] = db
            dal_ref[...] = dal
            dds_ref[...] = dds
            dg_ref[...] = dg

        @pl.when(jnp.logical_not(first))
        def _():
            db_ref[...] += db
            dal_ref[...] += dal
            dds_ref[...] += dds
            dg_ref[...] += dg

        if exchange:
            exchange.after_work(mine, pl.program_id(0) == nc - 1, mid_done=False)

    def rev(c):
        return lambda i: (nc - 1 - i, 0)

    small = pl.BlockSpec((1, SSD_HEADS), lambda i: (0, 0))
    gspec = pl.BlockSpec((1, SSD_DIM), lambda i: (0, 0))
    in_specs = [pl.BlockSpec((CHUNK, CONV_DIM), rev(0)), pl.BlockSpec((CHUNK, DT_PAD), rev(0)),
                pl.BlockSpec((CHUNK, SSD_DIM), rev(0)),
                pl.BlockSpec((1, srows, SSD_STATE), lambda i: (nc - 1 - i, 0, 0)),
                pl.BlockSpec((CHUNK, SSD_DIM), rev(0)), small, small, small, gspec]
    out_specs = [pl.BlockSpec((CHUNK, CONV_DIM), rev(0)), pl.BlockSpec((CHUNK, DT_PAD), rev(0)),
                 pl.BlockSpec((CHUNK, SSD_DIM), rev(0)), small, small, small, gspec]
    out_shape = [jax.ShapeDtypeStruct((s, CONV_DIM), F32), jax.ShapeDtypeStruct((s, DT_PAD), F32),
                 jax.ShapeDtypeStruct((s, SSD_DIM), F32), jax.ShapeDtypeStruct((1, SSD_HEADS), F32),
                 jax.ShapeDtypeStruct((1, SSD_HEADS), F32), jax.ShapeDtypeStruct((1, SSD_HEADS), F32),
                 jax.ShapeDtypeStruct((1, SSD_DIM), F32)]
    args, scratch = [xact, dtr, z, states, dy, dt_bias, a_log, d_skip, g], [pltpu.VMEM((srows, SSD_STATE), F32)]
    if exchange:
        x_in, x_out, sems = exchange.specs()
        in_specs, out_specs, scratch = in_specs + x_in, out_specs + x_out, scratch + sems
        out_shape, args = out_shape + exchange.outputs, args + exchange.inputs
    return pl.pallas_call(
        body, name="ssd_bwd" + tag, grid=(nc,), in_specs=in_specs, out_specs=out_specs, out_shape=out_shape,
        scratch_shapes=scratch, compiler_params=_params(("arbitrary",)),
    )(*args)


TQ = 128
TK = 128
SB_SCALE = 1.0 / math.sqrt(SB_HEAD_DIM)


def _split2(x):
    hi = x.astype(BF16)
    return hi, (x - hi.astype(F32)).astype(BF16)


TK_WIDE = 256
TK_NEAR = 512
SB_UNDERFLOW = -110.0


def _sb_logits(qhs, kb, t0, s0, masked):
    zs = [_dg(qh, kb, _NT) for qh in qhs]
    mask = None
    if masked:
        t_pos = t0 + lax.broadcasted_iota(jnp.int32, zs[0].shape, 0)
        s_pos = s0 + lax.broadcasted_iota(jnp.int32, zs[0].shape, 1)
        mask = s_pos < t_pos
    lbs = [jnp.minimum(z, 0.0) - jnp.log(1.0 + jnp.exp(-jnp.abs(z))) for z in zs]
    lss = [lb - z for lb, z in zip(lbs, zs)]
    if masked:
        lss = [jnp.where(mask, ls, 0.0) for ls in lss]
    return lbs, lss, mask


def _running_sums(xs, starts, u, reverse, two_terms=True):
    nsub = xs[0].shape[1] // TK
    order = list(reversed(range(nsub))) if reverse else list(range(nsub))
    chunks = [[x[:, c * TK:(c + 1) * TK] for c in range(nsub)] for x in xs]
    sums = [[(_lane_sums(xc, u) if two_terms else _dg(xc, u, _NN)) for xc in row] for row in chunks]
    out = []
    for row, srow, run in zip(chunks, sums, starts):
        parts = [None] * nsub
        for c in order:
            parts[c] = run + srow[c]
            run = run + jnp.sum(row[c], axis=1, keepdims=True)
        out.append((parts[0] if nsub == 1 else jnp.concatenate(parts, axis=1), run))
    return out


def _lane_sums(x, u):
    hi, lo = _split2(x)
    return _dg(hi, u, _NN) + _dg(lo, u, _NN)


def _tri(cmp):
    j = lax.broadcasted_iota(jnp.int32, (TK, TK), 0)
    s = lax.broadcasted_iota(jnp.int32, (TK, TK), 1)
    return cmp(j, s).astype(BF16)


def _sb_fwd(tag, q, k, v, exchange=None):
    s = q.shape[0]
    npair = SB_DIM // LANES
    nq = s // TQ
    wide = min(TK_WIDE, s)
    near = min(TK_NEAR, s)
    per = wide // TQ
    n_xi = len(exchange.inputs) if exchange else 0

    def body(*refs):
        q_ref, k_ref, v_ref = refs[:3]
        o_ref, t_ref = refs[3 + n_xi:5 + n_xi]
        tq = pl.program_id(1)
        if exchange:
            pair = pl.program_id(0)
            mine = exchange.pick(refs, 3, 2)
            exchange.before_work(mine, jnp.logical_and(pair == 0, tq == 0),
                                 jnp.logical_and(pair == npair - 1, tq == 0))
        diag = tq // per
        qp = q_ref[...]
        lane = lax.broadcasted_iota(jnp.int32, (1, LANES), 1)
        u_gt = _tri(lambda j, s: j > s)
        heads = range(LANES // SB_HEAD_DIM)
        hms = [(lane // SB_HEAD_DIM) == hh for hh in heads]
        qhs = [jnp.where(hm, qp, jnp.zeros_like(qp)) * SB_SCALE for hm in hms]

        def block(wb, width, carry, masked):
            off = pl.multiple_of(wb * wide, wide)
            kb = k_ref[pl.ds(off, width), :]
            vb = v_ref[pl.ds(off, width), :]
            lbs, lss, mask = _sb_logits(qhs, kb, tq * TQ, wb * wide, masked)
            sums = _running_sums(lss, [c[0] for c in carry], u_gt, reverse=True)
            ws = [jnp.exp(lb + later) for lb, (later, _) in zip(lbs, sums)]
            if masked:
                ws = [jnp.where(mask, w, 0.0) for w in ws]
            pvs = [_dg(w, vb, _NN) for w in ws]
            return tuple((r, c[1] + pv) for (_, r), c, pv in zip(sums, carry, pvs))

        def more(c):
            alive = jnp.max(c[1][0][0])
            for hh in heads[1:]:
                alive = jnp.maximum(alive, jnp.max(c[1][hh][0]))
            return jnp.logical_and(c[0] >= 0, alive > SB_UNDERFLOW)

        start = tuple((jnp.zeros((TQ, 1), F32), jnp.zeros((TQ, LANES), F32)) for _ in heads)
        near_blk = jnp.maximum(diag - (near // wide - 1), 0)
        wb, done = lax.while_loop(more, lambda c: (c[0] - 1, block(c[0], wide, c[1], False)),
                                  (near_blk - 1, block(near_blk, near, start, True)))
        first = (wb + 1).astype(F32)
        out = jnp.zeros((TQ, LANES), F32)
        tot = jnp.zeros((TQ, LANES), F32)
        for hh in heads:
            r, acc = done[hh]
            out = out + jnp.where(hms[hh], acc, 0.0)
            tot = tot + jnp.where(hms[hh], jnp.where(lane % SB_HEAD_DIM == 1, first, r), 0.0)
        o_ref[...] = out
        t_ref[...] = tot
        if exchange:
            exchange.after_work(mine, jnp.logical_and(pair == npair - 1, tq == nq - 1), mid_done=True)

    tile = pl.BlockSpec((TQ, LANES), lambda p, t: (t, p))
    full = pl.BlockSpec((s, LANES), lambda p, t: (0, p))
    in_specs, out_specs = [tile, full, full], [tile, tile]
    out_shape = [jax.ShapeDtypeStruct((s, SB_DIM), F32)] * 2
    args, scratch = [q, k, v], []
    if exchange:
        x_in, x_out, scratch = exchange.specs()
        in_specs, out_specs = in_specs + x_in, out_specs + x_out
        out_shape, args = out_shape + exchange.outputs, args + exchange.inputs
    return pl.pallas_call(
        body, name="sb_fwd" + tag, grid=(npair, nq), in_specs=in_specs, out_specs=out_specs, out_shape=out_shape,
        scratch_shapes=scratch, compiler_params=_params(("arbitrary", "arbitrary")),
    )(*args)


def _sb_bwd(tag, q, k, v, tot, do, exchange=None):
    s = q.shape[0]
    npair = SB_DIM // LANES
    nq = s // TQ
    wide = min(TK_WIDE, s)
    near = min(TK_NEAR, s)
    per = wide // TQ
    n_xi = len(exchange.inputs) if exchange else 0

    def body(*refs):
        q_ref, k_ref, v_ref, t_ref, do_ref = refs[:5]
        dq_ref, dk_ref, dv_ref = refs[5 + n_xi:8 + n_xi]
        tq = pl.program_id(1)
        if exchange:
            pair = pl.program_id(0)
            mine = exchange.pick(refs, 5, 3)
            exchange.before_work(mine, jnp.logical_and(pair == 0, tq == 0),
                                 jnp.logical_and(pair == npair - 1, tq == 0))
        diag = tq // per

        @pl.when(tq == 0)
        def _():
            dk_ref[...] = jnp.zeros_like(dk_ref)
            dv_ref[...] = jnp.zeros_like(dv_ref)

        qp = q_ref[...]
        dop = do_ref[...]
        totp = t_ref[...]
        lane = lax.broadcasted_iota(jnp.int32, (1, LANES), 1)
        u_le = _tri(lambda j, s: j <= s)
        u_lt = _tri(lambda j, s: j < s)
        heads = range(LANES // SB_HEAD_DIM)
        hms = [(lane // SB_HEAD_DIM) == hh for hh in heads]
        qhs = [jnp.where(hm, qp, jnp.zeros_like(qp)) * SB_SCALE for hm in hms]
        dohs = [jnp.where(hm, dop, 0.0).astype(BF16) for hm in hms]
        totals = [jnp.sum(jnp.where(lane == hh * SB_HEAD_DIM, totp, 0.0), axis=1, keepdims=True) for hh in heads]
        first = jnp.max(jnp.where(lane == 1, totp, 0.0)).astype(jnp.int32)

        def block(wb, width, carry, masked):
            off = pl.multiple_of(wb * wide, wide)
            kb = k_ref[pl.ds(off, width), :]
            vb = v_ref[pl.ds(off, width), :]
            lbs, lss, mask = _sb_logits(qhs, kb, tq * TQ, wb * wide, masked)
            dws = [_dg(doh, vb, _NT) for doh in dohs]
            pres = _running_sums(lss, [c[0] for c in carry], u_le, reverse=False)
            ws = [jnp.exp(lb + (total - before)) for lb, total, (before, _) in zip(lbs, totals, pres)]
            if masked:
                ws = [jnp.where(mask, w, 0.0) for w in ws]
            gs = [w * dw for w, dw in zip(ws, dws)]
            lefts = _running_sums(gs, [c[1] for c in carry], u_lt, reverse=False, two_terms=False)
            dzs = [g - jnp.exp(lb) * (g + g_left) for g, lb, (g_left, _) in zip(gs, lbs, lefts)]
            if masked:
                dzs = [jnp.where(mask, dz, 0.0) for dz in dzs]
            dzbs = [dz.astype(BF16) for dz in dzs]
            dks = [_dg(dzb, qh, _TN) for dzb, qh in zip(dzbs, qhs)]
            dvs = [_dg(w, doh, _TN) for w, doh in zip(ws, dohs)]
            dqs = [_dg(dzb, kb, _NN) for dzb in dzbs]
            dk_ref[pl.ds(off, width), :] += functools.reduce(jnp.add, dks)
            dv_ref[pl.ds(off, width), :] += functools.reduce(jnp.add, dvs)
            return tuple((pre, gc, c[2] + dq) for (_, pre), (_, gc), c, dq in zip(pres, lefts, carry, dqs))

        zero = jnp.zeros((TQ, 1), F32)
        start = tuple((zero, zero, jnp.zeros((TQ, LANES), F32)) for _ in heads)
        near_blk = jnp.maximum(diag - (near // wide - 1), 0)
        far = lax.fori_loop(first, near_blk, lambda j, c: block(j, wide, c, False), start)
        done = block(near_blk, near, far, True)
        dq = jnp.zeros((TQ, LANES), F32)
        for hh in heads:
            dq = dq + jnp.where(hms[hh], done[hh][2], 0.0)
        dq_ref[...] = dq * SB_SCALE
        if exchange:
            exchange.after_work(mine, jnp.logical_and(pair == npair - 1, tq == nq - 1), mid_done=True)

    tile = pl.BlockSpec((TQ, LANES), lambda p, t: (t, p))
    full = pl.BlockSpec((s, LANES), lambda p, t: (0, p))
    in_specs, out_specs = [tile, full, full, tile, tile], [tile, full, full]
    out_shape = [jax.ShapeDtypeStruct((s, SB_DIM), F32)] * 3
    args, scratch = [q, k, v, tot, do], []
    if exchange:
        x_in, x_out, scratch = exchange.specs()
        in_specs, out_specs = in_specs + x_in, out_specs + x_out
        out_shape, args = out_shape + exchange.outputs, args + exchange.inputs
    return pl.pallas_call(
        body, name="sb_bwd" + tag, grid=(npair, nq), in_specs=in_specs, out_specs=out_specs, out_shape=out_shape,
        scratch_shapes=scratch, compiler_params=_params(("arbitrary", "arbitrary")),
    )(*args)


def _out_tile(y_ssd, o, sb_g, w_out):
    y_all = jnp.concatenate([y_ssd, _rms(o, sb_g)], axis=1)
    return mm_nn(y_all, w_out)


def _out_fwd(tag, h, y_ssd, o, sb_g, w_out):
    return _rowcall("out_fwd" + tag, lambda h, y, o, g, w: (h + _out_tile(y, o, g, w),),
                    [h, y_ssd, o], [sb_g, w_out], [(D_MODEL, F32)], tm=512)[0]


def _out_bwd(tag, y_ssd, o, dh, sb_g, w_out):
    def fn(y, o, dh, g, w):
        _, vjp = jax.vjp(_out_tile, y, o, g, w.astype(F32))
        return vjp(dh)

    return _rowcall("out_bwd" + tag, fn, [y_ssd, o, dh], [sb_g, w_out], [(SSD_DIM, F32), (SB_DIM, F32)],
                    [((1, SB_DIM), F32), ((D_MODEL, D_MODEL), F32)], tm=256)


def _mem_tile(mem, g, w_k, w_v):
    m = _rms(mem, g)
    return mm_nn(m, w_k), mm_nn(m, w_v)


def _mem_fwd(tag, mem, g, w_k, w_v):
    return _rowcall("mem_fwd" + tag, _mem_tile, [mem], [g, w_k, w_v], [(XA_DIM, F32), (XA_DIM, F32)], tm=256)


def _mem_bwd(tag, mem, dkx, dvx, g, w_k, w_v):
    def fn(mem, dkx, dvx, g, w_k, w_v):
        _, vjp = jax.vjp(lambda g, a, b: _mem_tile(mem, g, a, b), g, w_k.astype(F32), w_v.astype(F32))
        return vjp((dkx, dvx))

    return _rowcall("mem_bwd" + tag, fn, [mem, dkx, dvx], [g, w_k, w_v], [],
                    [((1, D_MODEL), F32), ((D_MODEL, XA_DIM), F32), ((D_MODEL, XA_DIM), F32)], tm=256)


def _xattn_tile(h, g, w_q, kx, vx, w_o):
    q = mm_nn(_rms(h, g), w_q)
    scale = 1.0 / math.sqrt(XA_HEAD_DIM)
    outs = []
    for i in range(XA_HEADS):
        sl = slice(i * XA_HEAD_DIM, (i + 1) * XA_HEAD_DIM)
        p = jax.nn.softmax(mm_nt(q[:, sl], kx[:, sl]) * scale, axis=-1)
        outs.append(mm_nn(p, vx[:, sl]))
    return mm_nn(jnp.concatenate(outs, axis=1), w_o)


def _xattn_fwd(tag, h, g, w_q, kx, vx, w_o):
    return _rowcall("xattn_fwd" + tag, lambda h, g, wq, kx, vx, wo: (h + _xattn_tile(h, g, wq, kx, vx, wo),),
                    [h], [g, w_q, kx, vx, w_o], [(D_MODEL, F32)], tm=512)[0]


def _xattn_bwd(tag, h, dh_out, g, w_q, kx, vx, w_o, exchange=None):
    def fn(h, dh_out, g, w_q, kx, vx, w_o):
        _, vjp = jax.vjp(_xattn_tile, h, g, w_q.astype(F32), kx, vx, w_o.astype(F32))
        dh, dg, dwq, dkx, dvx, dwo = vjp(dh_out)
        return dh_out + dh, dg, dwq, dkx, dvx, dwo

    mlen = kx.shape[0]
    return _rowcall("xattn_bwd" + tag, fn, [h, dh_out], [g, w_q, kx, vx, w_o], [(D_MODEL, F32)],
                    [((1, D_MODEL), F32), ((D_MODEL, XA_DIM), F32), ((mlen, XA_DIM), F32), ((mlen, XA_DIM), F32),
                     ((XA_DIM, D_MODEL), F32)], tm=256, exchange=exchange)


def _mlp_fwd(tag, h, g, w1, w2, exchange=None):
    def fn(h, g, w1, w2):
        u = jnp.square(jnp.maximum(mm_nn(_rms(h, g), w1), 0.0))
        return (h + mm_nn(u, w2),)

    return _rowcall("mlp_fwd" + tag, fn, [h], [g, w1, w2], [(D_MODEL, F32)], tm=256, exchange=exchange)


def _mlp_bwd(tag, h, dh_out, g, w1, w2, exchange=None):
    def fn(h, dh_out, g, w1, w2):
        hn, vjp = jax.vjp(_rms, h, g)
        r = jnp.maximum(mm_nn(hn, w1), 0.0)
        dob = dh_out.astype(BF16)
        dp = mm_nt(dob, w2) * (2.0 * r)
        dh, dg = vjp(mm_nt(dp, w1))
        return dh_out + dh, hn, dp, r * r, dob, dg

    return _rowcall("mlp_bwd" + tag, fn, [h, dh_out], [g, w1, w2],
                    [(D_MODEL, F32), (D_MODEL, BF16), (D_FF, BF16), (D_FF, BF16), (D_MODEL, BF16)],
                    [((1, D_MODEL), F32)], tm=256, exchange=exchange)


def _head(h, g, target):
    def lossfn(h, g, t):
        err = jnp.square(_rms(h, g) - t)
        return 0.5 * jnp.sum(jnp.mean(err, axis=-1))

    def fn(h, t, g):
        loss, vjp = jax.vjp(lambda h, g: lossfn(h, g, t), h, g)
        dh, dg = vjp(jnp.ones((), F32))
        return dh, jnp.full((1, LANES), loss, F32), dg

    return _rowcall("head", fn, [h, target], [g], [(D_MODEL, F32)], [((1, LANES), F32), ((1, D_MODEL), F32)], tm=512)


def _row(v):
    return v.reshape(1, -1)


class LocalPlan:
    def __init__(self, mats):
        self.mats = mats

    def weights(self, l):
        return self.mats[l]

    def carried_by(self, kernel, l):
        return None

    def carried_out(self, kernel, l, outs):
        pass

    def mlp_grads_done(self, l, gm):
        pass

    def mixer_grads_done(self, l, gm):
        pass

    def grads_done(self, l, gm):
        pass


def _local_step(x, mem, target, sw, plan):
    h = x
    saved = []
    for l in range(DEPTH):
        tag = str(l)
        m = plan.weights(l)
        z, xbc, q, k, v, dtr, *carried = _proj_fwd(tag, h, _row(sw["norm_mix_g"][l]), m["w_in"],
                                                   exchange=plan.carried_by("proj_fwd", l))
        plan.carried_out("proj_fwd", l, carried)
        xact = _conv_fwd(tag, xbc, m["conv_w"], _row(sw["conv_b"][l]))
        y_ssd, states, *carried = _ssd_fwd(tag, xact, dtr, z, _row(sw["dt_bias"][l]), _row(sw["a_log"][l]),
                                           _row(sw["d_skip"][l]), _row(sw["ssd_norm_g"][l]),
                                           exchange=plan.carried_by("ssd_fwd", l))
        plan.carried_out("ssd_fwd", l, carried)
        o, sb_tot, *carried = _sb_fwd(tag, q, k, v, exchange=plan.carried_by("sb_fwd", l))
        plan.carried_out("sb_fwd", l, carried)
        h1 = _out_fwd(tag, h, y_ssd, o, _row(sw["sb_norm_g"][l]), m["w_out"])
        kx, vx = _mem_fwd(tag, mem, _row(sw["norm_mem_g"][l]), m["w_xk"], m["w_xv"])
        h2 = _xattn_fwd(tag, h1, _row(sw["norm_xa_g"][l]), m["w_xq"], kx, vx, m["w_xo"])
        h3, *carried = _mlp_fwd(tag, h2, _row(sw["norm_ff_g"][l]), m["w_ff1"], m["w_ff2"],
                                exchange=plan.carried_by("mlp_fwd", l))
        plan.carried_out("mlp_fwd", l, carried)
        saved.append((h, z, xbc, q, k, v, dtr, xact, y_ssd, states, o, sb_tot, h1, kx, vx, h2))
        h = h3

    dh, loss, d_final = _head(h, _row(sw["final_g"]), target)
    gm = [dict() for _ in range(DEPTH)]
    gs = {name: [None] * DEPTH for name in ("norm_mix_g", "conv_w", "conv_b", "dt_bias", "a_log", "d_skip",
                                            "ssd_norm_g", "sb_norm_g", "norm_xa_g", "norm_mem_g", "norm_ff_g")}
    for l in reversed(range(DEPTH)):
        tag = str(l)
        m = plan.weights(l)
        h0, z, xbc, q, k, v, dtr, xact, y_ssd, states, o, sb_tot, h1, kx, vx, h2 = saved[l]
        dh2, hn_b, dp_b, a_b, do_b, gs["norm_ff_g"][l], *carried = _mlp_bwd(
            tag, h2, dh, _row(sw["norm_ff_g"][l]), m["w_ff1"], m["w_ff2"], exchange=plan.carried_by("mlp_bwd", l))
        plan.carried_out("mlp_bwd", l, carried)
        gm[l]["w_ff1"] = _mm_tn_parts("dw_ff1" + tag, hn_b, dp_b, True)
        gm[l]["w_ff2"] = _mm_tn_parts("dw_ff2" + tag, a_b, do_b, False)
        plan.mlp_grads_done(l, gm[l])
        dh1, gs["norm_xa_g"][l], gm[l]["w_xq"], dkx, dvx, gm[l]["w_xo"], *carried = _xattn_bwd(
            tag, h1, dh2, _row(sw["norm_xa_g"][l]), m["w_xq"], kx, vx, m["w_xo"],
            exchange=plan.carried_by("xattn_bwd", l))
        plan.carried_out("xattn_bwd", l, carried)
        gs["norm_mem_g"][l], gm[l]["w_xk"], gm[l]["w_xv"] = _mem_bwd(
            tag, mem, dkx, dvx, _row(sw["norm_mem_g"][l]), m["w_xk"], m["w_xv"])
        dy_ssd, do, gs["sb_norm_g"][l], gm[l]["w_out"] = _out_bwd(
            tag, y_ssd, o, dh1, _row(sw["sb_norm_g"][l]), m["w_out"])
        plan.mixer_grads_done(l, gm[l])
        dq, dk, dv, *carried = _sb_bwd(tag, q, k, v, sb_tot, do, exchange=plan.carried_by("sb_bwd", l))
        plan.carried_out("sb_bwd", l, carried)
        dxact, ddtr, dz, gs["dt_bias"][l], gs["a_log"][l], gs["d_skip"][l], gs["ssd_norm_g"][l], *carried = _ssd_bwd(
            tag, xact, dtr, z, states, dy_ssd, _row(sw["dt_bias"][l]), _row(sw["a_log"][l]),
            _row(sw["d_skip"][l]), _row(sw["ssd_norm_g"][l]), exchange=plan.carried_by("ssd_bwd", l))
        plan.carried_out("ssd_bwd", l, carried)
        dxbc, gs["conv_w"][l], gs["conv_b"][l] = _conv_bwd(tag, xbc, dxact, m["conv_w"], _row(sw["conv_b"][l]))
        dh, hn_b, dp_b, gs["norm_mix_g"][l] = _proj_bwd(
            tag, h0, dh1, dz, dxbc, dq, dk, dv, ddtr, _row(sw["norm_mix_g"][l]), m["w_in"])
        gm[l]["w_in"] = _mm_tn_call("dw_in" + tag, hn_b, dp_b, 512, IN_PAD, 1024)
        plan.grads_done(l, gm[l])
    gs["final_g"] = d_final
    return loss, dh, gm, gs


ANY = pl.BlockSpec(memory_space=pl.ANY)
VMEM_SPEC = pl.BlockSpec(memory_space=pltpu.VMEM)


def _place():
    return lax.axis_index("x"), lax.axis_index("y"), lax.axis_index("c")


def _other_chips(x, y):
    return [(1 - x, y), (x, 1 - y), (1 - x, 1 - y)]


def _remote(send_sems, recv_sems, idx, src, dst, to):
    return pltpu.make_async_remote_copy(src_ref=src, dst_ref=dst, send_sem=send_sems.at[idx],
                                        recv_sem=recv_sems.at[idx], device_id=to, device_id_type=MESH)


def _run_exchange(name, ex):
    def body(*refs):
        mine = ex.pick(refs, 0, 0)
        ex.start(*mine)
        if ex.mid is not None:
            ex.mid(*mine)
        ex.end(*mine)

    x_in, x_out, scratch = ex.specs()
    return pl.pallas_call(body, name=name, in_specs=x_in, out_specs=x_out, out_shape=ex.outputs,
                          scratch_shapes=scratch)(*ex.inputs)


def _gather_exchange(layer, shards):
    n = len(shards)
    outs = [jax.ShapeDtypeStruct((N_CHIPS,) + s.shape[1:], s.dtype) for s in shards]

    def start(w_refs, o_refs, ss, rs):
        x, y, c = _place()

        @pl.when(c == layer)
        def _():
            for i in range(n):
                for kk, (cx, cy) in enumerate(_other_chips(x, y)):
                    _remote(ss, rs, 6 * i + kk, w_refs[i].at[layer], o_refs[i].at[2 * x + y], (cx, cy, layer)).start()

    def mid(w_refs, o_refs, ss, rs):
        x, y, c = _place()

        @pl.when(c == layer)
        def _():
            for i in range(n):
                for kk, (cx, cy) in enumerate(_other_chips(x, y)):
                    got = o_refs[i].at[2 * cx + cy]
                    _remote(ss, rs, 6 * i + kk, got, got, (x, y, c)).wait_recv()
                    _remote(ss, rs, 6 * i + 3 + kk, got, got, (x, y, 1 - layer)).start()

    def end(w_refs, o_refs, ss, rs):
        x, y, c = _place()
        for i in range(n):
            for kk, (cx, cy) in enumerate(_other_chips(x, y)):
                got = o_refs[i].at[2 * cx + cy]

                @pl.when(c == layer)
                def _():
                    _remote(ss, rs, 6 * i + kk, w_refs[i].at[layer], got, (x, y, c)).wait_send()
                    _remote(ss, rs, 6 * i + 3 + kk, got, got, (x, y, c)).wait_send()

                @pl.when(c != layer)
                def _():
                    _remote(ss, rs, 6 * i + 3 + kk, got, got, (x, y, c)).wait_recv()

    return Exchange(shards, outs, 6 * n, start, end, mid)


def _handover_exchange(layer, grads):
    n = len(grads)
    outs = [jax.ShapeDtypeStruct(g.shape, g.dtype) for g in grads]

    def start(g_refs, o_refs, ss, rs):
        x, y, c = _place()

        @pl.when(c != layer)
        def _():
            for i in range(n):
                _remote(ss, rs, i, g_refs[i], o_refs[i], (x, y, layer)).start()

    def end(g_refs, o_refs, ss, rs):
        x, y, c = _place()
        for i in range(n):
            @pl.when(c != layer)
            def _():
                _remote(ss, rs, i, g_refs[i], o_refs[i], (x, y, c)).wait_send()

            @pl.when(c == layer)
            def _():
                _remote(ss, rs, i, g_refs[i], o_refs[i], (x, y, c)).wait_recv()

    return Exchange(grads, outs, n, start, end)


def _scatter_exchange(layer, parts):
    n = len(parts)
    outs = [jax.ShapeDtypeStruct(p.shape, p.dtype) for p in parts]

    def start(s_refs, o_refs, ss, rs):
        x, y, c = _place()

        @pl.when(c == layer)
        def _():
            for i in range(n):
                for kk, (cx, cy) in enumerate(_other_chips(x, y)):
                    _remote(ss, rs, 3 * i + kk, s_refs[i].at[2 * cx + cy], o_refs[i].at[2 * x + y],
                            (cx, cy, layer)).start()

    def end(s_refs, o_refs, ss, rs):
        x, y, c = _place()

        @pl.when(c == layer)
        def _():
            for i in range(n):
                for kk, (cx, cy) in enumerate(_other_chips(x, y)):
                    got = o_refs[i].at[2 * cx + cy]
                    _remote(ss, rs, 3 * i + kk, got, got, (x, y, c)).wait_recv()
            for i in range(n):
                for kk, (cx, cy) in enumerate(_other_chips(x, y)):
                    _remote(ss, rs, 3 * i + kk, s_refs[i].at[2 * cx + cy], o_refs[i].at[2 * x + y],
                            (x, y, c)).wait_send()

    return Exchange(parts, outs, 3 * n, start, end)


def _return_exchange(reduced):
    flat = [g for layer in range(DEPTH) for g in reduced[layer]]
    n = len(reduced[0])
    outs = [jax.ShapeDtypeStruct(g.shape, g.dtype) for g in flat]

    def start(g_refs, o_refs, ss, rs):
        x, y, c = _place()
        for layer in range(DEPTH):
            @pl.when(c == layer)
            def _():
                for i in range(n):
                    k = layer * n + i
                    _remote(ss, rs, k, g_refs[k], o_refs[k], (x, y, 1 - layer)).start()

    def end(g_refs, o_refs, ss, rs):
        x, y, c = _place()
        for layer in range(DEPTH):
            for i in range(n):
                k = layer * n + i

                @pl.when(c == layer)
                def _():
                    _remote(ss, rs, k, g_refs[k], o_refs[k], (x, y, c)).wait_send()

                @pl.when(c != layer)
                def _():
                    _remote(ss, rs, k, g_refs[k], o_refs[k], (x, y, c)).wait_recv()

    return Exchange(flat, outs, DEPTH * n, start, end)


def _gather_small(tag, buf):
    shape = buf.shape

    def body(b_ref, o_ref, sum_ref, send_sems, recv_sems, local_sem):
        x, y, c = _place()
        me = 4 * x + 2 * y + c
        mine = pltpu.make_async_copy(b_ref, o_ref.at[me], local_sem)
        mine.start()
        flips = [(dx, dy, dc) for dx in (0, 1) for dy in (0, 1) for dc in (0, 1) if (dx, dy, dc) != (0, 0, 0)]
        sends = []

        def peer(dx, dy, dc):
            return (1 - x if dx else x, 1 - y if dy else y, 1 - c if dc else c)

        for kk, flip in enumerate(flips):
            cp = pltpu.make_async_remote_copy(src_ref=b_ref, dst_ref=o_ref.at[me], send_sem=send_sems.at[kk],
                                              recv_sem=recv_sems.at[kk], device_id=peer(*flip), device_id_type=MESH)
            cp.start()
            sends.append(cp)
        for kk, flip in enumerate(flips):
            px, py, pc = peer(*flip)
            frm = 4 * px + 2 * py + pc
            pltpu.make_async_remote_copy(src_ref=b_ref, dst_ref=o_ref.at[frm], send_sem=send_sems.at[kk],
                                         recv_sem=recv_sems.at[kk], device_id=(x, y, c),
                                         device_id_type=MESH).wait_recv()
        for cp in sends:
            cp.wait_send()
        mine.wait()
        total = o_ref[0]
        for d in range(1, N_DEV):
            total = total + o_ref[d]
        sum_ref[...] = total

    return pl.pallas_call(
        body, name="gather_small" + tag, in_specs=[VMEM_SPEC], out_specs=[VMEM_SPEC, VMEM_SPEC],
        out_shape=[jax.ShapeDtypeStruct((N_DEV,) + shape, buf.dtype), jax.ShapeDtypeStruct(shape, buf.dtype)],
        scratch_shapes=[pltpu.SemaphoreType.DMA((N_DEV - 1,)), pltpu.SemaphoreType.DMA((N_DEV - 1,)),
                        pltpu.SemaphoreType.DMA],
    )(buf)


def _add_handed(tag, layer, g, r, tr=256):
    _, rows, cols = g.shape
    tr = min(tr, rows)

    def body(g_ref, r_ref, o_ref):
        @pl.when(lax.axis_index("c") == layer)
        def _():
            o_ref[...] = (g_ref[...] + r_ref[...]).astype(o_ref.dtype)

    spec = pl.BlockSpec((1, tr, cols), lambda p, i: (p, i, 0))
    return pl.pallas_call(
        body, name="add_handed_" + tag, grid=(N_CHIPS, rows // tr), in_specs=[spec, spec], out_specs=spec,
        out_shape=jax.ShapeDtypeStruct(g.shape, BF16), compiler_params=_params(("arbitrary", "arbitrary")),
    )(g, r)


def _sum_parts(tag, layer, own, parts, tr=256):
    _, rows, cols = parts.shape
    tr = min(tr, rows)
    chip = (2 * lax.axis_index("x") + lax.axis_index("y")).astype(jnp.int32).reshape(1)

    def body(c_ref, own_ref, p1_ref, p2_ref, p3_ref, o_ref):
        @pl.when(lax.axis_index("c") == layer)
        def _():
            total = own_ref[0].astype(F32)
            for p_ref in (p1_ref, p2_ref, p3_ref):
                total = total + p_ref[0].astype(F32)
            o_ref[...] = total

    def after(kk):
        return pl.BlockSpec((1, tr, cols), lambda i, c_ref: ((c_ref[0] + kk) % N_CHIPS, i, 0))

    return pl.pallas_call(
        body, name="sum_parts_" + tag,
        grid_spec=pltpu.PrefetchScalarGridSpec(
            num_scalar_prefetch=1, grid=(rows // tr,), in_specs=[after(0), after(1), after(2), after(3)],
            out_specs=pl.BlockSpec((tr, cols), lambda i, c_ref: (i, 0))),
        out_shape=jax.ShapeDtypeStruct((rows, cols), F32),
        compiler_params=_params(("arbitrary",)),
    )(chip, own, parts, parts, parts)


def _adamw_math(w, g, m, v):
    m = ADAM_B1 * m + (1.0 - ADAM_B1) * g
    v = ADAM_B2 * v + (1.0 - ADAM_B2) * jnp.square(g)
    m_hat = m / (1.0 - ADAM_B1 ** ADAM_STEP)
    v_hat = v / (1.0 - ADAM_B2 ** ADAM_STEP)
    delta = -ADAM_LR * (m_hat / (jnp.sqrt(v_hat) + ADAM_EPS) + ADAM_WD * w)
    return delta, m, v


def _adamw(tag, w, computed, received, m, v, tr=256):
    _, rows, cols = w.shape
    tr = min(tr, rows)

    def body(*refs):
        w_ref, m_ref, v_ref = refs[:3]
        g_refs = refs[3:3 + 2 * DEPTH]
        g_ref, d_ref, nm_ref, nv_ref = refs[3 + 2 * DEPTH:]
        layer = pl.program_id(0)
        core = lax.axis_index("c")
        g = jnp.zeros((tr, cols), F32)
        for l in range(DEPTH):
            mine = jnp.where(core == l, g_refs[2 * l][...], g_refs[2 * l + 1][...])
            g = jnp.where(layer == l, mine, g)
        g_ref[0] = g
        d_ref[0], nm_ref[0], nv_ref[0] = _adamw_math(w_ref[0], g, m_ref[0], v_ref[0])

    stacked = pl.BlockSpec((1, tr, cols), lambda l, i: (l, i, 0))

    def of_layer(k):
        return pl.BlockSpec((tr, cols), lambda l, i: (jnp.where(l == k, i, 0), 0))

    g_specs = [of_layer(l) for l in range(DEPTH) for _ in range(2)]
    g_args = [a for l in range(DEPTH) for a in (computed[l], received[l])]
    return pl.pallas_call(
        body, name="adamw_" + tag, grid=(DEPTH, rows // tr), in_specs=[stacked] * 3 + g_specs,
        out_specs=[stacked] * 4, out_shape=[jax.ShapeDtypeStruct(w.shape, F32)] * 4,
        compiler_params=_params(("arbitrary", "arbitrary")),
    )(w, m, v, *g_args)


def _w_in_to_padded(w):
    d0 = SSD_DIM + CONV_DIM
    return jnp.concatenate([w[:, :d0], w[:, d0 + SSD_HEADS:], w[:, d0:d0 + SSD_HEADS],
                            jnp.zeros((w.shape[0], DT_PAD - SSD_HEADS), w.dtype)], axis=1)


def _w_in_from_padded(w):
    d0 = SSD_DIM + CONV_DIM
    return jnp.concatenate([w[:, :d0], w[:, DT_OFF:DT_OFF + SSD_HEADS], w[:, d0:DT_OFF]], axis=1)


def _small_layout():
    return (("norm_mix_g", 0, 0, D_MODEL), ("norm_xa_g", 1, 0, D_MODEL), ("norm_mem_g", 2, 0, D_MODEL),
            ("norm_ff_g", 3, 0, D_MODEL), ("conv_b", 4, 0, CONV_DIM), ("ssd_norm_g", 5, 0, SSD_DIM),
            ("sb_norm_g", 5, SSD_DIM, SB_DIM), ("dt_bias", 6, 0, SSD_HEADS), ("a_log", 6, LANES, SSD_HEADS),
            ("d_skip", 6, 2 * LANES, SSD_HEADS))


def _pack_small(gs, loss):
    lay = _small_layout()
    args = [gs[name][l] for l in range(DEPTH) for name, _, _, _ in lay]
    args += [gs["conv_w"][l] for l in range(DEPTH)] + [gs["final_g"], loss]
    n_lay = len(lay)

    def body(*refs):
        o_ref = refs[-1]
        o_ref[...] = jnp.zeros_like(o_ref)
        for l in range(DEPTH):
            for i, (_, rr, c0, width) in enumerate(lay):
                row = l * _SM_PER_LAYER + rr
                o_ref[row:row + 1, c0:c0 + width] = refs[l * n_lay + i][...]
            row = _SM_CONVW + l * CONV_K
            o_ref[row:row + CONV_K, 0:CONV_DIM] = refs[DEPTH * n_lay + l][...]
        o_ref[_SM_FINAL:_SM_FINAL + 1, :] = refs[DEPTH * n_lay + DEPTH][...]
        o_ref[_SM_LOSS:_SM_LOSS + 1, 0:LANES] = refs[DEPTH * n_lay + DEPTH + 1][...]

    return pl.pallas_call(
        body, name="pack_small", in_specs=[VMEM_SPEC] * len(args), out_specs=VMEM_SPEC,
        out_shape=jax.ShapeDtypeStruct((SMALL_ROWS, PACK_COLS), F32),
    )(*args)


def _small_update(buf, w, mom, var):
    lay = _small_layout()
    names = [name for name, _, _, _ in lay] + ["final_g", "conv_w"]
    conv_cols = CONV_DIM // N_CHIPS
    shapes2d = {name: (DEPTH, width) for name, _, _, width in lay}
    shapes2d["final_g"] = (1, D_MODEL)
    shapes2d["conv_w"] = (DEPTH * CONV_K, conv_cols)
    args = [buf]
    for src in (w, mom, var):
        args += [src[name].reshape(shapes2d[name]) for name in names]
    n = len(names)

    def body(*refs):
        b_ref = refs[0]
        w_refs, m_refs, v_refs = refs[1:1 + n], refs[1 + n:1 + 2 * n], refs[1 + 2 * n:1 + 3 * n]
        outs = refs[1 + 3 * n:]
        chip = 2 * lax.axis_index("x") + lax.axis_index("y")
        for i, name in enumerate(names):
            if name == "final_g":
                g = b_ref[_SM_FINAL:_SM_FINAL + 1, :]
            elif name == "conv_w":
                rows = b_ref[_SM_CONVW:_SM_CONVW + DEPTH * CONV_K, 0:CONV_DIM]
                g = jnp.zeros((DEPTH * CONV_K, conv_cols), F32)
                for j in range(N_CHIPS):
                    g = g + jnp.where(chip == j, rows[:, j * conv_cols:(j + 1) * conv_cols], 0.0)
            else:
                _, rr, c0, width = lay[i]
                g = jnp.concatenate([b_ref[l * _SM_PER_LAYER + rr:l * _SM_PER_LAYER + rr + 1, c0:c0 + width]
                                     for l in range(DEPTH)], axis=0)
            d, m2, v2 = _adamw_math(w_refs[i][...], g, m_refs[i][...], v_refs[i][...])
            outs[i][...] = g
            outs[n + i][...] = d
            outs[2 * n + i][...] = m2
            outs[3 * n + i][...] = v2

    out_shape = [jax.ShapeDtypeStruct(shapes2d[name], F32) for _ in range(4) for name in names]
    res = pl.pallas_call(
        body, name="small_update", in_specs=[VMEM_SPEC] * len(args), out_specs=[VMEM_SPEC] * (4 * n),
        out_shape=out_shape,
    )(*args)
    return tuple({name: res[k * n + i].reshape(w[name].shape) for i, name in enumerate(names)} for k in range(4))


SMALL_NAMES = ("norm_mix_g", "conv_b", "dt_bias", "a_log", "d_skip", "ssd_norm_g", "sb_norm_g", "norm_xa_g",
               "norm_mem_g", "norm_ff_g", "final_g")
WEIGHT_ORDER = ("norm_mix_g", "w_in", "conv_w", "conv_b", "dt_bias", "a_log", "d_skip", "ssd_norm_g", "sb_norm_g",
                "w_out", "norm_xa_g", "norm_mem_g", "w_xq", "w_xk", "w_xv", "w_xo", "norm_ff_g", "w_ff1", "w_ff2",
                "final_g")


_ALL = tuple(range(len(MATS)))
_IN = tuple(i for i in _ALL if MATS[i][0] == "w_in")
_MLP = tuple(i for i in _ALL if MATS[i][0] in ("w_ff1", "w_ff2"))
_MIXER = tuple(i for i in _ALL if i not in _IN + _MLP)
_CONV = len(MATS)


class PipelinedPlan(LocalPlan):
    def __init__(self, shards, conv_w):
        self.shards = list(shards) + [conv_w]
        self.chip = 2 * lax.axis_index("x") + lax.axis_index("y")
        self.mats = [dict() for _ in range(DEPTH)]
        n = len(MATS)
        self.parts = [[None] * n for _ in range(DEPTH)]
        self.to_chips = [[None] * n for _ in range(DEPTH)]
        self.reduced = [[None] * n for _ in range(DEPTH)]
        self.riders = {}
        first = _IN + (_CONV,)
        self._gathered(0, first, _run_exchange("gather_first", self._gather(0, first)))
        self._gather_behind(0, _MIXER, "proj_fwd", 0)
        self._gather_behind(0, _MLP[:1], "ssd_fwd", 0)
        self._gather_behind(0, _MLP[1:], "sb_fwd", 0)
        for l in range(1, DEPTH):
            self._gather_behind(l, first + _MIXER, "sb_fwd", l - 1)
            self._gather_behind(l, _MLP, "mlp_fwd", l - 1)

    def _ride(self, kernel, l, exchange, then):
        self.riders.setdefault((kernel, l), []).append((exchange, then))

    def carried_by(self, kernel, l):
        exchange = None
        for ex, _ in self.riders.get((kernel, l), []):
            exchange = _both(exchange, ex)
        return exchange

    def carried_out(self, kernel, l, outs):
        for ex, then in self.riders.pop((kernel, l), []):
            then(outs[:len(ex.outputs)])
            outs = outs[len(ex.outputs):]

    def _gather(self, l, which):
        return _gather_exchange(l, [self.shards[i] for i in which])

    def _gather_behind(self, l, which, kernel, host):
        self._ride(kernel, host, self._gather(l, which), lambda outs: self._gathered(l, which, outs))

    def _gathered(self, l, which, outs):
        for i, theirs in zip(which, outs):
            name, _, axis = MATS[i] if i != _CONV else ("conv_w", None, 1)
            full = jnp.concatenate([jnp.where(self.chip == j, self.shards[i][l], theirs[j]) for j in range(N_CHIPS)],
                                   axis=axis)
            self.mats[l][name] = _w_in_to_padded(full) if name == "w_in" else full

    def _set_parts(self, l, which, gm):
        for i in which:
            name, _, axis = MATS[i]
            g = _w_in_from_padded(gm[name]) if name == "w_in" else gm[name]
            if g.ndim == 2 and axis == 0:
                g = g.reshape((N_CHIPS, g.shape[0] // N_CHIPS, g.shape[1]))
            elif g.ndim == 2:
                g = jnp.swapaxes(g.reshape((g.shape[0], N_CHIPS, g.shape[1] // N_CHIPS)), 0, 1)
            self.parts[l][i] = g

    def _handover(self, l, which):
        return _handover_exchange(l, [self.parts[l][i] for i in which])

    def _handed(self, l, which, outs):
        for i, r in zip(which, outs):
            self.to_chips[l][i] = _add_handed(MATS[i][0] + str(l), l, self.parts[l][i], r)

    def _scatter(self, l, which):
        return _scatter_exchange(l, [self.to_chips[l][i] for i in which])

    def _scattered(self, l, which, outs):
        for i, got in zip(which, outs):
            self.reduced[l][i] = _sum_parts(MATS[i][0] + str(l), l, self.to_chips[l][i], got)

    def _send_behind(self, l, which, hand_kernel, cross_kernel, host):
        def handed(outs):
            self._handed(l, which, outs)
            self._ride(cross_kernel, host, self._scatter(l, which), lambda o: self._scattered(l, which, o))

        self._ride(hand_kernel, host, self._handover(l, which), handed)

    def mlp_grads_done(self, l, gm):
        self._set_parts(l, _MLP, gm)
        self._send_behind(l, _MLP, "xattn_bwd", "sb_bwd", l)

    def mixer_grads_done(self, l, gm):
        self._set_parts(l, _MIXER, gm)
        self._send_behind(l, _MIXER, "sb_bwd", "ssd_bwd", l)

    def grads_done(self, l, gm):
        self._set_parts(l, _IN, gm)
        if l > 0:
            self._send_behind(l, _IN, "mlp_bwd", "sb_bwd", l - 1)
        else:
            self._handed(0, _IN, _run_exchange("handover_last", self._handover(0, _IN)))
            self._scattered(0, _IN, _run_exchange("scatter_last", self._scatter(0, _IN)))

    def reduced_gradients(self):
        returned = _run_exchange("return_reduced", _return_exchange(self.reduced))
        n = len(MATS)
        return [([self.reduced[l][i] for l in range(DEPTH)], [returned[l * n + i] for l in range(DEPTH)])
                for i in range(n)]


def kernel(x, mem, norm_mix_g, w_in, conv_w, conv_b, dt_bias, a_log, d_skip, ssd_norm_g, sb_norm_g, w_out, norm_xa_g, norm_mem_g, w_xq, w_xk, w_xv, w_xo, norm_ff_g, w_ff1, w_ff2, final_g, loss_target, m_norm_mix_g, m_w_in, m_conv_w, m_conv_b, m_dt_bias, m_a_log, m_d_skip, m_ssd_norm_g, m_sb_norm_g, m_w_out, m_norm_xa_g, m_norm_mem_g, m_w_xq, m_w_xk, m_w_xv, m_w_xo, m_norm_ff_g, m_w_ff1, m_w_ff2, m_final_g, v_norm_mix_g, v_w_in, v_conv_w, v_conv_b, v_dt_bias, v_a_log, v_d_skip, v_ssd_norm_g, v_sb_norm_g, v_w_out, v_norm_xa_g, v_norm_mem_g, v_w_xq, v_w_xk, v_w_xv, v_w_xo, v_norm_ff_g, v_w_ff1, v_w_ff2, v_final_g):
    w = dict(norm_mix_g=norm_mix_g, w_in=w_in, conv_w=conv_w, conv_b=conv_b, dt_bias=dt_bias, a_log=a_log,
             d_skip=d_skip, ssd_norm_g=ssd_norm_g, sb_norm_g=sb_norm_g, w_out=w_out, norm_xa_g=norm_xa_g,
             norm_mem_g=norm_mem_g, w_xq=w_xq, w_xk=w_xk, w_xv=w_xv, w_xo=w_xo, norm_ff_g=norm_ff_g, w_ff1=w_ff1,
             w_ff2=w_ff2, final_g=final_g)
    mom = dict(norm_mix_g=m_norm_mix_g, w_in=m_w_in, conv_w=m_conv_w, conv_b=m_conv_b, dt_bias=m_dt_bias,
               a_log=m_a_log, d_skip=m_d_skip, ssd_norm_g=m_ssd_norm_g, sb_norm_g=m_sb_norm_g, w_out=m_w_out,
               norm_xa_g=m_norm_xa_g, norm_mem_g=m_norm_mem_g, w_xq=m_w_xq, w_xk=m_w_xk, w_xv=m_w_xv, w_xo=m_w_xo,
               norm_ff_g=m_norm_ff_g, w_ff1=m_w_ff1, w_ff2=m_w_ff2, final_g=m_final_g)
    var = dict(norm_mix_g=v_norm_mix_g, w_in=v_w_in, conv_w=v_conv_w, conv_b=v_conv_b, dt_bias=v_dt_bias,
               a_log=v_a_log, d_skip=v_d_skip, ssd_norm_g=v_ssd_norm_g, sb_norm_g=v_sb_norm_g, w_out=v_w_out,
               norm_xa_g=v_norm_xa_g, norm_mem_g=v_norm_mem_g, w_xq=v_w_xq, w_xk=v_w_xk, w_xv=v_w_xv, w_xo=v_w_xo,
               norm_ff_g=v_norm_ff_g, w_ff1=v_w_ff1, w_ff2=v_w_ff2, final_g=v_final_g)
    sw = {name: w[name] for name in SMALL_NAMES}
    plan = PipelinedPlan([w[name].astype(BF16) for name, _, _ in MATS], conv_w)
    loss, grad_x, gm, gs = _local_step(x[0], mem[0], loss_target[0], sw, plan)
    g_mats = plan.reduced_gradients()

    _, small_sum = _gather_small("_grads", _pack_small(gs, loss))
    loss_out = small_sum[_SM_LOSS, 0]

    grads, deltas, new_m, new_v = {}, {}, {}, {}
    for (name, _, _), (computed, received) in zip(MATS, g_mats):
        grads[name], deltas[name], new_m[name], new_v[name] = _adamw(
            name, w[name], computed, received, mom[name], var[name])
    g_s, d_s, m_s, v_s = _small_update(small_sum, w, mom, var)
    for name in g_s:
        grads[name], deltas[name], new_m[name], new_v[name] = g_s[name], d_s[name], m_s[name], v_s[name]

    return (loss_out, grad_x[None], *[grads[n] for n in WEIGHT_ORDER], *[deltas[n] for n in WEIGHT_ORDER],
            *[new_m[n] for n in WEIGHT_ORDER], *[new_v[n] for n in WEIGHT_ORDER])
```

```python
import functools
import math

import jax
import jax.numpy as jnp
from jax import lax
from jax.experimental import pallas as pl
from jax.experimental.pallas import tpu as pltpu

F32 = jnp.float32
BF16 = jnp.bfloat16
MESH = pl.DeviceIdType.MESH

D_MODEL = 1024
DEPTH = 2
SSD_DIM = 512
SSD_HEAD_DIM = 64
SSD_HEADS = 8
SSD_GROUPS = 2
SSD_STATE = 64
CONV_K = 4
CHUNK = 128
SB_DIM = 512
SB_HEAD_DIM = 64
XA_HEADS = 4
XA_HEAD_DIM = 128
XA_DIM = 512
D_FF = 4096
EPS = 1e-5
GN = SSD_GROUPS * SSD_STATE
CONV_DIM = SSD_DIM + 2 * GN
IN_DIM = SSD_DIM + CONV_DIM + SSD_HEADS + 3 * SB_DIM
LANES = 128
DT_PAD = LANES
IN_PAD = SSD_DIM + CONV_DIM + 3 * SB_DIM + DT_PAD
Q_OFF = SSD_DIM + CONV_DIM
DT_OFF = Q_OFF + 3 * SB_DIM
HALO = 8

ADAM_LR = 0.001
ADAM_B1 = 0.9
ADAM_B2 = 0.999
ADAM_EPS = 1e-08
ADAM_WD = 0.01
ADAM_STEP = 10

N_CHIPS = 4
N_DEV = 8
PACK_COLS = 1024
VMEM_LIMIT = 56 * 1024 * 1024

MATS = (
    ("w_in", (D_MODEL, IN_DIM), 1),
    ("w_out", (D_MODEL, D_MODEL), 0),
    ("w_xq", (D_MODEL, XA_DIM), 0),
    ("w_xk", (D_MODEL, XA_DIM), 0),
    ("w_xv", (D_MODEL, XA_DIM), 0),
    ("w_xo", (XA_DIM, D_MODEL), 1),
    ("w_ff1", (D_MODEL, D_FF), 1),
    ("w_ff2", (D_FF, D_MODEL), 0),
)


SMALL_ROWS = 24
_SM_PER_LAYER = 7
_SM_FINAL = 14
_SM_CONVW = 15
_SM_LOSS = 23


_NN = ((1,), (0,))
_NT = ((1,), (1,))
_TN = ((0,), (0,))


def _dg(a, b, dims):
    return lax.dot_general(a.astype(BF16), b.astype(BF16), (dims, ((), ())), preferred_element_type=F32)


@jax.custom_vjp
def mm_nn(a, b):
    return _dg(a, b, _NN)


@jax.custom_vjp
def mm_nt(a, b):
    return _dg(a, b, _NT)


@jax.custom_vjp
def mm_tn(a, b):
    return _dg(a, b, _TN)


def _nn_fwd(a, b):
    return _dg(a, b, _NN), (a, b)


def _nn_bwd(res, g):
    a, b = res
    return mm_nt(g, b).astype(a.dtype), mm_tn(a, g).astype(b.dtype)


def _nt_fwd(a, b):
    return _dg(a, b, _NT), (a, b)


def _nt_bwd(res, g):
    a, b = res
    return mm_nn(g, b).astype(a.dtype), mm_tn(g, a).astype(b.dtype)


def _tn_fwd(a, b):
    return _dg(a, b, _TN), (a, b)


def _tn_bwd(res, g):
    a, b = res
    return mm_nt(b, g).astype(a.dtype), mm_nn(a, g).astype(b.dtype)


mm_nn.defvjp(_nn_fwd, _nn_bwd)
mm_nt.defvjp(_nt_fwd, _nt_bwd)
mm_tn.defvjp(_tn_fwd, _tn_bwd)


def _rms(x, g):
    return x * lax.rsqrt(jnp.mean(x * x, axis=-1, keepdims=True) + EPS) * g


def _params(sem=None, vmem=VMEM_LIMIT):
    return pltpu.CompilerParams(dimension_semantics=sem, vmem_limit_bytes=vmem)


class Exchange:
    def __init__(self, inputs, outputs, n_sems, start, end, mid=None):
        self.inputs, self.outputs, self.n_sems = list(inputs), list(outputs), n_sems
        self.start, self.mid, self.end = start, mid, end

    def specs(self):
        hbm = pl.BlockSpec(memory_space=pl.ANY)
        sems = [pltpu.SemaphoreType.DMA((self.n_sems,)), pltpu.SemaphoreType.DMA((self.n_sems,))]
        return [hbm] * len(self.inputs), [hbm] * len(self.outputs), sems

    def pick(self, refs, n_before_in, n_before_out):
        n_in, n_out = len(self.inputs), len(self.outputs)
        o0 = n_before_in + n_in + n_before_out
        return refs[n_before_in:n_before_in + n_in], refs[o0:o0 + n_out], refs[-2], refs[-1]

    def before_work(self, mine, first, mid=None):
        @pl.when(first)
        def _():
            self.start(*mine)

        if self.mid is not None and mid is not None:
            @pl.when(mid)
            def _():
                self.mid(*mine)

    def after_work(self, mine, last, mid_done):
        @pl.when(last)
        def _():
            if self.mid is not None and not mid_done:
                self.mid(*mine)
            self.end(*mine)


class _Shifted:
    def __init__(self, ref, base):
        self.ref, self.base = ref, base

    @property
    def at(self):
        return self

    def __getitem__(self, idx):
        return self.ref.at[self.base + idx]


def _both(a, b):
    if a is None or b is None:
        return a or b
    n_i, n_o, n_s = len(a.inputs), len(a.outputs), a.n_sems

    def joined(fa, fb):
        def f(i_refs, o_refs, ss, rs):
            if fa is not None:
                fa(i_refs[:n_i], o_refs[:n_o], ss, rs)
            if fb is not None:
                fb(i_refs[n_i:], o_refs[n_o:], _Shifted(ss, n_s), _Shifted(rs, n_s))
        return f

    mid = joined(a.mid, b.mid) if (a.mid is not None or b.mid is not None) else None
    return Exchange(a.inputs + b.inputs, a.outputs + b.outputs, n_s + b.n_sems, joined(a.start, b.start),
                    joined(a.end, b.end), mid)


def _rowcall(name, fn, rows, fulls, row_out, acc_out=(), tm=256, exchange=None):
    s = rows[0].shape[0]
    tm = min(tm, s)
    nt = s // tm
    n_r, n_f, n_ro, n_ao = len(rows), len(fulls), len(row_out), len(acc_out)
    n_xi = len(exchange.inputs) if exchange else 0

    def body(*refs):
        if exchange:
            mine = exchange.pick(refs, n_r + n_f, n_ro + n_ao)
            exchange.before_work(mine, pl.program_id(0) == 0, pl.program_id(0) == (3 * nt) // 4)
        ins = [r[...] for r in refs[: n_r + n_f]]
        outs = fn(*ins)
        o_refs = refs[n_r + n_f + n_xi:]
        for o_ref, val in zip(o_refs[:n_ro], outs[:n_ro]):
            o_ref[...] = val.astype(o_ref.dtype)
        if n_ao:
            first = pl.program_id(0) == 0

            @pl.when(first)
            def _():
                for o_ref, val in zip(o_refs[n_ro:], outs[n_ro:]):
                    o_ref[...] = val.astype(o_ref.dtype)

            @pl.when(jnp.logical_not(first))
            def _():
                for o_ref, val in zip(o_refs[n_ro:], outs[n_ro:]):
                    o_ref[...] += val.astype(o_ref.dtype)
        if exchange:
            exchange.after_work(mine, pl.program_id(0) == nt - 1, mid_done=True)

    in_specs = [pl.BlockSpec((tm, a.shape[1]), lambda i: (i, 0)) for a in rows]
    in_specs += [pl.BlockSpec(a.shape, lambda i: (0, 0), pipeline_mode=pl.Buffered(1)) for a in fulls]
    out_specs = [pl.BlockSpec((tm, c), lambda i: (i, 0)) for c, _ in row_out]
    out_specs += [pl.BlockSpec(shape, lambda i: (0, 0)) for shape, _ in acc_out]
    out_shape = [jax.ShapeDtypeStruct((s, c), dt) for c, dt in row_out]
    out_shape += [jax.ShapeDtypeStruct(shape, dt) for shape, dt in acc_out]
    args, scratch = [*rows, *fulls], []
    if exchange:
        x_in, x_out, scratch = exchange.specs()
        in_specs += x_in
        out_specs += x_out
        out_shape += exchange.outputs
        args += exchange.inputs
    return pl.pallas_call(
        body, name=name, grid=(nt,), in_specs=in_specs, out_specs=out_specs, out_shape=out_shape,
        scratch_shapes=scratch, compiler_params=_params(("arbitrary",)),
    )(*args)


def _mm_tn_call(name, a, b, tm, tn, tk):
    s, m = a.shape
    n = b.shape[1]
    tk = min(tk, s)

    def body(a_ref, b_ref, o_ref):
        d = _dg(a_ref[...], b_ref[...], _TN)
        first = pl.program_id(2) == 0

        @pl.when(first)
        def _():
            o_ref[...] = d

        @pl.when(jnp.logical_not(first))
        def _():
            o_ref[...] += d

    return pl.pallas_call(
        body, name=name, grid=(m // tm, n // tn, s // tk),
        in_specs=[pl.BlockSpec((tk, tm), lambda i, j, k: (k, i)), pl.BlockSpec((tk, tn), lambda i, j, k: (k, j))],
        out_specs=pl.BlockSpec((tm, tn), lambda i, j, k: (i, j)),
        out_shape=jax.ShapeDtypeStruct((m, n), F32),
        compiler_params=_params(("parallel", "parallel", "arbitrary")),
    )(a, b)


def _mm_tn_parts(name, a, b, by_cols, tm=512, tk=2048):
    s, m = a.shape
    n = b.shape[1]
    r, c = (m, n // N_CHIPS) if by_cols else (m // N_CHIPS, n)
    per = r // tm
    tk = min(tk, s)

    def body(a_ref, b_ref, o_ref):
        d = _dg(a_ref[...], b_ref[...], _TN)
        first = pl.program_id(2) == 0

        @pl.when(first)
        def _():
            o_ref[0] = d

        @pl.when(jnp.logical_not(first))
        def _():
            o_ref[0] += d

    if by_cols:
        out_map = lambda i, j, k: (j, i, 0)
    else:
        out_map = lambda i, j, k: (i // per, i % per, 0)
    return pl.pallas_call(
        body, name=name, grid=(m // tm, n // c, s // tk),
        in_specs=[pl.BlockSpec((tk, tm), lambda i, j, k: (k, i)), pl.BlockSpec((tk, c), lambda i, j, k: (k, j))],
        out_specs=pl.BlockSpec((1, tm, c), out_map),
        out_shape=jax.ShapeDtypeStruct((N_CHIPS, r, c), F32),
        compiler_params=_params(("parallel", "parallel", "arbitrary")),
    )(a, b)


def _proj_tile(h, g, w):
    p = mm_nn(_rms(h, g), w)
    return (p[:, :SSD_DIM], p[:, SSD_DIM:Q_OFF], p[:, Q_OFF:Q_OFF + SB_DIM],
            p[:, Q_OFF + SB_DIM:Q_OFF + 2 * SB_DIM], p[:, Q_OFF + 2 * SB_DIM:DT_OFF], p[:, DT_OFF:])


def _proj_fwd(tag, h, g, w, exchange=None):
    return _rowcall(
        "proj_fwd" + tag, _proj_tile, [h], [g, w],
        [(SSD_DIM, F32), (CONV_DIM, F32), (SB_DIM, BF16), (SB_DIM, BF16), (SB_DIM, BF16), (DT_PAD, F32)], tm=512,
        exchange=exchange)


def _proj_bwd(tag, h, dh_out, dz, dxbc, dq, dk, dv, ddt, g, w):
    def fn(h, dh_out, dz, dxbc, dq, dk, dv, ddt, g, w):
        dp = jnp.concatenate([dz.astype(BF16), dxbc.astype(BF16), dq.astype(BF16), dk.astype(BF16),
                              dv.astype(BF16), ddt.astype(BF16)], axis=1)
        hn, vjp = jax.vjp(_rms, h, g)
        dh, dg = vjp(mm_nt(dp, w))
        return dh_out + dh, hn, dp, dg

    return _rowcall(
        "proj_bwd" + tag, fn, [h, dh_out, dz, dxbc, dq, dk, dv, ddt], [g, w],
        [(D_MODEL, F32), (D_MODEL, BF16), (IN_PAD, BF16)], [((1, D_MODEL), F32)], tm=256)


def _shift_down(x, tail, j):
    if j == 0:
        return x
    r = pltpu.roll(x, j, 0)
    rt = pltpu.roll(tail, j, 0)
    row = lax.broadcasted_iota(jnp.int32, (HALO, x.shape[1]), 0)
    first = jnp.where(row < j, rt, r[:HALO])
    if x.shape[0] == HALO:
        return first
    return jnp.concatenate([first, r[HALO:]], axis=0)


def _shift_up(x, head, j):
    if j == 0:
        return x
    n = x.shape[0]
    r = pltpu.roll(x, n - j, 0)
    rh = pltpu.roll(head, HALO - j, 0)
    row = lax.broadcasted_iota(jnp.int32, (HALO, x.shape[1]), 0)
    return jnp.concatenate([r[:n - HALO], jnp.where(row >= HALO - j, rh, r[n - HALO:])], axis=0)


def _conv_pre(x, tail, w, b):
    acc = b + w[CONV_K - 1:CONV_K] * x
    for j in range(1, CONV_K):
        acc = acc + w[CONV_K - 1 - j:CONV_K - j] * _shift_down(x, tail, j)
    return acc


def _dsilu(p):
    s = jax.nn.sigmoid(p)
    return s * (1.0 + p * (1.0 - s))


def _conv_fwd(tag, xbc, w, b, tc=512):
    s, c = xbc.shape
    tc = min(tc, s)
    per = tc // HALO

    def body(x_ref, prev_ref, w_ref, b_ref, o_ref):
        tail = jnp.where(pl.program_id(0) > 0, prev_ref[...], 0.0)
        o_ref[...] = jax.nn.silu(_conv_pre(x_ref[...], tail, w_ref[...], b_ref[...]))

    return pl.pallas_call(
        body, name="conv_fwd" + tag, grid=(s // tc,),
        in_specs=[pl.BlockSpec((tc, c), lambda i: (i, 0)),
                  pl.BlockSpec((HALO, c), lambda i: (jnp.maximum(i * per - 1, 0), 0)),
                  pl.BlockSpec((CONV_K, c), lambda i: (0, 0)), pl.BlockSpec((1, c), lambda i: (0, 0))],
        out_specs=pl.BlockSpec((tc, c), lambda i: (i, 0)),
        out_shape=jax.ShapeDtypeStruct((s, c), F32),
        compiler_params=_params(("arbitrary",)),
    )(xbc, xbc, w, b)


def _conv_bwd(tag, xbc, dact, w, b, tc=512):
    s, c = xbc.shape
    tc = min(tc, s)
    per = tc // HALO
    nt = s // tc
    last_blk = s // HALO - 1

    def body(x_ref, prev_ref, next_ref, d_ref, dnext_ref, w_ref, b_ref, dx_ref, dw_ref, db_ref):
        i = pl.program_id(0)
        x = x_ref[...]
        wv = w_ref[...]
        tail = jnp.where(i > 0, prev_ref[...], 0.0)
        dpre = d_ref[...] * _dsilu(_conv_pre(x, tail, wv, b_ref[...]))
        pre_n = _conv_pre(next_ref[...], x[tc - HALO:], wv, b_ref[...])
        dpre_n = jnp.where(i < nt - 1, dnext_ref[...] * _dsilu(pre_n), 0.0)
        dx = wv[CONV_K - 1:CONV_K] * dpre
        for j in range(1, CONV_K):
            dx = dx + wv[CONV_K - 1 - j:CONV_K - j] * _shift_up(dpre, dpre_n, j)
        dx_ref[...] = dx
        dws = [jnp.sum(dpre * _shift_down(x, tail, CONV_K - 1 - k), axis=0, keepdims=True) for k in range(CONV_K)]
        dwv = jnp.concatenate(dws, axis=0)
        dbv = jnp.sum(dpre, axis=0, keepdims=True)

        @pl.when(i == 0)
        def _():
            dw_ref[...] = dwv
            db_ref[...] = dbv

        @pl.when(i > 0)
        def _():
            dw_ref[...] += dwv
            db_ref[...] += dbv

    tile = pl.BlockSpec((tc, c), lambda i: (i, 0))
    prev = pl.BlockSpec((HALO, c), lambda i: (jnp.maximum(i * per - 1, 0), 0))
    nxt = pl.BlockSpec((HALO, c), lambda i: (jnp.minimum((i + 1) * per, last_blk), 0))
    return pl.pallas_call(
        body, name="conv_bwd" + tag, grid=(nt,),
        in_specs=[tile, prev, nxt, tile, nxt, pl.BlockSpec((CONV_K, c), lambda i: (0, 0)),
                  pl.BlockSpec((1, c), lambda i: (0, 0))],
        out_specs=[tile, pl.BlockSpec((CONV_K, c), lambda i: (0, 0)), pl.BlockSpec((1, c), lambda i: (0, 0))],
        out_shape=[jax.ShapeDtypeStruct((s, c), F32), jax.ShapeDtypeStruct((CONV_K, c), F32),
                   jax.ShapeDtypeStruct((1, c), F32)],
        compiler_params=_params(("arbitrary",)),
    )(xbc, xbc, xbc, dact, dact, w, b)


def _ssd_chunk(xs, bm, cm, dtr, z, dt_bias, a_log, d_skip, g, s_prev):
    n = CHUNK
    row = lax.broadcasted_iota(jnp.int32, (n, n), 0)
    col = lax.broadcasted_iota(jnp.int32, (n, n), 1)
    causal = row >= col
    dt = jax.nn.softplus(dtr + dt_bias)
    a_c = dt * (-jnp.exp(a_log))
    hi = lax.Precision.HIGHEST
    a_cum = jnp.dot(causal.astype(F32), a_c, precision=hi, preferred_element_type=F32)
    a_cum_t = lax.dot_general(a_c, (row <= col).astype(F32), (_TN, ((), ())), precision=hi,
                              preferred_element_type=F32)
    p, st = SSD_HEAD_DIM, SSD_STATE
    heads = range(SSD_HEADS)
    grp = [h // (SSD_HEADS // SSD_GROUPS) for h in heads]
    bgs = [bm[:, k * st:(k + 1) * st] for k in range(SSD_GROUPS)]
    cgs = [cm[:, k * st:(k + 1) * st] for k in range(SSD_GROUPS)]
    cb = [mm_nt(cgs[k], bgs[k]) for k in range(SSD_GROUPS)]
    acols = [a_cum[:, h:h + 1] for h in heads]
    a_lasts = [a_cum[n - 1:n, h:h + 1] for h in heads]
    xhs = [xs[:, h * p:(h + 1) * p] for h in heads]
    sps = [s_prev[h * p:(h + 1) * p, :] for h in heads]
    xdts = [xhs[h] * dt[:, h:h + 1] for h in heads]
    decays = [jnp.exp(jnp.where(causal, acols[h] - a_cum_t[h:h + 1, :], -jnp.inf)) for h in heads]
    y_offs = [mm_nt(cgs[grp[h]], sps[h]) for h in heads]
    y_diags = [mm_nn(cb[grp[h]] * decays[h], xdts[h]) for h in heads]
    states = [mm_tn(xdts[h] * jnp.exp(a_lasts[h] - acols[h]), bgs[grp[h]]) for h in heads]
    s_new = [sps[h] * jnp.exp(a_lasts[h]) + states[h] for h in heads]
    ys = [y_diags[h] + y_offs[h] * jnp.exp(acols[h]) + d_skip[:, h:h + 1] * xhs[h] for h in heads]
    y = jnp.concatenate(ys, axis=1) * jax.nn.silu(z)
    return _rms(y, g), jnp.concatenate(s_new, axis=0)


def _split_xbc(t):
    return t[:, :SSD_DIM], t[:, SSD_DIM:SSD_DIM + GN], t[:, SSD_DIM + GN:]


def _ssd_fwd(tag, xact, dtr, z, dt_bias, a_log, d_skip, g, exchange=None):
    s = xact.shape[0]
    nc = s // CHUNK
    srows = SSD_HEADS * SSD_HEAD_DIM
    n_xi = len(exchange.inputs) if exchange else 0

    def body(*refs):
        x_ref, dt_ref, z_ref, b_ref, al_ref, ds_ref, g_ref = refs[:7]
        y_ref, st_ref = refs[7 + n_xi:9 + n_xi]
        state = refs[-3] if exchange else refs[-1]
        if exchange:
            mine = exchange.pick(refs, 7, 2)
            exchange.before_work(mine, pl.program_id(0) == 0, pl.program_id(0) == (3 * nc) // 4)

        @pl.when(pl.program_id(0) == 0)
        def _():
            state[...] = jnp.zeros_like(state)

        sp = state[...]
        st_ref[0] = sp
        xs, bm, cm = _split_xbc(x_ref[...])
        y, sn = _ssd_chunk(xs, bm, cm, dt_ref[...][:, :SSD_HEADS], z_ref[...], b_ref[...], al_ref[...],
                           ds_ref[...], g_ref[...], sp)
        y_ref[...] = y
        state[...] = sn
        if exchange:
            exchange.after_work(mine, pl.program_id(0) == nc - 1, mid_done=True)

    small = pl.BlockSpec((1, SSD_HEADS), lambda i: (0, 0))
    in_specs = [pl.BlockSpec((CHUNK, CONV_DIM), lambda i: (i, 0)), pl.BlockSpec((CHUNK, DT_PAD), lambda i: (i, 0)),
                pl.BlockSpec((CHUNK, SSD_DIM), lambda i: (i, 0)), small, small, small,
                pl.BlockSpec((1, SSD_DIM), lambda i: (0, 0))]
    out_specs = [pl.BlockSpec((CHUNK, SSD_DIM), lambda i: (i, 0)),
                 pl.BlockSpec((1, srows, SSD_STATE), lambda i: (i, 0, 0))]
    out_shape = [jax.ShapeDtypeStruct((s, SSD_DIM), F32), jax.ShapeDtypeStruct((nc, srows, SSD_STATE), F32)]
    args, scratch = [xact, dtr, z, dt_bias, a_log, d_skip, g], [pltpu.VMEM((srows, SSD_STATE), F32)]
    if exchange:
        x_in, x_out, sems = exchange.specs()
        in_specs, out_specs, scratch = in_specs + x_in, out_specs + x_out, scratch + sems
        out_shape, args = out_shape + exchange.outputs, args + exchange.inputs
    return pl.pallas_call(
        body, name="ssd_fwd" + tag, grid=(nc,), in_specs=in_specs, out_specs=out_specs, out_shape=out_shape,
        scratch_shapes=scratch, compiler_params=_params(("arbitrary",)),
    )(*args)


def _ssd_bwd(tag, xact, dtr, z, states, dy, dt_bias, a_log, d_skip, g, exchange=None):
    s = xact.shape[0]
    nc = s // CHUNK
    srows = SSD_HEADS * SSD_HEAD_DIM
    n_xi = len(exchange.inputs) if exchange else 0

    def body(*refs):
        x_ref, dt_ref, z_ref, sp_ref, dy_ref, b_ref, al_ref, ds_ref, g_ref = refs[:9]
        dx_ref, ddt_ref, dz_ref, db_ref, dal_ref, dds_ref, dg_ref = refs[9 + n_xi:16 + n_xi]
        dstate = refs[-3] if exchange else refs[-1]
        first = pl.program_id(0) == 0
        if exchange:
            mine = exchange.pick(refs, 9, 7)
            exchange.before_work(mine, first, pl.program_id(0) == (3 * nc) // 4)

        @pl.when(first)
        def _():
            dstate[...] = jnp.zeros_like(dstate)

        xs, bm, cm = _split_xbc(x_ref[...])
        _, vjp = jax.vjp(_ssd_chunk, xs, bm, cm, dt_ref[...][:, :SSD_HEADS], z_ref[...], b_ref[...], al_ref[...],
                         ds_ref[...], g_ref[...], sp_ref[0])
        dxs, dbm, dcm, ddt, dz, db, dal, dds, dg, dsp = vjp((dy_ref[...], dstate[...]))
        dx_ref[...] = jnp.concatenate([dxs, dbm, dcm], axis=1)
        ddt_ref[...] = jnp.concatenate([ddt, jnp.zeros((CHUNK, DT_PAD - SSD_HEADS), F32)], axis=1)
        dz_ref[...] = dz
        dstate[...] = dsp

        @pl.when(first)
        def _():
            db_ref[...] = db
            dal_ref[...] = dal
            dds_ref[...] = dds
            dg_ref[...] = dg

        @pl.when(jnp.logical_not(first))
        def _():
            db_ref[...] += db
            dal_ref[...] += dal
            dds_ref[...] += dds
            dg_ref[...] += dg

        if exchange:
            exchange.after_work(mine, pl.program_id(0) == nc - 1, mid_done=True)

    def rev(c):
        return lambda i: (nc - 1 - i, 0)

    small = pl.BlockSpec((1, SSD_HEADS), lambda i: (0, 0))
    gspec = pl.BlockSpec((1, SSD_DIM), lambda i: (0, 0))
    in_specs = [pl.BlockSpec((CHUNK, CONV_DIM), rev(0)), pl.BlockSpec((CHUNK, DT_PAD), rev(0)),
                pl.BlockSpec((CHUNK, SSD_DIM), rev(0)),
                pl.BlockSpec((1, srows, SSD_STATE), lambda i: (nc - 1 - i, 0, 0)),
                pl.BlockSpec((CHUNK, SSD_DIM), rev(0)), small, small, small, gspec]
    out_specs = [pl.BlockSpec((CHUNK, CONV_DIM), rev(0)), pl.BlockSpec((CHUNK, DT_PAD), rev(0)),
                 pl.BlockSpec((CHUNK, SSD_DIM), rev(0)), small, small, small, gspec]
    out_shape = [jax.ShapeDtypeStruct((s, CONV_DIM), F32), jax.ShapeDtypeStruct((s, DT_PAD), F32),
                 jax.ShapeDtypeStruct((s, SSD_DIM), F32), jax.ShapeDtypeStruct((1, SSD_HEADS), F32),
                 jax.ShapeDtypeStruct((1, SSD_HEADS), F32), jax.ShapeDtypeStruct((1, SSD_HEADS), F32),
                 jax.ShapeDtypeStruct((1, SSD_DIM), F32)]
    args, scratch = [xact, dtr, z, states, dy, dt_bias, a_log, d_skip, g], [pltpu.VMEM((srows, SSD_STATE), F32)]
    if exchange:
        x_in, x_out, sems = exchange.specs()
        in_specs, out_specs, scratch = in_specs + x_in, out_specs + x_out, scratch + sems
        out_shape, args = out_shape + exchange.outputs, args + exchange.inputs
    return pl.pallas_call(
        body, name="ssd_bwd" + tag, grid=(nc,), in_specs=in_specs, out_specs=out_specs, out_shape=out_shape,
        scratch_shapes=scratch, compiler_params=_params(("arbitrary",)),
    )(*args)


TQ = 128
TK = 128
SB_SCALE = 1.0 / math.sqrt(SB_HEAD_DIM)


def _split2(x):
    hi = x.astype(BF16)
    return hi, (x - hi.astype(F32)).astype(BF16)


TK_NEAR = 384
SB_UNDERFLOW = -110.0


def _sb_logits(qhs, kb, t0, s0, masked):
    zs = [_dg(qh, kb, _NT) for qh in qhs]
    mask = None
    if masked:
        t_pos = t0 + lax.broadcasted_iota(jnp.int32, zs[0].shape, 0)
        s_pos = s0 + lax.broadcasted_iota(jnp.int32, zs[0].shape, 1)
        mask = s_pos < t_pos
    lbs = [jnp.minimum(z, 0.0) - jnp.log(1.0 + jnp.exp(-jnp.abs(z))) for z in zs]
    lss = [lb - z for lb, z in zip(lbs, zs)]
    if masked:
        lss = [jnp.where(mask, ls, 0.0) for ls in lss]
    return lbs, lss, mask


def _running_sums(xs, starts, u, reverse, two_terms=True):
    nsub = xs[0].shape[1] // TK
    order = list(reversed(range(nsub))) if reverse else list(range(nsub))
    chunks = [[x[:, c * TK:(c + 1) * TK] for c in range(nsub)] for x in xs]
    sums = [[(_lane_sums(xc, u) if two_terms else _dg(xc, u, _NN)) for xc in row] for row in chunks]
    out = []
    for row, srow, run in zip(chunks, sums, starts):
        parts = [None] * nsub
        for c in order:
            parts[c] = run + srow[c]
            run = run + jnp.sum(row[c], axis=1, keepdims=True)
        out.append((parts[0] if nsub == 1 else jnp.concatenate(parts, axis=1), run))
    return out


def _lane_sums(x, u):
    hi, lo = _split2(x)
    return _dg(hi, u, _NN) + _dg(lo, u, _NN)


def _tri(cmp):
    j = lax.broadcasted_iota(jnp.int32, (TK, TK), 0)
    s = lax.broadcasted_iota(jnp.int32, (TK, TK), 1)
    return cmp(j, s).astype(BF16)


def _sb_fwd(tag, q, k, v, exchange=None):
    s = q.shape[0]
    npair = SB_DIM // LANES
    nq = s // TQ
    near = min(TK_NEAR, s)
    n_xi = len(exchange.inputs) if exchange else 0

    def body(*refs):
        q_ref, k_ref, v_ref = refs[:3]
        o_ref, t_ref = refs[3 + n_xi:5 + n_xi]
        tq = pl.program_id(1)
        if exchange:
            pair = pl.program_id(0)
            mine = exchange.pick(refs, 3, 2)
            exchange.before_work(mine, jnp.logical_and(pair == 0, tq == 0),
                                 jnp.logical_and(pair == npair - 1, tq == 0))
        qp = q_ref[...]
        lane = lax.broadcasted_iota(jnp.int32, (1, LANES), 1)
        u_gt = _tri(lambda j, s: j > s)
        heads = range(LANES // SB_HEAD_DIM)
        hms = [(lane // SB_HEAD_DIM) == hh for hh in heads]
        qhs = [jnp.where(hm, qp, jnp.zeros_like(qp)) * SB_SCALE for hm in hms]

        def block(wb, width, carry, masked):
            off = pl.multiple_of(wb * TK, TK)
            kb = k_ref[pl.ds(off, width), :]
            vb = v_ref[pl.ds(off, width), :]
            lbs, lss, mask = _sb_logits(qhs, kb, tq * TQ, wb * TK, masked)
            sums = _running_sums(lss, [c[0] for c in carry], u_gt, reverse=True)
            ws = [jnp.exp(lb + later) for lb, (later, _) in zip(lbs, sums)]
            if masked:
                ws = [jnp.where(mask, w, 0.0) for w in ws]
            pvs = [_dg(w, vb, _NN) for w in ws]
            return tuple((r, c[1] + pv) for (_, r), c, pv in zip(sums, carry, pvs))

        def more(c):
            alive = jnp.max(c[1][0][0])
            for hh in heads[1:]:
                alive = jnp.maximum(alive, jnp.max(c[1][hh][0]))
            return jnp.logical_and(c[0] >= 0, alive > SB_UNDERFLOW)

        start = tuple((jnp.zeros((TQ, 1), F32), jnp.zeros((TQ, LANES), F32)) for _ in heads)
        near_blk = jnp.maximum(tq - (near // TK - 1), 0)
        wb, done = lax.while_loop(more, lambda c: (c[0] - 1, block(c[0], TK, c[1], False)),
                                  (near_blk - 1, block(near_blk, near, start, True)))
        first = (wb + 1).astype(F32)
        out = jnp.zeros((TQ, LANES), F32)
        tot = jnp.zeros((TQ, LANES), F32)
        for hh in heads:
            r, acc = done[hh]
            out = out + jnp.where(hms[hh], acc, 0.0)
            tot = tot + jnp.where(hms[hh], jnp.where(lane % SB_HEAD_DIM == 1, first, r), 0.0)
        o_ref[...] = out
        t_ref[...] = tot
        if exchange:
            exchange.after_work(mine, jnp.logical_and(pair == npair - 1, tq == nq - 1), mid_done=True)

    tile = pl.BlockSpec((TQ, LANES), lambda p, t: (t, p))
    full = pl.BlockSpec((s, LANES), lambda p, t: (0, p))
    in_specs, out_specs = [tile, full, full], [tile, tile]
    out_shape = [jax.ShapeDtypeStruct((s, SB_DIM), F32)] * 2
    args, scratch = [q, k, v], []
    if exchange:
        x_in, x_out, scratch = exchange.specs()
        in_specs, out_specs = in_specs + x_in, out_specs + x_out
        out_shape, args = out_shape + exchange.outputs, args + exchange.inputs
    return pl.pallas_call(
        body, name="sb_fwd" + tag, grid=(npair, nq), in_specs=in_specs, out_specs=out_specs, out_shape=out_shape,
        scratch_shapes=scratch, compiler_params=_params(("arbitrary", "arbitrary")),
    )(*args)


def _sb_bwd(tag, q, k, v, tot, do, exchange=None):
    s = q.shape[0]
    npair = SB_DIM // LANES
    nq = s // TQ
    near = min(TK_NEAR, s)
    n_xi = len(exchange.inputs) if exchange else 0

    def body(*refs):
        q_ref, k_ref, v_ref, t_ref, do_ref = refs[:5]
        dq_ref, dk_ref, dv_ref = refs[5 + n_xi:8 + n_xi]
        tq = pl.program_id(1)
        if exchange:
            pair = pl.program_id(0)
            mine = exchange.pick(refs, 5, 3)
            exchange.before_work(mine, jnp.logical_and(pair == 0, tq == 0),
                                 jnp.logical_and(pair == npair - 1, tq == 0))

        @pl.when(tq == 0)
        def _():
            dk_ref[...] = jnp.zeros_like(dk_ref)
            dv_ref[...] = jnp.zeros_like(dv_ref)

        qp = q_ref[...]
        dop = do_ref[...]
        totp = t_ref[...]
        lane = lax.broadcasted_iota(jnp.int32, (1, LANES), 1)
        u_le = _tri(lambda j, s: j <= s)
        u_lt = _tri(lambda j, s: j < s)
        heads = range(LANES // SB_HEAD_DIM)
        hms = [(lane // SB_HEAD_DIM) == hh for hh in heads]
        qhs = [jnp.where(hm, qp, jnp.zeros_like(qp)) * SB_SCALE for hm in hms]
        dohs = [jnp.where(hm, dop, 0.0).astype(BF16) for hm in hms]
        totals = [jnp.sum(jnp.where(lane == hh * SB_HEAD_DIM, totp, 0.0), axis=1, keepdims=True) for hh in heads]
        first = jnp.max(jnp.where(lane == 1, totp, 0.0)).astype(jnp.int32)

        def block(wb, width, carry, masked):
            off = pl.multiple_of(wb * TK, TK)
            kb = k_ref[pl.ds(off, width), :]
            vb = v_ref[pl.ds(off, width), :]
            lbs, lss, mask = _sb_logits(qhs, kb, tq * TQ, wb * TK, masked)
            dws = [_dg(doh, vb, _NT) for doh in dohs]
            pres = _running_sums(lss, [c[0] for c in carry], u_le, reverse=False)
            ws = [jnp.exp(lb + (total - before)) for lb, total, (before, _) in zip(lbs, totals, pres)]
            if masked:
                ws = [jnp.where(mask, w, 0.0) for w in ws]
            gs = [w * dw for w, dw in zip(ws, dws)]
            lefts = _running_sums(gs, [c[1] for c in carry], u_lt, reverse=False, two_terms=False)
            dzs = [g - jnp.exp(lb) * (g + g_left) for g, lb, (g_left, _) in zip(gs, lbs, lefts)]
            if masked:
                dzs = [jnp.where(mask, dz, 0.0) for dz in dzs]
            dzbs = [dz.astype(BF16) for dz in dzs]
            dks = [_dg(dzb, qh, _TN) for dzb, qh in zip(dzbs, qhs)]
            dvs = [_dg(w, doh, _TN) for w, doh in zip(ws, dohs)]
            dqs = [_dg(dzb, kb, _NN) for dzb in dzbs]
            dk_ref[pl.ds(off, width), :] += functools.reduce(jnp.add, dks)
            dv_ref[pl.ds(off, width), :] += functools.reduce(jnp.add, dvs)
            return tuple((pre, gc, c[2] + dq) for (_, pre), (_, gc), c, dq in zip(pres, lefts, carry, dqs))

        zero = jnp.zeros((TQ, 1), F32)
        start = tuple((zero, zero, jnp.zeros((TQ, LANES), F32)) for _ in heads)
        near_blk = jnp.maximum(tq - (near // TK - 1), 0)
        far = lax.fori_loop(first, near_blk, lambda j, c: block(j, TK, c, False), start)
        done = block(near_blk, near, far, True)
        dq = jnp.zeros((TQ, LANES), F32)
        for hh in heads:
            dq = dq + jnp.where(hms[hh], done[hh][2], 0.0)
        dq_ref[...] = dq * SB_SCALE
        if exchange:
            exchange.after_work(mine, jnp.logical_and(pair == npair - 1, tq == nq - 1), mid_done=True)

    tile = pl.BlockSpec((TQ, LANES), lambda p, t: (t, p))
    full = pl.BlockSpec((s, LANES), lambda p, t: (0, p))
    in_specs, out_specs = [tile, full, full, tile, tile], [tile, full, full]
    out_shape = [jax.ShapeDtypeStruct((s, SB_DIM), F32)] * 3
    args, scratch = [q, k, v, tot, do], []
    if exchange:
        x_in, x_out, scratch = exchange.specs()
        in_specs, out_specs = in_specs + x_in, out_specs + x_out
        out_shape, args = out_shape + exchange.outputs, args + exchange.inputs
    return pl.pallas_call(
        body, name="sb_bwd" + tag, grid=(npair, nq), in_specs=in_specs, out_specs=out_specs, out_shape=out_shape,
        scratch_shapes=scratch, compiler_params=_params(("arbitrary", "arbitrary")),
    )(*args)


def _out_tile(y_ssd, o, sb_g, w_out):
    y_all = jnp.concatenate([y_ssd, _rms(o, sb_g)], axis=1)
    return mm_nn(y_all, w_out)


def _out_fwd(tag, h, y_ssd, o, sb_g, w_out):
    return _rowcall("out_fwd" + tag, lambda h, y, o, g, w: (h + _out_tile(y, o, g, w),),
                    [h, y_ssd, o], [sb_g, w_out], [(D_MODEL, F32)], tm=512)[0]


def _out_bwd(tag, y_ssd, o, dh, sb_g, w_out):
    def fn(y, o, dh, g, w):
        _, vjp = jax.vjp(_out_tile, y, o, g, w.astype(F32))
        return vjp(dh)

    return _rowcall("out_bwd" + tag, fn, [y_ssd, o, dh], [sb_g, w_out], [(SSD_DIM, F32), (SB_DIM, F32)],
                    [((1, SB_DIM), F32), ((D_MODEL, D_MODEL), F32)], tm=256)


def _mem_tile(mem, g, w_k, w_v):
    m = _rms(mem, g)
    return mm_nn(m, w_k), mm_nn(m, w_v)


def _mem_fwd(tag, mem, g, w_k, w_v):
    return _rowcall("mem_fwd" + tag, _mem_tile, [mem], [g, w_k, w_v], [(XA_DIM, F32), (XA_DIM, F32)], tm=256)


def _mem_bwd(tag, mem, dkx, dvx, g, w_k, w_v):
    def fn(mem, dkx, dvx, g, w_k, w_v):
        _, vjp = jax.vjp(lambda g, a, b: _mem_tile(mem, g, a, b), g, w_k.astype(F32), w_v.astype(F32))
        return vjp((dkx, dvx))

    return _rowcall("mem_bwd" + tag, fn, [mem, dkx, dvx], [g, w_k, w_v], [],
                    [((1, D_MODEL), F32), ((D_MODEL, XA_DIM), F32), ((D_MODEL, XA_DIM), F32)], tm=256)


def _xattn_tile(h, g, w_q, kx, vx, w_o):
    q = mm_nn(_rms(h, g), w_q)
    scale = 1.0 / math.sqrt(XA_HEAD_DIM)
    outs = []
    for i in range(XA_HEADS):
        sl = slice(i * XA_HEAD_DIM, (i + 1) * XA_HEAD_DIM)
        p = jax.nn.softmax(mm_nt(q[:, sl], kx[:, sl]) * scale, axis=-1)
        outs.append(mm_nn(p, vx[:, sl]))
    return mm_nn(jnp.concatenate(outs, axis=1), w_o)


def _xattn_fwd(tag, h, g, w_q, kx, vx, w_o):
    return _rowcall("xattn_fwd" + tag, lambda h, g, wq, kx, vx, wo: (h + _xattn_tile(h, g, wq, kx, vx, wo),),
                    [h], [g, w_q, kx, vx, w_o], [(D_MODEL, F32)], tm=512)[0]


def _xattn_bwd(tag, h, dh_out, g, w_q, kx, vx, w_o, exchange=None):
    def fn(h, dh_out, g, w_q, kx, vx, w_o):
        _, vjp = jax.vjp(_xattn_tile, h, g, w_q.astype(F32), kx, vx, w_o.astype(F32))
        dh, dg, dwq, dkx, dvx, dwo = vjp(dh_out)
        return dh_out + dh, dg, dwq, dkx, dvx, dwo

    mlen = kx.shape[0]
    return _rowcall("xattn_bwd" + tag, fn, [h, dh_out], [g, w_q, kx, vx, w_o], [(D_MODEL, F32)],
                    [((1, D_MODEL), F32), ((D_MODEL, XA_DIM), F32), ((mlen, XA_DIM), F32), ((mlen, XA_DIM), F32),
                     ((XA_DIM, D_MODEL), F32)], tm=256, exchange=exchange)


def _mlp_fwd(tag, h, g, w1, w2, exchange=None):
    def fn(h, g, w1, w2):
        r = jnp.maximum(mm_nn(_rms(h, g), w1), 0.0)
        return h + mm_nn(r * r, w2), r

    return _rowcall("mlp_fwd" + tag, fn, [h], [g, w1, w2], [(D_MODEL, F32), (D_FF, BF16)], tm=256,
                    exchange=exchange)


def _mlp_bwd(tag, h, relu, dh_out, g, w1, w2, exchange=None):
    def fn(h, relu, dh_out, g, w1, w2):
        hn, vjp = jax.vjp(_rms, h, g)
        r = relu.astype(F32)
        dob = dh_out.astype(BF16)
        dp = mm_nt(dob, w2) * (2.0 * r)
        dh, dg = vjp(mm_nt(dp, w1))
        return dh_out + dh, hn, dp, r * r, dob, dg

    return _rowcall("mlp_bwd" + tag, fn, [h, relu, dh_out], [g, w1, w2],
                    [(D_MODEL, F32), (D_MODEL, BF16), (D_FF, BF16), (D_FF, BF16), (D_MODEL, BF16)],
                    [((1, D_MODEL), F32)], tm=256, exchange=exchange)


def _head(h, g, target):
    def lossfn(h, g, t):
        err = jnp.square(_rms(h, g) - t)
        return 0.5 * jnp.sum(jnp.mean(err, axis=-1))

    def fn(h, t, g):
        loss, vjp = jax.vjp(lambda h, g: lossfn(h, g, t), h, g)
        dh, dg = vjp(jnp.ones((), F32))
        return dh, jnp.full((1, LANES), loss, F32), dg

    return _rowcall("head", fn, [h, target], [g], [(D_MODEL, F32)], [((1, LANES), F32), ((1, D_MODEL), F32)], tm=512)


def _row(v):
    return v.reshape(1, -1)


class LocalPlan:
    def __init__(self, mats):
        self.mats = mats

    def weights(self, l):
        return self.mats[l]

    def carried_by(self, kernel, l):
        return None

    def carried_out(self, kernel, l, outs):
        pass

    def mlp_grads_done(self, l, gm):
        pass

    def mixer_grads_done(self, l, gm):
        pass

    def grads_done(self, l, gm):
        pass


def _local_step(x, mem, target, sw, plan):
    h = x
    saved = []
    for l in range(DEPTH):
        tag = str(l)
        m = plan.weights(l)
        z, xbc, q, k, v, dtr, *carried = _proj_fwd(tag, h, _row(sw["norm_mix_g"][l]), m["w_in"],
                                                   exchange=plan.carried_by("proj_fwd", l))
        plan.carried_out("proj_fwd", l, carried)
        xact = _conv_fwd(tag, xbc, m["conv_w"], _row(sw["conv_b"][l]))
        y_ssd, states, *carried = _ssd_fwd(tag, xact, dtr, z, _row(sw["dt_bias"][l]), _row(sw["a_log"][l]),
                                           _row(sw["d_skip"][l]), _row(sw["ssd_norm_g"][l]),
                                           exchange=plan.carried_by("ssd_fwd", l))
        plan.carried_out("ssd_fwd", l, carried)
        o, sb_tot, *carried = _sb_fwd(tag, q, k, v, exchange=plan.carried_by("sb_fwd", l))
        plan.carried_out("sb_fwd", l, carried)
        h1 = _out_fwd(tag, h, y_ssd, o, _row(sw["sb_norm_g"][l]), m["w_out"])
        kx, vx = _mem_fwd(tag, mem, _row(sw["norm_mem_g"][l]), m["w_xk"], m["w_xv"])
        h2 = _xattn_fwd(tag, h1, _row(sw["norm_xa_g"][l]), m["w_xq"], kx, vx, m["w_xo"])
        h3, relu, *carried = _mlp_fwd(tag, h2, _row(sw["norm_ff_g"][l]), m["w_ff1"], m["w_ff2"],
                                      exchange=plan.carried_by("mlp_fwd", l))
        plan.carried_out("mlp_fwd", l, carried)
        saved.append((h, z, xbc, q, k, v, dtr, xact, y_ssd, states, o, sb_tot, h1, kx, vx, h2, relu))
        h = h3

    dh, loss, d_final = _head(h, _row(sw["final_g"]), target)
    gm = [dict() for _ in range(DEPTH)]
    gs = {name: [None] * DEPTH for name in ("norm_mix_g", "conv_w", "conv_b", "dt_bias", "a_log", "d_skip",
                                            "ssd_norm_g", "sb_norm_g", "norm_xa_g", "norm_mem_g", "norm_ff_g")}
    for l in reversed(range(DEPTH)):
        tag = str(l)
        m = plan.weights(l)
        h0, z, xbc, q, k, v, dtr, xact, y_ssd, states, o, sb_tot, h1, kx, vx, h2, relu = saved[l]
        dh2, hn_b, dp_b, a_b, do_b, gs["norm_ff_g"][l], *carried = _mlp_bwd(
            tag, h2, relu, dh, _row(sw["norm_ff_g"][l]), m["w_ff1"], m["w_ff2"],
            exchange=plan.carried_by("mlp_bwd", l))
        plan.carried_out("mlp_bwd", l, carried)
        gm[l]["w_ff1"] = _mm_tn_parts("dw_ff1" + tag, hn_b, dp_b, True)
        gm[l]["w_ff2"] = _mm_tn_parts("dw_ff2" + tag, a_b, do_b, False)
        plan.mlp_grads_done(l, gm[l])
        dh1, gs["norm_xa_g"][l], gm[l]["w_xq"], dkx, dvx, gm[l]["w_xo"], *carried = _xattn_bwd(
            tag, h1, dh2, _row(sw["norm_xa_g"][l]), m["w_xq"], kx, vx, m["w_xo"],
            exchange=plan.carried_by("xattn_bwd", l))
        plan.carried_out("xattn_bwd", l, carried)
        gs["norm_mem_g"][l], gm[l]["w_xk"], gm[l]["w_xv"] = _mem_bwd(
            tag, mem, dkx, dvx, _row(sw["norm_mem_g"][l]), m["w_xk"], m["w_xv"])
        dy_ssd, do, gs["sb_norm_g"][l], gm[l]["w_out"] = _out_bwd(
            tag, y_ssd, o, dh1, _row(sw["sb_norm_g"][l]), m["w_out"])
        plan.mixer_grads_done(l, gm[l])
        dq, dk, dv, *carried = _sb_bwd(tag, q, k, v, sb_tot, do, exchange=plan.carried_by("sb_bwd", l))
        plan.carried_out("sb_bwd", l, carried)
        dxact, ddtr, dz, gs["dt_bias"][l], gs["a_log"][l], gs["d_skip"][l], gs["ssd_norm_g"][l], *carried = _ssd_bwd(
            tag, xact, dtr, z, states, dy_ssd, _row(sw["dt_bias"][l]), _row(sw["a_log"][l]),
            _row(sw["d_skip"][l]), _row(sw["ssd_norm_g"][l]), exchange=plan.carried_by("ssd_bwd", l))
        plan.carried_out("ssd_bwd", l, carried)
        dxbc, gs["conv_w"][l], gs["conv_b"][l] = _conv_bwd(tag, xbc, dxact, m["conv_w"], _row(sw["conv_b"][l]))
        dh, hn_b, dp_b, gs["norm_mix_g"][l] = _proj_bwd(
            tag, h0, dh1, dz, dxbc, dq, dk, dv, ddtr, _row(sw["norm_mix_g"][l]), m["w_in"])
        gm[l]["w_in"] = _mm_tn_call("dw_in" + tag, hn_b, dp_b, 512, IN_PAD, 1024)
        plan.grads_done(l, gm[l])
    gs["final_g"] = d_final
    return loss, dh, gm, gs


ANY = pl.BlockSpec(memory_space=pl.ANY)
VMEM_SPEC = pl.BlockSpec(memory_space=pltpu.VMEM)


def _place():
    return lax.axis_index("x"), lax.axis_index("y"), lax.axis_index("c")


def _other_chips(x, y):
    return [(1 - x, y), (x, 1 - y), (1 - x, 1 - y)]


def _remote(send_sems, recv_sems, idx, src, dst, to):
    return pltpu.make_async_remote_copy(src_ref=src, dst_ref=dst, send_sem=send_sems.at[idx],
                                        recv_sem=recv_sems.at[idx], device_id=to, device_id_type=MESH)


def _run_exchange(name, ex):
    def body(*refs):
        mine = ex.pick(refs, 0, 0)
        ex.start(*mine)
        if ex.mid is not None:
            ex.mid(*mine)
        ex.end(*mine)

    x_in, x_out, scratch = ex.specs()
    return pl.pallas_call(body, name=name, in_specs=x_in, out_specs=x_out, out_shape=ex.outputs,
                          scratch_shapes=scratch)(*ex.inputs)


def _gather_exchange(layer, shards):
    n = len(shards)
    outs = [jax.ShapeDtypeStruct((N_CHIPS,) + s.shape[1:], s.dtype) for s in shards]

    def start(w_refs, o_refs, ss, rs):
        x, y, c = _place()

        @pl.when(c == layer)
        def _():
            for i in range(n):
                for kk, (cx, cy) in enumerate(_other_chips(x, y)):
                    _remote(ss, rs, 6 * i + kk, w_refs[i].at[layer], o_refs[i].at[2 * x + y], (cx, cy, layer)).start()

    def mid(w_refs, o_refs, ss, rs):
        x, y, c = _place()

        @pl.when(c == layer)
        def _():
            for i in range(n):
                for kk, (cx, cy) in enumerate(_other_chips(x, y)):
                    got = o_refs[i].at[2 * cx + cy]
                    _remote(ss, rs, 6 * i + kk, got, got, (x, y, c)).wait_recv()
                    _remote(ss, rs, 6 * i + 3 + kk, got, got, (x, y, 1 - layer)).start()

    def end(w_refs, o_refs, ss, rs):
        x, y, c = _place()
        for i in range(n):
            for kk, (cx, cy) in enumerate(_other_chips(x, y)):
                got = o_refs[i].at[2 * cx + cy]

                @pl.when(c == layer)
                def _():
                    _remote(ss, rs, 6 * i + kk, w_refs[i].at[layer], got, (x, y, c)).wait_send()
                    _remote(ss, rs, 6 * i + 3 + kk, got, got, (x, y, c)).wait_send()

                @pl.when(c != layer)
                def _():
                    _remote(ss, rs, 6 * i + 3 + kk, got, got, (x, y, c)).wait_recv()

    return Exchange(shards, outs, 6 * n, start, end, mid)


def _handover_exchange(layer, grads):
    n = len(grads)
    outs = [jax.ShapeDtypeStruct(g.shape, g.dtype) for g in grads]

    def start(g_refs, o_refs, ss, rs):
        x, y, c = _place()

        @pl.when(c != layer)
        def _():
            for i in range(n):
                _remote(ss, rs, i, g_refs[i], o_refs[i], (x, y, layer)).start()

    def end(g_refs, o_refs, ss, rs):
        x, y, c = _place()
        for i in range(n):
            @pl.when(c != layer)
            def _():
                _remote(ss, rs, i, g_refs[i], o_refs[i], (x, y, c)).wait_send()

            @pl.when(c == layer)
            def _():
                _remote(ss, rs, i, g_refs[i], o_refs[i], (x, y, c)).wait_recv()

    return Exchange(grads, outs, n, start, end)


def _scatter_exchange(layer, parts):
    n = len(parts)
    outs = [jax.ShapeDtypeStruct(p.shape, p.dtype) for p in parts]

    def start(s_refs, o_refs, ss, rs):
        x, y, c = _place()

        @pl.when(c == layer)
        def _():
            for i in range(n):
                for kk, (cx, cy) in enumerate(_other_chips(x, y)):
                    _remote(ss, rs, 3 * i + kk, s_refs[i].at[2 * cx + cy], o_refs[i].at[2 * x + y],
                            (cx, cy, layer)).start()

    def end(s_refs, o_refs, ss, rs):
        x, y, c = _place()

        @pl.when(c == layer)
        def _():
            for i in range(n):
                for kk, (cx, cy) in enumerate(_other_chips(x, y)):
                    got = o_refs[i].at[2 * cx + cy]
                    _remote(ss, rs, 3 * i + kk, got, got, (x, y, c)).wait_recv()
            for i in range(n):
                for kk, (cx, cy) in enumerate(_other_chips(x, y)):
                    _remote(ss, rs, 3 * i + kk, s_refs[i].at[2 * cx + cy], o_refs[i].at[2 * x + y],
                            (x, y, c)).wait_send()

    return Exchange(parts, outs, 3 * n, start, end)


def _return_exchange(reduced):
    flat = [g for layer in range(DEPTH) for g in reduced[layer]]
    n = len(reduced[0])
    outs = [jax.ShapeDtypeStruct(g.shape, g.dtype) for g in flat]

    def start(g_refs, o_refs, ss, rs):
        x, y, c = _place()
        for layer in range(DEPTH):
            @pl.when(c == layer)
            def _():
                for i in range(n):
                    k = layer * n + i
                    _remote(ss, rs, k, g_refs[k], o_refs[k], (x, y, 1 - layer)).start()

    def end(g_refs, o_refs, ss, rs):
        x, y, c = _place()
        for layer in range(DEPTH):
            for i in range(n):
                k = layer * n + i

                @pl.when(c == layer)
                def _():
                    _remote(ss, rs, k, g_refs[k], o_refs[k], (x, y, c)).wait_send()

                @pl.when(c != layer)
                def _():
                    _remote(ss, rs, k, g_refs[k], o_refs[k], (x, y, c)).wait_recv()

    return Exchange(flat, outs, DEPTH * n, start, end)


def _gather_small(tag, buf):
    shape = buf.shape

    def body(b_ref, o_ref, sum_ref, send_sems, recv_sems, local_sem):
        x, y, c = _place()
        me = 4 * x + 2 * y + c
        mine = pltpu.make_async_copy(b_ref, o_ref.at[me], local_sem)
        mine.start()
        flips = [(dx, dy, dc) for dx in (0, 1) for dy in (0, 1) for dc in (0, 1) if (dx, dy, dc) != (0, 0, 0)]
        sends = []

        def peer(dx, dy, dc):
            return (1 - x if dx else x, 1 - y if dy else y, 1 - c if dc else c)

        for kk, flip in enumerate(flips):
            cp = pltpu.make_async_remote_copy(src_ref=b_ref, dst_ref=o_ref.at[me], send_sem=send_sems.at[kk],
                                              recv_sem=recv_sems.at[kk], device_id=peer(*flip), device_id_type=MESH)
            cp.start()
            sends.append(cp)
        for kk, flip in enumerate(flips):
            px, py, pc = peer(*flip)
            frm = 4 * px + 2 * py + pc
            pltpu.make_async_remote_copy(src_ref=b_ref, dst_ref=o_ref.at[frm], send_sem=send_sems.at[kk],
                                         recv_sem=recv_sems.at[kk], device_id=(x, y, c),
                                         device_id_type=MESH).wait_recv()
        for cp in sends:
            cp.wait_send()
        mine.wait()
        total = o_ref[0]
        for d in range(1, N_DEV):
            total = total + o_ref[d]
        sum_ref[...] = total

    return pl.pallas_call(
        body, name="gather_small" + tag, in_specs=[VMEM_SPEC], out_specs=[VMEM_SPEC, VMEM_SPEC],
        out_shape=[jax.ShapeDtypeStruct((N_DEV,) + shape, buf.dtype), jax.ShapeDtypeStruct(shape, buf.dtype)],
        scratch_shapes=[pltpu.SemaphoreType.DMA((N_DEV - 1,)), pltpu.SemaphoreType.DMA((N_DEV - 1,)),
                        pltpu.SemaphoreType.DMA],
    )(buf)


def _add_handed(tag, layer, g, r, tr=256):
    _, rows, cols = g.shape
    tr = min(tr, rows)

    def body(g_ref, r_ref, o_ref):
        @pl.when(lax.axis_index("c") == layer)
        def _():
            o_ref[...] = (g_ref[...] + r_ref[...]).astype(o_ref.dtype)

    spec = pl.BlockSpec((1, tr, cols), lambda p, i: (p, i, 0))
    return pl.pallas_call(
        body, name="add_handed_" + tag, grid=(N_CHIPS, rows // tr), in_specs=[spec, spec], out_specs=spec,
        out_shape=jax.ShapeDtypeStruct(g.shape, BF16), compiler_params=_params(("arbitrary", "arbitrary")),
    )(g, r)


def _sum_parts(tag, layer, own, parts, tr=256):
    _, rows, cols = parts.shape
    tr = min(tr, rows)
    chip = (2 * lax.axis_index("x") + lax.axis_index("y")).astype(jnp.int32).reshape(1)

    def body(c_ref, own_ref, p1_ref, p2_ref, p3_ref, o_ref):
        @pl.when(lax.axis_index("c") == layer)
        def _():
            total = own_ref[0].astype(F32)
            for p_ref in (p1_ref, p2_ref, p3_ref):
                total = total + p_ref[0].astype(F32)
            o_ref[...] = total

    def after(kk):
        return pl.BlockSpec((1, tr, cols), lambda i, c_ref: ((c_ref[0] + kk) % N_CHIPS, i, 0))

    return pl.pallas_call(
        body, name="sum_parts_" + tag,
        grid_spec=pltpu.PrefetchScalarGridSpec(
            num_scalar_prefetch=1, grid=(rows // tr,), in_specs=[after(0), after(1), after(2), after(3)],
            out_specs=pl.BlockSpec((tr, cols), lambda i, c_ref: (i, 0))),
        out_shape=jax.ShapeDtypeStruct((rows, cols), F32),
        compiler_params=_params(("arbitrary",)),
    )(chip, own, parts, parts, parts)


def _adamw_math(w, g, m, v):
    m = ADAM_B1 * m + (1.0 - ADAM_B1) * g
    v = ADAM_B2 * v + (1.0 - ADAM_B2) * jnp.square(g)
    m_hat = m / (1.0 - ADAM_B1 ** ADAM_STEP)
    v_hat = v / (1.0 - ADAM_B2 ** ADAM_STEP)
    delta = -ADAM_LR * (m_hat / (jnp.sqrt(v_hat) + ADAM_EPS) + ADAM_WD * w)
    return delta, m, v


def _adamw(tag, w, computed, received, m, v, tr=256):
    _, rows, cols = w.shape
    tr = min(tr, rows)

    def body(*refs):
        w_ref, m_ref, v_ref = refs[:3]
        g_refs = refs[3:3 + 2 * DEPTH]
        g_ref, d_ref, nm_ref, nv_ref = refs[3 + 2 * DEPTH:]
        layer = pl.program_id(0)
        core = lax.axis_index("c")
        g = jnp.zeros((tr, cols), F32)
        for l in range(DEPTH):
            mine = jnp.where(core == l, g_refs[2 * l][...], g_refs[2 * l + 1][...])
            g = jnp.where(layer == l, mine, g)
        g_ref[0] = g
        d_ref[0], nm_ref[0], nv_ref[0] = _adamw_math(w_ref[0], g, m_ref[0], v_ref[0])

    stacked = pl.BlockSpec((1, tr, cols), lambda l, i: (l, i, 0))

    def of_layer(k):
        return pl.BlockSpec((tr, cols), lambda l, i: (jnp.where(l == k, i, 0), 0))

    g_specs = [of_layer(l) for l in range(DEPTH) for _ in range(2)]
    g_args = [a for l in range(DEPTH) for a in (computed[l], received[l])]
    return pl.pallas_call(
        body, name="adamw_" + tag, grid=(DEPTH, rows // tr), in_specs=[stacked] * 3 + g_specs,
        out_specs=[stacked] * 4, out_shape=[jax.ShapeDtypeStruct(w.shape, F32)] * 4,
        compiler_params=_params(("arbitrary", "arbitrary")),
    )(w, m, v, *g_args)


def _w_in_to_padded(w):
    d0 = SSD_DIM + CONV_DIM
    return jnp.concatenate([w[:, :d0], w[:, d0 + SSD_HEADS:], w[:, d0:d0 + SSD_HEADS],
                            jnp.zeros((w.shape[0], DT_PAD - SSD_HEADS), w.dtype)], axis=1)


def _w_in_from_padded(w):
    d0 = SSD_DIM + CONV_DIM
    return jnp.concatenate([w[:, :d0], w[:, DT_OFF:DT_OFF + SSD_HEADS], w[:, d0:DT_OFF]], axis=1)


def _small_layout():
    return (("norm_mix_g", 0, 0, D_MODEL), ("norm_xa_g", 1, 0, D_MODEL), ("norm_mem_g", 2, 0, D_MODEL),
            ("norm_ff_g", 3, 0, D_MODEL), ("conv_b", 4, 0, CONV_DIM), ("ssd_norm_g", 5, 0, SSD_DIM),
            ("sb_norm_g", 5, SSD_DIM, SB_DIM), ("dt_bias", 6, 0, SSD_HEADS), ("a_log", 6, LANES, SSD_HEADS),
            ("d_skip", 6, 2 * LANES, SSD_HEADS))


def _pack_small(gs, loss):
    lay = _small_layout()
    args = [gs[name][l] for l in range(DEPTH) for name, _, _, _ in lay]
    args += [gs["conv_w"][l] for l in range(DEPTH)] + [gs["final_g"], loss]
    n_lay = len(lay)

    def body(*refs):
        o_ref = refs[-1]
        o_ref[...] = jnp.zeros_like(o_ref)
        for l in range(DEPTH):
            for i, (_, rr, c0, width) in enumerate(lay):
                row = l * _SM_PER_LAYER + rr
                o_ref[row:row + 1, c0:c0 + width] = refs[l * n_lay + i][...]
            row = _SM_CONVW + l * CONV_K
            o_ref[row:row + CONV_K, 0:CONV_DIM] = refs[DEPTH * n_lay + l][...]
        o_ref[_SM_FINAL:_SM_FINAL + 1, :] = refs[DEPTH * n_lay + DEPTH][...]
        o_ref[_SM_LOSS:_SM_LOSS + 1, 0:LANES] = refs[DEPTH * n_lay + DEPTH + 1][...]

    return pl.pallas_call(
        body, name="pack_small", in_specs=[VMEM_SPEC] * len(args), out_specs=VMEM_SPEC,
        out_shape=jax.ShapeDtypeStruct((SMALL_ROWS, PACK_COLS), F32),
    )(*args)


def _small_update(buf, w, mom, var):
    lay = _small_layout()
    names = [name for name, _, _, _ in lay] + ["final_g", "conv_w"]
    conv_cols = CONV_DIM // N_CHIPS
    shapes2d = {name: (DEPTH, width) for name, _, _, width in lay}
    shapes2d["final_g"] = (1, D_MODEL)
    shapes2d["conv_w"] = (DEPTH * CONV_K, conv_cols)
    args = [buf]
    for src in (w, mom, var):
        args += [src[name].reshape(shapes2d[name]) for name in names]
    n = len(names)

    def body(*refs):
        b_ref = refs[0]
        w_refs, m_refs, v_refs = refs[1:1 + n], refs[1 + n:1 + 2 * n], refs[1 + 2 * n:1 + 3 * n]
        outs = refs[1 + 3 * n:]
        chip = 2 * lax.axis_index("x") + lax.axis_index("y")
        for i, name in enumerate(names):
            if name == "final_g":
                g = b_ref[_SM_FINAL:_SM_FINAL + 1, :]
            elif name == "conv_w":
                rows = b_ref[_SM_CONVW:_SM_CONVW + DEPTH * CONV_K, 0:CONV_DIM]
                g = jnp.zeros((DEPTH * CONV_K, conv_cols), F32)
                for j in range(N_CHIPS):
                    g = g + jnp.where(chip == j, rows[:, j * conv_cols:(j + 1) * conv_cols], 0.0)
            else:
                _, rr, c0, width = lay[i]
                g = jnp.concatenate([b_ref[l * _SM_PER_LAYER + rr:l * _SM_PER_LAYER + rr + 1, c0:c0 + width]
                                     for l in range(DEPTH)], axis=0)
            d, m2, v2 = _adamw_math(w_refs[i][...], g, m_refs[i][...], v_refs[i][...])
            outs[i][...] = g
            outs[n + i][...] = d
            outs[2 * n + i][...] = m2
            outs[3 * n + i][...] = v2

    out_shape = [jax.ShapeDtypeStruct(shapes2d[name], F32) for _ in range(4) for name in names]
    res = pl.pallas_call(
        body, name="small_update", in_specs=[VMEM_SPEC] * len(args), out_specs=[VMEM_SPEC] * (4 * n),
        out_shape=out_shape,
    )(*args)
    return tuple({name: res[k * n + i].reshape(w[name].shape) for i, name in enumerate(names)} for k in range(4))


SMALL_NAMES = ("norm_mix_g", "conv_b", "dt_bias", "a_log", "d_skip", "ssd_norm_g", "sb_norm_g", "norm_xa_g",
               "norm_mem_g", "norm_ff_g", "final_g")
WEIGHT_ORDER = ("norm_mix_g", "w_in", "conv_w", "conv_b", "dt_bias", "a_log", "d_skip", "ssd_norm_g", "sb_norm_g",
                "w_out", "norm_xa_g", "norm_mem_g", "w_xq", "w_xk", "w_xv", "w_xo", "norm_ff_g", "w_ff1", "w_ff2",
                "final_g")


_ALL = tuple(range(len(MATS)))
_IN = tuple(i for i in _ALL if MATS[i][0] == "w_in")
_MLP = tuple(i for i in _ALL if MATS[i][0] in ("w_ff1", "w_ff2"))
_MIXER = tuple(i for i in _ALL if i not in _IN + _MLP)
_CONV = len(MATS)


class PipelinedPlan(LocalPlan):
    def __init__(self, shards, conv_w):
        self.shards = list(shards) + [conv_w]
        self.chip = 2 * lax.axis_index("x") + lax.axis_index("y")
        self.mats = [dict() for _ in range(DEPTH)]
        n = len(MATS)
        self.parts = [[None] * n for _ in range(DEPTH)]
        self.to_chips = [[None] * n for _ in range(DEPTH)]
        self.reduced = [[None] * n for _ in range(DEPTH)]
        self.riders = {}
        first = _IN + (_CONV,)
        self._gathered(0, first, _run_exchange("gather_first", self._gather(0, first)))
        self._gather_behind(0, _MIXER, "proj_fwd", 0)
        self._gather_behind(0, _MLP[:1], "ssd_fwd", 0)
        self._gather_behind(0, _MLP[1:], "sb_fwd", 0)
        for l in range(1, DEPTH):
            self._gather_behind(l, first + _MIXER, "sb_fwd", l - 1)
            self._gather_behind(l, _MLP, "mlp_fwd", l - 1)

    def _ride(self, kernel, l, exchange, then):
        self.riders.setdefault((kernel, l), []).append((exchange, then))

    def carried_by(self, kernel, l):
        exchange = None
        for ex, _ in self.riders.get((kernel, l), []):
            exchange = _both(exchange, ex)
        return exchange

    def carried_out(self, kernel, l, outs):
        for ex, then in self.riders.pop((kernel, l), []):
            then(outs[:len(ex.outputs)])
            outs = outs[len(ex.outputs):]

    def _gather(self, l, which):
        return _gather_exchange(l, [self.shards[i] for i in which])

    def _gather_behind(self, l, which, kernel, host):
        self._ride(kernel, host, self._gather(l, which), lambda outs: self._gathered(l, which, outs))

    def _gathered(self, l, which, outs):
        for i, theirs in zip(which, outs):
            name, _, axis = MATS[i] if i != _CONV else ("conv_w", None, 1)
            full = jnp.concatenate([jnp.where(self.chip == j, self.shards[i][l], theirs[j]) for j in range(N_CHIPS)],
                                   axis=axis)
            self.mats[l][name] = _w_in_to_padded(full) if name == "w_in" else full

    def _set_parts(self, l, which, gm):
        for i in which:
            name, _, axis = MATS[i]
            g = _w_in_from_padded(gm[name]) if name == "w_in" else gm[name]
            if g.ndim == 2 and axis == 0:
                g = g.reshape((N_CHIPS, g.shape[0] // N_CHIPS, g.shape[1]))
            elif g.ndim == 2:
                g = jnp.swapaxes(g.reshape((g.shape[0], N_CHIPS, g.shape[1] // N_CHIPS)), 0, 1)
            self.parts[l][i] = g

    def _handover(self, l, which):
        return _handover_exchange(l, [self.parts[l][i] for i in which])

    def _handed(self, l, which, outs):
        for i, r in zip(which, outs):
            self.to_chips[l][i] = _add_handed(MATS[i][0] + str(l), l, self.parts[l][i], r)

    def _scatter(self, l, which):
        return _scatter_exchange(l, [self.to_chips[l][i] for i in which])

    def _scattered(self, l, which, outs):
        for i, got in zip(which, outs):
            self.reduced[l][i] = _sum_parts(MATS[i][0] + str(l), l, self.to_chips[l][i], got)

    def _send_behind(self, l, which, hand_kernel, cross_kernel, host):
        def handed(outs):
            self._handed(l, which, outs)
            self._ride(cross_kernel, host, self._scatter(l, which), lambda o: self._scattered(l, which, o))

        self._ride(hand_kernel, host, self._handover(l, which), handed)

    def mlp_grads_done(self, l, gm):
        self._set_parts(l, _MLP, gm)
        self._send_behind(l, _MLP, "xattn_bwd", "sb_bwd", l)

    def mixer_grads_done(self, l, gm):
        self._set_parts(l, _MIXER, gm)
        self._send_behind(l, _MIXER, "sb_bwd", "ssd_bwd", l)

    def grads_done(self, l, gm):
        self._set_parts(l, _IN, gm)
        if l > 0:
            self._send_behind(l, _IN, "mlp_bwd", "sb_bwd", l - 1)
        else:
            self._handed(0, _IN, _run_exchange("handover_last", self._handover(0, _IN)))
            self._scattered(0, _IN, _run_exchange("scatter_last", self._scatter(0, _IN)))

    def reduced_gradients(self):
        returned = _run_exchange("return_reduced", _return_exchange(self.reduced))
        n = len(MATS)
        return [([self.reduced[l][i] for l in range(DEPTH)], [returned[l * n + i] for l in range(DEPTH)])
                for i in range(n)]


def kernel(x, mem, norm_mix_g, w_in, conv_w, conv_b, dt_bias, a_log, d_skip, ssd_norm_g, sb_norm_g, w_out, norm_xa_g, norm_mem_g, w_xq, w_xk, w_xv, w_xo, norm_ff_g, w_ff1, w_ff2, final_g, loss_target, m_norm_mix_g, m_w_in, m_conv_w, m_conv_b, m_dt_bias, m_a_log, m_d_skip, m_ssd_norm_g, m_sb_norm_g, m_w_out, m_norm_xa_g, m_norm_mem_g, m_w_xq, m_w_xk, m_w_xv, m_w_xo, m_norm_ff_g, m_w_ff1, m_w_ff2, m_final_g, v_norm_mix_g, v_w_in, v_conv_w, v_conv_b, v_dt_bias, v_a_log, v_d_skip, v_ssd_norm_g, v_sb_norm_g, v_w_out, v_norm_xa_g, v_norm_mem_g, v_w_xq, v_w_xk, v_w_xv, v_w_xo, v_norm_ff_g, v_w_ff1, v_w_ff2, v_final_g):
    w = dict(norm_mix_g=norm_mix_g, w_in=w_in, conv_w=conv_w, conv_b=conv_b, dt_bias=dt_bias, a_log=a_log,
             d_skip=d_skip, ssd_norm_g=ssd_norm_g, sb_norm_g=sb_norm_g, w_out=w_out, norm_xa_g=norm_xa_g,
             norm_mem_g=norm_mem_g, w_xq=w_xq, w_xk=w_xk, w_xv=w_xv, w_xo=w_xo, norm_ff_g=norm_ff_g, w_ff1=w_ff1,
             w_ff2=w_ff2, final_g=final_g)
    mom = dict(norm_mix_g=m_norm_mix_g, w_in=m_w_in, conv_w=m_conv_w, conv_b=m_conv_b, dt_bias=m_dt_bias,
               a_log=m_a_log, d_skip=m_d_skip, ssd_norm_g=m_ssd_norm_g, sb_norm_g=m_sb_norm_g, w_out=m_w_out,
               norm_xa_g=m_norm_xa_g, norm_mem_g=m_norm_mem_g, w_xq=m_w_xq, w_xk=m_w_xk, w_xv=m_w_xv, w_xo=m_w_xo,
               norm_ff_g=m_norm_ff_g, w_ff1=m_w_ff1, w_ff2=m_w_ff2, final_g=m_final_g)
    var = dict(norm_mix_g=v_norm_mix_g, w_in=v_w_in, conv_w=v_conv_w, conv_b=v_conv_b, dt_bias=v_dt_bias,
               a_log=v_a_log, d_skip=v_d_skip, ssd_norm_g=v_ssd_norm_g, sb_norm_g=v_sb_norm_g, w_out=v_w_out,
               norm_xa_g=v_norm_xa_g, norm_mem_g=v_norm_mem_g, w_xq=v_w_xq, w_xk=v_w_xk, w_xv=v_w_xv, w_xo=v_w_xo,
               norm_ff_g=v_norm_ff_g, w_ff1=v_w_ff1, w_ff2=v_w_ff2, final_g=v_final_g)
    sw = {name: w[name] for name in SMALL_NAMES}
    plan = PipelinedPlan([w[name].astype(BF16) for name, _, _ in MATS], conv_w)
    loss, grad_x, gm, gs = _local_step(x[0], mem[0], loss_target[0], sw, plan)
    g_mats = plan.reduced_gradients()

    _, small_sum = _gather_small("_grads", _pack_small(gs, loss))
    loss_out = small_sum[_SM_LOSS, 0]

    grads, deltas, new_m, new_v = {}, {}, {}, {}
    for (name, _, _), (computed, received) in zip(MATS, g_mats):
        grads[name], deltas[name], new_m[name], new_v[name] = _adamw(
            name, w[name], computed, received, mom[name], var[name])
    g_s, d_s, m_s, v_s = _small_update(small_sum, w, mom, var)
    for name in g_s:
        grads[name], deltas[name], new_m[name], new_v[name] = g_s[name], d_s[name], m_s[name], v_s[name]

    return (loss_out, grad_x[None], *[grads[n] for n in WEIGHT_ORDER], *[deltas[n] for n in WEIGHT_ORDER],
            *[new_m[n] for n in WEIGHT_ORDER], *[new_v[n] for n in WEIGHT_ORDER])
```

```python
import functools
import math

import jax
import jax.numpy as jnp
from jax import lax
from jax.experimental import pallas as pl
from jax.experimental.pallas import tpu as pltpu

F32 = jnp.float32
BF16 = jnp.bfloat16
MESH = pl.DeviceIdType.MESH

D_MODEL = 1024
DEPTH = 2
SSD_DIM = 512
SSD_HEAD_DIM = 64
SSD_HEADS = 8
SSD_GROUPS = 2
SSD_STATE = 64
CONV_K = 4
CHUNK = 128
SB_DIM = 512
SB_HEAD_DIM = 64
XA_HEADS = 4
XA_HEAD_DIM = 128
XA_DIM = 512
D_FF = 4096
EPS = 1e-5
GN = SSD_GROUPS * SSD_STATE
CONV_DIM = SSD_DIM + 2 * GN
IN_DIM = SSD_DIM + CONV_DIM + SSD_HEADS + 3 * SB_DIM
LANES = 128
DT_PAD = LANES
IN_PAD = SSD_DIM + CONV_DIM + 3 * SB_DIM + DT_PAD
Q_OFF = SSD_DIM + CONV_DIM
DT_OFF = Q_OFF + 3 * SB_DIM
HALO = 8

ADAM_LR = 0.001
ADAM_B1 = 0.9
ADAM_B2 = 0.999
ADAM_EPS = 1e-08
ADAM_WD = 0.01
ADAM_STEP = 10

N_CHIPS = 4
N_DEV = 8
PACK_COLS = 1024
VMEM_LIMIT = 56 * 1024 * 1024

MATS = (
    ("w_in", (D_MODEL, IN_DIM), 1),
    ("w_out", (D_MODEL, D_MODEL), 0),
    ("w_xq", (D_MODEL, XA_DIM), 0),
    ("w_xk", (D_MODEL, XA_DIM), 0),
    ("w_xv", (D_MODEL, XA_DIM), 0),
    ("w_xo", (XA_DIM, D_MODEL), 1),
    ("w_ff1", (D_MODEL, D_FF), 1),
    ("w_ff2", (D_FF, D_MODEL), 0),
)


SMALL_ROWS = 24
_SM_PER_LAYER = 7
_SM_FINAL = 14
_SM_CONVW = 15
_SM_LOSS = 23


_NN = ((1,), (0,))
_NT = ((1,), (1,))
_TN = ((0,), (0,))


def _dg(a, b, dims):
    return lax.dot_general(a.astype(BF16), b.astype(BF16), (dims, ((), ())), preferred_element_type=F32)


@jax.custom_vjp
def mm_nn(a, b):
    return _dg(a, b, _NN)


@jax.custom_vjp
def mm_nt(a, b):
    return _dg(a, b, _NT)


@jax.custom_vjp
def mm_tn(a, b):
    return _dg(a, b, _TN)


def _nn_fwd(a, b):
    return _dg(a, b, _NN), (a, b)


def _nn_bwd(res, g):
    a, b = res
    return mm_nt(g, b).astype(a.dtype), mm_tn(a, g).astype(b.dtype)


def _nt_fwd(a, b):
    return _dg(a, b, _NT), (a, b)


def _nt_bwd(res, g):
    a, b = res
    return mm_nn(g, b).astype(a.dtype), mm_tn(g, a).astype(b.dtype)


def _tn_fwd(a, b):
    return _dg(a, b, _TN), (a, b)


def _tn_bwd(res, g):
    a, b = res
    return mm_nt(b, g).astype(a.dtype), mm_nn(a, g).astype(b.dtype)


mm_nn.defvjp(_nn_fwd, _nn_bwd)
mm_nt.defvjp(_nt_fwd, _nt_bwd)
mm_tn.defvjp(_tn_fwd, _tn_bwd)


def _rms(x, g):
    return x * lax.rsqrt(jnp.mean(x * x, axis=-1, keepdims=True) + EPS) * g


def _params(sem=None, vmem=VMEM_LIMIT):
    return pltpu.CompilerParams(dimension_semantics=sem, vmem_limit_bytes=vmem)


class Exchange:
    def __init__(self, inputs, outputs, n_sems, start, end, mid=None):
        self.inputs, self.outputs, self.n_sems = list(inputs), list(outputs), n_sems
        self.start, self.mid, self.end = start, mid, end

    def specs(self):
        hbm = pl.BlockSpec(memory_space=pl.ANY)
        sems = [pltpu.SemaphoreType.DMA((self.n_sems,)), pltpu.SemaphoreType.DMA((self.n_sems,))]
        return [hbm] * len(self.inputs), [hbm] * len(self.outputs), sems

    def pick(self, refs, n_before_in, n_before_out):
        n_in, n_out = len(self.inputs), len(self.outputs)
        o0 = n_before_in + n_in + n_before_out
        return refs[n_before_in:n_before_in + n_in], refs[o0:o0 + n_out], refs[-2], refs[-1]

    def before_work(self, mine, first, mid=None):
        @pl.when(first)
        def _():
            self.start(*mine)

        if self.mid is not None and mid is not None:
            @pl.when(mid)
            def _():
                self.mid(*mine)

    def after_work(self, mine, last, mid_done):
        @pl.when(last)
        def _():
            if self.mid is not None and not mid_done:
                self.mid(*mine)
            self.end(*mine)


class _Shifted:
    def __init__(self, ref, base):
        self.ref, self.base = ref, base

    @property
    def at(self):
        return self

    def __getitem__(self, idx):
        return self.ref.at[self.base + idx]


def _both(a, b):
    if a is None or b is None:
        return a or b
    n_i, n_o, n_s = len(a.inputs), len(a.outputs), a.n_sems

    def joined(fa, fb):
        def f(i_refs, o_refs, ss, rs):
            if fa is not None:
                fa(i_refs[:n_i], o_refs[:n_o], ss, rs)
            if fb is not None:
                fb(i_refs[n_i:], o_refs[n_o:], _Shifted(ss, n_s), _Shifted(rs, n_s))
        return f

    mid = joined(a.mid, b.mid) if (a.mid is not None or b.mid is not None) else None
    return Exchange(a.inputs + b.inputs, a.outputs + b.outputs, n_s + b.n_sems, joined(a.start, b.start),
                    joined(a.end, b.end), mid)


def _rowcall(name, fn, rows, fulls, row_out, acc_out=(), tm=256, exchange=None):
    s = rows[0].shape[0]
    tm = min(tm, s)
    nt = s // tm
    n_r, n_f, n_ro, n_ao = len(rows), len(fulls), len(row_out), len(acc_out)
    n_xi = len(exchange.inputs) if exchange else 0

    def body(*refs):
        if exchange:
            mine = exchange.pick(refs, n_r + n_f, n_ro + n_ao)
            exchange.before_work(mine, pl.program_id(0) == 0, pl.program_id(0) == (3 * nt) // 4)
        ins = [r[...] for r in refs[: n_r + n_f]]
        outs = fn(*ins)
        o_refs = refs[n_r + n_f + n_xi:]
        for o_ref, val in zip(o_refs[:n_ro], outs[:n_ro]):
            o_ref[...] = val.astype(o_ref.dtype)
        if n_ao:
            first = pl.program_id(0) == 0

            @pl.when(first)
            def _():
                for o_ref, val in zip(o_refs[n_ro:], outs[n_ro:]):
                    o_ref[...] = val.astype(o_ref.dtype)

            @pl.when(jnp.logical_not(first))
            def _():
                for o_ref, val in zip(o_refs[n_ro:], outs[n_ro:]):
                    o_ref[...] += val.astype(o_ref.dtype)
        if exchange:
            exchange.after_work(mine, pl.program_id(0) == nt - 1, mid_done=True)

    in_specs = [pl.BlockSpec((tm, a.shape[1]), lambda i: (i, 0)) for a in rows]
    in_specs += [pl.BlockSpec(a.shape, lambda i: (0, 0), pipeline_mode=pl.Buffered(1)) for a in fulls]
    out_specs = [pl.BlockSpec((tm, c), lambda i: (i, 0)) for c, _ in row_out]
    out_specs += [pl.BlockSpec(shape, lambda i: (0, 0)) for shape, _ in acc_out]
    out_shape = [jax.ShapeDtypeStruct((s, c), dt) for c, dt in row_out]
    out_shape += [jax.ShapeDtypeStruct(shape, dt) for shape, dt in acc_out]
    args, scratch = [*rows, *fulls], []
    if exchange:
        x_in, x_out, scratch = exchange.specs()
        in_specs += x_in
        out_specs += x_out
        out_shape += exchange.outputs
        args += exchange.inputs
    return pl.pallas_call(
        body, name=name, grid=(nt,), in_specs=in_specs, out_specs=out_specs, out_shape=out_shape,
        scratch_shapes=scratch, compiler_params=_params(("arbitrary",)),
    )(*args)


def _mm_tn_call(name, a, b, tm, tn, tk):
    s, m = a.shape
    n = b.shape[1]
    tk = min(tk, s)

    def body(a_ref, b_ref, o_ref):
        d = _dg(a_ref[...], b_ref[...], _TN)
        first = pl.program_id(2) == 0

        @pl.when(first)
        def _():
            o_ref[...] = d

        @pl.when(jnp.logical_not(first))
        def _():
            o_ref[...] += d

    return pl.pallas_call(
        body, name=name, grid=(m // tm, n // tn, s // tk),
        in_specs=[pl.BlockSpec((tk, tm), lambda i, j, k: (k, i)), pl.BlockSpec((tk, tn), lambda i, j, k: (k, j))],
        out_specs=pl.BlockSpec((tm, tn), lambda i, j, k: (i, j)),
        out_shape=jax.ShapeDtypeStruct((m, n), F32),
        compiler_params=_params(("parallel", "parallel", "arbitrary")),
    )(a, b)


def _mm_tn_parts(name, a, b, by_cols, tm=512, tk=2048):
    s, m = a.shape
    n = b.shape[1]
    r, c = (m, n // N_CHIPS) if by_cols else (m // N_CHIPS, n)
    per = r // tm
    tk = min(tk, s)
    nk = s // tk

    def body(a_ref, b_ref, o_ref, acc):
        d = _dg(a_ref[...], b_ref[...], _TN)
        step = pl.program_id(2)

        @pl.when(step == 0)
        def _():
            acc[...] = d

        @pl.when(step > 0)
        def _():
            acc[...] += d

        @pl.when(step == nk - 1)
        def _():
            o_ref[0] = acc[...].astype(o_ref.dtype)

    if by_cols:
        out_map = lambda i, j, k: (j, i, 0)
    else:
        out_map = lambda i, j, k: (i // per, i % per, 0)
    return pl.pallas_call(
        body, name=name, grid=(m // tm, n // c, s // tk),
        in_specs=[pl.BlockSpec((tk, tm), lambda i, j, k: (k, i)), pl.BlockSpec((tk, c), lambda i, j, k: (k, j))],
        out_specs=pl.BlockSpec((1, tm, c), out_map),
        out_shape=jax.ShapeDtypeStruct((N_CHIPS, r, c), BF16), scratch_shapes=[pltpu.VMEM((tm, c), F32)],
        compiler_params=_params(("parallel", "parallel", "arbitrary")),
    )(a, b)


def _proj_tile(h, g, w):
    p = mm_nn(_rms(h, g), w)
    return (p[:, :SSD_DIM], p[:, SSD_DIM:Q_OFF], p[:, Q_OFF:Q_OFF + SB_DIM],
            p[:, Q_OFF + SB_DIM:Q_OFF + 2 * SB_DIM], p[:, Q_OFF + 2 * SB_DIM:DT_OFF], p[:, DT_OFF:])


def _proj_fwd(tag, h, g, w, exchange=None):
    return _rowcall(
        "proj_fwd" + tag, _proj_tile, [h], [g, w],
        [(SSD_DIM, F32), (CONV_DIM, F32), (SB_DIM, BF16), (SB_DIM, BF16), (SB_DIM, BF16), (DT_PAD, F32)], tm=512,
        exchange=exchange)


def _proj_bwd(tag, h, dh_out, dz, dxbc, dq, dk, dv, ddt, g, w):
    def fn(h, dh_out, dz, dxbc, dq, dk, dv, ddt, g, w):
        dp = jnp.concatenate([dz.astype(BF16), dxbc.astype(BF16), dq.astype(BF16), dk.astype(BF16),
                              dv.astype(BF16), ddt.astype(BF16)], axis=1)
        hn, vjp = jax.vjp(_rms, h, g)
        dh, dg = vjp(mm_nt(dp, w))
        return dh_out + dh, hn, dp, dg

    return _rowcall(
        "proj_bwd" + tag, fn, [h, dh_out, dz, dxbc, dq, dk, dv, ddt], [g, w],
        [(D_MODEL, F32), (D_MODEL, BF16), (IN_PAD, BF16)], [((1, D_MODEL), F32)], tm=256)


def _shift_down(x, tail, j):
    if j == 0:
        return x
    r = pltpu.roll(x, j, 0)
    rt = pltpu.roll(tail, j, 0)
    row = lax.broadcasted_iota(jnp.int32, (HALO, x.shape[1]), 0)
    first = jnp.where(row < j, rt, r[:HALO])
    if x.shape[0] == HALO:
        return first
    return jnp.concatenate([first, r[HALO:]], axis=0)


def _shift_up(x, head, j):
    if j == 0:
        return x
    n = x.shape[0]
    r = pltpu.roll(x, n - j, 0)
    rh = pltpu.roll(head, HALO - j, 0)
    row = lax.broadcasted_iota(jnp.int32, (HALO, x.shape[1]), 0)
    return jnp.concatenate([r[:n - HALO], jnp.where(row >= HALO - j, rh, r[n - HALO:])], axis=0)


def _conv_pre(x, tail, w, b):
    acc = b + w[CONV_K - 1:CONV_K] * x
    for j in range(1, CONV_K):
        acc = acc + w[CONV_K - 1 - j:CONV_K - j] * _shift_down(x, tail, j)
    return acc


def _dsilu(p):
    s = jax.nn.sigmoid(p)
    return s * (1.0 + p * (1.0 - s))


def _conv_fwd(tag, xbc, w, b, tc=512):
    s, c = xbc.shape
    tc = min(tc, s)
    per = tc // HALO

    def body(x_ref, prev_ref, w_ref, b_ref, o_ref):
        tail = jnp.where(pl.program_id(0) > 0, prev_ref[...], 0.0)
        o_ref[...] = jax.nn.silu(_conv_pre(x_ref[...], tail, w_ref[...], b_ref[...]))

    return pl.pallas_call(
        body, name="conv_fwd" + tag, grid=(s // tc,),
        in_specs=[pl.BlockSpec((tc, c), lambda i: (i, 0)),
                  pl.BlockSpec((HALO, c), lambda i: (jnp.maximum(i * per - 1, 0), 0)),
                  pl.BlockSpec((CONV_K, c), lambda i: (0, 0)), pl.BlockSpec((1, c), lambda i: (0, 0))],
        out_specs=pl.BlockSpec((tc, c), lambda i: (i, 0)),
        out_shape=jax.ShapeDtypeStruct((s, c), F32),
        compiler_params=_params(("arbitrary",)),
    )(xbc, xbc, w, b)


def _conv_bwd(tag, xbc, dact, w, b, tc=512):
    s, c = xbc.shape
    tc = min(tc, s)
    per = tc // HALO
    nt = s // tc
    last_blk = s // HALO - 1

    def body(x_ref, prev_ref, next_ref, d_ref, dnext_ref, w_ref, b_ref, dx_ref, dw_ref, db_ref):
        i = pl.program_id(0)
        x = x_ref[...]
        wv = w_ref[...]
        tail = jnp.where(i > 0, prev_ref[...], 0.0)
        dpre = d_ref[...] * _dsilu(_conv_pre(x, tail, wv, b_ref[...]))
        pre_n = _conv_pre(next_ref[...], x[tc - HALO:], wv, b_ref[...])
        dpre_n = jnp.where(i < nt - 1, dnext_ref[...] * _dsilu(pre_n), 0.0)
        dx = wv[CONV_K - 1:CONV_K] * dpre
        for j in range(1, CONV_K):
            dx = dx + wv[CONV_K - 1 - j:CONV_K - j] * _shift_up(dpre, dpre_n, j)
        dx_ref[...] = dx
        dws = [jnp.sum(dpre * _shift_down(x, tail, CONV_K - 1 - k), axis=0, keepdims=True) for k in range(CONV_K)]
        dwv = jnp.concatenate(dws, axis=0)
        dbv = jnp.sum(dpre, axis=0, keepdims=True)

        @pl.when(i == 0)
        def _():
            dw_ref[...] = dwv
            db_ref[...] = dbv

        @pl.when(i > 0)
        def _():
            dw_ref[...] += dwv
            db_ref[...] += dbv

    tile = pl.BlockSpec((tc, c), lambda i: (i, 0))
    prev = pl.BlockSpec((HALO, c), lambda i: (jnp.maximum(i * per - 1, 0), 0))
    nxt = pl.BlockSpec((HALO, c), lambda i: (jnp.minimum((i + 1) * per, last_blk), 0))
    return pl.pallas_call(
        body, name="conv_bwd" + tag, grid=(nt,),
        in_specs=[tile, prev, nxt, tile, nxt, pl.BlockSpec((CONV_K, c), lambda i: (0, 0)),
                  pl.BlockSpec((1, c), lambda i: (0, 0))],
        out_specs=[tile, pl.BlockSpec((CONV_K, c), lambda i: (0, 0)), pl.BlockSpec((1, c), lambda i: (0, 0))],
        out_shape=[jax.ShapeDtypeStruct((s, c), F32), jax.ShapeDtypeStruct((CONV_K, c), F32),
                   jax.ShapeDtypeStruct((1, c), F32)],
        compiler_params=_params(("arbitrary",)),
    )(xbc, xbc, xbc, dact, dact, w, b)


def _ssd_chunk(xs, bm, cm, dtr, z, dt_bias, a_log, d_skip, g, s_prev):
    n = CHUNK
    row = lax.broadcasted_iota(jnp.int32, (n, n), 0)
    col = lax.broadcasted_iota(jnp.int32, (n, n), 1)
    causal = row >= col
    dt = jax.nn.softplus(dtr + dt_bias)
    a_c = dt * (-jnp.exp(a_log))
    hi = lax.Precision.HIGHEST
    a_cum = jnp.dot(causal.astype(F32), a_c, precision=hi, preferred_element_type=F32)
    a_cum_t = lax.dot_general(a_c, (row <= col).astype(F32), (_TN, ((), ())), precision=hi,
                              preferred_element_type=F32)
    p, st = SSD_HEAD_DIM, SSD_STATE
    heads = range(SSD_HEADS)
    grp = [h // (SSD_HEADS // SSD_GROUPS) for h in heads]
    bgs = [bm[:, k * st:(k + 1) * st] for k in range(SSD_GROUPS)]
    cgs = [cm[:, k * st:(k + 1) * st] for k in range(SSD_GROUPS)]
    cb = [mm_nt(cgs[k], bgs[k]) for k in range(SSD_GROUPS)]
    acols = [a_cum[:, h:h + 1] for h in heads]
    a_lasts = [a_cum[n - 1:n, h:h + 1] for h in heads]
    xhs = [xs[:, h * p:(h + 1) * p] for h in heads]
    sps = [s_prev[h * p:(h + 1) * p, :] for h in heads]
    xdts = [xhs[h] * dt[:, h:h + 1] for h in heads]
    decays = [jnp.exp(jnp.where(causal, acols[h] - a_cum_t[h:h + 1, :], -jnp.inf)) for h in heads]
    y_offs = [mm_nt(cgs[grp[h]], sps[h]) for h in heads]
    y_diags = [mm_nn(cb[grp[h]] * decays[h], xdts[h]) for h in heads]
    states = [mm_tn(xdts[h] * jnp.exp(a_lasts[h] - acols[h]), bgs[grp[h]]) for h in heads]
    s_new = [sps[h] * jnp.exp(a_lasts[h]) + states[h] for h in heads]
    ys = [y_diags[h] + y_offs[h] * jnp.exp(acols[h]) + d_skip[:, h:h + 1] * xhs[h] for h in heads]
    y = jnp.concatenate(ys, axis=1) * jax.nn.silu(z)
    return _rms(y, g), jnp.concatenate(s_new, axis=0)


def _split_xbc(t):
    return t[:, :SSD_DIM], t[:, SSD_DIM:SSD_DIM + GN], t[:, SSD_DIM + GN:]


def _ssd_fwd(tag, xact, dtr, z, dt_bias, a_log, d_skip, g, exchange=None):
    s = xact.shape[0]
    nc = s // CHUNK
    srows = SSD_HEADS * SSD_HEAD_DIM
    n_xi = len(exchange.inputs) if exchange else 0

    def body(*refs):
        x_ref, dt_ref, z_ref, b_ref, al_ref, ds_ref, g_ref = refs[:7]
        y_ref, st_ref = refs[7 + n_xi:9 + n_xi]
        state = refs[-3] if exchange else refs[-1]
        if exchange:
            mine = exchange.pick(refs, 7, 2)
            exchange.before_work(mine, pl.program_id(0) == 0, pl.program_id(0) == (3 * nc) // 4)

        @pl.when(pl.program_id(0) == 0)
        def _():
            state[...] = jnp.zeros_like(state)

        sp = state[...]
        st_ref[0] = sp
        xs, bm, cm = _split_xbc(x_ref[...])
        y, sn = _ssd_chunk(xs, bm, cm, dt_ref[...][:, :SSD_HEADS], z_ref[...], b_ref[...], al_ref[...],
                           ds_ref[...], g_ref[...], sp)
        y_ref[...] = y
        state[...] = sn
        if exchange:
            exchange.after_work(mine, pl.program_id(0) == nc - 1, mid_done=True)

    small = pl.BlockSpec((1, SSD_HEADS), lambda i: (0, 0))
    in_specs = [pl.BlockSpec((CHUNK, CONV_DIM), lambda i: (i, 0)), pl.BlockSpec((CHUNK, DT_PAD), lambda i: (i, 0)),
                pl.BlockSpec((CHUNK, SSD_DIM), lambda i: (i, 0)), small, small, small,
                pl.BlockSpec((1, SSD_DIM), lambda i: (0, 0))]
    out_specs = [pl.BlockSpec((CHUNK, SSD_DIM), lambda i: (i, 0)),
                 pl.BlockSpec((1, srows, SSD_STATE), lambda i: (i, 0, 0))]
    out_shape = [jax.ShapeDtypeStruct((s, SSD_DIM), F32), jax.ShapeDtypeStruct((nc, srows, SSD_STATE), F32)]
    args, scratch = [xact, dtr, z, dt_bias, a_log, d_skip, g], [pltpu.VMEM((srows, SSD_STATE), F32)]
    if exchange:
        x_in, x_out, sems = exchange.specs()
        in_specs, out_specs, scratch = in_specs + x_in, out_specs + x_out, scratch + sems
        out_shape, args = out_shape + exchange.outputs, args + exchange.inputs
    return pl.pallas_call(
        body, name="ssd_fwd" + tag, grid=(nc,), in_specs=in_specs, out_specs=out_specs, out_shape=out_shape,
        scratch_shapes=scratch, compiler_params=_params(("arbitrary",)),
    )(*args)


def _ssd_bwd(tag, xact, dtr, z, states, dy, dt_bias, a_log, d_skip, g, exchange=None):
    s = xact.shape[0]
    nc = s // CHUNK
    srows = SSD_HEADS * SSD_HEAD_DIM
    n_xi = len(exchange.inputs) if exchange else 0

    def body(*refs):
        x_ref, dt_ref, z_ref, sp_ref, dy_ref, b_ref, al_ref, ds_ref, g_ref = refs[:9]
        dx_ref, ddt_ref, dz_ref, db_ref, dal_ref, dds_ref, dg_ref = refs[9 + n_xi:16 + n_xi]
        dstate = refs[-3] if exchange else refs[-1]
        first = pl.program_id(0) == 0
        if exchange:
            mine = exchange.pick(refs, 9, 7)
            exchange.before_work(mine, first, pl.program_id(0) == (3 * nc) // 4)

        @pl.when(first)
        def _():
            dstate[...] = jnp.zeros_like(dstate)

        xs, bm, cm = _split_xbc(x_ref[...])
        _, vjp = jax.vjp(_ssd_chunk, xs, bm, cm, dt_ref[...][:, :SSD_HEADS], z_ref[...], b_ref[...], al_ref[...],
                         ds_ref[...], g_ref[...], sp_ref[0])
        dxs, dbm, dcm, ddt, dz, db, dal, dds, dg, dsp = vjp((dy_ref[...], dstate[...]))
        dx_ref[...] = jnp.concatenate([dxs, dbm, dcm], axis=1)
        ddt_ref[...] = jnp.concatenate([ddt, jnp.zeros((CHUNK, DT_PAD - SSD_HEADS), F32)], axis=1)
        dz_ref[...] = dz
        dstate[...] = dsp

        @pl.when(first)
        def _():
            db_ref[...] = db
            dal_ref[...] = dal
            dds_ref[...] = dds
            dg_ref[...] = dg

        @pl.when(jnp.logical_not(first))
        def _():
            db_ref[...] += db
            dal_ref[...] += dal
            dds_ref[...] += dds
            dg_ref[...] += dg

        if exchange:
            exchange.after_work(mine, pl.program_id(0) == nc - 1, mid_done=True)

    def rev(c):
        return lambda i: (nc - 1 - i, 0)

    small = pl.BlockSpec((1, SSD_HEADS), lambda i: (0, 0))
    gspec = pl.BlockSpec((1, SSD_DIM), lambda i: (0, 0))
    in_specs = [pl.BlockSpec((CHUNK, CONV_DIM), rev(0)), pl.BlockSpec((CHUNK, DT_PAD), rev(0)),
                pl.BlockSpec((CHUNK, SSD_DIM), rev(0)),
                pl.BlockSpec((1, srows, SSD_STATE), lambda i: (nc - 1 - i, 0, 0)),
                pl.BlockSpec((CHUNK, SSD_DIM), rev(0)), small, small, small, gspec]
    out_specs = [pl.BlockSpec((CHUNK, CONV_DIM), rev(0)), pl.BlockSpec((CHUNK, DT_PAD), rev(0)),
                 pl.BlockSpec((CHUNK, SSD_DIM), rev(0)), small, small, small, gspec]
    out_shape = [jax.ShapeDtypeStruct((s, CONV_DIM), F32), jax.ShapeDtypeStruct((s, DT_PAD), F32),
                 jax.ShapeDtypeStruct((s, SSD_DIM), F32), jax.ShapeDtypeStruct((1, SSD_HEADS), F32),
                 jax.ShapeDtypeStruct((1, SSD_HEADS), F32), jax.ShapeDtypeStruct((1, SSD_HEADS), F32),
                 jax.ShapeDtypeStruct((1, SSD_DIM), F32)]
    args, scratch = [xact, dtr, z, states, dy, dt_bias, a_log, d_skip, g], [pltpu.VMEM((srows, SSD_STATE), F32)]
    if exchange:
        x_in, x_out, sems = exchange.specs()
        in_specs, out_specs, scratch = in_specs + x_in, out_specs + x_out, scratch + sems
        out_shape, args = out_shape + exchange.outputs, args + exchange.inputs
    return pl.pallas_call(
        body, name="ssd_bwd" + tag, grid=(nc,), in_specs=in_specs, out_specs=out_specs, out_shape=out_shape,
        scratch_shapes=scratch, compiler_params=_params(("arbitrary",)),
    )(*args)


TQ = 128
TK = 128
SB_SCALE = 1.0 / math.sqrt(SB_HEAD_DIM)


def _split2(x):
    hi = x.astype(BF16)
    return hi, (x - hi.astype(F32)).astype(BF16)


TK_NEAR = 384
SB_UNDERFLOW = -110.0


def _sb_logits(qhs, kb, t0, s0, masked):
    zs = [_dg(qh, kb, _NT) for qh in qhs]
    mask = None
    if masked:
        t_pos = t0 + lax.broadcasted_iota(jnp.int32, zs[0].shape, 0)
        s_pos = s0 + lax.broadcasted_iota(jnp.int32, zs[0].shape, 1)
        mask = s_pos < t_pos
    lbs = [jnp.minimum(z, 0.0) - jnp.log(1.0 + jnp.exp(-jnp.abs(z))) for z in zs]
    lss = [lb - z for lb, z in zip(lbs, zs)]
    if masked:
        lss = [jnp.where(mask, ls, 0.0) for ls in lss]
    return lbs, lss, mask


def _running_sums(xs, starts, u, reverse, two_terms=True):
    nsub = xs[0].shape[1] // TK
    order = list(reversed(range(nsub))) if reverse else list(range(nsub))
    chunks = [[x[:, c * TK:(c + 1) * TK] for c in range(nsub)] for x in xs]
    sums = [[(_lane_sums(xc, u) if two_terms else _dg(xc, u, _NN)) for xc in row] for row in chunks]
    out = []
    for row, srow, run in zip(chunks, sums, starts):
        parts = [None] * nsub
        for c in order:
            parts[c] = run + srow[c]
            run = run + jnp.sum(row[c], axis=1, keepdims=True)
        out.append((parts[0] if nsub == 1 else jnp.concatenate(parts, axis=1), run))
    return out


def _lane_sums(x, u):
    hi, lo = _split2(x)
    return _dg(hi, u, _NN) + _dg(lo, u, _NN)


def _tri(cmp):
    j = lax.broadcasted_iota(jnp.int32, (TK, TK), 0)
    s = lax.broadcasted_iota(jnp.int32, (TK, TK), 1)
    return cmp(j, s).astype(BF16)


def _sb_fwd(tag, q, k, v, exchange=None):
    s = q.shape[0]
    npair = SB_DIM // LANES
    nq = s // TQ
    near = min(TK_NEAR, s)
    n_xi = len(exchange.inputs) if exchange else 0

    def body(*refs):
        q_ref, k_ref, v_ref = refs[:3]
        o_ref, t_ref = refs[3 + n_xi:5 + n_xi]
        tq = pl.program_id(1)
        if exchange:
            pair = pl.program_id(0)
            mine = exchange.pick(refs, 3, 2)
            exchange.before_work(mine, jnp.logical_and(pair == 0, tq == 0),
                                 jnp.logical_and(pair == npair - 1, tq == 0))
        qp = q_ref[...]
        lane = lax.broadcasted_iota(jnp.int32, (1, LANES), 1)
        u_gt = _tri(lambda j, s: j > s)
        heads = range(LANES // SB_HEAD_DIM)
        hms = [(lane // SB_HEAD_DIM) == hh for hh in heads]
        qhs = [jnp.where(hm, qp, jnp.zeros_like(qp)) * SB_SCALE for hm in hms]

        def block(wb, width, carry, masked):
            off = pl.multiple_of(wb * TK, TK)
            kb = k_ref[pl.ds(off, width), :]
            vb = v_ref[pl.ds(off, width), :]
            lbs, lss, mask = _sb_logits(qhs, kb, tq * TQ, wb * TK, masked)
            sums = _running_sums(lss, [c[0] for c in carry], u_gt, reverse=True)
            ws = [jnp.exp(lb + later) for lb, (later, _) in zip(lbs, sums)]
            if masked:
                ws = [jnp.where(mask, w, 0.0) for w in ws]
            pvs = [_dg(w, vb, _NN) for w in ws]
            return tuple((r, c[1] + pv) for (_, r), c, pv in zip(sums, carry, pvs))

        def more(c):
            alive = jnp.max(c[1][0][0])
            for hh in heads[1:]:
                alive = jnp.maximum(alive, jnp.max(c[1][hh][0]))
            return jnp.logical_and(c[0] >= 0, alive > SB_UNDERFLOW)

        start = tuple((jnp.zeros((TQ, 1), F32), jnp.zeros((TQ, LANES), F32)) for _ in heads)
        near_blk = jnp.maximum(tq - (near // TK - 1), 0)
        wb, done = lax.while_loop(more, lambda c: (c[0] - 1, block(c[0], TK, c[1], False)),
                                  (near_blk - 1, block(near_blk, near, start, True)))
        first = (wb + 1).astype(F32)
        out = jnp.zeros((TQ, LANES), F32)
        tot = jnp.zeros((TQ, LANES), F32)
        for hh in heads:
            r, acc = done[hh]
            out = out + jnp.where(hms[hh], acc, 0.0)
            tot = tot + jnp.where(hms[hh], jnp.where(lane % SB_HEAD_DIM == 1, first, r), 0.0)
        o_ref[...] = out
        t_ref[...] = tot
        if exchange:
            exchange.after_work(mine, jnp.logical_and(pair == npair - 1, tq == nq - 1), mid_done=True)

    tile = pl.BlockSpec((TQ, LANES), lambda p, t: (t, p))
    full = pl.BlockSpec((s, LANES), lambda p, t: (0, p))
    in_specs, out_specs = [tile, full, full], [tile, tile]
    out_shape = [jax.ShapeDtypeStruct((s, SB_DIM), F32)] * 2
    args, scratch = [q, k, v], []
    if exchange:
        x_in, x_out, scratch = exchange.specs()
        in_specs, out_specs = in_specs + x_in, out_specs + x_out
        out_shape, args = out_shape + exchange.outputs, args + exchange.inputs
    return pl.pallas_call(
        body, name="sb_fwd" + tag, grid=(npair, nq), in_specs=in_specs, out_specs=out_specs, out_shape=out_shape,
        scratch_shapes=scratch, compiler_params=_params(("arbitrary", "arbitrary")),
    )(*args)


def _sb_bwd(tag, q, k, v, tot, do, exchange=None):
    s = q.shape[0]
    npair = SB_DIM // LANES
    nq = s // TQ
    near = min(TK_NEAR, s)
    n_xi = len(exchange.inputs) if exchange else 0

    def body(*refs):
        q_ref, k_ref, v_ref, t_ref, do_ref = refs[:5]
        dq_ref, dk_ref, dv_ref = refs[5 + n_xi:8 + n_xi]
        tq = pl.program_id(1)
        if exchange:
            pair = pl.program_id(0)
            mine = exchange.pick(refs, 5, 3)
            exchange.before_work(mine, jnp.logical_and(pair == 0, tq == 0),
                                 jnp.logical_and(pair == npair - 1, tq == 0))

        @pl.when(tq == 0)
        def _():
            dk_ref[...] = jnp.zeros_like(dk_ref)
            dv_ref[...] = jnp.zeros_like(dv_ref)

        qp = q_ref[...]
        dop = do_ref[...]
        totp = t_ref[...]
        lane = lax.broadcasted_iota(jnp.int32, (1, LANES), 1)
        u_le = _tri(lambda j, s: j <= s)
        u_lt = _tri(lambda j, s: j < s)
        heads = range(LANES // SB_HEAD_DIM)
        hms = [(lane // SB_HEAD_DIM) == hh for hh in heads]
        qhs = [jnp.where(hm, qp, jnp.zeros_like(qp)) * SB_SCALE for hm in hms]
        dohs = [jnp.where(hm, dop, 0.0).astype(BF16) for hm in hms]
        totals = [jnp.sum(jnp.where(lane == hh * SB_HEAD_DIM, totp, 0.0), axis=1, keepdims=True) for hh in heads]
        first = jnp.max(jnp.where(lane == 1, totp, 0.0)).astype(jnp.int32)

        def block(wb, width, carry, masked):
            off = pl.multiple_of(wb * TK, TK)
            kb = k_ref[pl.ds(off, width), :]
            vb = v_ref[pl.ds(off, width), :]
            lbs, lss, mask = _sb_logits(qhs, kb, tq * TQ, wb * TK, masked)
            dws = [_dg(doh, vb, _NT) for doh in dohs]
            pres = _running_sums(lss, [c[0] for c in carry], u_le, reverse=False)
            ws = [jnp.exp(lb + (total - before)) for lb, total, (before, _) in zip(lbs, totals, pres)]
            if masked:
                ws = [jnp.where(mask, w, 0.0) for w in ws]
            gs = [w * dw for w, dw in zip(ws, dws)]
            lefts = _running_sums(gs, [c[1] for c in carry], u_lt, reverse=False, two_terms=False)
            dzs = [g - jnp.exp(lb) * (g + g_left) for g, lb, (g_left, _) in zip(gs, lbs, lefts)]
            if masked:
                dzs = [jnp.where(mask, dz, 0.0) for dz in dzs]
            dzbs = [dz.astype(BF16) for dz in dzs]
            dks = [_dg(dzb, qh, _TN) for dzb, qh in zip(dzbs, qhs)]
            dvs = [_dg(w, doh, _TN) for w, doh in zip(ws, dohs)]
            dqs = [_dg(dzb, kb, _NN) for dzb in dzbs]
            dk_ref[pl.ds(off, width), :] += functools.reduce(jnp.add, dks)
            dv_ref[pl.ds(off, width), :] += functools.reduce(jnp.add, dvs)
            return tuple((pre, gc, c[2] + dq) for (_, pre), (_, gc), c, dq in zip(pres, lefts, carry, dqs))

        zero = jnp.zeros((TQ, 1), F32)
        start = tuple((zero, zero, jnp.zeros((TQ, LANES), F32)) for _ in heads)
        near_blk = jnp.maximum(tq - (near // TK - 1), 0)
        far = lax.fori_loop(first, near_blk, lambda j, c: block(j, TK, c, False), start)
        done = block(near_blk, near, far, True)
        dq = jnp.zeros((TQ, LANES), F32)
        for hh in heads:
            dq = dq + jnp.where(hms[hh], done[hh][2], 0.0)
        dq_ref[...] = dq * SB_SCALE
        if exchange:
            exchange.after_work(mine, jnp.logical_and(pair == npair - 1, tq == nq - 1), mid_done=True)

    tile = pl.BlockSpec((TQ, LANES), lambda p, t: (t, p))
    full = pl.BlockSpec((s, LANES), lambda p, t: (0, p))
    in_specs, out_specs = [tile, full, full, tile, tile], [tile, full, full]
    out_shape = [jax.ShapeDtypeStruct((s, SB_DIM), F32)] * 3
    args, scratch = [q, k, v, tot, do], []
    if exchange:
        x_in, x_out, scratch = exchange.specs()
        in_specs, out_specs = in_specs + x_in, out_specs + x_out
        out_shape, args = out_shape + exchange.outputs, args + exchange.inputs
    return pl.pallas_call(
        body, name="sb_bwd" + tag, grid=(npair, nq), in_specs=in_specs, out_specs=out_specs, out_shape=out_shape,
        scratch_shapes=scratch, compiler_params=_params(("arbitrary", "arbitrary")),
    )(*args)


def _out_tile(y_ssd, o, sb_g, w_out):
    y_all = jnp.concatenate([y_ssd, _rms(o, sb_g)], axis=1)
    return mm_nn(y_all, w_out)


def _out_fwd(tag, h, y_ssd, o, sb_g, w_out):
    return _rowcall("out_fwd" + tag, lambda h, y, o, g, w: (h + _out_tile(y, o, g, w),),
                    [h, y_ssd, o], [sb_g, w_out], [(D_MODEL, F32)], tm=512)[0]


def _out_bwd(tag, y_ssd, o, dh, sb_g, w_out):
    def fn(y, o, dh, g, w):
        _, vjp = jax.vjp(_out_tile, y, o, g, w.astype(F32))
        return vjp(dh)

    return _rowcall("out_bwd" + tag, fn, [y_ssd, o, dh], [sb_g, w_out], [(SSD_DIM, F32), (SB_DIM, F32)],
                    [((1, SB_DIM), F32), ((D_MODEL, D_MODEL), F32)], tm=256)


def _mem_tile(mem, g, w_k, w_v):
    m = _rms(mem, g)
    return mm_nn(m, w_k), mm_nn(m, w_v)


def _mem_fwd(tag, mem, g, w_k, w_v):
    return _rowcall("mem_fwd" + tag, _mem_tile, [mem], [g, w_k, w_v], [(XA_DIM, F32), (XA_DIM, F32)], tm=256)


def _mem_bwd(tag, mem, dkx, dvx, g, w_k, w_v):
    def fn(mem, dkx, dvx, g, w_k, w_v):
        _, vjp = jax.vjp(lambda g, a, b: _mem_tile(mem, g, a, b), g, w_k.astype(F32), w_v.astype(F32))
        return vjp((dkx, dvx))

    return _rowcall("mem_bwd" + tag, fn, [mem, dkx, dvx], [g, w_k, w_v], [],
                    [((1, D_MODEL), F32), ((D_MODEL, XA_DIM), F32), ((D_MODEL, XA_DIM), F32)], tm=256)


def _xattn_tile(h, g, w_q, kx, vx, w_o):
    q = mm_nn(_rms(h, g), w_q)
    scale = 1.0 / math.sqrt(XA_HEAD_DIM)
    outs = []
    for i in range(XA_HEADS):
        sl = slice(i * XA_HEAD_DIM, (i + 1) * XA_HEAD_DIM)
        p = jax.nn.softmax(mm_nt(q[:, sl], kx[:, sl]) * scale, axis=-1)
        outs.append(mm_nn(p, vx[:, sl]))
    return mm_nn(jnp.concatenate(outs, axis=1), w_o)


def _xattn_fwd(tag, h, g, w_q, kx, vx, w_o):
    return _rowcall("xattn_fwd" + tag, lambda h, g, wq, kx, vx, wo: (h + _xattn_tile(h, g, wq, kx, vx, wo),),
                    [h], [g, w_q, kx, vx, w_o], [(D_MODEL, F32)], tm=512)[0]


def _xattn_bwd(tag, h, dh_out, g, w_q, kx, vx, w_o, exchange=None):
    def fn(h, dh_out, g, w_q, kx, vx, w_o):
        _, vjp = jax.vjp(_xattn_tile, h, g, w_q.astype(F32), kx, vx, w_o.astype(F32))
        dh, dg, dwq, dkx, dvx, dwo = vjp(dh_out)
        return dh_out + dh, dg, dwq, dkx, dvx, dwo

    mlen = kx.shape[0]
    return _rowcall("xattn_bwd" + tag, fn, [h, dh_out], [g, w_q, kx, vx, w_o], [(D_MODEL, F32)],
                    [((1, D_MODEL), F32), ((D_MODEL, XA_DIM), F32), ((mlen, XA_DIM), F32), ((mlen, XA_DIM), F32),
                     ((XA_DIM, D_MODEL), F32)], tm=256, exchange=exchange)


def _mlp_fwd(tag, h, g, w1, w2, exchange=None):
    def fn(h, g, w1, w2):
        r = jnp.maximum(mm_nn(_rms(h, g), w1), 0.0)
        return h + mm_nn(r * r, w2), r

    return _rowcall("mlp_fwd" + tag, fn, [h], [g, w1, w2], [(D_MODEL, F32), (D_FF, BF16)], tm=256,
                    exchange=exchange)


def _mlp_bwd(tag, h, relu, dh_out, g, w1, w2, exchange=None):
    def fn(h, relu, dh_out, g, w1, w2):
        hn, vjp = jax.vjp(_rms, h, g)
        r = relu.astype(F32)
        dob = dh_out.astype(BF16)
        dp = mm_nt(dob, w2) * (2.0 * r)
        dh, dg = vjp(mm_nt(dp, w1))
        return dh_out + dh, hn, dp, r * r, dob, dg

    return _rowcall("mlp_bwd" + tag, fn, [h, relu, dh_out], [g, w1, w2],
                    [(D_MODEL, F32), (D_MODEL, BF16), (D_FF, BF16), (D_FF, BF16), (D_MODEL, BF16)],
                    [((1, D_MODEL), F32)], tm=256, exchange=exchange)


def _head(h, g, target):
    def lossfn(h, g, t):
        err = jnp.square(_rms(h, g) - t)
        return 0.5 * jnp.sum(jnp.mean(err, axis=-1))

    def fn(h, t, g):
        loss, vjp = jax.vjp(lambda h, g: lossfn(h, g, t), h, g)
        dh, dg = vjp(jnp.ones((), F32))
        return dh, jnp.full((1, LANES), loss, F32), dg

    return _rowcall("head", fn, [h, target], [g], [(D_MODEL, F32)], [((1, LANES), F32), ((1, D_MODEL), F32)], tm=512)


def _row(v):
    return v.reshape(1, -1)


class LocalPlan:
    def __init__(self, mats):
        self.mats = mats

    def weights(self, l):
        return self.mats[l]

    def carried_by(self, kernel, l):
        return None

    def carried_out(self, kernel, l, outs):
        pass

    def mlp_grads_done(self, l, gm):
        pass

    def mixer_grads_done(self, l, gm):
        pass

    def grads_done(self, l, gm):
        pass


def _local_step(x, mem, target, sw, plan):
    h = x
    saved = []
    for l in range(DEPTH):
        tag = str(l)
        m = plan.weights(l)
        z, xbc, q, k, v, dtr, *carried = _proj_fwd(tag, h, _row(sw["norm_mix_g"][l]), m["w_in"],
                                                   exchange=plan.carried_by("proj_fwd", l))
        plan.carried_out("proj_fwd", l, carried)
        xact = _conv_fwd(tag, xbc, m["conv_w"], _row(sw["conv_b"][l]))
        y_ssd, states, *carried = _ssd_fwd(tag, xact, dtr, z, _row(sw["dt_bias"][l]), _row(sw["a_log"][l]),
                                           _row(sw["d_skip"][l]), _row(sw["ssd_norm_g"][l]),
                                           exchange=plan.carried_by("ssd_fwd", l))
        plan.carried_out("ssd_fwd", l, carried)
        o, sb_tot, *carried = _sb_fwd(tag, q, k, v, exchange=plan.carried_by("sb_fwd", l))
        plan.carried_out("sb_fwd", l, carried)
        h1 = _out_fwd(tag, h, y_ssd, o, _row(sw["sb_norm_g"][l]), m["w_out"])
        kx, vx = _mem_fwd(tag, mem, _row(sw["norm_mem_g"][l]), m["w_xk"], m["w_xv"])
        h2 = _xattn_fwd(tag, h1, _row(sw["norm_xa_g"][l]), m["w_xq"], kx, vx, m["w_xo"])
        h3, relu, *carried = _mlp_fwd(tag, h2, _row(sw["norm_ff_g"][l]), m["w_ff1"], m["w_ff2"],
                                      exchange=plan.carried_by("mlp_fwd", l))
        plan.carried_out("mlp_fwd", l, carried)
        saved.append((h, z, xbc, q, k, v, dtr, xact, y_ssd, states, o, sb_tot, h1, kx, vx, h2, relu))
        h = h3

    dh, loss, d_final = _head(h, _row(sw["final_g"]), target)
    gm = [dict() for _ in range(DEPTH)]
    gs = {name: [None] * DEPTH for name in ("norm_mix_g", "conv_w", "conv_b", "dt_bias", "a_log", "d_skip",
                                            "ssd_norm_g", "sb_norm_g", "norm_xa_g", "norm_mem_g", "norm_ff_g")}
    for l in reversed(range(DEPTH)):
        tag = str(l)
        m = plan.weights(l)
        h0, z, xbc, q, k, v, dtr, xact, y_ssd, states, o, sb_tot, h1, kx, vx, h2, relu = saved[l]
        dh2, hn_b, dp_b, a_b, do_b, gs["norm_ff_g"][l], *carried = _mlp_bwd(
            tag, h2, relu, dh, _row(sw["norm_ff_g"][l]), m["w_ff1"], m["w_ff2"],
            exchange=plan.carried_by("mlp_bwd", l))
        plan.carried_out("mlp_bwd", l, carried)
        gm[l]["w_ff1"] = _mm_tn_parts("dw_ff1" + tag, hn_b, dp_b, True)
        gm[l]["w_ff2"] = _mm_tn_parts("dw_ff2" + tag, a_b, do_b, False)
        plan.mlp_grads_done(l, gm[l])
        dh1, gs["norm_xa_g"][l], gm[l]["w_xq"], dkx, dvx, gm[l]["w_xo"], *carried = _xattn_bwd(
            tag, h1, dh2, _row(sw["norm_xa_g"][l]), m["w_xq"], kx, vx, m["w_xo"],
            exchange=plan.carried_by("xattn_bwd", l))
        plan.carried_out("xattn_bwd", l, carried)
        gs["norm_mem_g"][l], gm[l]["w_xk"], gm[l]["w_xv"] = _mem_bwd(
            tag, mem, dkx, dvx, _row(sw["norm_mem_g"][l]), m["w_xk"], m["w_xv"])
        dy_ssd, do, gs["sb_norm_g"][l], gm[l]["w_out"] = _out_bwd(
            tag, y_ssd, o, dh1, _row(sw["sb_norm_g"][l]), m["w_out"])
        plan.mixer_grads_done(l, gm[l])
        dq, dk, dv, *carried = _sb_bwd(tag, q, k, v, sb_tot, do, exchange=plan.carried_by("sb_bwd", l))
        plan.carried_out("sb_bwd", l, carried)
        dxact, ddtr, dz, gs["dt_bias"][l], gs["a_log"][l], gs["d_skip"][l], gs["ssd_norm_g"][l], *carried = _ssd_bwd(
            tag, xact, dtr, z, states, dy_ssd, _row(sw["dt_bias"][l]), _row(sw["a_log"][l]),
            _row(sw["d_skip"][l]), _row(sw["ssd_norm_g"][l]), exchange=plan.carried_by("ssd_bwd", l))
        plan.carried_out("ssd_bwd", l, carried)
        dxbc, gs["conv_w"][l], gs["conv_b"][l] = _conv_bwd(tag, xbc, dxact, m["conv_w"], _row(sw["conv_b"][l]))
        dh, hn_b, dp_b, gs["norm_mix_g"][l] = _proj_bwd(
            tag, h0, dh1, dz, dxbc, dq, dk, dv, ddtr, _row(sw["norm_mix_g"][l]), m["w_in"])
        gm[l]["w_in"] = _mm_tn_call("dw_in" + tag, hn_b, dp_b, 512, IN_PAD, 1024)
        plan.grads_done(l, gm[l])
    gs["final_g"] = d_final
    return loss, dh, gm, gs


ANY = pl.BlockSpec(memory_space=pl.ANY)
VMEM_SPEC = pl.BlockSpec(memory_space=pltpu.VMEM)


def _place():
    return lax.axis_index("x"), lax.axis_index("y"), lax.axis_index("c")


def _other_chips(x, y):
    return [(1 - x, y), (x, 1 - y), (1 - x, 1 - y)]


def _remote(send_sems, recv_sems, idx, src, dst, to):
    return pltpu.make_async_remote_copy(src_ref=src, dst_ref=dst, send_sem=send_sems.at[idx],
                                        recv_sem=recv_sems.at[idx], device_id=to, device_id_type=MESH)


def _run_exchange(name, ex):
    def body(*refs):
        mine = ex.pick(refs, 0, 0)
        ex.start(*mine)
        if ex.mid is not None:
            ex.mid(*mine)
        ex.end(*mine)

    x_in, x_out, scratch = ex.specs()
    return pl.pallas_call(body, name=name, in_specs=x_in, out_specs=x_out, out_shape=ex.outputs,
                          scratch_shapes=scratch)(*ex.inputs)


def _gather_exchange(layer, shards):
    n = len(shards)
    outs = [jax.ShapeDtypeStruct((N_CHIPS,) + s.shape[1:], s.dtype) for s in shards]

    def start(w_refs, o_refs, ss, rs):
        x, y, c = _place()

        @pl.when(c == layer)
        def _():
            for i in range(n):
                for kk, (cx, cy) in enumerate(_other_chips(x, y)):
                    _remote(ss, rs, 6 * i + kk, w_refs[i].at[layer], o_refs[i].at[2 * x + y], (cx, cy, layer)).start()

    def mid(w_refs, o_refs, ss, rs):
        x, y, c = _place()

        @pl.when(c == layer)
        def _():
            for i in range(n):
                for kk, (cx, cy) in enumerate(_other_chips(x, y)):
                    got = o_refs[i].at[2 * cx + cy]
                    _remote(ss, rs, 6 * i + kk, got, got, (x, y, c)).wait_recv()
                    _remote(ss, rs, 6 * i + 3 + kk, got, got, (x, y, 1 - layer)).start()

    def end(w_refs, o_refs, ss, rs):
        x, y, c = _place()
        for i in range(n):
            for kk, (cx, cy) in enumerate(_other_chips(x, y)):
                got = o_refs[i].at[2 * cx + cy]

                @pl.when(c == layer)
                def _():
                    _remote(ss, rs, 6 * i + kk, w_refs[i].at[layer], got, (x, y, c)).wait_send()
                    _remote(ss, rs, 6 * i + 3 + kk, got, got, (x, y, c)).wait_send()

                @pl.when(c != layer)
                def _():
                    _remote(ss, rs, 6 * i + 3 + kk, got, got, (x, y, c)).wait_recv()

    return Exchange(shards, outs, 6 * n, start, end, mid)


def _handover_exchange(layer, grads):
    n = len(grads)
    outs = [jax.ShapeDtypeStruct(g.shape, g.dtype) for g in grads]

    def start(g_refs, o_refs, ss, rs):
        x, y, c = _place()

        @pl.when(c != layer)
        def _():
            for i in range(n):
                _remote(ss, rs, i, g_refs[i], o_refs[i], (x, y, layer)).start()

    def end(g_refs, o_refs, ss, rs):
        x, y, c = _place()
        for i in range(n):
            @pl.when(c != layer)
            def _():
                _remote(ss, rs, i, g_refs[i], o_refs[i], (x, y, c)).wait_send()

            @pl.when(c == layer)
            def _():
                _remote(ss, rs, i, g_refs[i], o_refs[i], (x, y, c)).wait_recv()

    return Exchange(grads, outs, n, start, end)


def _scatter_exchange(layer, parts):
    n = len(parts)
    outs = [jax.ShapeDtypeStruct(p.shape, p.dtype) for p in parts]

    def start(s_refs, o_refs, ss, rs):
        x, y, c = _place()

        @pl.when(c == layer)
        def _():
            for i in range(n):
                for kk, (cx, cy) in enumerate(_other_chips(x, y)):
                    _remote(ss, rs, 3 * i + kk, s_refs[i].at[2 * cx + cy], o_refs[i].at[2 * x + y],
                            (cx, cy, layer)).start()

    def end(s_refs, o_refs, ss, rs):
        x, y, c = _place()

        @pl.when(c == layer)
        def _():
            for i in range(n):
                for kk, (cx, cy) in enumerate(_other_chips(x, y)):
                    got = o_refs[i].at[2 * cx + cy]
                    _remote(ss, rs, 3 * i + kk, got, got, (x, y, c)).wait_recv()
            for i in range(n):
                for kk, (cx, cy) in enumerate(_other_chips(x, y)):
                    _remote(ss, rs, 3 * i + kk, s_refs[i].at[2 * cx + cy], o_refs[i].at[2 * x + y],
                            (x, y, c)).wait_send()

    return Exchange(parts, outs, 3 * n, start, end)


def _return_exchange(reduced):
    flat = [g for layer in range(DEPTH) for g in reduced[layer]]
    n = len(reduced[0])
    outs = [jax.ShapeDtypeStruct(g.shape, g.dtype) for g in flat]

    def start(g_refs, o_refs, ss, rs):
        x, y, c = _place()
        for layer in range(DEPTH):
            @pl.when(c == layer)
            def _():
                for i in range(n):
                    k = layer * n + i
                    _remote(ss, rs, k, g_refs[k], o_refs[k], (x, y, 1 - layer)).start()

    def end(g_refs, o_refs, ss, rs):
        x, y, c = _place()
        for layer in range(DEPTH):
            for i in range(n):
                k = layer * n + i

                @pl.when(c == layer)
                def _():
                    _remote(ss, rs, k, g_refs[k], o_refs[k], (x, y, c)).wait_send()

                @pl.when(c != layer)
                def _():
                    _remote(ss, rs, k, g_refs[k], o_refs[k], (x, y, c)).wait_recv()

    return Exchange(flat, outs, DEPTH * n, start, end)


def _gather_small(tag, buf):
    shape = buf.shape

    def body(b_ref, o_ref, sum_ref, send_sems, recv_sems, local_sem):
        x, y, c = _place()
        me = 4 * x + 2 * y + c
        mine = pltpu.make_async_copy(b_ref, o_ref.at[me], local_sem)
        mine.start()
        flips = [(dx, dy, dc) for dx in (0, 1) for dy in (0, 1) for dc in (0, 1) if (dx, dy, dc) != (0, 0, 0)]
        sends = []

        def peer(dx, dy, dc):
            return (1 - x if dx else x, 1 - y if dy else y, 1 - c if dc else c)

        for kk, flip in enumerate(flips):
            cp = pltpu.make_async_remote_copy(src_ref=b_ref, dst_ref=o_ref.at[me], send_sem=send_sems.at[kk],
                                              recv_sem=recv_sems.at[kk], device_id=peer(*flip), device_id_type=MESH)
            cp.start()
            sends.append(cp)
        for kk, flip in enumerate(flips):
            px, py, pc = peer(*flip)
            frm = 4 * px + 2 * py + pc
            pltpu.make_async_remote_copy(src_ref=b_ref, dst_ref=o_ref.at[frm], send_sem=send_sems.at[kk],
                                         recv_sem=recv_sems.at[kk], device_id=(x, y, c),
                                         device_id_type=MESH).wait_recv()
        for cp in sends:
            cp.wait_send()
        mine.wait()
        total = o_ref[0]
        for d in range(1, N_DEV):
            total = total + o_ref[d]
        sum_ref[...] = total

    return pl.pallas_call(
        body, name="gather_small" + tag, in_specs=[VMEM_SPEC], out_specs=[VMEM_SPEC, VMEM_SPEC],
        out_shape=[jax.ShapeDtypeStruct((N_DEV,) + shape, buf.dtype), jax.ShapeDtypeStruct(shape, buf.dtype)],
        scratch_shapes=[pltpu.SemaphoreType.DMA((N_DEV - 1,)), pltpu.SemaphoreType.DMA((N_DEV - 1,)),
                        pltpu.SemaphoreType.DMA],
    )(buf)


def _add_handed(tag, layer, g, r, tr=256):
    _, rows, cols = g.shape
    tr = min(tr, rows)

    def body(g_ref, r_ref, o_ref):
        @pl.when(lax.axis_index("c") == layer)
        def _():
            o_ref[...] = (g_ref[...].astype(F32) + r_ref[...].astype(F32)).astype(o_ref.dtype)

    spec = pl.BlockSpec((1, tr, cols), lambda p, i: (p, i, 0))
    return pl.pallas_call(
        body, name="add_handed_" + tag, grid=(N_CHIPS, rows // tr), in_specs=[spec, spec], out_specs=spec,
        out_shape=jax.ShapeDtypeStruct(g.shape, BF16), compiler_params=_params(("arbitrary", "arbitrary")),
    )(g, r)


def _sum_parts(tag, layer, own, parts, tr=256):
    _, rows, cols = parts.shape
    tr = min(tr, rows)
    chip = (2 * lax.axis_index("x") + lax.axis_index("y")).astype(jnp.int32).reshape(1)

    def body(c_ref, own_ref, p1_ref, p2_ref, p3_ref, o_ref):
        @pl.when(lax.axis_index("c") == layer)
        def _():
            total = own_ref[0].astype(F32)
            for p_ref in (p1_ref, p2_ref, p3_ref):
                total = total + p_ref[0].astype(F32)
            o_ref[...] = total

    def after(kk):
        return pl.BlockSpec((1, tr, cols), lambda i, c_ref: ((c_ref[0] + kk) % N_CHIPS, i, 0))

    return pl.pallas_call(
        body, name="sum_parts_" + tag,
        grid_spec=pltpu.PrefetchScalarGridSpec(
            num_scalar_prefetch=1, grid=(rows // tr,), in_specs=[after(0), after(1), after(2), after(3)],
            out_specs=pl.BlockSpec((tr, cols), lambda i, c_ref: (i, 0))),
        out_shape=jax.ShapeDtypeStruct((rows, cols), F32),
        compiler_params=_params(("arbitrary",)),
    )(chip, own, parts, parts, parts)


def _adamw_math(w, g, m, v):
    m = ADAM_B1 * m + (1.0 - ADAM_B1) * g
    v = ADAM_B2 * v + (1.0 - ADAM_B2) * jnp.square(g)
    m_hat = m / (1.0 - ADAM_B1 ** ADAM_STEP)
    v_hat = v / (1.0 - ADAM_B2 ** ADAM_STEP)
    delta = -ADAM_LR * (m_hat / (jnp.sqrt(v_hat) + ADAM_EPS) + ADAM_WD * w)
    return delta, m, v


def _adamw(tag, w, computed, received, m, v, tr=256):
    _, rows, cols = w.shape
    tr = min(tr, rows)

    def body(*refs):
        w_ref, m_ref, v_ref = refs[:3]
        g_refs = refs[3:3 + 2 * DEPTH]
        g_ref, d_ref, nm_ref, nv_ref = refs[3 + 2 * DEPTH:]
        layer = pl.program_id(0)
        core = lax.axis_index("c")
        g = jnp.zeros((tr, cols), F32)
        for l in range(DEPTH):
            mine = jnp.where(core == l, g_refs[2 * l][...], g_refs[2 * l + 1][...])
            g = jnp.where(layer == l, mine, g)
        g_ref[0] = g
        d_ref[0], nm_ref[0], nv_ref[0] = _adamw_math(w_ref[0], g, m_ref[0], v_ref[0])

    stacked = pl.BlockSpec((1, tr, cols), lambda l, i: (l, i, 0))

    def of_layer(k):
        return pl.BlockSpec((tr, cols), lambda l, i: (jnp.where(l == k, i, 0), 0))

    g_specs = [of_layer(l) for l in range(DEPTH) for _ in range(2)]
    g_args = [a for l in range(DEPTH) for a in (computed[l], received[l])]
    return pl.pallas_call(
        body, name="adamw_" + tag, grid=(DEPTH, rows // tr), in_specs=[stacked] * 3 + g_specs,
        out_specs=[stacked] * 4, out_shape=[jax.ShapeDtypeStruct(w.shape, F32)] * 4,
        compiler_params=_params(("arbitrary", "arbitrary")),
    )(w, m, v, *g_args)


def _w_in_to_padded(w):
    d0 = SSD_DIM + CONV_DIM
    return jnp.concatenate([w[:, :d0], w[:, d0 + SSD_HEADS:], w[:, d0:d0 + SSD_HEADS],
                            jnp.zeros((w.shape[0], DT_PAD - SSD_HEADS), w.dtype)], axis=1)


def _w_in_from_padded(w):
    d0 = SSD_DIM + CONV_DIM
    return jnp.concatenate([w[:, :d0], w[:, DT_OFF:DT_OFF + SSD_HEADS], w[:, d0:DT_OFF]], axis=1)


def _small_layout():
    return (("norm_mix_g", 0, 0, D_MODEL), ("norm_xa_g", 1, 0, D_MODEL), ("norm_mem_g", 2, 0, D_MODEL),
            ("norm_ff_g", 3, 0, D_MODEL), ("conv_b", 4, 0, CONV_DIM), ("ssd_norm_g", 5, 0, SSD_DIM),
            ("sb_norm_g", 5, SSD_DIM, SB_DIM), ("dt_bias", 6, 0, SSD_HEADS), ("a_log", 6, LANES, SSD_HEADS),
            ("d_skip", 6, 2 * LANES, SSD_HEADS))


def _pack_small(gs, loss):
    lay = _small_layout()
    args = [gs[name][l] for l in range(DEPTH) for name, _, _, _ in lay]
    args += [gs["conv_w"][l] for l in range(DEPTH)] + [gs["final_g"], loss]
    n_lay = len(lay)

    def body(*refs):
        o_ref = refs[-1]
        o_ref[...] = jnp.zeros_like(o_ref)
        for l in range(DEPTH):
            for i, (_, rr, c0, width) in enumerate(lay):
                row = l * _SM_PER_LAYER + rr
                o_ref[row:row + 1, c0:c0 + width] = refs[l * n_lay + i][...]
            row = _SM_CONVW + l * CONV_K
            o_ref[row:row + CONV_K, 0:CONV_DIM] = refs[DEPTH * n_lay + l][...]
        o_ref[_SM_FINAL:_SM_FINAL + 1, :] = refs[DEPTH * n_lay + DEPTH][...]
        o_ref[_SM_LOSS:_SM_LOSS + 1, 0:LANES] = refs[DEPTH * n_lay + DEPTH + 1][...]

    return pl.pallas_call(
        body, name="pack_small", in_specs=[VMEM_SPEC] * len(args), out_specs=VMEM_SPEC,
        out_shape=jax.ShapeDtypeStruct((SMALL_ROWS, PACK_COLS), F32),
    )(*args)


def _small_update(buf, w, mom, var):
    lay = _small_layout()
    names = [name for name, _, _, _ in lay] + ["final_g", "conv_w"]
    conv_cols = CONV_DIM // N_CHIPS
    shapes2d = {name: (DEPTH, width) for name, _, _, width in lay}
    shapes2d["final_g"] = (1, D_MODEL)
    shapes2d["conv_w"] = (DEPTH * CONV_K, conv_cols)
    args = [buf]
    for src in (w, mom, var):
        args += [src[name].reshape(shapes2d[name]) for name in names]
    n = len(names)

    def body(*refs):
        b_ref = refs[0]
        w_refs, m_refs, v_refs = refs[1:1 + n], refs[1 + n:1 + 2 * n], refs[1 + 2 * n:1 + 3 * n]
        outs = refs[1 + 3 * n:]
        chip = 2 * lax.axis_index("x") + lax.axis_index("y")
        for i, name in enumerate(names):
            if name == "final_g":
                g = b_ref[_SM_FINAL:_SM_FINAL + 1, :]
            elif name == "conv_w":
                rows = b_ref[_SM_CONVW:_SM_CONVW + DEPTH * CONV_K, 0:CONV_DIM]
                g = jnp.zeros((DEPTH * CONV_K, conv_cols), F32)
                for j in range(N_CHIPS):
                    g = g + jnp.where(chip == j, rows[:, j * conv_cols:(j + 1) * conv_cols], 0.0)
            else:
                _, rr, c0, width = lay[i]
                g = jnp.concatenate([b_ref[l * _SM_PER_LAYER + rr:l * _SM_PER_LAYER + rr + 1, c0:c0 + width]
                                     for l in range(DEPTH)], axis=0)
            d, m2, v2 = _adamw_math(w_refs[i][...], g, m_refs[i][...], v_refs[i][...])
            outs[i][...] = g
            outs[n + i][...] = d
            outs[2 * n + i][...] = m2
            outs[3 * n + i][...] = v2

    out_shape = [jax.ShapeDtypeStruct(shapes2d[name], F32) for _ in range(4) for name in names]
    res = pl.pallas_call(
        body, name="small_update", in_specs=[VMEM_SPEC] * len(args), out_specs=[VMEM_SPEC] * (4 * n),
        out_shape=out_shape,
    )(*args)
    return tuple({name: res[k * n + i].reshape(w[name].shape) for i, name in enumerate(names)} for k in range(4))


SMALL_NAMES = ("norm_mix_g", "conv_b", "dt_bias", "a_log", "d_skip", "ssd_norm_g", "sb_norm_g", "norm_xa_g",
               "norm_mem_g", "norm_ff_g", "final_g")
WEIGHT_ORDER = ("norm_mix_g", "w_in", "conv_w", "conv_b", "dt_bias", "a_log", "d_skip", "ssd_norm_g", "sb_norm_g",
                "w_out", "norm_xa_g", "norm_mem_g", "w_xq", "w_xk", "w_xv", "w_xo", "norm_ff_g", "w_ff1", "w_ff2",
                "final_g")


_ALL = tuple(range(len(MATS)))
_IN = tuple(i for i in _ALL if MATS[i][0] == "w_in")
_MLP = tuple(i for i in _ALL if MATS[i][0] in ("w_ff1", "w_ff2"))
_MIXER = tuple(i for i in _ALL if i not in _IN + _MLP)
_CONV = len(MATS)


class PipelinedPlan(LocalPlan):
    def __init__(self, shards, conv_w):
        self.shards = list(shards) + [conv_w]
        self.chip = 2 * lax.axis_index("x") + lax.axis_index("y")
        self.mats = [dict() for _ in range(DEPTH)]
        n = len(MATS)
        self.parts = [[None] * n for _ in range(DEPTH)]
        self.to_chips = [[None] * n for _ in range(DEPTH)]
        self.reduced = [[None] * n for _ in range(DEPTH)]
        self.riders = {}
        first = _IN + (_CONV,)
        self._gathered(0, first, _run_exchange("gather_first", self._gather(0, first)))
        self._gather_behind(0, _MIXER, "proj_fwd", 0)
        self._gather_behind(0, _MLP[:1], "ssd_fwd", 0)
        self._gather_behind(0, _MLP[1:], "sb_fwd", 0)
        for l in range(1, DEPTH):
            self._gather_behind(l, first, "sb_fwd", l - 1)
            self._gather_behind(l, _MIXER, "ssd_fwd", l)
            self._gather_behind(l, _MLP, "sb_fwd", l)

    def _ride(self, kernel, l, exchange, then):
        self.riders.setdefault((kernel, l), []).append((exchange, then))

    def carried_by(self, kernel, l):
        exchange = None
        for ex, _ in self.riders.get((kernel, l), []):
            exchange = _both(exchange, ex)
        return exchange

    def carried_out(self, kernel, l, outs):
        for ex, then in self.riders.pop((kernel, l), []):
            then(outs[:len(ex.outputs)])
            outs = outs[len(ex.outputs):]

    def _gather(self, l, which):
        return _gather_exchange(l, [self.shards[i] for i in which])

    def _gather_behind(self, l, which, kernel, host):
        self._ride(kernel, host, self._gather(l, which), lambda outs: self._gathered(l, which, outs))

    def _gathered(self, l, which, outs):
        for i, theirs in zip(which, outs):
            name, _, axis = MATS[i] if i != _CONV else ("conv_w", None, 1)
            full = jnp.concatenate([jnp.where(self.chip == j, self.shards[i][l], theirs[j]) for j in range(N_CHIPS)],
                                   axis=axis)
            self.mats[l][name] = _w_in_to_padded(full) if name == "w_in" else full

    def _set_parts(self, l, which, gm):
        for i in which:
            name, _, axis = MATS[i]
            g = _w_in_from_padded(gm[name]) if name == "w_in" else gm[name]
            if g.ndim == 2 and axis == 0:
                g = g.reshape((N_CHIPS, g.shape[0] // N_CHIPS, g.shape[1]))
            elif g.ndim == 2:
                g = jnp.swapaxes(g.reshape((g.shape[0], N_CHIPS, g.shape[1] // N_CHIPS)), 0, 1)
            self.parts[l][i] = g

    def _handover(self, l, which):
        return _handover_exchange(l, [self.parts[l][i] for i in which])

    def _handed(self, l, which, outs):
        for i, r in zip(which, outs):
            self.to_chips[l][i] = _add_handed(MATS[i][0] + str(l), l, self.parts[l][i], r)

    def _scatter(self, l, which):
        return _scatter_exchange(l, [self.to_chips[l][i] for i in which])

    def _scattered(self, l, which, outs):
        for i, got in zip(which, outs):
            self.reduced[l][i] = _sum_parts(MATS[i][0] + str(l), l, self.to_chips[l][i], got)

    def _send_behind(self, l, which, hand_kernel, cross_kernel, host):
        def handed(outs):
            self._handed(l, which, outs)
            self._ride(cross_kernel, host, self._scatter(l, which), lambda o: self._scattered(l, which, o))

        self._ride(hand_kernel, host, self._handover(l, which), handed)

    def mlp_grads_done(self, l, gm):
        self._set_parts(l, _MLP, gm)
        self._send_behind(l, _MLP, "xattn_bwd", "sb_bwd", l)

    def mixer_grads_done(self, l, gm):
        self._set_parts(l, _MIXER, gm)
        self._send_behind(l, _MIXER, "sb_bwd", "ssd_bwd", l)

    def grads_done(self, l, gm):
        self._set_parts(l, _IN, gm)
        if l > 0:
            self._send_behind(l, _IN, "mlp_bwd", "sb_bwd", l - 1)
        else:
            self._handed(0, _IN, _run_exchange("handover_last", self._handover(0, _IN)))
            self._scattered(0, _IN, _run_exchange("scatter_last", self._scatter(0, _IN)))

    def reduced_gradients(self):
        returned = _run_exchange("return_reduced", _return_exchange(self.reduced))
        n = len(MATS)
        return [([self.reduced[l][i] for l in range(DEPTH)], [returned[l * n + i] for l in range(DEPTH)])
                for i in range(n)]


def kernel(x, mem, norm_mix_g, w_in, conv_w, conv_b, dt_bias, a_log, d_skip, ssd_norm_g, sb_norm_g, w_out, norm_xa_g, norm_mem_g, w_xq, w_xk, w_xv, w_xo, norm_ff_g, w_ff1, w_ff2, final_g, loss_target, m_norm_mix_g, m_w_in, m_conv_w, m_conv_b, m_dt_bias, m_a_log, m_d_skip, m_ssd_norm_g, m_sb_norm_g, m_w_out, m_norm_xa_g, m_norm_mem_g, m_w_xq, m_w_xk, m_w_xv, m_w_xo, m_norm_ff_g, m_w_ff1, m_w_ff2, m_final_g, v_norm_mix_g, v_w_in, v_conv_w, v_conv_b, v_dt_bias, v_a_log, v_d_skip, v_ssd_norm_g, v_sb_norm_g, v_w_out, v_norm_xa_g, v_norm_mem_g, v_w_xq, v_w_xk, v_w_xv, v_w_xo, v_norm_ff_g, v_w_ff1, v_w_ff2, v_final_g):
    w = dict(norm_mix_g=norm_mix_g, w_in=w_in, conv_w=conv_w, conv_b=conv_b, dt_bias=dt_bias, a_log=a_log,
             d_skip=d_skip, ssd_norm_g=ssd_norm_g, sb_norm_g=sb_norm_g, w_out=w_out, norm_xa_g=norm_xa_g,
             norm_mem_g=norm_mem_g, w_xq=w_xq, w_xk=w_xk, w_xv=w_xv, w_xo=w_xo, norm_ff_g=norm_ff_g, w_ff1=w_ff1,
             w_ff2=w_ff2, final_g=final_g)
    mom = dict(norm_mix_g=m_norm_mix_g, w_in=m_w_in, conv_w=m_conv_w, conv_b=m_conv_b, dt_bias=m_dt_bias,
               a_log=m_a_log, d_skip=m_d_skip, ssd_norm_g=m_ssd_norm_g, sb_norm_g=m_sb_norm_g, w_out=m_w_out,
               norm_xa_g=m_norm_xa_g, norm_mem_g=m_norm_mem_g, w_xq=m_w_xq, w_xk=m_w_xk, w_xv=m_w_xv, w_xo=m_w_xo,
               norm_ff_g=m_norm_ff_g, w_ff1=m_w_ff1, w_ff2=m_w_ff2, final_g=m_final_g)
    var = dict(norm_mix_g=v_norm_mix_g, w_in=v_w_in, conv_w=v_conv_w, conv_b=v_conv_b, dt_bias=v_dt_bias,
               a_log=v_a_log, d_skip=v_d_skip, ssd_norm_g=v_ssd_norm_g, sb_norm_g=v_sb_norm_g, w_out=v_w_out,
               norm_xa_g=v_norm_xa_g, norm_mem_g=v_norm_mem_g, w_xq=v_w_xq, w_xk=v_w_xk, w_xv=v_w_xv, w_xo=v_w_xo,
               norm_ff_g=v_norm_ff_g, w_ff1=v_w_ff1, w_ff2=v_w_ff2, final_g=v_final_g)
    sw = {name: w[name] for name in SMALL_NAMES}
    plan = PipelinedPlan([w[name].astype(BF16) for name, _, _ in MATS], conv_w)
    loss, grad_x, gm, gs = _local_step(x[0], mem[0], loss_target[0], sw, plan)
    g_mats = plan.reduced_gradients()

    _, small_sum = _gather_small("_grads", _pack_small(gs, loss))
    loss_out = small_sum[_SM_LOSS, 0]

    grads, deltas, new_m, new_v = {}, {}, {}, {}
    for (name, _, _), (computed, received) in zip(MATS, g_mats):
        grads[name], deltas[name], new_m[name], new_v[name] = _adamw(
            name, w[name], computed, received, mom[name], var[name])
    g_s, d_s, m_s, v_s = _small_update(small_sum, w, mom, var)
    for name in g_s:
        grads[name], deltas[name], new_m[name], new_v[name] = g_s[name], d_s[name], m_s[name], v_s[name]

    return (loss_out, grad_x[None], *[grads[n] for n in WEIGHT_ORDER], *[deltas[n] for n in WEIGHT_ORDER],
            *[new_m[n] for n in WEIGHT_ORDER], *[new_v[n] for n in WEIGHT_ORDER])
```

```python
import functools
import math

import jax
import jax.numpy as jnp
from jax import lax
from jax.experimental import pallas as pl
from jax.experimental.pallas import tpu as pltpu

F32 = jnp.float32
BF16 = jnp.bfloat16
MESH = pl.DeviceIdType.MESH

D_MODEL = 1024
DEPTH = 2
SSD_DIM = 512
SSD_HEAD_DIM = 64
SSD_HEADS = 8
SSD_GROUPS = 2
SSD_STATE = 64
CONV_K = 4
CHUNK = 128
SB_DIM = 512
SB_HEAD_DIM = 64
XA_HEADS = 4
XA_HEAD_DIM = 128
XA_DIM = 512
D_FF = 4096
EPS = 1e-5
GN = SSD_GROUPS * SSD_STATE
CONV_DIM = SSD_DIM + 2 * GN
IN_DIM = SSD_DIM + CONV_DIM + SSD_HEADS + 3 * SB_DIM
LANES = 128
DT_PAD = LANES
IN_PAD = SSD_DIM + CONV_DIM + 3 * SB_DIM + DT_PAD
Q_OFF = SSD_DIM + CONV_DIM
DT_OFF = Q_OFF + 3 * SB_DIM
HALO = 8

ADAM_LR = 0.001
ADAM_B1 = 0.9
ADAM_B2 = 0.999
ADAM_EPS = 1e-08
ADAM_WD = 0.01
ADAM_STEP = 10

N_CHIPS = 4
N_DEV = 8
PACK_COLS = 1024
VMEM_LIMIT = 56 * 1024 * 1024

MATS = (
    ("w_in", (IN_DIM, D_MODEL), 0),
    ("w_out", (D_MODEL, D_MODEL), 0),
    ("w_xq", (D_MODEL, XA_DIM), 0),
    ("w_xk", (D_MODEL, XA_DIM), 0),
    ("w_xv", (D_MODEL, XA_DIM), 0),
    ("w_xo", (XA_DIM, D_MODEL), 1),
    ("w_ff1", (D_MODEL, D_FF), 1),
    ("w_ff2", (D_FF, D_MODEL), 0),
)


SMALL_ROWS = 24
_SM_PER_LAYER = 7
_SM_FINAL = 14
_SM_CONVW = 15
_SM_LOSS = 23


_NN = ((1,), (0,))
_NT = ((1,), (1,))
_TN = ((0,), (0,))


def _dg(a, b, dims):
    return lax.dot_general(a.astype(BF16), b.astype(BF16), (dims, ((), ())), preferred_element_type=F32)


@jax.custom_vjp
def mm_nn(a, b):
    return _dg(a, b, _NN)


@jax.custom_vjp
def mm_nt(a, b):
    return _dg(a, b, _NT)


@jax.custom_vjp
def mm_tn(a, b):
    return _dg(a, b, _TN)


def _nn_fwd(a, b):
    return _dg(a, b, _NN), (a, b)


def _nn_bwd(res, g):
    a, b = res
    return mm_nt(g, b).astype(a.dtype), mm_tn(a, g).astype(b.dtype)


def _nt_fwd(a, b):
    return _dg(a, b, _NT), (a, b)


def _nt_bwd(res, g):
    a, b = res
    return mm_nn(g, b).astype(a.dtype), mm_tn(g, a).astype(b.dtype)


def _tn_fwd(a, b):
    return _dg(a, b, _TN), (a, b)


def _tn_bwd(res, g):
    a, b = res
    return mm_nt(b, g).astype(a.dtype), mm_nn(a, g).astype(b.dtype)


mm_nn.defvjp(_nn_fwd, _nn_bwd)
mm_nt.defvjp(_nt_fwd, _nt_bwd)
mm_tn.defvjp(_tn_fwd, _tn_bwd)


def _rms(x, g):
    return x * lax.rsqrt(jnp.mean(x * x, axis=-1, keepdims=True) + EPS) * g


def _params(sem=None, vmem=VMEM_LIMIT):
    return pltpu.CompilerParams(dimension_semantics=sem, vmem_limit_bytes=vmem)


class Exchange:
    def __init__(self, inputs, outputs, n_sems, start, end, mid=None):
        self.inputs, self.outputs, self.n_sems = list(inputs), list(outputs), n_sems
        self.start, self.mid, self.end = start, mid, end

    def specs(self):
        hbm = pl.BlockSpec(memory_space=pl.ANY)
        sems = [pltpu.SemaphoreType.DMA((self.n_sems,)), pltpu.SemaphoreType.DMA((self.n_sems,))]
        return [hbm] * len(self.inputs), [hbm] * len(self.outputs), sems

    def pick(self, refs, n_before_in, n_before_out):
        n_in, n_out = len(self.inputs), len(self.outputs)
        o0 = n_before_in + n_in + n_before_out
        return refs[n_before_in:n_before_in + n_in], refs[o0:o0 + n_out], refs[-2], refs[-1]

    def before_work(self, mine, first, mid=None):
        @pl.when(first)
        def _():
            self.start(*mine)

        if self.mid is not None and mid is not None:
            @pl.when(mid)
            def _():
                self.mid(*mine)

    def after_work(self, mine, last, mid_done):
        @pl.when(last)
        def _():
            if self.mid is not None and not mid_done:
                self.mid(*mine)
            self.end(*mine)


class _Shifted:
    def __init__(self, ref, base):
        self.ref, self.base = ref, base

    @property
    def at(self):
        return self

    def __getitem__(self, idx):
        return self.ref.at[self.base + idx]


def _both(a, b):
    if a is None or b is None:
        return a or b
    n_i, n_o, n_s = len(a.inputs), len(a.outputs), a.n_sems

    def joined(fa, fb):
        def f(i_refs, o_refs, ss, rs):
            if fa is not None:
                fa(i_refs[:n_i], o_refs[:n_o], ss, rs)
            if fb is not None:
                fb(i_refs[n_i:], o_refs[n_o:], _Shifted(ss, n_s), _Shifted(rs, n_s))
        return f

    mid = joined(a.mid, b.mid) if (a.mid is not None or b.mid is not None) else None
    return Exchange(a.inputs + b.inputs, a.outputs + b.outputs, n_s + b.n_sems, joined(a.start, b.start),
                    joined(a.end, b.end), mid)


def _rowcall(name, fn, rows, fulls, row_out, acc_out=(), tm=256, exchange=None):
    s = rows[0].shape[0]
    tm = min(tm, s)
    nt = s // tm
    n_r, n_f, n_ro, n_ao = len(rows), len(fulls), len(row_out), len(acc_out)
    n_xi = len(exchange.inputs) if exchange else 0

    def body(*refs):
        if exchange:
            mine = exchange.pick(refs, n_r + n_f, n_ro + n_ao)
            exchange.before_work(mine, pl.program_id(0) == 0, pl.program_id(0) == (3 * nt) // 4)
        ins = [r[...] for r in refs[: n_r + n_f]]
        outs = fn(*ins)
        o_refs = refs[n_r + n_f + n_xi:]
        for o_ref, val in zip(o_refs[:n_ro], outs[:n_ro]):
            o_ref[...] = val.astype(o_ref.dtype)
        if n_ao:
            first = pl.program_id(0) == 0

            @pl.when(first)
            def _():
                for o_ref, val in zip(o_refs[n_ro:], outs[n_ro:]):
                    o_ref[...] = val.astype(o_ref.dtype)

            @pl.when(jnp.logical_not(first))
            def _():
                for o_ref, val in zip(o_refs[n_ro:], outs[n_ro:]):
                    o_ref[...] += val.astype(o_ref.dtype)
        if exchange:
            exchange.after_work(mine, pl.program_id(0) == nt - 1, mid_done=True)

    in_specs = [pl.BlockSpec((tm, a.shape[1]), lambda i: (i, 0)) for a in rows]
    in_specs += [pl.BlockSpec(a.shape, lambda i: (0, 0), pipeline_mode=pl.Buffered(1)) for a in fulls]
    out_specs = [pl.BlockSpec((tm, c), lambda i: (i, 0)) for c, _ in row_out]
    out_specs += [pl.BlockSpec(shape, lambda i: (0, 0)) for shape, _ in acc_out]
    out_shape = [jax.ShapeDtypeStruct((s, c), dt) for c, dt in row_out]
    out_shape += [jax.ShapeDtypeStruct(shape, dt) for shape, dt in acc_out]
    args, scratch = [*rows, *fulls], []
    if exchange:
        x_in, x_out, scratch = exchange.specs()
        in_specs += x_in
        out_specs += x_out
        out_shape += exchange.outputs
        args += exchange.inputs
    return pl.pallas_call(
        body, name=name, grid=(nt,), in_specs=in_specs, out_specs=out_specs, out_shape=out_shape,
        scratch_shapes=scratch, compiler_params=_params(("arbitrary",)),
    )(*args)


def _mm_tn_call(name, a, b, tm, tn, tk):
    s, m = a.shape
    n = b.shape[1]
    tk = min(tk, s)

    def body(a_ref, b_ref, o_ref):
        d = _dg(a_ref[...], b_ref[...], _TN)
        first = pl.program_id(2) == 0

        @pl.when(first)
        def _():
            o_ref[...] = d

        @pl.when(jnp.logical_not(first))
        def _():
            o_ref[...] += d

    return pl.pallas_call(
        body, name=name, grid=(m // tm, n // tn, s // tk),
        in_specs=[pl.BlockSpec((tk, tm), lambda i, j, k: (k, i)), pl.BlockSpec((tk, tn), lambda i, j, k: (k, j))],
        out_specs=pl.BlockSpec((tm, tn), lambda i, j, k: (i, j)),
        out_shape=jax.ShapeDtypeStruct((m, n), F32),
        compiler_params=_params(("parallel", "parallel", "arbitrary")),
    )(a, b)


def _mm_tn_parts(name, a, b, by_cols, tm=512, tk=2048):
    s, m = a.shape
    n = b.shape[1]
    r, c = (m, n // N_CHIPS) if by_cols else (m // N_CHIPS, n)
    per = r // tm
    tk = min(tk, s)
    nk = s // tk

    def body(a_ref, b_ref, o_ref, acc):
        d = _dg(a_ref[...], b_ref[...], _TN)
        step = pl.program_id(2)

        @pl.when(step == 0)
        def _():
            acc[...] = d

        @pl.when(step > 0)
        def _():
            acc[...] += d

        @pl.when(step == nk - 1)
        def _():
            o_ref[0] = acc[...].astype(o_ref.dtype)

    if by_cols:
        out_map = lambda i, j, k: (j, i, 0)
    else:
        out_map = lambda i, j, k: (i // per, i % per, 0)
    return pl.pallas_call(
        body, name=name, grid=(m // tm, n // c, s // tk),
        in_specs=[pl.BlockSpec((tk, tm), lambda i, j, k: (k, i)), pl.BlockSpec((tk, c), lambda i, j, k: (k, j))],
        out_specs=pl.BlockSpec((1, tm, c), out_map),
        out_shape=jax.ShapeDtypeStruct((N_CHIPS, r, c), BF16), scratch_shapes=[pltpu.VMEM((tm, c), F32)],
        compiler_params=_params(("parallel", "parallel", "arbitrary")),
    )(a, b)


def _proj_tile(h, g, w):
    p = mm_nn(_rms(h, g), w)
    return (p[:, :SSD_DIM], p[:, SSD_DIM:Q_OFF], p[:, Q_OFF:Q_OFF + SB_DIM],
            p[:, Q_OFF + SB_DIM:Q_OFF + 2 * SB_DIM], p[:, Q_OFF + 2 * SB_DIM:DT_OFF], p[:, DT_OFF:])


def _proj_fwd(tag, h, g, w, exchange=None):
    return _rowcall(
        "proj_fwd" + tag, _proj_tile, [h], [g, w],
        [(SSD_DIM, F32), (CONV_DIM, F32), (SB_DIM, BF16), (SB_DIM, BF16), (SB_DIM, BF16), (DT_PAD, F32)], tm=512,
        exchange=exchange)


def _proj_bwd(tag, h, dh_out, dz, dxbc, dq, dk, dv, ddt, g, w):
    def fn(h, dh_out, dz, dxbc, dq, dk, dv, ddt, g, w):
        dp = jnp.concatenate([dz.astype(BF16), dxbc.astype(BF16), dq.astype(BF16), dk.astype(BF16),
                              dv.astype(BF16), ddt.astype(BF16)], axis=1)
        hn, vjp = jax.vjp(_rms, h, g)
        dh, dg = vjp(mm_nt(dp, w))
        return dh_out + dh, hn, dp, dg

    return _rowcall(
        "proj_bwd" + tag, fn, [h, dh_out, dz, dxbc, dq, dk, dv, ddt], [g, w],
        [(D_MODEL, F32), (D_MODEL, BF16), (IN_PAD, BF16)], [((1, D_MODEL), F32)], tm=256)


def _shift_down(x, tail, j):
    if j == 0:
        return x
    r = pltpu.roll(x, j, 0)
    rt = pltpu.roll(tail, j, 0)
    row = lax.broadcasted_iota(jnp.int32, (HALO, x.shape[1]), 0)
    first = jnp.where(row < j, rt, r[:HALO])
    if x.shape[0] == HALO:
        return first
    return jnp.concatenate([first, r[HALO:]], axis=0)


def _shift_up(x, head, j):
    if j == 0:
        return x
    n = x.shape[0]
    r = pltpu.roll(x, n - j, 0)
    rh = pltpu.roll(head, HALO - j, 0)
    row = lax.broadcasted_iota(jnp.int32, (HALO, x.shape[1]), 0)
    return jnp.concatenate([r[:n - HALO], jnp.where(row >= HALO - j, rh, r[n - HALO:])], axis=0)


def _conv_pre(x, tail, w, b):
    acc = b + w[CONV_K - 1:CONV_K] * x
    for j in range(1, CONV_K):
        acc = acc + w[CONV_K - 1 - j:CONV_K - j] * _shift_down(x, tail, j)
    return acc


def _dsilu(p):
    s = jax.nn.sigmoid(p)
    return s * (1.0 + p * (1.0 - s))


def _conv_fwd(tag, xbc, w, b, tc=512):
    s, c = xbc.shape
    tc = min(tc, s)
    per = tc // HALO

    def body(x_ref, prev_ref, w_ref, b_ref, o_ref):
        tail = jnp.where(pl.program_id(0) > 0, prev_ref[...], 0.0)
        o_ref[...] = jax.nn.silu(_conv_pre(x_ref[...], tail, w_ref[...], b_ref[...]))

    return pl.pallas_call(
        body, name="conv_fwd" + tag, grid=(s // tc,),
        in_specs=[pl.BlockSpec((tc, c), lambda i: (i, 0)),
                  pl.BlockSpec((HALO, c), lambda i: (jnp.maximum(i * per - 1, 0), 0)),
                  pl.BlockSpec((CONV_K, c), lambda i: (0, 0)), pl.BlockSpec((1, c), lambda i: (0, 0))],
        out_specs=pl.BlockSpec((tc, c), lambda i: (i, 0)),
        out_shape=jax.ShapeDtypeStruct((s, c), F32),
        compiler_params=_params(("arbitrary",)),
    )(xbc, xbc, w, b)


def _conv_bwd(tag, xbc, dact, w, b, tc=512):
    s, c = xbc.shape
    tc = min(tc, s)
    per = tc // HALO
    nt = s // tc
    last_blk = s // HALO - 1

    def body(x_ref, prev_ref, next_ref, d_ref, dnext_ref, w_ref, b_ref, dx_ref, dw_ref, db_ref):
        i = pl.program_id(0)
        x = x_ref[...]
        wv = w_ref[...]
        tail = jnp.where(i > 0, prev_ref[...], 0.0)
        dpre = d_ref[...] * _dsilu(_conv_pre(x, tail, wv, b_ref[...]))
        pre_n = _conv_pre(next_ref[...], x[tc - HALO:], wv, b_ref[...])
        dpre_n = jnp.where(i < nt - 1, dnext_ref[...] * _dsilu(pre_n), 0.0)
        dx = wv[CONV_K - 1:CONV_K] * dpre
        for j in range(1, CONV_K):
            dx = dx + wv[CONV_K - 1 - j:CONV_K - j] * _shift_up(dpre, dpre_n, j)
        dx_ref[...] = dx
        dws = [jnp.sum(dpre * _shift_down(x, tail, CONV_K - 1 - k), axis=0, keepdims=True) for k in range(CONV_K)]
        dwv = jnp.concatenate(dws, axis=0)
        dbv = jnp.sum(dpre, axis=0, keepdims=True)

        @pl.when(i == 0)
        def _():
            dw_ref[...] = dwv
            db_ref[...] = dbv

        @pl.when(i > 0)
        def _():
            dw_ref[...] += dwv
            db_ref[...] += dbv

    tile = pl.BlockSpec((tc, c), lambda i: (i, 0))
    prev = pl.BlockSpec((HALO, c), lambda i: (jnp.maximum(i * per - 1, 0), 0))
    nxt = pl.BlockSpec((HALO, c), lambda i: (jnp.minimum((i + 1) * per, last_blk), 0))
    return pl.pallas_call(
        body, name="conv_bwd" + tag, grid=(nt,),
        in_specs=[tile, prev, nxt, tile, nxt, pl.BlockSpec((CONV_K, c), lambda i: (0, 0)),
                  pl.BlockSpec((1, c), lambda i: (0, 0))],
        out_specs=[tile, pl.BlockSpec((CONV_K, c), lambda i: (0, 0)), pl.BlockSpec((1, c), lambda i: (0, 0))],
        out_shape=[jax.ShapeDtypeStruct((s, c), F32), jax.ShapeDtypeStruct((CONV_K, c), F32),
                   jax.ShapeDtypeStruct((1, c), F32)],
        compiler_params=_params(("arbitrary",)),
    )(xbc, xbc, xbc, dact, dact, w, b)


def _ssd_chunk(xs, bm, cm, dtr, z, dt_bias, a_log, d_skip, g, s_prev):
    n = CHUNK
    row = lax.broadcasted_iota(jnp.int32, (n, n), 0)
    col = lax.broadcasted_iota(jnp.int32, (n, n), 1)
    causal = row >= col
    dt = jax.nn.softplus(dtr + dt_bias)
    a_c = dt * (-jnp.exp(a_log))
    hi = lax.Precision.HIGHEST
    a_cum = jnp.dot(causal.astype(F32), a_c, precision=hi, preferred_element_type=F32)
    a_cum_t = lax.dot_general(a_c, (row <= col).astype(F32), (_TN, ((), ())), precision=hi,
                              preferred_element_type=F32)
    p, st = SSD_HEAD_DIM, SSD_STATE
    heads = range(SSD_HEADS)
    grp = [h // (SSD_HEADS // SSD_GROUPS) for h in heads]
    bgs = [bm[:, k * st:(k + 1) * st] for k in range(SSD_GROUPS)]
    cgs = [cm[:, k * st:(k + 1) * st] for k in range(SSD_GROUPS)]
    cb = [mm_nt(cgs[k], bgs[k]) for k in range(SSD_GROUPS)]
    acols = [a_cum[:, h:h + 1] for h in heads]
    a_lasts = [a_cum[n - 1:n, h:h + 1] for h in heads]
    xhs = [xs[:, h * p:(h + 1) * p] for h in heads]
    sps = [s_prev[h * p:(h + 1) * p, :] for h in heads]
    xdts = [xhs[h] * dt[:, h:h + 1] for h in heads]
    decays = [jnp.exp(jnp.where(causal, acols[h] - a_cum_t[h:h + 1, :], -jnp.inf)) for h in heads]
    y_offs = [mm_nt(cgs[grp[h]], sps[h]) for h in heads]
    y_diags = [mm_nn(cb[grp[h]] * decays[h], xdts[h]) for h in heads]
    states = [mm_tn(xdts[h] * jnp.exp(a_lasts[h] - acols[h]), bgs[grp[h]]) for h in heads]
    s_new = [sps[h] * jnp.exp(a_lasts[h]) + states[h] for h in heads]
    ys = [y_diags[h] + y_offs[h] * jnp.exp(acols[h]) + d_skip[:, h:h + 1] * xhs[h] for h in heads]
    y = jnp.concatenate(ys, axis=1) * jax.nn.silu(z)
    return _rms(y, g), jnp.concatenate(s_new, axis=0)


def _split_xbc(t):
    return t[:, :SSD_DIM], t[:, SSD_DIM:SSD_DIM + GN], t[:, SSD_DIM + GN:]


def _ssd_fwd(tag, xact, dtr, z, dt_bias, a_log, d_skip, g, exchange=None):
    s = xact.shape[0]
    nc = s // CHUNK
    srows = SSD_HEADS * SSD_HEAD_DIM
    n_xi = len(exchange.inputs) if exchange else 0

    def body(*refs):
        x_ref, dt_ref, z_ref, b_ref, al_ref, ds_ref, g_ref = refs[:7]
        y_ref, st_ref = refs[7 + n_xi:9 + n_xi]
        state = refs[-3] if exchange else refs[-1]
        if exchange:
            mine = exchange.pick(refs, 7, 2)
            exchange.before_work(mine, pl.program_id(0) == 0, pl.program_id(0) == (3 * nc) // 4)

        @pl.when(pl.program_id(0) == 0)
        def _():
            state[...] = jnp.zeros_like(state)

        sp = state[...]
        st_ref[0] = sp
        xs, bm, cm = _split_xbc(x_ref[...])
        y, sn = _ssd_chunk(xs, bm, cm, dt_ref[...][:, :SSD_HEADS], z_ref[...], b_ref[...], al_ref[...],
                           ds_ref[...], g_ref[...], sp)
        y_ref[...] = y
        state[...] = sn
        if exchange:
            exchange.after_work(mine, pl.program_id(0) == nc - 1, mid_done=True)

    small = pl.BlockSpec((1, SSD_HEADS), lambda i: (0, 0))
    in_specs = [pl.BlockSpec((CHUNK, CONV_DIM), lambda i: (i, 0)), pl.BlockSpec((CHUNK, DT_PAD), lambda i: (i, 0)),
                pl.BlockSpec((CHUNK, SSD_DIM), lambda i: (i, 0)), small, small, small,
                pl.BlockSpec((1, SSD_DIM), lambda i: (0, 0))]
    out_specs = [pl.BlockSpec((CHUNK, SSD_DIM), lambda i: (i, 0)),
                 pl.BlockSpec((1, srows, SSD_STATE), lambda i: (i, 0, 0))]
    out_shape = [jax.ShapeDtypeStruct((s, SSD_DIM), F32), jax.ShapeDtypeStruct((nc, srows, SSD_STATE), F32)]
    args, scratch = [xact, dtr, z, dt_bias, a_log, d_skip, g], [pltpu.VMEM((srows, SSD_STATE), F32)]
    if exchange:
        x_in, x_out, sems = exchange.specs()
        in_specs, out_specs, scratch = in_specs + x_in, out_specs + x_out, scratch + sems
        out_shape, args = out_shape + exchange.outputs, args + exchange.inputs
    return pl.pallas_call(
        body, name="ssd_fwd" + tag, grid=(nc,), in_specs=in_specs, out_specs=out_specs, out_shape=out_shape,
        scratch_shapes=scratch, compiler_params=_params(("arbitrary",)),
    )(*args)


def _ssd_bwd(tag, xact, dtr, z, states, dy, dt_bias, a_log, d_skip, g, exchange=None):
    s = xact.shape[0]
    nc = s // CHUNK
    srows = SSD_HEADS * SSD_HEAD_DIM
    n_xi = len(exchange.inputs) if exchange else 0

    def body(*refs):
        x_ref, dt_ref, z_ref, sp_ref, dy_ref, b_ref, al_ref, ds_ref, g_ref = refs[:9]
        dx_ref, ddt_ref, dz_ref, db_ref, dal_ref, dds_ref, dg_ref = refs[9 + n_xi:16 + n_xi]
        dstate = refs[-3] if exchange else refs[-1]
        first = pl.program_id(0) == 0
        if exchange:
            mine = exchange.pick(refs, 9, 7)
            exchange.before_work(mine, first, pl.program_id(0) == (3 * nc) // 4)

        @pl.when(first)
        def _():
            dstate[...] = jnp.zeros_like(dstate)

        xs, bm, cm = _split_xbc(x_ref[...])
        _, vjp = jax.vjp(_ssd_chunk, xs, bm, cm, dt_ref[...][:, :SSD_HEADS], z_ref[...], b_ref[...], al_ref[...],
                         ds_ref[...], g_ref[...], sp_ref[0])
        dxs, dbm, dcm, ddt, dz, db, dal, dds, dg, dsp = vjp((dy_ref[...], dstate[...]))
        dx_ref[...] = jnp.concatenate([dxs, dbm, dcm], axis=1)
        ddt_ref[...] = jnp.concatenate([ddt, jnp.zeros((CHUNK, DT_PAD - SSD_HEADS), F32)], axis=1)
        dz_ref[...] = dz
        dstate[...] = dsp

        @pl.when(first)
        def _():
            db_ref[...] = db
            dal_ref[...] = dal
            dds_ref[...] = dds
            dg_ref[...] = dg

        @pl.when(jnp.logical_not(first))
        def _():
            db_ref[...] += db
            dal_ref[...] += dal
            dds_ref[...] += dds
            dg_ref[...] += dg

        if exchange:
            exchange.after_work(mine, pl.program_id(0) == nc - 1, mid_done=True)

    def rev(c):
        return lambda i: (nc - 1 - i, 0)

    small = pl.BlockSpec((1, SSD_HEADS), lambda i: (0, 0))
    gspec = pl.BlockSpec((1, SSD_DIM), lambda i: (0, 0))
    in_specs = [pl.BlockSpec((CHUNK, CONV_DIM), rev(0)), pl.BlockSpec((CHUNK, DT_PAD), rev(0)),
                pl.BlockSpec((CHUNK, SSD_DIM), rev(0)),
                pl.BlockSpec((1, srows, SSD_STATE), lambda i: (nc - 1 - i, 0, 0)),
                pl.BlockSpec((CHUNK, SSD_DIM), rev(0)), small, small, small, gspec]
    out_specs = [pl.BlockSpec((CHUNK, CONV_DIM), rev(0)), pl.BlockSpec((CHUNK, DT_PAD), rev(0)),
                 pl.BlockSpec((CHUNK, SSD_DIM), rev(0)), small, small, small, gspec]
    out_shape = [jax.ShapeDtypeStruct((s, CONV_DIM), F32), jax.ShapeDtypeStruct((s, DT_PAD), F32),
                 jax.ShapeDtypeStruct((s, SSD_DIM), F32), jax.ShapeDtypeStruct((1, SSD_HEADS), F32),
                 jax.ShapeDtypeStruct((1, SSD_HEADS), F32), jax.ShapeDtypeStruct((1, SSD_HEADS), F32),
                 jax.ShapeDtypeStruct((1, SSD_DIM), F32)]
    args, scratch = [xact, dtr, z, states, dy, dt_bias, a_log, d_skip, g], [pltpu.VMEM((srows, SSD_STATE), F32)]
    if exchange:
        x_in, x_out, sems = exchange.specs()
        in_specs, out_specs, scratch = in_specs + x_in, out_specs + x_out, scratch + sems
        out_shape, args = out_shape + exchange.outputs, args + exchange.inputs
    return pl.pallas_call(
        body, name="ssd_bwd" + tag, grid=(nc,), in_specs=in_specs, out_specs=out_specs, out_shape=out_shape,
        scratch_shapes=scratch, compiler_params=_params(("arbitrary",)),
    )(*args)


TQ = 128
TK = 128
SB_SCALE = 1.0 / math.sqrt(SB_HEAD_DIM)


def _split2(x):
    hi = x.astype(BF16)
    return hi, (x - hi.astype(F32)).astype(BF16)


TK_NEAR = 384
SB_UNDERFLOW = -110.0


def _sb_logits(qhs, kb, t0, s0, masked):
    zs = [_dg(qh, kb, _NT) for qh in qhs]
    mask = None
    if masked:
        t_pos = t0 + lax.broadcasted_iota(jnp.int32, zs[0].shape, 0)
        s_pos = s0 + lax.broadcasted_iota(jnp.int32, zs[0].shape, 1)
        mask = s_pos < t_pos
    lbs = [jnp.minimum(z, 0.0) - jnp.log(1.0 + jnp.exp(-jnp.abs(z))) for z in zs]
    lss = [lb - z for lb, z in zip(lbs, zs)]
    if masked:
        lss = [jnp.where(mask, ls, 0.0) for ls in lss]
    return lbs, lss, mask


def _running_sums(xs, starts, u, reverse, two_terms=True):
    nsub = xs[0].shape[1] // TK
    order = list(reversed(range(nsub))) if reverse else list(range(nsub))
    chunks = [[x[:, c * TK:(c + 1) * TK] for c in range(nsub)] for x in xs]
    sums = [[(_lane_sums(xc, u) if two_terms else _dg(xc, u, _NN)) for xc in row] for row in chunks]
    out = []
    for row, srow, run in zip(chunks, sums, starts):
        parts = [None] * nsub
        for c in order:
            parts[c] = run + srow[c]
            run = run + jnp.sum(row[c], axis=1, keepdims=True)
        out.append((parts[0] if nsub == 1 else jnp.concatenate(parts, axis=1), run))
    return out


def _lane_sums(x, u):
    hi, lo = _split2(x)
    return _dg(hi, u, _NN) + _dg(lo, u, _NN)


def _tri(cmp):
    j = lax.broadcasted_iota(jnp.int32, (TK, TK), 0)
    s = lax.broadcasted_iota(jnp.int32, (TK, TK), 1)
    return cmp(j, s).astype(BF16)


def _sb_fwd(tag, q, k, v, exchange=None):
    s = q.shape[0]
    npair = SB_DIM // LANES
    nq = s // TQ
    near = min(TK_NEAR, s)
    n_xi = len(exchange.inputs) if exchange else 0

    def body(*refs):
        q_ref, k_ref, v_ref = refs[:3]
        o_ref, t_ref = refs[3 + n_xi:5 + n_xi]
        tq = pl.program_id(1)
        if exchange:
            pair = pl.program_id(0)
            mine = exchange.pick(refs, 3, 2)
            exchange.before_work(mine, jnp.logical_and(pair == 0, tq == 0),
                                 jnp.logical_and(pair == npair - 1, tq == 0))
        qp = q_ref[...]
        lane = lax.broadcasted_iota(jnp.int32, (1, LANES), 1)
        u_gt = _tri(lambda j, s: j > s)
        heads = range(LANES // SB_HEAD_DIM)
        hms = [(lane // SB_HEAD_DIM) == hh for hh in heads]
        qhs = [jnp.where(hm, qp, jnp.zeros_like(qp)) * SB_SCALE for hm in hms]

        def block(wb, width, carry, masked):
            off = pl.multiple_of(wb * TK, TK)
            kb = k_ref[pl.ds(off, width), :]
            vb = v_ref[pl.ds(off, width), :]
            lbs, lss, mask = _sb_logits(qhs, kb, tq * TQ, wb * TK, masked)
            sums = _running_sums(lss, [c[0] for c in carry], u_gt, reverse=True)
            ws = [jnp.exp(lb + later) for lb, (later, _) in zip(lbs, sums)]
            if masked:
                ws = [jnp.where(mask, w, 0.0) for w in ws]
            pvs = [_dg(w, vb, _NN) for w in ws]
            return tuple((r, c[1] + pv) for (_, r), c, pv in zip(sums, carry, pvs))

        def more(c):
            alive = jnp.max(c[1][0][0])
            for hh in heads[1:]:
                alive = jnp.maximum(alive, jnp.max(c[1][hh][0]))
            return jnp.logical_and(c[0] >= 0, alive > SB_UNDERFLOW)

        start = tuple((jnp.zeros((TQ, 1), F32), jnp.zeros((TQ, LANES), F32)) for _ in heads)
        near_blk = jnp.maximum(tq - (near // TK - 1), 0)
        wb, done = lax.while_loop(more, lambda c: (c[0] - 1, block(c[0], TK, c[1], False)),
                                  (near_blk - 1, block(near_blk, near, start, True)))
        first = (wb + 1).astype(F32)
        out = jnp.zeros((TQ, LANES), F32)
        tot = jnp.zeros((TQ, LANES), F32)
        for hh in heads:
            r, acc = done[hh]
            out = out + jnp.where(hms[hh], acc, 0.0)
            tot = tot + jnp.where(hms[hh], jnp.where(lane % SB_HEAD_DIM == 1, first, r), 0.0)
        o_ref[...] = out
        t_ref[...] = tot
        if exchange:
            exchange.after_work(mine, jnp.logical_and(pair == npair - 1, tq == nq - 1), mid_done=True)

    tile = pl.BlockSpec((TQ, LANES), lambda p, t: (t, p))
    full = pl.BlockSpec((s, LANES), lambda p, t: (0, p))
    in_specs, out_specs = [tile, full, full], [tile, tile]
    out_shape = [jax.ShapeDtypeStruct((s, SB_DIM), F32)] * 2
    args, scratch = [q, k, v], []
    if exchange:
        x_in, x_out, scratch = exchange.specs()
        in_specs, out_specs = in_specs + x_in, out_specs + x_out
        out_shape, args = out_shape + exchange.outputs, args + exchange.inputs
    return pl.pallas_call(
        body, name="sb_fwd" + tag, grid=(npair, nq), in_specs=in_specs, out_specs=out_specs, out_shape=out_shape,
        scratch_shapes=scratch, compiler_params=_params(("arbitrary", "arbitrary")),
    )(*args)


def _sb_bwd(tag, q, k, v, tot, do, exchange=None):
    s = q.shape[0]
    npair = SB_DIM // LANES
    nq = s // TQ
    near = min(TK_NEAR, s)
    n_xi = len(exchange.inputs) if exchange else 0

    def body(*refs):
        q_ref, k_ref, v_ref, t_ref, do_ref = refs[:5]
        dq_ref, dk_ref, dv_ref = refs[5 + n_xi:8 + n_xi]
        tq = pl.program_id(1)
        if exchange:
            pair = pl.program_id(0)
            mine = exchange.pick(refs, 5, 3)
            exchange.before_work(mine, jnp.logical_and(pair == 0, tq == 0),
                                 jnp.logical_and(pair == npair - 1, tq == 0))

        @pl.when(tq == 0)
        def _():
            dk_ref[...] = jnp.zeros_like(dk_ref)
            dv_ref[...] = jnp.zeros_like(dv_ref)

        qp = q_ref[...]
        dop = do_ref[...]
        totp = t_ref[...]
        lane = lax.broadcasted_iota(jnp.int32, (1, LANES), 1)
        u_le = _tri(lambda j, s: j <= s)
        u_lt = _tri(lambda j, s: j < s)
        heads = range(LANES // SB_HEAD_DIM)
        hms = [(lane // SB_HEAD_DIM) == hh for hh in heads]
        qhs = [jnp.where(hm, qp, jnp.zeros_like(qp)) * SB_SCALE for hm in hms]
        dohs = [jnp.where(hm, dop, 0.0).astype(BF16) for hm in hms]
        totals = [jnp.sum(jnp.where(lane == hh * SB_HEAD_DIM, totp, 0.0), axis=1, keepdims=True) for hh in heads]
        first = jnp.max(jnp.where(lane == 1, totp, 0.0)).astype(jnp.int32)

        def block(wb, width, carry, masked):
            off = pl.multiple_of(wb * TK, TK)
            kb = k_ref[pl.ds(off, width), :]
            vb = v_ref[pl.ds(off, width), :]
            lbs, lss, mask = _sb_logits(qhs, kb, tq * TQ, wb * TK, masked)
            dws = [_dg(doh, vb, _NT) for doh in dohs]
            pres = _running_sums(lss, [c[0] for c in carry], u_le, reverse=False)
            ws = [jnp.exp(lb + (total - before)) for lb, total, (before, _) in zip(lbs, totals, pres)]
            if masked:
                ws = [jnp.where(mask, w, 0.0) for w in ws]
            gs = [w * dw for w, dw in zip(ws, dws)]
            lefts = _running_sums(gs, [c[1] for c in carry], u_lt, reverse=False, two_terms=False)
            dzs = [g - jnp.exp(lb) * (g + g_left) for g, lb, (g_left, _) in zip(gs, lbs, lefts)]
            if masked:
                dzs = [jnp.where(mask, dz, 0.0) for dz in dzs]
            dzbs = [dz.astype(BF16) for dz in dzs]
            dks = [_dg(dzb, qh, _TN) for dzb, qh in zip(dzbs, qhs)]
            dvs = [_dg(w, doh, _TN) for w, doh in zip(ws, dohs)]
            dqs = [_dg(dzb, kb, _NN) for dzb in dzbs]
            dk_ref[pl.ds(off, width), :] += functools.reduce(jnp.add, dks)
            dv_ref[pl.ds(off, width), :] += functools.reduce(jnp.add, dvs)
            return tuple((pre, gc, c[2] + dq) for (_, pre), (_, gc), c, dq in zip(pres, lefts, carry, dqs))

        zero = jnp.zeros((TQ, 1), F32)
        start = tuple((zero, zero, jnp.zeros((TQ, LANES), F32)) for _ in heads)
        near_blk = jnp.maximum(tq - (near // TK - 1), 0)
        far = lax.fori_loop(first, near_blk, lambda j, c: block(j, TK, c, False), start)
        done = block(near_blk, near, far, True)
        dq = jnp.zeros((TQ, LANES), F32)
        for hh in heads:
            dq = dq + jnp.where(hms[hh], done[hh][2], 0.0)
        dq_ref[...] = dq * SB_SCALE
        if exchange:
            exchange.after_work(mine, jnp.logical_and(pair == npair - 1, tq == nq - 1), mid_done=True)

    tile = pl.BlockSpec((TQ, LANES), lambda p, t: (t, p))
    full = pl.BlockSpec((s, LANES), lambda p, t: (0, p))
    in_specs, out_specs = [tile, full, full, tile, tile], [tile, full, full]
    out_shape = [jax.ShapeDtypeStruct((s, SB_DIM), F32)] * 3
    args, scratch = [q, k, v, tot, do], []
    if exchange:
        x_in, x_out, scratch = exchange.specs()
        in_specs, out_specs = in_specs + x_in, out_specs + x_out
        out_shape, args = out_shape + exchange.outputs, args + exchange.inputs
    return pl.pallas_call(
        body, name="sb_bwd" + tag, grid=(npair, nq), in_specs=in_specs, out_specs=out_specs, out_shape=out_shape,
        scratch_shapes=scratch, compiler_params=_params(("arbitrary", "arbitrary")),
    )(*args)


def _out_tile(y_ssd, o, sb_g, w_out):
    y_all = jnp.concatenate([y_ssd, _rms(o, sb_g)], axis=1)
    return mm_nn(y_all, w_out)


def _out_fwd(tag, h, y_ssd, o, sb_g, w_out):
    return _rowcall("out_fwd" + tag, lambda h, y, o, g, w: (h + _out_tile(y, o, g, w),),
                    [h, y_ssd, o], [sb_g, w_out], [(D_MODEL, F32)], tm=512)[0]


def _out_bwd(tag, y_ssd, o, dh, sb_g, w_out):
    def fn(y, o, dh, g, w):
        _, vjp = jax.vjp(_out_tile, y, o, g, w.astype(F32))
        return vjp(dh)

    return _rowcall("out_bwd" + tag, fn, [y_ssd, o, dh], [sb_g, w_out], [(SSD_DIM, F32), (SB_DIM, F32)],
                    [((1, SB_DIM), F32), ((D_MODEL, D_MODEL), F32)], tm=256)


def _mem_tile(mem, g, w_k, w_v):
    m = _rms(mem, g)
    return mm_nn(m, w_k), mm_nn(m, w_v)


def _mem_fwd(tag, mem, g, w_k, w_v):
    return _rowcall("mem_fwd" + tag, _mem_tile, [mem], [g, w_k, w_v], [(XA_DIM, F32), (XA_DIM, F32)], tm=256)


def _mem_bwd(tag, mem, dkx, dvx, g, w_k, w_v):
    def fn(mem, dkx, dvx, g, w_k, w_v):
        _, vjp = jax.vjp(lambda g, a, b: _mem_tile(mem, g, a, b), g, w_k.astype(F32), w_v.astype(F32))
        return vjp((dkx, dvx))

    return _rowcall("mem_bwd" + tag, fn, [mem, dkx, dvx], [g, w_k, w_v], [],
                    [((1, D_MODEL), F32), ((D_MODEL, XA_DIM), F32), ((D_MODEL, XA_DIM), F32)], tm=256)


def _xattn_tile(h, g, w_q, kx, vx, w_o):
    q = mm_nn(_rms(h, g), w_q)
    scale = 1.0 / math.sqrt(XA_HEAD_DIM)
    outs = []
    for i in range(XA_HEADS):
        sl = slice(i * XA_HEAD_DIM, (i + 1) * XA_HEAD_DIM)
        p = jax.nn.softmax(mm_nt(q[:, sl], kx[:, sl]) * scale, axis=-1)
        outs.append(mm_nn(p, vx[:, sl]))
    return mm_nn(jnp.concatenate(outs, axis=1), w_o)


def _xattn_fwd(tag, h, g, w_q, kx, vx, w_o):
    return _rowcall("xattn_fwd" + tag, lambda h, g, wq, kx, vx, wo: (h + _xattn_tile(h, g, wq, kx, vx, wo),),
                    [h], [g, w_q, kx, vx, w_o], [(D_MODEL, F32)], tm=512)[0]


def _xattn_bwd(tag, h, dh_out, g, w_q, kx, vx, w_o, exchange=None):
    def fn(h, dh_out, g, w_q, kx, vx, w_o):
        _, vjp = jax.vjp(_xattn_tile, h, g, w_q.astype(F32), kx, vx, w_o.astype(F32))
        dh, dg, dwq, dkx, dvx, dwo = vjp(dh_out)
        return dh_out + dh, dg, dwq, dkx, dvx, dwo

    mlen = kx.shape[0]
    return _rowcall("xattn_bwd" + tag, fn, [h, dh_out], [g, w_q, kx, vx, w_o], [(D_MODEL, F32)],
                    [((1, D_MODEL), F32), ((D_MODEL, XA_DIM), F32), ((mlen, XA_DIM), F32), ((mlen, XA_DIM), F32),
                     ((XA_DIM, D_MODEL), F32)], tm=256, exchange=exchange)


def _mlp_fwd(tag, h, g, w1, w2, exchange=None):
    def fn(h, g, w1, w2):
        r = jnp.maximum(mm_nn(_rms(h, g), w1), 0.0)
        return h + mm_nn(r * r, w2), r

    return _rowcall("mlp_fwd" + tag, fn, [h], [g, w1, w2], [(D_MODEL, F32), (D_FF, BF16)], tm=256,
                    exchange=exchange)


def _mlp_bwd(tag, h, relu, dh_out, g, w1, w2, exchange=None):
    def fn(h, relu, dh_out, g, w1, w2):
        hn, vjp = jax.vjp(_rms, h, g)
        r = relu.astype(F32)
        dob = dh_out.astype(BF16)
        dp = mm_nt(dob, w2) * (2.0 * r)
        dh, dg = vjp(mm_nt(dp, w1))
        return dh_out + dh, hn, dp, r * r, dob, dg

    return _rowcall("mlp_bwd" + tag, fn, [h, relu, dh_out], [g, w1, w2],
                    [(D_MODEL, F32), (D_MODEL, BF16), (D_FF, BF16), (D_FF, BF16), (D_MODEL, BF16)],
                    [((1, D_MODEL), F32)], tm=256, exchange=exchange)


def _head(h, g, target):
    def lossfn(h, g, t):
        err = jnp.square(_rms(h, g) - t)
        return 0.5 * jnp.sum(jnp.mean(err, axis=-1))

    def fn(h, t, g):
        loss, vjp = jax.vjp(lambda h, g: lossfn(h, g, t), h, g)
        dh, dg = vjp(jnp.ones((), F32))
        return dh, jnp.full((1, LANES), loss, F32), dg

    return _rowcall("head", fn, [h, target], [g], [(D_MODEL, F32)], [((1, LANES), F32), ((1, D_MODEL), F32)], tm=512)


def _row(v):
    return v.reshape(1, -1)


class LocalPlan:
    def __init__(self, mats):
        self.mats = mats

    def weights(self, l):
        return self.mats[l]

    def carried_by(self, kernel, l):
        return None

    def carried_out(self, kernel, l, outs):
        pass

    def mlp_grads_done(self, l, gm):
        pass

    def mixer_grads_done(self, l, gm):
        pass

    def grads_done(self, l, gm):
        pass


def _local_step(x, mem, target, sw, plan):
    h = x
    saved = []
    for l in range(DEPTH):
        tag = str(l)
        m = plan.weights(l)
        z, xbc, q, k, v, dtr, *carried = _proj_fwd(tag, h, _row(sw["norm_mix_g"][l]), m["w_in"],
                                                   exchange=plan.carried_by("proj_fwd", l))
        plan.carried_out("proj_fwd", l, carried)
        xact = _conv_fwd(tag, xbc, m["conv_w"], _row(sw["conv_b"][l]))
        y_ssd, states, *carried = _ssd_fwd(tag, xact, dtr, z, _row(sw["dt_bias"][l]), _row(sw["a_log"][l]),
                                           _row(sw["d_skip"][l]), _row(sw["ssd_norm_g"][l]),
                                           exchange=plan.carried_by("ssd_fwd", l))
        plan.carried_out("ssd_fwd", l, carried)
        o, sb_tot, *carried = _sb_fwd(tag, q, k, v, exchange=plan.carried_by("sb_fwd", l))
        plan.carried_out("sb_fwd", l, carried)
        h1 = _out_fwd(tag, h, y_ssd, o, _row(sw["sb_norm_g"][l]), m["w_out"])
        kx, vx = _mem_fwd(tag, mem, _row(sw["norm_mem_g"][l]), m["w_xk"], m["w_xv"])
        h2 = _xattn_fwd(tag, h1, _row(sw["norm_xa_g"][l]), m["w_xq"], kx, vx, m["w_xo"])
        h3, relu, *carried = _mlp_fwd(tag, h2, _row(sw["norm_ff_g"][l]), m["w_ff1"], m["w_ff2"],
                                      exchange=plan.carried_by("mlp_fwd", l))
        plan.carried_out("mlp_fwd", l, carried)
        saved.append((h, z, xbc, q, k, v, dtr, xact, y_ssd, states, o, sb_tot, h1, kx, vx, h2, relu))
        h = h3

    dh, loss, d_final = _head(h, _row(sw["final_g"]), target)
    gm = [dict() for _ in range(DEPTH)]
    gs = {name: [None] * DEPTH for name in ("norm_mix_g", "conv_w", "conv_b", "dt_bias", "a_log", "d_skip",
                                            "ssd_norm_g", "sb_norm_g", "norm_xa_g", "norm_mem_g", "norm_ff_g")}
    for l in reversed(range(DEPTH)):
        tag = str(l)
        m = plan.weights(l)
        h0, z, xbc, q, k, v, dtr, xact, y_ssd, states, o, sb_tot, h1, kx, vx, h2, relu = saved[l]
        dh2, hn_b, dp_b, a_b, do_b, gs["norm_ff_g"][l], *carried = _mlp_bwd(
            tag, h2, relu, dh, _row(sw["norm_ff_g"][l]), m["w_ff1"], m["w_ff2"],
            exchange=plan.carried_by("mlp_bwd", l))
        plan.carried_out("mlp_bwd", l, carried)
        gm[l]["w_ff1"] = _mm_tn_parts("dw_ff1" + tag, hn_b, dp_b, True)
        gm[l]["w_ff2"] = _mm_tn_parts("dw_ff2" + tag, a_b, do_b, False)
        plan.mlp_grads_done(l, gm[l])
        dh1, gs["norm_xa_g"][l], gm[l]["w_xq"], dkx, dvx, gm[l]["w_xo"], *carried = _xattn_bwd(
            tag, h1, dh2, _row(sw["norm_xa_g"][l]), m["w_xq"], kx, vx, m["w_xo"],
            exchange=plan.carried_by("xattn_bwd", l))
        plan.carried_out("xattn_bwd", l, carried)
        gs["norm_mem_g"][l], gm[l]["w_xk"], gm[l]["w_xv"] = _mem_bwd(
            tag, mem, dkx, dvx, _row(sw["norm_mem_g"][l]), m["w_xk"], m["w_xv"])
        dy_ssd, do, gs["sb_norm_g"][l], gm[l]["w_out"] = _out_bwd(
            tag, y_ssd, o, dh1, _row(sw["sb_norm_g"][l]), m["w_out"])
        plan.mixer_grads_done(l, gm[l])
        dq, dk, dv, *carried = _sb_bwd(tag, q, k, v, sb_tot, do, exchange=plan.carried_by("sb_bwd", l))
        plan.carried_out("sb_bwd", l, carried)
        dxact, ddtr, dz, gs["dt_bias"][l], gs["a_log"][l], gs["d_skip"][l], gs["ssd_norm_g"][l], *carried = _ssd_bwd(
            tag, xact, dtr, z, states, dy_ssd, _row(sw["dt_bias"][l]), _row(sw["a_log"][l]),
            _row(sw["d_skip"][l]), _row(sw["ssd_norm_g"][l]), exchange=plan.carried_by("ssd_bwd", l))
        plan.carried_out("ssd_bwd", l, carried)
        dxbc, gs["conv_w"][l], gs["conv_b"][l] = _conv_bwd(tag, xbc, dxact, m["conv_w"], _row(sw["conv_b"][l]))
        dh, hn_b, dp_b, gs["norm_mix_g"][l] = _proj_bwd(
            tag, h0, dh1, dz, dxbc, dq, dk, dv, ddtr, _row(sw["norm_mix_g"][l]), m["w_in"])
        gm[l]["w_in"] = _mm_tn_call("dw_in" + tag, dp_b, hn_b, IN_PAD, 512, 1024)
        plan.grads_done(l, gm[l])
    gs["final_g"] = d_final
    return loss, dh, gm, gs


ANY = pl.BlockSpec(memory_space=pl.ANY)
VMEM_SPEC = pl.BlockSpec(memory_space=pltpu.VMEM)


def _place():
    return lax.axis_index("x"), lax.axis_index("y"), lax.axis_index("c")


def _other_chips(x, y):
    return [(1 - x, y), (x, 1 - y), (1 - x, 1 - y)]


def _remote(send_sems, recv_sems, idx, src, dst, to):
    return pltpu.make_async_remote_copy(src_ref=src, dst_ref=dst, send_sem=send_sems.at[idx],
                                        recv_sem=recv_sems.at[idx], device_id=to, device_id_type=MESH)


def _run_exchange(name, ex):
    def body(*refs):
        mine = ex.pick(refs, 0, 0)
        ex.start(*mine)
        if ex.mid is not None:
            ex.mid(*mine)
        ex.end(*mine)

    x_in, x_out, scratch = ex.specs()
    return pl.pallas_call(body, name=name, in_specs=x_in, out_specs=x_out, out_shape=ex.outputs,
                          scratch_shapes=scratch)(*ex.inputs)


def _gather_exchange(layer, shards):
    n = len(shards)
    outs = [jax.ShapeDtypeStruct((N_CHIPS,) + s.shape[1:], s.dtype) for s in shards]

    def start(w_refs, o_refs, ss, rs):
        x, y, c = _place()

        @pl.when(c == layer)
        def _():
            for i in range(n):
                for kk, (cx, cy) in enumerate(_other_chips(x, y)):
                    _remote(ss, rs, 6 * i + kk, w_refs[i].at[layer], o_refs[i].at[2 * x + y], (cx, cy, layer)).start()

    def mid(w_refs, o_refs, ss, rs):
        x, y, c = _place()

        @pl.when(c == layer)
        def _():
            for i in range(n):
                for kk, (cx, cy) in enumerate(_other_chips(x, y)):
                    got = o_refs[i].at[2 * cx + cy]
                    _remote(ss, rs, 6 * i + kk, got, got, (x, y, c)).wait_recv()
                    _remote(ss, rs, 6 * i + 3 + kk, got, got, (x, y, 1 - layer)).start()

    def end(w_refs, o_refs, ss, rs):
        x, y, c = _place()
        for i in range(n):
            for kk, (cx, cy) in enumerate(_other_chips(x, y)):
                got = o_refs[i].at[2 * cx + cy]

                @pl.when(c == layer)
                def _():
                    _remote(ss, rs, 6 * i + kk, w_refs[i].at[layer], got, (x, y, c)).wait_send()
                    _remote(ss, rs, 6 * i + 3 + kk, got, got, (x, y, c)).wait_send()

                @pl.when(c != layer)
                def _():
                    _remote(ss, rs, 6 * i + 3 + kk, got, got, (x, y, c)).wait_recv()

    return Exchange(shards, outs, 6 * n, start, end, mid)


def _handover_exchange(layer, grads):
    n = len(grads)
    outs = [jax.ShapeDtypeStruct(g.shape, g.dtype) for g in grads]

    def start(g_refs, o_refs, ss, rs):
        x, y, c = _place()

        @pl.when(c != layer)
        def _():
            for i in range(n):
                _remote(ss, rs, i, g_refs[i], o_refs[i], (x, y, layer)).start()

    def end(g_refs, o_refs, ss, rs):
        x, y, c = _place()
        for i in range(n):
            @pl.when(c != layer)
            def _():
                _remote(ss, rs, i, g_refs[i], o_refs[i], (x, y, c)).wait_send()

            @pl.when(c == layer)
            def _():
                _remote(ss, rs, i, g_refs[i], o_refs[i], (x, y, c)).wait_recv()

    return Exchange(grads, outs, n, start, end)


def _scatter_exchange(layer, parts):
    n = len(parts)
    outs = [jax.ShapeDtypeStruct(p.shape, p.dtype) for p in parts]

    def start(s_refs, o_refs, ss, rs):
        x, y, c = _place()

        @pl.when(c == layer)
        def _():
            for i in range(n):
                for kk, (cx, cy) in enumerate(_other_chips(x, y)):
                    _remote(ss, rs, 3 * i + kk, s_refs[i].at[2 * cx + cy], o_refs[i].at[2 * x + y],
                            (cx, cy, layer)).start()

    def end(s_refs, o_refs, ss, rs):
        x, y, c = _place()

        @pl.when(c == layer)
        def _():
            for i in range(n):
                for kk, (cx, cy) in enumerate(_other_chips(x, y)):
                    got = o_refs[i].at[2 * cx + cy]
                    _remote(ss, rs, 3 * i + kk, got, got, (x, y, c)).wait_recv()
            for i in range(n):
                for kk, (cx, cy) in enumerate(_other_chips(x, y)):
                    _remote(ss, rs, 3 * i + kk, s_refs[i].at[2 * cx + cy], o_refs[i].at[2 * x + y],
                            (x, y, c)).wait_send()

    return Exchange(parts, outs, 3 * n, start, end)


def _return_exchange(reduced):
    flat = [g for layer in range(DEPTH) for g in reduced[layer]]
    n = len(reduced[0])
    outs = [jax.ShapeDtypeStruct(g.shape, g.dtype) for g in flat]

    def start(g_refs, o_refs, ss, rs):
        x, y, c = _place()
        for layer in range(DEPTH):
            @pl.when(c == layer)
            def _():
                for i in range(n):
                    k = layer * n + i
                    _remote(ss, rs, k, g_refs[k], o_refs[k], (x, y, 1 - layer)).start()

    def end(g_refs, o_refs, ss, rs):
        x, y, c = _place()
        for layer in range(DEPTH):
            for i in range(n):
                k = layer * n + i

                @pl.when(c == layer)
                def _():
                    _remote(ss, rs, k, g_refs[k], o_refs[k], (x, y, c)).wait_send()

                @pl.when(c != layer)
                def _():
                    _remote(ss, rs, k, g_refs[k], o_refs[k], (x, y, c)).wait_recv()

    return Exchange(flat, outs, DEPTH * n, start, end)


def _gather_small(tag, buf):
    shape = buf.shape

    def body(b_ref, o_ref, sum_ref, send_sems, recv_sems, local_sem):
        x, y, c = _place()
        me = 4 * x + 2 * y + c
        mine = pltpu.make_async_copy(b_ref, o_ref.at[me], local_sem)
        mine.start()
        flips = [(dx, dy, dc) for dx in (0, 1) for dy in (0, 1) for dc in (0, 1) if (dx, dy, dc) != (0, 0, 0)]
        sends = []

        def peer(dx, dy, dc):
            return (1 - x if dx else x, 1 - y if dy else y, 1 - c if dc else c)

        for kk, flip in enumerate(flips):
            cp = pltpu.make_async_remote_copy(src_ref=b_ref, dst_ref=o_ref.at[me], send_sem=send_sems.at[kk],
                                              recv_sem=recv_sems.at[kk], device_id=peer(*flip), device_id_type=MESH)
            cp.start()
            sends.append(cp)
        for kk, flip in enumerate(flips):
            px, py, pc = peer(*flip)
            frm = 4 * px + 2 * py + pc
            pltpu.make_async_remote_copy(src_ref=b_ref, dst_ref=o_ref.at[frm], send_sem=send_sems.at[kk],
                                         recv_sem=recv_sems.at[kk], device_id=(x, y, c),
                                         device_id_type=MESH).wait_recv()
        for cp in sends:
            cp.wait_send()
        mine.wait()
        total = o_ref[0]
        for d in range(1, N_DEV):
            total = total + o_ref[d]
        sum_ref[...] = total

    return pl.pallas_call(
        body, name="gather_small" + tag, in_specs=[VMEM_SPEC], out_specs=[VMEM_SPEC, VMEM_SPEC],
        out_shape=[jax.ShapeDtypeStruct((N_DEV,) + shape, buf.dtype), jax.ShapeDtypeStruct(shape, buf.dtype)],
        scratch_shapes=[pltpu.SemaphoreType.DMA((N_DEV - 1,)), pltpu.SemaphoreType.DMA((N_DEV - 1,)),
                        pltpu.SemaphoreType.DMA],
    )(buf)


def _add_handed(tag, layer, g, r, tr=256):
    _, rows, cols = g.shape
    tr, tc = _tile_of(rows, cols, tr)
    per_row = cols // tc

    def body(g_ref, r_ref, o_ref):
        @pl.when(lax.axis_index("c") == layer)
        def _():
            o_ref[...] = (g_ref[...].astype(F32) + r_ref[...].astype(F32)).astype(o_ref.dtype)

    spec = pl.BlockSpec((1, tr, tc), lambda p, i: (p, i // per_row, i % per_row))
    return pl.pallas_call(
        body, name="add_handed_" + tag, grid=(N_CHIPS, (rows // tr) * per_row), in_specs=[spec, spec], out_specs=spec,
        out_shape=jax.ShapeDtypeStruct(g.shape, BF16), compiler_params=_params(("arbitrary", "arbitrary")),
    )(g, r)


def _sum_parts(tag, layer, own, parts, tr=256):
    _, rows, cols = parts.shape
    tr, tc = _tile_of(rows, cols, tr)
    per_row = cols // tc
    chip = (2 * lax.axis_index("x") + lax.axis_index("y")).astype(jnp.int32).reshape(1)

    def body(c_ref, own_ref, p1_ref, p2_ref, p3_ref, o_ref):
        @pl.when(lax.axis_index("c") == layer)
        def _():
            total = own_ref[0].astype(F32)
            for p_ref in (p1_ref, p2_ref, p3_ref):
                total = total + p_ref[0].astype(F32)
            o_ref[...] = total

    def after(kk):
        return pl.BlockSpec((1, tr, tc), lambda i, c_ref: ((c_ref[0] + kk) % N_CHIPS, i // per_row, i % per_row))

    return pl.pallas_call(
        body, name="sum_parts_" + tag,
        grid_spec=pltpu.PrefetchScalarGridSpec(
            num_scalar_prefetch=1, grid=((rows // tr) * per_row,), in_specs=[after(0), after(1), after(2), after(3)],
            out_specs=pl.BlockSpec((tr, tc), lambda i, c_ref: (i // per_row, i % per_row))),
        out_shape=jax.ShapeDtypeStruct((rows, cols), F32),
        compiler_params=_params(("arbitrary",)),
    )(chip, own, parts, parts, parts)


def _adamw_math(w, g, m, v):
    m = ADAM_B1 * m + (1.0 - ADAM_B1) * g
    v = ADAM_B2 * v + (1.0 - ADAM_B2) * jnp.square(g)
    m_hat = m / (1.0 - ADAM_B1 ** ADAM_STEP)
    v_hat = v / (1.0 - ADAM_B2 ** ADAM_STEP)
    delta = -ADAM_LR * (m_hat / (jnp.sqrt(v_hat) + ADAM_EPS) + ADAM_WD * w)
    return delta, m, v


def _tile_of(rows, cols, tr):
    return (tr, cols) if rows % tr == 0 else (rows, 256)


def _adamw(tag, w, computed, received, m, v, tr=256):
    _, rows, cols = w.shape
    tr, tc = _tile_of(rows, cols, tr)
    per_row = cols // tc

    def body(*refs):
        w_ref, m_ref, v_ref = refs[:3]
        g_refs = refs[3:3 + 2 * DEPTH]
        g_ref, d_ref, nm_ref, nv_ref = refs[3 + 2 * DEPTH:]
        layer = pl.program_id(0)
        core = lax.axis_index("c")
        g = jnp.zeros((tr, tc), F32)
        for l in range(DEPTH):
            mine = jnp.where(core == l, g_refs[2 * l][...], g_refs[2 * l + 1][...])
            g = jnp.where(layer == l, mine, g)
        g_ref[0] = g
        d_ref[0], nm_ref[0], nv_ref[0] = _adamw_math(w_ref[0], g, m_ref[0], v_ref[0])

    stacked = pl.BlockSpec((1, tr, tc), lambda l, i: (l, i // per_row, i % per_row))

    def of_layer(k):
        return pl.BlockSpec((tr, tc), lambda l, i: (jnp.where(l == k, i // per_row, 0),
                                                    jnp.where(l == k, i % per_row, 0)))

    g_specs = [of_layer(l) for l in range(DEPTH) for _ in range(2)]
    g_args = [a for l in range(DEPTH) for a in (computed[l], received[l])]
    return pl.pallas_call(
        body, name="adamw_" + tag, grid=(DEPTH, (rows // tr) * per_row), in_specs=[stacked] * 3 + g_specs,
        out_specs=[stacked] * 4, out_shape=[jax.ShapeDtypeStruct(w.shape, F32)] * 4,
        compiler_params=_params(("arbitrary", "arbitrary")),
    )(w, m, v, *g_args)


def _w_in_to_padded(wt):
    d0 = SSD_DIM + CONV_DIM
    rows = jnp.concatenate([wt[:d0], wt[d0 + SSD_HEADS:], wt[d0:d0 + SSD_HEADS],
                            jnp.zeros((DT_PAD - SSD_HEADS, wt.shape[1]), wt.dtype)], axis=0)
    return rows.T


def _w_in_from_padded(gt):
    d0 = SSD_DIM + CONV_DIM
    return jnp.concatenate([gt[:d0], gt[DT_OFF:DT_OFF + SSD_HEADS], gt[d0:DT_OFF]], axis=0)


def _small_layout():
    return (("norm_mix_g", 0, 0, D_MODEL), ("norm_xa_g", 1, 0, D_MODEL), ("norm_mem_g", 2, 0, D_MODEL),
            ("norm_ff_g", 3, 0, D_MODEL), ("conv_b", 4, 0, CONV_DIM), ("ssd_norm_g", 5, 0, SSD_DIM),
            ("sb_norm_g", 5, SSD_DIM, SB_DIM), ("dt_bias", 6, 0, SSD_HEADS), ("a_log", 6, LANES, SSD_HEADS),
            ("d_skip", 6, 2 * LANES, SSD_HEADS))


def _pack_small(gs, loss):
    lay = _small_layout()
    args = [gs[name][l] for l in range(DEPTH) for name, _, _, _ in lay]
    args += [gs["conv_w"][l] for l in range(DEPTH)] + [gs["final_g"], loss]
    n_lay = len(lay)

    def body(*refs):
        o_ref = refs[-1]
        o_ref[...] = jnp.zeros_like(o_ref)
        for l in range(DEPTH):
            for i, (_, rr, c0, width) in enumerate(lay):
                row = l * _SM_PER_LAYER + rr
                o_ref[row:row + 1, c0:c0 + width] = refs[l * n_lay + i][...]
            row = _SM_CONVW + l * CONV_K
            o_ref[row:row + CONV_K, 0:CONV_DIM] = refs[DEPTH * n_lay + l][...]
        o_ref[_SM_FINAL:_SM_FINAL + 1, :] = refs[DEPTH * n_lay + DEPTH][...]
        o_ref[_SM_LOSS:_SM_LOSS + 1, 0:LANES] = refs[DEPTH * n_lay + DEPTH + 1][...]

    return pl.pallas_call(
        body, name="pack_small", in_specs=[VMEM_SPEC] * len(args), out_specs=VMEM_SPEC,
        out_shape=jax.ShapeDtypeStruct((SMALL_ROWS, PACK_COLS), F32),
    )(*args)


def _small_update(buf, w, mom, var):
    lay = _small_layout()
    names = [name for name, _, _, _ in lay] + ["final_g", "conv_w"]
    conv_cols = CONV_DIM // N_CHIPS
    shapes2d = {name: (DEPTH, width) for name, _, _, width in lay}
    shapes2d["final_g"] = (1, D_MODEL)
    shapes2d["conv_w"] = (DEPTH * CONV_K, conv_cols)
    args = [buf]
    for src in (w, mom, var):
        args += [src[name].reshape(shapes2d[name]) for name in names]
    n = len(names)

    def body(*refs):
        b_ref = refs[0]
        w_refs, m_refs, v_refs = refs[1:1 + n], refs[1 + n:1 + 2 * n], refs[1 + 2 * n:1 + 3 * n]
        outs = refs[1 + 3 * n:]
        chip = 2 * lax.axis_index("x") + lax.axis_index("y")
        for i, name in enumerate(names):
            if name == "final_g":
                g = b_ref[_SM_FINAL:_SM_FINAL + 1, :]
            elif name == "conv_w":
                rows = b_ref[_SM_CONVW:_SM_CONVW + DEPTH * CONV_K, 0:CONV_DIM]
                g = jnp.zeros((DEPTH * CONV_K, conv_cols), F32)
                for j in range(N_CHIPS):
                    g = g + jnp.where(chip == j, rows[:, j * conv_cols:(j + 1) * conv_cols], 0.0)
            else:
                _, rr, c0, width = lay[i]
                g = jnp.concatenate([b_ref[l * _SM_PER_LAYER + rr:l * _SM_PER_LAYER + rr + 1, c0:c0 + width]
                                     for l in range(DEPTH)], axis=0)
            d, m2, v2 = _adamw_math(w_refs[i][...], g, m_refs[i][...], v_refs[i][...])
            outs[i][...] = g
            outs[n + i][...] = d
            outs[2 * n + i][...] = m2
            outs[3 * n + i][...] = v2

    out_shape = [jax.ShapeDtypeStruct(shapes2d[name], F32) for _ in range(4) for name in names]
    res = pl.pallas_call(
        body, name="small_update", in_specs=[VMEM_SPEC] * len(args), out_specs=[VMEM_SPEC] * (4 * n),
        out_shape=out_shape,
    )(*args)
    return tuple({name: res[k * n + i].reshape(w[name].shape) for i, name in enumerate(names)} for k in range(4))


SMALL_NAMES = ("norm_mix_g", "conv_b", "dt_bias", "a_log", "d_skip", "ssd_norm_g", "sb_norm_g", "norm_xa_g",
               "norm_mem_g", "norm_ff_g", "final_g")
WEIGHT_ORDER = ("norm_mix_g", "w_in", "conv_w", "conv_b", "dt_bias", "a_log", "d_skip", "ssd_norm_g", "sb_norm_g",
                "w_out", "norm_xa_g", "norm_mem_g", "w_xq", "w_xk", "w_xv", "w_xo", "norm_ff_g", "w_ff1", "w_ff2",
                "final_g")


_ALL = tuple(range(len(MATS)))
_IN = tuple(i for i in _ALL if MATS[i][0] == "w_in")
_MLP = tuple(i for i in _ALL if MATS[i][0] in ("w_ff1", "w_ff2"))
_MIXER = tuple(i for i in _ALL if i not in _IN + _MLP)
_CONV = len(MATS)


class PipelinedPlan(LocalPlan):
    def __init__(self, shards, conv_w):
        self.shards = list(shards) + [conv_w]
        self.chip = 2 * lax.axis_index("x") + lax.axis_index("y")
        self.mats = [dict() for _ in range(DEPTH)]
        n = len(MATS)
        self.parts = [[None] * n for _ in range(DEPTH)]
        self.to_chips = [[None] * n for _ in range(DEPTH)]
        self.reduced = [[None] * n for _ in range(DEPTH)]
        self.riders = {}
        first = _IN + (_CONV,)
        self._gathered(0, first, _run_exchange("gather_first", self._gather(0, first)))
        self._gather_behind(0, _MIXER, "proj_fwd", 0)
        self._gather_behind(0, _MLP[:1], "ssd_fwd", 0)
        self._gather_behind(0, _MLP[1:], "sb_fwd", 0)
        for l in range(1, DEPTH):
            self._gather_behind(l, first, "sb_fwd", l - 1)
            self._gather_behind(l, _MIXER, "ssd_fwd", l)
            self._gather_behind(l, _MLP, "sb_fwd", l)

    def _ride(self, kernel, l, exchange, then):
        self.riders.setdefault((kernel, l), []).append((exchange, then))

    def carried_by(self, kernel, l):
        exchange = None
        for ex, _ in self.riders.get((kernel, l), []):
            exchange = _both(exchange, ex)
        return exchange

    def carried_out(self, kernel, l, outs):
        for ex, then in self.riders.pop((kernel, l), []):
            then(outs[:len(ex.outputs)])
            outs = outs[len(ex.outputs):]

    def _gather(self, l, which):
        return _gather_exchange(l, [self.shards[i] for i in which])

    def _gather_behind(self, l, which, kernel, host):
        self._ride(kernel, host, self._gather(l, which), lambda outs: self._gathered(l, which, outs))

    def _gathered(self, l, which, outs):
        for i, theirs in zip(which, outs):
            name, _, axis = MATS[i] if i != _CONV else ("conv_w", None, 1)
            full = jnp.concatenate([jnp.where(self.chip == j, self.shards[i][l], theirs[j]) for j in range(N_CHIPS)],
                                   axis=axis)
            self.mats[l][name] = _w_in_to_padded(full) if name == "w_in" else full

    def _set_parts(self, l, which, gm):
        for i in which:
            name, _, axis = MATS[i]
            g = _w_in_from_padded(gm[name]) if name == "w_in" else gm[name]
            if g.ndim == 2 and axis == 0:
                g = g.reshape((N_CHIPS, g.shape[0] // N_CHIPS, g.shape[1]))
            elif g.ndim == 2:
                g = jnp.swapaxes(g.reshape((g.shape[0], N_CHIPS, g.shape[1] // N_CHIPS)), 0, 1)
            self.parts[l][i] = g

    def _handover(self, l, which):
        return _handover_exchange(l, [self.parts[l][i] for i in which])

    def _handed(self, l, which, outs):
        for i, r in zip(which, outs):
            self.to_chips[l][i] = _add_handed(MATS[i][0] + str(l), l, self.parts[l][i], r)

    def _scatter(self, l, which):
        return _scatter_exchange(l, [self.to_chips[l][i] for i in which])

    def _scattered(self, l, which, outs):
        for i, got in zip(which, outs):
            self.reduced[l][i] = _sum_parts(MATS[i][0] + str(l), l, self.to_chips[l][i], got)

    def _send_behind(self, l, which, hand_kernel, cross_kernel, host):
        def handed(outs):
            self._handed(l, which, outs)
            self._ride(cross_kernel, host, self._scatter(l, which), lambda o: self._scattered(l, which, o))

        self._ride(hand_kernel, host, self._handover(l, which), handed)

    def mlp_grads_done(self, l, gm):
        self._set_parts(l, _MLP, gm)
        self._send_behind(l, _MLP, "xattn_bwd", "sb_bwd", l)

    def mixer_grads_done(self, l, gm):
        self._set_parts(l, _MIXER, gm)
        self._send_behind(l, _MIXER, "sb_bwd", "ssd_bwd", l)

    def grads_done(self, l, gm):
        self._set_parts(l, _IN, gm)
        if l > 0:
            self._send_behind(l, _IN, "mlp_bwd", "sb_bwd", l - 1)
        else:
            self._handed(0, _IN, _run_exchange("handover_last", self._handover(0, _IN)))
            self._scattered(0, _IN, _run_exchange("scatter_last", self._scatter(0, _IN)))

    def reduced_gradients(self):
        returned = _run_exchange("return_reduced", _return_exchange(self.reduced))
        n = len(MATS)
        return [([self.reduced[l][i] for l in range(DEPTH)], [returned[l * n + i] for l in range(DEPTH)])
                for i in range(n)]


def kernel(x, mem, norm_mix_g, w_in, conv_w, conv_b, dt_bias, a_log, d_skip, ssd_norm_g, sb_norm_g, w_out, norm_xa_g, norm_mem_g, w_xq, w_xk, w_xv, w_xo, norm_ff_g, w_ff1, w_ff2, final_g, loss_target, m_norm_mix_g, m_w_in, m_conv_w, m_conv_b, m_dt_bias, m_a_log, m_d_skip, m_ssd_norm_g, m_sb_norm_g, m_w_out, m_norm_xa_g, m_norm_mem_g, m_w_xq, m_w_xk, m_w_xv, m_w_xo, m_norm_ff_g, m_w_ff1, m_w_ff2, m_final_g, v_norm_mix_g, v_w_in, v_conv_w, v_conv_b, v_dt_bias, v_a_log, v_d_skip, v_ssd_norm_g, v_sb_norm_g, v_w_out, v_norm_xa_g, v_norm_mem_g, v_w_xq, v_w_xk, v_w_xv, v_w_xo, v_norm_ff_g, v_w_ff1, v_w_ff2, v_final_g):
    w = dict(norm_mix_g=norm_mix_g, w_in=w_in, conv_w=conv_w, conv_b=conv_b, dt_bias=dt_bias, a_log=a_log,
             d_skip=d_skip, ssd_norm_g=ssd_norm_g, sb_norm_g=sb_norm_g, w_out=w_out, norm_xa_g=norm_xa_g,
             norm_mem_g=norm_mem_g, w_xq=w_xq, w_xk=w_xk, w_xv=w_xv, w_xo=w_xo, norm_ff_g=norm_ff_g, w_ff1=w_ff1,
             w_ff2=w_ff2, final_g=final_g)
    mom = dict(norm_mix_g=m_norm_mix_g, w_in=m_w_in, conv_w=m_conv_w, conv_b=m_conv_b, dt_bias=m_dt_bias,
               a_log=m_a_log, d_skip=m_d_skip, ssd_norm_g=m_ssd_norm_g, sb_norm_g=m_sb_norm_g, w_out=m_w_out,
               norm_xa_g=m_norm_xa_g, norm_mem_g=m_norm_mem_g, w_xq=m_w_xq, w_xk=m_w_xk, w_xv=m_w_xv, w_xo=m_w_xo,
               norm_ff_g=m_norm_ff_g, w_ff1=m_w_ff1, w_ff2=m_w_ff2, final_g=m_final_g)
    var = dict(norm_mix_g=v_norm_mix_g, w_in=v_w_in, conv_w=v_conv_w, conv_b=v_conv_b, dt_bias=v_dt_bias,
               a_log=v_a_log, d_skip=v_d_skip, ssd_norm_g=v_ssd_norm_g, sb_norm_g=v_sb_norm_g, w_out=v_w_out,
               norm_xa_g=v_norm_xa_g, norm_mem_g=v_norm_mem_g, w_xq=v_w_xq, w_xk=v_w_xk, w_xv=v_w_xv, w_xo=v_w_xo,
               norm_ff_g=v_norm_ff_g, w_ff1=v_w_ff1, w_ff2=v_w_ff2, final_g=v_final_g)
    for params in (w, mom, var):
        params["w_in"] = jnp.swapaxes(params["w_in"], 1, 2)

    sw = {name: w[name] for name in SMALL_NAMES}
    plan = PipelinedPlan([w[name].astype(BF16) for name, _, _ in MATS], conv_w)
    loss, grad_x, gm, gs = _local_step(x[0], mem[0], loss_target[0], sw, plan)
    g_mats = plan.reduced_gradients()

    _, small_sum = _gather_small("_grads", _pack_small(gs, loss))
    loss_out = small_sum[_SM_LOSS, 0]

    grads, deltas, new_m, new_v = {}, {}, {}, {}
    for (name, _, _), (computed, received) in zip(MATS, g_mats):
        grads[name], deltas[name], new_m[name], new_v[name] = _adamw(
            name, w[name], computed, received, mom[name], var[name])
    g_s, d_s, m_s, v_s = _small_update(small_sum, w, mom, var)
    for name in g_s:
        grads[name], deltas[name], new_m[name], new_v[name] = g_s[name], d_s[name], m_s[name], v_s[name]
    for out in (grads, deltas, new_m, new_v):
        out["w_in"] = jnp.swapaxes(out["w_in"], 1, 2)

    return (loss_out, grad_x[None], *[grads[n] for n in WEIGHT_ORDER], *[deltas[n] for n in WEIGHT_ORDER],
            *[new_m[n] for n in WEIGHT_ORDER], *[new_v[n] for n in WEIGHT_ORDER])
```

```python
import functools
import math

import jax
import jax.numpy as jnp
from jax import lax
from jax.experimental import pallas as pl
from jax.experimental.pallas import tpu as pltpu

F32 = jnp.float32
BF16 = jnp.bfloat16
MESH = pl.DeviceIdType.MESH

D_MODEL = 1024
DEPTH = 2
SSD_DIM = 512
SSD_HEAD_DIM = 64
SSD_HEADS = 8
SSD_GROUPS = 2
SSD_STATE = 64
CONV_K = 4
CHUNK = 128
SB_DIM = 512
SB_HEAD_DIM = 64
XA_HEADS = 4
XA_HEAD_DIM = 128
XA_DIM = 512
D_FF = 4096
EPS = 1e-5
GN = SSD_GROUPS * SSD_STATE
CONV_DIM = SSD_DIM + 2 * GN
IN_DIM = SSD_DIM + CONV_DIM + SSD_HEADS + 3 * SB_DIM
LANES = 128
DT_PAD = LANES
IN_PAD = SSD_DIM + CONV_DIM + 3 * SB_DIM + DT_PAD
Q_OFF = SSD_DIM + CONV_DIM
DT_OFF = Q_OFF + 3 * SB_DIM
HALO = 8

ADAM_LR = 0.001
ADAM_B1 = 0.9
ADAM_B2 = 0.999
ADAM_EPS = 1e-08
ADAM_WD = 0.01
ADAM_STEP = 10

N_CHIPS = 4
N_DEV = 8
PACK_COLS = 1024
VMEM_LIMIT = 56 * 1024 * 1024

MATS = (
    ("w_in", (IN_DIM, D_MODEL), 0),
    ("w_out", (D_MODEL, D_MODEL), 0),
    ("w_xq", (D_MODEL, XA_DIM), 0),
    ("w_xk", (D_MODEL, XA_DIM), 0),
    ("w_xv", (D_MODEL, XA_DIM), 0),
    ("w_xo", (XA_DIM, D_MODEL), 1),
    ("w_ff1", (D_MODEL, D_FF), 1),
    ("w_ff2", (D_FF, D_MODEL), 0),
)


SMALL_ROWS = 24
_SM_PER_LAYER = 7
_SM_FINAL = 14
_SM_CONVW = 15
_SM_LOSS = 23


_NN = ((1,), (0,))
_NT = ((1,), (1,))
_TN = ((0,), (0,))


def _dg(a, b, dims):
    return lax.dot_general(a.astype(BF16), b.astype(BF16), (dims, ((), ())), preferred_element_type=F32)


@jax.custom_vjp
def mm_nn(a, b):
    return _dg(a, b, _NN)


@jax.custom_vjp
def mm_nt(a, b):
    return _dg(a, b, _NT)


@jax.custom_vjp
def mm_tn(a, b):
    return _dg(a, b, _TN)


def _nn_fwd(a, b):
    return _dg(a, b, _NN), (a, b)


def _nn_bwd(res, g):
    a, b = res
    return mm_nt(g, b).astype(a.dtype), mm_tn(a, g).astype(b.dtype)


def _nt_fwd(a, b):
    return _dg(a, b, _NT), (a, b)


def _nt_bwd(res, g):
    a, b = res
    return mm_nn(g, b).astype(a.dtype), mm_tn(g, a).astype(b.dtype)


def _tn_fwd(a, b):
    return _dg(a, b, _TN), (a, b)


def _tn_bwd(res, g):
    a, b = res
    return mm_nt(b, g).astype(a.dtype), mm_nn(a, g).astype(b.dtype)


mm_nn.defvjp(_nn_fwd, _nn_bwd)
mm_nt.defvjp(_nt_fwd, _nt_bwd)
mm_tn.defvjp(_tn_fwd, _tn_bwd)


def _rms(x, g):
    return x * lax.rsqrt(jnp.mean(x * x, axis=-1, keepdims=True) + EPS) * g


def _params(sem=None, vmem=VMEM_LIMIT):
    return pltpu.CompilerParams(dimension_semantics=sem, vmem_limit_bytes=vmem)


class Exchange:
    def __init__(self, inputs, outputs, n_sems, start, end, mid=None):
        self.inputs, self.outputs, self.n_sems = list(inputs), list(outputs), n_sems
        self.start, self.mid, self.end = start, mid, end

    def specs(self):
        hbm = pl.BlockSpec(memory_space=pl.ANY)
        sems = [pltpu.SemaphoreType.DMA((self.n_sems,)), pltpu.SemaphoreType.DMA((self.n_sems,))]
        return [hbm] * len(self.inputs), [hbm] * len(self.outputs), sems

    def pick(self, refs, n_before_in, n_before_out):
        n_in, n_out = len(self.inputs), len(self.outputs)
        o0 = n_before_in + n_in + n_before_out
        return refs[n_before_in:n_before_in + n_in], refs[o0:o0 + n_out], refs[-2], refs[-1]

    def before_work(self, mine, first, mid=None):
        @pl.when(first)
        def _():
            self.start(*mine)

        if self.mid is not None and mid is not None:
            @pl.when(mid)
            def _():
                self.mid(*mine)

    def after_work(self, mine, last, mid_done):
        @pl.when(last)
        def _():
            if self.mid is not None and not mid_done:
                self.mid(*mine)
            self.end(*mine)


class _Shifted:
    def __init__(self, ref, base):
        self.ref, self.base = ref, base

    @property
    def at(self):
        return self

    def __getitem__(self, idx):
        return self.ref.at[self.base + idx]


def _both(a, b):
    if a is None or b is None:
        return a or b
    n_i, n_o, n_s = len(a.inputs), len(a.outputs), a.n_sems

    def joined(fa, fb):
        def f(i_refs, o_refs, ss, rs):
            if fa is not None:
                fa(i_refs[:n_i], o_refs[:n_o], ss, rs)
            if fb is not None:
                fb(i_refs[n_i:], o_refs[n_o:], _Shifted(ss, n_s), _Shifted(rs, n_s))
        return f

    mid = joined(a.mid, b.mid) if (a.mid is not None or b.mid is not None) else None
    return Exchange(a.inputs + b.inputs, a.outputs + b.outputs, n_s + b.n_sems, joined(a.start, b.start),
                    joined(a.end, b.end), mid)


def _rowcall(name, fn, rows, fulls, row_out, acc_out=(), tm=256, exchange=None):
    s = rows[0].shape[0]
    tm = min(tm, s)
    nt = s // tm
    n_r, n_f, n_ro, n_ao = len(rows), len(fulls), len(row_out), len(acc_out)
    n_xi = len(exchange.inputs) if exchange else 0

    def body(*refs):
        if exchange:
            mine = exchange.pick(refs, n_r + n_f, n_ro + n_ao)
            exchange.before_work(mine, pl.program_id(0) == 0, pl.program_id(0) == (3 * nt) // 4)
        ins = [r[...] for r in refs[: n_r + n_f]]
        outs = fn(*ins)
        o_refs = refs[n_r + n_f + n_xi:]
        for o_ref, val in zip(o_refs[:n_ro], outs[:n_ro]):
            o_ref[...] = val.astype(o_ref.dtype)
        if n_ao:
            first = pl.program_id(0) == 0

            @pl.when(first)
            def _():
                for o_ref, val in zip(o_refs[n_ro:], outs[n_ro:]):
                    o_ref[...] = val.astype(o_ref.dtype)

            @pl.when(jnp.logical_not(first))
            def _():
                for o_ref, val in zip(o_refs[n_ro:], outs[n_ro:]):
                    o_ref[...] += val.astype(o_ref.dtype)
        if exchange:
            exchange.after_work(mine, pl.program_id(0) == nt - 1, mid_done=True)

    in_specs = [pl.BlockSpec((tm, a.shape[1]), lambda i: (i, 0)) for a in rows]
    in_specs += [pl.BlockSpec(a.shape, lambda i: (0, 0), pipeline_mode=pl.Buffered(1)) for a in fulls]
    out_specs = [pl.BlockSpec((tm, c), lambda i: (i, 0)) for c, _ in row_out]
    out_specs += [pl.BlockSpec(shape, lambda i: (0, 0)) for shape, _ in acc_out]
    out_shape = [jax.ShapeDtypeStruct((s, c), dt) for c, dt in row_out]
    out_shape += [jax.ShapeDtypeStruct(shape, dt) for shape, dt in acc_out]
    args, scratch = [*rows, *fulls], []
    if exchange:
        x_in, x_out, scratch = exchange.specs()
        in_specs += x_in
        out_specs += x_out
        out_shape += exchange.outputs
        args += exchange.inputs
    return pl.pallas_call(
        body, name=name, grid=(nt,), in_specs=in_specs, out_specs=out_specs, out_shape=out_shape,
        scratch_shapes=scratch, compiler_params=_params(("arbitrary",)),
    )(*args)


def _mm_tn_call(name, a, b, tm, tn, tk):
    s, m = a.shape
    n = b.shape[1]
    tk = min(tk, s)

    def body(a_ref, b_ref, o_ref):
        d = _dg(a_ref[...], b_ref[...], _TN)
        first = pl.program_id(2) == 0

        @pl.when(first)
        def _():
            o_ref[...] = d

        @pl.when(jnp.logical_not(first))
        def _():
            o_ref[...] += d

    return pl.pallas_call(
        body, name=name, grid=(m // tm, n // tn, s // tk),
        in_specs=[pl.BlockSpec((tk, tm), lambda i, j, k: (k, i)), pl.BlockSpec((tk, tn), lambda i, j, k: (k, j))],
        out_specs=pl.BlockSpec((tm, tn), lambda i, j, k: (i, j)),
        out_shape=jax.ShapeDtypeStruct((m, n), F32),
        compiler_params=_params(("parallel", "parallel", "arbitrary")),
    )(a, b)


def _mm_tn_parts(name, a, b, by_cols, tm=1024, tk=2048):
    s, m = a.shape
    n = b.shape[1]
    r, c = (m, n // N_CHIPS) if by_cols else (m // N_CHIPS, n)
    per = r // tm
    tk = min(tk, s)
    nk = s // tk

    def body(a_ref, b_ref, o_ref, acc):
        d = _dg(a_ref[...], b_ref[...], _TN)
        step = pl.program_id(2)

        @pl.when(step == 0)
        def _():
            acc[...] = d

        @pl.when(step > 0)
        def _():
            acc[...] += d

        @pl.when(step == nk - 1)
        def _():
            o_ref[0] = acc[...].astype(o_ref.dtype)

    if by_cols:
        out_map = lambda i, j, k: (j, i, 0)
    else:
        out_map = lambda i, j, k: (i // per, i % per, 0)
    return pl.pallas_call(
        body, name=name, grid=(m // tm, n // c, s // tk),
        in_specs=[pl.BlockSpec((tk, tm), lambda i, j, k: (k, i)), pl.BlockSpec((tk, c), lambda i, j, k: (k, j))],
        out_specs=pl.BlockSpec((1, tm, c), out_map),
        out_shape=jax.ShapeDtypeStruct((N_CHIPS, r, c), BF16), scratch_shapes=[pltpu.VMEM((tm, c), F32)],
        compiler_params=_params(("parallel", "parallel", "arbitrary")),
    )(a, b)


def _proj_tile(h, g, w):
    p = mm_nn(_rms(h, g), w)
    return (p[:, :SSD_DIM], p[:, SSD_DIM:Q_OFF], p[:, Q_OFF:Q_OFF + SB_DIM],
            p[:, Q_OFF + SB_DIM:Q_OFF + 2 * SB_DIM], p[:, Q_OFF + 2 * SB_DIM:DT_OFF], p[:, DT_OFF:])


def _proj_fwd(tag, h, g, w, exchange=None):
    return _rowcall(
        "proj_fwd" + tag, _proj_tile, [h], [g, w],
        [(SSD_DIM, F32), (CONV_DIM, F32), (SB_DIM, BF16), (SB_DIM, BF16), (SB_DIM, BF16), (DT_PAD, F32)], tm=512,
        exchange=exchange)


def _proj_bwd(tag, h, dh_out, dz, dxbc, dq, dk, dv, ddt, g, w):
    def fn(h, dh_out, dz, dxbc, dq, dk, dv, ddt, g, w):
        dp = jnp.concatenate([dz.astype(BF16), dxbc.astype(BF16), dq.astype(BF16), dk.astype(BF16),
                              dv.astype(BF16), ddt.astype(BF16)], axis=1)
        hn, vjp = jax.vjp(_rms, h, g)
        dh, dg = vjp(mm_nt(dp, w))
        return dh_out + dh, hn, dp, dg

    return _rowcall(
        "proj_bwd" + tag, fn, [h, dh_out, dz, dxbc, dq, dk, dv, ddt], [g, w],
        [(D_MODEL, F32), (D_MODEL, BF16), (IN_PAD, BF16)], [((1, D_MODEL), F32)], tm=256)


def _shift_down(x, tail, j):
    if j == 0:
        return x
    r = pltpu.roll(x, j, 0)
    rt = pltpu.roll(tail, j, 0)
    row = lax.broadcasted_iota(jnp.int32, (HALO, x.shape[1]), 0)
    first = jnp.where(row < j, rt, r[:HALO])
    if x.shape[0] == HALO:
        return first
    return jnp.concatenate([first, r[HALO:]], axis=0)


def _shift_up(x, head, j):
    if j == 0:
        return x
    n = x.shape[0]
    r = pltpu.roll(x, n - j, 0)
    rh = pltpu.roll(head, HALO - j, 0)
    row = lax.broadcasted_iota(jnp.int32, (HALO, x.shape[1]), 0)
    return jnp.concatenate([r[:n - HALO], jnp.where(row >= HALO - j, rh, r[n - HALO:])], axis=0)


def _conv_pre(x, tail, w, b):
    acc = b + w[CONV_K - 1:CONV_K] * x
    for j in range(1, CONV_K):
        acc = acc + w[CONV_K - 1 - j:CONV_K - j] * _shift_down(x, tail, j)
    return acc


def _dsilu(p):
    s = jax.nn.sigmoid(p)
    return s * (1.0 + p * (1.0 - s))


def _conv_fwd(tag, xbc, w, b, tc=512):
    s, c = xbc.shape
    tc = min(tc, s)
    per = tc // HALO

    def body(x_ref, prev_ref, w_ref, b_ref, o_ref):
        tail = jnp.where(pl.program_id(0) > 0, prev_ref[...], 0.0)
        o_ref[...] = jax.nn.silu(_conv_pre(x_ref[...], tail, w_ref[...], b_ref[...]))

    return pl.pallas_call(
        body, name="conv_fwd" + tag, grid=(s // tc,),
        in_specs=[pl.BlockSpec((tc, c), lambda i: (i, 0)),
                  pl.BlockSpec((HALO, c), lambda i: (jnp.maximum(i * per - 1, 0), 0)),
                  pl.BlockSpec((CONV_K, c), lambda i: (0, 0)), pl.BlockSpec((1, c), lambda i: (0, 0))],
        out_specs=pl.BlockSpec((tc, c), lambda i: (i, 0)),
        out_shape=jax.ShapeDtypeStruct((s, c), F32),
        compiler_params=_params(("arbitrary",)),
    )(xbc, xbc, w, b)


def _conv_bwd(tag, xbc, dact, w, b, tc=512):
    s, c = xbc.shape
    tc = min(tc, s)
    per = tc // HALO
    nt = s // tc
    last_blk = s // HALO - 1

    def body(x_ref, prev_ref, next_ref, d_ref, dnext_ref, w_ref, b_ref, dx_ref, dw_ref, db_ref):
        i = pl.program_id(0)
        x = x_ref[...]
        wv = w_ref[...]
        tail = jnp.where(i > 0, prev_ref[...], 0.0)
        dpre = d_ref[...] * _dsilu(_conv_pre(x, tail, wv, b_ref[...]))
        pre_n = _conv_pre(next_ref[...], x[tc - HALO:], wv, b_ref[...])
        dpre_n = jnp.where(i < nt - 1, dnext_ref[...] * _dsilu(pre_n), 0.0)
        dx = wv[CONV_K - 1:CONV_K] * dpre
        for j in range(1, CONV_K):
            dx = dx + wv[CONV_K - 1 - j:CONV_K - j] * _shift_up(dpre, dpre_n, j)
        dx_ref[...] = dx
        dws = [jnp.sum(dpre * _shift_down(x, tail, CONV_K - 1 - k), axis=0, keepdims=True) for k in range(CONV_K)]
        dwv = jnp.concatenate(dws, axis=0)
        dbv = jnp.sum(dpre, axis=0, keepdims=True)

        @pl.when(i == 0)
        def _():
            dw_ref[...] = dwv
            db_ref[...] = dbv

        @pl.when(i > 0)
        def _():
            dw_ref[...] += dwv
            db_ref[...] += dbv

    tile = pl.BlockSpec((tc, c), lambda i: (i, 0))
    prev = pl.BlockSpec((HALO, c), lambda i: (jnp.maximum(i * per - 1, 0), 0))
    nxt = pl.BlockSpec((HALO, c), lambda i: (jnp.minimum((i + 1) * per, last_blk), 0))
    return pl.pallas_call(
        body, name="conv_bwd" + tag, grid=(nt,),
        in_specs=[tile, prev, nxt, tile, nxt, pl.BlockSpec((CONV_K, c), lambda i: (0, 0)),
                  pl.BlockSpec((1, c), lambda i: (0, 0))],
        out_specs=[tile, pl.BlockSpec((CONV_K, c), lambda i: (0, 0)), pl.BlockSpec((1, c), lambda i: (0, 0))],
        out_shape=[jax.ShapeDtypeStruct((s, c), F32), jax.ShapeDtypeStruct((CONV_K, c), F32),
                   jax.ShapeDtypeStruct((1, c), F32)],
        compiler_params=_params(("arbitrary",)),
    )(xbc, xbc, xbc, dact, dact, w, b)


def _ssd_chunk(xs, bm, cm, dtr, z, dt_bias, a_log, d_skip, g, s_prev):
    n = CHUNK
    row = lax.broadcasted_iota(jnp.int32, (n, n), 0)
    col = lax.broadcasted_iota(jnp.int32, (n, n), 1)
    causal = row >= col
    dt = jax.nn.softplus(dtr + dt_bias)
    a_c = dt * (-jnp.exp(a_log))
    hi = lax.Precision.HIGHEST
    a_cum = jnp.dot(causal.astype(F32), a_c, precision=hi, preferred_element_type=F32)
    a_cum_t = lax.dot_general(a_c, (row <= col).astype(F32), (_TN, ((), ())), precision=hi,
                              preferred_element_type=F32)
    p, st = SSD_HEAD_DIM, SSD_STATE
    heads = range(SSD_HEADS)
    grp = [h // (SSD_HEADS // SSD_GROUPS) for h in heads]
    bgs = [bm[:, k * st:(k + 1) * st] for k in range(SSD_GROUPS)]
    cgs = [cm[:, k * st:(k + 1) * st] for k in range(SSD_GROUPS)]
    cb = [mm_nt(cgs[k], bgs[k]) for k in range(SSD_GROUPS)]
    acols = [a_cum[:, h:h + 1] for h in heads]
    a_lasts = [a_cum[n - 1:n, h:h + 1] for h in heads]
    xhs = [xs[:, h * p:(h + 1) * p] for h in heads]
    sps = [s_prev[h * p:(h + 1) * p, :] for h in heads]
    xdts = [xhs[h] * dt[:, h:h + 1] for h in heads]
    decays = [jnp.exp(jnp.where(causal, acols[h] - a_cum_t[h:h + 1, :], -jnp.inf)) for h in heads]
    y_offs = [mm_nt(cgs[grp[h]], sps[h]) for h in heads]
    y_diags = [mm_nn(cb[grp[h]] * decays[h], xdts[h]) for h in heads]
    states = [mm_tn(xdts[h] * jnp.exp(a_lasts[h] - acols[h]), bgs[grp[h]]) for h in heads]
    s_new = [sps[h] * jnp.exp(a_lasts[h]) + states[h] for h in heads]
    ys = [y_diags[h] + y_offs[h] * jnp.exp(acols[h]) + d_skip[:, h:h + 1] * xhs[h] for h in heads]
    y = jnp.concatenate(ys, axis=1) * jax.nn.silu(z)
    return _rms(y, g), jnp.concatenate(s_new, axis=0)


def _split_xbc(t):
    return t[:, :SSD_DIM], t[:, SSD_DIM:SSD_DIM + GN], t[:, SSD_DIM + GN:]


def _ssd_fwd(tag, xact, dtr, z, dt_bias, a_log, d_skip, g, exchange=None):
    s = xact.shape[0]
    nc = s // CHUNK
    srows = SSD_HEADS * SSD_HEAD_DIM
    n_xi = len(exchange.inputs) if exchange else 0

    def body(*refs):
        x_ref, dt_ref, z_ref, b_ref, al_ref, ds_ref, g_ref = refs[:7]
        y_ref, st_ref = refs[7 + n_xi:9 + n_xi]
        state = refs[-3] if exchange else refs[-1]
        if exchange:
            mine = exchange.pick(refs, 7, 2)
            exchange.before_work(mine, pl.program_id(0) == 0, pl.program_id(0) == (3 * nc) // 4)

        @pl.when(pl.program_id(0) == 0)
        def _():
            state[...] = jnp.zeros_like(state)

        sp = state[...]
        st_ref[0] = sp
        xs, bm, cm = _split_xbc(x_ref[...])
        y, sn = _ssd_chunk(xs, bm, cm, dt_ref[...][:, :SSD_HEADS], z_ref[...], b_ref[...], al_ref[...],
                           ds_ref[...], g_ref[...], sp)
        y_ref[...] = y
        state[...] = sn
        if exchange:
            exchange.after_work(mine, pl.program_id(0) == nc - 1, mid_done=True)

    small = pl.BlockSpec((1, SSD_HEADS), lambda i: (0, 0))
    in_specs = [pl.BlockSpec((CHUNK, CONV_DIM), lambda i: (i, 0)), pl.BlockSpec((CHUNK, DT_PAD), lambda i: (i, 0)),
                pl.BlockSpec((CHUNK, SSD_DIM), lambda i: (i, 0)), small, small, small,
                pl.BlockSpec((1, SSD_DIM), lambda i: (0, 0))]
    out_specs = [pl.BlockSpec((CHUNK, SSD_DIM), lambda i: (i, 0)),
                 pl.BlockSpec((1, srows, SSD_STATE), lambda i: (i, 0, 0))]
    out_shape = [jax.ShapeDtypeStruct((s, SSD_DIM), F32), jax.ShapeDtypeStruct((nc, srows, SSD_STATE), F32)]
    args, scratch = [xact, dtr, z, dt_bias, a_log, d_skip, g], [pltpu.VMEM((srows, SSD_STATE), F32)]
    if exchange:
        x_in, x_out, sems = exchange.specs()
        in_specs, out_specs, scratch = in_specs + x_in, out_specs + x_out, scratch + sems
        out_shape, args = out_shape + exchange.outputs, args + exchange.inputs
    return pl.pallas_call(
        body, name="ssd_fwd" + tag, grid=(nc,), in_specs=in_specs, out_specs=out_specs, out_shape=out_shape,
        scratch_shapes=scratch, compiler_params=_params(("arbitrary",)),
    )(*args)


def _ssd_bwd(tag, xact, dtr, z, states, dy, dt_bias, a_log, d_skip, g, exchange=None):
    s = xact.shape[0]
    nc = s // CHUNK
    srows = SSD_HEADS * SSD_HEAD_DIM
    n_xi = len(exchange.inputs) if exchange else 0

    def body(*refs):
        x_ref, dt_ref, z_ref, sp_ref, dy_ref, b_ref, al_ref, ds_ref, g_ref = refs[:9]
        dx_ref, ddt_ref, dz_ref, db_ref, dal_ref, dds_ref, dg_ref = refs[9 + n_xi:16 + n_xi]
        dstate = refs[-3] if exchange else refs[-1]
        first = pl.program_id(0) == 0
        if exchange:
            mine = exchange.pick(refs, 9, 7)
            exchange.before_work(mine, first, pl.program_id(0) == (3 * nc) // 4)

        @pl.when(first)
        def _():
            dstate[...] = jnp.zeros_like(dstate)

        xs, bm, cm = _split_xbc(x_ref[...])
        _, vjp = jax.vjp(_ssd_chunk, xs, bm, cm, dt_ref[...][:, :SSD_HEADS], z_ref[...], b_ref[...], al_ref[...],
                         ds_ref[...], g_ref[...], sp_ref[0])
        dxs, dbm, dcm, ddt, dz, db, dal, dds, dg, dsp = vjp((dy_ref[...], dstate[...]))
        dx_ref[...] = jnp.concatenate([dxs, dbm, dcm], axis=1)
        ddt_ref[...] = jnp.concatenate([ddt, jnp.zeros((CHUNK, DT_PAD - SSD_HEADS), F32)], axis=1)
        dz_ref[...] = dz
        dstate[...] = dsp

        @pl.when(first)
        def _():
            db_ref[...] = db
            dal_ref[...] = dal
            dds_ref[...] = dds
            dg_ref[...] = dg

        @pl.when(jnp.logical_not(first))
        def _():
            db_ref[...] += db
            dal_ref[...] += dal
            dds_ref[...] += dds
            dg_ref[...] += dg

        if exchange:
            exchange.after_work(mine, pl.program_id(0) == nc - 1, mid_done=True)

    def rev(c):
        return lambda i: (nc - 1 - i, 0)

    small = pl.BlockSpec((1, SSD_HEADS), lambda i: (0, 0))
    gspec = pl.BlockSpec((1, SSD_DIM), lambda i: (0, 0))
    in_specs = [pl.BlockSpec((CHUNK, CONV_DIM), rev(0)), pl.BlockSpec((CHUNK, DT_PAD), rev(0)),
                pl.BlockSpec((CHUNK, SSD_DIM), rev(0)),
                pl.BlockSpec((1, srows, SSD_STATE), lambda i: (nc - 1 - i, 0, 0)),
                pl.BlockSpec((CHUNK, SSD_DIM), rev(0)), small, small, small, gspec]
    out_specs = [pl.BlockSpec((CHUNK, CONV_DIM), rev(0)), pl.BlockSpec((CHUNK, DT_PAD), rev(0)),
                 pl.BlockSpec((CHUNK, SSD_DIM), rev(0)), small, small, small, gspec]
    out_shape = [jax.ShapeDtypeStruct((s, CONV_DIM), F32), jax.ShapeDtypeStruct((s, DT_PAD), F32),
                 jax.ShapeDtypeStruct((s, SSD_DIM), F32), jax.ShapeDtypeStruct((1, SSD_HEADS), F32),
                 jax.ShapeDtypeStruct((1, SSD_HEADS), F32), jax.ShapeDtypeStruct((1, SSD_HEADS), F32),
                 jax.ShapeDtypeStruct((1, SSD_DIM), F32)]
    args, scratch = [xact, dtr, z, states, dy, dt_bias, a_log, d_skip, g], [pltpu.VMEM((srows, SSD_STATE), F32)]
    if exchange:
        x_in, x_out, sems = exchange.specs()
        in_specs, out_specs, scratch = in_specs + x_in, out_specs + x_out, scratch + sems
        out_shape, args = out_shape + exchange.outputs, args + exchange.inputs
    return pl.pallas_call(
        body, name="ssd_bwd" + tag, grid=(nc,), in_specs=in_specs, out_specs=out_specs, out_shape=out_shape,
        scratch_shapes=scratch, compiler_params=_params(("arbitrary",)),
    )(*args)


TQ = 128
TK = 128
SB_SCALE = 1.0 / math.sqrt(SB_HEAD_DIM)


def _split2(x):
    hi = x.astype(BF16)
    return hi, (x - hi.astype(F32)).astype(BF16)


TK_NEAR = 384
SB_UNDERFLOW = -110.0


def _sb_logits(qhs, kb, t0, s0, masked):
    zs = [_dg(qh, kb, _NT) for qh in qhs]
    mask = None
    if masked:
        t_pos = t0 + lax.broadcasted_iota(jnp.int32, zs[0].shape, 0)
        s_pos = s0 + lax.broadcasted_iota(jnp.int32, zs[0].shape, 1)
        mask = s_pos < t_pos
    lbs = [jnp.minimum(z, 0.0) - jnp.log(1.0 + jnp.exp(-jnp.abs(z))) for z in zs]
    lss = [lb - z for lb, z in zip(lbs, zs)]
    if masked:
        lss = [jnp.where(mask, ls, 0.0) for ls in lss]
    return lbs, lss, mask


def _running_sums(xs, starts, u, reverse, two_terms=True):
    nsub = xs[0].shape[1] // TK
    order = list(reversed(range(nsub))) if reverse else list(range(nsub))
    chunks = [[x[:, c * TK:(c + 1) * TK] for c in range(nsub)] for x in xs]
    sums = [[(_lane_sums(xc, u) if two_terms else _dg(xc, u, _NN)) for xc in row] for row in chunks]
    out = []
    for row, srow, run in zip(chunks, sums, starts):
        parts = [None] * nsub
        for c in order:
            parts[c] = run + srow[c]
            run = run + jnp.sum(row[c], axis=1, keepdims=True)
        out.append((parts[0] if nsub == 1 else jnp.concatenate(parts, axis=1), run))
    return out


def _lane_sums(x, u):
    hi, lo = _split2(x)
    return _dg(hi, u, _NN) + _dg(lo, u, _NN)


def _tri(cmp):
    j = lax.broadcasted_iota(jnp.int32, (TK, TK), 0)
    s = lax.broadcasted_iota(jnp.int32, (TK, TK), 1)
    return cmp(j, s).astype(BF16)


def _sb_fwd(tag, q, k, v, exchange=None):
    s = q.shape[0]
    npair = SB_DIM // LANES
    nq = s // TQ
    near = min(TK_NEAR, s)
    n_xi = len(exchange.inputs) if exchange else 0

    def body(*refs):
        q_ref, k_ref, v_ref = refs[:3]
        o_ref, t_ref = refs[3 + n_xi:5 + n_xi]
        tq = pl.program_id(1)
        if exchange:
            pair = pl.program_id(0)
            mine = exchange.pick(refs, 3, 2)
            exchange.before_work(mine, jnp.logical_and(pair == 0, tq == 0),
                                 jnp.logical_and(pair == npair - 1, tq == 0))
        qp = q_ref[...]
        lane = lax.broadcasted_iota(jnp.int32, (1, LANES), 1)
        u_gt = _tri(lambda j, s: j > s)
        heads = range(LANES // SB_HEAD_DIM)
        hms = [(lane // SB_HEAD_DIM) == hh for hh in heads]
        qhs = [jnp.where(hm, qp, jnp.zeros_like(qp)) * SB_SCALE for hm in hms]

        def block(wb, width, carry, masked):
            off = pl.multiple_of(wb * TK, TK)
            kb = k_ref[pl.ds(off, width), :]
            vb = v_ref[pl.ds(off, width), :]
            lbs, lss, mask = _sb_logits(qhs, kb, tq * TQ, wb * TK, masked)
            sums = _running_sums(lss, [c[0] for c in carry], u_gt, reverse=True)
            ws = [jnp.exp(lb + later) for lb, (later, _) in zip(lbs, sums)]
            if masked:
                ws = [jnp.where(mask, w, 0.0) for w in ws]
            pvs = [_dg(w, vb, _NN) for w in ws]
            return tuple((r, c[1] + pv) for (_, r), c, pv in zip(sums, carry, pvs))

        def more(c):
            alive = jnp.max(c[1][0][0])
            for hh in heads[1:]:
                alive = jnp.maximum(alive, jnp.max(c[1][hh][0]))
            return jnp.logical_and(c[0] >= 0, alive > SB_UNDERFLOW)

        start = tuple((jnp.zeros((TQ, 1), F32), jnp.zeros((TQ, LANES), F32)) for _ in heads)
        near_blk = jnp.maximum(tq - (near // TK - 1), 0)
        wb, done = lax.while_loop(more, lambda c: (c[0] - 1, block(c[0], TK, c[1], False)),
                                  (near_blk - 1, block(near_blk, near, start, True)))
        first = (wb + 1).astype(F32)
        out = jnp.zeros((TQ, LANES), F32)
        tot = jnp.zeros((TQ, LANES), F32)
        for hh in heads:
            r, acc = done[hh]
            out = out + jnp.where(hms[hh], acc, 0.0)
            tot = tot + jnp.where(hms[hh], jnp.where(lane % SB_HEAD_DIM == 1, first, r), 0.0)
        o_ref[...] = out
        t_ref[...] = tot
        if exchange:
            exchange.after_work(mine, jnp.logical_and(pair == npair - 1, tq == nq - 1), mid_done=True)

    tile = pl.BlockSpec((TQ, LANES), lambda p, t: (t, p))
    full = pl.BlockSpec((s, LANES), lambda p, t: (0, p))
    in_specs, out_specs = [tile, full, full], [tile, tile]
    out_shape = [jax.ShapeDtypeStruct((s, SB_DIM), F32)] * 2
    args, scratch = [q, k, v], []
    if exchange:
        x_in, x_out, scratch = exchange.specs()
        in_specs, out_specs = in_specs + x_in, out_specs + x_out
        out_shape, args = out_shape + exchange.outputs, args + exchange.inputs
    return pl.pallas_call(
        body, name="sb_fwd" + tag, grid=(npair, nq), in_specs=in_specs, out_specs=out_specs, out_shape=out_shape,
        scratch_shapes=scratch, compiler_params=_params(("arbitrary", "arbitrary")),
    )(*args)


def _sb_bwd(tag, q, k, v, tot, do, exchange=None):
    s = q.shape[0]
    npair = SB_DIM // LANES
    nq = s // TQ
    near = min(TK_NEAR, s)
    n_xi = len(exchange.inputs) if exchange else 0

    def body(*refs):
        q_ref, k_ref, v_ref, t_ref, do_ref = refs[:5]
        dq_ref, dk_ref, dv_ref = refs[5 + n_xi:8 + n_xi]
        tq = pl.program_id(1)
        if exchange:
            pair = pl.program_id(0)
            mine = exchange.pick(refs, 5, 3)
            exchange.before_work(mine, jnp.logical_and(pair == 0, tq == 0),
                                 jnp.logical_and(pair == npair - 1, tq == 0))

        @pl.when(tq == 0)
        def _():
            dk_ref[...] = jnp.zeros_like(dk_ref)
            dv_ref[...] = jnp.zeros_like(dv_ref)

        qp = q_ref[...]
        dop = do_ref[...]
        totp = t_ref[...]
        lane = lax.broadcasted_iota(jnp.int32, (1, LANES), 1)
        u_le = _tri(lambda j, s: j <= s)
        u_lt = _tri(lambda j, s: j < s)
        heads = range(LANES // SB_HEAD_DIM)
        hms = [(lane // SB_HEAD_DIM) == hh for hh in heads]
        qhs = [jnp.where(hm, qp, jnp.zeros_like(qp)) * SB_SCALE for hm in hms]
        dohs = [jnp.where(hm, dop, 0.0).astype(BF16) for hm in hms]
        totals = [jnp.sum(jnp.where(lane == hh * SB_HEAD_DIM, totp, 0.0), axis=1, keepdims=True) for hh in heads]
        first = jnp.max(jnp.where(lane == 1, totp, 0.0)).astype(jnp.int32)

        def block(wb, width, carry, masked):
            off = pl.multiple_of(wb * TK, TK)
            kb = k_ref[pl.ds(off, width), :]
            vb = v_ref[pl.ds(off, width), :]
            lbs, lss, mask = _sb_logits(qhs, kb, tq * TQ, wb * TK, masked)
            dws = [_dg(doh, vb, _NT) for doh in dohs]
            pres = _running_sums(lss, [c[0] for c in carry], u_le, reverse=False)
            ws = [jnp.exp(lb + (total - before)) for lb, total, (before, _) in zip(lbs, totals, pres)]
            if masked:
                ws = [jnp.where(mask, w, 0.0) for w in ws]
            gs = [w * dw for w, dw in zip(ws, dws)]
            lefts = _running_sums(gs, [c[1] for c in carry], u_lt, reverse=False, two_terms=False)
            dzs = [g - jnp.exp(lb) * (g + g_left) for g, lb, (g_left, _) in zip(gs, lbs, lefts)]
            if masked:
                dzs = [jnp.where(mask, dz, 0.0) for dz in dzs]
            dzbs = [dz.astype(BF16) for dz in dzs]
            dks = [_dg(dzb, qh, _TN) for dzb, qh in zip(dzbs, qhs)]
            dvs = [_dg(w, doh, _TN) for w, doh in zip(ws, dohs)]
            dqs = [_dg(dzb, kb, _NN) for dzb in dzbs]
            dk_ref[pl.ds(off, width), :] += functools.reduce(jnp.add, dks)
            dv_ref[pl.ds(off, width), :] += functools.reduce(jnp.add, dvs)
            return tuple((pre, gc, c[2] + dq) for (_, pre), (_, gc), c, dq in zip(pres, lefts, carry, dqs))

        zero = jnp.zeros((TQ, 1), F32)
        start = tuple((zero, zero, jnp.zeros((TQ, LANES), F32)) for _ in heads)
        near_blk = jnp.maximum(tq - (near // TK - 1), 0)
        far = lax.fori_loop(first, near_blk, lambda j, c: block(j, TK, c, False), start)
        done = block(near_blk, near, far, True)
        dq = jnp.zeros((TQ, LANES), F32)
        for hh in heads:
            dq = dq + jnp.where(hms[hh], done[hh][2], 0.0)
        dq_ref[...] = dq * SB_SCALE
        if exchange:
            exchange.after_work(mine, jnp.logical_and(pair == npair - 1, tq == nq - 1), mid_done=True)

    tile = pl.BlockSpec((TQ, LANES), lambda p, t: (t, p))
    full = pl.BlockSpec((s, LANES), lambda p, t: (0, p))
    in_specs, out_specs = [tile, full, full, tile, tile], [tile, full, full]
    out_shape = [jax.ShapeDtypeStruct((s, SB_DIM), F32)] * 3
    args, scratch = [q, k, v, tot, do], []
    if exchange:
        x_in, x_out, scratch = exchange.specs()
        in_specs, out_specs = in_specs + x_in, out_specs + x_out
        out_shape, args = out_shape + exchange.outputs, args + exchange.inputs
    return pl.pallas_call(
        body, name="sb_bwd" + tag, grid=(npair, nq), in_specs=in_specs, out_specs=out_specs, out_shape=out_shape,
        scratch_shapes=scratch, compiler_params=_params(("arbitrary", "arbitrary")),
    )(*args)


def _out_tile(y_ssd, o, sb_g, w_out):
    y_all = jnp.concatenate([y_ssd, _rms(o, sb_g)], axis=1)
    return mm_nn(y_all, w_out)


def _out_fwd(tag, h, y_ssd, o, sb_g, w_out):
    return _rowcall("out_fwd" + tag, lambda h, y, o, g, w: (h + _out_tile(y, o, g, w),),
                    [h, y_ssd, o], [sb_g, w_out], [(D_MODEL, F32)], tm=512)[0]


def _out_bwd(tag, y_ssd, o, dh, sb_g, w_out):
    def fn(y, o, dh, g, w):
        _, vjp = jax.vjp(_out_tile, y, o, g, w.astype(F32))
        return vjp(dh)

    return _rowcall("out_bwd" + tag, fn, [y_ssd, o, dh], [sb_g, w_out], [(SSD_DIM, F32), (SB_DIM, F32)],
                    [((1, SB_DIM), F32), ((D_MODEL, D_MODEL), F32)], tm=512)


def _mem_tile(mem, g, w_k, w_v):
    m = _rms(mem, g)
    return mm_nn(m, w_k), mm_nn(m, w_v)


def _mem_fwd(tag, mem, g, w_k, w_v):
    return _rowcall("mem_fwd" + tag, _mem_tile, [mem], [g, w_k, w_v], [(XA_DIM, F32), (XA_DIM, F32)], tm=256)


def _mem_bwd(tag, mem, dkx, dvx, g, w_k, w_v):
    def fn(mem, dkx, dvx, g, w_k, w_v):
        _, vjp = jax.vjp(lambda g, a, b: _mem_tile(mem, g, a, b), g, w_k.astype(F32), w_v.astype(F32))
        return vjp((dkx, dvx))

    return _rowcall("mem_bwd" + tag, fn, [mem, dkx, dvx], [g, w_k, w_v], [],
                    [((1, D_MODEL), F32), ((D_MODEL, XA_DIM), F32), ((D_MODEL, XA_DIM), F32)], tm=256)


def _xattn_tile(h, g, w_q, kx, vx, w_o):
    q = mm_nn(_rms(h, g), w_q)
    scale = 1.0 / math.sqrt(XA_HEAD_DIM)
    outs = []
    for i in range(XA_HEADS):
        sl = slice(i * XA_HEAD_DIM, (i + 1) * XA_HEAD_DIM)
        p = jax.nn.softmax(mm_nt(q[:, sl], kx[:, sl]) * scale, axis=-1)
        outs.append(mm_nn(p, vx[:, sl]))
    return mm_nn(jnp.concatenate(outs, axis=1), w_o)


def _xattn_fwd(tag, h, g, w_q, kx, vx, w_o):
    return _rowcall("xattn_fwd" + tag, lambda h, g, wq, kx, vx, wo: (h + _xattn_tile(h, g, wq, kx, vx, wo),),
                    [h], [g, w_q, kx, vx, w_o], [(D_MODEL, F32)], tm=512)[0]


def _xattn_bwd(tag, h, dh_out, g, w_q, kx, vx, w_o, exchange=None):
    def fn(h, dh_out, g, w_q, kx, vx, w_o):
        _, vjp = jax.vjp(_xattn_tile, h, g, w_q.astype(F32), kx, vx, w_o.astype(F32))
        dh, dg, dwq, dkx, dvx, dwo = vjp(dh_out)
        return dh_out + dh, dg, dwq, dkx, dvx, dwo

    mlen = kx.shape[0]
    return _rowcall("xattn_bwd" + tag, fn, [h, dh_out], [g, w_q, kx, vx, w_o], [(D_MODEL, F32)],
                    [((1, D_MODEL), F32), ((D_MODEL, XA_DIM), F32), ((mlen, XA_DIM), F32), ((mlen, XA_DIM), F32),
                     ((XA_DIM, D_MODEL), F32)], tm=256, exchange=exchange)


def _mlp_fwd(tag, h, g, w1, w2, exchange=None):
    def fn(h, g, w1, w2):
        r = jnp.maximum(mm_nn(_rms(h, g), w1), 0.0)
        return h + mm_nn(r * r, w2), r

    return _rowcall("mlp_fwd" + tag, fn, [h], [g, w1, w2], [(D_MODEL, F32), (D_FF, BF16)], tm=256,
                    exchange=exchange)


def _mlp_bwd(tag, h, relu, dh_out, g, w1, w2, exchange=None):
    def fn(h, relu, dh_out, g, w1, w2):
        hn, vjp = jax.vjp(_rms, h, g)
        r = relu.astype(F32)
        dob = dh_out.astype(BF16)
        dp = mm_nt(dob, w2) * (2.0 * r)
        dh, dg = vjp(mm_nt(dp, w1))
        return dh_out + dh, hn, dp, r * r, dob, dg

    return _rowcall("mlp_bwd" + tag, fn, [h, relu, dh_out], [g, w1, w2],
                    [(D_MODEL, F32), (D_MODEL, BF16), (D_FF, BF16), (D_FF, BF16), (D_MODEL, BF16)],
                    [((1, D_MODEL), F32)], tm=256, exchange=exchange)


def _head(h, g, target):
    def lossfn(h, g, t):
        err = jnp.square(_rms(h, g) - t)
        return 0.5 * jnp.sum(jnp.mean(err, axis=-1))

    def fn(h, t, g):
        loss, vjp = jax.vjp(lambda h, g: lossfn(h, g, t), h, g)
        dh, dg = vjp(jnp.ones((), F32))
        return dh, jnp.full((1, LANES), loss, F32), dg

    return _rowcall("head", fn, [h, target], [g], [(D_MODEL, F32)], [((1, LANES), F32), ((1, D_MODEL), F32)], tm=512)


def _row(v):
    return v.reshape(1, -1)


class LocalPlan:
    def __init__(self, mats):
        self.mats = mats

    def weights(self, l):
        return self.mats[l]

    def carried_by(self, kernel, l):
        return None

    def carried_out(self, kernel, l, outs):
        pass

    def mlp_grads_done(self, l, gm):
        pass

    def mixer_grads_done(self, l, gm):
        pass

    def grads_done(self, l, gm):
        pass


def _local_step(x, mem, target, sw, plan):
    h = x
    saved = []
    for l in range(DEPTH):
        tag = str(l)
        m = plan.weights(l)
        z, xbc, q, k, v, dtr, *carried = _proj_fwd(tag, h, _row(sw["norm_mix_g"][l]), m["w_in"],
                                                   exchange=plan.carried_by("proj_fwd", l))
        plan.carried_out("proj_fwd", l, carried)
        xact = _conv_fwd(tag, xbc, m["conv_w"], _row(sw["conv_b"][l]))
        y_ssd, states, *carried = _ssd_fwd(tag, xact, dtr, z, _row(sw["dt_bias"][l]), _row(sw["a_log"][l]),
                                           _row(sw["d_skip"][l]), _row(sw["ssd_norm_g"][l]),
                                           exchange=plan.carried_by("ssd_fwd", l))
        plan.carried_out("ssd_fwd", l, carried)
        o, sb_tot, *carried = _sb_fwd(tag, q, k, v, exchange=plan.carried_by("sb_fwd", l))
        plan.carried_out("sb_fwd", l, carried)
        h1 = _out_fwd(tag, h, y_ssd, o, _row(sw["sb_norm_g"][l]), m["w_out"])
        kx, vx = _mem_fwd(tag, mem, _row(sw["norm_mem_g"][l]), m["w_xk"], m["w_xv"])
        h2 = _xattn_fwd(tag, h1, _row(sw["norm_xa_g"][l]), m["w_xq"], kx, vx, m["w_xo"])
        h3, relu, *carried = _mlp_fwd(tag, h2, _row(sw["norm_ff_g"][l]), m["w_ff1"], m["w_ff2"],
                                      exchange=plan.carried_by("mlp_fwd", l))
        plan.carried_out("mlp_fwd", l, carried)
        saved.append((h, z, xbc, q, k, v, dtr, xact, y_ssd, states, o, sb_tot, h1, kx, vx, h2, relu))
        h = h3

    dh, loss, d_final = _head(h, _row(sw["final_g"]), target)
    gm = [dict() for _ in range(DEPTH)]
    gs = {name: [None] * DEPTH for name in ("norm_mix_g", "conv_w", "conv_b", "dt_bias", "a_log", "d_skip",
                                            "ssd_norm_g", "sb_norm_g", "norm_xa_g", "norm_mem_g", "norm_ff_g")}
    for l in reversed(range(DEPTH)):
        tag = str(l)
        m = plan.weights(l)
        h0, z, xbc, q, k, v, dtr, xact, y_ssd, states, o, sb_tot, h1, kx, vx, h2, relu = saved[l]
        dh2, hn_b, dp_b, a_b, do_b, gs["norm_ff_g"][l], *carried = _mlp_bwd(
            tag, h2, relu, dh, _row(sw["norm_ff_g"][l]), m["w_ff1"], m["w_ff2"],
            exchange=plan.carried_by("mlp_bwd", l))
        plan.carried_out("mlp_bwd", l, carried)
        gm[l]["w_ff1"] = _mm_tn_parts("dw_ff1" + tag, hn_b, dp_b, True)
        gm[l]["w_ff2"] = _mm_tn_parts("dw_ff2" + tag, a_b, do_b, False)
        plan.mlp_grads_done(l, gm[l])
        dh1, gs["norm_xa_g"][l], gm[l]["w_xq"], dkx, dvx, gm[l]["w_xo"], *carried = _xattn_bwd(
            tag, h1, dh2, _row(sw["norm_xa_g"][l]), m["w_xq"], kx, vx, m["w_xo"],
            exchange=plan.carried_by("xattn_bwd", l))
        plan.carried_out("xattn_bwd", l, carried)
        gs["norm_mem_g"][l], gm[l]["w_xk"], gm[l]["w_xv"] = _mem_bwd(
            tag, mem, dkx, dvx, _row(sw["norm_mem_g"][l]), m["w_xk"], m["w_xv"])
        dy_ssd, do, gs["sb_norm_g"][l], gm[l]["w_out"] = _out_bwd(
            tag, y_ssd, o, dh1, _row(sw["sb_norm_g"][l]), m["w_out"])
        plan.mixer_grads_done(l, gm[l])
        dq, dk, dv, *carried = _sb_bwd(tag, q, k, v, sb_tot, do, exchange=plan.carried_by("sb_bwd", l))
        plan.carried_out("sb_bwd", l, carried)
        dxact, ddtr, dz, gs["dt_bias"][l], gs["a_log"][l], gs["d_skip"][l], gs["ssd_norm_g"][l], *carried = _ssd_bwd(
            tag, xact, dtr, z, states, dy_ssd, _row(sw["dt_bias"][l]), _row(sw["a_log"][l]),
            _row(sw["d_skip"][l]), _row(sw["ssd_norm_g"][l]), exchange=plan.carried_by("ssd_bwd", l))
        plan.carried_out("ssd_bwd", l, carried)
        dxbc, gs["conv_w"][l], gs["conv_b"][l] = _conv_bwd(tag, xbc, dxact, m["conv_w"], _row(sw["conv_b"][l]))
        dh, hn_b, dp_b, gs["norm_mix_g"][l] = _proj_bwd(
            tag, h0, dh1, dz, dxbc, dq, dk, dv, ddtr, _row(sw["norm_mix_g"][l]), m["w_in"])
        gm[l]["w_in"] = _mm_tn_call("dw_in" + tag, dp_b, hn_b, IN_PAD, 512, 1024)
        plan.grads_done(l, gm[l])
    gs["final_g"] = d_final
    return loss, dh, gm, gs


VMEM_SPEC = pl.BlockSpec(memory_space=pltpu.VMEM)


def _place():
    return lax.axis_index("x"), lax.axis_index("y"), lax.axis_index("c")


def _other_chips(x, y):
    return [(1 - x, y), (x, 1 - y), (1 - x, 1 - y)]


def _remote(send_sems, recv_sems, idx, src, dst, to):
    return pltpu.make_async_remote_copy(src_ref=src, dst_ref=dst, send_sem=send_sems.at[idx],
                                        recv_sem=recv_sems.at[idx], device_id=to, device_id_type=MESH)


def _run_exchange(name, ex):
    def body(*refs):
        mine = ex.pick(refs, 0, 0)
        ex.start(*mine)
        if ex.mid is not None:
            ex.mid(*mine)
        ex.end(*mine)

    x_in, x_out, scratch = ex.specs()
    return pl.pallas_call(body, name=name, in_specs=x_in, out_specs=x_out, out_shape=ex.outputs,
                          scratch_shapes=scratch)(*ex.inputs)


def _gather_exchange(layer, shards):
    n = len(shards)
    outs = [jax.ShapeDtypeStruct((N_CHIPS,) + s.shape[1:], s.dtype) for s in shards]

    def start(w_refs, o_refs, ss, rs):
        x, y, c = _place()

        @pl.when(c == layer)
        def _():
            for i in range(n):
                for kk, (cx, cy) in enumerate(_other_chips(x, y)):
                    _remote(ss, rs, 6 * i + kk, w_refs[i].at[layer], o_refs[i].at[2 * x + y], (cx, cy, layer)).start()

    def mid(w_refs, o_refs, ss, rs):
        x, y, c = _place()

        @pl.when(c == layer)
        def _():
            for i in range(n):
                for kk, (cx, cy) in enumerate(_other_chips(x, y)):
                    got = o_refs[i].at[2 * cx + cy]
                    _remote(ss, rs, 6 * i + kk, got, got, (x, y, c)).wait_recv()
                    _remote(ss, rs, 6 * i + 3 + kk, got, got, (x, y, 1 - layer)).start()

    def end(w_refs, o_refs, ss, rs):
        x, y, c = _place()
        for i in range(n):
            for kk, (cx, cy) in enumerate(_other_chips(x, y)):
                got = o_refs[i].at[2 * cx + cy]

                @pl.when(c == layer)
                def _():
                    _remote(ss, rs, 6 * i + kk, w_refs[i].at[layer], got, (x, y, c)).wait_send()
                    _remote(ss, rs, 6 * i + 3 + kk, got, got, (x, y, c)).wait_send()

                @pl.when(c != layer)
                def _():
                    _remote(ss, rs, 6 * i + 3 + kk, got, got, (x, y, c)).wait_recv()

    return Exchange(shards, outs, 6 * n, start, end, mid)


def _handover_exchange(layer, grads):
    n = len(grads)
    outs = [jax.ShapeDtypeStruct(g.shape, g.dtype) for g in grads]

    def start(g_refs, o_refs, ss, rs):
        x, y, c = _place()

        @pl.when(c != layer)
        def _():
            for i in range(n):
                _remote(ss, rs, i, g_refs[i], o_refs[i], (x, y, layer)).start()

    def end(g_refs, o_refs, ss, rs):
        x, y, c = _place()
        for i in range(n):
            @pl.when(c != layer)
            def _():
                _remote(ss, rs, i, g_refs[i], o_refs[i], (x, y, c)).wait_send()

            @pl.when(c == layer)
            def _():
                _remote(ss, rs, i, g_refs[i], o_refs[i], (x, y, c)).wait_recv()

    return Exchange(grads, outs, n, start, end)


def _scatter_exchange(layer, parts):
    n = len(parts)
    outs = [jax.ShapeDtypeStruct(p.shape, p.dtype) for p in parts]

    def start(s_refs, o_refs, ss, rs):
        x, y, c = _place()

        @pl.when(c == layer)
        def _():
            for i in range(n):
                for kk, (cx, cy) in enumerate(_other_chips(x, y)):
                    _remote(ss, rs, 3 * i + kk, s_refs[i].at[2 * cx + cy], o_refs[i].at[2 * x + y],
                            (cx, cy, layer)).start()

    def end(s_refs, o_refs, ss, rs):
        x, y, c = _place()

        @pl.when(c == layer)
        def _():
            for i in range(n):
                for kk, (cx, cy) in enumerate(_other_chips(x, y)):
                    got = o_refs[i].at[2 * cx + cy]
                    _remote(ss, rs, 3 * i + kk, got, got, (x, y, c)).wait_recv()
            for i in range(n):
                for kk, (cx, cy) in enumerate(_other_chips(x, y)):
                    _remote(ss, rs, 3 * i + kk, s_refs[i].at[2 * cx + cy], o_refs[i].at[2 * x + y],
                            (x, y, c)).wait_send()

    return Exchange(parts, outs, 3 * n, start, end)


def _return_exchange(reduced):
    flat = [g for layer in range(DEPTH) for g in reduced[layer]]
    n = len(reduced[0])
    outs = [jax.ShapeDtypeStruct(g.shape, g.dtype) for g in flat]

    def start(g_refs, o_refs, ss, rs):
        x, y, c = _place()
        for layer in range(DEPTH):
            @pl.when(c == layer)
            def _():
                for i in range(n):
                    k = layer * n + i
                    _remote(ss, rs, k, g_refs[k], o_refs[k], (x, y, 1 - layer)).start()

    def end(g_refs, o_refs, ss, rs):
        x, y, c = _place()
        for layer in range(DEPTH):
            for i in range(n):
                k = layer * n + i

                @pl.when(c == layer)
                def _():
                    _remote(ss, rs, k, g_refs[k], o_refs[k], (x, y, c)).wait_send()

                @pl.when(c != layer)
                def _():
                    _remote(ss, rs, k, g_refs[k], o_refs[k], (x, y, c)).wait_recv()

    return Exchange(flat, outs, DEPTH * n, start, end)


def _gather_small(tag, buf):
    shape = buf.shape

    def body(b_ref, o_ref, sum_ref, send_sems, recv_sems, local_sem):
        x, y, c = _place()
        me = 4 * x + 2 * y + c
        mine = pltpu.make_async_copy(b_ref, o_ref.at[me], local_sem)
        mine.start()
        flips = [(dx, dy, dc) for dx in (0, 1) for dy in (0, 1) for dc in (0, 1) if (dx, dy, dc) != (0, 0, 0)]
        sends = []

        def peer(dx, dy, dc):
            return (1 - x if dx else x, 1 - y if dy else y, 1 - c if dc else c)

        for kk, flip in enumerate(flips):
            cp = pltpu.make_async_remote_copy(src_ref=b_ref, dst_ref=o_ref.at[me], send_sem=send_sems.at[kk],
                                              recv_sem=recv_sems.at[kk], device_id=peer(*flip), device_id_type=MESH)
            cp.start()
            sends.append(cp)
        for kk, flip in enumerate(flips):
            px, py, pc = peer(*flip)
            frm = 4 * px + 2 * py + pc
            pltpu.make_async_remote_copy(src_ref=b_ref, dst_ref=o_ref.at[frm], send_sem=send_sems.at[kk],
                                         recv_sem=recv_sems.at[kk], device_id=(x, y, c),
                                         device_id_type=MESH).wait_recv()
        for cp in sends:
            cp.wait_send()
        mine.wait()
        total = o_ref[0]
        for d in range(1, N_DEV):
            total = total + o_ref[d]
        sum_ref[...] = total

    return pl.pallas_call(
        body, name="gather_small" + tag, in_specs=[VMEM_SPEC], out_specs=[VMEM_SPEC, VMEM_SPEC],
        out_shape=[jax.ShapeDtypeStruct((N_DEV,) + shape, buf.dtype), jax.ShapeDtypeStruct(shape, buf.dtype)],
        scratch_shapes=[pltpu.SemaphoreType.DMA((N_DEV - 1,)), pltpu.SemaphoreType.DMA((N_DEV - 1,)),
                        pltpu.SemaphoreType.DMA],
    )(buf)


def _add_handed(tag, layer, g, r, tr=256):
    _, rows, cols = g.shape
    tr, tc = _tile_of(rows, cols, tr)
    per_row = cols // tc

    def body(g_ref, r_ref, o_ref):
        @pl.when(lax.axis_index("c") == layer)
        def _():
            o_ref[...] = (g_ref[...].astype(F32) + r_ref[...].astype(F32)).astype(o_ref.dtype)

    spec = pl.BlockSpec((1, tr, tc), lambda p, i: (p, i // per_row, i % per_row))
    return pl.pallas_call(
        body, name="add_handed_" + tag, grid=(N_CHIPS, (rows // tr) * per_row), in_specs=[spec, spec], out_specs=spec,
        out_shape=jax.ShapeDtypeStruct(g.shape, BF16), compiler_params=_params(("arbitrary", "arbitrary")),
    )(g, r)


def _sum_parts(tag, layer, own, parts, tr=256):
    _, rows, cols = parts.shape
    tr, tc = _tile_of(rows, cols, tr)
    per_row = cols // tc
    chip = (2 * lax.axis_index("x") + lax.axis_index("y")).astype(jnp.int32).reshape(1)

    def body(c_ref, own_ref, p1_ref, p2_ref, p3_ref, o_ref):
        @pl.when(lax.axis_index("c") == layer)
        def _():
            total = own_ref[0].astype(F32)
            for p_ref in (p1_ref, p2_ref, p3_ref):
                total = total + p_ref[0].astype(F32)
            o_ref[...] = total

    def after(kk):
        return pl.BlockSpec((1, tr, tc), lambda i, c_ref: ((c_ref[0] + kk) % N_CHIPS, i // per_row, i % per_row))

    return pl.pallas_call(
        body, name="sum_parts_" + tag,
        grid_spec=pltpu.PrefetchScalarGridSpec(
            num_scalar_prefetch=1, grid=((rows // tr) * per_row,), in_specs=[after(0), after(1), after(2), after(3)],
            out_specs=pl.BlockSpec((tr, tc), lambda i, c_ref: (i // per_row, i % per_row))),
        out_shape=jax.ShapeDtypeStruct((rows, cols), F32),
        compiler_params=_params(("arbitrary",)),
    )(chip, own, parts, parts, parts)


def _adamw_math(w, g, m, v):
    m = ADAM_B1 * m + (1.0 - ADAM_B1) * g
    v = ADAM_B2 * v + (1.0 - ADAM_B2) * jnp.square(g)
    m_hat = m / (1.0 - ADAM_B1 ** ADAM_STEP)
    v_hat = v / (1.0 - ADAM_B2 ** ADAM_STEP)
    delta = -ADAM_LR * (m_hat / (jnp.sqrt(v_hat) + ADAM_EPS) + ADAM_WD * w)
    return delta, m, v


def _tile_of(rows, cols, tr):
    return (tr, cols) if rows % tr == 0 else (rows, 256)


def _adamw(tag, w, computed, received, m, v, tr=256):
    _, rows, cols = w.shape
    tr, tc = _tile_of(rows, cols, tr)
    per_row = cols // tc

    def body(*refs):
        w_ref, m_ref, v_ref = refs[:3]
        g_refs = refs[3:3 + 2 * DEPTH]
        g_ref, d_ref, nm_ref, nv_ref = refs[3 + 2 * DEPTH:]
        layer = pl.program_id(0)
        core = lax.axis_index("c")
        g = jnp.zeros((tr, tc), F32)
        for l in range(DEPTH):
            mine = jnp.where(core == l, g_refs[2 * l][...], g_refs[2 * l + 1][...])
            g = jnp.where(layer == l, mine, g)
        g_ref[0] = g
        d_ref[0], nm_ref[0], nv_ref[0] = _adamw_math(w_ref[0], g, m_ref[0], v_ref[0])

    stacked = pl.BlockSpec((1, tr, tc), lambda l, i: (l, i // per_row, i % per_row))

    def of_layer(k):
        return pl.BlockSpec((tr, tc), lambda l, i: (jnp.where(l == k, i // per_row, 0),
                                                    jnp.where(l == k, i % per_row, 0)))

    g_specs = [of_layer(l) for l in range(DEPTH) for _ in range(2)]
    g_args = [a for l in range(DEPTH) for a in (computed[l], received[l])]
    return pl.pallas_call(
        body, name="adamw_" + tag, grid=(DEPTH, (rows // tr) * per_row), in_specs=[stacked] * 3 + g_specs,
        out_specs=[stacked] * 4, out_shape=[jax.ShapeDtypeStruct(w.shape, F32)] * 4,
        compiler_params=_params(("arbitrary", "arbitrary")),
    )(w, m, v, *g_args)


def _w_in_to_padded(wt):
    d0 = SSD_DIM + CONV_DIM
    rows = jnp.concatenate([wt[:d0], wt[d0 + SSD_HEADS:], wt[d0:d0 + SSD_HEADS],
                            jnp.zeros((DT_PAD - SSD_HEADS, wt.shape[1]), wt.dtype)], axis=0)
    return rows.T


def _w_in_from_padded(gt):
    d0 = SSD_DIM + CONV_DIM
    return jnp.concatenate([gt[:d0], gt[DT_OFF:DT_OFF + SSD_HEADS], gt[d0:DT_OFF]], axis=0)


def _small_layout():
    return (("norm_mix_g", 0, 0, D_MODEL), ("norm_xa_g", 1, 0, D_MODEL), ("norm_mem_g", 2, 0, D_MODEL),
            ("norm_ff_g", 3, 0, D_MODEL), ("conv_b", 4, 0, CONV_DIM), ("ssd_norm_g", 5, 0, SSD_DIM),
            ("sb_norm_g", 5, SSD_DIM, SB_DIM), ("dt_bias", 6, 0, SSD_HEADS), ("a_log", 6, LANES, SSD_HEADS),
            ("d_skip", 6, 2 * LANES, SSD_HEADS))


def _pack_small(gs, loss):
    lay = _small_layout()
    args = [gs[name][l] for l in range(DEPTH) for name, _, _, _ in lay]
    args += [gs["conv_w"][l] for l in range(DEPTH)] + [gs["final_g"], loss]
    n_lay = len(lay)

    def body(*refs):
        o_ref = refs[-1]
        o_ref[...] = jnp.zeros_like(o_ref)
        for l in range(DEPTH):
            for i, (_, rr, c0, width) in enumerate(lay):
                row = l * _SM_PER_LAYER + rr
                o_ref[row:row + 1, c0:c0 + width] = refs[l * n_lay + i][...]
            row = _SM_CONVW + l * CONV_K
            o_ref[row:row + CONV_K, 0:CONV_DIM] = refs[DEPTH * n_lay + l][...]
        o_ref[_SM_FINAL:_SM_FINAL + 1, :] = refs[DEPTH * n_lay + DEPTH][...]
        o_ref[_SM_LOSS:_SM_LOSS + 1, 0:LANES] = refs[DEPTH * n_lay + DEPTH + 1][...]

    return pl.pallas_call(
        body, name="pack_small", in_specs=[VMEM_SPEC] * len(args), out_specs=VMEM_SPEC,
        out_shape=jax.ShapeDtypeStruct((SMALL_ROWS, PACK_COLS), F32),
    )(*args)


def _small_update(buf, w, mom, var):
    lay = _small_layout()
    names = [name for name, _, _, _ in lay] + ["final_g", "conv_w"]
    conv_cols = CONV_DIM // N_CHIPS
    shapes2d = {name: (DEPTH, width) for name, _, _, width in lay}
    shapes2d["final_g"] = (1, D_MODEL)
    shapes2d["conv_w"] = (DEPTH * CONV_K, conv_cols)
    args = [buf]
    for src in (w, mom, var):
        args += [src[name].reshape(shapes2d[name]) for name in names]
    n = len(names)

    def body(*refs):
        b_ref = refs[0]
        w_refs, m_refs, v_refs = refs[1:1 + n], refs[1 + n:1 + 2 * n], refs[1 + 2 * n:1 + 3 * n]
        outs = refs[1 + 3 * n:]
        chip = 2 * lax.axis_index("x") + lax.axis_index("y")
        for i, name in enumerate(names):
            if name == "final_g":
                g = b_ref[_SM_FINAL:_SM_FINAL + 1, :]
            elif name == "conv_w":
                rows = b_ref[_SM_CONVW:_SM_CONVW + DEPTH * CONV_K, 0:CONV_DIM]
                g = jnp.zeros((DEPTH * CONV_K, conv_cols), F32)
                for j in range(N_CHIPS):
                    g = g + jnp.where(chip == j, rows[:, j * conv_cols:(j + 1) * conv_cols], 0.0)
            else:
                _, rr, c0, width = lay[i]
                g = jnp.concatenate([b_ref[l * _SM_PER_LAYER + rr:l * _SM_PER_LAYER + rr + 1, c0:c0 + width]
                                     for l in range(DEPTH)], axis=0)
            d, m2, v2 = _adamw_math(w_refs[i][...], g, m_refs[i][...], v_refs[i][...])
            outs[i][...] = g
            outs[n + i][...] = d
            outs[2 * n + i][...] = m2
            outs[3 * n + i][...] = v2

    out_shape = [jax.ShapeDtypeStruct(shapes2d[name], F32) for _ in range(4) for name in names]
    res = pl.pallas_call(
        body, name="small_update", in_specs=[VMEM_SPEC] * len(args), out_specs=[VMEM_SPEC] * (4 * n),
        out_shape=out_shape,
    )(*args)
    return tuple({name: res[k * n + i].reshape(w[name].shape) for i, name in enumerate(names)} for k in range(4))


SMALL_NAMES = ("norm_mix_g", "conv_b", "dt_bias", "a_log", "d_skip", "ssd_norm_g", "sb_norm_g", "norm_xa_g",
               "norm_mem_g", "norm_ff_g", "final_g")
WEIGHT_ORDER = ("norm_mix_g", "w_in", "conv_w", "conv_b", "dt_bias", "a_log", "d_skip", "ssd_norm_g", "sb_norm_g",
                "w_out", "norm_xa_g", "norm_mem_g", "w_xq", "w_xk", "w_xv", "w_xo", "norm_ff_g", "w_ff1", "w_ff2",
                "final_g")


_ALL = tuple(range(len(MATS)))
_IN = tuple(i for i in _ALL if MATS[i][0] == "w_in")
_MLP = tuple(i for i in _ALL if MATS[i][0] in ("w_ff1", "w_ff2"))
_MIXER = tuple(i for i in _ALL if i not in _IN + _MLP)
_CONV = len(MATS)


class PipelinedPlan(LocalPlan):
    def __init__(self, shards, conv_w):
        self.shards = list(shards) + [conv_w]
        self.chip = 2 * lax.axis_index("x") + lax.axis_index("y")
        self.mats = [dict() for _ in range(DEPTH)]
        n = len(MATS)
        self.parts = [[None] * n for _ in range(DEPTH)]
        self.to_chips = [[None] * n for _ in range(DEPTH)]
        self.reduced = [[None] * n for _ in range(DEPTH)]
        self.riders = {}
        first = _IN + (_CONV,)
        self._gathered(0, first, _run_exchange("gather_first", self._gather(0, first)))
        self._gather_behind(0, _MIXER, "proj_fwd", 0)
        self._gather_behind(0, _MLP[:1], "ssd_fwd", 0)
        self._gather_behind(0, _MLP[1:], "sb_fwd", 0)
        for l in range(1, DEPTH):
            self._gather_behind(l, first, "sb_fwd", l - 1)
            self._gather_behind(l, _MIXER, "ssd_fwd", l)
            self._gather_behind(l, _MLP, "sb_fwd", l)

    def _ride(self, kernel, l, exchange, then):
        self.riders.setdefault((kernel, l), []).append((exchange, then))

    def carried_by(self, kernel, l):
        exchange = None
        for ex, _ in self.riders.get((kernel, l), []):
            exchange = _both(exchange, ex)
        return exchange

    def carried_out(self, kernel, l, outs):
        for ex, then in self.riders.pop((kernel, l), []):
            then(outs[:len(ex.outputs)])
            outs = outs[len(ex.outputs):]

    def _gather(self, l, which):
        return _gather_exchange(l, [self.shards[i] for i in which])

    def _gather_behind(self, l, which, kernel, host):
        self._ride(kernel, host, self._gather(l, which), lambda outs: self._gathered(l, which, outs))

    def _gathered(self, l, which, outs):
        for i, theirs in zip(which, outs):
            name, _, axis = MATS[i] if i != _CONV else ("conv_w", None, 1)
            full = jnp.concatenate([jnp.where(self.chip == j, self.shards[i][l], theirs[j]) for j in range(N_CHIPS)],
                                   axis=axis)
            self.mats[l][name] = _w_in_to_padded(full) if name == "w_in" else full

    def _set_parts(self, l, which, gm):
        for i in which:
            name, _, axis = MATS[i]
            g = _w_in_from_padded(gm[name]) if name == "w_in" else gm[name]
            if g.ndim == 2 and axis == 0:
                g = g.reshape((N_CHIPS, g.shape[0] // N_CHIPS, g.shape[1]))
            elif g.ndim == 2:
                g = jnp.swapaxes(g.reshape((g.shape[0], N_CHIPS, g.shape[1] // N_CHIPS)), 0, 1)
            self.parts[l][i] = g

    def _handover(self, l, which):
        return _handover_exchange(l, [self.parts[l][i] for i in which])

    def _handed(self, l, which, outs):
        for i, r in zip(which, outs):
            self.to_chips[l][i] = _add_handed(MATS[i][0] + str(l), l, self.parts[l][i], r)

    def _scatter(self, l, which):
        return _scatter_exchange(l, [self.to_chips[l][i] for i in which])

    def _scattered(self, l, which, outs):
        for i, got in zip(which, outs):
            self.reduced[l][i] = _sum_parts(MATS[i][0] + str(l), l, self.to_chips[l][i], got)

    def _send_behind(self, l, which, hand_kernel, cross_kernel, host):
        def handed(outs):
            self._handed(l, which, outs)
            self._ride(cross_kernel, host, self._scatter(l, which), lambda o: self._scattered(l, which, o))

        self._ride(hand_kernel, host, self._handover(l, which), handed)

    def mlp_grads_done(self, l, gm):
        self._set_parts(l, _MLP, gm)
        self._send_behind(l, _MLP, "xattn_bwd", "sb_bwd", l)

    def mixer_grads_done(self, l, gm):
        self._set_parts(l, _MIXER, gm)
        self._send_behind(l, _MIXER, "sb_bwd", "ssd_bwd", l)

    def grads_done(self, l, gm):
        self._set_parts(l, _IN, gm)
        if l > 0:
            self._send_behind(l, _IN, "mlp_bwd", "sb_bwd", l - 1)
        else:
            self._handed(0, _IN, _run_exchange("handover_last", self._handover(0, _IN)))
            self._scattered(0, _IN, _run_exchange("scatter_last", self._scatter(0, _IN)))

    def reduced_gradients(self):
        returned = _run_exchange("return_reduced", _return_exchange(self.reduced))
        n = len(MATS)
        return [([self.reduced[l][i] for l in range(DEPTH)], [returned[l * n + i] for l in range(DEPTH)])
                for i in range(n)]


def kernel(x, mem, norm_mix_g, w_in, conv_w, conv_b, dt_bias, a_log, d_skip, ssd_norm_g, sb_norm_g, w_out, norm_xa_g, norm_mem_g, w_xq, w_xk, w_xv, w_xo, norm_ff_g, w_ff1, w_ff2, final_g, loss_target, m_norm_mix_g, m_w_in, m_conv_w, m_conv_b, m_dt_bias, m_a_log, m_d_skip, m_ssd_norm_g, m_sb_norm_g, m_w_out, m_norm_xa_g, m_norm_mem_g, m_w_xq, m_w_xk, m_w_xv, m_w_xo, m_norm_ff_g, m_w_ff1, m_w_ff2, m_final_g, v_norm_mix_g, v_w_in, v_conv_w, v_conv_b, v_dt_bias, v_a_log, v_d_skip, v_ssd_norm_g, v_sb_norm_g, v_w_out, v_norm_xa_g, v_norm_mem_g, v_w_xq, v_w_xk, v_w_xv, v_w_xo, v_norm_ff_g, v_w_ff1, v_w_ff2, v_final_g):
    w = dict(norm_mix_g=norm_mix_g, w_in=w_in, conv_w=conv_w, conv_b=conv_b, dt_bias=dt_bias, a_log=a_log,
             d_skip=d_skip, ssd_norm_g=ssd_norm_g, sb_norm_g=sb_norm_g, w_out=w_out, norm_xa_g=norm_xa_g,
             norm_mem_g=norm_mem_g, w_xq=w_xq, w_xk=w_xk, w_xv=w_xv, w_xo=w_xo, norm_ff_g=norm_ff_g, w_ff1=w_ff1,
             w_ff2=w_ff2, final_g=final_g)
    mom = dict(norm_mix_g=m_norm_mix_g, w_in=m_w_in, conv_w=m_conv_w, conv_b=m_conv_b, dt_bias=m_dt_bias,
               a_log=m_a_log, d_skip=m_d_skip, ssd_norm_g=m_ssd_norm_g, sb_norm_g=m_sb_norm_g, w_out=m_w_out,
               norm_xa_g=m_norm_xa_g, norm_mem_g=m_norm_mem_g, w_xq=m_w_xq, w_xk=m_w_xk, w_xv=m_w_xv, w_xo=m_w_xo,
               norm_ff_g=m_norm_ff_g, w_ff1=m_w_ff1, w_ff2=m_w_ff2, final_g=m_final_g)
    var = dict(norm_mix_g=v_norm_mix_g, w_in=v_w_in, conv_w=v_conv_w, conv_b=v_conv_b, dt_bias=v_dt_bias,
               a_log=v_a_log, d_skip=v_d_skip, ssd_norm_g=v_ssd_norm_g, sb_norm_g=v_sb_norm_g, w_out=v_w_out,
               norm_xa_g=v_norm_xa_g, norm_mem_g=v_norm_mem_g, w_xq=v_w_xq, w_xk=v_w_xk, w_xv=v_w_xv, w_xo=v_w_xo,
               norm_ff_g=v_norm_ff_g, w_ff1=v_w_ff1, w_ff2=v_w_ff2, final_g=v_final_g)
    for params in (w, mom, var):
        params["w_in"] = jnp.swapaxes(params["w_in"], 1, 2)

    sw = {name: w[name] for name in SMALL_NAMES}
    plan = PipelinedPlan([w[name].astype(BF16) for name, _, _ in MATS], conv_w)
    loss, grad_x, gm, gs = _local_step(x[0], mem[0], loss_target[0], sw, plan)
    g_mats = plan.reduced_gradients()

    _, small_sum = _gather_small("_grads", _pack_small(gs, loss))
    loss_out = small_sum[_SM_LOSS, 0]

    grads, deltas, new_m, new_v = {}, {}, {}, {}
    for (name, _, _), (computed, received) in zip(MATS, g_mats):
        grads[name], deltas[name], new_m[name], new_v[name] = _adamw(
            name, w[name], computed, received, mom[name], var[name])
    g_s, d_s, m_s, v_s = _small_update(small_sum, w, mom, var)
    for name in g_s:
        grads[name], deltas[name], new_m[name], new_v[name] = g_s[name], d_s[name], m_s[name], v_s[name]
    for out in (grads, deltas, new_m, new_v):
        out["w_in"] = jnp.swapaxes(out["w_in"], 1, 2)

    return (loss_out, grad_x[None], *[grads[n] for n in WEIGHT_ORDER], *[deltas[n] for n in WEIGHT_ORDER],
            *[new_m[n] for n in WEIGHT_ORDER], *[new_v[n] for n in WEIGHT_ORDER])
```

```python
import functools
import math

import jax
import jax.numpy as jnp
from jax import lax
from jax.experimental import pallas as pl
from jax.experimental.pallas import tpu as pltpu

F32 = jnp.float32
BF16 = jnp.bfloat16
MESH = pl.DeviceIdType.MESH

D_MODEL = 1024
DEPTH = 2
SSD_DIM = 512
SSD_HEAD_DIM = 64
SSD_HEADS = 8
SSD_GROUPS = 2
SSD_STATE = 64
CONV_K = 4
CHUNK = 128
SB_DIM = 512
SB_HEAD_DIM = 64
XA_HEADS = 4
XA_HEAD_DIM = 128
XA_DIM = 512
D_FF = 4096
EPS = 1e-5
GN = SSD_GROUPS * SSD_STATE
CONV_DIM = SSD_DIM + 2 * GN
IN_DIM = SSD_DIM + CONV_DIM + SSD_HEADS + 3 * SB_DIM
LANES = 128
DT_PAD = LANES
IN_PAD = SSD_DIM + CONV_DIM + 3 * SB_DIM + DT_PAD
Q_OFF = SSD_DIM + CONV_DIM
DT_OFF = Q_OFF + 3 * SB_DIM
HALO = 8

ADAM_LR = 0.001
ADAM_B1 = 0.9
ADAM_B2 = 0.999
ADAM_EPS = 1e-08
ADAM_WD = 0.01
ADAM_STEP = 10

N_CHIPS = 4
N_DEV = 8
PACK_COLS = 1024
VMEM_LIMIT = 56 * 1024 * 1024

MATS = (
    ("w_in", (IN_DIM, D_MODEL), 0),
    ("w_out", (D_MODEL, D_MODEL), 0),
    ("w_xq", (D_MODEL, XA_DIM), 0),
    ("w_xk", (D_MODEL, XA_DIM), 0),
    ("w_xv", (D_MODEL, XA_DIM), 0),
    ("w_xo", (XA_DIM, D_MODEL), 1),
    ("w_ff1", (D_MODEL, D_FF), 1),
    ("w_ff2", (D_FF, D_MODEL), 0),
)


SMALL_ROWS = 24
_SM_PER_LAYER = 7
_SM_FINAL = 14
_SM_CONVW = 15
_SM_LOSS = 23


_NN = ((1,), (0,))
_NT = ((1,), (1,))
_TN = ((0,), (0,))


def _dg(a, b, dims):
    return lax.dot_general(a.astype(BF16), b.astype(BF16), (dims, ((), ())), preferred_element_type=F32)


@jax.custom_vjp
def mm_nn(a, b):
    return _dg(a, b, _NN)


@jax.custom_vjp
def mm_nt(a, b):
    return _dg(a, b, _NT)


@jax.custom_vjp
def mm_tn(a, b):
    return _dg(a, b, _TN)


def _nn_fwd(a, b):
    return _dg(a, b, _NN), (a, b)


def _nn_bwd(res, g):
    a, b = res
    return mm_nt(g, b).astype(a.dtype), mm_tn(a, g).astype(b.dtype)


def _nt_fwd(a, b):
    return _dg(a, b, _NT), (a, b)


def _nt_bwd(res, g):
    a, b = res
    return mm_nn(g, b).astype(a.dtype), mm_tn(g, a).astype(b.dtype)


def _tn_fwd(a, b):
    return _dg(a, b, _TN), (a, b)


def _tn_bwd(res, g):
    a, b = res
    return mm_nt(b, g).astype(a.dtype), mm_nn(a, g).astype(b.dtype)


mm_nn.defvjp(_nn_fwd, _nn_bwd)
mm_nt.defvjp(_nt_fwd, _nt_bwd)
mm_tn.defvjp(_tn_fwd, _tn_bwd)


def _rms(x, g):
    return x * lax.rsqrt(jnp.mean(x * x, axis=-1, keepdims=True) + EPS) * g


def _params(sem=None, vmem=VMEM_LIMIT):
    return pltpu.CompilerParams(dimension_semantics=sem, vmem_limit_bytes=vmem)


class Exchange:
    def __init__(self, inputs, outputs, n_sems, start, end, mid=None):
        self.inputs, self.outputs, self.n_sems = list(inputs), list(outputs), n_sems
        self.start, self.mid, self.end = start, mid, end

    def specs(self):
        hbm = pl.BlockSpec(memory_space=pl.ANY)
        sems = [pltpu.SemaphoreType.DMA((self.n_sems,)), pltpu.SemaphoreType.DMA((self.n_sems,))]
        return [hbm] * len(self.inputs), [hbm] * len(self.outputs), sems

    def pick(self, refs, n_before_in, n_before_out):
        n_in, n_out = len(self.inputs), len(self.outputs)
        o0 = n_before_in + n_in + n_before_out
        return refs[n_before_in:n_before_in + n_in], refs[o0:o0 + n_out], refs[-2], refs[-1]

    def before_work(self, mine, first, mid=None):
        @pl.when(first)
        def _():
            self.start(*mine)

        if self.mid is not None and mid is not None:
            @pl.when(mid)
            def _():
                self.mid(*mine)

    def after_work(self, mine, last, mid_done):
        @pl.when(last)
        def _():
            if self.mid is not None and not mid_done:
                self.mid(*mine)
            self.end(*mine)


class _Shifted:
    def __init__(self, ref, base):
        self.ref, self.base = ref, base

    @property
    def at(self):
        return self

    def __getitem__(self, idx):
        return self.ref.at[self.base + idx]


def _both(a, b):
    if a is None or b is None:
        return a or b
    n_i, n_o, n_s = len(a.inputs), len(a.outputs), a.n_sems

    def joined(fa, fb):
        def f(i_refs, o_refs, ss, rs):
            if fa is not None:
                fa(i_refs[:n_i], o_refs[:n_o], ss, rs)
            if fb is not None:
                fb(i_refs[n_i:], o_refs[n_o:], _Shifted(ss, n_s), _Shifted(rs, n_s))
        return f

    mid = joined(a.mid, b.mid) if (a.mid is not None or b.mid is not None) else None
    return Exchange(a.inputs + b.inputs, a.outputs + b.outputs, n_s + b.n_sems, joined(a.start, b.start),
                    joined(a.end, b.end), mid)


def _rowcall(name, fn, rows, fulls, row_out, acc_out=(), tm=256, exchange=None):
    s = rows[0].shape[0]
    tm = min(tm, s)
    nt = s // tm
    n_r, n_f, n_ro, n_ao = len(rows), len(fulls), len(row_out), len(acc_out)
    n_xi = len(exchange.inputs) if exchange else 0

    def body(*refs):
        if exchange:
            mine = exchange.pick(refs, n_r + n_f, n_ro + n_ao)
            exchange.before_work(mine, pl.program_id(0) == 0, pl.program_id(0) == (3 * nt) // 4)
        ins = [r[...] for r in refs[: n_r + n_f]]
        outs = fn(*ins)
        o_refs = refs[n_r + n_f + n_xi:]
        for o_ref, val in zip(o_refs[:n_ro], outs[:n_ro]):
            o_ref[...] = val.astype(o_ref.dtype)
        if n_ao:
            first = pl.program_id(0) == 0

            @pl.when(first)
            def _():
                for o_ref, val in zip(o_refs[n_ro:], outs[n_ro:]):
                    o_ref[...] = val.astype(o_ref.dtype)

            @pl.when(jnp.logical_not(first))
            def _():
                for o_ref, val in zip(o_refs[n_ro:], outs[n_ro:]):
                    o_ref[...] += val.astype(o_ref.dtype)
        if exchange:
            exchange.after_work(mine, pl.program_id(0) == nt - 1, mid_done=True)

    in_specs = [pl.BlockSpec((tm, a.shape[1]), lambda i: (i, 0)) for a in rows]
    in_specs += [pl.BlockSpec(a.shape, lambda i: (0, 0), pipeline_mode=pl.Buffered(1)) for a in fulls]
    out_specs = [pl.BlockSpec((tm, c), lambda i: (i, 0)) for c, _ in row_out]
    out_specs += [pl.BlockSpec(shape, lambda i: (0, 0)) for shape, _ in acc_out]
    out_shape = [jax.ShapeDtypeStruct((s, c), dt) for c, dt in row_out]
    out_shape += [jax.ShapeDtypeStruct(shape, dt) for shape, dt in acc_out]
    args, scratch = [*rows, *fulls], []
    if exchange:
        x_in, x_out, scratch = exchange.specs()
        in_specs += x_in
        out_specs += x_out
        out_shape += exchange.outputs
        args += exchange.inputs
    return pl.pallas_call(
        body, name=name, grid=(nt,), in_specs=in_specs, out_specs=out_specs, out_shape=out_shape,
        scratch_shapes=scratch, compiler_params=_params(("arbitrary",)),
    )(*args)


def _mm_tn_call(name, a, b, tm, tn, tk):
    s, m = a.shape
    n = b.shape[1]
    tk = min(tk, s)

    def body(a_ref, b_ref, o_ref):
        d = _dg(a_ref[...], b_ref[...], _TN)
        first = pl.program_id(2) == 0

        @pl.when(first)
        def _():
            o_ref[...] = d

        @pl.when(jnp.logical_not(first))
        def _():
            o_ref[...] += d

    return pl.pallas_call(
        body, name=name, grid=(m // tm, n // tn, s // tk),
        in_specs=[pl.BlockSpec((tk, tm), lambda i, j, k: (k, i)), pl.BlockSpec((tk, tn), lambda i, j, k: (k, j))],
        out_specs=pl.BlockSpec((tm, tn), lambda i, j, k: (i, j)),
        out_shape=jax.ShapeDtypeStruct((m, n), F32),
        compiler_params=_params(("parallel", "parallel", "arbitrary")),
    )(a, b)


def _mm_tn_parts(name, a, b, by_cols, tm=1024, tk=2048):
    s, m = a.shape
    n = b.shape[1]
    r, c = (m, n // N_CHIPS) if by_cols else (m // N_CHIPS, n)
    per = r // tm
    tk = min(tk, s)
    nk = s // tk

    def body(a_ref, b_ref, o_ref, acc):
        d = _dg(a_ref[...], b_ref[...], _TN)
        step = pl.program_id(2)

        @pl.when(step == 0)
        def _():
            acc[...] = d

        @pl.when(step > 0)
        def _():
            acc[...] += d

        @pl.when(step == nk - 1)
        def _():
            o_ref[0] = acc[...].astype(o_ref.dtype)

    if by_cols:
        out_map = lambda i, j, k: (j, i, 0)
    else:
        out_map = lambda i, j, k: (i // per, i % per, 0)
    return pl.pallas_call(
        body, name=name, grid=(m // tm, n // c, s // tk),
        in_specs=[pl.BlockSpec((tk, tm), lambda i, j, k: (k, i)), pl.BlockSpec((tk, c), lambda i, j, k: (k, j))],
        out_specs=pl.BlockSpec((1, tm, c), out_map),
        out_shape=jax.ShapeDtypeStruct((N_CHIPS, r, c), BF16), scratch_shapes=[pltpu.VMEM((tm, c), F32)],
        compiler_params=_params(("parallel", "parallel", "arbitrary")),
    )(a, b)


def _proj_tile(h, g, w):
    p = mm_nn(_rms(h, g), w)
    return (p[:, :SSD_DIM], p[:, SSD_DIM:Q_OFF], p[:, Q_OFF:Q_OFF + SB_DIM],
            p[:, Q_OFF + SB_DIM:Q_OFF + 2 * SB_DIM], p[:, Q_OFF + 2 * SB_DIM:DT_OFF], p[:, DT_OFF:])


def _proj_fwd(tag, h, g, w, exchange=None):
    return _rowcall(
        "proj_fwd" + tag, _proj_tile, [h], [g, w],
        [(SSD_DIM, F32), (CONV_DIM, F32), (SB_DIM, BF16), (SB_DIM, BF16), (SB_DIM, BF16), (DT_PAD, F32)], tm=512,
        exchange=exchange)


def _proj_bwd(tag, h, dh_out, dz, dxbc, dq, dk, dv, ddt, g, w):
    def fn(h, dh_out, dz, dxbc, dq, dk, dv, ddt, g, w):
        dp = jnp.concatenate([dz.astype(BF16), dxbc.astype(BF16), dq.astype(BF16), dk.astype(BF16),
                              dv.astype(BF16), ddt.astype(BF16)], axis=1)
        hn, vjp = jax.vjp(_rms, h, g)
        dh, dg = vjp(mm_nt(dp, w))
        return dh_out + dh, hn, dp, dg

    return _rowcall(
        "proj_bwd" + tag, fn, [h, dh_out, dz, dxbc, dq, dk, dv, ddt], [g, w],
        [(D_MODEL, F32), (D_MODEL, BF16), (IN_PAD, BF16)], [((1, D_MODEL), F32)], tm=256)


def _shift_down(x, tail, j):
    if j == 0:
        return x
    r = pltpu.roll(x, j, 0)
    rt = pltpu.roll(tail, j, 0)
    row = lax.broadcasted_iota(jnp.int32, (HALO, x.shape[1]), 0)
    first = jnp.where(row < j, rt, r[:HALO])
    if x.shape[0] == HALO:
        return first
    return jnp.concatenate([first, r[HALO:]], axis=0)


def _shift_up(x, head, j):
    if j == 0:
        return x
    n = x.shape[0]
    r = pltpu.roll(x, n - j, 0)
    rh = pltpu.roll(head, HALO - j, 0)
    row = lax.broadcasted_iota(jnp.int32, (HALO, x.shape[1]), 0)
    return jnp.concatenate([r[:n - HALO], jnp.where(row >= HALO - j, rh, r[n - HALO:])], axis=0)


def _conv_pre(x, tail, w, b):
    acc = b + w[CONV_K - 1:CONV_K] * x
    for j in range(1, CONV_K):
        acc = acc + w[CONV_K - 1 - j:CONV_K - j] * _shift_down(x, tail, j)
    return acc


def _dsilu(p):
    s = jax.nn.sigmoid(p)
    return s * (1.0 + p * (1.0 - s))


def _conv_fwd(tag, xbc, w, b, tc=512):
    s, c = xbc.shape
    tc = min(tc, s)
    per = tc // HALO

    def body(x_ref, prev_ref, w_ref, b_ref, o_ref):
        tail = jnp.where(pl.program_id(0) > 0, prev_ref[...], 0.0)
        o_ref[...] = jax.nn.silu(_conv_pre(x_ref[...], tail, w_ref[...], b_ref[...]))

    return pl.pallas_call(
        body, name="conv_fwd" + tag, grid=(s // tc,),
        in_specs=[pl.BlockSpec((tc, c), lambda i: (i, 0)),
                  pl.BlockSpec((HALO, c), lambda i: (jnp.maximum(i * per - 1, 0), 0)),
                  pl.BlockSpec((CONV_K, c), lambda i: (0, 0)), pl.BlockSpec((1, c), lambda i: (0, 0))],
        out_specs=pl.BlockSpec((tc, c), lambda i: (i, 0)),
        out_shape=jax.ShapeDtypeStruct((s, c), F32),
        compiler_params=_params(("arbitrary",)),
    )(xbc, xbc, w, b)


def _conv_bwd(tag, xbc, dact, w, b, tc=512):
    s, c = xbc.shape
    tc = min(tc, s)
    per = tc // HALO
    nt = s // tc
    last_blk = s // HALO - 1

    def body(x_ref, prev_ref, next_ref, d_ref, dnext_ref, w_ref, b_ref, dx_ref, dw_ref, db_ref):
        i = pl.program_id(0)
        x = x_ref[...]
        wv = w_ref[...]
        tail = jnp.where(i > 0, prev_ref[...], 0.0)
        dpre = d_ref[...] * _dsilu(_conv_pre(x, tail, wv, b_ref[...]))
        pre_n = _conv_pre(next_ref[...], x[tc - HALO:], wv, b_ref[...])
        dpre_n = jnp.where(i < nt - 1, dnext_ref[...] * _dsilu(pre_n), 0.0)
        dx = wv[CONV_K - 1:CONV_K] * dpre
        for j in range(1, CONV_K):
            dx = dx + wv[CONV_K - 1 - j:CONV_K - j] * _shift_up(dpre, dpre_n, j)
        dx_ref[...] = dx
        dws = [jnp.sum(dpre * _shift_down(x, tail, CONV_K - 1 - k), axis=0, keepdims=True) for k in range(CONV_K)]
        dwv = jnp.concatenate(dws, axis=0)
        dbv = jnp.sum(dpre, axis=0, keepdims=True)

        @pl.when(i == 0)
        def _():
            dw_ref[...] = dwv
            db_ref[...] = dbv

        @pl.when(i > 0)
        def _():
            dw_ref[...] += dwv
            db_ref[...] += dbv

    tile = pl.BlockSpec((tc, c), lambda i: (i, 0))
    prev = pl.BlockSpec((HALO, c), lambda i: (jnp.maximum(i * per - 1, 0), 0))
    nxt = pl.BlockSpec((HALO, c), lambda i: (jnp.minimum((i + 1) * per, last_blk), 0))
    return pl.pallas_call(
        body, name="conv_bwd" + tag, grid=(nt,),
        in_specs=[tile, prev, nxt, tile, nxt, pl.BlockSpec((CONV_K, c), lambda i: (0, 0)),
                  pl.BlockSpec((1, c), lambda i: (0, 0))],
        out_specs=[tile, pl.BlockSpec((CONV_K, c), lambda i: (0, 0)), pl.BlockSpec((1, c), lambda i: (0, 0))],
        out_shape=[jax.ShapeDtypeStruct((s, c), F32), jax.ShapeDtypeStruct((CONV_K, c), F32),
                   jax.ShapeDtypeStruct((1, c), F32)],
        compiler_params=_params(("arbitrary",)),
    )(xbc, xbc, xbc, dact, dact, w, b)


def _ssd_chunk(xs, bm, cm, dtr, z, dt_bias, a_log, d_skip, g, s_prev):
    n = CHUNK
    row = lax.broadcasted_iota(jnp.int32, (n, n), 0)
    col = lax.broadcasted_iota(jnp.int32, (n, n), 1)
    causal = row >= col
    dt = jax.nn.softplus(dtr + dt_bias)
    a_c = dt * (-jnp.exp(a_log))
    hi = lax.Precision.HIGHEST
    a_cum = jnp.dot(causal.astype(F32), a_c, precision=hi, preferred_element_type=F32)
    a_cum_t = lax.dot_general(a_c, (row <= col).astype(F32), (_TN, ((), ())), precision=hi,
                              preferred_element_type=F32)
    p, st = SSD_HEAD_DIM, SSD_STATE
    heads = range(SSD_HEADS)
    grp = [h // (SSD_HEADS // SSD_GROUPS) for h in heads]
    bgs = [bm[:, k * st:(k + 1) * st] for k in range(SSD_GROUPS)]
    cgs = [cm[:, k * st:(k + 1) * st] for k in range(SSD_GROUPS)]
    cb = [mm_nt(cgs[k], bgs[k]) for k in range(SSD_GROUPS)]
    acols = [a_cum[:, h:h + 1] for h in heads]
    a_lasts = [a_cum[n - 1:n, h:h + 1] for h in heads]
    xhs = [xs[:, h * p:(h + 1) * p] for h in heads]
    sps = [s_prev[h * p:(h + 1) * p, :] for h in heads]
    xdts = [xhs[h] * dt[:, h:h + 1] for h in heads]
    decays = [jnp.exp(jnp.where(causal, acols[h] - a_cum_t[h:h + 1, :], -jnp.inf)) for h in heads]
    y_offs = [mm_nt(cgs[grp[h]], sps[h]) for h in heads]
    y_diags = [mm_nn(cb[grp[h]] * decays[h], xdts[h]) for h in heads]
    states = [mm_tn(xdts[h] * jnp.exp(a_lasts[h] - acols[h]), bgs[grp[h]]) for h in heads]
    s_new = [sps[h] * jnp.exp(a_lasts[h]) + states[h] for h in heads]
    ys = [y_diags[h] + y_offs[h] * jnp.exp(acols[h]) + d_skip[:, h:h + 1] * xhs[h] for h in heads]
    y = jnp.concatenate(ys, axis=1) * jax.nn.silu(z)
    return _rms(y, g), jnp.concatenate(s_new, axis=0)


def _split_xbc(t):
    return t[:, :SSD_DIM], t[:, SSD_DIM:SSD_DIM + GN], t[:, SSD_DIM + GN:]


def _ssd_fwd(tag, xact, dtr, z, dt_bias, a_log, d_skip, g, exchange=None):
    s = xact.shape[0]
    nc = s // CHUNK
    srows = SSD_HEADS * SSD_HEAD_DIM
    n_xi = len(exchange.inputs) if exchange else 0

    def body(*refs):
        x_ref, dt_ref, z_ref, b_ref, al_ref, ds_ref, g_ref = refs[:7]
        y_ref, st_ref = refs[7 + n_xi:9 + n_xi]
        state = refs[-3] if exchange else refs[-1]
        if exchange:
            mine = exchange.pick(refs, 7, 2)
            exchange.before_work(mine, pl.program_id(0) == 0, pl.program_id(0) == (3 * nc) // 4)

        @pl.when(pl.program_id(0) == 0)
        def _():
            state[...] = jnp.zeros_like(state)

        sp = state[...]
        st_ref[0] = sp
        xs, bm, cm = _split_xbc(x_ref[...])
        y, sn = _ssd_chunk(xs, bm, cm, dt_ref[...][:, :SSD_HEADS], z_ref[...], b_ref[...], al_ref[...],
                           ds_ref[...], g_ref[...], sp)
        y_ref[...] = y
        state[...] = sn
        if exchange:
            exchange.after_work(mine, pl.program_id(0) == nc - 1, mid_done=True)

    small = pl.BlockSpec((1, SSD_HEADS), lambda i: (0, 0))
    in_specs = [pl.BlockSpec((CHUNK, CONV_DIM), lambda i: (i, 0)), pl.BlockSpec((CHUNK, DT_PAD), lambda i: (i, 0)),
                pl.BlockSpec((CHUNK, SSD_DIM), lambda i: (i, 0)), small, small, small,
                pl.BlockSpec((1, SSD_DIM), lambda i: (0, 0))]
    out_specs = [pl.BlockSpec((CHUNK, SSD_DIM), lambda i: (i, 0)),
                 pl.BlockSpec((1, srows, SSD_STATE), lambda i: (i, 0, 0))]
    out_shape = [jax.ShapeDtypeStruct((s, SSD_DIM), F32), jax.ShapeDtypeStruct((nc, srows, SSD_STATE), F32)]
    args, scratch = [xact, dtr, z, dt_bias, a_log, d_skip, g], [pltpu.VMEM((srows, SSD_STATE), F32)]
    if exchange:
        x_in, x_out, sems = exchange.specs()
        in_specs, out_specs, scratch = in_specs + x_in, out_specs + x_out, scratch + sems
        out_shape, args = out_shape + exchange.outputs, args + exchange.inputs
    return pl.pallas_call(
        body, name="ssd_fwd" + tag, grid=(nc,), in_specs=in_specs, out_specs=out_specs, out_shape=out_shape,
        scratch_shapes=scratch, compiler_params=_params(("arbitrary",)),
    )(*args)


def _ssd_bwd(tag, xact, dtr, z, states, dy, dt_bias, a_log, d_skip, g, exchange=None):
    s = xact.shape[0]
    nc = s // CHUNK
    srows = SSD_HEADS * SSD_HEAD_DIM
    n_xi = len(exchange.inputs) if exchange else 0

    def body(*refs):
        x_ref, dt_ref, z_ref, sp_ref, dy_ref, b_ref, al_ref, ds_ref, g_ref = refs[:9]
        dx_ref, ddt_ref, dz_ref, db_ref, dal_ref, dds_ref, dg_ref = refs[9 + n_xi:16 + n_xi]
        dstate = refs[-3] if exchange else refs[-1]
        first = pl.program_id(0) == 0
        if exchange:
            mine = exchange.pick(refs, 9, 7)
            exchange.before_work(mine, first, pl.program_id(0) == (3 * nc) // 4)

        @pl.when(first)
        def _():
            dstate[...] = jnp.zeros_like(dstate)

        xs, bm, cm = _split_xbc(x_ref[...])
        _, vjp = jax.vjp(_ssd_chunk, xs, bm, cm, dt_ref[...][:, :SSD_HEADS], z_ref[...], b_ref[...], al_ref[...],
                         ds_ref[...], g_ref[...], sp_ref[0])
        dxs, dbm, dcm, ddt, dz, db, dal, dds, dg, dsp = vjp((dy_ref[...], dstate[...]))
        dx_ref[...] = jnp.concatenate([dxs, dbm, dcm], axis=1)
        ddt_ref[...] = jnp.concatenate([ddt, jnp.zeros((CHUNK, DT_PAD - SSD_HEADS), F32)], axis=1)
        dz_ref[...] = dz
        dstate[...] = dsp

        @pl.when(first)
        def _():
            db_ref[...] = db
            dal_ref[...] = dal
            dds_ref[...] = dds
            dg_ref[...] = dg

        @pl.when(jnp.logical_not(first))
        def _():
            db_ref[...] += db
            dal_ref[...] += dal
            dds_ref[...] += dds
            dg_ref[...] += dg

        if exchange:
            exchange.after_work(mine, pl.program_id(0) == nc - 1, mid_done=True)

    def rev(c):
        return lambda i: (nc - 1 - i, 0)

    small = pl.BlockSpec((1, SSD_HEADS), lambda i: (0, 0))
    gspec = pl.BlockSpec((1, SSD_DIM), lambda i: (0, 0))
    in_specs = [pl.BlockSpec((CHUNK, CONV_DIM), rev(0)), pl.BlockSpec((CHUNK, DT_PAD), rev(0)),
                pl.BlockSpec((CHUNK, SSD_DIM), rev(0)),
                pl.BlockSpec((1, srows, SSD_STATE), lambda i: (nc - 1 - i, 0, 0)),
                pl.BlockSpec((CHUNK, SSD_DIM), rev(0)), small, small, small, gspec]
    out_specs = [pl.BlockSpec((CHUNK, CONV_DIM), rev(0)), pl.BlockSpec((CHUNK, DT_PAD), rev(0)),
                 pl.BlockSpec((CHUNK, SSD_DIM), rev(0)), small, small, small, gspec]
    out_shape = [jax.ShapeDtypeStruct((s, CONV_DIM), F32), jax.ShapeDtypeStruct((s, DT_PAD), F32),
                 jax.ShapeDtypeStruct((s, SSD_DIM), F32), jax.ShapeDtypeStruct((1, SSD_HEADS), F32),
                 jax.ShapeDtypeStruct((1, SSD_HEADS), F32), jax.ShapeDtypeStruct((1, SSD_HEADS), F32),
                 jax.ShapeDtypeStruct((1, SSD_DIM), F32)]
    args, scratch = [xact, dtr, z, states, dy, dt_bias, a_log, d_skip, g], [pltpu.VMEM((srows, SSD_STATE), F32)]
    if exchange:
        x_in, x_out, sems = exchange.specs()
        in_specs, out_specs, scratch = in_specs + x_in, out_specs + x_out, scratch + sems
        out_shape, args = out_shape + exchange.outputs, args + exchange.inputs
    return pl.pallas_call(
        body, name="ssd_bwd" + tag, grid=(nc,), in_specs=in_specs, out_specs=out_specs, out_shape=out_shape,
        scratch_shapes=scratch, compiler_params=_params(("arbitrary",)),
    )(*args)


TQ = 128
TK = 128
SB_SCALE = 1.0 / math.sqrt(SB_HEAD_DIM)


def _split2(x):
    hi = x.astype(BF16)
    return hi, (x - hi.astype(F32)).astype(BF16)


TK_NEAR = 384
SB_UNDERFLOW = -110.0


def _sb_logits(qhs, kb, t0, s0, masked):
    zs = [_dg(qh, kb, _NT) for qh in qhs]
    mask = None
    if masked:
        t_pos = t0 + lax.broadcasted_iota(jnp.int32, zs[0].shape, 0)
        s_pos = s0 + lax.broadcasted_iota(jnp.int32, zs[0].shape, 1)
        mask = s_pos < t_pos
    lbs = [jnp.minimum(z, 0.0) - jnp.log(1.0 + jnp.exp(-jnp.abs(z))) for z in zs]
    lss = [lb - z for lb, z in zip(lbs, zs)]
    if masked:
        lss = [jnp.where(mask, ls, 0.0) for ls in lss]
    return lbs, lss, mask


def _running_sums(xs, starts, u, reverse, two_terms=True):
    nsub = xs[0].shape[1] // TK
    order = list(reversed(range(nsub))) if reverse else list(range(nsub))
    chunks = [[x[:, c * TK:(c + 1) * TK] for c in range(nsub)] for x in xs]
    sums = [[(_lane_sums(xc, u) if two_terms else _dg(xc, u, _NN)) for xc in row] for row in chunks]
    out = []
    for row, srow, run in zip(chunks, sums, starts):
        parts = [None] * nsub
        for c in order:
            parts[c] = run + srow[c]
            run = run + jnp.sum(row[c], axis=1, keepdims=True)
        out.append((parts[0] if nsub == 1 else jnp.concatenate(parts, axis=1), run))
    return out


def _lane_sums(x, u):
    hi, lo = _split2(x)
    return _dg(hi, u, _NN) + _dg(lo, u, _NN)


def _tri(cmp):
    j = lax.broadcasted_iota(jnp.int32, (TK, TK), 0)
    s = lax.broadcasted_iota(jnp.int32, (TK, TK), 1)
    return cmp(j, s).astype(BF16)


def _sb_fwd(tag, q, k, v, exchange=None):
    s = q.shape[0]
    npair = SB_DIM // LANES
    nq = s // TQ
    near = min(TK_NEAR, s)
    n_xi = len(exchange.inputs) if exchange else 0

    def body(*refs):
        q_ref, k_ref, v_ref = refs[:3]
        o_ref, t_ref = refs[3 + n_xi:5 + n_xi]
        tq = pl.program_id(1)
        if exchange:
            pair = pl.program_id(0)
            mine = exchange.pick(refs, 3, 2)
            exchange.before_work(mine, jnp.logical_and(pair == 0, tq == 0),
                                 jnp.logical_and(pair == npair - 1, tq == 0))
        qp = q_ref[...]
        lane = lax.broadcasted_iota(jnp.int32, (1, LANES), 1)
        u_gt = _tri(lambda j, s: j > s)
        heads = range(LANES // SB_HEAD_DIM)
        hms = [(lane // SB_HEAD_DIM) == hh for hh in heads]
        qhs = [jnp.where(hm, qp, jnp.zeros_like(qp)) * SB_SCALE for hm in hms]

        def block(wb, width, carry, masked):
            off = pl.multiple_of(wb * TK, TK)
            kb = k_ref[pl.ds(off, width), :]
            vb = v_ref[pl.ds(off, width), :]
            lbs, lss, mask = _sb_logits(qhs, kb, tq * TQ, wb * TK, masked)
            sums = _running_sums(lss, [c[0] for c in carry], u_gt, reverse=True)
            ws = [jnp.exp(lb + later) for lb, (later, _) in zip(lbs, sums)]
            if masked:
                ws = [jnp.where(mask, w, 0.0) for w in ws]
            pvs = [_dg(w, vb, _NN) for w in ws]
            return tuple((r, c[1] + pv) for (_, r), c, pv in zip(sums, carry, pvs))

        def more(c):
            alive = jnp.max(c[1][0][0])
            for hh in heads[1:]:
                alive = jnp.maximum(alive, jnp.max(c[1][hh][0]))
            return jnp.logical_and(c[0] >= 0, alive > SB_UNDERFLOW)

        start = tuple((jnp.zeros((TQ, 1), F32), jnp.zeros((TQ, LANES), F32)) for _ in heads)
        near_blk = jnp.maximum(tq - (near // TK - 1), 0)
        wb, done = lax.while_loop(more, lambda c: (c[0] - 1, block(c[0], TK, c[1], False)),
                                  (near_blk - 1, block(near_blk, near, start, True)))
        first = (wb + 1).astype(F32)
        out = jnp.zeros((TQ, LANES), F32)
        tot = jnp.zeros((TQ, LANES), F32)
        for hh in heads:
            r, acc = done[hh]
            out = out + jnp.where(hms[hh], acc, 0.0)
            tot = tot + jnp.where(hms[hh], jnp.where(lane % SB_HEAD_DIM == 1, first, r), 0.0)
        o_ref[...] = out
        t_ref[...] = tot
        if exchange:
            exchange.after_work(mine, jnp.logical_and(pair == npair - 1, tq == nq - 1), mid_done=True)

    tile = pl.BlockSpec((TQ, LANES), lambda p, t: (t, p))
    full = pl.BlockSpec((s, LANES), lambda p, t: (0, p))
    in_specs, out_specs = [tile, full, full], [tile, tile]
    out_shape = [jax.ShapeDtypeStruct((s, SB_DIM), F32)] * 2
    args, scratch = [q, k, v], []
    if exchange:
        x_in, x_out, scratch = exchange.specs()
        in_specs, out_specs = in_specs + x_in, out_specs + x_out
        out_shape, args = out_shape + exchange.outputs, args + exchange.inputs
    return pl.pallas_call(
        body, name="sb_fwd" + tag, grid=(npair, nq), in_specs=in_specs, out_specs=out_specs, out_shape=out_shape,
        scratch_shapes=scratch, compiler_params=_params(("arbitrary", "arbitrary")),
    )(*args)


def _sb_bwd(tag, q, k, v, tot, do, exchange=None):
    s = q.shape[0]
    npair = SB_DIM // LANES
    nq = s // TQ
    near = min(TK_NEAR, s)
    n_xi = len(exchange.inputs) if exchange else 0

    def body(*refs):
        q_ref, k_ref, v_ref, t_ref, do_ref = refs[:5]
        dq_ref, dk_ref, dv_ref = refs[5 + n_xi:8 + n_xi]
        tq = pl.program_id(1)
        if exchange:
            pair = pl.program_id(0)
            mine = exchange.pick(refs, 5, 3)
            exchange.before_work(mine, jnp.logical_and(pair == 0, tq == 0),
                                 jnp.logical_and(pair == npair - 1, tq == 0))

        @pl.when(tq == 0)
        def _():
            dk_ref[...] = jnp.zeros_like(dk_ref)
            dv_ref[...] = jnp.zeros_like(dv_ref)

        qp = q_ref[...]
        dop = do_ref[...]
        totp = t_ref[...]
        lane = lax.broadcasted_iota(jnp.int32, (1, LANES), 1)
        u_le = _tri(lambda j, s: j <= s)
        u_lt = _tri(lambda j, s: j < s)
        heads = range(LANES // SB_HEAD_DIM)
        hms = [(lane // SB_HEAD_DIM) == hh for hh in heads]
        qhs = [jnp.where(hm, qp, jnp.zeros_like(qp)) * SB_SCALE for hm in hms]
        dohs = [jnp.where(hm, dop, 0.0).astype(BF16) for hm in hms]
        totals = [jnp.sum(jnp.where(lane == hh * SB_HEAD_DIM, totp, 0.0), axis=1, keepdims=True) for hh in heads]
        first = jnp.max(jnp.where(lane == 1, totp, 0.0)).astype(jnp.int32)

        def block(wb, width, carry, masked):
            off = pl.multiple_of(wb * TK, TK)
            kb = k_ref[pl.ds(off, width), :]
            vb = v_ref[pl.ds(off, width), :]
            lbs, lss, mask = _sb_logits(qhs, kb, tq * TQ, wb * TK, masked)
            dws = [_dg(doh, vb, _NT) for doh in dohs]
            pres = _running_sums(lss, [c[0] for c in carry], u_le, reverse=False)
            ws = [jnp.exp(lb + (total - before)) for lb, total, (before, _) in zip(lbs, totals, pres)]
            if masked:
                ws = [jnp.where(mask, w, 0.0) for w in ws]
            gs = [w * dw for w, dw in zip(ws, dws)]
            lefts = _running_sums(gs, [c[1] for c in carry], u_lt, reverse=False, two_terms=False)
            dzs = [g - jnp.exp(lb) * (g + g_left) for g, lb, (g_left, _) in zip(gs, lbs, lefts)]
            if masked:
                dzs = [jnp.where(mask, dz, 0.0) for dz in dzs]
            dzbs = [dz.astype(BF16) for dz in dzs]
            dks = [_dg(dzb, qh, _TN) for dzb, qh in zip(dzbs, qhs)]
            dvs = [_dg(w, doh, _TN) for w, doh in zip(ws, dohs)]
            dqs = [_dg(dzb, kb, _NN) for dzb in dzbs]
            dk_ref[pl.ds(off, width), :] += functools.reduce(jnp.add, dks)
            dv_ref[pl.ds(off, width), :] += functools.reduce(jnp.add, dvs)
            return tuple((pre, gc, c[2] + dq) for (_, pre), (_, gc), c, dq in zip(pres, lefts, carry, dqs))

        zero = jnp.zeros((TQ, 1), F32)
        start = tuple((zero, zero, jnp.zeros((TQ, LANES), F32)) for _ in heads)
        near_blk = jnp.maximum(tq - (near // TK - 1), 0)
        far = lax.fori_loop(first, near_blk, lambda j, c: block(j, TK, c, False), start)
        done = block(near_blk, near, far, True)
        dq = jnp.zeros((TQ, LANES), F32)
        for hh in heads:
            dq = dq + jnp.where(hms[hh], done[hh][2], 0.0)
        dq_ref[...] = dq * SB_SCALE
        if exchange:
            exchange.after_work(mine, jnp.logical_and(pair == npair - 1, tq == nq - 1), mid_done=True)

    tile = pl.BlockSpec((TQ, LANES), lambda p, t: (t, p))
    full = pl.BlockSpec((s, LANES), lambda p, t: (0, p))
    in_specs, out_specs = [tile, full, full, tile, tile], [tile, full, full]
    out_shape = [jax.ShapeDtypeStruct((s, SB_DIM), F32)] * 3
    args, scratch = [q, k, v, tot, do], []
    if exchange:
        x_in, x_out, scratch = exchange.specs()
        in_specs, out_specs = in_specs + x_in, out_specs + x_out
        out_shape, args = out_shape + exchange.outputs, args + exchange.inputs
    return pl.pallas_call(
        body, name="sb_bwd" + tag, grid=(npair, nq), in_specs=in_specs, out_specs=out_specs, out_shape=out_shape,
        scratch_shapes=scratch, compiler_params=_params(("arbitrary", "arbitrary")),
    )(*args)


def _out_tile(y_ssd, o, sb_g, w_out):
    y_all = jnp.concatenate([y_ssd, _rms(o, sb_g)], axis=1)
    return mm_nn(y_all, w_out)


def _out_bwd(tag, y_ssd, o, dh, sb_g, w_out):
    def fn(y, o, dh, g, w):
        _, vjp = jax.vjp(_out_tile, y, o, g, w.astype(F32))
        return vjp(dh)

    return _rowcall("out_bwd" + tag, fn, [y_ssd, o, dh], [sb_g, w_out], [(SSD_DIM, F32), (SB_DIM, F32)],
                    [((1, SB_DIM), F32), ((D_MODEL, D_MODEL), F32)], tm=512)


def _mem_tile(mem, g, w_k, w_v):
    m = _rms(mem, g)
    return mm_nn(m, w_k), mm_nn(m, w_v)


def _mem_fwd(tag, mem, g, w_k, w_v):
    return _rowcall("mem_fwd" + tag, _mem_tile, [mem], [g, w_k, w_v], [(XA_DIM, F32), (XA_DIM, F32)], tm=256)


def _mem_bwd(tag, mem, dkx, dvx, g, w_k, w_v):
    def fn(mem, dkx, dvx, g, w_k, w_v):
        _, vjp = jax.vjp(lambda g, a, b: _mem_tile(mem, g, a, b), g, w_k.astype(F32), w_v.astype(F32))
        return vjp((dkx, dvx))

    return _rowcall("mem_bwd" + tag, fn, [mem, dkx, dvx], [g, w_k, w_v], [],
                    [((1, D_MODEL), F32), ((D_MODEL, XA_DIM), F32), ((D_MODEL, XA_DIM), F32)], tm=256)


def _xattn_tile(h, g, w_q, kx, vx, w_o):
    q = mm_nn(_rms(h, g), w_q)
    scale = 1.0 / math.sqrt(XA_HEAD_DIM)
    outs = []
    for i in range(XA_HEADS):
        sl = slice(i * XA_HEAD_DIM, (i + 1) * XA_HEAD_DIM)
        p = jax.nn.softmax(mm_nt(q[:, sl], kx[:, sl]) * scale, axis=-1)
        outs.append(mm_nn(p, vx[:, sl]))
    return mm_nn(jnp.concatenate(outs, axis=1), w_o)


def _mixer_fwd(tag, h, y_ssd, o, sb_g, w_out, g, w_q, kx, vx, w_o):
    def fn(h, y_ssd, o, sb_g, w_out, g, w_q, kx, vx, w_o):
        h1 = h + _out_tile(y_ssd, o, sb_g, w_out)
        return h1, h1 + _xattn_tile(h1, g, w_q, kx, vx, w_o)

    return _rowcall("mixer_fwd" + tag, fn, [h, y_ssd, o], [sb_g, w_out, g, w_q, kx, vx, w_o],
                    [(D_MODEL, F32), (D_MODEL, F32)], tm=512)


def _xattn_bwd(tag, h, dh_out, g, w_q, kx, vx, w_o, exchange=None):
    def fn(h, dh_out, g, w_q, kx, vx, w_o):
        _, vjp = jax.vjp(_xattn_tile, h, g, w_q.astype(F32), kx, vx, w_o.astype(F32))
        dh, dg, dwq, dkx, dvx, dwo = vjp(dh_out)
        return dh_out + dh, dg, dwq, dkx, dvx, dwo

    mlen = kx.shape[0]
    return _rowcall("xattn_bwd" + tag, fn, [h, dh_out], [g, w_q, kx, vx, w_o], [(D_MODEL, F32)],
                    [((1, D_MODEL), F32), ((D_MODEL, XA_DIM), F32), ((mlen, XA_DIM), F32), ((mlen, XA_DIM), F32),
                     ((XA_DIM, D_MODEL), F32)], tm=256, exchange=exchange)


def _mlp_fwd(tag, h, g, w1, w2, exchange=None):
    def fn(h, g, w1, w2):
        r = jnp.maximum(mm_nn(_rms(h, g), w1), 0.0)
        return h + mm_nn(r * r, w2), r

    return _rowcall("mlp_fwd" + tag, fn, [h], [g, w1, w2], [(D_MODEL, F32), (D_FF, BF16)], tm=256,
                    exchange=exchange)


def _mlp_bwd(tag, h, relu, dh_out, g, w1, w2, exchange=None):
    def fn(h, relu, dh_out, g, w1, w2):
        hn, vjp = jax.vjp(_rms, h, g)
        r = relu.astype(F32)
        dob = dh_out.astype(BF16)
        dp = mm_nt(dob, w2) * (2.0 * r)
        dh, dg = vjp(mm_nt(dp, w1))
        return dh_out + dh, hn, dp, r * r, dob, dg

    return _rowcall("mlp_bwd" + tag, fn, [h, relu, dh_out], [g, w1, w2],
                    [(D_MODEL, F32), (D_MODEL, BF16), (D_FF, BF16), (D_FF, BF16), (D_MODEL, BF16)],
                    [((1, D_MODEL), F32)], tm=256, exchange=exchange)


def _head(h, g, target):
    def lossfn(h, g, t):
        err = jnp.square(_rms(h, g) - t)
        return 0.5 * jnp.sum(jnp.mean(err, axis=-1))

    def fn(h, t, g):
        loss, vjp = jax.vjp(lambda h, g: lossfn(h, g, t), h, g)
        dh, dg = vjp(jnp.ones((), F32))
        return dh, jnp.full((1, LANES), loss, F32), dg

    return _rowcall("head", fn, [h, target], [g], [(D_MODEL, F32)], [((1, LANES), F32), ((1, D_MODEL), F32)], tm=512)


def _row(v):
    return v.reshape(1, -1)


class LocalPlan:
    def __init__(self, mats):
        self.mats = mats

    def weights(self, l):
        return self.mats[l]

    def carried_by(self, kernel, l):
        return None

    def carried_out(self, kernel, l, outs):
        pass

    def mlp_grads_done(self, l, gm):
        pass

    def mixer_grads_done(self, l, gm):
        pass

    def grads_done(self, l, gm):
        pass


def _local_step(x, mem, target, sw, plan):
    h = x
    saved = []
    for l in range(DEPTH):
        tag = str(l)
        m = plan.weights(l)
        z, xbc, q, k, v, dtr, *carried = _proj_fwd(tag, h, _row(sw["norm_mix_g"][l]), m["w_in"],
                                                   exchange=plan.carried_by("proj_fwd", l))
        plan.carried_out("proj_fwd", l, carried)
        xact = _conv_fwd(tag, xbc, m["conv_w"], _row(sw["conv_b"][l]))
        y_ssd, states, *carried = _ssd_fwd(tag, xact, dtr, z, _row(sw["dt_bias"][l]), _row(sw["a_log"][l]),
                                           _row(sw["d_skip"][l]), _row(sw["ssd_norm_g"][l]),
                                           exchange=plan.carried_by("ssd_fwd", l))
        plan.carried_out("ssd_fwd", l, carried)
        o, sb_tot, *carried = _sb_fwd(tag, q, k, v, exchange=plan.carried_by("sb_fwd", l))
        plan.carried_out("sb_fwd", l, carried)
        kx, vx = _mem_fwd(tag, mem, _row(sw["norm_mem_g"][l]), m["w_xk"], m["w_xv"])
        h1, h2 = _mixer_fwd(tag, h, y_ssd, o, _row(sw["sb_norm_g"][l]), m["w_out"], _row(sw["norm_xa_g"][l]),
                            m["w_xq"], kx, vx, m["w_xo"])
        h3, relu, *carried = _mlp_fwd(tag, h2, _row(sw["norm_ff_g"][l]), m["w_ff1"], m["w_ff2"],
                                      exchange=plan.carried_by("mlp_fwd", l))
        plan.carried_out("mlp_fwd", l, carried)
        saved.append((h, z, xbc, q, k, v, dtr, xact, y_ssd, states, o, sb_tot, h1, kx, vx, h2, relu))
        h = h3

    dh, loss, d_final = _head(h, _row(sw["final_g"]), target)
    gm = [dict() for _ in range(DEPTH)]
    gs = {name: [None] * DEPTH for name in ("norm_mix_g", "conv_w", "conv_b", "dt_bias", "a_log", "d_skip",
                                            "ssd_norm_g", "sb_norm_g", "norm_xa_g", "norm_mem_g", "norm_ff_g")}
    for l in reversed(range(DEPTH)):
        tag = str(l)
        m = plan.weights(l)
        h0, z, xbc, q, k, v, dtr, xact, y_ssd, states, o, sb_tot, h1, kx, vx, h2, relu = saved[l]
        dh2, hn_b, dp_b, a_b, do_b, gs["norm_ff_g"][l], *carried = _mlp_bwd(
            tag, h2, relu, dh, _row(sw["norm_ff_g"][l]), m["w_ff1"], m["w_ff2"],
            exchange=plan.carried_by("mlp_bwd", l))
        plan.carried_out("mlp_bwd", l, carried)
        gm[l]["w_ff1"] = _mm_tn_parts("dw_ff1" + tag, hn_b, dp_b, True)
        gm[l]["w_ff2"] = _mm_tn_parts("dw_ff2" + tag, a_b, do_b, False)
        plan.mlp_grads_done(l, gm[l])
        dh1, gs["norm_xa_g"][l], gm[l]["w_xq"], dkx, dvx, gm[l]["w_xo"], *carried = _xattn_bwd(
            tag, h1, dh2, _row(sw["norm_xa_g"][l]), m["w_xq"], kx, vx, m["w_xo"],
            exchange=plan.carried_by("xattn_bwd", l))
        plan.carried_out("xattn_bwd", l, carried)
        gs["norm_mem_g"][l], gm[l]["w_xk"], gm[l]["w_xv"] = _mem_bwd(
            tag, mem, dkx, dvx, _row(sw["norm_mem_g"][l]), m["w_xk"], m["w_xv"])
        dy_ssd, do, gs["sb_norm_g"][l], gm[l]["w_out"] = _out_bwd(
            tag, y_ssd, o, dh1, _row(sw["sb_norm_g"][l]), m["w_out"])
        plan.mixer_grads_done(l, gm[l])
        dq, dk, dv, *carried = _sb_bwd(tag, q, k, v, sb_tot, do, exchange=plan.carried_by("sb_bwd", l))
        plan.carried_out("sb_bwd", l, carried)
        dxact, ddtr, dz, gs["dt_bias"][l], gs["a_log"][l], gs["d_skip"][l], gs["ssd_norm_g"][l], *carried = _ssd_bwd(
            tag, xact, dtr, z, states, dy_ssd, _row(sw["dt_bias"][l]), _row(sw["a_log"][l]),
            _row(sw["d_skip"][l]), _row(sw["ssd_norm_g"][l]), exchange=plan.carried_by("ssd_bwd", l))
        plan.carried_out("ssd_bwd", l, carried)
        dxbc, gs["conv_w"][l], gs["conv_b"][l] = _conv_bwd(tag, xbc, dxact, m["conv_w"], _row(sw["conv_b"][l]))
        dh, hn_b, dp_b, gs["norm_mix_g"][l] = _proj_bwd(
            tag, h0, dh1, dz, dxbc, dq, dk, dv, ddtr, _row(sw["norm_mix_g"][l]), m["w_in"])
        gm[l]["w_in"] = _mm_tn_call("dw_in" + tag, dp_b, hn_b, IN_PAD, 512, 1024)
        plan.grads_done(l, gm[l])
    gs["final_g"] = d_final
    return loss, dh, gm, gs


VMEM_SPEC = pl.BlockSpec(memory_space=pltpu.VMEM)


def _place():
    return lax.axis_index("x"), lax.axis_index("y"), lax.axis_index("c")


def _other_chips(x, y):
    return [(1 - x, y), (x, 1 - y), (1 - x, 1 - y)]


def _remote(send_sems, recv_sems, idx, src, dst, to):
    return pltpu.make_async_remote_copy(src_ref=src, dst_ref=dst, send_sem=send_sems.at[idx],
                                        recv_sem=recv_sems.at[idx], device_id=to, device_id_type=MESH)


def _run_exchange(name, ex):
    def body(*refs):
        mine = ex.pick(refs, 0, 0)
        ex.start(*mine)
        if ex.mid is not None:
            ex.mid(*mine)
        ex.end(*mine)

    x_in, x_out, scratch = ex.specs()
    return pl.pallas_call(body, name=name, in_specs=x_in, out_specs=x_out, out_shape=ex.outputs,
                          scratch_shapes=scratch)(*ex.inputs)


def _gather_exchange(layer, shards):
    n = len(shards)
    outs = [jax.ShapeDtypeStruct((N_CHIPS,) + s.shape[1:], s.dtype) for s in shards]

    def start(w_refs, o_refs, ss, rs):
        x, y, c = _place()

        @pl.when(c == layer)
        def _():
            for i in range(n):
                for kk, (cx, cy) in enumerate(_other_chips(x, y)):
                    _remote(ss, rs, 6 * i + kk, w_refs[i].at[layer], o_refs[i].at[2 * x + y], (cx, cy, layer)).start()

    def mid(w_refs, o_refs, ss, rs):
        x, y, c = _place()

        @pl.when(c == layer)
        def _():
            for i in range(n):
                for kk, (cx, cy) in enumerate(_other_chips(x, y)):
                    got = o_refs[i].at[2 * cx + cy]
                    _remote(ss, rs, 6 * i + kk, got, got, (x, y, c)).wait_recv()
                    _remote(ss, rs, 6 * i + 3 + kk, got, got, (x, y, 1 - layer)).start()

    def end(w_refs, o_refs, ss, rs):
        x, y, c = _place()
        for i in range(n):
            for kk, (cx, cy) in enumerate(_other_chips(x, y)):
                got = o_refs[i].at[2 * cx + cy]

                @pl.when(c == layer)
                def _():
                    _remote(ss, rs, 6 * i + kk, w_refs[i].at[layer], got, (x, y, c)).wait_send()
                    _remote(ss, rs, 6 * i + 3 + kk, got, got, (x, y, c)).wait_send()

                @pl.when(c != layer)
                def _():
                    _remote(ss, rs, 6 * i + 3 + kk, got, got, (x, y, c)).wait_recv()

    return Exchange(shards, outs, 6 * n, start, end, mid)


def _handover_exchange(layer, grads):
    n = len(grads)
    outs = [jax.ShapeDtypeStruct(g.shape, g.dtype) for g in grads]

    def start(g_refs, o_refs, ss, rs):
        x, y, c = _place()

        @pl.when(c != layer)
        def _():
            for i in range(n):
                _remote(ss, rs, i, g_refs[i], o_refs[i], (x, y, layer)).start()

    def end(g_refs, o_refs, ss, rs):
        x, y, c = _place()
        for i in range(n):
            @pl.when(c != layer)
            def _():
                _remote(ss, rs, i, g_refs[i], o_refs[i], (x, y, c)).wait_send()

            @pl.when(c == layer)
            def _():
                _remote(ss, rs, i, g_refs[i], o_refs[i], (x, y, c)).wait_recv()

    return Exchange(grads, outs, n, start, end)


def _scatter_exchange(layer, parts):
    n = len(parts)
    outs = [jax.ShapeDtypeStruct(p.shape, p.dtype) for p in parts]

    def start(s_refs, o_refs, ss, rs):
        x, y, c = _place()

        @pl.when(c == layer)
        def _():
            for i in range(n):
                for kk, (cx, cy) in enumerate(_other_chips(x, y)):
                    _remote(ss, rs, 3 * i + kk, s_refs[i].at[2 * cx + cy], o_refs[i].at[2 * x + y],
                            (cx, cy, layer)).start()

    def end(s_refs, o_refs, ss, rs):
        x, y, c = _place()

        @pl.when(c == layer)
        def _():
            for i in range(n):
                for kk, (cx, cy) in enumerate(_other_chips(x, y)):
                    got = o_refs[i].at[2 * cx + cy]
                    _remote(ss, rs, 3 * i + kk, got, got, (x, y, c)).wait_recv()
            for i in range(n):
                for kk, (cx, cy) in enumerate(_other_chips(x, y)):
                    _remote(ss, rs, 3 * i + kk, s_refs[i].at[2 * cx + cy], o_refs[i].at[2 * x + y],
                            (x, y, c)).wait_send()

    return Exchange(parts, outs, 3 * n, start, end)


def _return_exchange(reduced):
    flat = [g for layer in range(DEPTH) for g in reduced[layer]]
    n = len(reduced[0])
    outs = [jax.ShapeDtypeStruct(g.shape, g.dtype) for g in flat]

    def start(g_refs, o_refs, ss, rs):
        x, y, c = _place()
        for layer in range(DEPTH):
            @pl.when(c == layer)
            def _():
                for i in range(n):
                    k = layer * n + i
                    _remote(ss, rs, k, g_refs[k], o_refs[k], (x, y, 1 - layer)).start()

    def end(g_refs, o_refs, ss, rs):
        x, y, c = _place()
        for layer in range(DEPTH):
            for i in range(n):
                k = layer * n + i

                @pl.when(c == layer)
                def _():
                    _remote(ss, rs, k, g_refs[k], o_refs[k], (x, y, c)).wait_send()

                @pl.when(c != layer)
                def _():
                    _remote(ss, rs, k, g_refs[k], o_refs[k], (x, y, c)).wait_recv()

    return Exchange(flat, outs, DEPTH * n, start, end)


def _gather_small(tag, buf):
    shape = buf.shape

    def body(b_ref, o_ref, sum_ref, send_sems, recv_sems, local_sem):
        x, y, c = _place()
        me = 4 * x + 2 * y + c
        mine = pltpu.make_async_copy(b_ref, o_ref.at[me], local_sem)
        mine.start()
        flips = [(dx, dy, dc) for dx in (0, 1) for dy in (0, 1) for dc in (0, 1) if (dx, dy, dc) != (0, 0, 0)]
        sends = []

        def peer(dx, dy, dc):
            return (1 - x if dx else x, 1 - y if dy else y, 1 - c if dc else c)

        for kk, flip in enumerate(flips):
            cp = pltpu.make_async_remote_copy(src_ref=b_ref, dst_ref=o_ref.at[me], send_sem=send_sems.at[kk],
                                              recv_sem=recv_sems.at[kk], device_id=peer(*flip), device_id_type=MESH)
            cp.start()
            sends.append(cp)
        for kk, flip in enumerate(flips):
            px, py, pc = peer(*flip)
            frm = 4 * px + 2 * py + pc
            pltpu.make_async_remote_copy(src_ref=b_ref, dst_ref=o_ref.at[frm], send_sem=send_sems.at[kk],
                                         recv_sem=recv_sems.at[kk], device_id=(x, y, c),
                                         device_id_type=MESH).wait_recv()
        for cp in sends:
            cp.wait_send()
        mine.wait()
        total = o_ref[0]
        for d in range(1, N_DEV):
            total = total + o_ref[d]
        sum_ref[...] = total

    return pl.pallas_call(
        body, name="gather_small" + tag, in_specs=[VMEM_SPEC], out_specs=[VMEM_SPEC, VMEM_SPEC],
        out_shape=[jax.ShapeDtypeStruct((N_DEV,) + shape, buf.dtype), jax.ShapeDtypeStruct(shape, buf.dtype)],
        scratch_shapes=[pltpu.SemaphoreType.DMA((N_DEV - 1,)), pltpu.SemaphoreType.DMA((N_DEV - 1,)),
                        pltpu.SemaphoreType.DMA],
    )(buf)


def _add_handed(tag, layer, g, r, tr=256):
    _, rows, cols = g.shape
    tr, tc = _tile_of(rows, cols, tr)
    per_row = cols // tc

    def body(g_ref, r_ref, o_ref):
        @pl.when(lax.axis_index("c") == layer)
        def _():
            o_ref[...] = (g_ref[...].astype(F32) + r_ref[...].astype(F32)).astype(o_ref.dtype)

    spec = pl.BlockSpec((1, tr, tc), lambda p, i: (p, i // per_row, i % per_row))
    return pl.pallas_call(
        body, name="add_handed_" + tag, grid=(N_CHIPS, (rows // tr) * per_row), in_specs=[spec, spec], out_specs=spec,
        out_shape=jax.ShapeDtypeStruct(g.shape, BF16), compiler_params=_params(("arbitrary", "arbitrary")),
    )(g, r)


def _sum_parts(tag, layer, own, parts, tr=256):
    _, rows, cols = parts.shape
    tr, tc = _tile_of(rows, cols, tr)
    per_row = cols // tc
    chip = (2 * lax.axis_index("x") + lax.axis_index("y")).astype(jnp.int32).reshape(1)

    def body(c_ref, own_ref, p1_ref, p2_ref, p3_ref, o_ref):
        @pl.when(lax.axis_index("c") == layer)
        def _():
            total = own_ref[0].astype(F32)
            for p_ref in (p1_ref, p2_ref, p3_ref):
                total = total + p_ref[0].astype(F32)
            o_ref[...] = total

    def after(kk):
        return pl.BlockSpec((1, tr, tc), lambda i, c_ref: ((c_ref[0] + kk) % N_CHIPS, i // per_row, i % per_row))

    return pl.pallas_call(
        body, name="sum_parts_" + tag,
        grid_spec=pltpu.PrefetchScalarGridSpec(
            num_scalar_prefetch=1, grid=((rows // tr) * per_row,), in_specs=[after(0), after(1), after(2), after(3)],
            out_specs=pl.BlockSpec((tr, tc), lambda i, c_ref: (i // per_row, i % per_row))),
        out_shape=jax.ShapeDtypeStruct((rows, cols), F32),
        compiler_params=_params(("arbitrary",)),
    )(chip, own, parts, parts, parts)


def _adamw_math(w, g, m, v):
    m = ADAM_B1 * m + (1.0 - ADAM_B1) * g
    v = ADAM_B2 * v + (1.0 - ADAM_B2) * jnp.square(g)
    m_hat = m / (1.0 - ADAM_B1 ** ADAM_STEP)
    v_hat = v / (1.0 - ADAM_B2 ** ADAM_STEP)
    delta = -ADAM_LR * (m_hat / (jnp.sqrt(v_hat) + ADAM_EPS) + ADAM_WD * w)
    return delta, m, v


def _tile_of(rows, cols, tr):
    return (tr, cols) if rows % tr == 0 else (rows, 256)


def _adamw(tag, w, computed, received, m, v, tr=256):
    _, rows, cols = w.shape
    tr, tc = _tile_of(rows, cols, tr)
    per_row = cols // tc

    def body(*refs):
        w_ref, m_ref, v_ref = refs[:3]
        g_refs = refs[3:3 + 2 * DEPTH]
        g_ref, d_ref, nm_ref, nv_ref = refs[3 + 2 * DEPTH:]
        layer = pl.program_id(0)
        core = lax.axis_index("c")
        g = jnp.zeros((tr, tc), F32)
        for l in range(DEPTH):
            mine = jnp.where(core == l, g_refs[2 * l][...], g_refs[2 * l + 1][...])
            g = jnp.where(layer == l, mine, g)
        g_ref[0] = g
        d_ref[0], nm_ref[0], nv_ref[0] = _adamw_math(w_ref[0], g, m_ref[0], v_ref[0])

    stacked = pl.BlockSpec((1, tr, tc), lambda l, i: (l, i // per_row, i % per_row))

    def of_layer(k):
        return pl.BlockSpec((tr, tc), lambda l, i: (jnp.where(l == k, i // per_row, 0),
                                                    jnp.where(l == k, i % per_row, 0)))

    g_specs = [of_layer(l) for l in range(DEPTH) for _ in range(2)]
    g_args = [a for l in range(DEPTH) for a in (computed[l], received[l])]
    return pl.pallas_call(
        body, name="adamw_" + tag, grid=(DEPTH, (rows // tr) * per_row), in_specs=[stacked] * 3 + g_specs,
        out_specs=[stacked] * 4, out_shape=[jax.ShapeDtypeStruct(w.shape, F32)] * 4,
        compiler_params=_params(("arbitrary", "arbitrary")),
    )(w, m, v, *g_args)


def _w_in_to_padded(wt):
    d0 = SSD_DIM + CONV_DIM
    rows = jnp.concatenate([wt[:d0], wt[d0 + SSD_HEADS:], wt[d0:d0 + SSD_HEADS],
                            jnp.zeros((DT_PAD - SSD_HEADS, wt.shape[1]), wt.dtype)], axis=0)
    return rows.T


def _w_in_from_padded(gt):
    d0 = SSD_DIM + CONV_DIM
    return jnp.concatenate([gt[:d0], gt[DT_OFF:DT_OFF + SSD_HEADS], gt[d0:DT_OFF]], axis=0)


def _small_layout():
    return (("norm_mix_g", 0, 0, D_MODEL), ("norm_xa_g", 1, 0, D_MODEL), ("norm_mem_g", 2, 0, D_MODEL),
            ("norm_ff_g", 3, 0, D_MODEL), ("conv_b", 4, 0, CONV_DIM), ("ssd_norm_g", 5, 0, SSD_DIM),
            ("sb_norm_g", 5, SSD_DIM, SB_DIM), ("dt_bias", 6, 0, SSD_HEADS), ("a_log", 6, LANES, SSD_HEADS),
            ("d_skip", 6, 2 * LANES, SSD_HEADS))


def _pack_small(gs, loss):
    lay = _small_layout()
    args = [gs[name][l] for l in range(DEPTH) for name, _, _, _ in lay]
    args += [gs["conv_w"][l] for l in range(DEPTH)] + [gs["final_g"], loss]
    n_lay = len(lay)

    def body(*refs):
        o_ref = refs[-1]
        o_ref[...] = jnp.zeros_like(o_ref)
        for l in range(DEPTH):
            for i, (_, rr, c0, width) in enumerate(lay):
                row = l * _SM_PER_LAYER + rr
                o_ref[row:row + 1, c0:c0 + width] = refs[l * n_lay + i][...]
            row = _SM_CONVW + l * CONV_K
            o_ref[row:row + CONV_K, 0:CONV_DIM] = refs[DEPTH * n_lay + l][...]
        o_ref[_SM_FINAL:_SM_FINAL + 1, :] = refs[DEPTH * n_lay + DEPTH][...]
        o_ref[_SM_LOSS:_SM_LOSS + 1, 0:LANES] = refs[DEPTH * n_lay + DEPTH + 1][...]

    return pl.pallas_call(
        body, name="pack_small", in_specs=[VMEM_SPEC] * len(args), out_specs=VMEM_SPEC,
        out_shape=jax.ShapeDtypeStruct((SMALL_ROWS, PACK_COLS), F32),
    )(*args)


def _small_update(buf, w, mom, var):
    lay = _small_layout()
    names = [name for name, _, _, _ in lay] + ["final_g", "conv_w"]
    conv_cols = CONV_DIM // N_CHIPS
    shapes2d = {name: (DEPTH, width) for name, _, _, width in lay}
    shapes2d["final_g"] = (1, D_MODEL)
    shapes2d["conv_w"] = (DEPTH * CONV_K, conv_cols)
    args = [buf]
    for src in (w, mom, var):
        args += [src[name].reshape(shapes2d[name]) for name in names]
    n = len(names)

    def body(*refs):
        b_ref = refs[0]
        w_refs, m_refs, v_refs = refs[1:1 + n], refs[1 + n:1 + 2 * n], refs[1 + 2 * n:1 + 3 * n]
        outs = refs[1 + 3 * n:]
        chip = 2 * lax.axis_index("x") + lax.axis_index("y")
        for i, name in enumerate(names):
            if name == "final_g":
                g = b_ref[_SM_FINAL:_SM_FINAL + 1, :]
            elif name == "conv_w":
                rows = b_ref[_SM_CONVW:_SM_CONVW + DEPTH * CONV_K, 0:CONV_DIM]
                g = jnp.zeros((DEPTH * CONV_K, conv_cols), F32)
                for j in range(N_CHIPS):
                    g = g + jnp.where(chip == j, rows[:, j * conv_cols:(j + 1) * conv_cols], 0.0)
            else:
                _, rr, c0, width = lay[i]
                g = jnp.concatenate([b_ref[l * _SM_PER_LAYER + rr:l * _SM_PER_LAYER + rr + 1, c0:c0 + width]
                                     for l in range(DEPTH)], axis=0)
            d, m2, v2 = _adamw_math(w_refs[i][...], g, m_refs[i][...], v_refs[i][...])
            outs[i][...] = g
            outs[n + i][...] = d
            outs[2 * n + i][...] = m2
            outs[3 * n + i][...] = v2

    out_shape = [jax.ShapeDtypeStruct(shapes2d[name], F32) for _ in range(4) for name in names]
    res = pl.pallas_call(
        body, name="small_update", in_specs=[VMEM_SPEC] * len(args), out_specs=[VMEM_SPEC] * (4 * n),
        out_shape=out_shape,
    )(*args)
    return tuple({name: res[k * n + i].reshape(w[name].shape) for i, name in enumerate(names)} for k in range(4))


SMALL_NAMES = ("norm_mix_g", "conv_b", "dt_bias", "a_log", "d_skip", "ssd_norm_g", "sb_norm_g", "norm_xa_g",
               "norm_mem_g", "norm_ff_g", "final_g")
WEIGHT_ORDER = ("norm_mix_g", "w_in", "conv_w", "conv_b", "dt_bias", "a_log", "d_skip", "ssd_norm_g", "sb_norm_g",
                "w_out", "norm_xa_g", "norm_mem_g", "w_xq", "w_xk", "w_xv", "w_xo", "norm_ff_g", "w_ff1", "w_ff2",
                "final_g")


_ALL = tuple(range(len(MATS)))
_IN = tuple(i for i in _ALL if MATS[i][0] == "w_in")
_MLP = tuple(i for i in _ALL if MATS[i][0] in ("w_ff1", "w_ff2"))
_MIXER = tuple(i for i in _ALL if i not in _IN + _MLP)
_CONV = len(MATS)


class PipelinedPlan(LocalPlan):
    def __init__(self, shards, conv_w):
        self.shards = list(shards) + [conv_w]
        self.chip = 2 * lax.axis_index("x") + lax.axis_index("y")
        self.mats = [dict() for _ in range(DEPTH)]
        n = len(MATS)
        self.parts = [[None] * n for _ in range(DEPTH)]
        self.to_chips = [[None] * n for _ in range(DEPTH)]
        self.reduced = [[None] * n for _ in range(DEPTH)]
        self.riders = {}
        first = _IN + (_CONV,)
        self._gathered(0, first, _run_exchange("gather_first", self._gather(0, first)))
        self._gather_behind(0, _MIXER, "proj_fwd", 0)
        self._gather_behind(0, _MLP[:1], "ssd_fwd", 0)
        self._gather_behind(0, _MLP[1:], "sb_fwd", 0)
        for l in range(1, DEPTH):
            self._gather_behind(l, first, "sb_fwd", l - 1)
            self._gather_behind(l, _MIXER, "ssd_fwd", l)
            self._gather_behind(l, _MLP, "sb_fwd", l)

    def _ride(self, kernel, l, exchange, then):
        self.riders.setdefault((kernel, l), []).append((exchange, then))

    def carried_by(self, kernel, l):
        exchange = None
        for ex, _ in self.riders.get((kernel, l), []):
            exchange = _both(exchange, ex)
        return exchange

    def carried_out(self, kernel, l, outs):
        for ex, then in self.riders.pop((kernel, l), []):
            then(outs[:len(ex.outputs)])
            outs = outs[len(ex.outputs):]

    def _gather(self, l, which):
        return _gather_exchange(l, [self.shards[i] for i in which])

    def _gather_behind(self, l, which, kernel, host):
        self._ride(kernel, host, self._gather(l, which), lambda outs: self._gathered(l, which, outs))

    def _gathered(self, l, which, outs):
        for i, theirs in zip(which, outs):
            name, _, axis = MATS[i] if i != _CONV else ("conv_w", None, 1)
            full = jnp.concatenate([jnp.where(self.chip == j, self.shards[i][l], theirs[j]) for j in range(N_CHIPS)],
                                   axis=axis)
            self.mats[l][name] = _w_in_to_padded(full) if name == "w_in" else full

    def _set_parts(self, l, which, gm):
        for i in which:
            name, _, axis = MATS[i]
            g = _w_in_from_padded(gm[name]) if name == "w_in" else gm[name]
            if g.ndim == 2 and axis == 0:
                g = g.reshape((N_CHIPS, g.shape[0] // N_CHIPS, g.shape[1]))
            elif g.ndim == 2:
                g = jnp.swapaxes(g.reshape((g.shape[0], N_CHIPS, g.shape[1] // N_CHIPS)), 0, 1)
            self.parts[l][i] = g

    def _handover(self, l, which):
        return _handover_exchange(l, [self.parts[l][i] for i in which])

    def _handed(self, l, which, outs):
        for i, r in zip(which, outs):
            self.to_chips[l][i] = _add_handed(MATS[i][0] + str(l), l, self.parts[l][i], r)

    def _scatter(self, l, which):
        return _scatter_exchange(l, [self.to_chips[l][i] for i in which])

    def _scattered(self, l, which, outs):
        for i, got in zip(which, outs):
            self.reduced[l][i] = _sum_parts(MATS[i][0] + str(l), l, self.to_chips[l][i], got)

    def _send_behind(self, l, which, hand_kernel, cross_kernel, host):
        def handed(outs):
            self._handed(l, which, outs)
            self._ride(cross_kernel, host, self._scatter(l, which), lambda o: self._scattered(l, which, o))

        self._ride(hand_kernel, host, self._handover(l, which), handed)

    def mlp_grads_done(self, l, gm):
        self._set_parts(l, _MLP, gm)
        self._send_behind(l, _MLP, "xattn_bwd", "sb_bwd", l)

    def mixer_grads_done(self, l, gm):
        self._set_parts(l, _MIXER, gm)
        self._send_behind(l, _MIXER, "sb_bwd", "ssd_bwd", l)

    def grads_done(self, l, gm):
        self._set_parts(l, _IN, gm)
        if l > 0:
            self._send_behind(l, _IN, "mlp_bwd", "sb_bwd", l - 1)
        else:
            self._handed(0, _IN, _run_exchange("handover_last", self._handover(0, _IN)))
            self._scattered(0, _IN, _run_exchange("scatter_last", self._scatter(0, _IN)))

    def reduced_gradients(self):
        returned = _run_exchange("return_reduced", _return_exchange(self.reduced))
        n = len(MATS)
        return [([self.reduced[l][i] for l in range(DEPTH)], [returned[l * n + i] for l in range(DEPTH)])
                for i in range(n)]


def kernel(x, mem, norm_mix_g, w_in, conv_w, conv_b, dt_bias, a_log, d_skip, ssd_norm_g, sb_norm_g, w_out, norm_xa_g, norm_mem_g, w_xq, w_xk, w_xv, w_xo, norm_ff_g, w_ff1, w_ff2, final_g, loss_target, m_norm_mix_g, m_w_in, m_conv_w, m_conv_b, m_dt_bias, m_a_log, m_d_skip, m_ssd_norm_g, m_sb_norm_g, m_w_out, m_norm_xa_g, m_norm_mem_g, m_w_xq, m_w_xk, m_w_xv, m_w_xo, m_norm_ff_g, m_w_ff1, m_w_ff2, m_final_g, v_norm_mix_g, v_w_in, v_conv_w, v_conv_b, v_dt_bias, v_a_log, v_d_skip, v_ssd_norm_g, v_sb_norm_g, v_w_out, v_norm_xa_g, v_norm_mem_g, v_w_xq, v_w_xk, v_w_xv, v_w_xo, v_norm_ff_g, v_w_ff1, v_w_ff2, v_final_g):
    w = dict(norm_mix_g=norm_mix_g, w_in=w_in, conv_w=conv_w, conv_b=conv_b, dt_bias=dt_bias, a_log=a_log,
             d_skip=d_skip, ssd_norm_g=ssd_norm_g, sb_norm_g=sb_norm_g, w_out=w_out, norm_xa_g=norm_xa_g,
             norm_mem_g=norm_mem_g, w_xq=w_xq, w_xk=w_xk, w_xv=w_xv, w_xo=w_xo, norm_ff_g=norm_ff_g, w_ff1=w_ff1,
             w_ff2=w_ff2, final_g=final_g)
    mom = dict(norm_mix_g=m_norm_mix_g, w_in=m_w_in, conv_w=m_conv_w, conv_b=m_conv_b, dt_bias=m_dt_bias,
               a_log=m_a_log, d_skip=m_d_skip, ssd_norm_g=m_ssd_norm_g, sb_norm_g=m_sb_norm_g, w_out=m_w_out,
               norm_xa_g=m_norm_xa_g, norm_mem_g=m_norm_mem_g, w_xq=m_w_xq, w_xk=m_w_xk, w_xv=m_w_xv, w_xo=m_w_xo,
               norm_ff_g=m_norm_ff_g, w_ff1=m_w_ff1, w_ff2=m_w_ff2, final_g=m_final_g)
    var = dict(norm_mix_g=v_norm_mix_g, w_in=v_w_in, conv_w=v_conv_w, conv_b=v_conv_b, dt_bias=v_dt_bias,
               a_log=v_a_log, d_skip=v_d_skip, ssd_norm_g=v_ssd_norm_g, sb_norm_g=v_sb_norm_g, w_out=v_w_out,
               norm_xa_g=v_norm_xa_g, norm_mem_g=v_norm_mem_g, w_xq=v_w_xq, w_xk=v_w_xk, w_xv=v_w_xv, w_xo=v_w_xo,
               norm_ff_g=v_norm_ff_g, w_ff1=v_w_ff1, w_ff2=v_w_ff2, final_g=v_final_g)
    for params in (w, mom, var):
        params["w_in"] = jnp.swapaxes(params["w_in"], 1, 2)

    sw = {name: w[name] for name in SMALL_NAMES}
    plan = PipelinedPlan([w[name].astype(BF16) for name, _, _ in MATS], conv_w)
    loss, grad_x, gm, gs = _local_step(x[0], mem[0], loss_target[0], sw, plan)
    g_mats = plan.reduced_gradients()

    _, small_sum = _gather_small("_grads", _pack_small(gs, loss))
    loss_out = small_sum[_SM_LOSS, 0]

    grads, deltas, new_m, new_v = {}, {}, {}, {}
    for (name, _, _), (computed, received) in zip(MATS, g_mats):
        grads[name], deltas[name], new_m[name], new_v[name] = _adamw(
            name, w[name], computed, received, mom[name], var[name])
    g_s, d_s, m_s, v_s = _small_update(small_sum, w, mom, var)
    for name in g_s:
        grads[name], deltas[name], new_m[name], new_v[name] = g_s[name], d_s[name], m_s[name], v_s[name]
    for out in (grads, deltas, new_m, new_v):
        out["w_in"] = jnp.swapaxes(out["w_in"], 1, 2)

    return (loss_out, grad_x[None], *[grads[n] for n in WEIGHT_ORDER], *[deltas[n] for n in WEIGHT_ORDER],
            *[new_m[n] for n in WEIGHT_ORDER], *[new_v[n] for n in WEIGHT_ORDER])
```
